```python
import jax, jax.numpy as jnp
from jax import lax
import numpy as np

D_MODEL = 1024
BATCH = 8
SEQ = 8192
DEPTH = 1

CHUNK = 64
N_PREV_CHUNKS = 8
BAND = (N_PREV_CHUNKS + 1) * CHUNK
ATT_HEADS = 8
ATT_HEAD_DIM = 64
ATT_WIDTH = ATT_HEADS * ATT_HEAD_DIM
REL_CLIP = 128
SG_BLOCK = 128
SG_GROUPS = 8
SG_GROUP_DIM = 64
SG_WIDTH = SG_GROUPS * SG_GROUP_DIM
N_BRANCHES = 2
IN_COLS = 3 * ATT_WIDTH + 2 * SG_WIDTH + N_BRANCHES * D_MODEL
MEM_LEN = 256
XATT_HEADS = 4
XATT_HEAD_DIM = D_MODEL // XATT_HEADS
D_FF = -(-8 * D_MODEL // (3 * 256)) * 256
EPS = 1e-6
NEG_INF = -1e30

kernel_name = "hybrid_chunk_attn_sgu_block"


def rmsnorm(x, g):
    xf = x.astype(jnp.float32)
    y = xf * lax.rsqrt(jnp.mean(xf * xf, axis=-1, keepdims=True) + EPS)
    return (y * g.astype(jnp.float32)).astype(x.dtype)


def layernorm(x, g, b):
    xf = x.astype(jnp.float32)
    mu = jnp.mean(xf, axis=-1, keepdims=True)
    var = jnp.mean(jnp.square(xf - mu), axis=-1, keepdims=True)
    y = (xf - mu) * lax.rsqrt(var + EPS)
    return (y * g.astype(jnp.float32) + b.astype(jnp.float32)).astype(x.dtype)


def chunked_relpos_attention(q, k, v, rel_bias):
    B, S, H, Dh = q.shape
    nC = S // CHUNK
    q = (q * (Dh ** -0.5)).reshape(B, nC, CHUNK, H, Dh)
    k = k.reshape(B, nC, CHUNK, H, Dh)
    v = v.reshape(B, nC, CHUNK, H, Dh)
    pad = ((0, 0), (N_PREV_CHUNKS, 0), (0, 0), (0, 0), (0, 0))
    kp = jnp.pad(k, pad)
    vp = jnp.pad(v, pad)
    kb = jnp.stack([kp[:, j:j + nC] for j in range(N_PREV_CHUNKS + 1)], axis=2).reshape(B, nC, BAND, H, Dh)
    vb = jnp.stack([vp[:, j:j + nC] for j in range(N_PREV_CHUNKS + 1)], axis=2).reshape(B, nC, BAND, H, Dh)
    s = jnp.einsum('bcihd,bcmhd->bhcim', q, kb, preferred_element_type=jnp.float32)
    qi = np.arange(CHUNK)[:, None]
    mi = np.arange(BAND)[None, :]
    dist = qi - mi + N_PREV_CHUNKS * CHUNK
    idx = np.clip(dist, -REL_CLIP, REL_CLIP) + REL_CLIP
    bias = rel_bias[:, idx].astype(jnp.float32)
    s = s + bias[None, :, None]
    valid = (np.arange(nC)[:, None] - N_PREV_CHUNKS + np.arange(BAND)[None, :] // CHUNK) >= 0
    s = jnp.where(valid[None, None, :, None, :], s, NEG_INF)
    p = jax.nn.softmax(s, axis=-1)
    o = jnp.einsum('bhcim,bcmhd->bcihd', p.astype(vb.dtype), vb)
    return o.reshape(B, S, H * Dh)


def spatial_gating(u, v, ln_g, ln_b, w_s, b_s):
    B, S, _ = u.shape
    nB = S // SG_BLOCK
    v = v.reshape(B, nB, SG_BLOCK, SG_GROUPS, SG_GROUP_DIM)
    v = layernorm(v, ln_g, ln_b)
    t = np.arange(SG_BLOCK)
    mask = (t[None, :] // CHUNK) <= (t[:, None] // CHUNK)
    w = jnp.where(mask[None], w_s, 0.0)
    sv = jnp.einsum('gts,bnsgd->bntgd', w, v) + b_s.T[None, None, :, :, None]
    return (u.reshape(B, nB, SG_BLOCK, SG_GROUPS, SG_GROUP_DIM) * sv).reshape(B, S, SG_WIDTH)


def cross_attention(h, m, w_xq, w_xkv, w_xo):
    B, S, _ = h.shape
    M = m.shape[1]
    q = (h @ w_xq).reshape(B, S, XATT_HEADS, XATT_HEAD_DIM) * (XATT_HEAD_DIM ** -0.5)
    k, v = jnp.split(m @ w_xkv, 2, axis=-1)
    k = k.reshape(B, M, XATT_HEADS, XATT_HEAD_DIM)
    v = v.reshape(B, M, XATT_HEADS, XATT_HEAD_DIM)
    s = jnp.einsum('bshd,bmhd->bhsm', q, k, preferred_element_type=jnp.float32)
    p = jax.nn.softmax(s, axis=-1)
    o = jnp.einsum('bhsm,bmhd->bshd', p.astype(v.dtype), v).reshape(B, S, D_MODEL)
    return o @ w_xo


def _fwd_setup_inputs(seed: int = 0) -> dict:
    key = jax.random.key(seed)
    ks = jax.random.split(key, 24)
    f32 = jnp.float32
    nrm = lambda k, shape, scale: jax.random.normal(k, shape, f32) * scale
    L = DEPTH
    return {
        "x": nrm(ks[0], (BATCH, SEQ, D_MODEL), 1.0),
        "mem": nrm(ks[1], (BATCH, MEM_LEN, D_MODEL), 1.0),
        "norm_mix_g": 1.0 + nrm(ks[2], (L, D_MODEL), 0.02),
        "w_in": nrm(ks[3], (L, D_MODEL, IN_COLS), D_MODEL ** -0.5),
        "rel_bias": nrm(ks[4], (L, ATT_HEADS, 2 * REL_CLIP + 1), 0.5),
        "sg_ln_g": 1.0 + nrm(ks[5], (L, SG_GROUPS, SG_GROUP_DIM), 0.02),
        "sg_ln_b": nrm(ks[6], (L, SG_GROUPS, SG_GROUP_DIM), 0.02),
        "sg_w": nrm(ks[7], (L, SG_GROUPS, SG_BLOCK, SG_BLOCK), SG_BLOCK ** -0.5),
        "sg_b": 1.0 + nrm(ks[8], (L, SG_GROUPS, SG_BLOCK), 0.02),
        "w_branch_att": nrm(ks[9], (L, ATT_WIDTH, D_MODEL), ATT_WIDTH ** -0.5),
        "w_branch_sg": nrm(ks[10], (L, SG_WIDTH, D_MODEL), SG_WIDTH ** -0.5),
        "w_out": nrm(ks[11], (L, D_MODEL, D_MODEL), D_MODEL ** -0.5),
        "norm_xattn_g": 1.0 + nrm(ks[12], (L, D_MODEL), 0.02),
        "norm_mem_g": 1.0 + nrm(ks[13], (L, D_MODEL), 0.02),
        "w_xq": nrm(ks[14], (L, D_MODEL, D_MODEL), D_MODEL ** -0.5),
        "w_xkv": nrm(ks[15], (L, D_MODEL, 2 * D_MODEL), D_MODEL ** -0.5),
        "w_xo": nrm(ks[16], (L, D_MODEL, D_MODEL), D_MODEL ** -0.5),
        "norm_ffn_g": 1.0 + nrm(ks[17], (L, D_MODEL), 0.02),
        "w_ffn_in": nrm(ks[18], (L, D_MODEL, 2 * D_FF), D_MODEL ** -0.5),
        "w_ffn_out": nrm(ks[19], (L, D_FF, D_MODEL), D_FF ** -0.5),
        "norm_final_g": 1.0 + nrm(ks[20], (D_MODEL,), 0.02),
    }


def _fwd_reference(x, mem, norm_mix_g, w_in, rel_bias, sg_ln_g, sg_ln_b, sg_w, sg_b,
              w_branch_att, w_branch_sg, w_out, norm_xattn_g, norm_mem_g,
              w_xq, w_xkv, w_xo, norm_ffn_g, w_ffn_in, w_ffn_out, norm_final_g):
    B, S, _ = x.shape
    col = np.cumsum([ATT_WIDTH, ATT_WIDTH, ATT_WIDTH, SG_WIDTH, SG_WIDTH, D_MODEL])
    for l in range(DEPTH):
        h = rmsnorm(x, norm_mix_g[l])
        z = h @ w_in[l]
        q, k, v, u_sg, v_sg, g_a, g_b = jnp.split(z, col, axis=-1)
        q = q.reshape(B, S, ATT_HEADS, ATT_HEAD_DIM)
        k = k.reshape(B, S, ATT_HEADS, ATT_HEAD_DIM)
        v = v.reshape(B, S, ATT_HEADS, ATT_HEAD_DIM)
        y_att = chunked_relpos_attention(q, k, v, rel_bias[l])
        y_sg = spatial_gating(jax.nn.gelu(u_sg), jax.nn.gelu(v_sg),
                              sg_ln_g[l], sg_ln_b[l], sg_w[l], sg_b[l])
        merged = (jax.nn.sigmoid(g_a) * (y_att @ w_branch_att[l])
                  + jax.nn.sigmoid(g_b) * (y_sg @ w_branch_sg[l]))
        x = x + merged @ w_out[l]
        x = x + cross_attention(rmsnorm(x, norm_xattn_g[l]), rmsnorm(mem, norm_mem_g[l]),
                                w_xq[l], w_xkv[l], w_xo[l])
        gate, up = jnp.split(rmsnorm(x, norm_ffn_g[l]) @ w_ffn_in[l], 2, axis=-1)
        x = x + (jax.nn.silu(gate) * up) @ w_ffn_out[l]
    return rmsnorm(x, norm_final_g)


import jax as _jax
import jax.numpy as _jnp

TWIN_FORMAT = 'train_step'
FWD_PARAMS = ['x', 'mem', 'norm_mix_g', 'w_in', 'rel_bias', 'sg_ln_g', 'sg_ln_b', 'sg_w', 'sg_b', 'w_branch_att', 'w_branch_sg', 'w_out', 'norm_xattn_g', 'norm_mem_g', 'w_xq', 'w_xkv', 'w_xo', 'norm_ffn_g', 'w_ffn_in', 'w_ffn_out', 'norm_final_g']
TWIN_WEIGHTS = ['norm_mix_g', 'w_in', 'rel_bias', 'sg_ln_g', 'sg_ln_b', 'sg_w', 'sg_b', 'w_branch_att', 'w_branch_sg', 'w_out', 'norm_xattn_g', 'norm_mem_g', 'w_xq', 'w_xkv', 'w_xo', 'norm_ffn_g', 'w_ffn_in', 'w_ffn_out', 'norm_final_g']
TWIN_DIFF_INPUT = 'x'
TWIN_INPUTS = ['x', 'mem', 'norm_mix_g', 'w_in', 'rel_bias', 'sg_ln_g', 'sg_ln_b', 'sg_w', 'sg_b', 'w_branch_att', 'w_branch_sg', 'w_out', 'norm_xattn_g', 'norm_mem_g', 'w_xq', 'w_xkv', 'w_xo', 'norm_ffn_g', 'w_ffn_in', 'w_ffn_out', 'norm_final_g', 'loss_target', 'm_norm_mix_g', 'm_w_in', 'm_rel_bias', 'm_sg_ln_g', 'm_sg_ln_b', 'm_sg_w', 'm_sg_b', 'm_w_branch_att', 'm_w_branch_sg', 'm_w_out', 'm_norm_xattn_g', 'm_norm_mem_g', 'm_w_xq', 'm_w_xkv', 'm_w_xo', 'm_norm_ffn_g', 'm_w_ffn_in', 'm_w_ffn_out', 'm_norm_final_g', 'v_norm_mix_g', 'v_w_in', 'v_rel_bias', 'v_sg_ln_g', 'v_sg_ln_b', 'v_sg_w', 'v_sg_b', 'v_w_branch_att', 'v_w_branch_sg', 'v_w_out', 'v_norm_xattn_g', 'v_norm_mem_g', 'v_w_xq', 'v_w_xkv', 'v_w_xo', 'v_norm_ffn_g', 'v_w_ffn_in', 'v_w_ffn_out', 'v_norm_final_g']
TWIN_OUTPUTS = ['loss', 'grad_x', 'grad_norm_mix_g', 'grad_w_in', 'grad_rel_bias', 'grad_sg_ln_g', 'grad_sg_ln_b', 'grad_sg_w', 'grad_sg_b', 'grad_w_branch_att', 'grad_w_branch_sg', 'grad_w_out', 'grad_norm_xattn_g', 'grad_norm_mem_g', 'grad_w_xq', 'grad_w_xkv', 'grad_w_xo', 'grad_norm_ffn_g', 'grad_w_ffn_in', 'grad_w_ffn_out', 'grad_norm_final_g', 'delta_norm_mix_g', 'delta_w_in', 'delta_rel_bias', 'delta_sg_ln_g', 'delta_sg_ln_b', 'delta_sg_w', 'delta_sg_b', 'delta_w_branch_att', 'delta_w_branch_sg', 'delta_w_out', 'delta_norm_xattn_g', 'delta_norm_mem_g', 'delta_w_xq', 'delta_w_xkv', 'delta_w_xo', 'delta_norm_ffn_g', 'delta_w_ffn_in', 'delta_w_ffn_out', 'delta_norm_final_g', 'new_m_norm_mix_g', 'new_m_w_in', 'new_m_rel_bias', 'new_m_sg_ln_g', 'new_m_sg_ln_b', 'new_m_sg_w', 'new_m_sg_b', 'new_m_w_branch_att', 'new_m_w_branch_sg', 'new_m_w_out', 'new_m_norm_xattn_g', 'new_m_norm_mem_g', 'new_m_w_xq', 'new_m_w_xkv', 'new_m_w_xo', 'new_m_norm_ffn_g', 'new_m_w_ffn_in', 'new_m_w_ffn_out', 'new_m_norm_final_g', 'new_v_norm_mix_g', 'new_v_w_in', 'new_v_rel_bias', 'new_v_sg_ln_g', 'new_v_sg_ln_b', 'new_v_sg_w', 'new_v_sg_b', 'new_v_w_branch_att', 'new_v_w_branch_sg', 'new_v_w_out', 'new_v_norm_xattn_g', 'new_v_norm_mem_g', 'new_v_w_xq', 'new_v_w_xkv', 'new_v_w_xo', 'new_v_norm_ffn_g', 'new_v_w_ffn_in', 'new_v_w_ffn_out', 'new_v_norm_final_g']
TWIN_LEAF_KINDS = {'loss': 'loss', 'grad_x': 'grad_x', 'grad_norm_mix_g': 'grad_w', 'grad_w_in': 'grad_w', 'grad_rel_bias': 'grad_w', 'grad_sg_ln_g': 'grad_w', 'grad_sg_ln_b': 'grad_w', 'grad_sg_w': 'grad_w', 'grad_sg_b': 'grad_w', 'grad_w_branch_att': 'grad_w', 'grad_w_branch_sg': 'grad_w', 'grad_w_out': 'grad_w', 'grad_norm_xattn_g': 'grad_w', 'grad_norm_mem_g': 'grad_w', 'grad_w_xq': 'grad_w', 'grad_w_xkv': 'grad_w', 'grad_w_xo': 'grad_w', 'grad_norm_ffn_g': 'grad_w', 'grad_w_ffn_in': 'grad_w', 'grad_w_ffn_out': 'grad_w', 'grad_norm_final_g': 'grad_w', 'delta_norm_mix_g': 'delta_w', 'delta_w_in': 'delta_w', 'delta_rel_bias': 'delta_w', 'delta_sg_ln_g': 'delta_w', 'delta_sg_ln_b': 'delta_w', 'delta_sg_w': 'delta_w', 'delta_sg_b': 'delta_w', 'delta_w_branch_att': 'delta_w', 'delta_w_branch_sg': 'delta_w', 'delta_w_out': 'delta_w', 'delta_norm_xattn_g': 'delta_w', 'delta_norm_mem_g': 'delta_w', 'delta_w_xq': 'delta_w', 'delta_w_xkv': 'delta_w', 'delta_w_xo': 'delta_w', 'delta_norm_ffn_g': 'delta_w', 'delta_w_ffn_in': 'delta_w', 'delta_w_ffn_out': 'delta_w', 'delta_norm_final_g': 'delta_w', 'new_m_norm_mix_g': 'new_m', 'new_m_w_in': 'new_m', 'new_m_rel_bias': 'new_m', 'new_m_sg_ln_g': 'new_m', 'new_m_sg_ln_b': 'new_m', 'new_m_sg_w': 'new_m', 'new_m_sg_b': 'new_m', 'new_m_w_branch_att': 'new_m', 'new_m_w_branch_sg': 'new_m', 'new_m_w_out': 'new_m', 'new_m_norm_xattn_g': 'new_m', 'new_m_norm_mem_g': 'new_m', 'new_m_w_xq': 'new_m', 'new_m_w_xkv': 'new_m', 'new_m_w_xo': 'new_m', 'new_m_norm_ffn_g': 'new_m', 'new_m_w_ffn_in': 'new_m', 'new_m_w_ffn_out': 'new_m', 'new_m_norm_final_g': 'new_m', 'new_v_norm_mix_g': 'new_v', 'new_v_w_in': 'new_v', 'new_v_rel_bias': 'new_v', 'new_v_sg_ln_g': 'new_v', 'new_v_sg_ln_b': 'new_v', 'new_v_sg_w': 'new_v', 'new_v_sg_b': 'new_v', 'new_v_w_branch_att': 'new_v', 'new_v_w_branch_sg': 'new_v', 'new_v_w_out': 'new_v', 'new_v_norm_xattn_g': 'new_v', 'new_v_norm_mem_g': 'new_v', 'new_v_w_xq': 'new_v', 'new_v_w_xkv': 'new_v', 'new_v_w_xo': 'new_v', 'new_v_norm_ffn_g': 'new_v', 'new_v_w_ffn_in': 'new_v', 'new_v_w_ffn_out': 'new_v', 'new_v_norm_final_g': 'new_v'}


def _forward(args):
    return _fwd_reference(*[args[k] for k in FWD_PARAMS])


def _output_shape():
    def fwd():
        inp = _fwd_setup_inputs(0)
        return _fwd_reference(*[inp[k] for k in FWD_PARAMS])
    out = _jax.eval_shape(fwd)
    return out.shape, out.dtype

N_MICROBATCH = 1
ADAM_LR = 0.001
ADAM_B1 = 0.9
ADAM_B2 = 0.999
ADAM_EPS = 1e-08
ADAM_WD = 0.01
ADAM_STEP = 10
PER_EXAMPLE_BATCH_AXIS = {'x': 0, 'mem': 0, 'loss_target': 0}
SHARED_INPUTS = []
_WEIGHT_DTYPES = {'norm_mix_g': _jnp.float32, 'w_in': _jnp.float32, 'rel_bias': _jnp.float32, 'sg_ln_g': _jnp.float32, 'sg_ln_b': _jnp.float32, 'sg_w': _jnp.float32, 'sg_b': _jnp.float32, 'w_branch_att': _jnp.float32, 'w_branch_sg': _jnp.float32, 'w_out': _jnp.float32, 'norm_xattn_g': _jnp.float32, 'norm_mem_g': _jnp.float32, 'w_xq': _jnp.float32, 'w_xkv': _jnp.float32, 'w_xo': _jnp.float32, 'norm_ffn_g': _jnp.float32, 'w_ffn_in': _jnp.float32, 'w_ffn_out': _jnp.float32, 'norm_final_g': _jnp.float32}
MOMENT_SCALE = {'norm_mix_g': 1.664618e-01, 'w_in': 7.680339e-02, 'rel_bias': 1.444492e-02, 'sg_ln_g': 1.140827e-01, 'sg_ln_b': 1.184028e-01, 'sg_w': 7.696165e-02, 'sg_b': 9.707237e-02, 'w_branch_att': 2.228427e-02, 'w_branch_sg': 1.217595e-01, 'w_out': 1.198450e-01, 'norm_xattn_g': 2.682160e-02, 'norm_mem_g': 4.059104e-02, 'w_xq': 2.643040e-02, 'w_xkv': 2.661925e-02, 'w_xo': 2.686203e-02, 'norm_ffn_g': 1.718991e-01, 'w_ffn_in': 7.312620e-02, 'w_ffn_out': 1.194713e-01, 'norm_final_g': 6.403402e+01}


def _to_microbatches(a, axis):
    t = _jnp.moveaxis(a, axis, 0)
    t = t.reshape((N_MICROBATCH, t.shape[0] // N_MICROBATCH) + t.shape[1:])
    return _jnp.moveaxis(t, 1, axis + 1)


def setup_inputs(seed: int = 0) -> dict:
    inp = _fwd_setup_inputs(seed)
    key = _jax.random.fold_in(_jax.random.key(seed), 7919)
    shape, _ = _output_shape()
    out = dict(inp)
    out["loss_target"] = _jax.random.normal(_jax.random.fold_in(key, 0), shape, _jnp.float32)
    for i, name in enumerate(TWIN_WEIGHTS):
        w = inp[name].astype(_jnp.float32)
        if MOMENT_SCALE is None:
            s = _jnp.sqrt(_jnp.mean(_jnp.square(w)) + 1e-30)
        else:
            s = MOMENT_SCALE[name]
        km, kv = _jax.random.split(_jax.random.fold_in(key, i + 1))
        out[name] = w
        out["m_" + name] = s * _jax.random.normal(km, w.shape, _jnp.float32)
        out["v_" + name] = (s * s) * _jax.random.uniform(kv, w.shape, _jnp.float32, 0.5, 1.5)
    if N_MICROBATCH > 1:
        for name, axis in PER_EXAMPLE_BATCH_AXIS.items():
            out[name] = _to_microbatches(out[name], axis)
    return {'x': out['x'], 'mem': out['mem'], 'norm_mix_g': out['norm_mix_g'], 'w_in': out['w_in'], 'rel_bias': out['rel_bias'], 'sg_ln_g': out['sg_ln_g'], 'sg_ln_b': out['sg_ln_b'], 'sg_w': out['sg_w'], 'sg_b': out['sg_b'], 'w_branch_att': out['w_branch_att'], 'w_branch_sg': out['w_branch_sg'], 'w_out': out['w_out'], 'norm_xattn_g': out['norm_xattn_g'], 'norm_mem_g': out['norm_mem_g'], 'w_xq': out['w_xq'], 'w_xkv': out['w_xkv'], 'w_xo': out['w_xo'], 'norm_ffn_g': out['norm_ffn_g'], 'w_ffn_in': out['w_ffn_in'], 'w_ffn_out': out['w_ffn_out'], 'norm_final_g': out['norm_final_g'], 'loss_target': out['loss_target'], 'm_norm_mix_g': out['m_norm_mix_g'], 'm_w_in': out['m_w_in'], 'm_rel_bias': out['m_rel_bias'], 'm_sg_ln_g': out['m_sg_ln_g'], 'm_sg_ln_b': out['m_sg_ln_b'], 'm_sg_w': out['m_sg_w'], 'm_sg_b': out['m_sg_b'], 'm_w_branch_att': out['m_w_branch_att'], 'm_w_branch_sg': out['m_w_branch_sg'], 'm_w_out': out['m_w_out'], 'm_norm_xattn_g': out['m_norm_xattn_g'], 'm_norm_mem_g': out['m_norm_mem_g'], 'm_w_xq': out['m_w_xq'], 'm_w_xkv': out['m_w_xkv'], 'm_w_xo': out['m_w_xo'], 'm_norm_ffn_g': out['m_norm_ffn_g'], 'm_w_ffn_in': out['m_w_ffn_in'], 'm_w_ffn_out': out['m_w_ffn_out'], 'm_norm_final_g': out['m_norm_final_g'], 'v_norm_mix_g': out['v_norm_mix_g'], 'v_w_in': out['v_w_in'], 'v_rel_bias': out['v_rel_bias'], 'v_sg_ln_g': out['v_sg_ln_g'], 'v_sg_ln_b': out['v_sg_ln_b'], 'v_sg_w': out['v_sg_w'], 'v_sg_b': out['v_sg_b'], 'v_w_branch_att': out['v_w_branch_att'], 'v_w_branch_sg': out['v_w_branch_sg'], 'v_w_out': out['v_w_out'], 'v_norm_xattn_g': out['v_norm_xattn_g'], 'v_norm_mem_g': out['v_norm_mem_g'], 'v_w_xq': out['v_w_xq'], 'v_w_xkv': out['v_w_xkv'], 'v_w_xo': out['v_w_xo'], 'v_norm_ffn_g': out['v_norm_ffn_g'], 'v_w_ffn_in': out['v_w_ffn_in'], 'v_w_ffn_out': out['v_w_ffn_out'], 'v_norm_final_g': out['v_norm_final_g']}


def _loss(weights, diff, rest, loss_target):
    with _jax.named_scope("forward"):
        args = {**rest, TWIN_DIFF_INPUT: diff, **{k: w.astype(_WEIGHT_DTYPES[k]) for k, w in weights.items()}}
        y = _forward(args)
    with _jax.named_scope("loss_head"):
        err = _jnp.square(y.astype(_jnp.float32) - loss_target)
        return 0.5 * _jnp.sum(_jnp.mean(err, axis=-1)) if err.ndim else 0.5 * err


def _adamw(w, g, m, v):
    m = ADAM_B1 * m + (1.0 - ADAM_B1) * g
    v = ADAM_B2 * v + (1.0 - ADAM_B2) * _jnp.square(g)
    m_hat = m / (1.0 - ADAM_B1 ** ADAM_STEP)
    v_hat = v / (1.0 - ADAM_B2 ** ADAM_STEP)
    delta = -ADAM_LR * (m_hat / (_jnp.sqrt(v_hat) + ADAM_EPS) + ADAM_WD * w)
    return delta, m, v


def reference(x, mem, norm_mix_g, w_in, rel_bias, sg_ln_g, sg_ln_b, sg_w, sg_b, w_branch_att, w_branch_sg, w_out, norm_xattn_g, norm_mem_g, w_xq, w_xkv, w_xo, norm_ffn_g, w_ffn_in, w_ffn_out, norm_final_g, loss_target, m_norm_mix_g, m_w_in, m_rel_bias, m_sg_ln_g, m_sg_ln_b, m_sg_w, m_sg_b, m_w_branch_att, m_w_branch_sg, m_w_out, m_norm_xattn_g, m_norm_mem_g, m_w_xq, m_w_xkv, m_w_xo, m_norm_ffn_g, m_w_ffn_in, m_w_ffn_out, m_norm_final_g, v_norm_mix_g, v_w_in, v_rel_bias, v_sg_ln_g, v_sg_ln_b, v_sg_w, v_sg_b, v_w_branch_att, v_w_branch_sg, v_w_out, v_norm_xattn_g, v_norm_mem_g, v_w_xq, v_w_xkv, v_w_xo, v_norm_ffn_g, v_w_ffn_in, v_w_ffn_out, v_norm_final_g):
    given = dict(x=x, mem=mem, norm_mix_g=norm_mix_g, w_in=w_in, rel_bias=rel_bias, sg_ln_g=sg_ln_g, sg_ln_b=sg_ln_b, sg_w=sg_w, sg_b=sg_b, w_branch_att=w_branch_att, w_branch_sg=w_branch_sg, w_out=w_out, norm_xattn_g=norm_xattn_g, norm_mem_g=norm_mem_g, w_xq=w_xq, w_xkv=w_xkv, w_xo=w_xo, norm_ffn_g=norm_ffn_g, w_ffn_in=w_ffn_in, w_ffn_out=w_ffn_out, norm_final_g=norm_final_g, loss_target=loss_target, m_norm_mix_g=m_norm_mix_g, m_w_in=m_w_in, m_rel_bias=m_rel_bias, m_sg_ln_g=m_sg_ln_g, m_sg_ln_b=m_sg_ln_b, m_sg_w=m_sg_w, m_sg_b=m_sg_b, m_w_branch_att=m_w_branch_att, m_w_branch_sg=m_w_branch_sg, m_w_out=m_w_out, m_norm_xattn_g=m_norm_xattn_g, m_norm_mem_g=m_norm_mem_g, m_w_xq=m_w_xq, m_w_xkv=m_w_xkv, m_w_xo=m_w_xo, m_norm_ffn_g=m_norm_ffn_g, m_w_ffn_in=m_w_ffn_in, m_w_ffn_out=m_w_ffn_out, m_norm_final_g=m_norm_final_g, v_norm_mix_g=v_norm_mix_g, v_w_in=v_w_in, v_rel_bias=v_rel_bias, v_sg_ln_g=v_sg_ln_g, v_sg_ln_b=v_sg_ln_b, v_sg_w=v_sg_w, v_sg_b=v_sg_b, v_w_branch_att=v_w_branch_att, v_w_branch_sg=v_w_branch_sg, v_w_out=v_w_out, v_norm_xattn_g=v_norm_xattn_g, v_norm_mem_g=v_norm_mem_g, v_w_xq=v_w_xq, v_w_xkv=v_w_xkv, v_w_xo=v_w_xo, v_norm_ffn_g=v_norm_ffn_g, v_w_ffn_in=v_w_ffn_in, v_w_ffn_out=v_w_ffn_out, v_norm_final_g=v_norm_final_g)
    weights = {n: given[n] for n in TWIN_WEIGHTS}
    shared = {n: given[n] for n in SHARED_INPUTS}
    per_example = {n: given[n] for n in ['x', 'mem']}
    grad_fn = _jax.value_and_grad(_loss, argnums=(0, 1))

    def one_microbatch(ex, loss_target):
        ex = dict(ex)
        diff = ex.pop(TWIN_DIFF_INPUT)
        return grad_fn(weights, diff, {**shared, **ex}, loss_target)

    if N_MICROBATCH == 1:
        loss, (grad_w, grad_x) = one_microbatch(per_example, given["loss_target"])
    else:
        def body(carry, xs):
            loss_sum, grad_sum = carry
            l_k, (gw_k, gx_k) = one_microbatch(xs[0], xs[1])
            with _jax.named_scope("update"):
                return (loss_sum + l_k, _jax.tree.map(_jnp.add, grad_sum, gw_k)), gx_k

        init = (_jnp.zeros((), _jnp.float32), _jax.tree.map(_jnp.zeros_like, weights))
        (loss, grad_w), grad_x = _jax.lax.scan(body, init, (per_example, given["loss_target"]))
    with _jax.named_scope("update"):
        delta_w, new_m, new_v = {}, {}, {}
        for n in TWIN_WEIGHTS:
            delta_w[n], new_m[n], new_v[n] = _adamw(weights[n], grad_w[n], given["m_" + n], given["v_" + n])
    return (loss, grad_x, *[grad_w[n] for n in TWIN_WEIGHTS], *[delta_w[n] for n in TWIN_WEIGHTS],
            *[new_m[n] for n in TWIN_WEIGHTS], *[new_v[n] for n in TWIN_WEIGHTS])
```

```python
import functools
import math

import numpy as np
import jax
import jax.numpy as jnp
from jax import lax
from jax.experimental import pallas as pl
from jax.experimental.pallas import tpu as pltpu

F32 = jnp.float32
BF16 = jnp.bfloat16

D = 1024
ATT_W = 512
SG_W = 512
IN_COLS = 4608
DFF = 2816
MEM = 256
XH = 4
CHUNK = 64
BAND_KEYS = 640
REL_CLIP = 128
NREL = 2 * REL_CLIP + 1
EPS = 1e-6
NEG = -1e30
N_DEV = 8

ADAM_LR = 0.001
ADAM_B1 = 0.9
ADAM_B2 = 0.999
ADAM_EPS = 1e-08
ADAM_WD = 0.01
ADAM_STEP = 10

LANES = 128
VMEM_LIMIT = 56 * 1024 * 1024
MESH = pl.DeviceIdType.MESH

PACK_COLS = 1024
SMALL_ROWS = 18


def _cparams(n_axes):
    return pltpu.CompilerParams(dimension_semantics=("arbitrary",) * n_axes, vmem_limit_bytes=VMEM_LIMIT)


def _resident(shape):
    zeros = (0,) * len(shape)
    return pl.BlockSpec(shape, lambda *_: zeros, pipeline_mode=pl.Buffered(1))


def _rows(tm, cols, col_block=0):
    return pl.BlockSpec((tm, cols), lambda i: (i, col_block))


def _sigmoid(x):
    return 1.0 / (1.0 + jnp.exp(-x))


_GELU_C = math.sqrt(2.0 / math.pi)


def _gelu(x):
    t = jnp.tanh(_GELU_C * (x + 0.044715 * (x * x * x)))
    return x * (0.5 * (1.0 + t))


def _gelu_and_grad(x):
    x2 = x * x
    t = jnp.tanh(_GELU_C * (x + 0.044715 * (x2 * x)))
    cdf = 0.5 * (1.0 + t)
    dcdf = 0.5 * (1.0 - t * t) * (_GELU_C * (1.0 + 3.0 * 0.044715 * x2))
    return x * cdf, cdf + x * dcdf


def _rstd(x):
    return lax.rsqrt(jnp.mean(x * x, axis=-1, keepdims=True) + EPS)


def _rms_bwd(dh, x, r, g):
    xh = x * r
    dxh = dh * g
    dx = r * (dxh - xh * jnp.mean(dxh * xh, axis=-1, keepdims=True))
    dg = jnp.sum(dh * xh, axis=0, keepdims=True)
    return dx, dg


def _group_sum64(x):
    def one(v):
        lane = lax.broadcasted_iota(jnp.int32, v.shape, 1)
        for k in (1, 2, 4, 8, 16, 32):
            up = pltpu.roll(v, LANES - k, 1)
            down = pltpu.roll(v, k, 1)
            v = v + jnp.where((lane & k) == 0, up, down)
        return v

    pieces = [one(x[:, LANES * j:LANES * (j + 1)]) for j in range(x.shape[1] // LANES)]
    return pieces[0] if len(pieces) == 1 else jnp.concatenate(pieces, axis=1)


def _dot(a, b):
    return jnp.dot(a, b, preferred_element_type=F32)


def _dot_nt(a, b):
    return lax.dot_general(a, b, (((1,), (1,)), ((), ())), preferred_element_type=F32)


def _dot_tn(a, b):
    return lax.dot_general(a, b, (((0,), (0,)), ((), ())), preferred_element_type=F32)


def _rel_index(a, b):
    return jnp.clip(a - b + 8 * CHUNK, -REL_CLIP, REL_CLIP) + REL_CLIP


def _bias_table(rel_bias_pad):
    def body(rb_ref, out_ref):
        rb = rb_ref[...]
        r_idx = lax.broadcasted_iota(jnp.int32, (384, BAND_KEYS), 0)
        b_idx = lax.broadcasted_iota(jnp.int32, (384, BAND_KEYS), 1)
        b_row = lax.broadcasted_iota(jnp.int32, (1, BAND_KEYS), 1)

        def step(a, carry):
            onehot = (_rel_index(a, b_idx) == r_idx).astype(F32)
            row = jnp.dot(rb, onehot, preferred_element_type=F32, precision=lax.Precision.HIGHEST)
            kc = b_row // CHUNK
            qc = a // CHUNK
            ok = (kc >= qc) & (kc <= qc + 8)
            out_ref[a] = jnp.where(ok, row, NEG)
            return carry

        lax.fori_loop(0, 128, step, 0)

    return pl.pallas_call(
        body, name="bias_table",
        out_shape=jax.ShapeDtypeStruct((128, 8, BAND_KEYS), F32),
        in_specs=[pl.BlockSpec(memory_space=pltpu.VMEM)],
        out_specs=pl.BlockSpec(memory_space=pltpu.VMEM),
    )(rel_bias_pad)


def _bias_grad(ds_sum):
    def body(ds_ref, out_ref):
        b_idx = lax.broadcasted_iota(jnp.int32, (BAND_KEYS, 384), 0)
        r_idx = lax.broadcasted_iota(jnp.int32, (BAND_KEYS, 384), 1)

        def step(a, acc):
            onehot = (_rel_index(a, b_idx) == r_idx).astype(F32)
            return acc + jnp.dot(ds_ref[a], onehot, preferred_element_type=F32, precision=lax.Precision.HIGHEST)

        out_ref[...] = lax.fori_loop(0, 128, step, jnp.zeros((8, 384), F32))

    return pl.pallas_call(
        body, name="bias_grad",
        out_shape=jax.ShapeDtypeStruct((8, 384), F32),
        in_specs=[pl.BlockSpec(memory_space=pltpu.VMEM)],
        out_specs=pl.BlockSpec(memory_space=pltpu.VMEM),
    )(ds_sum)


def _in_proj(x, g1, w_in, tm=512):
    S = x.shape[0]

    def body(x_ref, g_ref, w_ref, qkv_ref, uv_ref, gate_ref, h_ref):
        xv = x_ref[...]
        h = (xv * _rstd(xv) * g_ref[...]).astype(BF16)
        h_ref[...] = h
        for c in range(IN_COLS // 512):
            zc = _dot(h, w_ref[:, 512 * c:512 * (c + 1)])
            if c == 0:
                qkv_ref[:, 0:512] = (zc * 0.125).astype(BF16)
            elif c < 3:
                qkv_ref[:, 512 * c:512 * (c + 1)] = zc.astype(BF16)
            elif c < 5:
                uv_ref[:, 512 * (c - 3):512 * (c - 2)] = zc.astype(BF16)
            else:
                gate_ref[:, 512 * (c - 5):512 * (c - 4)] = zc.astype(BF16)

    return pl.pallas_call(
        body, name="in_proj", grid=(S // tm,),
        out_shape=(jax.ShapeDtypeStruct((S, 3 * ATT_W), BF16), jax.ShapeDtypeStruct((S, 2 * SG_W), BF16),
                   jax.ShapeDtypeStruct((S, 2 * D), BF16), jax.ShapeDtypeStruct((S, D), BF16)),
        in_specs=[_rows(tm, D), _resident((1, D)), _resident((D, IN_COLS))],
        out_specs=(_rows(tm, 3 * ATT_W), _rows(tm, 2 * SG_W), _rows(tm, 2 * D), _rows(tm, D)),
        compiler_params=_cparams(1),
    )(x, g1, w_in)


ATT_R = 512


def _att_specs():
    R = ATT_R
    q = pl.BlockSpec((R, LANES), lambda j, i: (i, j))
    kp = pl.BlockSpec((R, LANES), lambda j, i: (jnp.maximum(i - 1, 0), 4 + j))
    kc = pl.BlockSpec((R, LANES), lambda j, i: (i, 4 + j))
    vp = pl.BlockSpec((R, LANES), lambda j, i: (jnp.maximum(i - 1, 0), 8 + j))
    vc = pl.BlockSpec((R, LANES), lambda j, i: (i, 8 + j))
    bias = pl.BlockSpec((2, 128, BAND_KEYS), lambda j, i: (j, 0, 0))
    return [q, kp, kc, vp, vc, bias]


def _att_probs(qm, kw, bias_h, valid):
    s = _dot_nt(qm, kw) + bias_h
    s = jnp.where(valid, s, NEG)
    p = jnp.exp(s - jnp.max(s, axis=-1, keepdims=True))
    return p / jnp.sum(p, axis=-1, keepdims=True)


def _attn_fwd(qkv, bias):
    S = qkv.shape[0]
    R = ATT_R

    def body(q_ref, kp_ref, kc_ref, vp_ref, vc_ref, b_ref, o_ref):
        i = pl.program_id(1)
        lane = lax.broadcasted_iota(jnp.int32, (1, LANES), 1)
        col = lax.broadcasted_iota(jnp.int32, (1, BAND_KEYS), 1)
        kwin = jnp.concatenate([kp_ref[...], kc_ref[...]], axis=0)
        vwin = jnp.concatenate([vp_ref[...], vc_ref[...]], axis=0)
        for sub in range(R // 128):
            q2 = q_ref[128 * sub:128 * (sub + 1), :]
            kw = kwin[128 * sub:128 * sub + BAND_KEYS]
            vw = vwin[128 * sub:128 * sub + BAND_KEYS]
            valid = (i * R + (128 * sub - 8 * CHUNK) + col) >= 0
            o = jnp.zeros((128, LANES), F32)
            for hh in range(2):
                mine = (lane >= 64 * hh) & (lane < 64 * (hh + 1))
                p = _att_probs(jnp.where(mine, q2, 0), kw, b_ref[hh], valid)
                o = o + _dot(p.astype(BF16), jnp.where(mine, vw, 0))
            o_ref[128 * sub:128 * (sub + 1), :] = o.astype(BF16)

    return pl.pallas_call(
        body, name="attn_fwd", grid=(4, S // R),
        out_shape=jax.ShapeDtypeStruct((S, ATT_W), BF16),
        in_specs=_att_specs(),
        out_specs=pl.BlockSpec((R, LANES), lambda j, i: (i, j)),
        compiler_params=_cparams(2),
    )(qkv, qkv, qkv, qkv, qkv, bias)


def _sg_mask():
    t = lax.broadcasted_iota(jnp.int32, (128, 128), 0)
    s = lax.broadcasted_iota(jnp.int32, (128, 128), 1)
    return (s // CHUNK) <= (t // CHUNK)


def _sg_layernorm(gv, lng, lnb):
    mu = _group_sum64(gv) * (1.0 / 64)
    xc = gv - mu
    var = _group_sum64(xc * xc) * (1.0 / 64)
    rstd = lax.rsqrt(var + EPS)
    vhat = xc * rstd
    return vhat * lng + lnb, vhat, rstd


def _sgu_fwd(uv, lng, lnb, sg_w, b_exp, tm=512):
    S = uv.shape[0]

    def body(uv_ref, lng_ref, lnb_ref, w_ref, b_ref, y_ref):
        lane = lax.broadcasted_iota(jnp.int32, (1, LANES), 1)
        lo = lane < 64
        mask = _sg_mask()
        gu = _gelu(uv_ref[:, 0:SG_W].astype(F32))
        vln, _, _ = _sg_layernorm(_gelu(uv_ref[:, SG_W:2 * SG_W].astype(F32)), lng_ref[...], lnb_ref[...])
        for gp in range(4):
            w0 = jnp.where(mask, w_ref[2 * gp], 0).astype(BF16)
            w1 = jnp.where(mask, w_ref[2 * gp + 1], 0).astype(BF16)
            cols = slice(128 * gp, 128 * (gp + 1))
            for n in range(tm // 128):
                rows = slice(128 * n, 128 * (n + 1))
                vl = vln[rows, cols]
                sv = (_dot(w0, jnp.where(lo, vl, 0).astype(BF16)) + _dot(w1, jnp.where(lo, 0, vl).astype(BF16))
                      + b_ref[:, cols])
                y_ref[rows, cols] = (gu[rows, cols] * sv).astype(BF16)

    return pl.pallas_call(
        body, name="sgu_fwd", grid=(S // tm,),
        out_shape=jax.ShapeDtypeStruct((S, SG_W), BF16),
        in_specs=[_rows(tm, 2 * SG_W), _resident((1, SG_W)), _resident((1, SG_W)),
                  _resident((8, 128, 128)), _resident((128, SG_W))],
        out_specs=_rows(tm, SG_W),
        compiler_params=_cparams(1),
    )(uv, lng, lnb, sg_w, b_exp)


def _merge_fwd(x, y_att, y_sg, gates, wba, wbs, w_out, tm=512):
    S = x.shape[0]

    def body(x_ref, ya_ref, ys_ref, g_ref, wba_ref, wbs_ref, wo_ref, x1_ref):
        a = _dot(ya_ref[...], wba_ref[...])
        b = _dot(ys_ref[...], wbs_ref[...])
        merged = _sigmoid(g_ref[:, 0:D].astype(F32)) * a + _sigmoid(g_ref[:, D:2 * D].astype(F32)) * b
        x1_ref[...] = x_ref[...] + _dot(merged.astype(BF16), wo_ref[...])

    return pl.pallas_call(
        body, name="merge_fwd", grid=(S // tm,),
        out_shape=jax.ShapeDtypeStruct((S, D), F32),
        in_specs=[_rows(tm, D), _rows(tm, ATT_W), _rows(tm, SG_W), _rows(tm, 2 * D),
                  _resident((ATT_W, D)), _resident((SG_W, D)), _resident((D, D))],
        out_specs=_rows(tm, D),
        compiler_params=_cparams(1),
    )(x, y_att, y_sg, gates, wba, wbs, w_out)


def _mem_kv(mem, g_mem, w_xkv):
    def body(m_ref, g_ref, w_ref, kv_ref, mn_ref):
        mv = m_ref[...]
        mn = (mv * _rstd(mv) * g_ref[...]).astype(BF16)
        mn_ref[...] = mn
        kv_ref[...] = _dot(mn, w_ref[...]).astype(BF16)

    vm = pl.BlockSpec(memory_space=pltpu.VMEM)
    return pl.pallas_call(
        body, name="mem_kv",
        out_shape=(jax.ShapeDtypeStruct((MEM, 2 * D), BF16), jax.ShapeDtypeStruct((MEM, D), BF16)),
        in_specs=[vm, vm, vm], out_specs=(vm, vm),
        compiler_params=pltpu.CompilerParams(vmem_limit_bytes=VMEM_LIMIT),
    )(mem, g_mem, w_xkv)


def _xatt_head(qx, kv_ref, h):
    hs = slice(256 * h, 256 * (h + 1))
    s = _dot_nt(qx[:, hs], kv_ref[:, hs])
    p = jnp.exp(s - jnp.max(s, axis=-1, keepdims=True))
    return p / jnp.sum(p, axis=-1, keepdims=True)


def _xattn_fwd(x1, g2, w_xq, kv, w_xo, tm=512):
    S = x1.shape[0]

    def body(x_ref, g_ref, wq_ref, kv_ref, wo_ref, x2_ref):
        xv = x_ref[...]
        hx = (xv * _rstd(xv) * g_ref[...]).astype(BF16)
        qx = (_dot(hx, wq_ref[...]) * (1.0 / 16)).astype(BF16)
        outs = []
        for h in range(XH):
            p = _xatt_head(qx, kv_ref, h)
            outs.append(_dot(p.astype(BF16), kv_ref[:, D + 256 * h:D + 256 * (h + 1)]).astype(BF16))
        o = jnp.concatenate(outs, axis=1)
        x2_ref[...] = xv + _dot(o, wo_ref[...])

    return pl.pallas_call(
        body, name="xattn_fwd", grid=(S // tm,),
        out_shape=jax.ShapeDtypeStruct((S, D), F32),
        in_specs=[_rows(tm, D), _resident((1, D)), _resident((D, D)), _resident((MEM, 2 * D)), _resident((D, D))],
        out_specs=_rows(tm, D),
        compiler_params=_cparams(1),
    )(x1, g2, w_xq, kv, w_xo)


FF_CHUNK = 1408


def _ffn_fwd(x2, tgt, g3, w_ffn_in, w_ffn_out, g4, tm=256):
    S = x2.shape[0]

    def body(x_ref, t_ref, g3_ref, wi_ref, wo_ref, g4_ref, dx3_ref, gu_ref, hf_ref, act_ref, loss_ref, dg4_ref):
        i = pl.program_id(0)
        xv = x_ref[...]
        hf = (xv * _rstd(xv) * g3_ref[...]).astype(BF16)
        hf_ref[...] = hf
        acc = xv
        for c in range(DFF // FF_CHUNK):
            cs = slice(FF_CHUNK * c, FF_CHUNK * (c + 1))
            us = slice(DFF + FF_CHUNK * c, DFF + FF_CHUNK * (c + 1))
            gate = _dot(hf, wi_ref[:, cs])
            up = _dot(hf, wi_ref[:, us])
            gu_ref[:, cs] = gate.astype(BF16)
            gu_ref[:, us] = up.astype(BF16)
            act = ((gate * _sigmoid(gate)) * up).astype(BF16)
            act_ref[:, cs] = act
            acc = acc + _dot(act, wo_ref[cs, :])
        r4 = _rstd(acc)
        g4 = g4_ref[...]
        diff = acc * r4 * g4 - t_ref[...]
        dy = diff * (1.0 / D)
        dx3, dg4 = _rms_bwd(dy, acc, r4, g4)
        dx3_ref[...] = dx3
        part = 0.5 * jnp.sum(jnp.mean(diff * diff, axis=-1, keepdims=True))

        @pl.when(i == 0)
        def _():
            loss_ref[...] = jnp.zeros_like(loss_ref)
            dg4_ref[...] = jnp.zeros_like(dg4_ref)

        loss_ref[...] += jnp.full(loss_ref.shape, part, F32)
        dg4_ref[...] += dg4

    return pl.pallas_call(
        body, name="ffn_fwd", grid=(S // tm,),
        out_shape=(jax.ShapeDtypeStruct((S, D), F32), jax.ShapeDtypeStruct((S, 2 * DFF), BF16),
                   jax.ShapeDtypeStruct((S, D), BF16), jax.ShapeDtypeStruct((S, DFF), BF16),
                   jax.ShapeDtypeStruct((8, LANES), F32), jax.ShapeDtypeStruct((1, D), F32)),
        in_specs=[_rows(tm, D), _rows(tm, D), _resident((1, D)), _resident((D, 2 * DFF)), _resident((DFF, D)),
                  _resident((1, D))],
        out_specs=(_rows(tm, D), _rows(tm, 2 * DFF), _rows(tm, D), _rows(tm, DFF),
                   pl.BlockSpec((8, LANES), lambda i: (0, 0)), pl.BlockSpec((1, D), lambda i: (0, 0))),
        compiler_params=_cparams(1),
    )(x2, tgt, g3, w_ffn_in, w_ffn_out, g4)


def _ffn_bwd(dx3, gu, x2, g3, w_out_t, w_in_t, tm=256):
    S = x2.shape[0]

    def body(d_ref, gu_ref, x_ref, g3_ref, wot_ref, wit_ref, dx2_ref, dgu_ref, dg3_ref):
        i = pl.program_id(0)
        d3 = d_ref[...]
        d3b = d3.astype(BF16)
        dhf = jnp.zeros((tm, D), F32)
        for c in range(DFF // FF_CHUNK):
            cs = slice(FF_CHUNK * c, FF_CHUNK * (c + 1))
            us = slice(DFF + FF_CHUNK * c, DFF + FF_CHUNK * (c + 1))
            da = _dot(d3b, wot_ref[:, cs])
            gate = gu_ref[:, cs].astype(F32)
            up = gu_ref[:, us].astype(F32)
            sg = _sigmoid(gate)
            dgate = (da * up * (sg * (1.0 + gate * (1.0 - sg)))).astype(BF16)
            dup = (da * (gate * sg)).astype(BF16)
            dgu_ref[:, cs] = dgate
            dgu_ref[:, us] = dup
            dhf = dhf + _dot(dgate, wit_ref[cs, :]) + _dot(dup, wit_ref[us, :])
        xv = x_ref[...]
        dx, dg3 = _rms_bwd(dhf, xv, _rstd(xv), g3_ref[...])
        dx2_ref[...] = d3 + dx

        @pl.when(i == 0)
        def _():
            dg3_ref[...] = jnp.zeros_like(dg3_ref)

        dg3_ref[...] += dg3

    return pl.pallas_call(
        body, name="ffn_bwd", grid=(S // tm,),
        out_shape=(jax.ShapeDtypeStruct((S, D), F32), jax.ShapeDtypeStruct((S, 2 * DFF), BF16),
                   jax.ShapeDtypeStruct((1, D), F32)),
        in_specs=[_rows(tm, D), _rows(tm, 2 * DFF), _rows(tm, D), _resident((1, D)),
                  _resident((D, DFF)), _resident((2 * DFF, D))],
        out_specs=(_rows(tm, D), _rows(tm, 2 * DFF), pl.BlockSpec((1, D), lambda i: (0, 0))),
        compiler_params=_cparams(1),
    )(dx3, gu, x2, g3, w_out_t, w_in_t)


def _dw(a, b, tn, ts, name, out_dtype=BF16):
    S, M = a.shape
    N = b.shape[1]
    nk = S // ts

    def body(a_ref, b_ref, o_ref, acc_ref):
        k = pl.program_id(1)

        @pl.when(k == 0)
        def _():
            acc_ref[...] = jnp.zeros_like(acc_ref)

        acc_ref[...] += _dot_tn(a_ref[...].astype(BF16), b_ref[...].astype(BF16))

        @pl.when(k == nk - 1)
        def _():
            o_ref[...] = acc_ref[...].astype(out_dtype)

    return pl.pallas_call(
        body, name=name, grid=(N // tn, nk),
        out_shape=jax.ShapeDtypeStruct((M, N), out_dtype),
        in_specs=[pl.BlockSpec((ts, M), lambda n, k: (k, 0)), pl.BlockSpec((ts, tn), lambda n, k: (k, n))],
        out_specs=pl.BlockSpec((M, tn), lambda n, k: (0, n)),
        scratch_shapes=[pltpu.VMEM((M, tn), F32)],
        compiler_params=_cparams(2),
    )(a, b)


def _xattn_bwd(dx2, x1, g2, w_xq, w_xq_t, w_xo_t, kv, tm=512):
    S = x1.shape[0]

    def body(d_ref, x_ref, g_ref, wq_ref, wqt_ref, wot_ref, kv_ref, dx1_ref, o_ref, dq_ref, hx_ref, dkv_ref, dg2_ref):
        i = pl.program_id(0)

        @pl.when(i == 0)
        def _():
            dkv_ref[...] = jnp.zeros_like(dkv_ref)
            dg2_ref[...] = jnp.zeros_like(dg2_ref)

        d2 = d_ref[...]
        xv = x_ref[...]
        r2 = _rstd(xv)
        hx = (xv * r2 * g_ref[...]).astype(BF16)
        hx_ref[...] = hx
        qx = (_dot(hx, wq_ref[...]) * (1.0 / 16)).astype(BF16)
        do = _dot(d2.astype(BF16), wot_ref[...]).astype(BF16)
        for h in range(XH):
            hs = slice(256 * h, 256 * (h + 1))
            vs = slice(D + 256 * h, D + 256 * (h + 1))
            p = _xatt_head(qx, kv_ref, h)
            pb = p.astype(BF16)
            o_ref[:, hs] = _dot(pb, kv_ref[:, vs]).astype(BF16)
            dp = _dot_nt(do[:, hs], kv_ref[:, vs])
            ds = (p * (dp - jnp.sum(dp * p, axis=-1, keepdims=True))).astype(BF16)
            dq_ref[:, hs] = (_dot(ds, kv_ref[:, hs]) * (1.0 / 16)).astype(BF16)
            dkv_ref[:, hs] += _dot_tn(ds, qx[:, hs])
            dkv_ref[:, vs] += _dot_tn(pb, do[:, hs])
        dhx = _dot(dq_ref[...], wqt_ref[...])
        dx, dg2 = _rms_bwd(dhx, xv, r2, g_ref[...])
        dx1_ref[...] = d2 + dx
        dg2_ref[...] += dg2

    return pl.pallas_call(
        body, name="xattn_bwd", grid=(S // tm,),
        out_shape=(jax.ShapeDtypeStruct((S, D), F32), jax.ShapeDtypeStruct((S, D), BF16),
                   jax.ShapeDtypeStruct((S, D), BF16), jax.ShapeDtypeStruct((S, D), BF16),
                   jax.ShapeDtypeStruct((MEM, 2 * D), F32), jax.ShapeDtypeStruct((1, D), F32)),
        in_specs=[_rows(tm, D), _rows(tm, D), _resident((1, D)), _resident((D, D)), _resident((D, D)),
                  _resident((D, D)), _resident((MEM, 2 * D))],
        out_specs=(_rows(tm, D), _rows(tm, D), _rows(tm, D), _rows(tm, D),
                   pl.BlockSpec((MEM, 2 * D), lambda i: (0, 0)), pl.BlockSpec((1, D), lambda i: (0, 0))),
        compiler_params=_cparams(1),
    )(dx2, x1, g2, w_xq, w_xq_t, w_xo_t, kv)


def _mem_kv_bwd(dkv, mem, g_mem, mn, w_xkv_t):
    def body(dkv_ref, m_ref, g_ref, mn_ref, wt_ref, dw_ref, dg_ref):
        dkvb = dkv_ref[...].astype(BF16)
        dw_ref[...] = _dot_tn(mn_ref[...], dkvb).astype(BF16)
        dmn = _dot(dkvb, wt_ref[...])
        mv = m_ref[...]
        dg_ref[...] = jnp.sum(dmn * (mv * _rstd(mv)), axis=0, keepdims=True)

    vm = pl.BlockSpec(memory_space=pltpu.VMEM)
    return pl.pallas_call(
        body, name="mem_kv_bwd",
        out_shape=(jax.ShapeDtypeStruct((D, 2 * D), BF16), jax.ShapeDtypeStruct((1, D), F32)),
        in_specs=[vm] * 5, out_specs=(vm, vm),
        compiler_params=pltpu.CompilerParams(vmem_limit_bytes=VMEM_LIMIT),
    )(dkv, mem, g_mem, mn, w_xkv_t)


def _merge_bwd(dx1, y_att, y_sg, gates, wba, wbs, w_out_t, wba_t, wbs_t, tm=512):
    S = dx1.shape[0]

    def body(d_ref, ya_ref, ys_ref, g_ref, wba_ref, wbs_ref, wot_ref, wbat_ref, wbst_ref,
             mg_ref, da_ref, db_ref, dya_ref, dys_ref, dg_ref):
        dm = _dot(d_ref[...].astype(BF16), wot_ref[...])
        a = _dot(ya_ref[...], wba_ref[...])
        b = _dot(ys_ref[...], wbs_ref[...])
        sa = _sigmoid(g_ref[:, 0:D].astype(F32))
        sb = _sigmoid(g_ref[:, D:2 * D].astype(F32))
        mg_ref[...] = (sa * a + sb * b).astype(BF16)
        da = (dm * sa).astype(BF16)
        db = (dm * sb).astype(BF16)
        da_ref[...] = da
        db_ref[...] = db
        dg_ref[:, 0:D] = (dm * a * sa * (1.0 - sa)).astype(BF16)
        dg_ref[:, D:2 * D] = (dm * b * sb * (1.0 - sb)).astype(BF16)
        dya_ref[...] = _dot(da, wbat_ref[...]).astype(BF16)
        dys_ref[...] = _dot(db, wbst_ref[...]).astype(BF16)

    return pl.pallas_call(
        body, name="merge_bwd", grid=(S // tm,),
        out_shape=(jax.ShapeDtypeStruct((S, D), BF16), jax.ShapeDtypeStruct((S, D), BF16),
                   jax.ShapeDtypeStruct((S, D), BF16), jax.ShapeDtypeStruct((S, ATT_W), BF16),
                   jax.ShapeDtypeStruct((S, SG_W), BF16), jax.ShapeDtypeStruct((S, 2 * D), BF16)),
        in_specs=[_rows(tm, D), _rows(tm, ATT_W), _rows(tm, SG_W), _rows(tm, 2 * D),
                  _resident((ATT_W, D)), _resident((SG_W, D)), _resident((D, D)),
                  _resident((D, ATT_W)), _resident((D, SG_W))],
        out_specs=(_rows(tm, D), _rows(tm, D), _rows(tm, D), _rows(tm, ATT_W), _rows(tm, SG_W), _rows(tm, 2 * D)),
        compiler_params=_cparams(1),
    )(dx1, y_att, y_sg, gates, wba, wbs, w_out_t, wba_t, wbs_t)


def _sgu_bwd(uv, dy_sg, lng, lnb, sg_w, b_exp, tm=512):
    S = uv.shape[0]
    n_steps = S // tm

    def body(uv_ref, dy_ref, lng_ref, lnb_ref, w_ref, b_ref, duv_ref, dw_ref, dbx_ref, dlng_ref, dlnb_ref, dvln_ref):
        i = pl.program_id(0)

        @pl.when(i == 0)
        def _():
            dw_ref[...] = jnp.zeros_like(dw_ref)
            dbx_ref[...] = jnp.zeros_like(dbx_ref)
            dlng_ref[...] = jnp.zeros_like(dlng_ref)
            dlnb_ref[...] = jnp.zeros_like(dlnb_ref)

        lane = lax.broadcasted_iota(jnp.int32, (1, LANES), 1)
        lo = lane < 64
        mask = _sg_mask()
        lng = lng_ref[...]
        gu, dgelu_u = _gelu_and_grad(uv_ref[:, 0:SG_W].astype(F32))
        gv, dgelu_v = _gelu_and_grad(uv_ref[:, SG_W:2 * SG_W].astype(F32))
        vln, vhat, rstd = _sg_layernorm(gv, lng, lnb_ref[...])
        dy = dy_ref[...].astype(F32)
        dsv_all = dy * gu
        for gp in range(4):
            wf0 = jnp.where(mask, w_ref[2 * gp], 0)
            wf1 = jnp.where(mask, w_ref[2 * gp + 1], 0)
            w0 = wf0.astype(BF16)
            w1 = wf1.astype(BF16)
            cols = slice(128 * gp, 128 * (gp + 1))
            dw0 = jnp.zeros((128, 128), F32)
            dw1 = jnp.zeros((128, 128), F32)
            dbx = jnp.zeros((128, LANES), F32)
            for n in range(tm // 128):
                rows = slice(128 * n, 128 * (n + 1))
                vl = vln[rows, cols]
                vl0 = jnp.where(lo, vl, 0).astype(BF16)
                vl1 = jnp.where(lo, 0, vl).astype(BF16)
                sv = _dot(w0, vl0) + _dot(w1, vl1) + b_ref[:, cols]
                duv_ref[rows, cols] = (dy[rows, cols] * sv * dgelu_u[rows, cols]).astype(BF16)
                dsv = dsv_all[rows, cols]
                dbx = dbx + dsv
                ds0 = jnp.where(lo, dsv, 0).astype(BF16)
                ds1 = jnp.where(lo, 0, dsv).astype(BF16)
                dw0 = dw0 + _dot_nt(ds0, vl0)
                dw1 = dw1 + _dot_nt(ds1, vl1)
                dvln_ref[rows, cols] = _dot_tn(w0, ds0) + _dot_tn(w1, ds1)
            dw_ref[2 * gp] += jnp.where(mask, dw0, 0)
            dw_ref[2 * gp + 1] += jnp.where(mask, dw1, 0)
            dbx_ref[:, cols] += dbx
        dvln = dvln_ref[...]
        dlng_ref[...] += jnp.sum(dvln * vhat, axis=0, keepdims=True)
        dlnb_ref[...] += jnp.sum(dvln, axis=0, keepdims=True)
        dvh = dvln * lng
        dgv = rstd * (dvh - _group_sum64(dvh) * (1.0 / 64) - vhat * (_group_sum64(dvh * vhat) * (1.0 / 64)))
        duv_ref[:, SG_W:2 * SG_W] = (dgv * dgelu_v).astype(BF16)

        @pl.when(i == n_steps - 1)
        def _():
            dbx_ref[...] = _group_sum64(dbx_ref[...])

    return pl.pallas_call(
        body, name="sgu_bwd", grid=(n_steps,),
        out_shape=(jax.ShapeDtypeStruct((S, 2 * SG_W), BF16), jax.ShapeDtypeStruct((8, 128, 128), F32),
                   jax.ShapeDtypeStruct((128, SG_W), F32), jax.ShapeDtypeStruct((1, SG_W), F32),
                   jax.ShapeDtypeStruct((1, SG_W), F32)),
        in_specs=[_rows(tm, 2 * SG_W), _rows(tm, SG_W), _resident((1, SG_W)), _resident((1, SG_W)),
                  _resident((8, 128, 128)), _resident((128, SG_W))],
        out_specs=(_rows(tm, 2 * SG_W), pl.BlockSpec((8, 128, 128), lambda i: (0, 0, 0)),
                   pl.BlockSpec((128, SG_W), lambda i: (0, 0)), pl.BlockSpec((1, SG_W), lambda i: (0, 0)),
                   pl.BlockSpec((1, SG_W), lambda i: (0, 0))),
        scratch_shapes=[pltpu.VMEM((tm, SG_W), F32)],
        compiler_params=_cparams(1),
    )(uv, dy_sg, lng, lnb, sg_w, b_exp)


def _attn_bwd(qkv, dy_att, bias):
    S = qkv.shape[0]
    R = ATT_R

    def body(q_ref, kp_ref, kc_ref, vp_ref, vc_ref, b_ref, dy_ref, dq_ref, dk_ref, dv_ref, dss_ref):
        i = pl.program_id(1)

        @pl.when(i == 0)
        def _():
            dk_ref[...] = jnp.zeros_like(dk_ref)
            dv_ref[...] = jnp.zeros_like(dv_ref)
            dss_ref[...] = jnp.zeros_like(dss_ref)

        lane = lax.broadcasted_iota(jnp.int32, (1, LANES), 1)
        col = lax.broadcasted_iota(jnp.int32, (1, BAND_KEYS), 1)
        kwin = jnp.concatenate([kp_ref[...], kc_ref[...]], axis=0)
        vwin = jnp.concatenate([vp_ref[...], vc_ref[...]], axis=0)
        for sub in range(R // 128):
            rows = slice(128 * sub, 128 * (sub + 1))
            q2 = q_ref[rows, :]
            do2 = dy_ref[rows, :]
            kw = kwin[128 * sub:128 * sub + BAND_KEYS]
            vw = vwin[128 * sub:128 * sub + BAND_KEYS]
            valid = (i * R + (128 * sub - 8 * CHUNK) + col) >= 0
            dq = jnp.zeros((128, LANES), F32)
            dkw = jnp.zeros((BAND_KEYS, LANES), F32)
            dvw = jnp.zeros((BAND_KEYS, LANES), F32)
            for hh in range(2):
                mine = (lane >= 64 * hh) & (lane < 64 * (hh + 1))
                qm = jnp.where(mine, q2, 0)
                dom = jnp.where(mine, do2, 0)
                p = _att_probs(qm, kw, b_ref[hh], valid)
                dp = _dot_nt(dom, vw)
                ds = p * (dp - jnp.sum(dp * p, axis=-1, keepdims=True))
                dss_ref[hh] += ds
                dsb = ds.astype(BF16)
                dq = dq + _dot(dsb, jnp.where(mine, kw, 0))
                dkw = dkw + _dot_tn(dsb, qm)
                dvw = dvw + _dot_tn(p.astype(BF16), dom)
            dq_ref[rows, :] = dq.astype(BF16)
            start = pl.multiple_of(i * R + 128 * sub, 128)
            dk_ref[pl.ds(start, BAND_KEYS), :] += dkw
            dv_ref[pl.ds(start, BAND_KEYS), :] += dvw

    acc_spec = pl.BlockSpec((S + 8 * CHUNK, LANES), lambda j, i: (0, j))
    return pl.pallas_call(
        body, name="attn_bwd", grid=(4, S // R),
        out_shape=(jax.ShapeDtypeStruct((S, ATT_W), BF16), jax.ShapeDtypeStruct((S + 8 * CHUNK, ATT_W), F32),
                   jax.ShapeDtypeStruct((S + 8 * CHUNK, ATT_W), F32), jax.ShapeDtypeStruct((8, 128, BAND_KEYS), F32)),
        in_specs=_att_specs() + [pl.BlockSpec((R, LANES), lambda j, i: (i, j))],
        out_specs=(pl.BlockSpec((R, LANES), lambda j, i: (i, j)), acc_spec, acc_spec,
                   pl.BlockSpec((2, 128, BAND_KEYS), lambda j, i: (j, 0, 0))),
        compiler_params=_cparams(2),
    )(qkv, qkv, qkv, qkv, qkv, bias, dy_att)


def _in_bwd(dq, dk, dv, duv, dgates, x, dx1, g1, w_in_t, tm=256):
    S = x.shape[0]
    pad_blocks = (8 * CHUNK) // tm

    def body(dq_ref, dk_ref, dv_ref, duv_ref, dg_ref, x_ref, d1_ref, g_ref, wt_ref, dx_ref, dz_ref, dg1_ref):
        i = pl.program_id(0)
        dz_ref[:, 0:ATT_W] = (dq_ref[...].astype(F32) * 0.125).astype(BF16)
        dz_ref[:, ATT_W:2 * ATT_W] = dk_ref[...].astype(BF16)
        dz_ref[:, 2 * ATT_W:3 * ATT_W] = dv_ref[...].astype(BF16)
        dz_ref[:, 3 * ATT_W:3 * ATT_W + 2 * SG_W] = duv_ref[...]
        dz_ref[:, 3 * ATT_W + 2 * SG_W:IN_COLS] = dg_ref[...]
        dh = jnp.zeros((tm, D), F32)
        for c in range(IN_COLS // 512):
            cs = slice(512 * c, 512 * (c + 1))
            dh = dh + _dot(dz_ref[:, cs], wt_ref[cs, :])
        xv = x_ref[...]
        dx, dg1 = _rms_bwd(dh, xv, _rstd(xv), g_ref[...])
        dx_ref[...] = d1_ref[...] + dx

        @pl.when(i == 0)
        def _():
            dg1_ref[...] = jnp.zeros_like(dg1_ref)

        dg1_ref[...] += dg1

    shifted = pl.BlockSpec((tm, ATT_W), lambda i: (i + pad_blocks, 0))
    return pl.pallas_call(
        body, name="in_bwd", grid=(S // tm,),
        out_shape=(jax.ShapeDtypeStruct((S, D), F32), jax.ShapeDtypeStruct((S, IN_COLS), BF16),
                   jax.ShapeDtypeStruct((1, D), F32)),
        in_specs=[_rows(tm, ATT_W), shifted, shifted, _rows(tm, 2 * SG_W), _rows(tm, 2 * D), _rows(tm, D),
                  _rows(tm, D), _resident((1, D)), _resident((IN_COLS, D))],
        out_specs=(_rows(tm, D), _rows(tm, IN_COLS), pl.BlockSpec((1, D), lambda i: (0, 0))),
        compiler_params=_cparams(1),
    )(dq, dk, dv, duv, dgates, x, dx1, g1, w_in_t)


def _adam_math(w, g, m, v):
    m = ADAM_B1 * m + (1.0 - ADAM_B1) * g
    v = ADAM_B2 * v + (1.0 - ADAM_B2) * (g * g)
    m_hat = m / (1.0 - ADAM_B1 ** ADAM_STEP)
    v_hat = v / (1.0 - ADAM_B2 ** ADAM_STEP)
    delta = -ADAM_LR * (m_hat / (jnp.sqrt(v_hat) + ADAM_EPS) + ADAM_WD * w)
    return delta, m, v


def _adam(parts, w, m, v, tr, name):
    P, R, C = parts.shape

    def body(p_ref, w_ref, m_ref, v_ref, g_ref, d_ref, mo_ref, vo_ref):
        g = p_ref[0].astype(F32)
        for k in range(1, P):
            g = g + p_ref[k].astype(F32)
        g_ref[...] = g
        d_ref[...], mo_ref[...], vo_ref[...] = _adam_math(w_ref[...], g, m_ref[...], v_ref[...])

    row = pl.BlockSpec((tr, C), lambda i: (i, 0))
    return pl.pallas_call(
        body, name=name, grid=(R // tr,),
        out_shape=tuple(jax.ShapeDtypeStruct((R, C), F32) for _ in range(4)),
        in_specs=[pl.BlockSpec((P, tr, C), lambda i: (0, i, 0)), row, row, row],
        out_specs=(row, row, row, row),
        compiler_params=_cparams(1),
    )(parts, w, m, v)


def _my_place():
    return lax.axis_index("x"), lax.axis_index("y"), lax.axis_index("c")


def _other_chips(x, y):
    return [(1 - x, y), (x, 1 - y), (1 - x, 1 - y)]


def _all_gather(shards, name):
    n = len(shards)

    def body(*refs):
        ins, outs = refs[:n], refs[n:2 * n]
        send_sems, recv_sems, local_sems = refs[2 * n:]
        x, y, c = _my_place()
        me, sibling = (x, y, c), (x, y, 1 - c)
        chips = _other_chips(x, y)

        def idx(p):
            return 4 * p[0] + 2 * p[1] + p[2]

        def copy(a, k, block, to, src=None):
            return pltpu.make_async_remote_copy(
                src_ref=outs[a].at[idx(block)] if src is None else src, dst_ref=outs[a].at[idx(block)],
                send_sem=send_sems.at[a, k], recv_sem=recv_sems.at[a, k], device_id=to, device_id_type=MESH)

        mine = [pltpu.make_async_copy(ins[a], outs[a].at[idx(me)], local_sems.at[a]) for a in range(n)]
        for cp in mine:
            cp.start()
        first = []
        for a in range(n):
            first.append(copy(a, 0, me, sibling, src=ins[a]))
            first += [copy(a, 1 + j, me, (*chip, c), src=ins[a]) for j, chip in enumerate(chips)]
        for cp in first:
            cp.start()
        passed = []
        for j, chip in enumerate(chips):
            for a in range(n):
                copy(a, 1 + j, (*chip, c), me).wait_recv()
                fwd = copy(a, 4 + j, (*chip, c), sibling)
                fwd.start()
                passed.append(fwd)
        for a in range(n):
            copy(a, 0, sibling, me).wait_recv()
            for j, chip in enumerate(chips):
                copy(a, 4 + j, (*chip, 1 - c), me).wait_recv()
        for cp in first + passed:
            cp.wait_send()
        for cp in mine:
            cp.wait()

    hbm = pl.BlockSpec(memory_space=pl.ANY)
    return pl.pallas_call(
        body, name=name,
        out_shape=tuple(jax.ShapeDtypeStruct((N_DEV,) + s.shape, s.dtype) for s in shards),
        in_specs=[hbm] * n, out_specs=tuple([hbm] * n),
        scratch_shapes=[pltpu.SemaphoreType.DMA((n, 7)), pltpu.SemaphoreType.DMA((n, 7)),
                        pltpu.SemaphoreType.DMA((n,))],
    )(*shards)


def _exchange_sibling(parts, name):
    n = len(parts)

    def body(*refs):
        ins, outs = refs[:n], refs[n:2 * n]
        send_sems, recv_sems = refs[2 * n:]
        x, y, c = _my_place()
        copies = []
        for a in range(n):
            for k in range(4):
                cp = pltpu.make_async_remote_copy(
                    src_ref=ins[a].at[2 * k + (1 - c)], dst_ref=outs[a].at[k],
                    send_sem=send_sems.at[a, k], recv_sem=recv_sems.at[a, k],
                    device_id=(x, y, 1 - c), device_id_type=MESH)
                cp.start()
                copies.append(cp)
        for cp in copies:
            cp.wait()

    hbm = pl.BlockSpec(memory_space=pl.ANY)
    return pl.pallas_call(
        body, name=name,
        out_shape=tuple(jax.ShapeDtypeStruct((4,) + p.shape[1:], p.dtype) for p in parts),
        in_specs=[hbm] * n, out_specs=tuple([hbm] * n),
        scratch_shapes=[pltpu.SemaphoreType.DMA((n, 4)), pltpu.SemaphoreType.DMA((n, 4))],
    )(*parts)


def _pair_sum(mine, theirs, c_idx, tr, name):
    _, R, C = mine.shape

    def body(c_ref, a_ref, b_ref, o_ref):
        o_ref[...] = (a_ref[...].astype(F32) + b_ref[...].astype(F32)).astype(o_ref.dtype)

    grid_spec = pltpu.PrefetchScalarGridSpec(
        num_scalar_prefetch=1, grid=(4, R // tr),
        in_specs=[pl.BlockSpec((1, tr, C), lambda k, r, c_ref: (2 * k + c_ref[0], r, 0)),
                  pl.BlockSpec((1, tr, C), lambda k, r, c_ref: (k, r, 0))],
        out_specs=pl.BlockSpec((1, tr, C), lambda k, r, c_ref: (k, r, 0)))
    return pl.pallas_call(
        body, name=name, grid_spec=grid_spec,
        out_shape=jax.ShapeDtypeStruct((4, R, C), mine.dtype),
        compiler_params=_cparams(2),
    )(c_idx, mine, theirs)


def _exchange_chips(sums, name):
    n = len(sums)

    def body(*refs):
        ins, outs = refs[:n], refs[n:2 * n]
        send_sems, recv_sems, local_sems = refs[2 * n:]
        x, y, c = _my_place()
        my_chip = 2 * x + y
        local = [pltpu.make_async_copy(ins[a].at[my_chip], outs[a].at[my_chip], local_sems.at[a]) for a in range(n)]
        for cp in local:
            cp.start()
        copies = []
        for a in range(n):
            for j, (px, py) in enumerate(_other_chips(x, y)):
                cp = pltpu.make_async_remote_copy(
                    src_ref=ins[a].at[2 * px + py], dst_ref=outs[a].at[my_chip],
                    send_sem=send_sems.at[a, j], recv_sem=recv_sems.at[a, j],
                    device_id=(px, py, c), device_id_type=MESH)
                cp.start()
                copies.append(cp)
        for cp in copies:
            cp.wait()
        for cp in local:
            cp.wait()

    hbm = pl.BlockSpec(memory_space=pl.ANY)
    return pl.pallas_call(
        body, name=name,
        out_shape=tuple(jax.ShapeDtypeStruct(s.shape, s.dtype) for s in sums),
        in_specs=[hbm] * n, out_specs=tuple([hbm] * n),
        scratch_shapes=[pltpu.SemaphoreType.DMA((n, 3)), pltpu.SemaphoreType.DMA((n, 3)),
                        pltpu.SemaphoreType.DMA((n,))],
    )(*sums)


def _sum4(parts, name):
    _, R, C = parts.shape

    def body(p_ref, o_ref):
        o_ref[...] = ((p_ref[0] + p_ref[1]) + p_ref[2]) + p_ref[3]

    vm = pl.BlockSpec(memory_space=pltpu.VMEM)
    return pl.pallas_call(body, name=name, out_shape=jax.ShapeDtypeStruct((R, C), F32),
                          in_specs=[vm], out_specs=vm)(parts)


BIG = [("w_in", (D, IN_COLS), 1), ("w_branch_att", (ATT_W, D), 1), ("w_branch_sg", (SG_W, D), 1),
       ("w_out", (D, D), 0), ("w_xq", (D, D), 0), ("w_xkv", (D, 2 * D), 1), ("w_xo", (D, D), 0),
       ("w_ffn_in", (D, 2 * DFF), 1), ("w_ffn_out", (DFF, D), 0)]
SMALL = [("norm_mix_g", (1, D)), ("rel_bias", (1, 8, NREL)), ("sg_ln_g", (1, 8, 64)), ("sg_ln_b", (1, 8, 64)),
         ("sg_w", (1, 8, 128, 128)), ("sg_b", (1, 8, 128)), ("norm_xattn_g", (1, D)), ("norm_mem_g", (1, D)),
         ("norm_ffn_g", (1, D)), ("norm_final_g", (D,))]
SMALL_TOTAL = N_DEV * SMALL_ROWS * PACK_COLS


def _shard_shape(shape, axis):
    s = list(shape)
    s[axis] //= N_DEV
    return tuple(s)


def _pack_big(shards):
    return jnp.concatenate([shards[n].reshape(-1, PACK_COLS) for n, _, _ in BIG], axis=0)


def _unpack_big(packed):
    out, r = {}, 0
    for n, shape, axis in BIG:
        ss = _shard_shape(shape, axis)
        rows = ss[0] * ss[1] // PACK_COLS
        out[n] = packed[r:r + rows].reshape(ss)
        r += rows
    return out


def _gathered_full(gathered):
    out, r = {}, 0
    for n, shape, axis in BIG:
        ss = _shard_shape(shape, axis)
        rows = ss[0] * ss[1] // PACK_COLS
        blk = gathered[:, r:r + rows].reshape((N_DEV,) + ss)
        out[n] = blk.reshape(shape) if axis == 0 else jnp.transpose(blk, (1, 0, 2)).reshape(shape)
        r += rows
    return out


def _scatter_blocks(full):
    parts = []
    for n, shape, axis in BIG:
        ss = _shard_shape(shape, axis)
        g = full[n]
        blk = g.reshape((N_DEV,) + ss) if axis == 0 else jnp.transpose(g.reshape(shape[0], N_DEV, ss[1]), (1, 0, 2))
        parts.append(blk.reshape(N_DEV, -1, PACK_COLS))
    return jnp.concatenate(parts, axis=1)


def _pack_small(vals):
    flat = jnp.concatenate([vals[n].reshape(-1) for n, _ in SMALL])
    return jnp.pad(flat, (0, SMALL_TOTAL - flat.shape[0])).reshape(N_DEV * SMALL_ROWS, PACK_COLS)


def _unpack_small(packed):
    flat, out, r = packed.reshape(-1), {}, 0
    for n, shape in SMALL:
        size = int(np.prod(shape))
        out[n] = flat[r:r + size].reshape(shape)
        r += size
    return out


def _local_step(x, mem, tgt, w, small):
    g1, g2, g3 = (small[n].reshape(1, D) for n in ("norm_mix_g", "norm_xattn_g", "norm_ffn_g"))
    g_mem, g4 = small["norm_mem_g"].reshape(1, D), small["norm_final_g"].reshape(1, D)
    lng = small["sg_ln_g"].reshape(1, SG_W)
    lnb = small["sg_ln_b"].reshape(1, SG_W)
    sg_w = small["sg_w"].reshape(8, 128, 128)
    b_exp = jnp.broadcast_to(small["sg_b"].reshape(8, 128).T[:, :, None], (128, 8, 64)).reshape(128, SG_W)
    rel_pad = jnp.pad(small["rel_bias"].reshape(8, NREL), ((0, 0), (0, 384 - NREL)))
    wt = {n: w[n].T for n in ("w_in", "w_branch_att", "w_branch_sg", "w_out", "w_xq", "w_xkv", "w_xo",
                              "w_ffn_in", "w_ffn_out")}

    bias = jnp.transpose(_bias_table(rel_pad), (1, 0, 2))
    qkv, uv, gates, h = _in_proj(x, g1, w["w_in"])
    y_att = _attn_fwd(qkv, bias)
    y_sg = _sgu_fwd(uv, lng, lnb, sg_w, b_exp)
    x1 = _merge_fwd(x, y_att, y_sg, gates, w["w_branch_att"], w["w_branch_sg"], w["w_out"])
    kv, mn = _mem_kv(mem, g_mem, w["w_xkv"])
    x2 = _xattn_fwd(x1, g2, w["w_xq"], kv, w["w_xo"])
    dx3, gu, hf, act, loss_acc, dg4 = _ffn_fwd(x2, tgt, g3, w["w_ffn_in"], w["w_ffn_out"], g4)

    gw = {}
    dx2, dgu, dg3 = _ffn_bwd(dx3, gu, x2, g3, wt["w_ffn_out"], wt["w_ffn_in"])
    gw["w_ffn_out"] = _dw(act, dx3, 512, 512, "dw_ffn_out")
    gw["w_ffn_in"] = _dw(hf, dgu, 512, 1024, "dw_ffn_in")
    dx1, o_x, dq_x, hx, dkv, dg2 = _xattn_bwd(dx2, x1, g2, w["w_xq"], wt["w_xq"], wt["w_xo"], kv)
    gw["w_xo"] = _dw(o_x, dx2, 512, 1024, "dw_xo")
    gw["w_xq"] = _dw(hx, dq_x, 512, 1024, "dw_xq")
    gw["w_xkv"], dg_mem = _mem_kv_bwd(dkv, mem, g_mem, mn, wt["w_xkv"])
    merged, d_a, d_b, dy_att, dy_sg, dgates = _merge_bwd(
        dx1, y_att, y_sg, gates, w["w_branch_att"], w["w_branch_sg"], wt["w_out"], wt["w_branch_att"],
        wt["w_branch_sg"])
    gw["w_out"] = _dw(merged, dx1, 512, 1024, "dw_out")
    gw["w_branch_att"] = _dw(y_att, d_a, 512, 1024, "dw_branch_att")
    gw["w_branch_sg"] = _dw(y_sg, d_b, 512, 1024, "dw_branch_sg")
    duv, d_sgw, d_bx, d_lng, d_lnb = _sgu_bwd(uv, dy_sg, lng, lnb, sg_w, b_exp)
    dq, dk, dv, ds_sum = _attn_bwd(qkv, dy_att, bias)
    d_rel = _bias_grad(jnp.transpose(ds_sum, (1, 0, 2)))
    grad_x, dz, dg1 = _in_bwd(dq, dk, dv, duv, dgates, x, dx1, g1, wt["w_in"])
    gw["w_in"] = _dw(h, dz, 512, 1024, "dw_in")

    gs = {"norm_mix_g": dg1, "rel_bias": d_rel[:, :NREL], "sg_ln_g": d_lng, "sg_ln_b": d_lnb, "sg_w": d_sgw,
          "sg_b": d_bx.reshape(128, 8, 64)[:, :, 0].T, "norm_xattn_g": dg2, "norm_mem_g": dg_mem,
          "norm_ffn_g": dg3, "norm_final_g": dg4}
    return loss_acc[0, 0], grad_x, gw, gs


def kernel(x, mem, norm_mix_g, w_in, rel_bias, sg_ln_g, sg_ln_b, sg_w, sg_b, w_branch_att, w_branch_sg, w_out, norm_xattn_g, norm_mem_g, w_xq, w_xkv, w_xo, norm_ffn_g, w_ffn_in, w_ffn_out, norm_final_g, loss_target, m_norm_mix_g, m_w_in, m_rel_bias, m_sg_ln_g, m_sg_ln_b, m_sg_w, m_sg_b, m_w_branch_att, m_w_branch_sg, m_w_out, m_norm_xattn_g, m_norm_mem_g, m_w_xq, m_w_xkv, m_w_xo, m_norm_ffn_g, m_w_ffn_in, m_w_ffn_out, m_norm_final_g, v_norm_mix_g, v_w_in, v_rel_bias, v_sg_ln_g, v_sg_ln_b, v_sg_w, v_sg_b, v_w_branch_att, v_w_branch_sg, v_w_out, v_norm_xattn_g, v_norm_mem_g, v_w_xq, v_w_xkv, v_w_xo, v_norm_ffn_g, v_w_ffn_in, v_w_ffn_out, v_norm_final_g):
    args = dict(locals())
    big_names = [n for n, _, _ in BIG]
    small_names = [n for n, _ in SMALL]
    S = x.shape[1]

    w_shard = {n: args[n][0] for n in big_names}
    packed_w = _pack_big({n: w_shard[n].astype(BF16) for n in big_names})
    (gathered,) = _all_gather([packed_w], "ag_weights")
    w_full = _gathered_full(gathered)
    small = {n: args[n] for n in small_names}

    loss_part, grad_x, gw, gs = _local_step(x.reshape(S, D), mem.reshape(MEM, D), loss_target.reshape(S, D), w_full,
                                            small)

    big_parts = _scatter_blocks(gw)
    small_parts = _pack_small(gs).reshape(N_DEV, SMALL_ROWS, PACK_COLS)
    c_idx = lax.axis_index("c").astype(jnp.int32).reshape(1)
    sib_big, sib_small = _exchange_sibling([big_parts, small_parts], "rs_sibling")
    chip_big = _pair_sum(big_parts, sib_big, c_idx, 480, "rs_pair_big")
    chip_small = _pair_sum(small_parts, sib_small, c_idx, SMALL_ROWS, "rs_pair_small")
    all_big, all_small = _exchange_chips([chip_big, chip_small], "rs_chips")

    pw = _pack_big(w_shard)
    pm = _pack_big({n: args["m_" + n][0] for n in big_names})
    pv = _pack_big({n: args["v_" + n][0] for n in big_names})
    g_b, d_b, m_b, v_b = (_unpack_big(t) for t in _adam(all_big, pw, pm, pv, 480, "adam_big"))

    small_sum = _sum4(all_small, "rs_small_sum")
    (small_all,) = _all_gather([small_sum], "ag_small")
    g_small = small_all.reshape(1, N_DEV * SMALL_ROWS, PACK_COLS)
    g_s, d_s, m_s, v_s = (_unpack_small(t) for t in _adam(
        g_small, _pack_small(small), _pack_small({n: args["m_" + n] for n in small_names}),
        _pack_small({n: args["v_" + n] for n in small_names}), N_DEV * SMALL_ROWS, "adam_small"))

    loss = lax.psum(loss_part, ("x", "y", "c"))

    def leaf(big, sm, n):
        if n in big:
            return big[n][None]
        return sm[n]

    order = ["norm_mix_g", "w_in", "rel_bias", "sg_ln_g", "sg_ln_b", "sg_w", "sg_b", "w_branch_att", "w_branch_sg",
             "w_out", "norm_xattn_g", "norm_mem_g", "w_xq", "w_xkv", "w_xo", "norm_ffn_g", "w_ffn_in", "w_ffn_out",
             "norm_final_g"]
    outs = [loss, grad_x.reshape(1, S, D)]
    for big, sm in ((g_b, g_s), (d_b, d_s), (m_b, m_s), (v_b, v_s)):
        outs += [leaf(big, sm, n) for n in order]
    return tuple(outs)
```

```python
import math

import jax
import jax.numpy as jnp
from jax import lax
from jax.experimental import pallas as pl
from jax.experimental.pallas import tpu as pltpu

F32 = jnp.float32
BF16 = jnp.bfloat16

D = 1024
ATT_W = 512
SG_W = 512
IN_COLS = 4608
DFF = 2816
MEM = 256
XH = 4
CHUNK = 64
BAND_KEYS = 640
REL_CLIP = 128
NREL = 2 * REL_CLIP + 1
EPS = 1e-6
NEG = -1e30
N_DEV = 8

ADAM_LR = 0.001
ADAM_B1 = 0.9
ADAM_B2 = 0.999
ADAM_EPS = 1e-08
ADAM_WD = 0.01
ADAM_STEP = 10

LANES = 128
VMEM_LIMIT = 56 * 1024 * 1024
MESH = pl.DeviceIdType.MESH


def _cparams(n_axes):
    return pltpu.CompilerParams(dimension_semantics=("arbitrary",) * n_axes, vmem_limit_bytes=VMEM_LIMIT)


def _resident(shape):
    zeros = (0,) * len(shape)
    return pl.BlockSpec(shape, lambda *_: zeros, pipeline_mode=pl.Buffered(1))


def _rows(tm, cols, col_block=0):
    return pl.BlockSpec((tm, cols), lambda i: (i, col_block))


def _sigmoid(x):
    return 1.0 / (1.0 + jnp.exp(-x))


_GELU_C = math.sqrt(2.0 / math.pi)


def _gelu(x):
    t = jnp.tanh(_GELU_C * (x + 0.044715 * (x * x * x)))
    return x * (0.5 * (1.0 + t))


def _gelu_and_grad(x):
    x2 = x * x
    t = jnp.tanh(_GELU_C * (x + 0.044715 * (x2 * x)))
    cdf = 0.5 * (1.0 + t)
    dcdf = 0.5 * (1.0 - t * t) * (_GELU_C * (1.0 + 3.0 * 0.044715 * x2))
    return x * cdf, cdf + x * dcdf


def _rstd(x):
    return lax.rsqrt(jnp.mean(x * x, axis=-1, keepdims=True) + EPS)


def _rms_bwd(dh, x, r, g):
    xh = x * r
    dxh = dh * g
    dx = r * (dxh - xh * jnp.mean(dxh * xh, axis=-1, keepdims=True))
    dg = jnp.sum(dh * xh, axis=0, keepdims=True)
    return dx, dg


def _group_sum64(x):
    def one(v):
        lane = lax.broadcasted_iota(jnp.int32, v.shape, 1)
        for k in (1, 2, 4, 8, 16, 32):
            up = pltpu.roll(v, LANES - k, 1)
            down = pltpu.roll(v, k, 1)
            v = v + jnp.where((lane & k) == 0, up, down)
        return v

    pieces = [one(x[:, LANES * j:LANES * (j + 1)]) for j in range(x.shape[1] // LANES)]
    return pieces[0] if len(pieces) == 1 else jnp.concatenate(pieces, axis=1)


def _dot(a, b):
    return jnp.dot(a, b, preferred_element_type=F32)


def _dot_nt(a, b):
    return lax.dot_general(a, b, (((1,), (1,)), ((), ())), preferred_element_type=F32)


def _dot_tn(a, b):
    return lax.dot_general(a, b, (((0,), (0,)), ((), ())), preferred_element_type=F32)


DIAGS = 768


def _diag_onehot():
    r_idx = lax.broadcasted_iota(jnp.int32, (384, DIAGS), 0)
    t_idx = lax.broadcasted_iota(jnp.int32, (384, DIAGS), 1)
    dist = (8 * CHUNK + 127) - t_idx
    return (jnp.clip(dist, -REL_CLIP, REL_CLIP) + REL_CLIP == r_idx).astype(F32)


def _shift_rows(x, reverse):
    row = lax.broadcasted_iota(jnp.int32, x.shape, 0)
    for k in range(7):
        amt = (DIAGS - (1 << k)) if reverse else (1 << k)
        x = jnp.where(((row >> k) & 1) == 1, pltpu.roll(x, amt, 1), x)
    return x


def _bias_table(rel_bias_pad):
    def body(rb_ref, out_ref):
        per_diag = jnp.dot(rb_ref[...], _diag_onehot(), preferred_element_type=F32,
                           precision=lax.Precision.HIGHEST)
        a = lax.broadcasted_iota(jnp.int32, (128, BAND_KEYS), 0)
        b = lax.broadcasted_iota(jnp.int32, (128, BAND_KEYS), 1)
        ok = (b // CHUNK >= a // CHUNK) & (b // CHUNK <= a // CHUNK + 8)
        for h in range(8):
            rows = jnp.broadcast_to(per_diag[h:h + 1, :], (128, DIAGS))
            table = _shift_rows(pltpu.roll(rows, DIAGS - 127, 1), reverse=False)
            out_ref[h] = jnp.where(ok, table[:, :BAND_KEYS], NEG)

    return pl.pallas_call(
        body, name="bias_table",
        out_shape=jax.ShapeDtypeStruct((8, 128, BAND_KEYS), F32),
        in_specs=[pl.BlockSpec(memory_space=pltpu.VMEM)],
        out_specs=pl.BlockSpec(memory_space=pltpu.VMEM),
    )(rel_bias_pad)


def _bias_grad(ds_sum):
    def body(ds_ref, out_ref):
        sums = []
        for h in range(8):
            padded = jnp.concatenate([ds_ref[h], jnp.zeros((128, DIAGS - BAND_KEYS), F32)], axis=1)
            skewed = pltpu.roll(_shift_rows(padded, reverse=True), 127, 1)
            sums.append(jnp.sum(skewed, axis=0, keepdims=True))
        per_diag = jnp.concatenate(sums, axis=0)
        out_ref[...] = lax.dot_general(per_diag, _diag_onehot(), (((1,), (1,)), ((), ())),
                                       preferred_element_type=F32, precision=lax.Precision.HIGHEST)

    return pl.pallas_call(
        body, name="bias_grad",
        out_shape=jax.ShapeDtypeStruct((8, 384), F32),
        in_specs=[pl.BlockSpec(memory_space=pltpu.VMEM)],
        out_specs=pl.BlockSpec(memory_space=pltpu.VMEM),
    )(ds_sum)


def _in_proj(x, g1, w_in_t, tm=512):
    S = x.shape[0]

    def body(x_ref, g_ref, w_ref, qkv_ref, uv_ref, gate_ref, h_ref):
        xv = x_ref[...]
        h = (xv * _rstd(xv) * g_ref[...]).astype(BF16)
        h_ref[...] = h
        for c in range(IN_COLS // 512):
            zc = _dot_nt(h, w_ref[512 * c:512 * (c + 1), :])
            if c == 0:
                qkv_ref[:, 0:512] = (zc * 0.125).astype(BF16)
            elif c < 3:
                qkv_ref[:, 512 * c:512 * (c + 1)] = zc.astype(BF16)
            elif c < 5:
                uv_ref[:, 512 * (c - 3):512 * (c - 2)] = zc.astype(BF16)
            else:
                gate_ref[:, 512 * (c - 5):512 * (c - 4)] = zc.astype(BF16)

    return pl.pallas_call(
        body, name="in_proj", grid=(S // tm,),
        out_shape=(jax.ShapeDtypeStruct((S, 3 * ATT_W), BF16), jax.ShapeDtypeStruct((S, 2 * SG_W), BF16),
                   jax.ShapeDtypeStruct((S, 2 * D), BF16), jax.ShapeDtypeStruct((S, D), BF16)),
        in_specs=[_rows(tm, D), _resident((1, D)), _resident((IN_COLS, D))],
        out_specs=(_rows(tm, 3 * ATT_W), _rows(tm, 2 * SG_W), _rows(tm, 2 * D), _rows(tm, D)),
        compiler_params=_cparams(1),
    )(x, g1, w_in_t)


ATT_R = 512


def _att_specs():
    R = ATT_R
    q = pl.BlockSpec((R, LANES), lambda j, i: (i, j))
    kp = pl.BlockSpec((R, LANES), lambda j, i: (jnp.maximum(i - 1, 0), 4 + j))
    kc = pl.BlockSpec((R, LANES), lambda j, i: (i, 4 + j))
    vp = pl.BlockSpec((R, LANES), lambda j, i: (jnp.maximum(i - 1, 0), 8 + j))
    vc = pl.BlockSpec((R, LANES), lambda j, i: (i, 8 + j))
    bias = pl.BlockSpec((2, 128, BAND_KEYS), lambda j, i: (j, 0, 0))
    return [q, kp, kc, vp, vc, bias]


def _att_probs(qm, kw, bias_h, valid):
    s = _dot_nt(qm, kw) + bias_h
    s = jnp.where(valid, s, NEG)
    p = jnp.exp(s - jnp.max(s, axis=-1, keepdims=True))
    return p / jnp.sum(p, axis=-1, keepdims=True)


def _attn_fwd(qkv, bias):
    S = qkv.shape[0]
    R = ATT_R

    def body(q_ref, kp_ref, kc_ref, vp_ref, vc_ref, b_ref, o_ref):
        i = pl.program_id(1)
        lane = lax.broadcasted_iota(jnp.int32, (1, LANES), 1)
        col = lax.broadcasted_iota(jnp.int32, (1, BAND_KEYS), 1)
        kwin = jnp.concatenate([kp_ref[...], kc_ref[...]], axis=0)
        vwin = jnp.concatenate([vp_ref[...], vc_ref[...]], axis=0)
        for sub in range(R // 128):
            q2 = q_ref[128 * sub:128 * (sub + 1), :]
            kw = kwin[128 * sub:128 * sub + BAND_KEYS]
            vw = vwin[128 * sub:128 * sub + BAND_KEYS]
            valid = (i * R + (128 * sub - 8 * CHUNK) + col) >= 0
            o = jnp.zeros((128, LANES), F32)
            for hh in range(2):
                mine = (lane >= 64 * hh) & (lane < 64 * (hh + 1))
                p = _att_probs(jnp.where(mine, q2, 0), kw, b_ref[hh], valid)
                o = o + _dot(p.astype(BF16), jnp.where(mine, vw, 0))
            o_ref[128 * sub:128 * (sub + 1), :] = o.astype(BF16)

    return pl.pallas_call(
        body, name="attn_fwd", grid=(4, S // R),
        out_shape=jax.ShapeDtypeStruct((S, ATT_W), BF16),
        in_specs=_att_specs(),
        out_specs=pl.BlockSpec((R, LANES), lambda j, i: (i, j)),
        compiler_params=_cparams(2),
    )(qkv, qkv, qkv, qkv, qkv, bias)


def _sg_mask():
    t = lax.broadcasted_iota(jnp.int32, (128, 128), 0)
    s = lax.broadcasted_iota(jnp.int32, (128, 128), 1)
    return (s // CHUNK) <= (t // CHUNK)


def _sg_layernorm(gv, lng, lnb):
    mu = _group_sum64(gv) * (1.0 / 64)
    xc = gv - mu
    var = _group_sum64(xc * xc) * (1.0 / 64)
    rstd = lax.rsqrt(var + EPS)
    vhat = xc * rstd
    return vhat * lng + lnb, vhat, rstd


def _sgu_fwd(uv, lng, lnb, sg_w, b_exp, tm=512):
    S = uv.shape[0]

    def body(uv_ref, lng_ref, lnb_ref, w_ref, b_ref, y_ref):
        lane = lax.broadcasted_iota(jnp.int32, (1, LANES), 1)
        lo = lane < 64
        mask = _sg_mask()
        gu = _gelu(uv_ref[:, 0:SG_W].astype(F32))
        vln, _, _ = _sg_layernorm(_gelu(uv_ref[:, SG_W:2 * SG_W].astype(F32)), lng_ref[...], lnb_ref[...])
        for gp in range(4):
            w0 = jnp.where(mask, w_ref[2 * gp], 0).astype(BF16)
            w1 = jnp.where(mask, w_ref[2 * gp + 1], 0).astype(BF16)
            cols = slice(128 * gp, 128 * (gp + 1))
            for n in range(tm // 128):
                rows = slice(128 * n, 128 * (n + 1))
                vl = vln[rows, cols]
                sv = (_dot(w0, jnp.where(lo, vl, 0).astype(BF16)) + _dot(w1, jnp.where(lo, 0, vl).astype(BF16))
                      + b_ref[:, cols])
                y_ref[rows, cols] = (gu[rows, cols] * sv).astype(BF16)

    return pl.pallas_call(
        body, name="sgu_fwd", grid=(S // tm,),
        out_shape=jax.ShapeDtypeStruct((S, SG_W), BF16),
        in_specs=[_rows(tm, 2 * SG_W), _resident((1, SG_W)), _resident((1, SG_W)),
                  _resident((8, 128, 128)), _resident((128, SG_W))],
        out_specs=_rows(tm, SG_W),
        compiler_params=_cparams(1),
    )(uv, lng, lnb, sg_w, b_exp)


def _merge_fwd(x, y_att, y_sg, gates, wba_t, wbs_t, w_out, tm=512):
    S = x.shape[0]

    def body(x_ref, ya_ref, ys_ref, g_ref, wba_ref, wbs_ref, wo_ref, x1_ref):
        a = _dot_nt(ya_ref[...], wba_ref[...])
        b = _dot_nt(ys_ref[...], wbs_ref[...])
        merged = _sigmoid(g_ref[:, 0:D].astype(F32)) * a + _sigmoid(g_ref[:, D:2 * D].astype(F32)) * b
        x1_ref[...] = x_ref[...] + _dot(merged.astype(BF16), wo_ref[...])

    return pl.pallas_call(
        body, name="merge_fwd", grid=(S // tm,),
        out_shape=jax.ShapeDtypeStruct((S, D), F32),
        in_specs=[_rows(tm, D), _rows(tm, ATT_W), _rows(tm, SG_W), _rows(tm, 2 * D),
                  _resident((D, ATT_W)), _resident((D, SG_W)), _resident((D, D))],
        out_specs=_rows(tm, D),
        compiler_params=_cparams(1),
    )(x, y_att, y_sg, gates, wba_t, wbs_t, w_out)


def _mem_kv(mem, g_mem, w_xkv_t):
    def body(m_ref, g_ref, w_ref, kv_ref, mn_ref):
        mv = m_ref[...]
        mn = (mv * _rstd(mv) * g_ref[...]).astype(BF16)
        mn_ref[...] = mn
        kv_ref[...] = _dot_nt(mn, w_ref[...]).astype(BF16)

    vm = pl.BlockSpec(memory_space=pltpu.VMEM)
    return pl.pallas_call(
        body, name="mem_kv",
        out_shape=(jax.ShapeDtypeStruct((MEM, 2 * D), BF16), jax.ShapeDtypeStruct((MEM, D), BF16)),
        in_specs=[vm, vm, vm], out_specs=(vm, vm),
        compiler_params=pltpu.CompilerParams(vmem_limit_bytes=VMEM_LIMIT),
    )(mem, g_mem, w_xkv_t)


def _xatt_head(qx, kv_ref, h):
    hs = slice(256 * h, 256 * (h + 1))
    s = _dot_nt(qx[:, hs], kv_ref[:, hs])
    p = jnp.exp(s - jnp.max(s, axis=-1, keepdims=True))
    return p / jnp.sum(p, axis=-1, keepdims=True)


def _xattn_fwd(x1, g2, w_xq, kv, w_xo, tm=512):
    S = x1.shape[0]

    def body(x_ref, g_ref, wq_ref, kv_ref, wo_ref, x2_ref):
        xv = x_ref[...]
        hx = (xv * _rstd(xv) * g_ref[...]).astype(BF16)
        qx = (_dot(hx, wq_ref[...]) * (1.0 / 16)).astype(BF16)
        outs = []
        for h in range(XH):
            p = _xatt_head(qx, kv_ref, h)
            outs.append(_dot(p.astype(BF16), kv_ref[:, D + 256 * h:D + 256 * (h + 1)]).astype(BF16))
        o = jnp.concatenate(outs, axis=1)
        x2_ref[...] = xv + _dot(o, wo_ref[...])

    return pl.pallas_call(
        body, name="xattn_fwd", grid=(S // tm,),
        out_shape=jax.ShapeDtypeStruct((S, D), F32),
        in_specs=[_rows(tm, D), _resident((1, D)), _resident((D, D)), _resident((MEM, 2 * D)), _resident((D, D))],
        out_specs=_rows(tm, D),
        compiler_params=_cparams(1),
    )(x1, g2, w_xq, kv, w_xo)


FF_CHUNK = 1408


def _ffn_fwd(x2, tgt, g3, w_ffn_in_t, w_ffn_out, g4, tm=256):
    S = x2.shape[0]

    def body(x_ref, t_ref, g3_ref, wi_ref, wo_ref, g4_ref, dx3_ref, gu_ref, hf_ref, act_ref, loss_ref, dg4_ref):
        i = pl.program_id(0)
        xv = x_ref[...]
        hf = (xv * _rstd(xv) * g3_ref[...]).astype(BF16)
        hf_ref[...] = hf
        acc = xv
        for c in range(DFF // FF_CHUNK):
            cs = slice(FF_CHUNK * c, FF_CHUNK * (c + 1))
            us = slice(DFF + FF_CHUNK * c, DFF + FF_CHUNK * (c + 1))
            gate = _dot_nt(hf, wi_ref[cs, :])
            up = _dot_nt(hf, wi_ref[us, :])
            gu_ref[:, cs] = gate.astype(BF16)
            gu_ref[:, us] = up.astype(BF16)
            act = ((gate * _sigmoid(gate)) * up).astype(BF16)
            act_ref[:, cs] = act
            acc = acc + _dot(act, wo_ref[cs, :])
        r4 = _rstd(acc)
        g4 = g4_ref[...]
        diff = acc * r4 * g4 - t_ref[...]
        dy = diff * (1.0 / D)
        dx3, dg4 = _rms_bwd(dy, acc, r4, g4)
        dx3_ref[...] = dx3
        part = 0.5 * jnp.sum(jnp.mean(diff * diff, axis=-1, keepdims=True))

        @pl.when(i == 0)
        def _():
            loss_ref[...] = jnp.zeros_like(loss_ref)
            dg4_ref[...] = jnp.zeros_like(dg4_ref)

        loss_ref[...] += jnp.full(loss_ref.shape, part, F32)
        dg4_ref[...] += dg4

    return pl.pallas_call(
        body, name="ffn_fwd", grid=(S // tm,),
        out_shape=(jax.ShapeDtypeStruct((S, D), F32), jax.ShapeDtypeStruct((S, 2 * DFF), BF16),
                   jax.ShapeDtypeStruct((S, D), BF16), jax.ShapeDtypeStruct((S, DFF), BF16),
                   jax.ShapeDtypeStruct((8, LANES), F32), jax.ShapeDtypeStruct((1, D), F32)),
        in_specs=[_rows(tm, D), _rows(tm, D), _resident((1, D)), _resident((2 * DFF, D)), _resident((DFF, D)),
                  _resident((1, D))],
        out_specs=(_rows(tm, D), _rows(tm, 2 * DFF), _rows(tm, D), _rows(tm, DFF),
                   pl.BlockSpec((8, LANES), lambda i: (0, 0)), pl.BlockSpec((1, D), lambda i: (0, 0))),
        compiler_params=_cparams(1),
    )(x2, tgt, g3, w_ffn_in_t, w_ffn_out, g4)


def _ffn_bwd(dx3, gu, x2, g3, w_ffn_out, w_ffn_in_t, tm=256):
    S = x2.shape[0]

    def body(d_ref, gu_ref, x_ref, g3_ref, wo_ref, wit_ref, dx2_ref, dgu_ref, dg3_ref):
        i = pl.program_id(0)
        d3 = d_ref[...]
        d3b = d3.astype(BF16)
        dhf = jnp.zeros((tm, D), F32)
        for c in range(DFF // FF_CHUNK):
            cs = slice(FF_CHUNK * c, FF_CHUNK * (c + 1))
            us = slice(DFF + FF_CHUNK * c, DFF + FF_CHUNK * (c + 1))
            da = _dot_nt(d3b, wo_ref[cs, :])
            gate = gu_ref[:, cs].astype(F32)
            up = gu_ref[:, us].astype(F32)
            sg = _sigmoid(gate)
            dgate = (da * up * (sg * (1.0 + gate * (1.0 - sg)))).astype(BF16)
            dup = (da * (gate * sg)).astype(BF16)
            dgu_ref[:, cs] = dgate
            dgu_ref[:, us] = dup
            dhf = dhf + _dot(dgate, wit_ref[cs, :]) + _dot(dup, wit_ref[us, :])
        xv = x_ref[...]
        dx, dg3 = _rms_bwd(dhf, xv, _rstd(xv), g3_ref[...])
        dx2_ref[...] = d3 + dx

        @pl.when(i == 0)
        def _():
            dg3_ref[...] = jnp.zeros_like(dg3_ref)

        dg3_ref[...] += dg3

    return pl.pallas_call(
        body, name="ffn_bwd", grid=(S // tm,),
        out_shape=(jax.ShapeDtypeStruct((S, D), F32), jax.ShapeDtypeStruct((S, 2 * DFF), BF16),
                   jax.ShapeDtypeStruct((1, D), F32)),
        in_specs=[_rows(tm, D), _rows(tm, 2 * DFF), _rows(tm, D), _resident((1, D)),
                  _resident((DFF, D)), _resident((2 * DFF, D))],
        out_specs=(_rows(tm, D), _rows(tm, 2 * DFF), pl.BlockSpec((1, D), lambda i: (0, 0))),
        compiler_params=_cparams(1),
    )(dx3, gu, x2, g3, w_ffn_out, w_ffn_in_t)


def _dw(a, b, tmm, tn, ts, name, out_dtype=BF16):
    S, M = a.shape
    N = b.shape[1]
    nk = S // ts

    def body(a_ref, b_ref, o_ref, acc_ref):
        k = pl.program_id(2)

        @pl.when(k == 0)
        def _():
            acc_ref[...] = jnp.zeros_like(acc_ref)

        acc_ref[...] += _dot_tn(a_ref[...].astype(BF16), b_ref[...].astype(BF16))

        @pl.when(k == nk - 1)
        def _():
            o_ref[...] = acc_ref[...].astype(out_dtype)

    return pl.pallas_call(
        body, name=name, grid=(M // tmm, N // tn, nk),
        out_shape=jax.ShapeDtypeStruct((M, N), out_dtype),
        in_specs=[pl.BlockSpec((ts, tmm), lambda m, n, k: (k, m)), pl.BlockSpec((ts, tn), lambda m, n, k: (k, n))],
        out_specs=pl.BlockSpec((tmm, tn), lambda m, n, k: (m, n)),
        scratch_shapes=[pltpu.VMEM((tmm, tn), F32)],
        compiler_params=_cparams(3),
    )(a, b)


def _xattn_bwd(dx2, x1, g2, w_xq, w_xo, kv, tm=512):
    S = x1.shape[0]

    def body(d_ref, x_ref, g_ref, wq_ref, wo_ref, kv_ref, dx1_ref, o_ref, dq_ref, hx_ref, dkv_ref, dg2_ref):
        i = pl.program_id(0)

        @pl.when(i == 0)
        def _():
            dkv_ref[...] = jnp.zeros_like(dkv_ref)
            dg2_ref[...] = jnp.zeros_like(dg2_ref)

        d2 = d_ref[...]
        xv = x_ref[...]
        r2 = _rstd(xv)
        hx = (xv * r2 * g_ref[...]).astype(BF16)
        hx_ref[...] = hx
        qx = (_dot(hx, wq_ref[...]) * (1.0 / 16)).astype(BF16)
        do = _dot_nt(d2.astype(BF16), wo_ref[...]).astype(BF16)
        for h in range(XH):
            hs = slice(256 * h, 256 * (h + 1))
            vs = slice(D + 256 * h, D + 256 * (h + 1))
            p = _xatt_head(qx, kv_ref, h)
            pb = p.astype(BF16)
            o_ref[:, hs] = _dot(pb, kv_ref[:, vs]).astype(BF16)
            dp = _dot_nt(do[:, hs], kv_ref[:, vs])
            ds = (p * (dp - jnp.sum(dp * p, axis=-1, keepdims=True))).astype(BF16)
            dq_ref[:, hs] = (_dot(ds, kv_ref[:, hs]) * (1.0 / 16)).astype(BF16)
            dkv_ref[:, hs] += _dot_tn(ds, qx[:, hs])
            dkv_ref[:, vs] += _dot_tn(pb, do[:, hs])
        dhx = _dot_nt(dq_ref[...], wq_ref[...])
        dx, dg2 = _rms_bwd(dhx, xv, r2, g_ref[...])
        dx1_ref[...] = d2 + dx
        dg2_ref[...] += dg2

    return pl.pallas_call(
        body, name="xattn_bwd", grid=(S // tm,),
        out_shape=(jax.ShapeDtypeStruct((S, D), F32), jax.ShapeDtypeStruct((S, D), BF16),
                   jax.ShapeDtypeStruct((S, D), BF16), jax.ShapeDtypeStruct((S, D), BF16),
                   jax.ShapeDtypeStruct((MEM, 2 * D), F32), jax.ShapeDtypeStruct((1, D), F32)),
        in_specs=[_rows(tm, D), _rows(tm, D), _resident((1, D)), _resident((D, D)), _resident((D, D)),
                  _resident((MEM, 2 * D))],
        out_specs=(_rows(tm, D), _rows(tm, D), _rows(tm, D), _rows(tm, D),
                   pl.BlockSpec((MEM, 2 * D), lambda i: (0, 0)), pl.BlockSpec((1, D), lambda i: (0, 0))),
        compiler_params=_cparams(1),
    )(dx2, x1, g2, w_xq, w_xo, kv)


def _mem_kv_bwd(dkv, mem, g_mem, mn, w_xkv_t):
    def body(dkv_ref, m_ref, g_ref, mn_ref, wt_ref, dw_ref, dg_ref):
        dkvb = dkv_ref[...].astype(BF16)
        dw_ref[...] = _dot_tn(dkvb, mn_ref[...]).astype(BF16)
        dmn = _dot(dkvb, wt_ref[...])
        mv = m_ref[...]
        dg_ref[...] = jnp.sum(dmn * (mv * _rstd(mv)), axis=0, keepdims=True)

    vm = pl.BlockSpec(memory_space=pltpu.VMEM)
    return pl.pallas_call(
        body, name="mem_kv_bwd",
        out_shape=(jax.ShapeDtypeStruct((2 * D, D), BF16), jax.ShapeDtypeStruct((1, D), F32)),
        in_specs=[vm] * 5, out_specs=(vm, vm),
        compiler_params=pltpu.CompilerParams(vmem_limit_bytes=VMEM_LIMIT),
    )(dkv, mem, g_mem, mn, w_xkv_t)


def _merge_bwd(dx1, y_att, y_sg, gates, wba_t, wbs_t, w_out, tm=512):
    S = dx1.shape[0]

    def body(d_ref, ya_ref, ys_ref, g_ref, wbat_ref, wbst_ref, wo_ref,
             mg_ref, da_ref, db_ref, dya_ref, dys_ref, dg_ref):
        dm = _dot_nt(d_ref[...].astype(BF16), wo_ref[...])
        a = _dot_nt(ya_ref[...], wbat_ref[...])
        b = _dot_nt(ys_ref[...], wbst_ref[...])
        sa = _sigmoid(g_ref[:, 0:D].astype(F32))
        sb = _sigmoid(g_ref[:, D:2 * D].astype(F32))
        mg_ref[...] = (sa * a + sb * b).astype(BF16)
        da = (dm * sa).astype(BF16)
        db = (dm * sb).astype(BF16)
        da_ref[...] = da
        db_ref[...] = db
        dg_ref[:, 0:D] = (dm * a * sa * (1.0 - sa)).astype(BF16)
        dg_ref[:, D:2 * D] = (dm * b * sb * (1.0 - sb)).astype(BF16)
        dya_ref[...] = _dot(da, wbat_ref[...]).astype(BF16)
        dys_ref[...] = _dot(db, wbst_ref[...]).astype(BF16)

    return pl.pallas_call(
        body, name="merge_bwd", grid=(S // tm,),
        out_shape=(jax.ShapeDtypeStruct((S, D), BF16), jax.ShapeDtypeStruct((S, D), BF16),
                   jax.ShapeDtypeStruct((S, D), BF16), jax.ShapeDtypeStruct((S, ATT_W), BF16),
                   jax.ShapeDtypeStruct((S, SG_W), BF16), jax.ShapeDtypeStruct((S, 2 * D), BF16)),
        in_specs=[_rows(tm, D), _rows(tm, ATT_W), _rows(tm, SG_W), _rows(tm, 2 * D),
                  _resident((D, ATT_W)), _resident((D, SG_W)), _resident((D, D))],
        out_specs=(_rows(tm, D), _rows(tm, D), _rows(tm, D), _rows(tm, ATT_W), _rows(tm, SG_W), _rows(tm, 2 * D)),
        compiler_params=_cparams(1),
    )(dx1, y_att, y_sg, gates, wba_t, wbs_t, w_out)


def _sgu_bwd(uv, dy_sg, lng, lnb, sg_w, b_exp, tm=512):
    S = uv.shape[0]
    n_steps = S // tm

    def body(uv_ref, dy_ref, lng_ref, lnb_ref, w_ref, b_ref, duv_ref, dw_ref, dbx_ref, dlng_ref, dlnb_ref, dvln_ref):
        i = pl.program_id(0)

        @pl.when(i == 0)
        def _():
            dw_ref[...] = jnp.zeros_like(dw_ref)
            dbx_ref[...] = jnp.zeros_like(dbx_ref)
            dlng_ref[...] = jnp.zeros_like(dlng_ref)
            dlnb_ref[...] = jnp.zeros_like(dlnb_ref)

        lane = lax.broadcasted_iota(jnp.int32, (1, LANES), 1)
        lo = lane < 64
        mask = _sg_mask()
        lng = lng_ref[...]
        gu, dgelu_u = _gelu_and_grad(uv_ref[:, 0:SG_W].astype(F32))
        gv, dgelu_v = _gelu_and_grad(uv_ref[:, SG_W:2 * SG_W].astype(F32))
        vln, vhat, rstd = _sg_layernorm(gv, lng, lnb_ref[...])
        dy = dy_ref[...].astype(F32)
        dsv_all = dy * gu
        for gp in range(4):
            wf0 = jnp.where(mask, w_ref[2 * gp], 0)
            wf1 = jnp.where(mask, w_ref[2 * gp + 1], 0)
            w0 = wf0.astype(BF16)
            w1 = wf1.astype(BF16)
            cols = slice(128 * gp, 128 * (gp + 1))
            dw0 = jnp.zeros((128, 128), F32)
            dw1 = jnp.zeros((128, 128), F32)
            dbx = jnp.zeros((128, LANES), F32)
            for n in range(tm // 128):
                rows = slice(128 * n, 128 * (n + 1))
                vl = vln[rows, cols]
                vl0 = jnp.where(lo, vl, 0).astype(BF16)
                vl1 = jnp.where(lo, 0, vl).astype(BF16)
                sv = _dot(w0, vl0) + _dot(w1, vl1) + b_ref[:, cols]
                duv_ref[rows, cols] = (dy[rows, cols] * sv * dgelu_u[rows, cols]).astype(BF16)
                dsv = dsv_all[rows, cols]
                dbx = dbx + dsv
                ds0 = jnp.where(lo, dsv, 0).astype(BF16)
                ds1 = jnp.where(lo, 0, dsv).astype(BF16)
                dw0 = dw0 + _dot_nt(ds0, vl0)
                dw1 = dw1 + _dot_nt(ds1, vl1)
                dvln_ref[rows, cols] = _dot_tn(w0, ds0) + _dot_tn(w1, ds1)
            dw_ref[2 * gp] += jnp.where(mask, dw0, 0)
            dw_ref[2 * gp + 1] += jnp.where(mask, dw1, 0)
            dbx_ref[:, cols] += dbx
        dvln = dvln_ref[...]
        dlng_ref[...] += jnp.sum(dvln * vhat, axis=0, keepdims=True)
        dlnb_ref[...] += jnp.sum(dvln, axis=0, keepdims=True)
        dvh = dvln * lng
        dgv = rstd * (dvh - _group_sum64(dvh) * (1.0 / 64) - vhat * (_group_sum64(dvh * vhat) * (1.0 / 64)))
        duv_ref[:, SG_W:2 * SG_W] = (dgv * dgelu_v).astype(BF16)

        @pl.when(i == n_steps - 1)
        def _():
            dbx_ref[...] = _group_sum64(dbx_ref[...])

    return pl.pallas_call(
        body, name="sgu_bwd", grid=(n_steps,),
        out_shape=(jax.ShapeDtypeStruct((S, 2 * SG_W), BF16), jax.ShapeDtypeStruct((8, 128, 128), F32),
                   jax.ShapeDtypeStruct((128, SG_W), F32), jax.ShapeDtypeStruct((1, SG_W), F32),
                   jax.ShapeDtypeStruct((1, SG_W), F32)),
        in_specs=[_rows(tm, 2 * SG_W), _rows(tm, SG_W), _resident((1, SG_W)), _resident((1, SG_W)),
                  _resident((8, 128, 128)), _resident((128, SG_W))],
        out_specs=(_rows(tm, 2 * SG_W), pl.BlockSpec((8, 128, 128), lambda i: (0, 0, 0)),
                   pl.BlockSpec((128, SG_W), lambda i: (0, 0)), pl.BlockSpec((1, SG_W), lambda i: (0, 0)),
                   pl.BlockSpec((1, SG_W), lambda i: (0, 0))),
        scratch_shapes=[pltpu.VMEM((tm, SG_W), F32)],
        compiler_params=_cparams(1),
    )(uv, dy_sg, lng, lnb, sg_w, b_exp)


def _attn_bwd(qkv, dy_att, bias):
    S = qkv.shape[0]
    R = ATT_R

    def body(q_ref, kp_ref, kc_ref, vp_ref, vc_ref, b_ref, dy_ref, dq_ref, dk_ref, dv_ref, dss_ref):
        i = pl.program_id(1)

        @pl.when(i == 0)
        def _():
            dk_ref[...] = jnp.zeros_like(dk_ref)
            dv_ref[...] = jnp.zeros_like(dv_ref)
            dss_ref[...] = jnp.zeros_like(dss_ref)

        lane = lax.broadcasted_iota(jnp.int32, (1, LANES), 1)
        col = lax.broadcasted_iota(jnp.int32, (1, BAND_KEYS), 1)
        kwin = jnp.concatenate([kp_ref[...], kc_ref[...]], axis=0)
        vwin = jnp.concatenate([vp_ref[...], vc_ref[...]], axis=0)
        for sub in range(R // 128):
            rows = slice(128 * sub, 128 * (sub + 1))
            q2 = q_ref[rows, :]
            do2 = dy_ref[rows, :]
            kw = kwin[128 * sub:128 * sub + BAND_KEYS]
            vw = vwin[128 * sub:128 * sub + BAND_KEYS]
            valid = (i * R + (128 * sub - 8 * CHUNK) + col) >= 0
            dq = jnp.zeros((128, LANES), F32)
            dkw = jnp.zeros((BAND_KEYS, LANES), F32)
            dvw = jnp.zeros((BAND_KEYS, LANES), F32)
            for hh in range(2):
                mine = (lane >= 64 * hh) & (lane < 64 * (hh + 1))
                qm = jnp.where(mine, q2, 0)
                dom = jnp.where(mine, do2, 0)
                p = _att_probs(qm, kw, b_ref[hh], valid)
                dp = _dot_nt(dom, vw)
                ds = p * (dp - jnp.sum(dp * p, axis=-1, keepdims=True))
                dss_ref[hh] += ds
                dsb = ds.astype(BF16)
                dq = dq + _dot(dsb, jnp.where(mine, kw, 0))
                dkw = dkw + _dot_tn(dsb, qm)
                dvw = dvw + _dot_tn(p.astype(BF16), dom)
            dq_ref[rows, :] = dq.astype(BF16)
            start = pl.multiple_of(i * R + 128 * sub, 128)
            dk_ref[pl.ds(start, BAND_KEYS), :] += dkw
            dv_ref[pl.ds(start, BAND_KEYS), :] += dvw

    acc_spec = pl.BlockSpec((S + 8 * CHUNK, LANES), lambda j, i: (0, j))
    return pl.pallas_call(
        body, name="attn_bwd", grid=(4, S // R),
        out_shape=(jax.ShapeDtypeStruct((S, ATT_W), BF16), jax.ShapeDtypeStruct((S + 8 * CHUNK, ATT_W), F32),
                   jax.ShapeDtypeStruct((S + 8 * CHUNK, ATT_W), F32), jax.ShapeDtypeStruct((8, 128, BAND_KEYS), F32)),
        in_specs=_att_specs() + [pl.BlockSpec((R, LANES), lambda j, i: (i, j))],
        out_specs=(pl.BlockSpec((R, LANES), lambda j, i: (i, j)), acc_spec, acc_spec,
                   pl.BlockSpec((2, 128, BAND_KEYS), lambda j, i: (j, 0, 0))),
        compiler_params=_cparams(2),
    )(qkv, qkv, qkv, qkv, qkv, bias, dy_att)


def _in_bwd(dq, dk, dv, duv, dgates, x, dx1, g1, w_in_t, tm=256):
    S = x.shape[0]
    pad_blocks = (8 * CHUNK) // tm

    def body(dq_ref, dk_ref, dv_ref, duv_ref, dg_ref, x_ref, d1_ref, g_ref, wt_ref, dx_ref, dz_ref, dg1_ref):
        i = pl.program_id(0)
        dz_ref[:, 0:ATT_W] = (dq_ref[...].astype(F32) * 0.125).astype(BF16)
        dz_ref[:, ATT_W:2 * ATT_W] = dk_ref[...].astype(BF16)
        dz_ref[:, 2 * ATT_W:3 * ATT_W] = dv_ref[...].astype(BF16)
        dz_ref[:, 3 * ATT_W:3 * ATT_W + 2 * SG_W] = duv_ref[...]
        dz_ref[:, 3 * ATT_W + 2 * SG_W:IN_COLS] = dg_ref[...]
        dh = jnp.zeros((tm, D), F32)
        for c in range(IN_COLS // 512):
            cs = slice(512 * c, 512 * (c + 1))
            dh = dh + _dot(dz_ref[:, cs], wt_ref[cs, :])
        xv = x_ref[...]
        dx, dg1 = _rms_bwd(dh, xv, _rstd(xv), g_ref[...])
        dx_ref[...] = d1_ref[...] + dx

        @pl.when(i == 0)
        def _():
            dg1_ref[...] = jnp.zeros_like(dg1_ref)

        dg1_ref[...] += dg1

    shifted = pl.BlockSpec((tm, ATT_W), lambda i: (i + pad_blocks, 0))
    return pl.pallas_call(
        body, name="in_bwd", grid=(S // tm,),
        out_shape=(jax.ShapeDtypeStruct((S, D), F32), jax.ShapeDtypeStruct((S, IN_COLS), BF16),
                   jax.ShapeDtypeStruct((1, D), F32)),
        in_specs=[_rows(tm, ATT_W), shifted, shifted, _rows(tm, 2 * SG_W), _rows(tm, 2 * D), _rows(tm, D),
                  _rows(tm, D), _resident((1, D)), _resident((IN_COLS, D))],
        out_specs=(_rows(tm, D), _rows(tm, IN_COLS), pl.BlockSpec((1, D), lambda i: (0, 0))),
        compiler_params=_cparams(1),
    )(dq, dk, dv, duv, dgates, x, dx1, g1, w_in_t)


def _adam_math(w, g, m, v):
    m = ADAM_B1 * m + (1.0 - ADAM_B1) * g
    v = ADAM_B2 * v + (1.0 - ADAM_B2) * (g * g)
    m_hat = m / (1.0 - ADAM_B1 ** ADAM_STEP)
    v_hat = v / (1.0 - ADAM_B2 ** ADAM_STEP)
    delta = -ADAM_LR * (m_hat / (jnp.sqrt(v_hat) + ADAM_EPS) + ADAM_WD * w)
    return delta, m, v


def _adam(parts, w, m, v, tr, name):
    P, R, C = parts.shape

    def body(p_ref, w_ref, m_ref, v_ref, g_ref, d_ref, mo_ref, vo_ref):
        g = p_ref[0].astype(F32)
        for k in range(1, P):
            g = g + p_ref[k].astype(F32)
        g_ref[...] = g
        d_ref[...], mo_ref[...], vo_ref[...] = _adam_math(w_ref[...], g, m_ref[...], v_ref[...])

    row = pl.BlockSpec((tr, C), lambda i: (i, 0))
    return pl.pallas_call(
        body, name=name, grid=(R // tr,),
        out_shape=tuple(jax.ShapeDtypeStruct((R, C), F32) for _ in range(4)),
        in_specs=[pl.BlockSpec((P, tr, C), lambda i: (0, i, 0)), row, row, row],
        out_specs=(row, row, row, row),
        compiler_params=_cparams(1),
    )(parts, w, m, v)


def _my_place():
    return lax.axis_index("x"), lax.axis_index("y"), lax.axis_index("c")


def _other_chips(x, y):
    return [(1 - x, y), (x, 1 - y), (1 - x, 1 - y)]


def _all_gather(shards, name):
    n = len(shards)

    def body(*refs):
        ins, outs = refs[:n], refs[n:2 * n]
        send_sems, recv_sems, local_sems = refs[2 * n:]
        x, y, c = _my_place()
        me, sibling = (x, y, c), (x, y, 1 - c)
        chips = _other_chips(x, y)

        def idx(p):
            return 4 * p[0] + 2 * p[1] + p[2]

        def copy(a, k, block, to, src=None):
            return pltpu.make_async_remote_copy(
                src_ref=outs[a].at[idx(block)] if src is None else src, dst_ref=outs[a].at[idx(block)],
                send_sem=send_sems.at[a, k], recv_sem=recv_sems.at[a, k], device_id=to, device_id_type=MESH)

        mine = [pltpu.make_async_copy(ins[a], outs[a].at[idx(me)], local_sems.at[a]) for a in range(n)]
        for cp in mine:
            cp.start()
        first = []
        for a in range(n):
            first.append(copy(a, 0, me, sibling, src=ins[a]))
            first += [copy(a, 1 + j, me, (*chip, c), src=ins[a]) for j, chip in enumerate(chips)]
        for cp in first:
            cp.start()
        passed = []
        for j, chip in enumerate(chips):
            for a in range(n):
                copy(a, 1 + j, (*chip, c), me).wait_recv()
                fwd = copy(a, 4 + j, (*chip, c), sibling)
                fwd.start()
                passed.append(fwd)
        for a in range(n):
            copy(a, 0, sibling, me).wait_recv()
            for j, chip in enumerate(chips):
                copy(a, 4 + j, (*chip, 1 - c), me).wait_recv()
        for cp in first + passed:
            cp.wait_send()
        for cp in mine:
            cp.wait()

    hbm = pl.BlockSpec(memory_space=pl.ANY)
    return pl.pallas_call(
        body, name=name,
        out_shape=tuple(jax.ShapeDtypeStruct((N_DEV,) + s.shape, s.dtype) for s in shards),
        in_specs=[hbm] * n, out_specs=tuple([hbm] * n),
        scratch_shapes=[pltpu.SemaphoreType.DMA((n, 7)), pltpu.SemaphoreType.DMA((n, 7)),
                        pltpu.SemaphoreType.DMA((n,))],
    )(*shards)


def _exchange_sibling(scatter, whole, name):
    ns, nw = len(scatter), len(whole)
    n = ns + nw

    def body(*refs):
        ins, outs = refs[:n], refs[n:2 * n]
        send_sems, recv_sems = refs[2 * n:]
        x, y, c = _my_place()
        copies = []
        for a in range(n):
            for k in range(4 if a < ns else 1):
                src = ins[a].at[2 * k + (1 - c)] if a < ns else ins[a]
                dst = outs[a].at[k] if a < ns else outs[a]
                cp = pltpu.make_async_remote_copy(
                    src_ref=src, dst_ref=dst, send_sem=send_sems.at[a, k], recv_sem=recv_sems.at[a, k],
                    device_id=(x, y, 1 - c), device_id_type=MESH)
                cp.start()
                copies.append(cp)
        for cp in copies:
            cp.wait()

    hbm = pl.BlockSpec(memory_space=pl.ANY)
    shapes = [jax.ShapeDtypeStruct((4,) + p.shape[1:], p.dtype) for p in scatter]
    shapes += [jax.ShapeDtypeStruct(p.shape, p.dtype) for p in whole]
    outs = pl.pallas_call(
        body, name=name, out_shape=tuple(shapes),
        in_specs=[hbm] * n, out_specs=tuple([hbm] * n),
        scratch_shapes=[pltpu.SemaphoreType.DMA((n, 4)), pltpu.SemaphoreType.DMA((n, 4))],
    )(*scatter, *whole)
    return outs[:ns], outs[ns:]


def _pair_sum(mine, theirs, c_idx, name):
    _, R, C = mine.shape

    def body(c_ref, a_ref, b_ref, o_ref):
        o_ref[...] = (a_ref[...].astype(F32) + b_ref[...].astype(F32)).astype(o_ref.dtype)

    grid_spec = pltpu.PrefetchScalarGridSpec(
        num_scalar_prefetch=1, grid=(4,),
        in_specs=[pl.BlockSpec((1, R, C), lambda k, c_ref: (2 * k + c_ref[0], 0, 0)),
                  pl.BlockSpec((1, R, C), lambda k, c_ref: (k, 0, 0))],
        out_specs=pl.BlockSpec((1, R, C), lambda k, c_ref: (k, 0, 0)))
    return pl.pallas_call(
        body, name=name, grid_spec=grid_spec,
        out_shape=jax.ShapeDtypeStruct((4, R, C), mine.dtype),
        compiler_params=_cparams(1),
    )(c_idx, mine, theirs)


def _pair_sum_whole(mine, theirs, name):
    n = len(mine)

    def body(*refs):
        for a in range(n):
            refs[2 * n + a][...] = refs[a][...] + refs[n + a][...]

    vm = pl.BlockSpec(memory_space=pltpu.VMEM)
    return pl.pallas_call(
        body, name=name, out_shape=tuple(jax.ShapeDtypeStruct(p.shape, p.dtype) for p in mine),
        in_specs=[vm] * (2 * n), out_specs=tuple([vm] * n),
    )(*mine, *theirs)


def _exchange_chips(scatter, whole, name):
    ns, nw = len(scatter), len(whole)
    n = ns + nw

    def body(*refs):
        ins, outs = refs[:n], refs[n:2 * n]
        send_sems, recv_sems, local_sems = refs[2 * n:]
        x, y, c = _my_place()
        my_chip = 2 * x + y

        def src(a, k):
            return ins[a].at[k] if a < ns else ins[a]

        local = [pltpu.make_async_copy(src(a, my_chip), outs[a].at[my_chip], local_sems.at[a]) for a in range(n)]
        for cp in local:
            cp.start()
        copies = []
        for a in range(n):
            for j, (px, py) in enumerate(_other_chips(x, y)):
                cp = pltpu.make_async_remote_copy(
                    src_ref=src(a, 2 * px + py), dst_ref=outs[a].at[my_chip],
                    send_sem=send_sems.at[a, j], recv_sem=recv_sems.at[a, j],
                    device_id=(px, py, c), device_id_type=MESH)
                cp.start()
                copies.append(cp)
        for cp in copies:
            cp.wait()
        for cp in local:
            cp.wait()

    hbm = pl.BlockSpec(memory_space=pl.ANY)
    shapes = [jax.ShapeDtypeStruct(s.shape, s.dtype) for s in scatter]
    shapes += [jax.ShapeDtypeStruct((4,) + s.shape, s.dtype) for s in whole]
    outs = pl.pallas_call(
        body, name=name, out_shape=tuple(shapes),
        in_specs=[hbm] * n, out_specs=tuple([hbm] * n),
        scratch_shapes=[pltpu.SemaphoreType.DMA((n, 3)), pltpu.SemaphoreType.DMA((n, 3)),
                        pltpu.SemaphoreType.DMA((n,))],
    )(*scatter, *whole)
    return outs[:ns], outs[ns:]


def _sum_parts(parts, name):
    _, R, C = parts.shape

    def body(p_ref, o_ref):
        o_ref[...] = ((p_ref[0].astype(F32) + p_ref[1].astype(F32)) + p_ref[2].astype(F32)) + p_ref[3].astype(F32)

    vm = pl.BlockSpec(memory_space=pltpu.VMEM)
    return pl.pallas_call(body, name=name, out_shape=jax.ShapeDtypeStruct((R, C), F32),
                          in_specs=[vm], out_specs=vm,
                          compiler_params=pltpu.CompilerParams(vmem_limit_bytes=VMEM_LIMIT))(parts)


def _small_update(parts, w, m, v, loss_parts, name):
    n = len(parts)

    def body(*refs):
        p_refs, w_refs, m_refs, v_refs = (refs[i * n:(i + 1) * n] for i in range(4))
        lp_ref = refs[4 * n]
        outs = refs[4 * n + 1:]
        g_refs, d_refs, mo_refs, vo_refs = (outs[i * n:(i + 1) * n] for i in range(4))
        for a in range(n):
            g = ((p_refs[a][0] + p_refs[a][1]) + p_refs[a][2]) + p_refs[a][3]
            g_refs[a][...] = g
            d_refs[a][...], mo_refs[a][...], vo_refs[a][...] = _adam_math(w_refs[a][...], g, m_refs[a][...],
                                                                          v_refs[a][...])
        outs[4 * n][...] = ((lp_ref[0] + lp_ref[1]) + lp_ref[2]) + lp_ref[3]

    vm = pl.BlockSpec(memory_space=pltpu.VMEM)
    shapes = [jax.ShapeDtypeStruct(t.shape, F32) for _ in range(4) for t in w]
    shapes.append(jax.ShapeDtypeStruct(loss_parts.shape[1:], F32))
    outs = pl.pallas_call(
        body, name=name, out_shape=tuple(shapes),
        in_specs=[vm] * (4 * n + 1), out_specs=tuple([vm] * (4 * n + 1)),
    )(*parts, *w, *m, *v, loss_parts)
    return outs[0:n], outs[n:2 * n], outs[2 * n:3 * n], outs[3 * n:4 * n], outs[4 * n]


BIG = [("w_in", 1), ("w_branch_att", 1), ("w_branch_sg", 1), ("w_out", 0), ("w_xq", 0), ("w_xkv", 1), ("w_xo", 0),
       ("w_ffn_in", 1), ("w_ffn_out", 0)]
SMALL = [("norm_mix_g", (1, D)), ("rel_bias", (8, NREL)), ("sg_ln_g", (8, 64)), ("sg_ln_b", (8, 64)),
         ("sg_w", (8, 128, 128)), ("sg_b", (8, 128)), ("norm_xattn_g", (1, D)), ("norm_mem_g", (1, D)),
         ("norm_ffn_g", (1, D)), ("norm_final_g", (1, D))]
ADAM_ROWS = {"w_in": 256, "w_branch_att": 512, "w_branch_sg": 512, "w_xkv": 1024, "w_ffn_in": 256}


def _local_step(x, mem, tgt, w, small):
    g1, g2, g3 = small["norm_mix_g"], small["norm_xattn_g"], small["norm_ffn_g"]
    g_mem, g4 = small["norm_mem_g"], small["norm_final_g"]
    lng = small["sg_ln_g"].reshape(1, SG_W)
    lnb = small["sg_ln_b"].reshape(1, SG_W)
    sg_w = small["sg_w"]
    b_exp = jnp.broadcast_to(small["sg_b"].T[:, :, None], (128, 8, 64)).reshape(128, SG_W)
    rel_pad = jnp.pad(small["rel_bias"], ((0, 0), (0, 384 - NREL)))
    w_in_t, wba_t, wbs_t, w_xkv_t, w_ffn_in_t = (w[n] for n in ("w_in", "w_branch_att", "w_branch_sg", "w_xkv",
                                                                "w_ffn_in"))

    bias = _bias_table(rel_pad)
    qkv, uv, gates, h = _in_proj(x, g1, w_in_t)
    y_att = _attn_fwd(qkv, bias)
    y_sg = _sgu_fwd(uv, lng, lnb, sg_w, b_exp)
    x1 = _merge_fwd(x, y_att, y_sg, gates, wba_t, wbs_t, w["w_out"])
    kv, mn = _mem_kv(mem, g_mem, w_xkv_t)
    x2 = _xattn_fwd(x1, g2, w["w_xq"], kv, w["w_xo"])
    dx3, gu, hf, act, loss_acc, dg4 = _ffn_fwd(x2, tgt, g3, w_ffn_in_t, w["w_ffn_out"], g4)

    gw = {}
    dx2, dgu, dg3 = _ffn_bwd(dx3, gu, x2, g3, w["w_ffn_out"], w_ffn_in_t)
    gw["w_ffn_out"] = _dw(act, dx3, 1408, 1024, 512, "dw_ffn_out")
    gw["w_ffn_in"] = _dw(dgu, hf, 1408, 1024, 512, "dw_ffn_in")
    dx1, o_x, dq_x, hx, dkv, dg2 = _xattn_bwd(dx2, x1, g2, w["w_xq"], w["w_xo"], kv)
    gw["w_xo"] = _dw(o_x, dx2, 1024, 1024, 512, "dw_xo")
    gw["w_xq"] = _dw(hx, dq_x, 1024, 1024, 512, "dw_xq")
    gw["w_xkv"], dg_mem = _mem_kv_bwd(dkv, mem, g_mem, mn, w_xkv_t)
    merged, d_a, d_b, dy_att, dy_sg, dgates = _merge_bwd(dx1, y_att, y_sg, gates, wba_t, wbs_t, w["w_out"])
    gw["w_out"] = _dw(merged, dx1, 1024, 1024, 512, "dw_out")
    gw["w_branch_att"] = _dw(d_a, y_att, 1024, 512, 512, "dw_branch_att")
    gw["w_branch_sg"] = _dw(d_b, y_sg, 1024, 512, 512, "dw_branch_sg")
    duv, d_sgw, d_bx, d_lng, d_lnb = _sgu_bwd(uv, dy_sg, lng, lnb, sg_w, b_exp)
    dq, dk, dv, ds_sum = _attn_bwd(qkv, dy_att, bias)
    d_rel = _bias_grad(ds_sum)
    grad_x, dz, dg1 = _in_bwd(dq, dk, dv, duv, dgates, x, dx1, g1, w_in_t)
    gw["w_in"] = _dw(dz, h, 1152, 1024, 512, "dw_in")

    gs = {"norm_mix_g": dg1, "rel_bias": d_rel[:, :NREL], "sg_ln_g": d_lng.reshape(8, 64),
          "sg_ln_b": d_lnb.reshape(8, 64), "sg_w": d_sgw, "sg_b": d_bx.reshape(128, 8, 64)[:, :, 0].T,
          "norm_xattn_g": dg2, "norm_mem_g": dg_mem, "norm_ffn_g": dg3, "norm_final_g": dg4}
    return loss_acc, grad_x, gw, gs


def kernel(x, mem, norm_mix_g, w_in, rel_bias, sg_ln_g, sg_ln_b, sg_w, sg_b, w_branch_att, w_branch_sg, w_out, norm_xattn_g, norm_mem_g, w_xq, w_xkv, w_xo, norm_ffn_g, w_ffn_in, w_ffn_out, norm_final_g, loss_target, m_norm_mix_g, m_w_in, m_rel_bias, m_sg_ln_g, m_sg_ln_b, m_sg_w, m_sg_b, m_w_branch_att, m_w_branch_sg, m_w_out, m_norm_xattn_g, m_norm_mem_g, m_w_xq, m_w_xkv, m_w_xo, m_norm_ffn_g, m_w_ffn_in, m_w_ffn_out, m_norm_final_g, v_norm_mix_g, v_w_in, v_rel_bias, v_sg_ln_g, v_sg_ln_b, v_sg_w, v_sg_b, v_w_branch_att, v_w_branch_sg, v_w_out, v_norm_xattn_g, v_norm_mem_g, v_w_xq, v_w_xkv, v_w_xo, v_norm_ffn_g, v_w_ffn_in, v_w_ffn_out, v_norm_final_g):
    args = dict(locals())
    big_names = [n for n, _ in BIG]
    small_names = [n for n, _ in SMALL]
    S = x.shape[1]

    shards = [(args[n][0].T if axis == 1 else args[n][0]).astype(BF16) for n, axis in BIG]
    gathered = _all_gather(shards, "ag_weights")
    w_full = {n: g.reshape(N_DEV * g.shape[1], g.shape[2]) for n, g in zip(big_names, gathered)}
    small = {n: args[n].reshape(shape) for n, shape in SMALL}

    loss_part, grad_x, gw, gs = _local_step(x.reshape(S, D), mem.reshape(MEM, D), loss_target.reshape(S, D), w_full,
                                            small)

    scatter = [gw[n].reshape((N_DEV,) + s.shape) for n, s in zip(big_names, shards)]
    whole = [gs[n] for n in small_names] + [loss_part]
    c_idx = lax.axis_index("c").astype(jnp.int32).reshape(1)
    sib_scatter, sib_whole = _exchange_sibling(scatter, whole, "rs_sibling")
    chip_scatter = [_pair_sum(a, b, c_idx, "rs_pair_" + n) for a, b, n in zip(scatter, sib_scatter, big_names)]
    chip_whole = _pair_sum_whole(whole, sib_whole, "rs_pair_small")
    all_scatter, all_whole = _exchange_chips(chip_scatter, chip_whole, "rs_chips")

    res = {}
    for (n, axis), parts in zip(BIG, all_scatter):
        wmv = [args[p + n][0] for p in ("", "m_", "v_")]
        if axis == 1:
            parts = _sum_parts(parts, "rs_sum_" + n).T[None]
        tr = ADAM_ROWS[n] if axis == 1 else wmv[0].shape[0]
        res[n] = [t[None] for t in _adam(parts, *wmv, tr, "adam_" + n)]
    small_res = _small_update(
        all_whole[:-1], [small[n] for n in small_names], [args["m_" + n].reshape(s) for n, s in SMALL],
        [args["v_" + n].reshape(s) for n, s in SMALL], all_whole[-1], "adam_small")
    for i, n in enumerate(small_names):
        res[n] = [small_res[k][i].reshape(args[n].shape) for k in range(4)]
    loss = small_res[4][0, 0]

    order = ["norm_mix_g", "w_in", "rel_bias", "sg_ln_g", "sg_ln_b", "sg_w", "sg_b", "w_branch_att", "w_branch_sg",
             "w_out", "norm_xattn_g", "norm_mem_g", "w_xq", "w_xkv", "w_xo", "norm_ffn_g", "w_ffn_in", "w_ffn_out",
             "norm_final_g"]
    outs = [loss, grad_x.reshape(1, S, D)]
    for k in range(4):
        outs += [res[n][k] for n in order]
    return tuple(outs)
```

```python
import math

import jax
import jax.numpy as jnp
from jax import lax
from jax.experimental import pallas as pl
from jax.experimental.pallas import tpu as pltpu

F32 = jnp.float32
BF16 = jnp.bfloat16

D = 1024
ATT_W = 512
SG_W = 512
IN_COLS = 4608
DFF = 2816
MEM = 256
XH = 4
CHUNK = 64
BAND_KEYS = 640
REL_CLIP = 128
NREL = 2 * REL_CLIP + 1
EPS = 1e-6
NEG = -1e30
N_DEV = 8

ADAM_LR = 0.001
ADAM_B1 = 0.9
ADAM_B2 = 0.999
ADAM_EPS = 1e-08
ADAM_WD = 0.01
ADAM_STEP = 10

LANES = 128
VMEM_LIMIT = 56 * 1024 * 1024
MESH = pl.DeviceIdType.MESH


def _cparams(n_axes):
    return pltpu.CompilerParams(dimension_semantics=("arbitrary",) * n_axes, vmem_limit_bytes=VMEM_LIMIT)


def _resident(shape):
    zeros = (0,) * len(shape)
    return pl.BlockSpec(shape, lambda *_: zeros, pipeline_mode=pl.Buffered(1))


def _rows(tm, cols, col_block=0):
    return pl.BlockSpec((tm, cols), lambda i: (i, col_block))


def _sigmoid(x):
    return 1.0 / (1.0 + jnp.exp(-x))


_GELU_C = math.sqrt(2.0 / math.pi)


def _gelu(x):
    t = jnp.tanh(_GELU_C * (x + 0.044715 * (x * x * x)))
    return x * (0.5 * (1.0 + t))


def _gelu_and_grad(x):
    x2 = x * x
    t = jnp.tanh(_GELU_C * (x + 0.044715 * (x2 * x)))
    cdf = 0.5 * (1.0 + t)
    dcdf = 0.5 * (1.0 - t * t) * (_GELU_C * (1.0 + 3.0 * 0.044715 * x2))
    return x * cdf, cdf + x * dcdf


def _rstd(x):
    return lax.rsqrt(jnp.mean(x * x, axis=-1, keepdims=True) + EPS)


def _rms_bwd(dh, x, r, g):
    xh = x * r
    dxh = dh * g
    dx = r * (dxh - xh * jnp.mean(dxh * xh, axis=-1, keepdims=True))
    dg = jnp.sum(dh * xh, axis=0, keepdims=True)
    return dx, dg


def _group_sum64(x):
    def one(v):
        lane = lax.broadcasted_iota(jnp.int32, v.shape, 1)
        for k in (1, 2, 4, 8, 16, 32):
            up = pltpu.roll(v, LANES - k, 1)
            down = pltpu.roll(v, k, 1)
            v = v + jnp.where((lane & k) == 0, up, down)
        return v

    pieces = [one(x[:, LANES * j:LANES * (j + 1)]) for j in range(x.shape[1] // LANES)]
    return pieces[0] if len(pieces) == 1 else jnp.concatenate(pieces, axis=1)


def _dot(a, b):
    return jnp.dot(a, b, preferred_element_type=F32)


def _dot_nt(a, b):
    return lax.dot_general(a, b, (((1,), (1,)), ((), ())), preferred_element_type=F32)


def _dot_tn(a, b):
    return lax.dot_general(a, b, (((0,), (0,)), ((), ())), preferred_element_type=F32)


DIAGS = 768


def _diag_onehot():
    r_idx = lax.broadcasted_iota(jnp.int32, (384, DIAGS), 0)
    t_idx = lax.broadcasted_iota(jnp.int32, (384, DIAGS), 1)
    dist = (8 * CHUNK + 127) - t_idx
    return (jnp.clip(dist, -REL_CLIP, REL_CLIP) + REL_CLIP == r_idx).astype(F32)


def _shift_rows(x, reverse):
    row = lax.broadcasted_iota(jnp.int32, x.shape, 0)
    for k in range(7):
        amt = (DIAGS - (1 << k)) if reverse else (1 << k)
        x = jnp.where(((row >> k) & 1) == 1, pltpu.roll(x, amt, 1), x)
    return x


def _bias_table(rel_bias_pad):
    def body(rb_ref, out_ref):
        per_diag = jnp.dot(rb_ref[...], _diag_onehot(), preferred_element_type=F32,
                           precision=lax.Precision.HIGHEST)
        a = lax.broadcasted_iota(jnp.int32, (128, BAND_KEYS), 0)
        b = lax.broadcasted_iota(jnp.int32, (128, BAND_KEYS), 1)
        ok = (b // CHUNK >= a // CHUNK) & (b // CHUNK <= a // CHUNK + 8)
        for h in range(8):
            rows = jnp.broadcast_to(per_diag[h:h + 1, :], (128, DIAGS))
            table = _shift_rows(pltpu.roll(rows, DIAGS - 127, 1), reverse=False)
            out_ref[h] = jnp.where(ok, table[:, :BAND_KEYS], NEG)

    return pl.pallas_call(
        body, name="bias_table",
        out_shape=jax.ShapeDtypeStruct((8, 128, BAND_KEYS), F32),
        in_specs=[pl.BlockSpec(memory_space=pltpu.VMEM)],
        out_specs=pl.BlockSpec(memory_space=pltpu.VMEM),
    )(rel_bias_pad)


def _bias_grad(ds_sum):
    def body(ds_ref, out_ref):
        sums = []
        for h in range(8):
            padded = jnp.concatenate([ds_ref[h], jnp.zeros((128, DIAGS - BAND_KEYS), F32)], axis=1)
            skewed = pltpu.roll(_shift_rows(padded, reverse=True), 127, 1)
            sums.append(jnp.sum(skewed, axis=0, keepdims=True))
        per_diag = jnp.concatenate(sums, axis=0)
        out_ref[...] = lax.dot_general(per_diag, _diag_onehot(), (((1,), (1,)), ((), ())),
                                       preferred_element_type=F32, precision=lax.Precision.HIGHEST)

    return pl.pallas_call(
        body, name="bias_grad",
        out_shape=jax.ShapeDtypeStruct((8, 384), F32),
        in_specs=[pl.BlockSpec(memory_space=pltpu.VMEM)],
        out_specs=pl.BlockSpec(memory_space=pltpu.VMEM),
    )(ds_sum)


def _in_proj(x, g1, w_in_t, plans=(), tm=512):
    S = x.shape[0]

    def body(x_ref, g_ref, w_ref, qkv_ref, uv_ref, gate_ref, h_ref):
        xv = x_ref[...]
        h = (xv * _rstd(xv) * g_ref[...]).astype(BF16)
        h_ref[...] = h
        for c in range(IN_COLS // 512):
            zc = _dot_nt(h, w_ref[512 * c:512 * (c + 1), :])
            if c == 0:
                qkv_ref[:, 0:512] = (zc * 0.125).astype(BF16)
            elif c < 3:
                qkv_ref[:, 512 * c:512 * (c + 1)] = zc.astype(BF16)
            elif c < 5:
                uv_ref[:, 512 * (c - 3):512 * (c - 2)] = zc.astype(BF16)
            else:
                gate_ref[:, 512 * (c - 5):512 * (c - 4)] = zc.astype(BF16)

    return _call(
        body, name="in_proj", grid=(S // tm,),
        out_shape=(jax.ShapeDtypeStruct((S, 3 * ATT_W), BF16), jax.ShapeDtypeStruct((S, 2 * SG_W), BF16),
                   jax.ShapeDtypeStruct((S, 2 * D), BF16), jax.ShapeDtypeStruct((S, D), BF16)),
        in_specs=[_rows(tm, D), _resident((1, D)), _resident((IN_COLS, D))],
        out_specs=(_rows(tm, 3 * ATT_W), _rows(tm, 2 * SG_W), _rows(tm, 2 * D), _rows(tm, D)),
        operands=(x, g1, w_in_t), plans=plans)


ATT_R = 512


def _att_specs():
    R = ATT_R
    q = pl.BlockSpec((R, LANES), lambda j, i: (i, j))
    kp = pl.BlockSpec((R, LANES), lambda j, i: (jnp.maximum(i - 1, 0), 4 + j))
    kc = pl.BlockSpec((R, LANES), lambda j, i: (i, 4 + j))
    vp = pl.BlockSpec((R, LANES), lambda j, i: (jnp.maximum(i - 1, 0), 8 + j))
    vc = pl.BlockSpec((R, LANES), lambda j, i: (i, 8 + j))
    bias = pl.BlockSpec((2, 128, BAND_KEYS), lambda j, i: (j, 0, 0))
    return [q, kp, kc, vp, vc, bias]


def _att_probs(qm, kw, bias_h, valid):
    s = _dot_nt(qm, kw) + bias_h
    s = jnp.where(valid, s, NEG)
    p = jnp.exp(s - jnp.max(s, axis=-1, keepdims=True))
    return p / jnp.sum(p, axis=-1, keepdims=True)


def _attn_fwd(qkv, bias, plans=()):
    S = qkv.shape[0]
    R = ATT_R

    def body(q_ref, kp_ref, kc_ref, vp_ref, vc_ref, b_ref, o_ref):
        i = pl.program_id(1)
        lane = lax.broadcasted_iota(jnp.int32, (1, LANES), 1)
        col = lax.broadcasted_iota(jnp.int32, (1, BAND_KEYS), 1)
        kwin = jnp.concatenate([kp_ref[...], kc_ref[...]], axis=0)
        vwin = jnp.concatenate([vp_ref[...], vc_ref[...]], axis=0)
        for sub in range(R // 128):
            q2 = q_ref[128 * sub:128 * (sub + 1), :]
            kw = kwin[128 * sub:128 * sub + BAND_KEYS]
            vw = vwin[128 * sub:128 * sub + BAND_KEYS]
            valid = (i * R + (128 * sub - 8 * CHUNK) + col) >= 0
            o = jnp.zeros((128, LANES), F32)
            for hh in range(2):
                mine = (lane >= 64 * hh) & (lane < 64 * (hh + 1))
                p = _att_probs(jnp.where(mine, q2, 0), kw, b_ref[hh], valid)
                o = o + _dot(p.astype(BF16), jnp.where(mine, vw, 0))
            o_ref[128 * sub:128 * (sub + 1), :] = o.astype(BF16)

    return _call(
        body, name="attn_fwd", grid=(4, S // R),
        out_shape=(jax.ShapeDtypeStruct((S, ATT_W), BF16),),
        in_specs=_att_specs(),
        out_specs=(pl.BlockSpec((R, LANES), lambda j, i: (i, j)),),
        operands=(qkv, qkv, qkv, qkv, qkv, bias), plans=plans)


def _sg_mask():
    t = lax.broadcasted_iota(jnp.int32, (128, 128), 0)
    s = lax.broadcasted_iota(jnp.int32, (128, 128), 1)
    return (s // CHUNK) <= (t // CHUNK)


def _sg_layernorm(gv, lng, lnb):
    mu = _group_sum64(gv) * (1.0 / 64)
    xc = gv - mu
    var = _group_sum64(xc * xc) * (1.0 / 64)
    rstd = lax.rsqrt(var + EPS)
    vhat = xc * rstd
    return vhat * lng + lnb, vhat, rstd


def _sgu_fwd(uv, lng, lnb, sg_w, b_exp, tm=512):
    S = uv.shape[0]

    def body(uv_ref, lng_ref, lnb_ref, w_ref, b_ref, y_ref):
        lane = lax.broadcasted_iota(jnp.int32, (1, LANES), 1)
        lo = lane < 64
        mask = _sg_mask()
        gu = _gelu(uv_ref[:, 0:SG_W].astype(F32))
        vln, _, _ = _sg_layernorm(_gelu(uv_ref[:, SG_W:2 * SG_W].astype(F32)), lng_ref[...], lnb_ref[...])
        for gp in range(4):
            w0 = jnp.where(mask, w_ref[2 * gp], 0).astype(BF16)
            w1 = jnp.where(mask, w_ref[2 * gp + 1], 0).astype(BF16)
            cols = slice(128 * gp, 128 * (gp + 1))
            for n in range(tm // 128):
                rows = slice(128 * n, 128 * (n + 1))
                vl = vln[rows, cols]
                sv = (_dot(w0, jnp.where(lo, vl, 0).astype(BF16)) + _dot(w1, jnp.where(lo, 0, vl).astype(BF16))
                      + b_ref[:, cols])
                y_ref[rows, cols] = (gu[rows, cols] * sv).astype(BF16)

    return pl.pallas_call(
        body, name="sgu_fwd", grid=(S // tm,),
        out_shape=jax.ShapeDtypeStruct((S, SG_W), BF16),
        in_specs=[_rows(tm, 2 * SG_W), _resident((1, SG_W)), _resident((1, SG_W)),
                  _resident((8, 128, 128)), _resident((128, SG_W))],
        out_specs=_rows(tm, SG_W),
        compiler_params=_cparams(1),
    )(uv, lng, lnb, sg_w, b_exp)


def _merge_fwd(x, y_att, y_sg, gates, wba_t, wbs_t, w_out, tm=512):
    S = x.shape[0]

    def body(x_ref, ya_ref, ys_ref, g_ref, wba_ref, wbs_ref, wo_ref, x1_ref):
        a = _dot_nt(ya_ref[...], wba_ref[...])
        b = _dot_nt(ys_ref[...], wbs_ref[...])
        merged = _sigmoid(g_ref[:, 0:D].astype(F32)) * a + _sigmoid(g_ref[:, D:2 * D].astype(F32)) * b
        x1_ref[...] = x_ref[...] + _dot(merged.astype(BF16), wo_ref[...])

    return pl.pallas_call(
        body, name="merge_fwd", grid=(S // tm,),
        out_shape=jax.ShapeDtypeStruct((S, D), F32),
        in_specs=[_rows(tm, D), _rows(tm, ATT_W), _rows(tm, SG_W), _rows(tm, 2 * D),
                  _resident((D, ATT_W)), _resident((D, SG_W)), _resident((D, D))],
        out_specs=_rows(tm, D),
        compiler_params=_cparams(1),
    )(x, y_att, y_sg, gates, wba_t, wbs_t, w_out)


def _mem_kv(mem, g_mem, w_xkv_t):
    def body(m_ref, g_ref, w_ref, kv_ref, mn_ref):
        mv = m_ref[...]
        mn = (mv * _rstd(mv) * g_ref[...]).astype(BF16)
        mn_ref[...] = mn
        kv_ref[...] = _dot_nt(mn, w_ref[...]).astype(BF16)

    vm = pl.BlockSpec(memory_space=pltpu.VMEM)
    return pl.pallas_call(
        body, name="mem_kv",
        out_shape=(jax.ShapeDtypeStruct((MEM, 2 * D), BF16), jax.ShapeDtypeStruct((MEM, D), BF16)),
        in_specs=[vm, vm, vm], out_specs=(vm, vm),
        compiler_params=pltpu.CompilerParams(vmem_limit_bytes=VMEM_LIMIT),
    )(mem, g_mem, w_xkv_t)


def _xatt_head(qx, kv_ref, h):
    hs = slice(256 * h, 256 * (h + 1))
    s = _dot_nt(qx[:, hs], kv_ref[:, hs])
    p = jnp.exp(s - jnp.max(s, axis=-1, keepdims=True))
    return p / jnp.sum(p, axis=-1, keepdims=True)


def _xattn_fwd(x1, g2, w_xq, kv, w_xo, tm=512):
    S = x1.shape[0]

    def body(x_ref, g_ref, wq_ref, kv_ref, wo_ref, x2_ref):
        xv = x_ref[...]
        hx = (xv * _rstd(xv) * g_ref[...]).astype(BF16)
        qx = (_dot(hx, wq_ref[...]) * (1.0 / 16)).astype(BF16)
        outs = []
        for h in range(XH):
            p = _xatt_head(qx, kv_ref, h)
            outs.append(_dot(p.astype(BF16), kv_ref[:, D + 256 * h:D + 256 * (h + 1)]).astype(BF16))
        o = jnp.concatenate(outs, axis=1)
        x2_ref[...] = xv + _dot(o, wo_ref[...])

    return pl.pallas_call(
        body, name="xattn_fwd", grid=(S // tm,),
        out_shape=jax.ShapeDtypeStruct((S, D), F32),
        in_specs=[_rows(tm, D), _resident((1, D)), _resident((D, D)), _resident((MEM, 2 * D)), _resident((D, D))],
        out_specs=_rows(tm, D),
        compiler_params=_cparams(1),
    )(x1, g2, w_xq, kv, w_xo)


FF_CHUNK = 1408


def _ffn_fwd(x2, tgt, g3, w_ffn_in_t, w_ffn_out, g4, tm=256):
    S = x2.shape[0]

    def body(x_ref, t_ref, g3_ref, wi_ref, wo_ref, g4_ref, dx3_ref, gu_ref, hf_ref, act_ref, loss_ref, dg4_ref):
        i = pl.program_id(0)
        xv = x_ref[...]
        hf = (xv * _rstd(xv) * g3_ref[...]).astype(BF16)
        hf_ref[...] = hf
        acc = xv
        for c in range(DFF // FF_CHUNK):
            cs = slice(FF_CHUNK * c, FF_CHUNK * (c + 1))
            us = slice(DFF + FF_CHUNK * c, DFF + FF_CHUNK * (c + 1))
            gate = _dot_nt(hf, wi_ref[cs, :])
            up = _dot_nt(hf, wi_ref[us, :])
            gu_ref[:, cs] = gate.astype(BF16)
            gu_ref[:, us] = up.astype(BF16)
            act = ((gate * _sigmoid(gate)) * up).astype(BF16)
            act_ref[:, cs] = act
            acc = acc + _dot(act, wo_ref[cs, :])
        r4 = _rstd(acc)
        g4 = g4_ref[...]
        diff = acc * r4 * g4 - t_ref[...]
        dy = diff * (1.0 / D)
        dx3, dg4 = _rms_bwd(dy, acc, r4, g4)
        dx3_ref[...] = dx3
        part = 0.5 * jnp.sum(jnp.mean(diff * diff, axis=-1, keepdims=True))

        @pl.when(i == 0)
        def _():
            loss_ref[...] = jnp.zeros_like(loss_ref)
            dg4_ref[...] = jnp.zeros_like(dg4_ref)

        loss_ref[...] += jnp.full(loss_ref.shape, part, F32)
        dg4_ref[...] += dg4

    return pl.pallas_call(
        body, name="ffn_fwd", grid=(S // tm,),
        out_shape=(jax.ShapeDtypeStruct((S, D), F32), jax.ShapeDtypeStruct((S, 2 * DFF), BF16),
                   jax.ShapeDtypeStruct((S, D), BF16), jax.ShapeDtypeStruct((S, DFF), BF16),
                   jax.ShapeDtypeStruct((8, LANES), F32), jax.ShapeDtypeStruct((1, D), F32)),
        in_specs=[_rows(tm, D), _rows(tm, D), _resident((1, D)), _resident((2 * DFF, D)), _resident((DFF, D)),
                  _resident((1, D))],
        out_specs=(_rows(tm, D), _rows(tm, 2 * DFF), _rows(tm, D), _rows(tm, DFF),
                   pl.BlockSpec((8, LANES), lambda i: (0, 0)), pl.BlockSpec((1, D), lambda i: (0, 0))),
        compiler_params=_cparams(1),
    )(x2, tgt, g3, w_ffn_in_t, w_ffn_out, g4)


def _ffn_bwd(dx3, gu, x2, g3, w_ffn_out, w_ffn_in_t, tm=256):
    S = x2.shape[0]

    def body(d_ref, gu_ref, x_ref, g3_ref, wo_ref, wit_ref, dx2_ref, dgu_ref, dg3_ref):
        i = pl.program_id(0)
        d3 = d_ref[...]
        d3b = d3.astype(BF16)
        dhf = jnp.zeros((tm, D), F32)
        for c in range(DFF // FF_CHUNK):
            cs = slice(FF_CHUNK * c, FF_CHUNK * (c + 1))
            us = slice(DFF + FF_CHUNK * c, DFF + FF_CHUNK * (c + 1))
            da = _dot_nt(d3b, wo_ref[cs, :])
            gate = gu_ref[:, cs].astype(F32)
            up = gu_ref[:, us].astype(F32)
            sg = _sigmoid(gate)
            dgate = (da * up * (sg * (1.0 + gate * (1.0 - sg)))).astype(BF16)
            dup = (da * (gate * sg)).astype(BF16)
            dgu_ref[:, cs] = dgate
            dgu_ref[:, us] = dup
            dhf = dhf + _dot(dgate, wit_ref[cs, :]) + _dot(dup, wit_ref[us, :])
        xv = x_ref[...]
        dx, dg3 = _rms_bwd(dhf, xv, _rstd(xv), g3_ref[...])
        dx2_ref[...] = d3 + dx

        @pl.when(i == 0)
        def _():
            dg3_ref[...] = jnp.zeros_like(dg3_ref)

        dg3_ref[...] += dg3

    return pl.pallas_call(
        body, name="ffn_bwd", grid=(S // tm,),
        out_shape=(jax.ShapeDtypeStruct((S, D), F32), jax.ShapeDtypeStruct((S, 2 * DFF), BF16),
                   jax.ShapeDtypeStruct((1, D), F32)),
        in_specs=[_rows(tm, D), _rows(tm, 2 * DFF), _rows(tm, D), _resident((1, D)),
                  _resident((DFF, D)), _resident((2 * DFF, D))],
        out_specs=(_rows(tm, D), _rows(tm, 2 * DFF), pl.BlockSpec((1, D), lambda i: (0, 0))),
        compiler_params=_cparams(1),
    )(dx3, gu, x2, g3, w_ffn_out, w_ffn_in_t)


def _dw(a, b, tmm, tn, ts, name, out_dtype=BF16):
    S, M = a.shape
    N = b.shape[1]
    nk = S // ts

    def body(a_ref, b_ref, o_ref, acc_ref):
        k = pl.program_id(2)

        @pl.when(k == 0)
        def _():
            acc_ref[...] = jnp.zeros_like(acc_ref)

        acc_ref[...] += _dot_tn(a_ref[...].astype(BF16), b_ref[...].astype(BF16))

        @pl.when(k == nk - 1)
        def _():
            o_ref[...] = acc_ref[...].astype(out_dtype)

    return pl.pallas_call(
        body, name=name, grid=(M // tmm, N // tn, nk),
        out_shape=jax.ShapeDtypeStruct((M, N), out_dtype),
        in_specs=[pl.BlockSpec((ts, tmm), lambda m, n, k: (k, m)), pl.BlockSpec((ts, tn), lambda m, n, k: (k, n))],
        out_specs=pl.BlockSpec((tmm, tn), lambda m, n, k: (m, n)),
        scratch_shapes=[pltpu.VMEM((tmm, tn), F32)],
        compiler_params=_cparams(3),
    )(a, b)


def _xattn_bwd(dx2, x1, g2, w_xq, w_xo, kv, plans=(), tm=512):
    S = x1.shape[0]

    def body(d_ref, x_ref, g_ref, wq_ref, wo_ref, kv_ref, dx1_ref, o_ref, dq_ref, hx_ref, dkv_ref, dg2_ref):
        i = pl.program_id(0)

        @pl.when(i == 0)
        def _():
            dkv_ref[...] = jnp.zeros_like(dkv_ref)
            dg2_ref[...] = jnp.zeros_like(dg2_ref)

        d2 = d_ref[...]
        xv = x_ref[...]
        r2 = _rstd(xv)
        hx = (xv * r2 * g_ref[...]).astype(BF16)
        hx_ref[...] = hx
        qx = (_dot(hx, wq_ref[...]) * (1.0 / 16)).astype(BF16)
        do = _dot_nt(d2.astype(BF16), wo_ref[...]).astype(BF16)
        for h in range(XH):
            hs = slice(256 * h, 256 * (h + 1))
            vs = slice(D + 256 * h, D + 256 * (h + 1))
            p = _xatt_head(qx, kv_ref, h)
            pb = p.astype(BF16)
            o_ref[:, hs] = _dot(pb, kv_ref[:, vs]).astype(BF16)
            dp = _dot_nt(do[:, hs], kv_ref[:, vs])
            ds = (p * (dp - jnp.sum(dp * p, axis=-1, keepdims=True))).astype(BF16)
            dq_ref[:, hs] = (_dot(ds, kv_ref[:, hs]) * (1.0 / 16)).astype(BF16)
            dkv_ref[:, hs] += _dot_tn(ds, qx[:, hs])
            dkv_ref[:, vs] += _dot_tn(pb, do[:, hs])
        dhx = _dot_nt(dq_ref[...], wq_ref[...])
        dx, dg2 = _rms_bwd(dhx, xv, r2, g_ref[...])
        dx1_ref[...] = d2 + dx
        dg2_ref[...] += dg2

    return _call(
        body, name="xattn_bwd", grid=(S // tm,),
        out_shape=(jax.ShapeDtypeStruct((S, D), F32), jax.ShapeDtypeStruct((S, D), BF16),
                   jax.ShapeDtypeStruct((S, D), BF16), jax.ShapeDtypeStruct((S, D), BF16),
                   jax.ShapeDtypeStruct((MEM, 2 * D), F32), jax.ShapeDtypeStruct((1, D), F32)),
        in_specs=[_rows(tm, D), _rows(tm, D), _resident((1, D)), _resident((D, D)), _resident((D, D)),
                  _resident((MEM, 2 * D))],
        out_specs=(_rows(tm, D), _rows(tm, D), _rows(tm, D), _rows(tm, D),
                   pl.BlockSpec((MEM, 2 * D), lambda i: (0, 0)), pl.BlockSpec((1, D), lambda i: (0, 0))),
        operands=(dx2, x1, g2, w_xq, w_xo, kv), plans=plans)


def _mem_kv_bwd(dkv, mem, g_mem, mn, w_xkv_t):
    def body(dkv_ref, m_ref, g_ref, mn_ref, wt_ref, dw_ref, dg_ref):
        dkvb = dkv_ref[...].astype(BF16)
        dw_ref[...] = _dot_tn(dkvb, mn_ref[...]).astype(BF16)
        dmn = _dot(dkvb, wt_ref[...])
        mv = m_ref[...]
        dg_ref[...] = jnp.sum(dmn * (mv * _rstd(mv)), axis=0, keepdims=True)

    vm = pl.BlockSpec(memory_space=pltpu.VMEM)
    return pl.pallas_call(
        body, name="mem_kv_bwd",
        out_shape=(jax.ShapeDtypeStruct((2 * D, D), BF16), jax.ShapeDtypeStruct((1, D), F32)),
        in_specs=[vm] * 5, out_specs=(vm, vm),
        compiler_params=pltpu.CompilerParams(vmem_limit_bytes=VMEM_LIMIT),
    )(dkv, mem, g_mem, mn, w_xkv_t)


def _merge_bwd(dx1, y_att, y_sg, gates, wba_t, wbs_t, w_out, tm=512):
    S = dx1.shape[0]

    def body(d_ref, ya_ref, ys_ref, g_ref, wbat_ref, wbst_ref, wo_ref,
             mg_ref, da_ref, db_ref, dya_ref, dys_ref, dg_ref):
        dm = _dot_nt(d_ref[...].astype(BF16), wo_ref[...])
        a = _dot_nt(ya_ref[...], wbat_ref[...])
        b = _dot_nt(ys_ref[...], wbst_ref[...])
        sa = _sigmoid(g_ref[:, 0:D].astype(F32))
        sb = _sigmoid(g_ref[:, D:2 * D].astype(F32))
        mg_ref[...] = (sa * a + sb * b).astype(BF16)
        da = (dm * sa).astype(BF16)
        db = (dm * sb).astype(BF16)
        da_ref[...] = da
        db_ref[...] = db
        dg_ref[:, 0:D] = (dm * a * sa * (1.0 - sa)).astype(BF16)
        dg_ref[:, D:2 * D] = (dm * b * sb * (1.0 - sb)).astype(BF16)
        dya_ref[...] = _dot(da, wbat_ref[...]).astype(BF16)
        dys_ref[...] = _dot(db, wbst_ref[...]).astype(BF16)

    return pl.pallas_call(
        body, name="merge_bwd", grid=(S // tm,),
        out_shape=(jax.ShapeDtypeStruct((S, D), BF16), jax.ShapeDtypeStruct((S, D), BF16),
                   jax.ShapeDtypeStruct((S, D), BF16), jax.ShapeDtypeStruct((S, ATT_W), BF16),
                   jax.ShapeDtypeStruct((S, SG_W), BF16), jax.ShapeDtypeStruct((S, 2 * D), BF16)),
        in_specs=[_rows(tm, D), _rows(tm, ATT_W), _rows(tm, SG_W), _rows(tm, 2 * D),
                  _resident((D, ATT_W)), _resident((D, SG_W)), _resident((D, D))],
        out_specs=(_rows(tm, D), _rows(tm, D), _rows(tm, D), _rows(tm, ATT_W), _rows(tm, SG_W), _rows(tm, 2 * D)),
        compiler_params=_cparams(1),
    )(dx1, y_att, y_sg, gates, wba_t, wbs_t, w_out)


def _sgu_bwd(uv, dy_sg, lng, lnb, sg_w, b_exp, plans=(), tm=512):
    S = uv.shape[0]
    n_steps = S // tm

    def body(uv_ref, dy_ref, lng_ref, lnb_ref, w_ref, b_ref, duv_ref, dw_ref, dbx_ref, dlng_ref, dlnb_ref, dvln_ref):
        i = pl.program_id(0)

        @pl.when(i == 0)
        def _():
            dw_ref[...] = jnp.zeros_like(dw_ref)
            dbx_ref[...] = jnp.zeros_like(dbx_ref)
            dlng_ref[...] = jnp.zeros_like(dlng_ref)
            dlnb_ref[...] = jnp.zeros_like(dlnb_ref)

        lane = lax.broadcasted_iota(jnp.int32, (1, LANES), 1)
        lo = lane < 64
        mask = _sg_mask()
        lng = lng_ref[...]
        gu, dgelu_u = _gelu_and_grad(uv_ref[:, 0:SG_W].astype(F32))
        gv, dgelu_v = _gelu_and_grad(uv_ref[:, SG_W:2 * SG_W].astype(F32))
        vln, vhat, rstd = _sg_layernorm(gv, lng, lnb_ref[...])
        dy = dy_ref[...].astype(F32)
        dsv_all = dy * gu
        for gp in range(4):
            wf0 = jnp.where(mask, w_ref[2 * gp], 0)
            wf1 = jnp.where(mask, w_ref[2 * gp + 1], 0)
            w0 = wf0.astype(BF16)
            w1 = wf1.astype(BF16)
            cols = slice(128 * gp, 128 * (gp + 1))
            dw0 = jnp.zeros((128, 128), F32)
            dw1 = jnp.zeros((128, 128), F32)
            dbx = jnp.zeros((128, LANES), F32)
            for n in range(tm // 128):
                rows = slice(128 * n, 128 * (n + 1))
                vl = vln[rows, cols]
                vl0 = jnp.where(lo, vl, 0).astype(BF16)
                vl1 = jnp.where(lo, 0, vl).astype(BF16)
                sv = _dot(w0, vl0) + _dot(w1, vl1) + b_ref[:, cols]
                duv_ref[rows, cols] = (dy[rows, cols] * sv * dgelu_u[rows, cols]).astype(BF16)
                dsv = dsv_all[rows, cols]
                dbx = dbx + dsv
                ds0 = jnp.where(lo, dsv, 0).astype(BF16)
                ds1 = jnp.where(lo, 0, dsv).astype(BF16)
                dw0 = dw0 + _dot_nt(ds0, vl0)
                dw1 = dw1 + _dot_nt(ds1, vl1)
                dvln_ref[rows, cols] = _dot_tn(w0, ds0) + _dot_tn(w1, ds1)
            dw_ref[2 * gp] += jnp.where(mask, dw0, 0)
            dw_ref[2 * gp + 1] += jnp.where(mask, dw1, 0)
            dbx_ref[:, cols] += dbx
        dvln = dvln_ref[...]
        dlng_ref[...] += jnp.sum(dvln * vhat, axis=0, keepdims=True)
        dlnb_ref[...] += jnp.sum(dvln, axis=0, keepdims=True)
        dvh = dvln * lng
        dgv = rstd * (dvh - _group_sum64(dvh) * (1.0 / 64) - vhat * (_group_sum64(dvh * vhat) * (1.0 / 64)))
        duv_ref[:, SG_W:2 * SG_W] = (dgv * dgelu_v).astype(BF16)

        @pl.when(i == n_steps - 1)
        def _():
            dbx_ref[...] = _group_sum64(dbx_ref[...])

    return _call(
        body, name="sgu_bwd", grid=(n_steps,),
        out_shape=(jax.ShapeDtypeStruct((S, 2 * SG_W), BF16), jax.ShapeDtypeStruct((8, 128, 128), F32),
                   jax.ShapeDtypeStruct((128, SG_W), F32), jax.ShapeDtypeStruct((1, SG_W), F32),
                   jax.ShapeDtypeStruct((1, SG_W), F32)),
        in_specs=[_rows(tm, 2 * SG_W), _rows(tm, SG_W), _resident((1, SG_W)), _resident((1, SG_W)),
                  _resident((8, 128, 128)), _resident((128, SG_W))],
        out_specs=(_rows(tm, 2 * SG_W), pl.BlockSpec((8, 128, 128), lambda i: (0, 0, 0)),
                   pl.BlockSpec((128, SG_W), lambda i: (0, 0)), pl.BlockSpec((1, SG_W), lambda i: (0, 0)),
                   pl.BlockSpec((1, SG_W), lambda i: (0, 0))),
        scratch_shapes=[pltpu.VMEM((tm, SG_W), F32)],
        operands=(uv, dy_sg, lng, lnb, sg_w, b_exp), plans=plans)


def _attn_bwd(qkv, dy_att, bias, plans=()):
    S = qkv.shape[0]
    R = ATT_R

    def body(q_ref, kp_ref, kc_ref, vp_ref, vc_ref, b_ref, dy_ref, dq_ref, dk_ref, dv_ref, dss_ref):
        i = pl.program_id(1)

        @pl.when(i == 0)
        def _():
            dk_ref[...] = jnp.zeros_like(dk_ref)
            dv_ref[...] = jnp.zeros_like(dv_ref)
            dss_ref[...] = jnp.zeros_like(dss_ref)

        lane = lax.broadcasted_iota(jnp.int32, (1, LANES), 1)
        col = lax.broadcasted_iota(jnp.int32, (1, BAND_KEYS), 1)
        kwin = jnp.concatenate([kp_ref[...], kc_ref[...]], axis=0)
        vwin = jnp.concatenate([vp_ref[...], vc_ref[...]], axis=0)
        for sub in range(R // 128):
            rows = slice(128 * sub, 128 * (sub + 1))
            q2 = q_ref[rows, :]
            do2 = dy_ref[rows, :]
            kw = kwin[128 * sub:128 * sub + BAND_KEYS]
            vw = vwin[128 * sub:128 * sub + BAND_KEYS]
            valid = (i * R + (128 * sub - 8 * CHUNK) + col) >= 0
            dq = jnp.zeros((128, LANES), F32)
            dkw = jnp.zeros((BAND_KEYS, LANES), F32)
            dvw = jnp.zeros((BAND_KEYS, LANES), F32)
            for hh in range(2):
                mine = (lane >= 64 * hh) & (lane < 64 * (hh + 1))
                qm = jnp.where(mine, q2, 0)
                dom = jnp.where(mine, do2, 0)
                p = _att_probs(qm, kw, b_ref[hh], valid)
                dp = _dot_nt(dom, vw)
                ds = p * (dp - jnp.sum(dp * p, axis=-1, keepdims=True))
                dss_ref[hh] += ds
                dsb = ds.astype(BF16)
                dq = dq + _dot(dsb, jnp.where(mine, kw, 0))
                dkw = dkw + _dot_tn(dsb, qm)
                dvw = dvw + _dot_tn(p.astype(BF16), dom)
            dq_ref[rows, :] = dq.astype(BF16)
            start = pl.multiple_of(i * R + 128 * sub, 128)
            dk_ref[pl.ds(start, BAND_KEYS), :] += dkw
            dv_ref[pl.ds(start, BAND_KEYS), :] += dvw

    acc_spec = pl.BlockSpec((S + 8 * CHUNK, LANES), lambda j, i: (0, j))
    return _call(
        body, name="attn_bwd", grid=(4, S // R),
        out_shape=(jax.ShapeDtypeStruct((S, ATT_W), BF16), jax.ShapeDtypeStruct((S + 8 * CHUNK, ATT_W), F32),
                   jax.ShapeDtypeStruct((S + 8 * CHUNK, ATT_W), F32), jax.ShapeDtypeStruct((8, 128, BAND_KEYS), F32)),
        in_specs=_att_specs() + [pl.BlockSpec((R, LANES), lambda j, i: (i, j))],
        out_specs=(pl.BlockSpec((R, LANES), lambda j, i: (i, j)), acc_spec, acc_spec,
                   pl.BlockSpec((2, 128, BAND_KEYS), lambda j, i: (j, 0, 0))),
        operands=(qkv, qkv, qkv, qkv, qkv, bias, dy_att), plans=plans)


def _in_bwd(dq, dk, dv, duv, dgates, x, dx1, g1, w_in_t, tm=256):
    S = x.shape[0]
    pad_blocks = (8 * CHUNK) // tm

    def body(dq_ref, dk_ref, dv_ref, duv_ref, dg_ref, x_ref, d1_ref, g_ref, wt_ref, dx_ref, dz_ref, dg1_ref):
        i = pl.program_id(0)
        dz_ref[:, 0:ATT_W] = (dq_ref[...].astype(F32) * 0.125).astype(BF16)
        dz_ref[:, ATT_W:2 * ATT_W] = dk_ref[...].astype(BF16)
        dz_ref[:, 2 * ATT_W:3 * ATT_W] = dv_ref[...].astype(BF16)
        dz_ref[:, 3 * ATT_W:3 * ATT_W + 2 * SG_W] = duv_ref[...]
        dz_ref[:, 3 * ATT_W + 2 * SG_W:IN_COLS] = dg_ref[...]
        dh = jnp.zeros((tm, D), F32)
        for c in range(IN_COLS // 512):
            cs = slice(512 * c, 512 * (c + 1))
            dh = dh + _dot(dz_ref[:, cs], wt_ref[cs, :])
        xv = x_ref[...]
        dx, dg1 = _rms_bwd(dh, xv, _rstd(xv), g_ref[...])
        dx_ref[...] = d1_ref[...] + dx

        @pl.when(i == 0)
        def _():
            dg1_ref[...] = jnp.zeros_like(dg1_ref)

        dg1_ref[...] += dg1

    shifted = pl.BlockSpec((tm, ATT_W), lambda i: (i + pad_blocks, 0))
    return pl.pallas_call(
        body, name="in_bwd", grid=(S // tm,),
        out_shape=(jax.ShapeDtypeStruct((S, D), F32), jax.ShapeDtypeStruct((S, IN_COLS), BF16),
                   jax.ShapeDtypeStruct((1, D), F32)),
        in_specs=[_rows(tm, ATT_W), shifted, shifted, _rows(tm, 2 * SG_W), _rows(tm, 2 * D), _rows(tm, D),
                  _rows(tm, D), _resident((1, D)), _resident((IN_COLS, D))],
        out_specs=(_rows(tm, D), _rows(tm, IN_COLS), pl.BlockSpec((1, D), lambda i: (0, 0))),
        compiler_params=_cparams(1),
    )(dq, dk, dv, duv, dgates, x, dx1, g1, w_in_t)


def _adam_math(w, g, m, v):
    m = ADAM_B1 * m + (1.0 - ADAM_B1) * g
    v = ADAM_B2 * v + (1.0 - ADAM_B2) * (g * g)
    m_hat = m / (1.0 - ADAM_B1 ** ADAM_STEP)
    v_hat = v / (1.0 - ADAM_B2 ** ADAM_STEP)
    delta = -ADAM_LR * (m_hat / (jnp.sqrt(v_hat) + ADAM_EPS) + ADAM_WD * w)
    return delta, m, v


def _adam(parts, w, m, v, tr, name):
    P, R, C = parts.shape

    def body(p_ref, w_ref, m_ref, v_ref, g_ref, d_ref, mo_ref, vo_ref):
        g = p_ref[0].astype(F32)
        for k in range(1, P):
            g = g + p_ref[k].astype(F32)
        g_ref[...] = g
        d_ref[...], mo_ref[...], vo_ref[...] = _adam_math(w_ref[...], g, m_ref[...], v_ref[...])

    row = pl.BlockSpec((tr, C), lambda i: (i, 0))
    return pl.pallas_call(
        body, name=name, grid=(R // tr,),
        out_shape=tuple(jax.ShapeDtypeStruct((R, C), F32) for _ in range(4)),
        in_specs=[pl.BlockSpec((P, tr, C), lambda i: (0, i, 0)), row, row, row],
        out_specs=(row, row, row, row),
        compiler_params=_cparams(1),
    )(parts, w, m, v)


def _my_place():
    return lax.axis_index("x"), lax.axis_index("y"), lax.axis_index("c")


def _other_chips(x, y):
    return [(1 - x, y), (x, 1 - y), (1 - x, 1 - y)]


class _Plan:
    def __init__(self, arrays, out_shapes, sems, start, finish, forward=None, forward_at=0.7):
        self.arrays, self.out_shapes, self.sems = list(arrays), list(out_shapes), list(sems)
        self.start, self.finish, self.forward, self.forward_at = start, finish, forward, forward_at


def _call(body, *, name, grid, in_specs, out_specs, out_shape, operands, scratch_shapes=(), plans=()):
    n_in, n_out, n_scr = len(operands), len(out_shape), len(scratch_shapes)
    p_in = [a for p in plans for a in p.arrays]
    p_out = [s for p in plans for s in p.out_shapes]
    p_sem = [s for p in plans for s in p.sems]
    steps = math.prod(grid)

    def wrapped(*refs):
        ins, refs = refs[:n_in], refs[n_in:]
        pins, refs = refs[:len(p_in)], refs[len(p_in):]
        outs, refs = refs[:n_out], refs[n_out:]
        pouts, refs = refs[:len(p_out)], refs[len(p_out):]
        scr, psems = refs[:n_scr], refs[n_scr:]
        step = 0
        for axis, size in enumerate(grid):
            step = step * size + pl.program_id(axis)
        bound = []
        for p in plans:
            bound.append((p, pins[:len(p.arrays)], pouts[:len(p.out_shapes)], psems[:len(p.sems)]))
            pins, pouts, psems = pins[len(p.arrays):], pouts[len(p.out_shapes):], psems[len(p.sems):]

        @pl.when(step == 0)
        def _():
            for p, a, b, s in bound:
                p.start(a, b, s)

        body(*ins, *outs, *scr)

        for p, a, b, s in bound:
            if p.forward is not None:
                @pl.when(step == min(int(p.forward_at * steps), steps - 1))
                def _(p=p, a=a, b=b, s=s):
                    p.forward(a, b, s)

        @pl.when(step == steps - 1)
        def _():
            for p, a, b, s in bound:
                p.finish(a, b, s)

    hbm = pl.BlockSpec(memory_space=pl.ANY)
    return pl.pallas_call(
        wrapped if plans else body, name=name, grid=grid,
        out_shape=tuple(out_shape) + tuple(p_out),
        in_specs=list(in_specs) + [hbm] * len(p_in),
        out_specs=tuple(out_specs) + tuple([hbm] * len(p_out)),
        scratch_shapes=list(scratch_shapes) + p_sem,
        compiler_params=_cparams(len(grid)),
    )(*operands, *p_in)


def _run_plan(plan, name):
    n_in, n_out = len(plan.arrays), len(plan.out_shapes)

    def body(*refs):
        a, b, s = refs[:n_in], refs[n_in:n_in + n_out], refs[n_in + n_out:]
        plan.start(a, b, s)
        if plan.forward is not None:
            plan.forward(a, b, s)
        plan.finish(a, b, s)

    hbm = pl.BlockSpec(memory_space=pl.ANY)
    return pl.pallas_call(
        body, name=name, out_shape=tuple(plan.out_shapes),
        in_specs=[hbm] * n_in, out_specs=tuple([hbm] * n_out), scratch_shapes=plan.sems,
    )(*plan.arrays)


def _gather_plan(shards, forward_at=0.7):
    n = len(shards)

    def copies(ins, outs, sems):
        send_sems, recv_sems, local_sems = sems
        x, y, c = _my_place()
        me, sibling = (x, y, c), (x, y, 1 - c)
        chips = _other_chips(x, y)

        def idx(p):
            return 4 * p[0] + 2 * p[1] + p[2]

        def copy(a, k, block, to, src=None):
            return pltpu.make_async_remote_copy(
                src_ref=outs[a].at[idx(block)] if src is None else src, dst_ref=outs[a].at[idx(block)],
                send_sem=send_sems.at[a, k], recv_sem=recv_sems.at[a, k], device_id=to, device_id_type=MESH)

        mine = [pltpu.make_async_copy(ins[a], outs[a].at[idx(me)], local_sems.at[a]) for a in range(n)]
        first = []
        for a in range(n):
            first.append(copy(a, 0, me, sibling, src=ins[a]))
            first += [copy(a, 1 + j, me, (*chip, c), src=ins[a]) for j, chip in enumerate(chips)]
        arrived = [copy(a, 1 + j, (*chip, c), me) for j, chip in enumerate(chips) for a in range(n)]
        passed = [copy(a, 4 + j, (*chip, c), sibling) for j, chip in enumerate(chips) for a in range(n)]
        from_sibling = []
        for a in range(n):
            from_sibling.append(copy(a, 0, sibling, me))
            from_sibling += [copy(a, 4 + j, (*chip, 1 - c), me) for j, chip in enumerate(chips)]
        return mine, first, arrived, passed, from_sibling

    def start(ins, outs, sems):
        mine, first, _, _, _ = copies(ins, outs, sems)
        for cp in mine + first:
            cp.start()

    def forward(ins, outs, sems):
        _, _, arrived, passed, _ = copies(ins, outs, sems)
        for landed, onward in zip(arrived, passed):
            landed.wait_recv()
            onward.start()

    def finish(ins, outs, sems):
        mine, first, _, passed, from_sibling = copies(ins, outs, sems)
        for cp in from_sibling:
            cp.wait_recv()
        for cp in first + passed:
            cp.wait_send()
        for cp in mine:
            cp.wait()

    return _Plan(shards, [jax.ShapeDtypeStruct((N_DEV,) + s.shape, s.dtype) for s in shards],
                 [pltpu.SemaphoreType.DMA((n, 7)), pltpu.SemaphoreType.DMA((n, 7)), pltpu.SemaphoreType.DMA((n,))],
                 start, finish, forward, forward_at)


def _sibling_plan(scatter, whole=()):
    ns = len(scatter)
    n = ns + len(whole)

    def copies(ins, outs, sems):
        send_sems, recv_sems = sems
        x, y, c = _my_place()
        out = []
        for a in range(n):
            for k in range(4 if a < ns else 1):
                src = ins[a].at[2 * k + (1 - c)] if a < ns else ins[a]
                dst = outs[a].at[k] if a < ns else outs[a]
                out.append(pltpu.make_async_remote_copy(
                    src_ref=src, dst_ref=dst, send_sem=send_sems.at[a, k], recv_sem=recv_sems.at[a, k],
                    device_id=(x, y, 1 - c), device_id_type=MESH))
        return out

    def start(ins, outs, sems):
        for cp in copies(ins, outs, sems):
            cp.start()

    def finish(ins, outs, sems):
        for cp in copies(ins, outs, sems):
            cp.wait()

    shapes = [jax.ShapeDtypeStruct((4,) + p.shape[1:], p.dtype) for p in scatter]
    shapes += [jax.ShapeDtypeStruct(p.shape, p.dtype) for p in whole]
    return _Plan(list(scatter) + list(whole), shapes,
                 [pltpu.SemaphoreType.DMA((n, 4)), pltpu.SemaphoreType.DMA((n, 4))], start, finish)


def _pair_sum(mine, theirs, c_idx, name):
    _, R, C = mine.shape

    def body(c_ref, a_ref, b_ref, o_ref):
        o_ref[...] = (a_ref[...].astype(F32) + b_ref[...].astype(F32)).astype(o_ref.dtype)

    grid_spec = pltpu.PrefetchScalarGridSpec(
        num_scalar_prefetch=1, grid=(4,),
        in_specs=[pl.BlockSpec((1, R, C), lambda k, c_ref: (2 * k + c_ref[0], 0, 0)),
                  pl.BlockSpec((1, R, C), lambda k, c_ref: (k, 0, 0))],
        out_specs=pl.BlockSpec((1, R, C), lambda k, c_ref: (k, 0, 0)))
    return pl.pallas_call(
        body, name=name, grid_spec=grid_spec,
        out_shape=jax.ShapeDtypeStruct((4, R, C), mine.dtype),
        compiler_params=_cparams(1),
    )(c_idx, mine, theirs)


def _pair_sum_whole(mine, theirs, name):
    n = len(mine)

    def body(*refs):
        for a in range(n):
            refs[2 * n + a][...] = refs[a][...] + refs[n + a][...]

    vm = pl.BlockSpec(memory_space=pltpu.VMEM)
    return pl.pallas_call(
        body, name=name, out_shape=tuple(jax.ShapeDtypeStruct(p.shape, p.dtype) for p in mine),
        in_specs=[vm] * (2 * n), out_specs=tuple([vm] * n),
    )(*mine, *theirs)


def _chips_plan(scatter, whole=()):
    ns = len(scatter)
    n = ns + len(whole)

    def copies(ins, outs, sems):
        send_sems, recv_sems, local_sems = sems
        x, y, c = _my_place()
        my_chip = 2 * x + y

        def src(a, k):
            return ins[a].at[k] if a < ns else ins[a]

        local = [pltpu.make_async_copy(src(a, my_chip), outs[a].at[my_chip], local_sems.at[a]) for a in range(n)]
        remote = []
        for a in range(n):
            for j, (px, py) in enumerate(_other_chips(x, y)):
                remote.append(pltpu.make_async_remote_copy(
                    src_ref=src(a, 2 * px + py), dst_ref=outs[a].at[my_chip],
                    send_sem=send_sems.at[a, j], recv_sem=recv_sems.at[a, j],
                    device_id=(px, py, c), device_id_type=MESH))
        return local + remote

    def start(ins, outs, sems):
        for cp in copies(ins, outs, sems):
            cp.start()

    def finish(ins, outs, sems):
        for cp in copies(ins, outs, sems):
            cp.wait()

    shapes = [jax.ShapeDtypeStruct(s.shape, s.dtype) for s in scatter]
    shapes += [jax.ShapeDtypeStruct((4,) + s.shape, s.dtype) for s in whole]
    return _Plan(list(scatter) + list(whole), shapes,
                 [pltpu.SemaphoreType.DMA((n, 3)), pltpu.SemaphoreType.DMA((n, 3)), pltpu.SemaphoreType.DMA((n,))],
                 start, finish)


def _sum_parts(parts, name):
    _, R, C = parts.shape

    def body(p_ref, o_ref):
        o_ref[...] = ((p_ref[0].astype(F32) + p_ref[1].astype(F32)) + p_ref[2].astype(F32)) + p_ref[3].astype(F32)

    vm = pl.BlockSpec(memory_space=pltpu.VMEM)
    return pl.pallas_call(body, name=name, out_shape=jax.ShapeDtypeStruct((R, C), F32),
                          in_specs=[vm], out_specs=vm,
                          compiler_params=pltpu.CompilerParams(vmem_limit_bytes=VMEM_LIMIT))(parts)


def _small_update(parts, w, m, v, loss_parts, name):
    n = len(parts)

    def body(*refs):
        p_refs, w_refs, m_refs, v_refs = (refs[i * n:(i + 1) * n] for i in range(4))
        lp_ref = refs[4 * n]
        outs = refs[4 * n + 1:]
        g_refs, d_refs, mo_refs, vo_refs = (outs[i * n:(i + 1) * n] for i in range(4))
        for a in range(n):
            g = ((p_refs[a][0] + p_refs[a][1]) + p_refs[a][2]) + p_refs[a][3]
            g_refs[a][...] = g
            d_refs[a][...], mo_refs[a][...], vo_refs[a][...] = _adam_math(w_refs[a][...], g, m_refs[a][...],
                                                                          v_refs[a][...])
        outs[4 * n][...] = ((lp_ref[0] + lp_ref[1]) + lp_ref[2]) + lp_ref[3]

    vm = pl.BlockSpec(memory_space=pltpu.VMEM)
    shapes = [jax.ShapeDtypeStruct(t.shape, F32) for _ in range(4) for t in w]
    shapes.append(jax.ShapeDtypeStruct(loss_parts.shape[1:], F32))
    outs = pl.pallas_call(
        body, name=name, out_shape=tuple(shapes),
        in_specs=[vm] * (4 * n + 1), out_specs=tuple([vm] * (4 * n + 1)),
    )(*parts, *w, *m, *v, loss_parts)
    return outs[0:n], outs[n:2 * n], outs[2 * n:3 * n], outs[3 * n:4 * n], outs[4 * n]


BIG = [("w_in", 1), ("w_branch_att", 1), ("w_branch_sg", 1), ("w_out", 0), ("w_xq", 0), ("w_xkv", 1), ("w_xo", 0),
       ("w_ffn_in", 1), ("w_ffn_out", 0)]
SMALL = [("norm_mix_g", (1, D)), ("rel_bias", (8, NREL)), ("sg_ln_g", (8, 64)), ("sg_ln_b", (8, 64)),
         ("sg_w", (8, 128, 128)), ("sg_b", (8, 128)), ("norm_xattn_g", (1, D)), ("norm_mem_g", (1, D)),
         ("norm_ffn_g", (1, D)), ("norm_final_g", (1, D))]
ADAM_ROWS = {"w_in": 256, "w_branch_att": 512, "w_branch_sg": 512, "w_xkv": 1024, "w_ffn_in": 256}


def _full(gathered):
    return gathered.reshape(N_DEV * gathered.shape[1], gathered.shape[2])


def _blocks(grad):
    return grad.reshape(N_DEV, grad.shape[0] // N_DEV, grad.shape[1])


def kernel(x, mem, norm_mix_g, w_in, rel_bias, sg_ln_g, sg_ln_b, sg_w, sg_b, w_branch_att, w_branch_sg, w_out, norm_xattn_g, norm_mem_g, w_xq, w_xkv, w_xo, norm_ffn_g, w_ffn_in, w_ffn_out, norm_final_g, loss_target, m_norm_mix_g, m_w_in, m_rel_bias, m_sg_ln_g, m_sg_ln_b, m_sg_w, m_sg_b, m_w_branch_att, m_w_branch_sg, m_w_out, m_norm_xattn_g, m_norm_mem_g, m_w_xq, m_w_xkv, m_w_xo, m_norm_ffn_g, m_w_ffn_in, m_w_ffn_out, m_norm_final_g, v_norm_mix_g, v_w_in, v_rel_bias, v_sg_ln_g, v_sg_ln_b, v_sg_w, v_sg_b, v_w_branch_att, v_w_branch_sg, v_w_out, v_norm_xattn_g, v_norm_mem_g, v_w_xq, v_w_xkv, v_w_xo, v_norm_ffn_g, v_w_ffn_in, v_w_ffn_out, v_norm_final_g):
    args = dict(locals())
    big_names = [n for n, _ in BIG]
    small_names = [n for n, _ in SMALL]
    S = x.shape[1]

    x, mem, tgt = x.reshape(S, D), mem.reshape(MEM, D), loss_target.reshape(S, D)
    small = {n: args[n].reshape(shape) for n, shape in SMALL}
    g1, g2, g3 = small["norm_mix_g"], small["norm_xattn_g"], small["norm_ffn_g"]
    g_mem, g4 = small["norm_mem_g"], small["norm_final_g"]
    lng = small["sg_ln_g"].reshape(1, SG_W)
    lnb = small["sg_ln_b"].reshape(1, SG_W)
    b_exp = jnp.broadcast_to(small["sg_b"].T[:, :, None], (128, 8, 64)).reshape(128, SG_W)
    rel_pad = jnp.pad(small["rel_bias"], ((0, 0), (0, 384 - NREL)))
    c_idx = lax.axis_index("c").astype(jnp.int32).reshape(1)

    shard = {n: (args[n][0].T if axis == 1 else args[n][0]).astype(BF16) for n, axis in BIG}
    w_in_t = _full(_run_plan(_gather_plan([shard["w_in"]]), "ag_w_in")[0])
    bias = _bias_table(rel_pad)
    mix_names = ["w_branch_att", "w_branch_sg", "w_out", "w_xq", "w_xkv", "w_xo"]
    qkv, uv, gates, h, *got = _in_proj(x, g1, w_in_t, plans=[_gather_plan([shard[n] for n in mix_names])])
    wba_t, wbs_t, w_out_f, w_xq_f, w_xkv_t, w_xo_f = (_full(g) for g in got)
    y_att, *got = _attn_fwd(qkv, bias, plans=[_gather_plan([shard["w_ffn_in"], shard["w_ffn_out"]])])
    w_ffn_in_t, w_ffn_out_f = (_full(g) for g in got)
    y_sg = _sgu_fwd(uv, lng, lnb, small["sg_w"], b_exp)
    x1 = _merge_fwd(x, y_att, y_sg, gates, wba_t, wbs_t, w_out_f)
    kv, mn = _mem_kv(mem, g_mem, w_xkv_t)
    x2 = _xattn_fwd(x1, g2, w_xq_f, kv, w_xo_f)
    dx3, gu, hf, act, loss_part, dg4 = _ffn_fwd(x2, tgt, g3, w_ffn_in_t, w_ffn_out_f, g4)

    def pair_sums(names, mine, theirs):
        return [_pair_sum(a, b, c_idx, "rs_pair_" + n) for n, a, b in zip(names, mine, theirs)]

    dx2, dgu, dg3 = _ffn_bwd(dx3, gu, x2, g3, w_ffn_out_f, w_ffn_in_t)
    ffn_names = ["w_ffn_out", "w_ffn_in"]
    ffn_mine = [_blocks(_dw(act, dx3, 1408, 1024, 512, "dw_ffn_out")),
                _blocks(_dw(dgu, hf, 1408, 1024, 512, "dw_ffn_in"))]
    dx1, o_x, dq_x, hx, dkv, dg2, *ffn_theirs = _xattn_bwd(dx2, x1, g2, w_xq_f, w_xo_f, kv,
                                                           plans=[_sibling_plan(ffn_mine)])
    ffn_chip = pair_sums(ffn_names, ffn_mine, ffn_theirs)
    d_xkv, dg_mem = _mem_kv_bwd(dkv, mem, g_mem, mn, w_xkv_t)
    merged, d_a, d_b, dy_att, dy_sg, dgates = _merge_bwd(dx1, y_att, y_sg, gates, wba_t, wbs_t, w_out_f)
    mid_names = ["w_xo", "w_xq", "w_xkv", "w_out", "w_branch_att", "w_branch_sg"]
    mid_mine = [_blocks(g) for g in (
        _dw(o_x, dx2, 1024, 1024, 512, "dw_xo"), _dw(hx, dq_x, 1024, 1024, 512, "dw_xq"), d_xkv,
        _dw(merged, dx1, 1024, 1024, 512, "dw_out"), _dw(d_a, y_att, 1024, 512, 512, "dw_branch_att"),
        _dw(d_b, y_sg, 1024, 512, 512, "dw_branch_sg"))]
    duv, d_sgw, d_bx, d_lng, d_lnb, *got = _sgu_bwd(uv, dy_sg, lng, lnb, small["sg_w"], b_exp,
                                                    plans=[_chips_plan(ffn_chip), _sibling_plan(mid_mine)])
    ffn_all, mid_theirs = got[:2], got[2:]
    mid_chip = pair_sums(mid_names, mid_mine, mid_theirs)
    dq, dk, dv, ds_sum, *mid_all = _attn_bwd(qkv, dy_att, bias, plans=[_chips_plan(mid_chip)])
    d_rel = _bias_grad(ds_sum)
    grad_x, dz, dg1 = _in_bwd(dq, dk, dv, duv, dgates, x, dx1, g1, w_in_t)
    in_mine = [_blocks(_dw(dz, h, 1152, 1024, 512, "dw_in"))]

    gs = {"norm_mix_g": dg1, "rel_bias": d_rel[:, :NREL], "sg_ln_g": d_lng.reshape(8, 64),
          "sg_ln_b": d_lnb.reshape(8, 64), "sg_w": d_sgw, "sg_b": d_bx.reshape(128, 8, 64)[:, :, 0].T,
          "norm_xattn_g": dg2, "norm_mem_g": dg_mem, "norm_ffn_g": dg3, "norm_final_g": dg4}
    whole = [gs[n] for n in small_names] + [loss_part]
    in_theirs, *sib_whole = _run_plan(_sibling_plan(in_mine, whole), "rs_sibling")
    in_chip = pair_sums(["w_in"], in_mine, [in_theirs])
    chip_whole = _pair_sum_whole(whole, sib_whole, "rs_pair_small")
    in_all, *all_whole = _run_plan(_chips_plan(in_chip, chip_whole), "rs_chips")
    all_parts = dict(zip(ffn_names + mid_names + ["w_in"], list(ffn_all) + list(mid_all) + [in_all]))

    res = {}
    for n, axis in BIG:
        parts = all_parts[n]
        wmv = [args[p + n][0] for p in ("", "m_", "v_")]
        if axis == 1:
            parts = _sum_parts(parts, "rs_sum_" + n).T[None]
        tr = ADAM_ROWS[n] if axis == 1 else wmv[0].shape[0]
        res[n] = [t[None] for t in _adam(parts, *wmv, tr, "adam_" + n)]
    small_res = _small_update(
        all_whole[:-1], [small[n] for n in small_names], [args["m_" + n].reshape(s) for n, s in SMALL],
        [args["v_" + n].reshape(s) for n, s in SMALL], all_whole[-1], "adam_small")
    for i, n in enumerate(small_names):
        res[n] = [small_res[k][i].reshape(args[n].shape) for k in range(4)]
    loss = small_res[4][0, 0]

    order = ["norm_mix_g", "w_in", "rel_bias", "sg_ln_g", "sg_ln_b", "sg_w", "sg_b", "w_branch_att", "w_branch_sg",
             "w_out", "norm_xattn_g", "norm_mem_g", "w_xq", "w_xkv", "w_xo", "norm_ffn_g", "w_ffn_in", "w_ffn_out",
             "norm_final_g"]
    outs = [loss, grad_x.reshape(1, S, D)]
    for k in range(4):
        outs += [res[n][k] for n in order]
    return tuple(outs)
```

```python
import math

import jax
import jax.numpy as jnp
from jax import lax
from jax.experimental import pallas as pl
from jax.experimental.pallas import tpu as pltpu

F32 = jnp.float32
BF16 = jnp.bfloat16

D = 1024
ATT_W = 512
SG_W = 512
IN_COLS = 4608
DFF = 2816
MEM = 256
XH = 4
CHUNK = 64
BAND_KEYS = 640
ATT_R = 512
ATT_SUBS = ATT_R // 128
REL_CLIP = 128
NREL = 2 * REL_CLIP + 1
EPS = 1e-6
NEG = -1e30
N_DEV = 8

ADAM_LR = 0.001
ADAM_B1 = 0.9
ADAM_B2 = 0.999
ADAM_EPS = 1e-08
ADAM_WD = 0.01
ADAM_STEP = 10

LANES = 128
VMEM_LIMIT = 56 * 1024 * 1024
MESH = pl.DeviceIdType.MESH


def _cparams(n_axes):
    return pltpu.CompilerParams(dimension_semantics=("arbitrary",) * n_axes, vmem_limit_bytes=VMEM_LIMIT)


def _resident(shape):
    zeros = (0,) * len(shape)
    return pl.BlockSpec(shape, lambda *_: zeros, pipeline_mode=pl.Buffered(1))


def _rows(tm, cols, col_block=0):
    return pl.BlockSpec((tm, cols), lambda i: (i, col_block))


def _sigmoid(x):
    return 1.0 / (1.0 + jnp.exp(-x))


_GELU_C = math.sqrt(2.0 / math.pi)


def _gelu(x):
    t = jnp.tanh(_GELU_C * (x + 0.044715 * (x * x * x)))
    return x * (0.5 * (1.0 + t))


def _gelu_and_grad(x):
    x2 = x * x
    t = jnp.tanh(_GELU_C * (x + 0.044715 * (x2 * x)))
    cdf = 0.5 * (1.0 + t)
    dcdf = 0.5 * (1.0 - t * t) * (_GELU_C * (1.0 + 3.0 * 0.044715 * x2))
    return x * cdf, cdf + x * dcdf


def _rstd(x):
    return lax.rsqrt(jnp.mean(x * x, axis=-1, keepdims=True) + EPS)


def _rms_bwd(dh, x, r, g):
    xh = x * r
    dxh = dh * g
    dx = r * (dxh - xh * jnp.mean(dxh * xh, axis=-1, keepdims=True))
    dg = jnp.sum(dh * xh, axis=0, keepdims=True)
    return dx, dg


def _group_sum64(x):
    r = lax.broadcasted_iota(jnp.int32, (LANES, LANES), 0) // 64
    c = lax.broadcasted_iota(jnp.int32, (LANES, LANES), 1) // 64
    same_group = (r == c).astype(BF16)

    def one(v):
        hi = v.astype(BF16)
        rest = v - hi.astype(F32)
        mid = rest.astype(BF16)
        lo = (rest - mid.astype(F32)).astype(BF16)
        return _dot(hi, same_group) + _dot(mid, same_group) + _dot(lo, same_group)

    pieces = [one(x[:, LANES * j:LANES * (j + 1)]) for j in range(x.shape[1] // LANES)]
    return pieces[0] if len(pieces) == 1 else jnp.concatenate(pieces, axis=1)


def _dot(a, b):
    return jnp.dot(a, b, preferred_element_type=F32)


def _dot_nt(a, b):
    return lax.dot_general(a, b, (((1,), (1,)), ((), ())), preferred_element_type=F32)


def _dot_tn(a, b):
    return lax.dot_general(a, b, (((0,), (0,)), ((), ())), preferred_element_type=F32)


DIAGS = 768


def _diag_onehot():
    r_idx = lax.broadcasted_iota(jnp.int32, (384, DIAGS), 0)
    t_idx = lax.broadcasted_iota(jnp.int32, (384, DIAGS), 1)
    dist = (8 * CHUNK + 127) - t_idx
    return (jnp.clip(dist, -REL_CLIP, REL_CLIP) + REL_CLIP == r_idx).astype(F32)


def _shift_rows(x, reverse):
    row = lax.broadcasted_iota(jnp.int32, x.shape, 0)
    for k in range(7):
        amt = (DIAGS - (1 << k)) if reverse else (1 << k)
        x = jnp.where(((row >> k) & 1) == 1, pltpu.roll(x, amt, 1), x)
    return x


N_TABLES = 1 + ATT_SUBS


def _bias_table(rel_bias_pad):
    def body(rb_ref, out_ref):
        per_diag = jnp.dot(rb_ref[...], _diag_onehot(), preferred_element_type=F32,
                           precision=lax.Precision.HIGHEST)
        a = lax.broadcasted_iota(jnp.int32, (128, BAND_KEYS), 0)
        b = lax.broadcasted_iota(jnp.int32, (128, BAND_KEYS), 1)
        band = (b // CHUNK >= a // CHUNK) & (b // CHUNK <= a // CHUNK + 8)
        for h in range(8):
            rows = jnp.broadcast_to(per_diag[h:h + 1, :], (128, DIAGS))
            table = _shift_rows(pltpu.roll(rows, DIAGS - 127, 1), reverse=False)[:, :BAND_KEYS]
            out_ref[0, h] = jnp.where(band, table, NEG)
            for s in range(ATT_SUBS):
                out_ref[1 + s, h] = jnp.where(band & (b >= 8 * CHUNK - 128 * s), table, NEG)

    return pl.pallas_call(
        body, name="bias_table",
        out_shape=jax.ShapeDtypeStruct((N_TABLES, 8, 128, BAND_KEYS), F32),
        in_specs=[pl.BlockSpec(memory_space=pltpu.VMEM)],
        out_specs=pl.BlockSpec(memory_space=pltpu.VMEM),
        compiler_params=pltpu.CompilerParams(vmem_limit_bytes=VMEM_LIMIT),
    )(rel_bias_pad)


def _bias_grad(ds_sum):
    def body(ds_ref, out_ref):
        sums = []
        for h in range(8):
            padded = jnp.concatenate([ds_ref[h], jnp.zeros((128, DIAGS - BAND_KEYS), F32)], axis=1)
            skewed = pltpu.roll(_shift_rows(padded, reverse=True), 127, 1)
            sums.append(jnp.sum(skewed, axis=0, keepdims=True))
        per_diag = jnp.concatenate(sums, axis=0)
        out_ref[...] = lax.dot_general(per_diag, _diag_onehot(), (((1,), (1,)), ((), ())),
                                       preferred_element_type=F32, precision=lax.Precision.HIGHEST)

    return pl.pallas_call(
        body, name="bias_grad",
        out_shape=jax.ShapeDtypeStruct((8, 384), F32),
        in_specs=[pl.BlockSpec(memory_space=pltpu.VMEM)],
        out_specs=pl.BlockSpec(memory_space=pltpu.VMEM),
    )(ds_sum)


def _in_proj(x, g1, w_in_t, plans=(), tm=512):
    S = x.shape[0]

    def body(x_ref, g_ref, w_ref, qkv_ref, uv_ref, gate_ref, h_ref):
        xv = x_ref[...]
        h = (xv * _rstd(xv) * g_ref[...]).astype(BF16)
        h_ref[...] = h
        for c in range(IN_COLS // 512):
            zc = _dot_nt(h, w_ref[512 * c:512 * (c + 1), :])
            if c == 0:
                qkv_ref[:, 0:512] = (zc * 0.125).astype(BF16)
            elif c < 3:
                qkv_ref[:, 512 * c:512 * (c + 1)] = zc.astype(BF16)
            elif c < 5:
                uv_ref[:, 512 * (c - 3):512 * (c - 2)] = zc.astype(BF16)
            else:
                gate_ref[:, 512 * (c - 5):512 * (c - 4)] = zc.astype(BF16)

    return _call(
        body, name="in_proj", grid=(S // tm,),
        out_shape=(jax.ShapeDtypeStruct((S, 3 * ATT_W), BF16), jax.ShapeDtypeStruct((S, 2 * SG_W), BF16),
                   jax.ShapeDtypeStruct((S, 2 * D), BF16), jax.ShapeDtypeStruct((S, D), BF16)),
        in_specs=[_rows(tm, D), _resident((1, D)), _resident((IN_COLS, D))],
        out_specs=(_rows(tm, 3 * ATT_W), _rows(tm, 2 * SG_W), _rows(tm, 2 * D), _rows(tm, D)),
        operands=(x, g1, w_in_t), plans=plans)


def _att_specs():
    R = ATT_R
    q = pl.BlockSpec((R, LANES), lambda j, i: (i, j))
    kp = pl.BlockSpec((R, LANES), lambda j, i: (jnp.maximum(i - 1, 0), 4 + j))
    kc = pl.BlockSpec((R, LANES), lambda j, i: (i, 4 + j))
    vp = pl.BlockSpec((R, LANES), lambda j, i: (jnp.maximum(i - 1, 0), 8 + j))
    vc = pl.BlockSpec((R, LANES), lambda j, i: (i, 8 + j))
    bias = pl.BlockSpec((N_TABLES, 2, 128, BAND_KEYS), lambda j, i: (0, j, 0, 0))
    return [q, kp, kc, vp, vc, bias]


def _attn_fwd(qkv, bias, plans=()):
    S = qkv.shape[0]
    R = ATT_R

    def body(q_ref, kp_ref, kc_ref, vp_ref, vc_ref, b_ref, o_ref, lse_ref):
        i = pl.program_id(1)
        lane = lax.broadcasted_iota(jnp.int32, (1, LANES), 1)
        kwin = jnp.concatenate([kp_ref[...], kc_ref[...]], axis=0)
        vwin = jnp.concatenate([vp_ref[...], vc_ref[...]], axis=0)
        for sub in range(ATT_SUBS):
            q2 = q_ref[128 * sub:128 * (sub + 1), :]
            kw = kwin[128 * sub:128 * sub + BAND_KEYS]
            vw = vwin[128 * sub:128 * sub + BAND_KEYS]
            table = jnp.where(i == 0, 1 + sub, 0)
            o = jnp.zeros((128, LANES), F32)
            lse = jnp.zeros((128, LANES), F32)
            for hh in range(2):
                mine = (lane >= 64 * hh) & (lane < 64 * (hh + 1))
                s = _dot_nt(jnp.where(mine, q2, 0), kw) + b_ref[table, hh]
                top = jnp.max(s, axis=-1, keepdims=True)
                p = jnp.exp(s - top)
                total = jnp.sum(p, axis=-1, keepdims=True)
                p = p / total
                o = o + _dot(p.astype(BF16), jnp.where(mine, vw, 0))
                lse = jnp.where(mine, top + jnp.log(total), lse)
            o_ref[128 * sub:128 * (sub + 1), :] = o.astype(BF16)
            lse_ref[128 * sub:128 * (sub + 1), :] = lse

    blk = pl.BlockSpec((R, LANES), lambda j, i: (i, j))
    return _call(
        body, name="attn_fwd", grid=(4, S // R),
        out_shape=(jax.ShapeDtypeStruct((S, ATT_W), BF16), jax.ShapeDtypeStruct((S, ATT_W), F32)),
        in_specs=_att_specs(), out_specs=(blk, blk),
        operands=(qkv, qkv, qkv, qkv, qkv, bias), plans=plans)


def _sg_mask():
    t = lax.broadcasted_iota(jnp.int32, (128, 128), 0)
    s = lax.broadcasted_iota(jnp.int32, (128, 128), 1)
    return (s // CHUNK) <= (t // CHUNK)


def _sg_layernorm(gv, lng, lnb):
    mu = _group_sum64(gv) * (1.0 / 64)
    xc = gv - mu
    var = _group_sum64(xc * xc) * (1.0 / 64)
    rstd = lax.rsqrt(var + EPS)
    vhat = xc * rstd
    return vhat * lng + lnb, vhat, rstd


def _sgu_fwd(uv, lng, lnb, sg_w, b_exp, tm=512):
    S = uv.shape[0]

    def body(uv_ref, lng_ref, lnb_ref, w_ref, b_ref, y_ref):
        lane = lax.broadcasted_iota(jnp.int32, (1, LANES), 1)
        lo = lane < 64
        mask = _sg_mask()
        gu = _gelu(uv_ref[:, 0:SG_W].astype(F32))
        vln, _, _ = _sg_layernorm(_gelu(uv_ref[:, SG_W:2 * SG_W].astype(F32)), lng_ref[...], lnb_ref[...])
        for gp in range(4):
            w0 = jnp.where(mask, w_ref[2 * gp], 0).astype(BF16)
            w1 = jnp.where(mask, w_ref[2 * gp + 1], 0).astype(BF16)
            cols = slice(128 * gp, 128 * (gp + 1))
            for n in range(tm // 128):
                rows = slice(128 * n, 128 * (n + 1))
                vl = vln[rows, cols]
                sv = (_dot(w0, jnp.where(lo, vl, 0).astype(BF16)) + _dot(w1, jnp.where(lo, 0, vl).astype(BF16))
                      + b_ref[:, cols])
                y_ref[rows, cols] = (gu[rows, cols] * sv).astype(BF16)

    return pl.pallas_call(
        body, name="sgu_fwd", grid=(S // tm,),
        out_shape=jax.ShapeDtypeStruct((S, SG_W), BF16),
        in_specs=[_rows(tm, 2 * SG_W), _resident((1, SG_W)), _resident((1, SG_W)),
                  _resident((8, 128, 128)), _resident((128, SG_W))],
        out_specs=_rows(tm, SG_W),
        compiler_params=_cparams(1),
    )(uv, lng, lnb, sg_w, b_exp)


def _merge_fwd(x, y_att, y_sg, gates, wba_t, wbs_t, w_out, tm=512):
    S = x.shape[0]

    def body(x_ref, ya_ref, ys_ref, g_ref, wba_ref, wbs_ref, wo_ref, x1_ref):
        a = _dot_nt(ya_ref[...], wba_ref[...])
        b = _dot_nt(ys_ref[...], wbs_ref[...])
        merged = _sigmoid(g_ref[:, 0:D].astype(F32)) * a + _sigmoid(g_ref[:, D:2 * D].astype(F32)) * b
        x1_ref[...] = x_ref[...] + _dot(merged.astype(BF16), wo_ref[...])

    return pl.pallas_call(
        body, name="merge_fwd", grid=(S // tm,),
        out_shape=jax.ShapeDtypeStruct((S, D), F32),
        in_specs=[_rows(tm, D), _rows(tm, ATT_W), _rows(tm, SG_W), _rows(tm, 2 * D),
                  _resident((D, ATT_W)), _resident((D, SG_W)), _resident((D, D))],
        out_specs=_rows(tm, D),
        compiler_params=_cparams(1),
    )(x, y_att, y_sg, gates, wba_t, wbs_t, w_out)


def _mem_kv(mem, g_mem, w_xkv_t):
    def body(m_ref, g_ref, w_ref, kv_ref, mn_ref):
        mv = m_ref[...]
        mn = (mv * _rstd(mv) * g_ref[...]).astype(BF16)
        mn_ref[...] = mn
        kv_ref[...] = _dot_nt(mn, w_ref[...]).astype(BF16)

    vm = pl.BlockSpec(memory_space=pltpu.VMEM)
    return pl.pallas_call(
        body, name="mem_kv",
        out_shape=(jax.ShapeDtypeStruct((MEM, 2 * D), BF16), jax.ShapeDtypeStruct((MEM, D), BF16)),
        in_specs=[vm, vm, vm], out_specs=(vm, vm),
        compiler_params=pltpu.CompilerParams(vmem_limit_bytes=VMEM_LIMIT),
    )(mem, g_mem, w_xkv_t)


def _xatt_head(qx, kv_ref, h):
    hs = slice(256 * h, 256 * (h + 1))
    s = _dot_nt(qx[:, hs], kv_ref[:, hs])
    p = jnp.exp(s - jnp.max(s, axis=-1, keepdims=True))
    return p / jnp.sum(p, axis=-1, keepdims=True)


def _xattn_fwd(x1, g2, w_xq, kv, w_xo, tm=512):
    S = x1.shape[0]

    def body(x_ref, g_ref, wq_ref, kv_ref, wo_ref, x2_ref):
        xv = x_ref[...]
        hx = (xv * _rstd(xv) * g_ref[...]).astype(BF16)
        qx = (_dot(hx, wq_ref[...]) * (1.0 / 16)).astype(BF16)
        outs = []
        for h in range(XH):
            p = _xatt_head(qx, kv_ref, h)
            outs.append(_dot(p.astype(BF16), kv_ref[:, D + 256 * h:D + 256 * (h + 1)]).astype(BF16))
        o = jnp.concatenate(outs, axis=1)
        x2_ref[...] = xv + _dot(o, wo_ref[...])

    return pl.pallas_call(
        body, name="xattn_fwd", grid=(S // tm,),
        out_shape=jax.ShapeDtypeStruct((S, D), F32),
        in_specs=[_rows(tm, D), _resident((1, D)), _resident((D, D)), _resident((MEM, 2 * D)), _resident((D, D))],
        out_specs=_rows(tm, D),
        compiler_params=_cparams(1),
    )(x1, g2, w_xq, kv, w_xo)


FF_CHUNK = 1408


def _ffn_fwd(x2, tgt, g3, w_ffn_in_t, w_ffn_out, g4, tm=256):
    S = x2.shape[0]

    def body(x_ref, t_ref, g3_ref, wi_ref, wo_ref, g4_ref, dx3_ref, gu_ref, hf_ref, act_ref, loss_ref, dg4_ref):
        i = pl.program_id(0)
        xv = x_ref[...]
        hf = (xv * _rstd(xv) * g3_ref[...]).astype(BF16)
        hf_ref[...] = hf
        acc = xv
        for c in range(DFF // FF_CHUNK):
            cs = slice(FF_CHUNK * c, FF_CHUNK * (c + 1))
            us = slice(DFF + FF_CHUNK * c, DFF + FF_CHUNK * (c + 1))
            gate = _dot_nt(hf, wi_ref[cs, :])
            up = _dot_nt(hf, wi_ref[us, :])
            gu_ref[:, cs] = gate.astype(BF16)
            gu_ref[:, us] = up.astype(BF16)
            act = ((gate * _sigmoid(gate)) * up).astype(BF16)
            act_ref[:, cs] = act
            acc = acc + _dot(act, wo_ref[cs, :])
        r4 = _rstd(acc)
        g4 = g4_ref[...]
        diff = acc * r4 * g4 - t_ref[...]
        dy = diff * (1.0 / D)
        dx3, dg4 = _rms_bwd(dy, acc, r4, g4)
        dx3_ref[...] = dx3
        part = 0.5 * jnp.sum(jnp.mean(diff * diff, axis=-1, keepdims=True))

        @pl.when(i == 0)
        def _():
            loss_ref[...] = jnp.zeros_like(loss_ref)
            dg4_ref[...] = jnp.zeros_like(dg4_ref)

        loss_ref[...] += jnp.full(loss_ref.shape, part, F32)
        dg4_ref[...] += dg4

    return pl.pallas_call(
        body, name="ffn_fwd", grid=(S // tm,),
        out_shape=(jax.ShapeDtypeStruct((S, D), F32), jax.ShapeDtypeStruct((S, 2 * DFF), BF16),
                   jax.ShapeDtypeStruct((S, D), BF16), jax.ShapeDtypeStruct((S, DFF), BF16),
                   jax.ShapeDtypeStruct((8, LANES), F32), jax.ShapeDtypeStruct((1, D), F32)),
        in_specs=[_rows(tm, D), _rows(tm, D), _resident((1, D)), _resident((2 * DFF, D)), _resident((DFF, D)),
                  _resident((1, D))],
        out_specs=(_rows(tm, D), _rows(tm, 2 * DFF), _rows(tm, D), _rows(tm, DFF),
                   pl.BlockSpec((8, LANES), lambda i: (0, 0)), pl.BlockSpec((1, D), lambda i: (0, 0))),
        compiler_params=_cparams(1),
    )(x2, tgt, g3, w_ffn_in_t, w_ffn_out, g4)


def _ffn_bwd(dx3, gu, x2, g3, w_ffn_out, w_ffn_in_t, tm=256):
    S = x2.shape[0]

    def body(d_ref, gu_ref, x_ref, g3_ref, wo_ref, wit_ref, dx2_ref, dgu_ref, dg3_ref):
        i = pl.program_id(0)
        d3 = d_ref[...]
        d3b = d3.astype(BF16)
        dhf = jnp.zeros((tm, D), F32)
        for c in range(DFF // FF_CHUNK):
            cs = slice(FF_CHUNK * c, FF_CHUNK * (c + 1))
            us = slice(DFF + FF_CHUNK * c, DFF + FF_CHUNK * (c + 1))
            da = _dot_nt(d3b, wo_ref[cs, :])
            gate = gu_ref[:, cs].astype(F32)
            up = gu_ref[:, us].astype(F32)
            sg = _sigmoid(gate)
            dgate = (da * up * (sg * (1.0 + gate * (1.0 - sg)))).astype(BF16)
            dup = (da * (gate * sg)).astype(BF16)
            dgu_ref[:, cs] = dgate
            dgu_ref[:, us] = dup
            dhf = dhf + _dot(dgate, wit_ref[cs, :]) + _dot(dup, wit_ref[us, :])
        xv = x_ref[...]
        dx, dg3 = _rms_bwd(dhf, xv, _rstd(xv), g3_ref[...])
        dx2_ref[...] = d3 + dx

        @pl.when(i == 0)
        def _():
            dg3_ref[...] = jnp.zeros_like(dg3_ref)

        dg3_ref[...] += dg3

    return pl.pallas_call(
        body, name="ffn_bwd", grid=(S // tm,),
        out_shape=(jax.ShapeDtypeStruct((S, D), F32), jax.ShapeDtypeStruct((S, 2 * DFF), BF16),
                   jax.ShapeDtypeStruct((1, D), F32)),
        in_specs=[_rows(tm, D), _rows(tm, 2 * DFF), _rows(tm, D), _resident((1, D)),
                  _resident((DFF, D)), _resident((2 * DFF, D))],
        out_specs=(_rows(tm, D), _rows(tm, 2 * DFF), pl.BlockSpec((1, D), lambda i: (0, 0))),
        compiler_params=_cparams(1),
    )(dx3, gu, x2, g3, w_ffn_out, w_ffn_in_t)


def _dw(a, b, tmm, tn, ts, name, out_dtype=BF16):
    S, M = a.shape
    N = b.shape[1]
    nk = S // ts

    def body(a_ref, b_ref, o_ref, acc_ref):
        k = pl.program_id(2)

        @pl.when(k == 0)
        def _():
            acc_ref[...] = jnp.zeros_like(acc_ref)

        acc_ref[...] += _dot_tn(a_ref[...].astype(BF16), b_ref[...].astype(BF16))

        @pl.when(k == nk - 1)
        def _():
            o_ref[...] = acc_ref[...].astype(out_dtype)

    return pl.pallas_call(
        body, name=name, grid=(M // tmm, N // tn, nk),
        out_shape=jax.ShapeDtypeStruct((M, N), out_dtype),
        in_specs=[pl.BlockSpec((ts, tmm), lambda m, n, k: (k, m)), pl.BlockSpec((ts, tn), lambda m, n, k: (k, n))],
        out_specs=pl.BlockSpec((tmm, tn), lambda m, n, k: (m, n)),
        scratch_shapes=[pltpu.VMEM((tmm, tn), F32)],
        compiler_params=_cparams(3),
    )(a, b)


def _xattn_bwd(dx2, x1, g2, w_xq, w_xo, kv, plans=(), tm=512):
    S = x1.shape[0]

    def body(d_ref, x_ref, g_ref, wq_ref, wo_ref, kv_ref, dx1_ref, o_ref, dq_ref, hx_ref, dkv_ref, dg2_ref):
        i = pl.program_id(0)

        @pl.when(i == 0)
        def _():
            dkv_ref[...] = jnp.zeros_like(dkv_ref)
            dg2_ref[...] = jnp.zeros_like(dg2_ref)

        d2 = d_ref[...]
        xv = x_ref[...]
        r2 = _rstd(xv)
        hx = (xv * r2 * g_ref[...]).astype(BF16)
        hx_ref[...] = hx
        qx = (_dot(hx, wq_ref[...]) * (1.0 / 16)).astype(BF16)
        do = _dot_nt(d2.astype(BF16), wo_ref[...]).astype(BF16)
        for h in range(XH):
            hs = slice(256 * h, 256 * (h + 1))
            vs = slice(D + 256 * h, D + 256 * (h + 1))
            p = _xatt_head(qx, kv_ref, h)
            pb = p.astype(BF16)
            o_ref[:, hs] = _dot(pb, kv_ref[:, vs]).astype(BF16)
            dp = _dot_nt(do[:, hs], kv_ref[:, vs])
            ds = (p * (dp - jnp.sum(dp * p, axis=-1, keepdims=True))).astype(BF16)
            dq_ref[:, hs] = (_dot(ds, kv_ref[:, hs]) * (1.0 / 16)).astype(BF16)
            dkv_ref[:, hs] += _dot_tn(ds, qx[:, hs])
            dkv_ref[:, vs] += _dot_tn(pb, do[:, hs])
        dhx = _dot_nt(dq_ref[...], wq_ref[...])
        dx, dg2 = _rms_bwd(dhx, xv, r2, g_ref[...])
        dx1_ref[...] = d2 + dx
        dg2_ref[...] += dg2

    return _call(
        body, name="xattn_bwd", grid=(S // tm,),
        out_shape=(jax.ShapeDtypeStruct((S, D), F32), jax.ShapeDtypeStruct((S, D), BF16),
                   jax.ShapeDtypeStruct((S, D), BF16), jax.ShapeDtypeStruct((S, D), BF16),
                   jax.ShapeDtypeStruct((MEM, 2 * D), F32), jax.ShapeDtypeStruct((1, D), F32)),
        in_specs=[_rows(tm, D), _rows(tm, D), _resident((1, D)), _resident((D, D)), _resident((D, D)),
                  _resident((MEM, 2 * D))],
        out_specs=(_rows(tm, D), _rows(tm, D), _rows(tm, D), _rows(tm, D),
                   pl.BlockSpec((MEM, 2 * D), lambda i: (0, 0)), pl.BlockSpec((1, D), lambda i: (0, 0))),
        operands=(dx2, x1, g2, w_xq, w_xo, kv), plans=plans)


def _mem_kv_bwd(dkv, mem, g_mem, mn, w_xkv_t):
    def body(dkv_ref, m_ref, g_ref, mn_ref, wt_ref, dw_ref, dg_ref):
        dkvb = dkv_ref[...].astype(BF16)
        dw_ref[...] = _dot_tn(dkvb, mn_ref[...]).astype(BF16)
        dmn = _dot(dkvb, wt_ref[...])
        mv = m_ref[...]
        dg_ref[...] = jnp.sum(dmn * (mv * _rstd(mv)), axis=0, keepdims=True)

    vm = pl.BlockSpec(memory_space=pltpu.VMEM)
    return pl.pallas_call(
        body, name="mem_kv_bwd",
        out_shape=(jax.ShapeDtypeStruct((2 * D, D), BF16), jax.ShapeDtypeStruct((1, D), F32)),
        in_specs=[vm] * 5, out_specs=(vm, vm),
        compiler_params=pltpu.CompilerParams(vmem_limit_bytes=VMEM_LIMIT),
    )(dkv, mem, g_mem, mn, w_xkv_t)


def _merge_bwd(dx1, y_att, y_sg, gates, wba_t, wbs_t, w_out, tm=512):
    S = dx1.shape[0]

    def body(d_ref, ya_ref, ys_ref, g_ref, wbat_ref, wbst_ref, wo_ref,
             mg_ref, da_ref, db_ref, dya_ref, dys_ref, dg_ref):
        dm = _dot_nt(d_ref[...].astype(BF16), wo_ref[...])
        a = _dot_nt(ya_ref[...], wbat_ref[...])
        b = _dot_nt(ys_ref[...], wbst_ref[...])
        sa = _sigmoid(g_ref[:, 0:D].astype(F32))
        sb = _sigmoid(g_ref[:, D:2 * D].astype(F32))
        mg_ref[...] = (sa * a + sb * b).astype(BF16)
        da = (dm * sa).astype(BF16)
        db = (dm * sb).astype(BF16)
        da_ref[...] = da
        db_ref[...] = db
        dg_ref[:, 0:D] = (dm * a * sa * (1.0 - sa)).astype(BF16)
        dg_ref[:, D:2 * D] = (dm * b * sb * (1.0 - sb)).astype(BF16)
        dya_ref[...] = _dot(da, wbat_ref[...]).astype(BF16)
        dys_ref[...] = _dot(db, wbst_ref[...]).astype(BF16)

    return pl.pallas_call(
        body, name="merge_bwd", grid=(S // tm,),
        out_shape=(jax.ShapeDtypeStruct((S, D), BF16), jax.ShapeDtypeStruct((S, D), BF16),
                   jax.ShapeDtypeStruct((S, D), BF16), jax.ShapeDtypeStruct((S, ATT_W), BF16),
                   jax.ShapeDtypeStruct((S, SG_W), BF16), jax.ShapeDtypeStruct((S, 2 * D), BF16)),
        in_specs=[_rows(tm, D), _rows(tm, ATT_W), _rows(tm, SG_W), _rows(tm, 2 * D),
                  _resident((D, ATT_W)), _resident((D, SG_W)), _resident((D, D))],
        out_specs=(_rows(tm, D), _rows(tm, D), _rows(tm, D), _rows(tm, ATT_W), _rows(tm, SG_W), _rows(tm, 2 * D)),
        compiler_params=_cparams(1),
    )(dx1, y_att, y_sg, gates, wba_t, wbs_t, w_out)


def _sgu_bwd(uv, dy_sg, lng, lnb, sg_w, b_exp, plans=(), tm=512):
    S = uv.shape[0]
    n_steps = S // tm

    def body(uv_ref, dy_ref, lng_ref, lnb_ref, w_ref, b_ref, duv_ref, dw_ref, dbx_ref, dlng_ref, dlnb_ref, dvln_ref):
        i = pl.program_id(0)

        @pl.when(i == 0)
        def _():
            dw_ref[...] = jnp.zeros_like(dw_ref)
            dbx_ref[...] = jnp.zeros_like(dbx_ref)
            dlng_ref[...] = jnp.zeros_like(dlng_ref)
            dlnb_ref[...] = jnp.zeros_like(dlnb_ref)

        lane = lax.broadcasted_iota(jnp.int32, (1, LANES), 1)
        lo = lane < 64
        mask = _sg_mask()
        lng = lng_ref[...]
        gu, dgelu_u = _gelu_and_grad(uv_ref[:, 0:SG_W].astype(F32))
        gv, dgelu_v = _gelu_and_grad(uv_ref[:, SG_W:2 * SG_W].astype(F32))
        vln, vhat, rstd = _sg_layernorm(gv, lng, lnb_ref[...])
        dy = dy_ref[...].astype(F32)
        dsv_all = dy * gu
        for gp in range(4):
            wf0 = jnp.where(mask, w_ref[2 * gp], 0)
            wf1 = jnp.where(mask, w_ref[2 * gp + 1], 0)
            w0 = wf0.astype(BF16)
            w1 = wf1.astype(BF16)
            cols = slice(128 * gp, 128 * (gp + 1))
            dw0 = jnp.zeros((128, 128), F32)
            dw1 = jnp.zeros((128, 128), F32)
            dbx = jnp.zeros((128, LANES), F32)
            for n in range(tm // 128):
                rows = slice(128 * n, 128 * (n + 1))
                vl = vln[rows, cols]
                vl0 = jnp.where(lo, vl, 0).astype(BF16)
                vl1 = jnp.where(lo, 0, vl).astype(BF16)
                sv = _dot(w0, vl0) + _dot(w1, vl1) + b_ref[:, cols]
                duv_ref[rows, cols] = (dy[rows, cols] * sv * dgelu_u[rows, cols]).astype(BF16)
                dsv = dsv_all[rows, cols]
                dbx = dbx + dsv
                ds0 = jnp.where(lo, dsv, 0).astype(BF16)
                ds1 = jnp.where(lo, 0, dsv).astype(BF16)
                dw0 = dw0 + _dot_nt(ds0, vl0)
                dw1 = dw1 + _dot_nt(ds1, vl1)
                dvln_ref[rows, cols] = _dot_tn(w0, ds0) + _dot_tn(w1, ds1)
            dw_ref[2 * gp] += jnp.where(mask, dw0, 0)
            dw_ref[2 * gp + 1] += jnp.where(mask, dw1, 0)
            dbx_ref[:, cols] += dbx
        dvln = dvln_ref[...]
        dlng_ref[...] += jnp.sum(dvln * vhat, axis=0, keepdims=True)
        dlnb_ref[...] += jnp.sum(dvln, axis=0, keepdims=True)
        dvh = dvln * lng
        dgv = rstd * (dvh - _group_sum64(dvh) * (1.0 / 64) - vhat * (_group_sum64(dvh * vhat) * (1.0 / 64)))
        duv_ref[:, SG_W:2 * SG_W] = (dgv * dgelu_v).astype(BF16)

        @pl.when(i == n_steps - 1)
        def _():
            dbx_ref[...] = _group_sum64(dbx_ref[...])

    return _call(
        body, name="sgu_bwd", grid=(n_steps,),
        out_shape=(jax.ShapeDtypeStruct((S, 2 * SG_W), BF16), jax.ShapeDtypeStruct((8, 128, 128), F32),
                   jax.ShapeDtypeStruct((128, SG_W), F32), jax.ShapeDtypeStruct((1, SG_W), F32),
                   jax.ShapeDtypeStruct((1, SG_W), F32)),
        in_specs=[_rows(tm, 2 * SG_W), _rows(tm, SG_W), _resident((1, SG_W)), _resident((1, SG_W)),
                  _resident((8, 128, 128)), _resident((128, SG_W))],
        out_specs=(_rows(tm, 2 * SG_W), pl.BlockSpec((8, 128, 128), lambda i: (0, 0, 0)),
                   pl.BlockSpec((128, SG_W), lambda i: (0, 0)), pl.BlockSpec((1, SG_W), lambda i: (0, 0)),
                   pl.BlockSpec((1, SG_W), lambda i: (0, 0))),
        scratch_shapes=[pltpu.VMEM((tm, SG_W), F32)],
        operands=(uv, dy_sg, lng, lnb, sg_w, b_exp), plans=plans)


def _attn_bwd(qkv, dy_att, y_att, lse, bias, plans=()):
    S = qkv.shape[0]
    R = ATT_R

    def body(q_ref, kp_ref, kc_ref, vp_ref, vc_ref, b_ref, dy_ref, y_ref, lse_ref, dq_ref, dk_ref, dv_ref, dss_ref):
        i = pl.program_id(1)

        @pl.when(i == 0)
        def _():
            dk_ref[...] = jnp.zeros_like(dk_ref)
            dv_ref[...] = jnp.zeros_like(dv_ref)
            dss_ref[...] = jnp.zeros_like(dss_ref)

        lane = lax.broadcasted_iota(jnp.int32, (1, LANES), 1)
        kwin = jnp.concatenate([kp_ref[...], kc_ref[...]], axis=0)
        vwin = jnp.concatenate([vp_ref[...], vc_ref[...]], axis=0)
        for sub in range(ATT_SUBS):
            rows = slice(128 * sub, 128 * (sub + 1))
            q2 = q_ref[rows, :]
            do2 = dy_ref[rows, :]
            kw = kwin[128 * sub:128 * sub + BAND_KEYS]
            vw = vwin[128 * sub:128 * sub + BAND_KEYS]
            table = jnp.where(i == 0, 1 + sub, 0)
            dyy = do2.astype(F32) * y_ref[rows, :].astype(F32)
            lse2 = lse_ref[rows, :]
            dq = jnp.zeros((128, LANES), F32)
            dkw = jnp.zeros((BAND_KEYS, LANES), F32)
            dvw = jnp.zeros((BAND_KEYS, LANES), F32)
            for hh in range(2):
                mine = (lane >= 64 * hh) & (lane < 64 * (hh + 1))
                qm = jnp.where(mine, q2, 0)
                dom = jnp.where(mine, do2, 0)
                p = jnp.exp(_dot_nt(qm, kw) + b_ref[table, hh] - lse2[:, 64 * hh:64 * hh + 1])
                dp = _dot_nt(dom, vw)
                ds = p * (dp - jnp.sum(jnp.where(mine, dyy, 0.0), axis=-1, keepdims=True))
                dss_ref[hh] += ds
                dsb = ds.astype(BF16)
                dq = dq + _dot(dsb, jnp.where(mine, kw, 0))
                dkw = dkw + _dot_tn(dsb, qm)
                dvw = dvw + _dot_tn(p.astype(BF16), dom)
            dq_ref[rows, :] = dq.astype(BF16)
            start = pl.multiple_of(i * R + 128 * sub, 128)
            dk_ref[pl.ds(start, BAND_KEYS), :] += dkw
            dv_ref[pl.ds(start, BAND_KEYS), :] += dvw

    acc_spec = pl.BlockSpec((S + 8 * CHUNK, LANES), lambda j, i: (0, j))
    return _call(
        body, name="attn_bwd", grid=(4, S // R),
        out_shape=(jax.ShapeDtypeStruct((S, ATT_W), BF16), jax.ShapeDtypeStruct((S + 8 * CHUNK, ATT_W), F32),
                   jax.ShapeDtypeStruct((S + 8 * CHUNK, ATT_W), F32), jax.ShapeDtypeStruct((8, 128, BAND_KEYS), F32)),
        in_specs=_att_specs() + [pl.BlockSpec((R, LANES), lambda j, i: (i, j))] * 3,
        out_specs=(pl.BlockSpec((R, LANES), lambda j, i: (i, j)), acc_spec, acc_spec,
                   pl.BlockSpec((2, 128, BAND_KEYS), lambda j, i: (j, 0, 0))),
        operands=(qkv, qkv, qkv, qkv, qkv, bias, dy_att, y_att, lse), plans=plans)


def _in_bwd(dq, dk, dv, duv, dgates, x, dx1, g1, w_in_t, tm=256):
    S = x.shape[0]
    pad_blocks = (8 * CHUNK) // tm

    def body(dq_ref, dk_ref, dv_ref, duv_ref, dg_ref, x_ref, d1_ref, g_ref, wt_ref, dx_ref, dz_ref, dg1_ref):
        i = pl.program_id(0)
        dz_ref[:, 0:ATT_W] = (dq_ref[...].astype(F32) * 0.125).astype(BF16)
        dz_ref[:, ATT_W:2 * ATT_W] = dk_ref[...].astype(BF16)
        dz_ref[:, 2 * ATT_W:3 * ATT_W] = dv_ref[...].astype(BF16)
        dz_ref[:, 3 * ATT_W:3 * ATT_W + 2 * SG_W] = duv_ref[...]
        dz_ref[:, 3 * ATT_W + 2 * SG_W:IN_COLS] = dg_ref[...]
        dh = jnp.zeros((tm, D), F32)
        for c in range(IN_COLS // 512):
            cs = slice(512 * c, 512 * (c + 1))
            dh = dh + _dot(dz_ref[:, cs], wt_ref[cs, :])
        xv = x_ref[...]
        dx, dg1 = _rms_bwd(dh, xv, _rstd(xv), g_ref[...])
        dx_ref[...] = d1_ref[...] + dx

        @pl.when(i == 0)
        def _():
            dg1_ref[...] = jnp.zeros_like(dg1_ref)

        dg1_ref[...] += dg1

    shifted = pl.BlockSpec((tm, ATT_W), lambda i: (i + pad_blocks, 0))
    return pl.pallas_call(
        body, name="in_bwd", grid=(S // tm,),
        out_shape=(jax.ShapeDtypeStruct((S, D), F32), jax.ShapeDtypeStruct((S, IN_COLS), BF16),
                   jax.ShapeDtypeStruct((1, D), F32)),
        in_specs=[_rows(tm, ATT_W), shifted, shifted, _rows(tm, 2 * SG_W), _rows(tm, 2 * D), _rows(tm, D),
                  _rows(tm, D), _resident((1, D)), _resident((IN_COLS, D))],
        out_specs=(_rows(tm, D), _rows(tm, IN_COLS), pl.BlockSpec((1, D), lambda i: (0, 0))),
        compiler_params=_cparams(1),
    )(dq, dk, dv, duv, dgates, x, dx1, g1, w_in_t)


def _adam_math(w, g, m, v):
    m = ADAM_B1 * m + (1.0 - ADAM_B1) * g
    v = ADAM_B2 * v + (1.0 - ADAM_B2) * (g * g)
    m_hat = m / (1.0 - ADAM_B1 ** ADAM_STEP)
    v_hat = v / (1.0 - ADAM_B2 ** ADAM_STEP)
    delta = -ADAM_LR * (m_hat / (jnp.sqrt(v_hat) + ADAM_EPS) + ADAM_WD * w)
    return delta, m, v


def _adam(parts, w, m, v, tr, name):
    P, R, C = parts.shape

    def body(p_ref, w_ref, m_ref, v_ref, g_ref, d_ref, mo_ref, vo_ref):
        g = p_ref[0].astype(F32)
        for k in range(1, P):
            g = g + p_ref[k].astype(F32)
        g_ref[...] = g
        d_ref[...], mo_ref[...], vo_ref[...] = _adam_math(w_ref[...], g, m_ref[...], v_ref[...])

    row = pl.BlockSpec((tr, C), lambda i: (i, 0))
    return pl.pallas_call(
        body, name=name, grid=(R // tr,),
        out_shape=tuple(jax.ShapeDtypeStruct((R, C), F32) for _ in range(4)),
        in_specs=[pl.BlockSpec((P, tr, C), lambda i: (0, i, 0)), row, row, row],
        out_specs=(row, row, row, row),
        compiler_params=_cparams(1),
    )(parts, w, m, v)


def _my_place():
    return lax.axis_index("x"), lax.axis_index("y"), lax.axis_index("c")


def _other_chips(x, y):
    return [(1 - x, y), (x, 1 - y), (1 - x, 1 - y)]


class _Plan:
    def __init__(self, arrays, out_shapes, sems, start, finish, forward=None, forward_at=0.7):
        self.arrays, self.out_shapes, self.sems = list(arrays), list(out_shapes), list(sems)
        self.start, self.finish, self.forward, self.forward_at = start, finish, forward, forward_at


def _call(body, *, name, grid, in_specs, out_specs, out_shape, operands, scratch_shapes=(), plans=()):
    n_in, n_out, n_scr = len(operands), len(out_shape), len(scratch_shapes)
    p_in = [a for p in plans for a in p.arrays]
    p_out = [s for p in plans for s in p.out_shapes]
    p_sem = [s for p in plans for s in p.sems]
    steps = math.prod(grid)

    def wrapped(*refs):
        ins, refs = refs[:n_in], refs[n_in:]
        pins, refs = refs[:len(p_in)], refs[len(p_in):]
        outs, refs = refs[:n_out], refs[n_out:]
        pouts, refs = refs[:len(p_out)], refs[len(p_out):]
        scr, psems = refs[:n_scr], refs[n_scr:]
        step = 0
        for axis, size in enumerate(grid):
            step = step * size + pl.program_id(axis)
        bound = []
        for p in plans:
            bound.append((p, pins[:len(p.arrays)], pouts[:len(p.out_shapes)], psems[:len(p.sems)]))
            pins, pouts, psems = pins[len(p.arrays):], pouts[len(p.out_shapes):], psems[len(p.sems):]

        @pl.when(step == 0)
        def _():
            for p, a, b, s in bound:
                p.start(a, b, s)

        body(*ins, *outs, *scr)

        for p, a, b, s in bound:
            if p.forward is not None:
                @pl.when(step == min(int(p.forward_at * steps), steps - 1))
                def _(p=p, a=a, b=b, s=s):
                    p.forward(a, b, s)

        @pl.when(step == steps - 1)
        def _():
            for p, a, b, s in bound:
                p.finish(a, b, s)

    hbm = pl.BlockSpec(memory_space=pl.ANY)
    return pl.pallas_call(
        wrapped if plans else body, name=name, grid=grid,
        out_shape=tuple(out_shape) + tuple(p_out),
        in_specs=list(in_specs) + [hbm] * len(p_in),
        out_specs=tuple(out_specs) + tuple([hbm] * len(p_out)),
        scratch_shapes=list(scratch_shapes) + p_sem,
        compiler_params=_cparams(len(grid)),
    )(*operands, *p_in)


def _run_plan(plan, name):
    n_in, n_out = len(plan.arrays), len(plan.out_shapes)

    def body(*refs):
        a, b, s = refs[:n_in], refs[n_in:n_in + n_out], refs[n_in + n_out:]
        plan.start(a, b, s)
        if plan.forward is not None:
            plan.forward(a, b, s)
        plan.finish(a, b, s)

    hbm = pl.BlockSpec(memory_space=pl.ANY)
    return pl.pallas_call(
        body, name=name, out_shape=tuple(plan.out_shapes),
        in_specs=[hbm] * n_in, out_specs=tuple([hbm] * n_out), scratch_shapes=plan.sems,
    )(*plan.arrays)


def _gather_plan(shards, forward_at=0.7):
    n = len(shards)

    def copies(ins, outs, sems):
        send_sems, recv_sems, local_sems = sems
        x, y, c = _my_place()
        me, sibling = (x, y, c), (x, y, 1 - c)
        chips = _other_chips(x, y)

        def idx(p):
            return 4 * p[0] + 2 * p[1] + p[2]

        def copy(a, k, block, to, src=None):
            return pltpu.make_async_remote_copy(
                src_ref=outs[a].at[idx(block)] if src is None else src, dst_ref=outs[a].at[idx(block)],
                send_sem=send_sems.at[a, k], recv_sem=recv_sems.at[a, k], device_id=to, device_id_type=MESH)

        mine = [pltpu.make_async_copy(ins[a], outs[a].at[idx(me)], local_sems.at[a]) for a in range(n)]
        first = []
        for a in range(n):
            first.append(copy(a, 0, me, sibling, src=ins[a]))
            first += [copy(a, 1 + j, me, (*chip, c), src=ins[a]) for j, chip in enumerate(chips)]
        arrived = [copy(a, 1 + j, (*chip, c), me) for j, chip in enumerate(chips) for a in range(n)]
        passed = [copy(a, 4 + j, (*chip, c), sibling) for j, chip in enumerate(chips) for a in range(n)]
        from_sibling = []
        for a in range(n):
            from_sibling.append(copy(a, 0, sibling, me))
            from_sibling += [copy(a, 4 + j, (*chip, 1 - c), me) for j, chip in enumerate(chips)]
        return mine, first, arrived, passed, from_sibling

    def start(ins, outs, sems):
        mine, first, _, _, _ = copies(ins, outs, sems)
        for cp in mine + first:
            cp.start()

    def forward(ins, outs, sems):
        _, _, arrived, passed, _ = copies(ins, outs, sems)
        for landed, onward in zip(arrived, passed):
            landed.wait_recv()
            onward.start()

    def finish(ins, outs, sems):
        mine, first, _, passed, from_sibling = copies(ins, outs, sems)
        for cp in from_sibling:
            cp.wait_recv()
        for cp in first + passed:
            cp.wait_send()
        for cp in mine:
            cp.wait()

    return _Plan(shards, [jax.ShapeDtypeStruct((N_DEV,) + s.shape, s.dtype) for s in shards],
                 [pltpu.SemaphoreType.DMA((n, 7)), pltpu.SemaphoreType.DMA((n, 7)), pltpu.SemaphoreType.DMA((n,))],
                 start, finish, forward, forward_at)


def _sibling_plan(scatter, whole=()):
    ns = len(scatter)
    n = ns + len(whole)

    def copies(ins, outs, sems):
        send_sems, recv_sems = sems
        x, y, c = _my_place()
        out = []
        for a in range(n):
            for k in range(4 if a < ns else 1):
                src = ins[a].at[2 * k + (1 - c)] if a < ns else ins[a]
                dst = outs[a].at[k] if a < ns else outs[a]
                out.append(pltpu.make_async_remote_copy(
                    src_ref=src, dst_ref=dst, send_sem=send_sems.at[a, k], recv_sem=recv_sems.at[a, k],
                    device_id=(x, y, 1 - c), device_id_type=MESH))
        return out

    def start(ins, outs, sems):
        for cp in copies(ins, outs, sems):
            cp.start()

    def finish(ins, outs, sems):
        for cp in copies(ins, outs, sems):
            cp.wait()

    shapes = [jax.ShapeDtypeStruct((4,) + p.shape[1:], p.dtype) for p in scatter]
    shapes += [jax.ShapeDtypeStruct(p.shape, p.dtype) for p in whole]
    return _Plan(list(scatter) + list(whole), shapes,
                 [pltpu.SemaphoreType.DMA((n, 4)), pltpu.SemaphoreType.DMA((n, 4))], start, finish)


def _pair_sum(mine, theirs, c_idx, name):
    _, R, C = mine.shape

    def body(c_ref, a_ref, b_ref, o_ref):
        o_ref[...] = (a_ref[...].astype(F32) + b_ref[...].astype(F32)).astype(o_ref.dtype)

    grid_spec = pltpu.PrefetchScalarGridSpec(
        num_scalar_prefetch=1, grid=(4,),
        in_specs=[pl.BlockSpec((1, R, C), lambda k, c_ref: (2 * k + c_ref[0], 0, 0)),
                  pl.BlockSpec((1, R, C), lambda k, c_ref: (k, 0, 0))],
        out_specs=pl.BlockSpec((1, R, C), lambda k, c_ref: (k, 0, 0)))
    return pl.pallas_call(
        body, name=name, grid_spec=grid_spec,
        out_shape=jax.ShapeDtypeStruct((4, R, C), mine.dtype),
        compiler_params=_cparams(1),
    )(c_idx, mine, theirs)


def _pair_sum_whole(mine, theirs, name):
    n = len(mine)

    def body(*refs):
        for a in range(n):
            refs[2 * n + a][...] = refs[a][...] + refs[n + a][...]

    vm = pl.BlockSpec(memory_space=pltpu.VMEM)
    return pl.pallas_call(
        body, name=name, out_shape=tuple(jax.ShapeDtypeStruct(p.shape, p.dtype) for p in mine),
        in_specs=[vm] * (2 * n), out_specs=tuple([vm] * n),
    )(*mine, *theirs)


def _chips_plan(scatter, whole=()):
    ns = len(scatter)
    n = ns + len(whole)

    def copies(ins, outs, sems):
        send_sems, recv_sems, local_sems = sems
        x, y, c = _my_place()
        my_chip = 2 * x + y

        def src(a, k):
            return ins[a].at[k] if a < ns else ins[a]

        local = [pltpu.make_async_copy(src(a, my_chip), outs[a].at[my_chip], local_sems.at[a]) for a in range(n)]
        remote = []
        for a in range(n):
            for j, (px, py) in enumerate(_other_chips(x, y)):
                remote.append(pltpu.make_async_remote_copy(
                    src_ref=src(a, 2 * px + py), dst_ref=outs[a].at[my_chip],
                    send_sem=send_sems.at[a, j], recv_sem=recv_sems.at[a, j],
                    device_id=(px, py, c), device_id_type=MESH))
        return local + remote

    def start(ins, outs, sems):
        for cp in copies(ins, outs, sems):
            cp.start()

    def finish(ins, outs, sems):
        for cp in copies(ins, outs, sems):
            cp.wait()

    shapes = [jax.ShapeDtypeStruct(s.shape, s.dtype) for s in scatter]
    shapes += [jax.ShapeDtypeStruct((4,) + s.shape, s.dtype) for s in whole]
    return _Plan(list(scatter) + list(whole), shapes,
                 [pltpu.SemaphoreType.DMA((n, 3)), pltpu.SemaphoreType.DMA((n, 3)), pltpu.SemaphoreType.DMA((n,))],
                 start, finish)


def _sum_parts(parts, name):
    _, R, C = parts.shape

    def body(p_ref, o_ref):
        o_ref[...] = ((p_ref[0].astype(F32) + p_ref[1].astype(F32)) + p_ref[2].astype(F32)) + p_ref[3].astype(F32)

    vm = pl.BlockSpec(memory_space=pltpu.VMEM)
    return pl.pallas_call(body, name=name, out_shape=jax.ShapeDtypeStruct((R, C), F32),
                          in_specs=[vm], out_specs=vm,
                          compiler_params=pltpu.CompilerParams(vmem_limit_bytes=VMEM_LIMIT))(parts)


def _small_update(parts, w, m, v, loss_parts, name):
    n = len(parts)

    def body(*refs):
        p_refs, w_refs, m_refs, v_refs = (refs[i * n:(i + 1) * n] for i in range(4))
        lp_ref = refs[4 * n]
        outs = refs[4 * n + 1:]
        g_refs, d_refs, mo_refs, vo_refs = (outs[i * n:(i + 1) * n] for i in range(4))
        for a in range(n):
            g = ((p_refs[a][0] + p_refs[a][1]) + p_refs[a][2]) + p_refs[a][3]
            g_refs[a][...] = g
            d_refs[a][...], mo_refs[a][...], vo_refs[a][...] = _adam_math(w_refs[a][...], g, m_refs[a][...],
                                                                          v_refs[a][...])
        outs[4 * n][...] = ((lp_ref[0] + lp_ref[1]) + lp_ref[2]) + lp_ref[3]

    vm = pl.BlockSpec(memory_space=pltpu.VMEM)
    shapes = [jax.ShapeDtypeStruct(t.shape, F32) for _ in range(4) for t in w]
    shapes.append(jax.ShapeDtypeStruct(loss_parts.shape[1:], F32))
    outs = pl.pallas_call(
        body, name=name, out_shape=tuple(shapes),
        in_specs=[vm] * (4 * n + 1), out_specs=tuple([vm] * (4 * n + 1)),
    )(*parts, *w, *m, *v, loss_parts)
    return outs[0:n], outs[n:2 * n], outs[2 * n:3 * n], outs[3 * n:4 * n], outs[4 * n]


BIG = [("w_in", 1), ("w_branch_att", 1), ("w_branch_sg", 1), ("w_out", 0), ("w_xq", 0), ("w_xkv", 1), ("w_xo", 0),
       ("w_ffn_in", 1), ("w_ffn_out", 0)]
SMALL = [("norm_mix_g", (1, D)), ("rel_bias", (8, NREL)), ("sg_ln_g", (8, 64)), ("sg_ln_b", (8, 64)),
         ("sg_w", (8, 128, 128)), ("sg_b", (8, 128)), ("norm_xattn_g", (1, D)), ("norm_mem_g", (1, D)),
         ("norm_ffn_g", (1, D)), ("norm_final_g", (1, D))]
ADAM_ROWS = {"w_in": 256, "w_branch_att": 512, "w_branch_sg": 512, "w_xkv": 1024, "w_ffn_in": 256}


def _full(gathered):
    return gathered.reshape(N_DEV * gathered.shape[1], gathered.shape[2])


def _blocks(grad):
    return grad.reshape(N_DEV, grad.shape[0] // N_DEV, grad.shape[1])


def kernel(x, mem, norm_mix_g, w_in, rel_bias, sg_ln_g, sg_ln_b, sg_w, sg_b, w_branch_att, w_branch_sg, w_out, norm_xattn_g, norm_mem_g, w_xq, w_xkv, w_xo, norm_ffn_g, w_ffn_in, w_ffn_out, norm_final_g, loss_target, m_norm_mix_g, m_w_in, m_rel_bias, m_sg_ln_g, m_sg_ln_b, m_sg_w, m_sg_b, m_w_branch_att, m_w_branch_sg, m_w_out, m_norm_xattn_g, m_norm_mem_g, m_w_xq, m_w_xkv, m_w_xo, m_norm_ffn_g, m_w_ffn_in, m_w_ffn_out, m_norm_final_g, v_norm_mix_g, v_w_in, v_rel_bias, v_sg_ln_g, v_sg_ln_b, v_sg_w, v_sg_b, v_w_branch_att, v_w_branch_sg, v_w_out, v_norm_xattn_g, v_norm_mem_g, v_w_xq, v_w_xkv, v_w_xo, v_norm_ffn_g, v_w_ffn_in, v_w_ffn_out, v_norm_final_g):
    args = dict(locals())
    big_names = [n for n, _ in BIG]
    small_names = [n for n, _ in SMALL]
    S = x.shape[1]

    x, mem, tgt = x.reshape(S, D), mem.reshape(MEM, D), loss_target.reshape(S, D)
    small = {n: args[n].reshape(shape) for n, shape in SMALL}
    g1, g2, g3 = small["norm_mix_g"], small["norm_xattn_g"], small["norm_ffn_g"]
    g_mem, g4 = small["norm_mem_g"], small["norm_final_g"]
    lng = small["sg_ln_g"].reshape(1, SG_W)
    lnb = small["sg_ln_b"].reshape(1, SG_W)
    b_exp = jnp.broadcast_to(small["sg_b"].T[:, :, None], (128, 8, 64)).reshape(128, SG_W)
    rel_pad = jnp.pad(small["rel_bias"], ((0, 0), (0, 384 - NREL)))
    c_idx = lax.axis_index("c").astype(jnp.int32).reshape(1)

    shard = {n: (args[n][0].T if axis == 1 else args[n][0]).astype(BF16) for n, axis in BIG}
    w_in_t = _full(_run_plan(_gather_plan([shard["w_in"]]), "ag_w_in")[0])
    bias = _bias_table(rel_pad)
    mix_names = ["w_branch_att", "w_branch_sg", "w_out", "w_xq", "w_xkv", "w_xo"]
    qkv, uv, gates, h, *got = _in_proj(x, g1, w_in_t, plans=[_gather_plan([shard[n] for n in mix_names])])
    wba_t, wbs_t, w_out_f, w_xq_f, w_xkv_t, w_xo_f = (_full(g) for g in got)
    y_att, lse, *got = _attn_fwd(qkv, bias, plans=[_gather_plan([shard["w_ffn_in"], shard["w_ffn_out"]])])
    w_ffn_in_t, w_ffn_out_f = (_full(g) for g in got)
    y_sg = _sgu_fwd(uv, lng, lnb, small["sg_w"], b_exp)
    x1 = _merge_fwd(x, y_att, y_sg, gates, wba_t, wbs_t, w_out_f)
    kv, mn = _mem_kv(mem, g_mem, w_xkv_t)
    x2 = _xattn_fwd(x1, g2, w_xq_f, kv, w_xo_f)
    dx3, gu, hf, act, loss_part, dg4 = _ffn_fwd(x2, tgt, g3, w_ffn_in_t, w_ffn_out_f, g4)

    def pair_sums(names, mine, theirs):
        return [_pair_sum(a, b, c_idx, "rs_pair_" + n) for n, a, b in zip(names, mine, theirs)]

    dx2, dgu, dg3 = _ffn_bwd(dx3, gu, x2, g3, w_ffn_out_f, w_ffn_in_t)
    ffn_names = ["w_ffn_out", "w_ffn_in"]
    ffn_mine = [_blocks(_dw(act, dx3, 1408, 1024, 512, "dw_ffn_out")),
                _blocks(_dw(dgu, hf, 1408, 1024, 512, "dw_ffn_in"))]
    dx1, o_x, dq_x, hx, dkv, dg2, *ffn_theirs = _xattn_bwd(dx2, x1, g2, w_xq_f, w_xo_f, kv,
                                                           plans=[_sibling_plan(ffn_mine)])
    ffn_chip = pair_sums(ffn_names, ffn_mine, ffn_theirs)
    d_xkv, dg_mem = _mem_kv_bwd(dkv, mem, g_mem, mn, w_xkv_t)
    merged, d_a, d_b, dy_att, dy_sg, dgates = _merge_bwd(dx1, y_att, y_sg, gates, wba_t, wbs_t, w_out_f)
    mid_names = ["w_xo", "w_xq", "w_xkv", "w_out", "w_branch_att", "w_branch_sg"]
    mid_mine = [_blocks(g) for g in (
        _dw(o_x, dx2, 1024, 1024, 512, "dw_xo"), _dw(hx, dq_x, 1024, 1024, 512, "dw_xq"), d_xkv,
        _dw(merged, dx1, 1024, 1024, 512, "dw_out"), _dw(d_a, y_att, 1024, 512, 512, "dw_branch_att"),
        _dw(d_b, y_sg, 1024, 512, 512, "dw_branch_sg"))]
    duv, d_sgw, d_bx, d_lng, d_lnb, *got = _sgu_bwd(uv, dy_sg, lng, lnb, small["sg_w"], b_exp,
                                                    plans=[_chips_plan(ffn_chip), _sibling_plan(mid_mine)])
    ffn_all, mid_theirs = got[:2], got[2:]
    mid_chip = pair_sums(mid_names, mid_mine, mid_theirs)
    dq, dk, dv, ds_sum, *mid_all = _attn_bwd(qkv, dy_att, y_att, lse, bias, plans=[_chips_plan(mid_chip)])
    d_rel = _bias_grad(ds_sum)
    grad_x, dz, dg1 = _in_bwd(dq, dk, dv, duv, dgates, x, dx1, g1, w_in_t)
    in_mine = [_blocks(_dw(dz, h, 1152, 1024, 512, "dw_in"))]

    gs = {"norm_mix_g": dg1, "rel_bias": d_rel[:, :NREL], "sg_ln_g": d_lng.reshape(8, 64),
          "sg_ln_b": d_lnb.reshape(8, 64), "sg_w": d_sgw, "sg_b": d_bx.reshape(128, 8, 64)[:, :, 0].T,
          "norm_xattn_g": dg2, "norm_mem_g": dg_mem, "norm_ffn_g": dg3, "norm_final_g": dg4}
    whole = [gs[n] for n in small_names] + [loss_part]
    in_theirs, *sib_whole = _run_plan(_sibling_plan(in_mine, whole), "rs_sibling")
    in_chip = pair_sums(["w_in"], in_mine, [in_theirs])
    chip_whole = _pair_sum_whole(whole, sib_whole, "rs_pair_small")
    in_all, *all_whole = _run_plan(_chips_plan(in_chip, chip_whole), "rs_chips")
    all_parts = dict(zip(ffn_names + mid_names + ["w_in"], list(ffn_all) + list(mid_all) + [in_all]))

    res = {}
    for n, axis in BIG:
        parts = all_parts[n]
        wmv = [args[p + n][0] for p in ("", "m_", "v_")]
        if axis == 1:
            parts = _sum_parts(parts, "rs_sum_" + n).T[None]
        tr = ADAM_ROWS[n] if axis == 1 else wmv[0].shape[0]
        res[n] = [t[None] for t in _adam(parts, *wmv, tr, "adam_" + n)]
    small_res = _small_update(
        all_whole[:-1], [small[n] for n in small_names], [args["m_" + n].reshape(s) for n, s in SMALL],
        [args["v_" + n].reshape(s) for n, s in SMALL], all_whole[-1], "adam_small")
    for i, n in enumerate(small_names):
        res[n] = [small_res[k][i].reshape(args[n].shape) for k in range(4)]
    loss = small_res[4][0, 0]

    order = ["norm_mix_g", "w_in", "rel_bias", "sg_ln_g", "sg_ln_b", "sg_w", "sg_b", "w_branch_att", "w_branch_sg",
             "w_out", "norm_xattn_g", "norm_mem_g", "w_xq", "w_xkv", "w_xo", "norm_ffn_g", "w_ffn_in", "w_ffn_out",
             "norm_final_g"]
    outs = [loss, grad_x.reshape(1, S, D)]
    for k in range(4):
        outs += [res[n][k] for n in order]
    return tuple(outs)
```

```python
import math

import jax
import jax.numpy as jnp
from jax import lax
from jax.experimental import pallas as pl
from jax.experimental.pallas import tpu as pltpu

F32 = jnp.float32
BF16 = jnp.bfloat16

D = 1024
ATT_W = 512
SG_W = 512
IN_COLS = 4608
DFF = 2816
MEM = 256
XH = 4
CHUNK = 64
BAND_KEYS = 640
ATT_R = 512
ATT_SUBS = ATT_R // 128
DW_TOKENS = 1024
REL_CLIP = 128
NREL = 2 * REL_CLIP + 1
EPS = 1e-6
NEG = -1e30
N_DEV = 8

ADAM_LR = 0.001
ADAM_B1 = 0.9
ADAM_B2 = 0.999
ADAM_EPS = 1e-08
ADAM_WD = 0.01
ADAM_STEP = 10

LANES = 128
VMEM_LIMIT = 56 * 1024 * 1024
MESH = pl.DeviceIdType.MESH


def _cparams(n_axes):
    return pltpu.CompilerParams(dimension_semantics=("arbitrary",) * n_axes, vmem_limit_bytes=VMEM_LIMIT)


def _resident(shape):
    zeros = (0,) * len(shape)
    return pl.BlockSpec(shape, lambda *_: zeros, pipeline_mode=pl.Buffered(1))


def _rows(tm, cols, col_block=0):
    return pl.BlockSpec((tm, cols), lambda i: (i, col_block))


def _sigmoid(x):
    return 1.0 / (1.0 + jnp.exp(-x))


_GELU_C = math.sqrt(2.0 / math.pi)


def _gelu(x):
    t = jnp.tanh(_GELU_C * (x + 0.044715 * (x * x * x)))
    return x * (0.5 * (1.0 + t))


def _gelu_and_grad(x):
    x2 = x * x
    t = jnp.tanh(_GELU_C * (x + 0.044715 * (x2 * x)))
    cdf = 0.5 * (1.0 + t)
    dcdf = 0.5 * (1.0 - t * t) * (_GELU_C * (1.0 + 3.0 * 0.044715 * x2))
    return x * cdf, cdf + x * dcdf


def _rstd(x):
    return lax.rsqrt(jnp.mean(x * x, axis=-1, keepdims=True) + EPS)


def _rms_bwd(dh, x, r, g):
    xh = x * r
    dxh = dh * g
    dx = r * (dxh - xh * jnp.mean(dxh * xh, axis=-1, keepdims=True))
    dg = jnp.sum(dh * xh, axis=0, keepdims=True)
    return dx, dg


def _group_sum64(x):
    r = lax.broadcasted_iota(jnp.int32, (LANES, LANES), 0) // 64
    c = lax.broadcasted_iota(jnp.int32, (LANES, LANES), 1) // 64
    same_group = (r == c).astype(BF16)

    def one(v):
        hi = v.astype(BF16)
        rest = v - hi.astype(F32)
        mid = rest.astype(BF16)
        lo = (rest - mid.astype(F32)).astype(BF16)
        return _dot(hi, same_group) + _dot(mid, same_group) + _dot(lo, same_group)

    pieces = [one(x[:, LANES * j:LANES * (j + 1)]) for j in range(x.shape[1] // LANES)]
    return pieces[0] if len(pieces) == 1 else jnp.concatenate(pieces, axis=1)


def _dot(a, b):
    return jnp.dot(a, b, preferred_element_type=F32)


def _dot_nt(a, b):
    return lax.dot_general(a, b, (((1,), (1,)), ((), ())), preferred_element_type=F32)


def _dot_tn(a, b):
    return lax.dot_general(a, b, (((0,), (0,)), ((), ())), preferred_element_type=F32)


DIAGS = 768


def _diag_onehot():
    r_idx = lax.broadcasted_iota(jnp.int32, (384, DIAGS), 0)
    t_idx = lax.broadcasted_iota(jnp.int32, (384, DIAGS), 1)
    dist = (8 * CHUNK + 127) - t_idx
    return (jnp.clip(dist, -REL_CLIP, REL_CLIP) + REL_CLIP == r_idx).astype(F32)


def _shift_rows(x, reverse):
    row = lax.broadcasted_iota(jnp.int32, x.shape, 0)
    for k in range(7):
        amt = (DIAGS - (1 << k)) if reverse else (1 << k)
        x = jnp.where(((row >> k) & 1) == 1, pltpu.roll(x, amt, 1), x)
    return x


N_TABLES = 1 + ATT_SUBS


def _bias_table(rel_bias_pad):
    def body(rb_ref, out_ref):
        per_diag = jnp.dot(rb_ref[...], _diag_onehot(), preferred_element_type=F32,
                           precision=lax.Precision.HIGHEST)
        a = lax.broadcasted_iota(jnp.int32, (128, BAND_KEYS), 0)
        b = lax.broadcasted_iota(jnp.int32, (128, BAND_KEYS), 1)
        band = (b // CHUNK >= a // CHUNK) & (b // CHUNK <= a // CHUNK + 8)
        for h in range(8):
            rows = jnp.broadcast_to(per_diag[h:h + 1, :], (128, DIAGS))
            table = _shift_rows(pltpu.roll(rows, DIAGS - 127, 1), reverse=False)[:, :BAND_KEYS]
            out_ref[0, h] = jnp.where(band, table, NEG)
            for s in range(ATT_SUBS):
                out_ref[1 + s, h] = jnp.where(band & (b >= 8 * CHUNK - 128 * s), table, NEG)

    return pl.pallas_call(
        body, name="bias_table",
        out_shape=jax.ShapeDtypeStruct((N_TABLES, 8, 128, BAND_KEYS), F32),
        in_specs=[pl.BlockSpec(memory_space=pltpu.VMEM)],
        out_specs=pl.BlockSpec(memory_space=pltpu.VMEM),
        compiler_params=pltpu.CompilerParams(vmem_limit_bytes=VMEM_LIMIT),
    )(rel_bias_pad)


def _bias_grad(ds_sum):
    def body(ds_ref, out_ref):
        sums = []
        for h in range(8):
            padded = jnp.concatenate([ds_ref[h], jnp.zeros((128, DIAGS - BAND_KEYS), F32)], axis=1)
            skewed = pltpu.roll(_shift_rows(padded, reverse=True), 127, 1)
            sums.append(jnp.sum(skewed, axis=0, keepdims=True))
        per_diag = jnp.concatenate(sums, axis=0)
        out_ref[...] = lax.dot_general(per_diag, _diag_onehot(), (((1,), (1,)), ((), ())),
                                       preferred_element_type=F32, precision=lax.Precision.HIGHEST)

    return pl.pallas_call(
        body, name="bias_grad",
        out_shape=jax.ShapeDtypeStruct((8, 384), F32),
        in_specs=[pl.BlockSpec(memory_space=pltpu.VMEM)],
        out_specs=pl.BlockSpec(memory_space=pltpu.VMEM),
    )(ds_sum)


def _in_proj(x, g1, w_in_t, plans=(), tm=512):
    S = x.shape[0]

    def body(x_ref, g_ref, w_ref, qkv_ref, uv_ref, gate_ref, h_ref):
        xv = x_ref[...]
        h = (xv * _rstd(xv) * g_ref[...]).astype(BF16)
        h_ref[...] = h
        for c in range(IN_COLS // 512):
            zc = _dot_nt(h, w_ref[512 * c:512 * (c + 1), :])
            if c == 0:
                qkv_ref[:, 0:512] = (zc * 0.125).astype(BF16)
            elif c < 3:
                qkv_ref[:, 512 * c:512 * (c + 1)] = zc.astype(BF16)
            elif c < 5:
                uv_ref[:, 512 * (c - 3):512 * (c - 2)] = zc.astype(BF16)
            else:
                gate_ref[:, 512 * (c - 5):512 * (c - 4)] = zc.astype(BF16)

    return _call(
        body, name="in_proj", grid=(S // tm,),
        out_shape=(jax.ShapeDtypeStruct((S, 3 * ATT_W), BF16), jax.ShapeDtypeStruct((S, 2 * SG_W), BF16),
                   jax.ShapeDtypeStruct((S, 2 * D), BF16), jax.ShapeDtypeStruct((S, D), BF16)),
        in_specs=[_rows(tm, D), _resident((1, D)), _resident((IN_COLS, D))],
        out_specs=(_rows(tm, 3 * ATT_W), _rows(tm, 2 * SG_W), _rows(tm, 2 * D), _rows(tm, D)),
        operands=(x, g1, w_in_t), plans=plans)


def _att_specs():
    R = ATT_R
    q = pl.BlockSpec((R, LANES), lambda j, i: (i, j))
    kp = pl.BlockSpec((R, LANES), lambda j, i: (jnp.maximum(i - 1, 0), 4 + j))
    kc = pl.BlockSpec((R, LANES), lambda j, i: (i, 4 + j))
    vp = pl.BlockSpec((R, LANES), lambda j, i: (jnp.maximum(i - 1, 0), 8 + j))
    vc = pl.BlockSpec((R, LANES), lambda j, i: (i, 8 + j))
    bias = pl.BlockSpec((N_TABLES, 2, 128, BAND_KEYS), lambda j, i: (0, j, 0, 0))
    return [q, kp, kc, vp, vc, bias]


def _attn_fwd(qkv, bias, plans=()):
    S = qkv.shape[0]
    R = ATT_R

    def body(q_ref, kp_ref, kc_ref, vp_ref, vc_ref, b_ref, o_ref, lse_ref):
        i = pl.program_id(1)
        lane = lax.broadcasted_iota(jnp.int32, (1, LANES), 1)
        kwin = jnp.concatenate([kp_ref[...], kc_ref[...]], axis=0)
        vwin = jnp.concatenate([vp_ref[...], vc_ref[...]], axis=0)
        for sub in range(ATT_SUBS):
            q2 = q_ref[128 * sub:128 * (sub + 1), :]
            kw = kwin[128 * sub:128 * sub + BAND_KEYS]
            vw = vwin[128 * sub:128 * sub + BAND_KEYS]
            table = jnp.where(i == 0, 1 + sub, 0)
            o = jnp.zeros((128, LANES), F32)
            lse = jnp.zeros((128, LANES), F32)
            for hh in range(2):
                mine = (lane >= 64 * hh) & (lane < 64 * (hh + 1))
                s = _dot_nt(jnp.where(mine, q2, 0), kw) + b_ref[table, hh]
                top = jnp.max(s, axis=-1, keepdims=True)
                p = jnp.exp(s - top).astype(BF16)
                pv = _dot(p, jnp.where(mine, vw, 1))
                total = pltpu.roll(pv, 64, 1)
                o = o + jnp.where(mine, pv / total, 0.0)
                lse = jnp.where(mine, top + jnp.log(total), lse)
            o_ref[128 * sub:128 * (sub + 1), :] = o.astype(BF16)
            lse_ref[128 * sub:128 * (sub + 1), :] = lse

    blk = pl.BlockSpec((R, LANES), lambda j, i: (i, j))
    return _call(
        body, name="attn_fwd", grid=(4, S // R),
        out_shape=(jax.ShapeDtypeStruct((S, ATT_W), BF16), jax.ShapeDtypeStruct((S, ATT_W), F32)),
        in_specs=_att_specs(), out_specs=(blk, blk),
        operands=(qkv, qkv, qkv, qkv, qkv, bias), plans=plans)


def _sg_mask():
    t = lax.broadcasted_iota(jnp.int32, (128, 128), 0)
    s = lax.broadcasted_iota(jnp.int32, (128, 128), 1)
    return (s // CHUNK) <= (t // CHUNK)


def _sg_layernorm(gv, lng, lnb):
    mu = _group_sum64(gv) * (1.0 / 64)
    xc = gv - mu
    var = _group_sum64(xc * xc) * (1.0 / 64)
    rstd = lax.rsqrt(var + EPS)
    vhat = xc * rstd
    return vhat * lng + lnb, vhat, rstd


def _sgu_fwd(uv, lng, lnb, sg_w, b_exp, tm=512):
    S = uv.shape[0]

    def body(uv_ref, lng_ref, lnb_ref, w_ref, b_ref, y_ref):
        lane = lax.broadcasted_iota(jnp.int32, (1, LANES), 1)
        lo = lane < 64
        mask = _sg_mask()
        gu = _gelu(uv_ref[:, 0:SG_W].astype(F32))
        vln, _, _ = _sg_layernorm(_gelu(uv_ref[:, SG_W:2 * SG_W].astype(F32)), lng_ref[...], lnb_ref[...])
        for gp in range(4):
            w0 = jnp.where(mask, w_ref[2 * gp], 0).astype(BF16)
            w1 = jnp.where(mask, w_ref[2 * gp + 1], 0).astype(BF16)
            cols = slice(128 * gp, 128 * (gp + 1))
            for n in range(tm // 128):
                rows = slice(128 * n, 128 * (n + 1))
                vl = vln[rows, cols]
                sv = (_dot(w0, jnp.where(lo, vl, 0).astype(BF16)) + _dot(w1, jnp.where(lo, 0, vl).astype(BF16))
                      + b_ref[:, cols])
                y_ref[rows, cols] = (gu[rows, cols] * sv).astype(BF16)

    return pl.pallas_call(
        body, name="sgu_fwd", grid=(S // tm,),
        out_shape=jax.ShapeDtypeStruct((S, SG_W), BF16),
        in_specs=[_rows(tm, 2 * SG_W), _resident((1, SG_W)), _resident((1, SG_W)),
                  _resident((8, 128, 128)), _resident((128, SG_W))],
        out_specs=_rows(tm, SG_W),
        compiler_params=_cparams(1),
    )(uv, lng, lnb, sg_w, b_exp)


def _merge_fwd(x, y_att, y_sg, gates, wba_t, wbs_t, w_out, tm=512):
    S = x.shape[0]

    def body(x_ref, ya_ref, ys_ref, g_ref, wba_ref, wbs_ref, wo_ref, x1_ref):
        a = _dot_nt(ya_ref[...], wba_ref[...])
        b = _dot_nt(ys_ref[...], wbs_ref[...])
        merged = _sigmoid(g_ref[:, 0:D].astype(F32)) * a + _sigmoid(g_ref[:, D:2 * D].astype(F32)) * b
        x1_ref[...] = x_ref[...] + _dot(merged.astype(BF16), wo_ref[...])

    return pl.pallas_call(
        body, name="merge_fwd", grid=(S // tm,),
        out_shape=jax.ShapeDtypeStruct((S, D), F32),
        in_specs=[_rows(tm, D), _rows(tm, ATT_W), _rows(tm, SG_W), _rows(tm, 2 * D),
                  _resident((D, ATT_W)), _resident((D, SG_W)), _resident((D, D))],
        out_specs=_rows(tm, D),
        compiler_params=_cparams(1),
    )(x, y_att, y_sg, gates, wba_t, wbs_t, w_out)


def _mem_kv(mem, g_mem, w_xkv_t):
    def body(m_ref, g_ref, w_ref, kv_ref, mn_ref):
        mv = m_ref[...]
        mn = (mv * _rstd(mv) * g_ref[...]).astype(BF16)
        mn_ref[...] = mn
        kv_ref[...] = _dot_nt(mn, w_ref[...]).astype(BF16)

    vm = pl.BlockSpec(memory_space=pltpu.VMEM)
    return pl.pallas_call(
        body, name="mem_kv",
        out_shape=(jax.ShapeDtypeStruct((MEM, 2 * D), BF16), jax.ShapeDtypeStruct((MEM, D), BF16)),
        in_specs=[vm, vm, vm], out_specs=(vm, vm),
        compiler_params=pltpu.CompilerParams(vmem_limit_bytes=VMEM_LIMIT),
    )(mem, g_mem, w_xkv_t)


def _xatt_head(qx, kv_ref, h):
    hs = slice(256 * h, 256 * (h + 1))
    s = _dot_nt(qx[:, hs], kv_ref[:, hs])
    p = jnp.exp(s - jnp.max(s, axis=-1, keepdims=True))
    return p / jnp.sum(p, axis=-1, keepdims=True)


def _xattn_fwd(x1, g2, w_xq, kv, w_xo, tm=512):
    S = x1.shape[0]

    def body(x_ref, g_ref, wq_ref, kv_ref, wo_ref, x2_ref):
        xv = x_ref[...]
        hx = (xv * _rstd(xv) * g_ref[...]).astype(BF16)
        qx = (_dot(hx, wq_ref[...]) * (1.0 / 16)).astype(BF16)
        outs = []
        for h in range(XH):
            p = _xatt_head(qx, kv_ref, h)
            outs.append(_dot(p.astype(BF16), kv_ref[:, D + 256 * h:D + 256 * (h + 1)]).astype(BF16))
        o = jnp.concatenate(outs, axis=1)
        x2_ref[...] = xv + _dot(o, wo_ref[...])

    return pl.pallas_call(
        body, name="xattn_fwd", grid=(S // tm,),
        out_shape=jax.ShapeDtypeStruct((S, D), F32),
        in_specs=[_rows(tm, D), _resident((1, D)), _resident((D, D)), _resident((MEM, 2 * D)), _resident((D, D))],
        out_specs=_rows(tm, D),
        compiler_params=_cparams(1),
    )(x1, g2, w_xq, kv, w_xo)


FF_CHUNK = 1408


def _ffn_fwd(x2, tgt, g3, w_ffn_in_t, w_ffn_out, g4, tm=256):
    S = x2.shape[0]

    def body(x_ref, t_ref, g3_ref, wi_ref, wo_ref, g4_ref, dx3_ref, gu_ref, hf_ref, act_ref, loss_ref, dg4_ref):
        i = pl.program_id(0)
        xv = x_ref[...]
        hf = (xv * _rstd(xv) * g3_ref[...]).astype(BF16)
        hf_ref[...] = hf
        acc = xv
        for c in range(DFF // FF_CHUNK):
            cs = slice(FF_CHUNK * c, FF_CHUNK * (c + 1))
            us = slice(DFF + FF_CHUNK * c, DFF + FF_CHUNK * (c + 1))
            gate = _dot_nt(hf, wi_ref[cs, :])
            up = _dot_nt(hf, wi_ref[us, :])
            gu_ref[:, cs] = gate.astype(BF16)
            gu_ref[:, us] = up.astype(BF16)
            act = ((gate * _sigmoid(gate)) * up).astype(BF16)
            act_ref[:, cs] = act
            acc = acc + _dot(act, wo_ref[cs, :])
        r4 = _rstd(acc)
        g4 = g4_ref[...]
        diff = acc * r4 * g4 - t_ref[...]
        dy = diff * (1.0 / D)
        dx3, dg4 = _rms_bwd(dy, acc, r4, g4)
        dx3_ref[...] = dx3
        part = 0.5 * jnp.sum(jnp.mean(diff * diff, axis=-1, keepdims=True))

        @pl.when(i == 0)
        def _():
            loss_ref[...] = jnp.zeros_like(loss_ref)
            dg4_ref[...] = jnp.zeros_like(dg4_ref)

        loss_ref[...] += jnp.full(loss_ref.shape, part, F32)
        dg4_ref[...] += dg4

    return pl.pallas_call(
        body, name="ffn_fwd", grid=(S // tm,),
        out_shape=(jax.ShapeDtypeStruct((S, D), F32), jax.ShapeDtypeStruct((S, 2 * DFF), BF16),
                   jax.ShapeDtypeStruct((S, D), BF16), jax.ShapeDtypeStruct((S, DFF), BF16),
                   jax.ShapeDtypeStruct((8, LANES), F32), jax.ShapeDtypeStruct((1, D), F32)),
        in_specs=[_rows(tm, D), _rows(tm, D), _resident((1, D)), _resident((2 * DFF, D)), _resident((DFF, D)),
                  _resident((1, D))],
        out_specs=(_rows(tm, D), _rows(tm, 2 * DFF), _rows(tm, D), _rows(tm, DFF),
                   pl.BlockSpec((8, LANES), lambda i: (0, 0)), pl.BlockSpec((1, D), lambda i: (0, 0))),
        compiler_params=_cparams(1),
    )(x2, tgt, g3, w_ffn_in_t, w_ffn_out, g4)


def _ffn_bwd(dx3, gu, x2, g3, w_ffn_out, w_ffn_in_t, tm=256):
    S = x2.shape[0]

    def body(d_ref, gu_ref, x_ref, g3_ref, wo_ref, wit_ref, dx2_ref, dgu_ref, dg3_ref):
        i = pl.program_id(0)
        d3 = d_ref[...]
        d3b = d3.astype(BF16)
        for c in range(DFF // FF_CHUNK):
            cs = slice(FF_CHUNK * c, FF_CHUNK * (c + 1))
            us = slice(DFF + FF_CHUNK * c, DFF + FF_CHUNK * (c + 1))
            da = _dot_nt(d3b, wo_ref[cs, :])
            gate = gu_ref[:, cs].astype(F32)
            up = gu_ref[:, us].astype(F32)
            sg = _sigmoid(gate)
            dgate = (da * up * (sg * (1.0 + gate * (1.0 - sg)))).astype(BF16)
            dup = (da * (gate * sg)).astype(BF16)
            dgu_ref[:, cs] = dgate
            dgu_ref[:, us] = dup
        dhf = _dot(dgu_ref[...], wit_ref[...])
        xv = x_ref[...]
        dx, dg3 = _rms_bwd(dhf, xv, _rstd(xv), g3_ref[...])
        dx2_ref[...] = d3 + dx

        @pl.when(i == 0)
        def _():
            dg3_ref[...] = jnp.zeros_like(dg3_ref)

        dg3_ref[...] += dg3

    return pl.pallas_call(
        body, name="ffn_bwd", grid=(S // tm,),
        out_shape=(jax.ShapeDtypeStruct((S, D), F32), jax.ShapeDtypeStruct((S, 2 * DFF), BF16),
                   jax.ShapeDtypeStruct((1, D), F32)),
        in_specs=[_rows(tm, D), _rows(tm, 2 * DFF), _rows(tm, D), _resident((1, D)),
                  _resident((DFF, D)), _resident((2 * DFF, D))],
        out_specs=(_rows(tm, D), _rows(tm, 2 * DFF), pl.BlockSpec((1, D), lambda i: (0, 0))),
        compiler_params=_cparams(1),
    )(dx3, gu, x2, g3, w_ffn_out, w_ffn_in_t)


def _dw(a, b, tmm, tn, ts, name, out_dtype=BF16):
    S, M = a.shape
    N = b.shape[1]
    nk = S // ts

    def body(a_ref, b_ref, o_ref, acc_ref):
        k = pl.program_id(2)

        @pl.when(k == 0)
        def _():
            acc_ref[...] = jnp.zeros_like(acc_ref)

        acc_ref[...] += _dot_tn(a_ref[...].astype(BF16), b_ref[...].astype(BF16))

        @pl.when(k == nk - 1)
        def _():
            o_ref[...] = acc_ref[...].astype(out_dtype)

    return pl.pallas_call(
        body, name=name, grid=(M // tmm, N // tn, nk),
        out_shape=jax.ShapeDtypeStruct((M, N), out_dtype),
        in_specs=[pl.BlockSpec((ts, tmm), lambda m, n, k: (k, m)), pl.BlockSpec((ts, tn), lambda m, n, k: (k, n))],
        out_specs=pl.BlockSpec((tmm, tn), lambda m, n, k: (m, n)),
        scratch_shapes=[pltpu.VMEM((tmm, tn), F32)],
        compiler_params=_cparams(3),
    )(a, b)


def _xattn_bwd(dx2, x1, g2, w_xq, w_xo, kv, plans=(), tm=512):
    S = x1.shape[0]

    def body(d_ref, x_ref, g_ref, wq_ref, wo_ref, kv_ref, dx1_ref, o_ref, dq_ref, hx_ref, dkv_ref, dg2_ref):
        i = pl.program_id(0)

        @pl.when(i == 0)
        def _():
            dkv_ref[...] = jnp.zeros_like(dkv_ref)
            dg2_ref[...] = jnp.zeros_like(dg2_ref)

        d2 = d_ref[...]
        xv = x_ref[...]
        r2 = _rstd(xv)
        hx = (xv * r2 * g_ref[...]).astype(BF16)
        hx_ref[...] = hx
        qx = (_dot(hx, wq_ref[...]) * (1.0 / 16)).astype(BF16)
        do = _dot_nt(d2.astype(BF16), wo_ref[...]).astype(BF16)
        for h in range(XH):
            hs = slice(256 * h, 256 * (h + 1))
            vs = slice(D + 256 * h, D + 256 * (h + 1))
            p = _xatt_head(qx, kv_ref, h)
            pb = p.astype(BF16)
            o_ref[:, hs] = _dot(pb, kv_ref[:, vs]).astype(BF16)
            dp = _dot_nt(do[:, hs], kv_ref[:, vs])
            ds = (p * (dp - jnp.sum(dp * p, axis=-1, keepdims=True))).astype(BF16)
            dq_ref[:, hs] = (_dot(ds, kv_ref[:, hs]) * (1.0 / 16)).astype(BF16)
            dkv_ref[:, hs] += _dot_tn(ds, qx[:, hs])
            dkv_ref[:, vs] += _dot_tn(pb, do[:, hs])
        dhx = _dot_nt(dq_ref[...], wq_ref[...])
        dx, dg2 = _rms_bwd(dhx, xv, r2, g_ref[...])
        dx1_ref[...] = d2 + dx
        dg2_ref[...] += dg2

    return _call(
        body, name="xattn_bwd", grid=(S // tm,),
        out_shape=(jax.ShapeDtypeStruct((S, D), F32), jax.ShapeDtypeStruct((S, D), BF16),
                   jax.ShapeDtypeStruct((S, D), BF16), jax.ShapeDtypeStruct((S, D), BF16),
                   jax.ShapeDtypeStruct((MEM, 2 * D), F32), jax.ShapeDtypeStruct((1, D), F32)),
        in_specs=[_rows(tm, D), _rows(tm, D), _resident((1, D)), _resident((D, D)), _resident((D, D)),
                  _resident((MEM, 2 * D))],
        out_specs=(_rows(tm, D), _rows(tm, D), _rows(tm, D), _rows(tm, D),
                   pl.BlockSpec((MEM, 2 * D), lambda i: (0, 0)), pl.BlockSpec((1, D), lambda i: (0, 0))),
        operands=(dx2, x1, g2, w_xq, w_xo, kv), plans=plans)


def _mem_kv_bwd(dkv, mem, g_mem, mn, w_xkv_t):
    def body(dkv_ref, m_ref, g_ref, mn_ref, wt_ref, dw_ref, dg_ref):
        dkvb = dkv_ref[...].astype(BF16)
        dw_ref[...] = _dot_tn(dkvb, mn_ref[...]).astype(BF16)
        dmn = _dot(dkvb, wt_ref[...])
        mv = m_ref[...]
        dg_ref[...] = jnp.sum(dmn * (mv * _rstd(mv)), axis=0, keepdims=True)

    vm = pl.BlockSpec(memory_space=pltpu.VMEM)
    return pl.pallas_call(
        body, name="mem_kv_bwd",
        out_shape=(jax.ShapeDtypeStruct((2 * D, D), BF16), jax.ShapeDtypeStruct((1, D), F32)),
        in_specs=[vm] * 5, out_specs=(vm, vm),
        compiler_params=pltpu.CompilerParams(vmem_limit_bytes=VMEM_LIMIT),
    )(dkv, mem, g_mem, mn, w_xkv_t)


def _merge_bwd(dx1, y_att, y_sg, gates, wba_t, wbs_t, w_out, tm=512):
    S = dx1.shape[0]

    def body(d_ref, ya_ref, ys_ref, g_ref, wbat_ref, wbst_ref, wo_ref,
             mg_ref, da_ref, db_ref, dya_ref, dys_ref, dg_ref):
        dm = _dot_nt(d_ref[...].astype(BF16), wo_ref[...])
        a = _dot_nt(ya_ref[...], wbat_ref[...])
        b = _dot_nt(ys_ref[...], wbst_ref[...])
        sa = _sigmoid(g_ref[:, 0:D].astype(F32))
        sb = _sigmoid(g_ref[:, D:2 * D].astype(F32))
        mg_ref[...] = (sa * a + sb * b).astype(BF16)
        da = (dm * sa).astype(BF16)
        db = (dm * sb).astype(BF16)
        da_ref[...] = da
        db_ref[...] = db
        dg_ref[:, 0:D] = (dm * a * sa * (1.0 - sa)).astype(BF16)
        dg_ref[:, D:2 * D] = (dm * b * sb * (1.0 - sb)).astype(BF16)
        dya_ref[...] = _dot(da, wbat_ref[...]).astype(BF16)
        dys_ref[...] = _dot(db, wbst_ref[...]).astype(BF16)

    return pl.pallas_call(
        body, name="merge_bwd", grid=(S // tm,),
        out_shape=(jax.ShapeDtypeStruct((S, D), BF16), jax.ShapeDtypeStruct((S, D), BF16),
                   jax.ShapeDtypeStruct((S, D), BF16), jax.ShapeDtypeStruct((S, ATT_W), BF16),
                   jax.ShapeDtypeStruct((S, SG_W), BF16), jax.ShapeDtypeStruct((S, 2 * D), BF16)),
        in_specs=[_rows(tm, D), _rows(tm, ATT_W), _rows(tm, SG_W), _rows(tm, 2 * D),
                  _resident((D, ATT_W)), _resident((D, SG_W)), _resident((D, D))],
        out_specs=(_rows(tm, D), _rows(tm, D), _rows(tm, D), _rows(tm, ATT_W), _rows(tm, SG_W), _rows(tm, 2 * D)),
        compiler_params=_cparams(1),
    )(dx1, y_att, y_sg, gates, wba_t, wbs_t, w_out)


def _sgu_bwd(uv, dy_sg, lng, lnb, sg_w, b_exp, plans=(), tm=512):
    S = uv.shape[0]
    n_steps = S // tm

    def body(uv_ref, dy_ref, lng_ref, lnb_ref, w_ref, b_ref, duv_ref, dw_ref, dbx_ref, dlng_ref, dlnb_ref, dvln_ref):
        i = pl.program_id(0)

        @pl.when(i == 0)
        def _():
            dw_ref[...] = jnp.zeros_like(dw_ref)
            dbx_ref[...] = jnp.zeros_like(dbx_ref)
            dlng_ref[...] = jnp.zeros_like(dlng_ref)
            dlnb_ref[...] = jnp.zeros_like(dlnb_ref)

        lane = lax.broadcasted_iota(jnp.int32, (1, LANES), 1)
        lo = lane < 64
        mask = _sg_mask()
        lng = lng_ref[...]
        gu, dgelu_u = _gelu_and_grad(uv_ref[:, 0:SG_W].astype(F32))
        gv, dgelu_v = _gelu_and_grad(uv_ref[:, SG_W:2 * SG_W].astype(F32))
        vln, vhat, rstd = _sg_layernorm(gv, lng, lnb_ref[...])
        dy = dy_ref[...].astype(F32)
        dsv_all = dy * gu
        for gp in range(4):
            wf0 = jnp.where(mask, w_ref[2 * gp], 0)
            wf1 = jnp.where(mask, w_ref[2 * gp + 1], 0)
            w0 = wf0.astype(BF16)
            w1 = wf1.astype(BF16)
            cols = slice(128 * gp, 128 * (gp + 1))
            dw0 = jnp.zeros((128, 128), F32)
            dw1 = jnp.zeros((128, 128), F32)
            dbx = jnp.zeros((128, LANES), F32)
            for n in range(tm // 128):
                rows = slice(128 * n, 128 * (n + 1))
                vl = vln[rows, cols]
                vl0 = jnp.where(lo, vl, 0).astype(BF16)
                vl1 = jnp.where(lo, 0, vl).astype(BF16)
                sv = _dot(w0, vl0) + _dot(w1, vl1) + b_ref[:, cols]
                duv_ref[rows, cols] = (dy[rows, cols] * sv * dgelu_u[rows, cols]).astype(BF16)
                dsv = dsv_all[rows, cols]
                dbx = dbx + dsv
                ds0 = jnp.where(lo, dsv, 0).astype(BF16)
                ds1 = jnp.where(lo, 0, dsv).astype(BF16)
                dw0 = dw0 + _dot_nt(ds0, vl0)
                dw1 = dw1 + _dot_nt(ds1, vl1)
                dvln_ref[rows, cols] = _dot_tn(w0, ds0) + _dot_tn(w1, ds1)
            dw_ref[2 * gp] += jnp.where(mask, dw0, 0)
            dw_ref[2 * gp + 1] += jnp.where(mask, dw1, 0)
            dbx_ref[:, cols] += dbx
        dvln = dvln_ref[...]
        dlng_ref[...] += jnp.sum(dvln * vhat, axis=0, keepdims=True)
        dlnb_ref[...] += jnp.sum(dvln, axis=0, keepdims=True)
        dvh = dvln * lng
        dgv = rstd * (dvh - _group_sum64(dvh) * (1.0 / 64) - vhat * (_group_sum64(dvh * vhat) * (1.0 / 64)))
        duv_ref[:, SG_W:2 * SG_W] = (dgv * dgelu_v).astype(BF16)

        @pl.when(i == n_steps - 1)
        def _():
            dbx_ref[...] = _group_sum64(dbx_ref[...])

    return _call(
        body, name="sgu_bwd", grid=(n_steps,),
        out_shape=(jax.ShapeDtypeStruct((S, 2 * SG_W), BF16), jax.ShapeDtypeStruct((8, 128, 128), F32),
                   jax.ShapeDtypeStruct((128, SG_W), F32), jax.ShapeDtypeStruct((1, SG_W), F32),
                   jax.ShapeDtypeStruct((1, SG_W), F32)),
        in_specs=[_rows(tm, 2 * SG_W), _rows(tm, SG_W), _resident((1, SG_W)), _resident((1, SG_W)),
                  _resident((8, 128, 128)), _resident((128, SG_W))],
        out_specs=(_rows(tm, 2 * SG_W), pl.BlockSpec((8, 128, 128), lambda i: (0, 0, 0)),
                   pl.BlockSpec((128, SG_W), lambda i: (0, 0)), pl.BlockSpec((1, SG_W), lambda i: (0, 0)),
                   pl.BlockSpec((1, SG_W), lambda i: (0, 0))),
        scratch_shapes=[pltpu.VMEM((tm, SG_W), F32)],
        operands=(uv, dy_sg, lng, lnb, sg_w, b_exp), plans=plans)


def _attn_bwd(qkv, dy_att, y_att, lse, bias, plans=()):
    S = qkv.shape[0]
    R = ATT_R

    def body(q_ref, kp_ref, kc_ref, vp_ref, vc_ref, b_ref, dy_ref, y_ref, lse_ref, dq_ref, dk_ref, dv_ref, dss_ref):
        i = pl.program_id(1)

        @pl.when(i == 0)
        def _():
            dk_ref[...] = jnp.zeros_like(dk_ref)
            dv_ref[...] = jnp.zeros_like(dv_ref)
            dss_ref[...] = jnp.zeros_like(dss_ref)

        lane = lax.broadcasted_iota(jnp.int32, (1, LANES), 1)
        kwin = jnp.concatenate([kp_ref[...], kc_ref[...]], axis=0)
        vwin = jnp.concatenate([vp_ref[...], vc_ref[...]], axis=0)
        for sub in range(ATT_SUBS):
            rows = slice(128 * sub, 128 * (sub + 1))
            q2 = q_ref[rows, :]
            do2 = dy_ref[rows, :]
            kw = kwin[128 * sub:128 * sub + BAND_KEYS]
            vw = vwin[128 * sub:128 * sub + BAND_KEYS]
            table = jnp.where(i == 0, 1 + sub, 0)
            dyy = do2.astype(F32) * y_ref[rows, :].astype(F32)
            lse2 = lse_ref[rows, :]
            dq = jnp.zeros((128, LANES), F32)
            dkw = jnp.zeros((BAND_KEYS, LANES), F32)
            dvw = jnp.zeros((BAND_KEYS, LANES), F32)
            for hh in range(2):
                mine = (lane >= 64 * hh) & (lane < 64 * (hh + 1))
                qm = jnp.where(mine, q2, 0)
                dom = jnp.where(mine, do2, 0)
                p = jnp.exp(_dot_nt(qm, kw) + b_ref[table, hh] - lse2[:, 64 * hh:64 * hh + 1])
                dp = _dot_nt(dom, vw)
                ds = p * (dp - jnp.sum(jnp.where(mine, dyy, 0.0), axis=-1, keepdims=True))
                dss_ref[hh] += ds
                dsb = ds.astype(BF16)
                dq = dq + _dot(dsb, jnp.where(mine, kw, 0))
                dkw = dkw + _dot_tn(dsb, qm)
                dvw = dvw + _dot_tn(p.astype(BF16), dom)
            dq_ref[rows, :] = dq.astype(BF16)
            start = pl.multiple_of(i * R + 128 * sub, 128)
            dk_ref[pl.ds(start, BAND_KEYS), :] += dkw
            dv_ref[pl.ds(start, BAND_KEYS), :] += dvw

    acc_spec = pl.BlockSpec((S + 8 * CHUNK, LANES), lambda j, i: (0, j))
    return _call(
        body, name="attn_bwd", grid=(4, S // R),
        out_shape=(jax.ShapeDtypeStruct((S, ATT_W), BF16), jax.ShapeDtypeStruct((S + 8 * CHUNK, ATT_W), F32),
                   jax.ShapeDtypeStruct((S + 8 * CHUNK, ATT_W), F32), jax.ShapeDtypeStruct((8, 128, BAND_KEYS), F32)),
        in_specs=_att_specs() + [pl.BlockSpec((R, LANES), lambda j, i: (i, j))] * 3,
        out_specs=(pl.BlockSpec((R, LANES), lambda j, i: (i, j)), acc_spec, acc_spec,
                   pl.BlockSpec((2, 128, BAND_KEYS), lambda j, i: (j, 0, 0))),
        operands=(qkv, qkv, qkv, qkv, qkv, bias, dy_att, y_att, lse), plans=plans)


def _in_bwd(dq, dk, dv, duv, dgates, x, dx1, g1, w_in_t, tm=256):
    S = x.shape[0]
    pad_blocks = (8 * CHUNK) // tm

    def body(dq_ref, dk_ref, dv_ref, duv_ref, dg_ref, x_ref, d1_ref, g_ref, wt_ref, dx_ref, dz_ref, dg1_ref):
        i = pl.program_id(0)
        dz_ref[:, 0:ATT_W] = (dq_ref[...].astype(F32) * 0.125).astype(BF16)
        dz_ref[:, ATT_W:2 * ATT_W] = dk_ref[...].astype(BF16)
        dz_ref[:, 2 * ATT_W:3 * ATT_W] = dv_ref[...].astype(BF16)
        dz_ref[:, 3 * ATT_W:3 * ATT_W + 2 * SG_W] = duv_ref[...]
        dz_ref[:, 3 * ATT_W + 2 * SG_W:IN_COLS] = dg_ref[...]
        dh = _dot(dz_ref[...], wt_ref[...])
        xv = x_ref[...]
        dx, dg1 = _rms_bwd(dh, xv, _rstd(xv), g_ref[...])
        dx_ref[...] = d1_ref[...] + dx

        @pl.when(i == 0)
        def _():
            dg1_ref[...] = jnp.zeros_like(dg1_ref)

        dg1_ref[...] += dg1

    shifted = pl.BlockSpec((tm, ATT_W), lambda i: (i + pad_blocks, 0))
    return pl.pallas_call(
        body, name="in_bwd", grid=(S // tm,),
        out_shape=(jax.ShapeDtypeStruct((S, D), F32), jax.ShapeDtypeStruct((S, IN_COLS), BF16),
                   jax.ShapeDtypeStruct((1, D), F32)),
        in_specs=[_rows(tm, ATT_W), shifted, shifted, _rows(tm, 2 * SG_W), _rows(tm, 2 * D), _rows(tm, D),
                  _rows(tm, D), _resident((1, D)), _resident((IN_COLS, D))],
        out_specs=(_rows(tm, D), _rows(tm, IN_COLS), pl.BlockSpec((1, D), lambda i: (0, 0))),
        compiler_params=_cparams(1),
    )(dq, dk, dv, duv, dgates, x, dx1, g1, w_in_t)


def _adam_math(w, g, m, v):
    m = ADAM_B1 * m + (1.0 - ADAM_B1) * g
    v = ADAM_B2 * v + (1.0 - ADAM_B2) * (g * g)
    m_hat = m / (1.0 - ADAM_B1 ** ADAM_STEP)
    v_hat = v / (1.0 - ADAM_B2 ** ADAM_STEP)
    delta = -ADAM_LR * (m_hat / (jnp.sqrt(v_hat) + ADAM_EPS) + ADAM_WD * w)
    return delta, m, v


def _adam(parts, w, m, v, tr, name, transposed):
    P = parts.shape[0]
    R, C = w.shape

    def body(p_ref, w_ref, m_ref, v_ref, g_ref, d_ref, mo_ref, vo_ref):
        if transposed:
            eye = (lax.broadcasted_iota(jnp.int32, (C, C), 0) == lax.broadcasted_iota(jnp.int32, (C, C), 1)).astype(BF16)
            part = lambda k: _dot_tn(p_ref[k], eye)
        else:
            part = lambda k: p_ref[k].astype(F32)
        g = part(0)
        for k in range(1, P):
            g = g + part(k)
        g_ref[...] = g
        d_ref[...], mo_ref[...], vo_ref[...] = _adam_math(w_ref[...], g, m_ref[...], v_ref[...])

    row = pl.BlockSpec((tr, C), lambda i: (i, 0))
    p_spec = pl.BlockSpec((P, C, tr), lambda i: (0, 0, i)) if transposed else pl.BlockSpec((P, tr, C), lambda i: (0, i, 0))
    return pl.pallas_call(
        body, name=name, grid=(R // tr,),
        out_shape=tuple(jax.ShapeDtypeStruct((R, C), F32) for _ in range(4)),
        in_specs=[p_spec, row, row, row],
        out_specs=(row, row, row, row),
        compiler_params=_cparams(1),
    )(parts, w, m, v)


def _my_place():
    return lax.axis_index("x"), lax.axis_index("y"), lax.axis_index("c")


def _other_chips(x, y):
    return [(1 - x, y), (x, 1 - y), (1 - x, 1 - y)]


class _Plan:
    def __init__(self, arrays, out_shapes, sems, start, finish, forward=None, forward_at=0.7):
        self.arrays, self.out_shapes, self.sems = list(arrays), list(out_shapes), list(sems)
        self.start, self.finish, self.forward, self.forward_at = start, finish, forward, forward_at


def _call(body, *, name, grid, in_specs, out_specs, out_shape, operands, scratch_shapes=(), plans=()):
    n_in, n_out, n_scr = len(operands), len(out_shape), len(scratch_shapes)
    p_in = [a for p in plans for a in p.arrays]
    p_out = [s for p in plans for s in p.out_shapes]
    p_sem = [s for p in plans for s in p.sems]
    steps = math.prod(grid)

    def wrapped(*refs):
        ins, refs = refs[:n_in], refs[n_in:]
        pins, refs = refs[:len(p_in)], refs[len(p_in):]
        outs, refs = refs[:n_out], refs[n_out:]
        pouts, refs = refs[:len(p_out)], refs[len(p_out):]
        scr, psems = refs[:n_scr], refs[n_scr:]
        step = 0
        for axis, size in enumerate(grid):
            step = step * size + pl.program_id(axis)
        bound = []
        for p in plans:
            bound.append((p, pins[:len(p.arrays)], pouts[:len(p.out_shapes)], psems[:len(p.sems)]))
            pins, pouts, psems = pins[len(p.arrays):], pouts[len(p.out_shapes):], psems[len(p.sems):]

        @pl.when(step == 0)
        def _():
            for p, a, b, s in bound:
                p.start(a, b, s)

        body(*ins, *outs, *scr)

        for p, a, b, s in bound:
            if p.forward is not None:
                @pl.when(step == min(int(p.forward_at * steps), steps - 1))
                def _(p=p, a=a, b=b, s=s):
                    p.forward(a, b, s)

        @pl.when(step == steps - 1)
        def _():
            for p, a, b, s in bound:
                p.finish(a, b, s)

    hbm = pl.BlockSpec(memory_space=pl.ANY)
    return pl.pallas_call(
        wrapped if plans else body, name=name, grid=grid,
        out_shape=tuple(out_shape) + tuple(p_out),
        in_specs=list(in_specs) + [hbm] * len(p_in),
        out_specs=tuple(out_specs) + tuple([hbm] * len(p_out)),
        scratch_shapes=list(scratch_shapes) + p_sem,
        compiler_params=_cparams(len(grid)),
    )(*operands, *p_in)


def _run_plan(plan, name):
    n_in, n_out = len(plan.arrays), len(plan.out_shapes)

    def body(*refs):
        a, b, s = refs[:n_in], refs[n_in:n_in + n_out], refs[n_in + n_out:]
        plan.start(a, b, s)
        if plan.forward is not None:
            plan.forward(a, b, s)
        plan.finish(a, b, s)

    hbm = pl.BlockSpec(memory_space=pl.ANY)
    return pl.pallas_call(
        body, name=name, out_shape=tuple(plan.out_shapes),
        in_specs=[hbm] * n_in, out_specs=tuple([hbm] * n_out), scratch_shapes=plan.sems,
    )(*plan.arrays)


def _gather_plan(shards, forward_at=0.7):
    n = len(shards)

    def copies(ins, outs, sems):
        send_sems, recv_sems, local_sems = sems
        x, y, c = _my_place()
        me, sibling = (x, y, c), (x, y, 1 - c)
        chips = _other_chips(x, y)

        def idx(p):
            return 4 * p[0] + 2 * p[1] + p[2]

        def copy(a, k, block, to, src=None):
            return pltpu.make_async_remote_copy(
                src_ref=outs[a].at[idx(block)] if src is None else src, dst_ref=outs[a].at[idx(block)],
                send_sem=send_sems.at[a, k], recv_sem=recv_sems.at[a, k], device_id=to, device_id_type=MESH)

        mine = [pltpu.make_async_copy(ins[a], outs[a].at[idx(me)], local_sems.at[a]) for a in range(n)]
        first = []
        for a in range(n):
            first.append(copy(a, 0, me, sibling, src=ins[a]))
            first += [copy(a, 1 + j, me, (*chip, c), src=ins[a]) for j, chip in enumerate(chips)]
        arrived = [copy(a, 1 + j, (*chip, c), me) for j, chip in enumerate(chips) for a in range(n)]
        passed = [copy(a, 4 + j, (*chip, c), sibling) for j, chip in enumerate(chips) for a in range(n)]
        from_sibling = []
        for a in range(n):
            from_sibling.append(copy(a, 0, sibling, me))
            from_sibling += [copy(a, 4 + j, (*chip, 1 - c), me) for j, chip in enumerate(chips)]
        return mine, first, arrived, passed, from_sibling

    def start(ins, outs, sems):
        mine, first, _, _, _ = copies(ins, outs, sems)
        for cp in mine + first:
            cp.start()

    def forward(ins, outs, sems):
        _, _, arrived, passed, _ = copies(ins, outs, sems)
        for landed, onward in zip(arrived, passed):
            landed.wait_recv()
            onward.start()

    def finish(ins, outs, sems):
        mine, first, _, passed, from_sibling = copies(ins, outs, sems)
        for cp in from_sibling:
            cp.wait_recv()
        for cp in first + passed:
            cp.wait_send()
        for cp in mine:
            cp.wait()

    return _Plan(shards, [jax.ShapeDtypeStruct((N_DEV,) + s.shape, s.dtype) for s in shards],
                 [pltpu.SemaphoreType.DMA((n, 7)), pltpu.SemaphoreType.DMA((n, 7)), pltpu.SemaphoreType.DMA((n,))],
                 start, finish, forward, forward_at)


def _sibling_plan(scatter, whole=()):
    ns = len(scatter)
    n = ns + len(whole)

    def copies(ins, outs, sems):
        send_sems, recv_sems = sems
        x, y, c = _my_place()
        out = []
        for a in range(n):
            for k in range(4 if a < ns else 1):
                src = ins[a].at[2 * k + (1 - c)] if a < ns else ins[a]
                dst = outs[a].at[k] if a < ns else outs[a]
                out.append(pltpu.make_async_remote_copy(
                    src_ref=src, dst_ref=dst, send_sem=send_sems.at[a, k], recv_sem=recv_sems.at[a, k],
                    device_id=(x, y, 1 - c), device_id_type=MESH))
        return out

    def start(ins, outs, sems):
        for cp in copies(ins, outs, sems):
            cp.start()

    def finish(ins, outs, sems):
        for cp in copies(ins, outs, sems):
            cp.wait()

    shapes = [jax.ShapeDtypeStruct((4,) + p.shape[1:], p.dtype) for p in scatter]
    shapes += [jax.ShapeDtypeStruct(p.shape, p.dtype) for p in whole]
    return _Plan(list(scatter) + list(whole), shapes,
                 [pltpu.SemaphoreType.DMA((n, 4)), pltpu.SemaphoreType.DMA((n, 4))], start, finish)


def _pair_sum(mine, theirs, c_idx, name):
    _, R, C = mine.shape

    def body(c_ref, a_ref, b_ref, o_ref):
        o_ref[...] = (a_ref[...].astype(F32) + b_ref[...].astype(F32)).astype(o_ref.dtype)

    grid_spec = pltpu.PrefetchScalarGridSpec(
        num_scalar_prefetch=1, grid=(4,),
        in_specs=[pl.BlockSpec((1, R, C), lambda k, c_ref: (2 * k + c_ref[0], 0, 0)),
                  pl.BlockSpec((1, R, C), lambda k, c_ref: (k, 0, 0))],
        out_specs=pl.BlockSpec((1, R, C), lambda k, c_ref: (k, 0, 0)))
    return pl.pallas_call(
        body, name=name, grid_spec=grid_spec,
        out_shape=jax.ShapeDtypeStruct((4, R, C), mine.dtype),
        compiler_params=_cparams(1),
    )(c_idx, mine, theirs)


def _pair_sum_whole(mine, theirs, name):
    n = len(mine)

    def body(*refs):
        for a in range(n):
            refs[2 * n + a][...] = refs[a][...] + refs[n + a][...]

    vm = pl.BlockSpec(memory_space=pltpu.VMEM)
    return pl.pallas_call(
        body, name=name, out_shape=tuple(jax.ShapeDtypeStruct(p.shape, p.dtype) for p in mine),
        in_specs=[vm] * (2 * n), out_specs=tuple([vm] * n),
    )(*mine, *theirs)


def _chips_plan(scatter, whole=()):
    ns = len(scatter)
    n = ns + len(whole)

    def copies(ins, outs, sems):
        send_sems, recv_sems, local_sems = sems
        x, y, c = _my_place()
        my_chip = 2 * x + y

        def src(a, k):
            return ins[a].at[k] if a < ns else ins[a]

        local = [pltpu.make_async_copy(src(a, my_chip), outs[a].at[my_chip], local_sems.at[a]) for a in range(n)]
        remote = []
        for a in range(n):
            for j, (px, py) in enumerate(_other_chips(x, y)):
                remote.append(pltpu.make_async_remote_copy(
                    src_ref=src(a, 2 * px + py), dst_ref=outs[a].at[my_chip],
                    send_sem=send_sems.at[a, j], recv_sem=recv_sems.at[a, j],
                    device_id=(px, py, c), device_id_type=MESH))
        return local + remote

    def start(ins, outs, sems):
        for cp in copies(ins, outs, sems):
            cp.start()

    def finish(ins, outs, sems):
        for cp in copies(ins, outs, sems):
            cp.wait()

    shapes = [jax.ShapeDtypeStruct(s.shape, s.dtype) for s in scatter]
    shapes += [jax.ShapeDtypeStruct((4,) + s.shape, s.dtype) for s in whole]
    return _Plan(list(scatter) + list(whole), shapes,
                 [pltpu.SemaphoreType.DMA((n, 3)), pltpu.SemaphoreType.DMA((n, 3)), pltpu.SemaphoreType.DMA((n,))],
                 start, finish)


def _small_update(parts, w, m, v, loss_parts, name):
    n = len(parts)

    def body(*refs):
        p_refs, w_refs, m_refs, v_refs = (refs[i * n:(i + 1) * n] for i in range(4))
        lp_ref = refs[4 * n]
        outs = refs[4 * n + 1:]
        g_refs, d_refs, mo_refs, vo_refs = (outs[i * n:(i + 1) * n] for i in range(4))
        for a in range(n):
            g = ((p_refs[a][0] + p_refs[a][1]) + p_refs[a][2]) + p_refs[a][3]
            g_refs[a][...] = g
            d_refs[a][...], mo_refs[a][...], vo_refs[a][...] = _adam_math(w_refs[a][...], g, m_refs[a][...],
                                                                          v_refs[a][...])
        outs[4 * n][...] = ((lp_ref[0] + lp_ref[1]) + lp_ref[2]) + lp_ref[3]

    vm = pl.BlockSpec(memory_space=pltpu.VMEM)
    shapes = [jax.ShapeDtypeStruct(t.shape, F32) for _ in range(4) for t in w]
    shapes.append(jax.ShapeDtypeStruct(loss_parts.shape[1:], F32))
    outs = pl.pallas_call(
        body, name=name, out_shape=tuple(shapes),
        in_specs=[vm] * (4 * n + 1), out_specs=tuple([vm] * (4 * n + 1)),
    )(*parts, *w, *m, *v, loss_parts)
    return outs[0:n], outs[n:2 * n], outs[2 * n:3 * n], outs[3 * n:4 * n], outs[4 * n]


BIG = [("w_in", 1), ("w_branch_att", 1), ("w_branch_sg", 1), ("w_out", 0), ("w_xq", 0), ("w_xkv", 1), ("w_xo", 0),
       ("w_ffn_in", 1), ("w_ffn_out", 0)]
SMALL = [("norm_mix_g", (1, D)), ("rel_bias", (8, NREL)), ("sg_ln_g", (8, 64)), ("sg_ln_b", (8, 64)),
         ("sg_w", (8, 128, 128)), ("sg_b", (8, 128)), ("norm_xattn_g", (1, D)), ("norm_mem_g", (1, D)),
         ("norm_ffn_g", (1, D)), ("norm_final_g", (1, D))]
ADAM_ROWS = {"w_in": 256, "w_branch_att": 512, "w_branch_sg": 512, "w_xkv": 1024, "w_ffn_in": 256}


def _full(gathered):
    return gathered.reshape(N_DEV * gathered.shape[1], gathered.shape[2])


def _blocks(grad):
    return grad.reshape(N_DEV, grad.shape[0] // N_DEV, grad.shape[1])


def kernel(x, mem, norm_mix_g, w_in, rel_bias, sg_ln_g, sg_ln_b, sg_w, sg_b, w_branch_att, w_branch_sg, w_out, norm_xattn_g, norm_mem_g, w_xq, w_xkv, w_xo, norm_ffn_g, w_ffn_in, w_ffn_out, norm_final_g, loss_target, m_norm_mix_g, m_w_in, m_rel_bias, m_sg_ln_g, m_sg_ln_b, m_sg_w, m_sg_b, m_w_branch_att, m_w_branch_sg, m_w_out, m_norm_xattn_g, m_norm_mem_g, m_w_xq, m_w_xkv, m_w_xo, m_norm_ffn_g, m_w_ffn_in, m_w_ffn_out, m_norm_final_g, v_norm_mix_g, v_w_in, v_rel_bias, v_sg_ln_g, v_sg_ln_b, v_sg_w, v_sg_b, v_w_branch_att, v_w_branch_sg, v_w_out, v_norm_xattn_g, v_norm_mem_g, v_w_xq, v_w_xkv, v_w_xo, v_norm_ffn_g, v_w_ffn_in, v_w_ffn_out, v_norm_final_g):
    args = dict(locals())
    big_names = [n for n, _ in BIG]
    small_names = [n for n, _ in SMALL]
    S = x.shape[1]

    x, mem, tgt = x.reshape(S, D), mem.reshape(MEM, D), loss_target.reshape(S, D)
    small = {n: args[n].reshape(shape) for n, shape in SMALL}
    g1, g2, g3 = small["norm_mix_g"], small["norm_xattn_g"], small["norm_ffn_g"]
    g_mem, g4 = small["norm_mem_g"], small["norm_final_g"]
    lng = small["sg_ln_g"].reshape(1, SG_W)
    lnb = small["sg_ln_b"].reshape(1, SG_W)
    b_exp = jnp.broadcast_to(small["sg_b"].T[:, :, None], (128, 8, 64)).reshape(128, SG_W)
    rel_pad = jnp.pad(small["rel_bias"], ((0, 0), (0, 384 - NREL)))
    c_idx = lax.axis_index("c").astype(jnp.int32).reshape(1)

    shard = {n: (args[n][0].T if axis == 1 else args[n][0]).astype(BF16) for n, axis in BIG}
    w_in_t = _full(_run_plan(_gather_plan([shard["w_in"]]), "ag_w_in")[0])
    bias = _bias_table(rel_pad)
    mix_names = ["w_branch_att", "w_branch_sg", "w_out", "w_xq", "w_xkv", "w_xo"]
    qkv, uv, gates, h, *got = _in_proj(x, g1, w_in_t, plans=[_gather_plan([shard[n] for n in mix_names])])
    wba_t, wbs_t, w_out_f, w_xq_f, w_xkv_t, w_xo_f = (_full(g) for g in got)
    y_att, lse, *got = _attn_fwd(qkv, bias, plans=[_gather_plan([shard["w_ffn_in"], shard["w_ffn_out"]])])
    w_ffn_in_t, w_ffn_out_f = (_full(g) for g in got)
    y_sg = _sgu_fwd(uv, lng, lnb, small["sg_w"], b_exp)
    x1 = _merge_fwd(x, y_att, y_sg, gates, wba_t, wbs_t, w_out_f)
    kv, mn = _mem_kv(mem, g_mem, w_xkv_t)
    x2 = _xattn_fwd(x1, g2, w_xq_f, kv, w_xo_f)
    dx3, gu, hf, act, loss_part, dg4 = _ffn_fwd(x2, tgt, g3, w_ffn_in_t, w_ffn_out_f, g4)

    def pair_sums(names, mine, theirs):
        return [_pair_sum(a, b, c_idx, "rs_pair_" + n) for n, a, b in zip(names, mine, theirs)]

    dx2, dgu, dg3 = _ffn_bwd(dx3, gu, x2, g3, w_ffn_out_f, w_ffn_in_t)
    ffn_names = ["w_ffn_out", "w_ffn_in"]
    ffn_mine = [_blocks(_dw(act, dx3, 1408, 1024, DW_TOKENS, "dw_ffn_out")),
                _blocks(_dw(dgu, hf, 1408, 1024, DW_TOKENS, "dw_ffn_in"))]
    dx1, o_x, dq_x, hx, dkv, dg2, *ffn_theirs = _xattn_bwd(dx2, x1, g2, w_xq_f, w_xo_f, kv,
                                                           plans=[_sibling_plan(ffn_mine)])
    ffn_chip = pair_sums(ffn_names, ffn_mine, ffn_theirs)
    d_xkv, dg_mem = _mem_kv_bwd(dkv, mem, g_mem, mn, w_xkv_t)
    merged, d_a, d_b, dy_att, dy_sg, dgates = _merge_bwd(dx1, y_att, y_sg, gates, wba_t, wbs_t, w_out_f)
    mid_names = ["w_xo", "w_xq", "w_xkv", "w_out", "w_branch_att", "w_branch_sg"]
    mid_mine = [_blocks(g) for g in (
        _dw(o_x, dx2, 1024, 1024, DW_TOKENS, "dw_xo"), _dw(hx, dq_x, 1024, 1024, DW_TOKENS, "dw_xq"), d_xkv,
        _dw(merged, dx1, 1024, 1024, DW_TOKENS, "dw_out"), _dw(d_a, y_att, 1024, 512, DW_TOKENS, "dw_branch_att"),
        _dw(d_b, y_sg, 1024, 512, DW_TOKENS, "dw_branch_sg"))]
    duv, d_sgw, d_bx, d_lng, d_lnb, *got = _sgu_bwd(uv, dy_sg, lng, lnb, small["sg_w"], b_exp,
                                                    plans=[_chips_plan(ffn_chip), _sibling_plan(mid_mine)])
    ffn_all, mid_theirs = got[:2], got[2:]
    mid_chip = pair_sums(mid_names, mid_mine, mid_theirs)
    dq, dk, dv, ds_sum, *mid_all = _attn_bwd(qkv, dy_att, y_att, lse, bias, plans=[_chips_plan(mid_chip)])
    d_rel = _bias_grad(ds_sum)
    grad_x, dz, dg1 = _in_bwd(dq, dk, dv, duv, dgates, x, dx1, g1, w_in_t)
    in_mine = [_blocks(_dw(dz, h, 1152, 1024, DW_TOKENS, "dw_in"))]

    gs = {"norm_mix_g": dg1, "rel_bias": d_rel[:, :NREL], "sg_ln_g": d_lng.reshape(8, 64),
          "sg_ln_b": d_lnb.reshape(8, 64), "sg_w": d_sgw, "sg_b": d_bx.reshape(128, 8, 64)[:, :, 0].T,
          "norm_xattn_g": dg2, "norm_mem_g": dg_mem, "norm_ffn_g": dg3, "norm_final_g": dg4}
    whole = [gs[n] for n in small_names] + [loss_part]
    in_theirs, *sib_whole = _run_plan(_sibling_plan(in_mine, whole), "rs_sibling")
    in_chip = pair_sums(["w_in"], in_mine, [in_theirs])
    chip_whole = _pair_sum_whole(whole, sib_whole, "rs_pair_small")
    in_all, *all_whole = _run_plan(_chips_plan(in_chip, chip_whole), "rs_chips")
    all_parts = dict(zip(ffn_names + mid_names + ["w_in"], list(ffn_all) + list(mid_all) + [in_all]))

    res = {}
    for n, axis in BIG:
        parts = all_parts[n]
        wmv = [args[p + n][0] for p in ("", "m_", "v_")]
        tr = ADAM_ROWS[n] if axis == 1 else wmv[0].shape[0]
        res[n] = [t[None] for t in _adam(parts, *wmv, tr, "adam_" + n, transposed=(axis == 1))]
    small_res = _small_update(
        all_whole[:-1], [small[n] for n in small_names], [args["m_" + n].reshape(s) for n, s in SMALL],
        [args["v_" + n].reshape(s) for n, s in SMALL], all_whole[-1], "adam_small")
    for i, n in enumerate(small_names):
        res[n] = [small_res[k][i].reshape(args[n].shape) for k in range(4)]
    loss = small_res[4][0, 0]

    order = ["norm_mix_g", "w_in", "rel_bias", "sg_ln_g", "sg_ln_b", "sg_w", "sg_b", "w_branch_att", "w_branch_sg",
             "w_out", "norm_xattn_g", "norm_mem_g", "w_xq", "w_xkv", "w_xo", "norm_ffn_g", "w_ffn_in", "w_ffn_out",
             "norm_final_g"]
    outs = [loss, grad_x.reshape(1, S, D)]
    for k in range(4):
        outs += [res[n][k] for n in order]
    return tuple(outs)
```

```python
import math

import jax
import jax.numpy as jnp
from jax import lax
from jax.experimental import pallas as pl
from jax.experimental.pallas import tpu as pltpu

F32 = jnp.float32
BF16 = jnp.bfloat16

D = 1024
ATT_W = 512
SG_W = 512
IN_COLS = 4608
DFF = 2816
MEM = 256
XH = 4
CHUNK = 64
BAND_KEYS = 640
ATT_R = 512
ATT_SUBS = ATT_R // 128
DW_TOKENS = 1024
REL_CLIP = 128
NREL = 2 * REL_CLIP + 1
EPS = 1e-6
NEG = -1e30
N_DEV = 8

ADAM_LR = 0.001
ADAM_B1 = 0.9
ADAM_B2 = 0.999
ADAM_EPS = 1e-08
ADAM_WD = 0.01
ADAM_STEP = 10

LANES = 128
VMEM_LIMIT = 56 * 1024 * 1024
MESH = pl.DeviceIdType.MESH


def _cparams(n_axes):
    return pltpu.CompilerParams(dimension_semantics=("arbitrary",) * n_axes, vmem_limit_bytes=VMEM_LIMIT)


def _resident(shape):
    zeros = (0,) * len(shape)
    return pl.BlockSpec(shape, lambda *_: zeros, pipeline_mode=pl.Buffered(1))


def _rows(tm, cols, col_block=0):
    return pl.BlockSpec((tm, cols), lambda i: (i, col_block))


def _sigmoid(x):
    return 1.0 / (1.0 + jnp.exp(-x))


_GELU_C = math.sqrt(2.0 / math.pi)


def _gelu(x):
    t = jnp.tanh(_GELU_C * (x + 0.044715 * (x * x * x)))
    return x * (0.5 * (1.0 + t))


def _gelu_and_grad(x):
    x2 = x * x
    t = jnp.tanh(_GELU_C * (x + 0.044715 * (x2 * x)))
    cdf = 0.5 * (1.0 + t)
    dcdf = 0.5 * (1.0 - t * t) * (_GELU_C * (1.0 + 3.0 * 0.044715 * x2))
    return x * cdf, cdf + x * dcdf


def _rstd(x):
    return lax.rsqrt(jnp.mean(x * x, axis=-1, keepdims=True) + EPS)


def _rms_bwd(dh, x, r, g):
    xh = x * r
    dxh = dh * g
    dx = r * (dxh - xh * jnp.mean(dxh * xh, axis=-1, keepdims=True))
    dg = jnp.sum(dh * xh, axis=0, keepdims=True)
    return dx, dg


def _group_sum64(x):
    r = lax.broadcasted_iota(jnp.int32, (LANES, LANES), 0) // 64
    c = lax.broadcasted_iota(jnp.int32, (LANES, LANES), 1) // 64
    same_group = (r == c).astype(BF16)

    def one(v):
        hi = v.astype(BF16)
        rest = v - hi.astype(F32)
        mid = rest.astype(BF16)
        lo = (rest - mid.astype(F32)).astype(BF16)
        return _dot(hi, same_group) + _dot(mid, same_group) + _dot(lo, same_group)

    pieces = [one(x[:, LANES * j:LANES * (j + 1)]) for j in range(x.shape[1] // LANES)]
    return pieces[0] if len(pieces) == 1 else jnp.concatenate(pieces, axis=1)


def _dot(a, b):
    return jnp.dot(a, b, preferred_element_type=F32)


def _dot_nt(a, b):
    return lax.dot_general(a, b, (((1,), (1,)), ((), ())), preferred_element_type=F32)


def _dot_tn(a, b):
    return lax.dot_general(a, b, (((0,), (0,)), ((), ())), preferred_element_type=F32)


DIAGS = 768


def _diag_onehot():
    r_idx = lax.broadcasted_iota(jnp.int32, (384, DIAGS), 0)
    t_idx = lax.broadcasted_iota(jnp.int32, (384, DIAGS), 1)
    dist = (8 * CHUNK + 127) - t_idx
    return (jnp.clip(dist, -REL_CLIP, REL_CLIP) + REL_CLIP == r_idx).astype(F32)


def _shift_rows(x, reverse):
    row = lax.broadcasted_iota(jnp.int32, x.shape, 0)
    for k in range(7):
        amt = (DIAGS - (1 << k)) if reverse else (1 << k)
        x = jnp.where(((row >> k) & 1) == 1, pltpu.roll(x, amt, 1), x)
    return x


N_TABLES = 1 + ATT_SUBS


def _bias_table(rel_bias_pad):
    def body(rb_ref, out_ref):
        per_diag = jnp.dot(rb_ref[...], _diag_onehot(), preferred_element_type=F32,
                           precision=lax.Precision.HIGHEST)
        a = lax.broadcasted_iota(jnp.int32, (128, BAND_KEYS), 0)
        b = lax.broadcasted_iota(jnp.int32, (128, BAND_KEYS), 1)
        band = (b // CHUNK >= a // CHUNK) & (b // CHUNK <= a // CHUNK + 8)
        for h in range(8):
            rows = jnp.broadcast_to(per_diag[h:h + 1, :], (128, DIAGS))
            table = _shift_rows(pltpu.roll(rows, DIAGS - 127, 1), reverse=False)[:, :BAND_KEYS]
            out_ref[0, h] = jnp.where(band, table, NEG)
            for s in range(ATT_SUBS):
                out_ref[1 + s, h] = jnp.where(band & (b >= 8 * CHUNK - 128 * s), table, NEG)

    return pl.pallas_call(
        body, name="bias_table",
        out_shape=jax.ShapeDtypeStruct((N_TABLES, 8, 128, BAND_KEYS), F32),
        in_specs=[pl.BlockSpec(memory_space=pltpu.VMEM)],
        out_specs=pl.BlockSpec(memory_space=pltpu.VMEM),
        compiler_params=pltpu.CompilerParams(vmem_limit_bytes=VMEM_LIMIT),
    )(rel_bias_pad)


def _bias_grad(ds_sum):
    def body(ds_ref, out_ref):
        sums = []
        for h in range(8):
            padded = jnp.concatenate([ds_ref[h], jnp.zeros((128, DIAGS - BAND_KEYS), F32)], axis=1)
            skewed = pltpu.roll(_shift_rows(padded, reverse=True), 127, 1)
            sums.append(jnp.sum(skewed, axis=0, keepdims=True))
        per_diag = jnp.concatenate(sums, axis=0)
        out_ref[...] = lax.dot_general(per_diag, _diag_onehot(), (((1,), (1,)), ((), ())),
                                       preferred_element_type=F32, precision=lax.Precision.HIGHEST)

    return pl.pallas_call(
        body, name="bias_grad",
        out_shape=jax.ShapeDtypeStruct((8, 384), F32),
        in_specs=[pl.BlockSpec(memory_space=pltpu.VMEM)],
        out_specs=pl.BlockSpec(memory_space=pltpu.VMEM),
    )(ds_sum)


def _in_proj(x, g1, w_in_t, plans=(), tm=512):
    S = x.shape[0]

    def body(x_ref, g_ref, w_ref, qkv_ref, uv_ref, gate_ref, h_ref):
        xv = x_ref[...]
        h = (xv * _rstd(xv) * g_ref[...]).astype(BF16)
        h_ref[...] = h
        for c in range(IN_COLS // 512):
            zc = _dot_nt(h, w_ref[512 * c:512 * (c + 1), :])
            if c == 0:
                qkv_ref[:, 0:512] = (zc * 0.125).astype(BF16)
            elif c < 3:
                qkv_ref[:, 512 * c:512 * (c + 1)] = zc.astype(BF16)
            elif c < 5:
                uv_ref[:, 512 * (c - 3):512 * (c - 2)] = zc.astype(BF16)
            else:
                gate_ref[:, 512 * (c - 5):512 * (c - 4)] = zc.astype(BF16)

    return _call(
        body, name="in_proj", grid=(S // tm,),
        out_shape=(jax.ShapeDtypeStruct((S, 3 * ATT_W), BF16), jax.ShapeDtypeStruct((S, 2 * SG_W), BF16),
                   jax.ShapeDtypeStruct((S, 2 * D), BF16), jax.ShapeDtypeStruct((S, D), BF16)),
        in_specs=[_rows(tm, D), _resident((1, D)), _resident((IN_COLS, D))],
        out_specs=(_rows(tm, 3 * ATT_W), _rows(tm, 2 * SG_W), _rows(tm, 2 * D), _rows(tm, D)),
        operands=(x, g1, w_in_t), plans=plans)


def _two_heads(a, lo):
    return jnp.concatenate([jnp.where(lo, a, 0), jnp.where(lo, 0, a)], axis=0)


def _att_specs():
    R = ATT_R
    q = pl.BlockSpec((R, LANES), lambda j, i: (i, j))
    kp = pl.BlockSpec((R, LANES), lambda j, i: (jnp.maximum(i - 1, 0), 4 + j))
    kc = pl.BlockSpec((R, LANES), lambda j, i: (i, 4 + j))
    vp = pl.BlockSpec((R, LANES), lambda j, i: (jnp.maximum(i - 1, 0), 8 + j))
    vc = pl.BlockSpec((R, LANES), lambda j, i: (i, 8 + j))
    bias = pl.BlockSpec((N_TABLES, 2, 128, BAND_KEYS), lambda j, i: (0, j, 0, 0))
    return [q, kp, kc, vp, vc, bias]


def _attn_fwd(qkv, bias, plans=()):
    S = qkv.shape[0]
    R = ATT_R

    def body(q_ref, kp_ref, kc_ref, vp_ref, vc_ref, b_ref, o_ref, lse_ref):
        i = pl.program_id(1)
        lo = lax.broadcasted_iota(jnp.int32, (1, LANES), 1) < 64
        kwin = jnp.concatenate([kp_ref[...], kc_ref[...]], axis=0)
        vwin = jnp.concatenate([vp_ref[...], vc_ref[...]], axis=0)
        for sub in range(ATT_SUBS):
            q2 = q_ref[128 * sub:128 * (sub + 1), :]
            kw = kwin[128 * sub:128 * sub + BAND_KEYS]
            vw = vwin[128 * sub:128 * sub + BAND_KEYS]
            table = jnp.where(i == 0, 1 + sub, 0)
            s = _dot_nt(_two_heads(q2, lo), kw) + b_ref[table].reshape(256, BAND_KEYS)
            top = jnp.max(s, axis=-1, keepdims=True)
            p = jnp.exp(s - top)
            total = jnp.sum(p, axis=-1, keepdims=True)
            p = (p / total).astype(BF16)
            o = _dot(jnp.concatenate([p[:128], p[128:]], axis=1), _two_heads(vw, lo))
            lse = top + jnp.log(total)
            o_ref[128 * sub:128 * (sub + 1), :] = o.astype(BF16)
            lse_ref[128 * sub:128 * (sub + 1), :] = jnp.where(lo, lse[:128], lse[128:])

    blk = pl.BlockSpec((R, LANES), lambda j, i: (i, j))
    return _call(
        body, name="attn_fwd", grid=(4, S // R),
        out_shape=(jax.ShapeDtypeStruct((S, ATT_W), BF16), jax.ShapeDtypeStruct((S, ATT_W), F32)),
        in_specs=_att_specs(), out_specs=(blk, blk),
        operands=(qkv, qkv, qkv, qkv, qkv, bias), plans=plans)


def _sg_mask():
    t = lax.broadcasted_iota(jnp.int32, (128, 128), 0)
    s = lax.broadcasted_iota(jnp.int32, (128, 128), 1)
    return (s // CHUNK) <= (t // CHUNK)


def _sg_layernorm(gv, lng, lnb):
    mu = _group_sum64(gv) * (1.0 / 64)
    xc = gv - mu
    var = _group_sum64(xc * xc) * (1.0 / 64)
    rstd = lax.rsqrt(var + EPS)
    vhat = xc * rstd
    return vhat * lng + lnb, vhat, rstd


def _sgu_fwd(uv, lng, lnb, sg_w, b_exp, tm=512):
    S = uv.shape[0]

    def body(uv_ref, lng_ref, lnb_ref, w_ref, b_ref, y_ref):
        lane = lax.broadcasted_iota(jnp.int32, (1, LANES), 1)
        lo = lane < 64
        mask = _sg_mask()
        gu = _gelu(uv_ref[:, 0:SG_W].astype(F32))
        vln, _, _ = _sg_layernorm(_gelu(uv_ref[:, SG_W:2 * SG_W].astype(F32)), lng_ref[...], lnb_ref[...])
        for gp in range(4):
            w0 = jnp.where(mask, w_ref[2 * gp], 0).astype(BF16)
            w1 = jnp.where(mask, w_ref[2 * gp + 1], 0).astype(BF16)
            cols = slice(128 * gp, 128 * (gp + 1))
            for n in range(tm // 128):
                rows = slice(128 * n, 128 * (n + 1))
                vl = vln[rows, cols]
                sv = (_dot(w0, jnp.where(lo, vl, 0).astype(BF16)) + _dot(w1, jnp.where(lo, 0, vl).astype(BF16))
                      + b_ref[:, cols])
                y_ref[rows, cols] = (gu[rows, cols] * sv).astype(BF16)

    return pl.pallas_call(
        body, name="sgu_fwd", grid=(S // tm,),
        out_shape=jax.ShapeDtypeStruct((S, SG_W), BF16),
        in_specs=[_rows(tm, 2 * SG_W), _resident((1, SG_W)), _resident((1, SG_W)),
                  _resident((8, 128, 128)), _resident((128, SG_W))],
        out_specs=_rows(tm, SG_W),
        compiler_params=_cparams(1),
    )(uv, lng, lnb, sg_w, b_exp)


def _merge_fwd(x, y_att, y_sg, gates, wba_t, wbs_t, w_out, tm=512):
    S = x.shape[0]

    def body(x_ref, ya_ref, ys_ref, g_ref, wba_ref, wbs_ref, wo_ref, x1_ref, a_ref, b_ref, mg_ref):
        a = _dot_nt(ya_ref[...], wba_ref[...])
        b = _dot_nt(ys_ref[...], wbs_ref[...])
        a_ref[...] = a.astype(BF16)
        b_ref[...] = b.astype(BF16)
        merged = (_sigmoid(g_ref[:, 0:D].astype(F32)) * a + _sigmoid(g_ref[:, D:2 * D].astype(F32)) * b).astype(BF16)
        mg_ref[...] = merged
        x1_ref[...] = x_ref[...] + _dot(merged, wo_ref[...])

    return pl.pallas_call(
        body, name="merge_fwd", grid=(S // tm,),
        out_shape=(jax.ShapeDtypeStruct((S, D), F32),) + (jax.ShapeDtypeStruct((S, D), BF16),) * 3,
        in_specs=[_rows(tm, D), _rows(tm, ATT_W), _rows(tm, SG_W), _rows(tm, 2 * D),
                  _resident((D, ATT_W)), _resident((D, SG_W)), _resident((D, D))],
        out_specs=(_rows(tm, D),) * 4,
        compiler_params=_cparams(1),
    )(x, y_att, y_sg, gates, wba_t, wbs_t, w_out)


def _mem_kv(mem, g_mem, w_xkv_t):
    def body(m_ref, g_ref, w_ref, kv_ref, mn_ref):
        mv = m_ref[...]
        mn = (mv * _rstd(mv) * g_ref[...]).astype(BF16)
        mn_ref[...] = mn
        kv_ref[...] = _dot_nt(mn, w_ref[...]).astype(BF16)

    vm = pl.BlockSpec(memory_space=pltpu.VMEM)
    return pl.pallas_call(
        body, name="mem_kv",
        out_shape=(jax.ShapeDtypeStruct((MEM, 2 * D), BF16), jax.ShapeDtypeStruct((MEM, D), BF16)),
        in_specs=[vm, vm, vm], out_specs=(vm, vm),
        compiler_params=pltpu.CompilerParams(vmem_limit_bytes=VMEM_LIMIT),
    )(mem, g_mem, w_xkv_t)


def _xatt_head(qx, kv_ref, h):
    hs = slice(256 * h, 256 * (h + 1))
    s = _dot_nt(qx[:, hs], kv_ref[:, hs])
    p = jnp.exp(s - jnp.max(s, axis=-1, keepdims=True))
    return p / jnp.sum(p, axis=-1, keepdims=True)


def _xattn_fwd(x1, g2, w_xq, kv, w_xo, tm=512):
    S = x1.shape[0]

    def body(x_ref, g_ref, wq_ref, kv_ref, wo_ref, x2_ref, hx_ref, qx_ref, o_ref):
        xv = x_ref[...]
        hx = (xv * _rstd(xv) * g_ref[...]).astype(BF16)
        hx_ref[...] = hx
        qx = (_dot(hx, wq_ref[...]) * (1.0 / 16)).astype(BF16)
        qx_ref[...] = qx
        for h in range(XH):
            p = _xatt_head(qx, kv_ref, h)
            o_ref[:, 256 * h:256 * (h + 1)] = _dot(p.astype(BF16), kv_ref[:, D + 256 * h:D + 256 * (h + 1)]).astype(BF16)
        x2_ref[...] = xv + _dot(o_ref[...], wo_ref[...])

    return pl.pallas_call(
        body, name="xattn_fwd", grid=(S // tm,),
        out_shape=(jax.ShapeDtypeStruct((S, D), F32),) + (jax.ShapeDtypeStruct((S, D), BF16),) * 3,
        in_specs=[_rows(tm, D), _resident((1, D)), _resident((D, D)), _resident((MEM, 2 * D)), _resident((D, D))],
        out_specs=(_rows(tm, D),) * 4,
        compiler_params=_cparams(1),
    )(x1, g2, w_xq, kv, w_xo)


FF_CHUNK = 1408


def _ffn_fwd(x2, tgt, g3, w_ffn_in_t, w_ffn_out, g4, tm=256):
    S = x2.shape[0]

    def body(x_ref, t_ref, g3_ref, wi_ref, wo_ref, g4_ref, dx3_ref, gu_ref, hf_ref, act_ref, loss_ref, dg4_ref):
        i = pl.program_id(0)
        xv = x_ref[...]
        hf = (xv * _rstd(xv) * g3_ref[...]).astype(BF16)
        hf_ref[...] = hf
        acc = xv
        for c in range(DFF // FF_CHUNK):
            cs = slice(FF_CHUNK * c, FF_CHUNK * (c + 1))
            us = slice(DFF + FF_CHUNK * c, DFF + FF_CHUNK * (c + 1))
            gate = _dot_nt(hf, wi_ref[cs, :])
            up = _dot_nt(hf, wi_ref[us, :])
            gu_ref[:, cs] = gate.astype(BF16)
            gu_ref[:, us] = up.astype(BF16)
            act = ((gate * _sigmoid(gate)) * up).astype(BF16)
            act_ref[:, cs] = act
            acc = acc + _dot(act, wo_ref[cs, :])
        r4 = _rstd(acc)
        g4 = g4_ref[...]
        diff = acc * r4 * g4 - t_ref[...]
        dy = diff * (1.0 / D)
        dx3, dg4 = _rms_bwd(dy, acc, r4, g4)
        dx3_ref[...] = dx3
        part = 0.5 * jnp.sum(jnp.mean(diff * diff, axis=-1, keepdims=True))

        @pl.when(i == 0)
        def _():
            loss_ref[...] = jnp.zeros_like(loss_ref)
            dg4_ref[...] = jnp.zeros_like(dg4_ref)

        loss_ref[...] += jnp.full(loss_ref.shape, part, F32)
        dg4_ref[...] += dg4

    return pl.pallas_call(
        body, name="ffn_fwd", grid=(S // tm,),
        out_shape=(jax.ShapeDtypeStruct((S, D), F32), jax.ShapeDtypeStruct((S, 2 * DFF), BF16),
                   jax.ShapeDtypeStruct((S, D), BF16), jax.ShapeDtypeStruct((S, DFF), BF16),
                   jax.ShapeDtypeStruct((8, LANES), F32), jax.ShapeDtypeStruct((1, D), F32)),
        in_specs=[_rows(tm, D), _rows(tm, D), _resident((1, D)), _resident((2 * DFF, D)), _resident((DFF, D)),
                  _resident((1, D))],
        out_specs=(_rows(tm, D), _rows(tm, 2 * DFF), _rows(tm, D), _rows(tm, DFF),
                   pl.BlockSpec((8, LANES), lambda i: (0, 0)), pl.BlockSpec((1, D), lambda i: (0, 0))),
        compiler_params=_cparams(1),
    )(x2, tgt, g3, w_ffn_in_t, w_ffn_out, g4)


def _ffn_bwd(dx3, gu, x2, g3, w_ffn_out, w_ffn_in_t, tm=256):
    S = x2.shape[0]

    def body(d_ref, gu_ref, x_ref, g3_ref, wo_ref, wit_ref, dx2_ref, dgu_ref, dg3_ref):
        i = pl.program_id(0)
        d3 = d_ref[...]
        d3b = d3.astype(BF16)
        for c in range(DFF // FF_CHUNK):
            cs = slice(FF_CHUNK * c, FF_CHUNK * (c + 1))
            us = slice(DFF + FF_CHUNK * c, DFF + FF_CHUNK * (c + 1))
            da = _dot_nt(d3b, wo_ref[cs, :])
            gate = gu_ref[:, cs].astype(F32)
            up = gu_ref[:, us].astype(F32)
            sg = _sigmoid(gate)
            dgate = (da * up * (sg * (1.0 + gate * (1.0 - sg)))).astype(BF16)
            dup = (da * (gate * sg)).astype(BF16)
            dgu_ref[:, cs] = dgate
            dgu_ref[:, us] = dup
        dhf = _dot(dgu_ref[...], wit_ref[...])
        xv = x_ref[...]
        dx, dg3 = _rms_bwd(dhf, xv, _rstd(xv), g3_ref[...])
        dx2_ref[...] = d3 + dx

        @pl.when(i == 0)
        def _():
            dg3_ref[...] = jnp.zeros_like(dg3_ref)

        dg3_ref[...] += dg3

    return pl.pallas_call(
        body, name="ffn_bwd", grid=(S // tm,),
        out_shape=(jax.ShapeDtypeStruct((S, D), F32), jax.ShapeDtypeStruct((S, 2 * DFF), BF16),
                   jax.ShapeDtypeStruct((1, D), F32)),
        in_specs=[_rows(tm, D), _rows(tm, 2 * DFF), _rows(tm, D), _resident((1, D)),
                  _resident((DFF, D)), _resident((2 * DFF, D))],
        out_specs=(_rows(tm, D), _rows(tm, 2 * DFF), pl.BlockSpec((1, D), lambda i: (0, 0))),
        compiler_params=_cparams(1),
    )(dx3, gu, x2, g3, w_ffn_out, w_ffn_in_t)


def _dw(a, b, tmm, tn, ts, name, out_dtype=BF16):
    S, M = a.shape
    N = b.shape[1]
    nk = S // ts

    def body(a_ref, b_ref, o_ref, acc_ref):
        k = pl.program_id(2)

        @pl.when(k == 0)
        def _():
            acc_ref[...] = jnp.zeros_like(acc_ref)

        acc_ref[...] += _dot_tn(a_ref[...].astype(BF16), b_ref[...].astype(BF16))

        @pl.when(k == nk - 1)
        def _():
            o_ref[...] = acc_ref[...].astype(out_dtype)

    return pl.pallas_call(
        body, name=name, grid=(M // tmm, N // tn, nk),
        out_shape=jax.ShapeDtypeStruct((M, N), out_dtype),
        in_specs=[pl.BlockSpec((ts, tmm), lambda m, n, k: (k, m)), pl.BlockSpec((ts, tn), lambda m, n, k: (k, n))],
        out_specs=pl.BlockSpec((tmm, tn), lambda m, n, k: (m, n)),
        scratch_shapes=[pltpu.VMEM((tmm, tn), F32)],
        compiler_params=_cparams(3),
    )(a, b)


def _xattn_bwd(dx2, x1, qx, g2, w_xq, w_xo, kv, plans=(), tm=512):
    S = x1.shape[0]

    def body(d_ref, x_ref, qx_ref, g_ref, wq_ref, wo_ref, kv_ref, dx1_ref, dq_ref, dkv_ref, dg2_ref):
        i = pl.program_id(0)

        @pl.when(i == 0)
        def _():
            dkv_ref[...] = jnp.zeros_like(dkv_ref)
            dg2_ref[...] = jnp.zeros_like(dg2_ref)

        d2 = d_ref[...]
        qx = qx_ref[...]
        do = _dot_nt(d2.astype(BF16), wo_ref[...]).astype(BF16)
        for h in range(XH):
            hs = slice(256 * h, 256 * (h + 1))
            vs = slice(D + 256 * h, D + 256 * (h + 1))
            p = _xatt_head(qx, kv_ref, h)
            dp = _dot_nt(do[:, hs], kv_ref[:, vs])
            ds = (p * (dp - jnp.sum(dp * p, axis=-1, keepdims=True))).astype(BF16)
            dq_ref[:, hs] = (_dot(ds, kv_ref[:, hs]) * (1.0 / 16)).astype(BF16)
            dkv_ref[:, hs] += _dot_tn(ds, qx[:, hs])
            dkv_ref[:, vs] += _dot_tn(p.astype(BF16), do[:, hs])
        dhx = _dot_nt(dq_ref[...], wq_ref[...])
        xv = x_ref[...]
        dx, dg2 = _rms_bwd(dhx, xv, _rstd(xv), g_ref[...])
        dx1_ref[...] = d2 + dx
        dg2_ref[...] += dg2

    return _call(
        body, name="xattn_bwd", grid=(S // tm,),
        out_shape=(jax.ShapeDtypeStruct((S, D), F32), jax.ShapeDtypeStruct((S, D), BF16),
                   jax.ShapeDtypeStruct((MEM, 2 * D), F32), jax.ShapeDtypeStruct((1, D), F32)),
        in_specs=[_rows(tm, D), _rows(tm, D), _rows(tm, D), _resident((1, D)), _resident((D, D)), _resident((D, D)),
                  _resident((MEM, 2 * D))],
        out_specs=(_rows(tm, D), _rows(tm, D),
                   pl.BlockSpec((MEM, 2 * D), lambda i: (0, 0)), pl.BlockSpec((1, D), lambda i: (0, 0))),
        operands=(dx2, x1, qx, g2, w_xq, w_xo, kv), plans=plans)


def _mem_kv_bwd(dkv, mem, g_mem, mn, w_xkv_t):
    def body(dkv_ref, m_ref, g_ref, mn_ref, wt_ref, dw_ref, dg_ref):
        dkvb = dkv_ref[...].astype(BF16)
        dw_ref[...] = _dot_tn(dkvb, mn_ref[...]).astype(BF16)
        dmn = _dot(dkvb, wt_ref[...])
        mv = m_ref[...]
        dg_ref[...] = jnp.sum(dmn * (mv * _rstd(mv)), axis=0, keepdims=True)

    vm = pl.BlockSpec(memory_space=pltpu.VMEM)
    return pl.pallas_call(
        body, name="mem_kv_bwd",
        out_shape=(jax.ShapeDtypeStruct((2 * D, D), BF16), jax.ShapeDtypeStruct((1, D), F32)),
        in_specs=[vm] * 5, out_specs=(vm, vm),
        compiler_params=pltpu.CompilerParams(vmem_limit_bytes=VMEM_LIMIT),
    )(dkv, mem, g_mem, mn, w_xkv_t)


def _merge_bwd(dx1, a, b, gates, wba_t, wbs_t, w_out, tm=512):
    S = dx1.shape[0]

    def body(d_ref, a_ref, b_ref, g_ref, wbat_ref, wbst_ref, wo_ref, da_ref, db_ref, dya_ref, dys_ref, dg_ref):
        dm = _dot_nt(d_ref[...].astype(BF16), wo_ref[...])
        a = a_ref[...].astype(F32)
        b = b_ref[...].astype(F32)
        sa = _sigmoid(g_ref[:, 0:D].astype(F32))
        sb = _sigmoid(g_ref[:, D:2 * D].astype(F32))
        da = (dm * sa).astype(BF16)
        db = (dm * sb).astype(BF16)
        da_ref[...] = da
        db_ref[...] = db
        dg_ref[:, 0:D] = (dm * a * sa * (1.0 - sa)).astype(BF16)
        dg_ref[:, D:2 * D] = (dm * b * sb * (1.0 - sb)).astype(BF16)
        dya_ref[...] = _dot(da, wbat_ref[...]).astype(BF16)
        dys_ref[...] = _dot(db, wbst_ref[...]).astype(BF16)

    return pl.pallas_call(
        body, name="merge_bwd", grid=(S // tm,),
        out_shape=(jax.ShapeDtypeStruct((S, D), BF16), jax.ShapeDtypeStruct((S, D), BF16),
                   jax.ShapeDtypeStruct((S, ATT_W), BF16),
                   jax.ShapeDtypeStruct((S, SG_W), BF16), jax.ShapeDtypeStruct((S, 2 * D), BF16)),
        in_specs=[_rows(tm, D), _rows(tm, D), _rows(tm, D), _rows(tm, 2 * D),
                  _resident((D, ATT_W)), _resident((D, SG_W)), _resident((D, D))],
        out_specs=(_rows(tm, D), _rows(tm, D), _rows(tm, ATT_W), _rows(tm, SG_W), _rows(tm, 2 * D)),
        compiler_params=_cparams(1),
    )(dx1, a, b, gates, wba_t, wbs_t, w_out)


def _sgu_bwd(uv, dy_sg, lng, lnb, sg_w, b_exp, plans=(), tm=512):
    S = uv.shape[0]
    n_steps = S // tm

    def body(uv_ref, dy_ref, lng_ref, lnb_ref, w_ref, b_ref, duv_ref, dw_ref, dbx_ref, dlng_ref, dlnb_ref, dvln_ref):
        i = pl.program_id(0)

        @pl.when(i == 0)
        def _():
            dw_ref[...] = jnp.zeros_like(dw_ref)
            dbx_ref[...] = jnp.zeros_like(dbx_ref)
            dlng_ref[...] = jnp.zeros_like(dlng_ref)
            dlnb_ref[...] = jnp.zeros_like(dlnb_ref)

        lane = lax.broadcasted_iota(jnp.int32, (1, LANES), 1)
        lo = lane < 64
        mask = _sg_mask()
        lng = lng_ref[...]
        gu, dgelu_u = _gelu_and_grad(uv_ref[:, 0:SG_W].astype(F32))
        gv, dgelu_v = _gelu_and_grad(uv_ref[:, SG_W:2 * SG_W].astype(F32))
        vln, vhat, rstd = _sg_layernorm(gv, lng, lnb_ref[...])
        dy = dy_ref[...].astype(F32)
        dsv_all = dy * gu
        for gp in range(4):
            wf0 = jnp.where(mask, w_ref[2 * gp], 0)
            wf1 = jnp.where(mask, w_ref[2 * gp + 1], 0)
            w0 = wf0.astype(BF16)
            w1 = wf1.astype(BF16)
            cols = slice(128 * gp, 128 * (gp + 1))
            dw0 = jnp.zeros((128, 128), F32)
            dw1 = jnp.zeros((128, 128), F32)
            dbx = jnp.zeros((128, LANES), F32)
            for n in range(tm // 128):
                rows = slice(128 * n, 128 * (n + 1))
                vl = vln[rows, cols]
                vl0 = jnp.where(lo, vl, 0).astype(BF16)
                vl1 = jnp.where(lo, 0, vl).astype(BF16)
                sv = _dot(w0, vl0) + _dot(w1, vl1) + b_ref[:, cols]
                duv_ref[rows, cols] = (dy[rows, cols] * sv * dgelu_u[rows, cols]).astype(BF16)
                dsv = dsv_all[rows, cols]
                dbx = dbx + dsv
                ds0 = jnp.where(lo, dsv, 0).astype(BF16)
                ds1 = jnp.where(lo, 0, dsv).astype(BF16)
                dw0 = dw0 + _dot_nt(ds0, vl0)
                dw1 = dw1 + _dot_nt(ds1, vl1)
                dvln_ref[rows, cols] = _dot_tn(w0, ds0) + _dot_tn(w1, ds1)
            dw_ref[2 * gp] += jnp.where(mask, dw0, 0)
            dw_ref[2 * gp + 1] += jnp.where(mask, dw1, 0)
            dbx_ref[:, cols] += dbx
        dvln = dvln_ref[...]
        dlng_ref[...] += jnp.sum(dvln * vhat, axis=0, keepdims=True)
        dlnb_ref[...] += jnp.sum(dvln, axis=0, keepdims=True)
        dvh = dvln * lng
        dgv = rstd * (dvh - _group_sum64(dvh) * (1.0 / 64) - vhat * (_group_sum64(dvh * vhat) * (1.0 / 64)))
        duv_ref[:, SG_W:2 * SG_W] = (dgv * dgelu_v).astype(BF16)

        @pl.when(i == n_steps - 1)
        def _():
            dbx_ref[...] = _group_sum64(dbx_ref[...])

    return _call(
        body, name="sgu_bwd", grid=(n_steps,),
        out_shape=(jax.ShapeDtypeStruct((S, 2 * SG_W), BF16), jax.ShapeDtypeStruct((8, 128, 128), F32),
                   jax.ShapeDtypeStruct((128, SG_W), F32), jax.ShapeDtypeStruct((1, SG_W), F32),
                   jax.ShapeDtypeStruct((1, SG_W), F32)),
        in_specs=[_rows(tm, 2 * SG_W), _rows(tm, SG_W), _resident((1, SG_W)), _resident((1, SG_W)),
                  _resident((8, 128, 128)), _resident((128, SG_W))],
        out_specs=(_rows(tm, 2 * SG_W), pl.BlockSpec((8, 128, 128), lambda i: (0, 0, 0)),
                   pl.BlockSpec((128, SG_W), lambda i: (0, 0)), pl.BlockSpec((1, SG_W), lambda i: (0, 0)),
                   pl.BlockSpec((1, SG_W), lambda i: (0, 0))),
        scratch_shapes=[pltpu.VMEM((tm, SG_W), F32)],
        operands=(uv, dy_sg, lng, lnb, sg_w, b_exp), plans=plans)


def _attn_bwd(qkv, dy_att, y_att, lse, bias, plans=()):
    S = qkv.shape[0]
    R = ATT_R

    def body(q_ref, kp_ref, kc_ref, vp_ref, vc_ref, b_ref, dy_ref, y_ref, lse_ref, dq_ref, dk_ref, dv_ref, dss_ref):
        i = pl.program_id(1)

        @pl.when(i == 0)
        def _():
            dk_ref[...] = jnp.zeros_like(dk_ref)
            dv_ref[...] = jnp.zeros_like(dv_ref)
            dss_ref[...] = jnp.zeros_like(dss_ref)

        lo = lax.broadcasted_iota(jnp.int32, (1, LANES), 1) < 64
        kwin = jnp.concatenate([kp_ref[...], kc_ref[...]], axis=0)
        vwin = jnp.concatenate([vp_ref[...], vc_ref[...]], axis=0)
        for sub in range(ATT_SUBS):
            rows = slice(128 * sub, 128 * (sub + 1))
            kw = kwin[128 * sub:128 * sub + BAND_KEYS]
            vw = vwin[128 * sub:128 * sub + BAND_KEYS]
            table = jnp.where(i == 0, 1 + sub, 0)
            qs = _two_heads(q_ref[rows, :], lo)
            dos = _two_heads(dy_ref[rows, :], lo)
            dyy = dy_ref[rows, :].astype(F32) * y_ref[rows, :].astype(F32)
            delta = jnp.concatenate([jnp.sum(jnp.where(lo, dyy, 0.0), axis=-1, keepdims=True),
                                     jnp.sum(jnp.where(lo, 0.0, dyy), axis=-1, keepdims=True)], axis=0)
            lse2 = lse_ref[rows, :]
            lse = jnp.concatenate([lse2[:, 0:1], lse2[:, 64:65]], axis=0)
            p = jnp.exp(_dot_nt(qs, kw) + b_ref[table].reshape(256, BAND_KEYS) - lse)
            ds = p * (_dot_nt(dos, vw) - delta)
            dss_ref[...] += ds.reshape(2, 128, BAND_KEYS)
            dsb = ds.astype(BF16)
            dq = _dot(jnp.concatenate([dsb[:128], dsb[128:]], axis=1), _two_heads(kw, lo))
            dq_ref[rows, :] = dq.astype(BF16)
            start = pl.multiple_of(i * R + 128 * sub, 128)
            dk_ref[pl.ds(start, BAND_KEYS), :] += _dot_tn(dsb, qs)
            dv_ref[pl.ds(start, BAND_KEYS), :] += _dot_tn(p.astype(BF16), dos)

    acc_spec = pl.BlockSpec((S + 8 * CHUNK, LANES), lambda j, i: (0, j))
    return _call(
        body, name="attn_bwd", grid=(4, S // R),
        out_shape=(jax.ShapeDtypeStruct((S, ATT_W), BF16), jax.ShapeDtypeStruct((S + 8 * CHUNK, ATT_W), F32),
                   jax.ShapeDtypeStruct((S + 8 * CHUNK, ATT_W), F32), jax.ShapeDtypeStruct((8, 128, BAND_KEYS), F32)),
        in_specs=_att_specs() + [pl.BlockSpec((R, LANES), lambda j, i: (i, j))] * 3,
        out_specs=(pl.BlockSpec((R, LANES), lambda j, i: (i, j)), acc_spec, acc_spec,
                   pl.BlockSpec((2, 128, BAND_KEYS), lambda j, i: (j, 0, 0))),
        operands=(qkv, qkv, qkv, qkv, qkv, bias, dy_att, y_att, lse), plans=plans)


def _in_bwd(dq, dk, dv, duv, dgates, x, dx1, g1, w_in_t, tm=512):
    S = x.shape[0]
    pad_blocks = (8 * CHUNK) // tm

    def body(dq_ref, dk_ref, dv_ref, duv_ref, dg_ref, x_ref, d1_ref, g_ref, wt_ref, dx_ref, dz_ref, dg1_ref):
        i = pl.program_id(0)
        dz_ref[:, 0:ATT_W] = (dq_ref[...].astype(F32) * 0.125).astype(BF16)
        dz_ref[:, ATT_W:2 * ATT_W] = dk_ref[...].astype(BF16)
        dz_ref[:, 2 * ATT_W:3 * ATT_W] = dv_ref[...].astype(BF16)
        dz_ref[:, 3 * ATT_W:3 * ATT_W + 2 * SG_W] = duv_ref[...]
        dz_ref[:, 3 * ATT_W + 2 * SG_W:IN_COLS] = dg_ref[...]
        dh = _dot(dz_ref[...], wt_ref[...])
        xv = x_ref[...]
        dx, dg1 = _rms_bwd(dh, xv, _rstd(xv), g_ref[...])
        dx_ref[...] = d1_ref[...] + dx

        @pl.when(i == 0)
        def _():
            dg1_ref[...] = jnp.zeros_like(dg1_ref)

        dg1_ref[...] += dg1

    shifted = pl.BlockSpec((tm, ATT_W), lambda i: (i + pad_blocks, 0))
    return pl.pallas_call(
        body, name="in_bwd", grid=(S // tm,),
        out_shape=(jax.ShapeDtypeStruct((S, D), F32), jax.ShapeDtypeStruct((S, IN_COLS), BF16),
                   jax.ShapeDtypeStruct((1, D), F32)),
        in_specs=[_rows(tm, ATT_W), shifted, shifted, _rows(tm, 2 * SG_W), _rows(tm, 2 * D), _rows(tm, D),
                  _rows(tm, D), _resident((1, D)), _resident((IN_COLS, D))],
        out_specs=(_rows(tm, D), _rows(tm, IN_COLS), pl.BlockSpec((1, D), lambda i: (0, 0))),
        compiler_params=_cparams(1),
    )(dq, dk, dv, duv, dgates, x, dx1, g1, w_in_t)


def _adam_math(w, g, m, v):
    m = ADAM_B1 * m + (1.0 - ADAM_B1) * g
    v = ADAM_B2 * v + (1.0 - ADAM_B2) * (g * g)
    m_hat = m / (1.0 - ADAM_B1 ** ADAM_STEP)
    v_hat = v / (1.0 - ADAM_B2 ** ADAM_STEP)
    delta = -ADAM_LR * (m_hat / (jnp.sqrt(v_hat) + ADAM_EPS) + ADAM_WD * w)
    return delta, m, v


def _adam(parts, w, m, v, tr, name, transposed):
    P = parts.shape[0]
    R, C = w.shape

    def body(p_ref, w_ref, m_ref, v_ref, g_ref, d_ref, mo_ref, vo_ref):
        if transposed:
            eye = (lax.broadcasted_iota(jnp.int32, (C, C), 0) == lax.broadcasted_iota(jnp.int32, (C, C), 1)).astype(BF16)
            part = lambda k: _dot_tn(p_ref[k], eye)
        else:
            part = lambda k: p_ref[k].astype(F32)
        g = part(0)
        for k in range(1, P):
            g = g + part(k)
        g_ref[...] = g
        d_ref[...], mo_ref[...], vo_ref[...] = _adam_math(w_ref[...], g, m_ref[...], v_ref[...])

    row = pl.BlockSpec((tr, C), lambda i: (i, 0))
    p_spec = pl.BlockSpec((P, C, tr), lambda i: (0, 0, i)) if transposed else pl.BlockSpec((P, tr, C), lambda i: (0, i, 0))
    return pl.pallas_call(
        body, name=name, grid=(R // tr,),
        out_shape=tuple(jax.ShapeDtypeStruct((R, C), F32) for _ in range(4)),
        in_specs=[p_spec, row, row, row],
        out_specs=(row, row, row, row),
        compiler_params=_cparams(1),
    )(parts, w, m, v)


def _my_place():
    return lax.axis_index("x"), lax.axis_index("y"), lax.axis_index("c")


def _other_chips(x, y):
    return [(1 - x, y), (x, 1 - y), (1 - x, 1 - y)]


class _Plan:
    def __init__(self, arrays, out_shapes, sems, start, finish, forward=None, forward_at=0.7):
        self.arrays, self.out_shapes, self.sems = list(arrays), list(out_shapes), list(sems)
        self.start, self.finish, self.forward, self.forward_at = start, finish, forward, forward_at


def _call(body, *, name, grid, in_specs, out_specs, out_shape, operands, scratch_shapes=(), plans=()):
    n_in, n_out, n_scr = len(operands), len(out_shape), len(scratch_shapes)
    p_in = [a for p in plans for a in p.arrays]
    p_out = [s for p in plans for s in p.out_shapes]
    p_sem = [s for p in plans for s in p.sems]
    steps = math.prod(grid)

    def wrapped(*refs):
        ins, refs = refs[:n_in], refs[n_in:]
        pins, refs = refs[:len(p_in)], refs[len(p_in):]
        outs, refs = refs[:n_out], refs[n_out:]
        pouts, refs = refs[:len(p_out)], refs[len(p_out):]
        scr, psems = refs[:n_scr], refs[n_scr:]
        step = 0
        for axis, size in enumerate(grid):
            step = step * size + pl.program_id(axis)
        bound = []
        for p in plans:
            bound.append((p, pins[:len(p.arrays)], pouts[:len(p.out_shapes)], psems[:len(p.sems)]))
            pins, pouts, psems = pins[len(p.arrays):], pouts[len(p.out_shapes):], psems[len(p.sems):]

        @pl.when(step == 0)
        def _():
            for p, a, b, s in bound:
                p.start(a, b, s)

        body(*ins, *outs, *scr)

        for p, a, b, s in bound:
            if p.forward is not None:
                @pl.when(step == min(int(p.forward_at * steps), steps - 1))
                def _(p=p, a=a, b=b, s=s):
                    p.forward(a, b, s)

        @pl.when(step == steps - 1)
        def _():
            for p, a, b, s in bound:
                p.finish(a, b, s)

    hbm = pl.BlockSpec(memory_space=pl.ANY)
    return pl.pallas_call(
        wrapped if plans else body, name=name, grid=grid,
        out_shape=tuple(out_shape) + tuple(p_out),
        in_specs=list(in_specs) + [hbm] * len(p_in),
        out_specs=tuple(out_specs) + tuple([hbm] * len(p_out)),
        scratch_shapes=list(scratch_shapes) + p_sem,
        compiler_params=_cparams(len(grid)),
    )(*operands, *p_in)


def _run_plan(plan, name):
    n_in, n_out = len(plan.arrays), len(plan.out_shapes)

    def body(*refs):
        a, b, s = refs[:n_in], refs[n_in:n_in + n_out], refs[n_in + n_out:]
        plan.start(a, b, s)
        if plan.forward is not None:
            plan.forward(a, b, s)
        plan.finish(a, b, s)

    hbm = pl.BlockSpec(memory_space=pl.ANY)
    return pl.pallas_call(
        body, name=name, out_shape=tuple(plan.out_shapes),
        in_specs=[hbm] * n_in, out_specs=tuple([hbm] * n_out), scratch_shapes=plan.sems,
    )(*plan.arrays)


def _gather_plan(shards, forward_at=0.7):
    n = len(shards)

    def copies(ins, outs, sems):
        send_sems, recv_sems, local_sems = sems
        x, y, c = _my_place()
        me, sibling = (x, y, c), (x, y, 1 - c)
        chips = _other_chips(x, y)

        def idx(p):
            return 4 * p[0] + 2 * p[1] + p[2]

        def copy(a, k, block, to, src=None):
            return pltpu.make_async_remote_copy(
                src_ref=outs[a].at[idx(block)] if src is None else src, dst_ref=outs[a].at[idx(block)],
                send_sem=send_sems.at[a, k], recv_sem=recv_sems.at[a, k], device_id=to, device_id_type=MESH)

        mine = [pltpu.make_async_copy(ins[a], outs[a].at[idx(me)], local_sems.at[a]) for a in range(n)]
        first = []
        for a in range(n):
            first.append(copy(a, 0, me, sibling, src=ins[a]))
            first += [copy(a, 1 + j, me, (*chip, c), src=ins[a]) for j, chip in enumerate(chips)]
        arrived = [copy(a, 1 + j, (*chip, c), me) for j, chip in enumerate(chips) for a in range(n)]
        passed = [copy(a, 4 + j, (*chip, c), sibling) for j, chip in enumerate(chips) for a in range(n)]
        from_sibling = []
        for a in range(n):
            from_sibling.append(copy(a, 0, sibling, me))
            from_sibling += [copy(a, 4 + j, (*chip, 1 - c), me) for j, chip in enumerate(chips)]
        return mine, first, arrived, passed, from_sibling

    def start(ins, outs, sems):
        mine, first, _, _, _ = copies(ins, outs, sems)
        for cp in mine + first:
            cp.start()

    def forward(ins, outs, sems):
        _, _, arrived, passed, _ = copies(ins, outs, sems)
        for landed, onward in zip(arrived, passed):
            landed.wait_recv()
            onward.start()

    def finish(ins, outs, sems):
        mine, first, _, passed, from_sibling = copies(ins, outs, sems)
        for cp in from_sibling:
            cp.wait_recv()
        for cp in first + passed:
            cp.wait_send()
        for cp in mine:
            cp.wait()

    return _Plan(shards, [jax.ShapeDtypeStruct((N_DEV,) + s.shape, s.dtype) for s in shards],
                 [pltpu.SemaphoreType.DMA((n, 7)), pltpu.SemaphoreType.DMA((n, 7)), pltpu.SemaphoreType.DMA((n,))],
                 start, finish, forward, forward_at)


def _sibling_plan(scatter, whole=()):
    ns = len(scatter)
    n = ns + len(whole)

    def copies(ins, outs, sems):
        send_sems, recv_sems = sems
        x, y, c = _my_place()
        out = []
        for a in range(n):
            for k in range(4 if a < ns else 1):
                src = ins[a].at[2 * k + (1 - c)] if a < ns else ins[a]
                dst = outs[a].at[k] if a < ns else outs[a]
                out.append(pltpu.make_async_remote_copy(
                    src_ref=src, dst_ref=dst, send_sem=send_sems.at[a, k], recv_sem=recv_sems.at[a, k],
                    device_id=(x, y, 1 - c), device_id_type=MESH))
        return out

    def start(ins, outs, sems):
        for cp in copies(ins, outs, sems):
            cp.start()

    def finish(ins, outs, sems):
        for cp in copies(ins, outs, sems):
            cp.wait()

    shapes = [jax.ShapeDtypeStruct((4,) + p.shape[1:], p.dtype) for p in scatter]
    shapes += [jax.ShapeDtypeStruct(p.shape, p.dtype) for p in whole]
    return _Plan(list(scatter) + list(whole), shapes,
                 [pltpu.SemaphoreType.DMA((n, 4)), pltpu.SemaphoreType.DMA((n, 4))], start, finish)


def _pair_sum(mine, theirs, c_idx, name):
    _, R, C = mine.shape

    def body(c_ref, a_ref, b_ref, o_ref):
        o_ref[...] = (a_ref[...].astype(F32) + b_ref[...].astype(F32)).astype(o_ref.dtype)

    grid_spec = pltpu.PrefetchScalarGridSpec(
        num_scalar_prefetch=1, grid=(4,),
        in_specs=[pl.BlockSpec((1, R, C), lambda k, c_ref: (2 * k + c_ref[0], 0, 0)),
                  pl.BlockSpec((1, R, C), lambda k, c_ref: (k, 0, 0))],
        out_specs=pl.BlockSpec((1, R, C), lambda k, c_ref: (k, 0, 0)))
    return pl.pallas_call(
        body, name=name, grid_spec=grid_spec,
        out_shape=jax.ShapeDtypeStruct((4, R, C), mine.dtype),
        compiler_params=_cparams(1),
    )(c_idx, mine, theirs)


def _pair_sum_whole(mine, theirs, name):
    n = len(mine)

    def body(*refs):
        for a in range(n):
            refs[2 * n + a][...] = refs[a][...] + refs[n + a][...]

    vm = pl.BlockSpec(memory_space=pltpu.VMEM)
    return pl.pallas_call(
        body, name=name, out_shape=tuple(jax.ShapeDtypeStruct(p.shape, p.dtype) for p in mine),
        in_specs=[vm] * (2 * n), out_specs=tuple([vm] * n),
    )(*mine, *theirs)


def _chips_plan(scatter, whole=()):
    ns = len(scatter)
    n = ns + len(whole)

    def copies(ins, outs, sems):
        send_sems, recv_sems, local_sems = sems
        x, y, c = _my_place()
        my_chip = 2 * x + y

        def src(a, k):
            return ins[a].at[k] if a < ns else ins[a]

        local = [pltpu.make_async_copy(src(a, my_chip), outs[a].at[my_chip], local_sems.at[a]) for a in range(n)]
        remote = []
        for a in range(n):
            for j, (px, py) in enumerate(_other_chips(x, y)):
                remote.append(pltpu.make_async_remote_copy(
                    src_ref=src(a, 2 * px + py), dst_ref=outs[a].at[my_chip],
                    send_sem=send_sems.at[a, j], recv_sem=recv_sems.at[a, j],
                    device_id=(px, py, c), device_id_type=MESH))
        return local + remote

    def start(ins, outs, sems):
        for cp in copies(ins, outs, sems):
            cp.start()

    def finish(ins, outs, sems):
        for cp in copies(ins, outs, sems):
            cp.wait()

    shapes = [jax.ShapeDtypeStruct(s.shape, s.dtype) for s in scatter]
    shapes += [jax.ShapeDtypeStruct((4,) + s.shape, s.dtype) for s in whole]
    return _Plan(list(scatter) + list(whole), shapes,
                 [pltpu.SemaphoreType.DMA((n, 3)), pltpu.SemaphoreType.DMA((n, 3)), pltpu.SemaphoreType.DMA((n,))],
                 start, finish)


def _small_update(parts, w, m, v, loss_parts, name):
    n = len(parts)

    def body(*refs):
        p_refs, w_refs, m_refs, v_refs = (refs[i * n:(i + 1) * n] for i in range(4))
        lp_ref = refs[4 * n]
        outs = refs[4 * n + 1:]
        g_refs, d_refs, mo_refs, vo_refs = (outs[i * n:(i + 1) * n] for i in range(4))
        for a in range(n):
            g = ((p_refs[a][0] + p_refs[a][1]) + p_refs[a][2]) + p_refs[a][3]
            g_refs[a][...] = g
            d_refs[a][...], mo_refs[a][...], vo_refs[a][...] = _adam_math(w_refs[a][...], g, m_refs[a][...],
                                                                          v_refs[a][...])
        outs[4 * n][...] = ((lp_ref[0] + lp_ref[1]) + lp_ref[2]) + lp_ref[3]

    vm = pl.BlockSpec(memory_space=pltpu.VMEM)
    shapes = [jax.ShapeDtypeStruct(t.shape, F32) for _ in range(4) for t in w]
    shapes.append(jax.ShapeDtypeStruct(loss_parts.shape[1:], F32))
    outs = pl.pallas_call(
        body, name=name, out_shape=tuple(shapes),
        in_specs=[vm] * (4 * n + 1), out_specs=tuple([vm] * (4 * n + 1)),
    )(*parts, *w, *m, *v, loss_parts)
    return outs[0:n], outs[n:2 * n], outs[2 * n:3 * n], outs[3 * n:4 * n], outs[4 * n]


BIG = [("w_in", 1), ("w_branch_att", 1), ("w_branch_sg", 1), ("w_out", 0), ("w_xq", 0), ("w_xkv", 1), ("w_xo", 0),
       ("w_ffn_in", 1), ("w_ffn_out", 0)]
SMALL = [("norm_mix_g", (1, D)), ("rel_bias", (8, NREL)), ("sg_ln_g", (8, 64)), ("sg_ln_b", (8, 64)),
         ("sg_w", (8, 128, 128)), ("sg_b", (8, 128)), ("norm_xattn_g", (1, D)), ("norm_mem_g", (1, D)),
         ("norm_ffn_g", (1, D)), ("norm_final_g", (1, D))]
ADAM_ROWS = {"w_in": 256, "w_branch_att": 512, "w_branch_sg": 512, "w_xkv": 1024, "w_ffn_in": 256}


def _full(gathered):
    return gathered.reshape(N_DEV * gathered.shape[1], gathered.shape[2])


def _blocks(grad):
    return grad.reshape(N_DEV, grad.shape[0] // N_DEV, grad.shape[1])


def kernel(x, mem, norm_mix_g, w_in, rel_bias, sg_ln_g, sg_ln_b, sg_w, sg_b, w_branch_att, w_branch_sg, w_out, norm_xattn_g, norm_mem_g, w_xq, w_xkv, w_xo, norm_ffn_g, w_ffn_in, w_ffn_out, norm_final_g, loss_target, m_norm_mix_g, m_w_in, m_rel_bias, m_sg_ln_g, m_sg_ln_b, m_sg_w, m_sg_b, m_w_branch_att, m_w_branch_sg, m_w_out, m_norm_xattn_g, m_norm_mem_g, m_w_xq, m_w_xkv, m_w_xo, m_norm_ffn_g, m_w_ffn_in, m_w_ffn_out, m_norm_final_g, v_norm_mix_g, v_w_in, v_rel_bias, v_sg_ln_g, v_sg_ln_b, v_sg_w, v_sg_b, v_w_branch_att, v_w_branch_sg, v_w_out, v_norm_xattn_g, v_norm_mem_g, v_w_xq, v_w_xkv, v_w_xo, v_norm_ffn_g, v_w_ffn_in, v_w_ffn_out, v_norm_final_g):
    args = dict(locals())
    big_names = [n for n, _ in BIG]
    small_names = [n for n, _ in SMALL]
    S = x.shape[1]

    x, mem, tgt = x.reshape(S, D), mem.reshape(MEM, D), loss_target.reshape(S, D)
    small = {n: args[n].reshape(shape) for n, shape in SMALL}
    g1, g2, g3 = small["norm_mix_g"], small["norm_xattn_g"], small["norm_ffn_g"]
    g_mem, g4 = small["norm_mem_g"], small["norm_final_g"]
    lng = small["sg_ln_g"].reshape(1, SG_W)
    lnb = small["sg_ln_b"].reshape(1, SG_W)
    b_exp = jnp.broadcast_to(small["sg_b"].T[:, :, None], (128, 8, 64)).reshape(128, SG_W)
    rel_pad = jnp.pad(small["rel_bias"], ((0, 0), (0, 384 - NREL)))
    c_idx = lax.axis_index("c").astype(jnp.int32).reshape(1)

    shard = {n: (args[n][0].T if axis == 1 else args[n][0]).astype(BF16) for n, axis in BIG}
    w_in_t = _full(_run_plan(_gather_plan([shard["w_in"]]), "ag_w_in")[0])
    bias = _bias_table(rel_pad)
    mix_names = ["w_branch_att", "w_branch_sg", "w_out", "w_xq", "w_xkv", "w_xo"]
    qkv, uv, gates, h, *got = _in_proj(x, g1, w_in_t, plans=[_gather_plan([shard[n] for n in mix_names])])
    wba_t, wbs_t, w_out_f, w_xq_f, w_xkv_t, w_xo_f = (_full(g) for g in got)
    y_att, lse, *got = _attn_fwd(qkv, bias, plans=[_gather_plan([shard["w_ffn_in"], shard["w_ffn_out"]])])
    w_ffn_in_t, w_ffn_out_f = (_full(g) for g in got)
    y_sg = _sgu_fwd(uv, lng, lnb, small["sg_w"], b_exp)
    x1, a_att, b_sg, merged = _merge_fwd(x, y_att, y_sg, gates, wba_t, wbs_t, w_out_f)
    kv, mn = _mem_kv(mem, g_mem, w_xkv_t)
    x2, hx, qx, o_x = _xattn_fwd(x1, g2, w_xq_f, kv, w_xo_f)
    dx3, gu, hf, act, loss_part, dg4 = _ffn_fwd(x2, tgt, g3, w_ffn_in_t, w_ffn_out_f, g4)

    def pair_sums(names, mine, theirs):
        return [_pair_sum(a, b, c_idx, "rs_pair_" + n) for n, a, b in zip(names, mine, theirs)]

    dx2, dgu, dg3 = _ffn_bwd(dx3, gu, x2, g3, w_ffn_out_f, w_ffn_in_t)
    ffn_names = ["w_ffn_out", "w_ffn_in"]
    ffn_mine = [_blocks(_dw(act, dx3, 1408, 1024, DW_TOKENS, "dw_ffn_out")),
                _blocks(_dw(dgu, hf, 1408, 1024, DW_TOKENS, "dw_ffn_in"))]
    dx1, dq_x, dkv, dg2, *ffn_theirs = _xattn_bwd(dx2, x1, qx, g2, w_xq_f, w_xo_f, kv,
                                                  plans=[_sibling_plan(ffn_mine)])
    ffn_chip = pair_sums(ffn_names, ffn_mine, ffn_theirs)
    d_xkv, dg_mem = _mem_kv_bwd(dkv, mem, g_mem, mn, w_xkv_t)
    d_a, d_b, dy_att, dy_sg, dgates = _merge_bwd(dx1, a_att, b_sg, gates, wba_t, wbs_t, w_out_f)
    mid_names = ["w_xo", "w_xq", "w_xkv", "w_out", "w_branch_att", "w_branch_sg"]
    mid_mine = [_blocks(g) for g in (
        _dw(o_x, dx2, 1024, 1024, DW_TOKENS, "dw_xo"), _dw(hx, dq_x, 1024, 1024, DW_TOKENS, "dw_xq"), d_xkv,
        _dw(merged, dx1, 1024, 1024, DW_TOKENS, "dw_out"), _dw(d_a, y_att, 1024, 512, DW_TOKENS, "dw_branch_att"),
        _dw(d_b, y_sg, 1024, 512, DW_TOKENS, "dw_branch_sg"))]
    duv, d_sgw, d_bx, d_lng, d_lnb, *got = _sgu_bwd(uv, dy_sg, lng, lnb, small["sg_w"], b_exp,
                                                    plans=[_chips_plan(ffn_chip), _sibling_plan(mid_mine)])
    ffn_all, mid_theirs = got[:2], got[2:]
    mid_chip = pair_sums(mid_names, mid_mine, mid_theirs)
    dq, dk, dv, ds_sum, *mid_all = _attn_bwd(qkv, dy_att, y_att, lse, bias, plans=[_chips_plan(mid_chip)])
    d_rel = _bias_grad(ds_sum)
    grad_x, dz, dg1 = _in_bwd(dq, dk, dv, duv, dgates, x, dx1, g1, w_in_t)
    in_mine = [_blocks(_dw(dz, h, 1152, 1024, DW_TOKENS, "dw_in"))]

    gs = {"norm_mix_g": dg1, "rel_bias": d_rel[:, :NREL], "sg_ln_g": d_lng.reshape(8, 64),
          "sg_ln_b": d_lnb.reshape(8, 64), "sg_w": d_sgw, "sg_b": d_bx.reshape(128, 8, 64)[:, :, 0].T,
          "norm_xattn_g": dg2, "norm_mem_g": dg_mem, "norm_ffn_g": dg3, "norm_final_g": dg4}
    whole = [gs[n] for n in small_names] + [loss_part]
    in_theirs, *sib_whole = _run_plan(_sibling_plan(in_mine, whole), "rs_sibling")
    in_chip = pair_sums(["w_in"], in_mine, [in_theirs])
    chip_whole = _pair_sum_whole(whole, sib_whole, "rs_pair_small")
    in_all, *all_whole = _run_plan(_chips_plan(in_chip, chip_whole), "rs_chips")
    all_parts = dict(zip(ffn_names + mid_names + ["w_in"], list(ffn_all) + list(mid_all) + [in_all]))

    res = {}
    for n, axis in BIG:
        parts = all_parts[n]
        wmv = [args[p + n][0] for p in ("", "m_", "v_")]
        tr = ADAM_ROWS[n] if axis == 1 else wmv[0].shape[0]
        res[n] = [t[None] for t in _adam(parts, *wmv, tr, "adam_" + n, transposed=(axis == 1))]
    small_res = _small_update(
        all_whole[:-1], [small[n] for n in small_names], [args["m_" + n].reshape(s) for n, s in SMALL],
        [args["v_" + n].reshape(s) for n, s in SMALL], all_whole[-1], "adam_small")
    for i, n in enumerate(small_names):
        res[n] = [small_res[k][i].reshape(args[n].shape) for k in range(4)]
    loss = small_res[4][0, 0]

    order = ["norm_mix_g", "w_in", "rel_bias", "sg_ln_g", "sg_ln_b", "sg_w", "sg_b", "w_branch_att", "w_branch_sg",
             "w_out", "norm_xattn_g", "norm_mem_g", "w_xq", "w_xkv", "w_xo", "norm_ffn_g", "w_ffn_in", "w_ffn_out",
             "norm_final_g"]
    outs = [loss, grad_x.reshape(1, S, D)]
    for k in range(4):
        outs += [res[n][k] for n in order]
    return tuple(outs)
```

```python
import math

import jax
import jax.numpy as jnp
from jax import lax
from jax.experimental import pallas as pl
from jax.experimental.pallas import tpu as pltpu

F32 = jnp.float32
BF16 = jnp.bfloat16

D = 1024
ATT_W = 512
SG_W = 512
IN_COLS = 4608
DFF = 2816
MEM = 256
XH = 4
CHUNK = 64
BAND_KEYS = 640
ATT_R = 512
ATT_SUBS = ATT_R // 128
DW_TOKENS = 1024
REL_CLIP = 128
NREL = 2 * REL_CLIP + 1
EPS = 1e-6
NEG = -1e30
N_DEV = 8

ADAM_LR = 0.001
ADAM_B1 = 0.9
ADAM_B2 = 0.999
ADAM_EPS = 1e-08
ADAM_WD = 0.01
ADAM_STEP = 10

LANES = 128
VMEM_LIMIT = 56 * 1024 * 1024
MESH = pl.DeviceIdType.MESH


def _cparams(n_axes):
    return pltpu.CompilerParams(dimension_semantics=("arbitrary",) * n_axes, vmem_limit_bytes=VMEM_LIMIT)


def _resident(shape):
    zeros = (0,) * len(shape)
    return pl.BlockSpec(shape, lambda *_: zeros, pipeline_mode=pl.Buffered(1))


def _rows(tm, cols, col_block=0):
    return pl.BlockSpec((tm, cols), lambda i: (i, col_block))


def _sigmoid(x):
    return 1.0 / (1.0 + jnp.exp(-x))


_GELU_C = math.sqrt(2.0 / math.pi)


def _gelu(x):
    t = jnp.tanh(_GELU_C * (x + 0.044715 * (x * x * x)))
    return x * (0.5 * (1.0 + t))


def _gelu_and_grad(x):
    x2 = x * x
    t = jnp.tanh(_GELU_C * (x + 0.044715 * (x2 * x)))
    cdf = 0.5 * (1.0 + t)
    dcdf = 0.5 * (1.0 - t * t) * (_GELU_C * (1.0 + 3.0 * 0.044715 * x2))
    return x * cdf, cdf + x * dcdf


def _rstd(x):
    return lax.rsqrt(jnp.mean(x * x, axis=-1, keepdims=True) + EPS)


def _rms_bwd(dh, x, r, g):
    xh = x * r
    dxh = dh * g
    dx = r * (dxh - xh * jnp.mean(dxh * xh, axis=-1, keepdims=True))
    dg = jnp.sum(dh * xh, axis=0, keepdims=True)
    return dx, dg


def _group_sum64(x):
    r = lax.broadcasted_iota(jnp.int32, (LANES, LANES), 0) // 64
    c = lax.broadcasted_iota(jnp.int32, (LANES, LANES), 1) // 64
    same_group = (r == c).astype(BF16)

    def one(v):
        hi = v.astype(BF16)
        rest = v - hi.astype(F32)
        mid = rest.astype(BF16)
        lo = (rest - mid.astype(F32)).astype(BF16)
        return _dot(hi, same_group) + _dot(mid, same_group) + _dot(lo, same_group)

    pieces = [one(x[:, LANES * j:LANES * (j + 1)]) for j in range(x.shape[1] // LANES)]
    return pieces[0] if len(pieces) == 1 else jnp.concatenate(pieces, axis=1)


def _dot(a, b):
    return jnp.dot(a, b, preferred_element_type=F32)


def _dot_nt(a, b):
    return lax.dot_general(a, b, (((1,), (1,)), ((), ())), preferred_element_type=F32)


def _dot_tn(a, b):
    return lax.dot_general(a, b, (((0,), (0,)), ((), ())), preferred_element_type=F32)


DIAGS = 768


def _diag_onehot():
    r_idx = lax.broadcasted_iota(jnp.int32, (384, DIAGS), 0)
    t_idx = lax.broadcasted_iota(jnp.int32, (384, DIAGS), 1)
    dist = (8 * CHUNK + 127) - t_idx
    return (jnp.clip(dist, -REL_CLIP, REL_CLIP) + REL_CLIP == r_idx).astype(F32)


def _shift_rows(x, reverse):
    row = lax.broadcasted_iota(jnp.int32, x.shape, 0)
    for k in range(7):
        amt = (DIAGS - (1 << k)) if reverse else (1 << k)
        x = jnp.where(((row >> k) & 1) == 1, pltpu.roll(x, amt, 1), x)
    return x


N_TABLES = 1 + ATT_SUBS


def _bias_table(rel_bias_pad):
    def body(rb_ref, out_ref):
        per_diag = jnp.dot(rb_ref[...], _diag_onehot(), preferred_element_type=F32,
                           precision=lax.Precision.HIGHEST)
        a = lax.broadcasted_iota(jnp.int32, (128, BAND_KEYS), 0)
        b = lax.broadcasted_iota(jnp.int32, (128, BAND_KEYS), 1)
        band = (b // CHUNK >= a // CHUNK) & (b // CHUNK <= a // CHUNK + 8)
        for h in range(8):
            rows = jnp.broadcast_to(per_diag[h:h + 1, :], (128, DIAGS))
            table = _shift_rows(pltpu.roll(rows, DIAGS - 127, 1), reverse=False)[:, :BAND_KEYS]
            out_ref[0, h] = jnp.where(band, table, NEG)
            for s in range(ATT_SUBS):
                out_ref[1 + s, h] = jnp.where(band & (b >= 8 * CHUNK - 128 * s), table, NEG)

    return pl.pallas_call(
        body, name="bias_table",
        out_shape=jax.ShapeDtypeStruct((N_TABLES, 8, 128, BAND_KEYS), F32),
        in_specs=[pl.BlockSpec(memory_space=pltpu.VMEM)],
        out_specs=pl.BlockSpec(memory_space=pltpu.VMEM),
        compiler_params=pltpu.CompilerParams(vmem_limit_bytes=VMEM_LIMIT),
    )(rel_bias_pad)


def _bias_grad(ds_sum):
    def body(ds_ref, out_ref):
        sums = []
        for h in range(8):
            padded = jnp.concatenate([ds_ref[h], jnp.zeros((128, DIAGS - BAND_KEYS), F32)], axis=1)
            skewed = pltpu.roll(_shift_rows(padded, reverse=True), 127, 1)
            sums.append(jnp.sum(skewed, axis=0, keepdims=True))
        per_diag = jnp.concatenate(sums, axis=0)
        out_ref[...] = lax.dot_general(per_diag, _diag_onehot(), (((1,), (1,)), ((), ())),
                                       preferred_element_type=F32, precision=lax.Precision.HIGHEST)

    return pl.pallas_call(
        body, name="bias_grad",
        out_shape=jax.ShapeDtypeStruct((8, 384), F32),
        in_specs=[pl.BlockSpec(memory_space=pltpu.VMEM)],
        out_specs=pl.BlockSpec(memory_space=pltpu.VMEM),
    )(ds_sum)


def _in_proj(x, g1, w_in_t, plans=(), tm=512):
    S = x.shape[0]

    def body(x_ref, g_ref, w_ref, qkv_ref, uv_ref, gate_ref, h_ref):
        xv = x_ref[...]
        h = (xv * _rstd(xv) * g_ref[...]).astype(BF16)
        h_ref[...] = h
        for c in range(IN_COLS // 512):
            zc = _dot_nt(h, w_ref[512 * c:512 * (c + 1), :])
            if c == 0:
                qkv_ref[:, 0:512] = (zc * 0.125).astype(BF16)
            elif c < 3:
                qkv_ref[:, 512 * c:512 * (c + 1)] = zc.astype(BF16)
            elif c < 5:
                uv_ref[:, 512 * (c - 3):512 * (c - 2)] = zc.astype(BF16)
            else:
                gate_ref[:, 512 * (c - 5):512 * (c - 4)] = zc.astype(BF16)

    return _call(
        body, name="in_proj", grid=(S // tm,),
        out_shape=(jax.ShapeDtypeStruct((S, 3 * ATT_W), BF16), jax.ShapeDtypeStruct((S, 2 * SG_W), BF16),
                   jax.ShapeDtypeStruct((S, 2 * D), BF16), jax.ShapeDtypeStruct((S, D), BF16)),
        in_specs=[_rows(tm, D), _resident((1, D)), _resident((IN_COLS, D))],
        out_specs=(_rows(tm, 3 * ATT_W), _rows(tm, 2 * SG_W), _rows(tm, 2 * D), _rows(tm, D)),
        operands=(x, g1, w_in_t), plans=plans)


def _two_heads(a, lo):
    return jnp.concatenate([jnp.where(lo, a, 0), jnp.where(lo, 0, a)], axis=0)


def _att_specs():
    R = ATT_R
    q = pl.BlockSpec((R, LANES), lambda j, i: (i, j))
    kp = pl.BlockSpec((R, LANES), lambda j, i: (jnp.maximum(i - 1, 0), 4 + j))
    kc = pl.BlockSpec((R, LANES), lambda j, i: (i, 4 + j))
    vp = pl.BlockSpec((R, LANES), lambda j, i: (jnp.maximum(i - 1, 0), 8 + j))
    vc = pl.BlockSpec((R, LANES), lambda j, i: (i, 8 + j))
    bias = pl.BlockSpec((N_TABLES, 2, 128, BAND_KEYS), lambda j, i: (0, j, 0, 0))
    return [q, kp, kc, vp, vc, bias]


def _attn_fwd(qkv, bias, plans=()):
    S = qkv.shape[0]
    R = ATT_R

    def body(q_ref, kp_ref, kc_ref, vp_ref, vc_ref, b_ref, o_ref, lse_ref):
        i = pl.program_id(1)
        lo = lax.broadcasted_iota(jnp.int32, (1, LANES), 1) < 64
        kwin = jnp.concatenate([kp_ref[...], kc_ref[...]], axis=0)
        vwin = jnp.concatenate([vp_ref[...], vc_ref[...]], axis=0)
        for sub in range(ATT_SUBS):
            q2 = q_ref[128 * sub:128 * (sub + 1), :]
            kw = kwin[128 * sub:128 * sub + BAND_KEYS]
            vw = vwin[128 * sub:128 * sub + BAND_KEYS]
            table = jnp.where(i == 0, 1 + sub, 0)
            s = _dot_nt(_two_heads(q2, lo), kw) + b_ref[table].reshape(256, BAND_KEYS)
            top = jnp.max(s, axis=-1, keepdims=True)
            p = jnp.exp(s - top)
            total = jnp.sum(p, axis=-1, keepdims=True)
            p = (p / total).astype(BF16)
            o = _dot(jnp.concatenate([p[:128], p[128:]], axis=1), _two_heads(vw, lo))
            lse = top + jnp.log(total)
            o_ref[128 * sub:128 * (sub + 1), :] = o.astype(BF16)
            lse_ref[128 * sub:128 * (sub + 1), :] = jnp.where(lo, lse[:128], lse[128:])

    blk = pl.BlockSpec((R, LANES), lambda j, i: (i, j))
    return _call(
        body, name="attn_fwd", grid=(4, S // R),
        out_shape=(jax.ShapeDtypeStruct((S, ATT_W), BF16), jax.ShapeDtypeStruct((S, ATT_W), F32)),
        in_specs=_att_specs(), out_specs=(blk, blk),
        operands=(qkv, qkv, qkv, qkv, qkv, bias), plans=plans)


def _sg_mask():
    t = lax.broadcasted_iota(jnp.int32, (128, 128), 0)
    s = lax.broadcasted_iota(jnp.int32, (128, 128), 1)
    return (s // CHUNK) <= (t // CHUNK)


def _sg_layernorm(gv, lng, lnb):
    mu = _group_sum64(gv) * (1.0 / 64)
    xc = gv - mu
    var = _group_sum64(xc * xc) * (1.0 / 64)
    rstd = lax.rsqrt(var + EPS)
    vhat = xc * rstd
    return vhat * lng + lnb, vhat, rstd


def _sgu_fwd(uv, lng, lnb, sg_w, b_exp, tm=512):
    S = uv.shape[0]

    def body(uv_ref, lng_ref, lnb_ref, w_ref, b_ref, y_ref):
        lane = lax.broadcasted_iota(jnp.int32, (1, LANES), 1)
        lo = lane < 64
        mask = _sg_mask()
        gu = _gelu(uv_ref[:, 0:SG_W].astype(F32))
        vln, _, _ = _sg_layernorm(_gelu(uv_ref[:, SG_W:2 * SG_W].astype(F32)), lng_ref[...], lnb_ref[...])
        for gp in range(4):
            w0 = jnp.where(mask, w_ref[2 * gp], 0).astype(BF16)
            w1 = jnp.where(mask, w_ref[2 * gp + 1], 0).astype(BF16)
            cols = slice(128 * gp, 128 * (gp + 1))
            for n in range(tm // 128):
                rows = slice(128 * n, 128 * (n + 1))
                vl = vln[rows, cols]
                sv = (_dot(w0, jnp.where(lo, vl, 0).astype(BF16)) + _dot(w1, jnp.where(lo, 0, vl).astype(BF16))
                      + b_ref[:, cols])
                y_ref[rows, cols] = (gu[rows, cols] * sv).astype(BF16)

    return pl.pallas_call(
        body, name="sgu_fwd", grid=(S // tm,),
        out_shape=jax.ShapeDtypeStruct((S, SG_W), BF16),
        in_specs=[_rows(tm, 2 * SG_W), _resident((1, SG_W)), _resident((1, SG_W)),
                  _resident((8, 128, 128)), _resident((128, SG_W))],
        out_specs=_rows(tm, SG_W),
        compiler_params=_cparams(1),
    )(uv, lng, lnb, sg_w, b_exp)


def _merge_fwd(x, y_att, y_sg, gates, wba_t, wbs_t, w_out, tm=512):
    S = x.shape[0]

    def body(x_ref, ya_ref, ys_ref, g_ref, wba_ref, wbs_ref, wo_ref, x1_ref):
        a = _dot_nt(ya_ref[...], wba_ref[...])
        b = _dot_nt(ys_ref[...], wbs_ref[...])
        merged = _sigmoid(g_ref[:, 0:D].astype(F32)) * a + _sigmoid(g_ref[:, D:2 * D].astype(F32)) * b
        x1_ref[...] = x_ref[...] + _dot(merged.astype(BF16), wo_ref[...])

    return pl.pallas_call(
        body, name="merge_fwd", grid=(S // tm,),
        out_shape=jax.ShapeDtypeStruct((S, D), F32),
        in_specs=[_rows(tm, D), _rows(tm, ATT_W), _rows(tm, SG_W), _rows(tm, 2 * D),
                  _resident((D, ATT_W)), _resident((D, SG_W)), _resident((D, D))],
        out_specs=_rows(tm, D),
        compiler_params=_cparams(1),
    )(x, y_att, y_sg, gates, wba_t, wbs_t, w_out)


def _mem_kv(mem, g_mem, w_xkv_t):
    def body(m_ref, g_ref, w_ref, kv_ref, mn_ref):
        mv = m_ref[...]
        mn = (mv * _rstd(mv) * g_ref[...]).astype(BF16)
        mn_ref[...] = mn
        kv_ref[...] = _dot_nt(mn, w_ref[...]).astype(BF16)

    vm = pl.BlockSpec(memory_space=pltpu.VMEM)
    return pl.pallas_call(
        body, name="mem_kv",
        out_shape=(jax.ShapeDtypeStruct((MEM, 2 * D), BF16), jax.ShapeDtypeStruct((MEM, D), BF16)),
        in_specs=[vm, vm, vm], out_specs=(vm, vm),
        compiler_params=pltpu.CompilerParams(vmem_limit_bytes=VMEM_LIMIT),
    )(mem, g_mem, w_xkv_t)


def _xatt_head(qx, kv_ref, h):
    hs = slice(256 * h, 256 * (h + 1))
    s = _dot_nt(qx[:, hs], kv_ref[:, hs])
    p = jnp.exp(s - jnp.max(s, axis=-1, keepdims=True))
    return p / jnp.sum(p, axis=-1, keepdims=True)


def _xattn_fwd(x1, g2, w_xq, kv, w_xo, tm=512):
    S = x1.shape[0]

    def body(x_ref, g_ref, wq_ref, kv_ref, wo_ref, x2_ref, hx_ref, qx_ref, o_ref):
        xv = x_ref[...]
        hx = (xv * _rstd(xv) * g_ref[...]).astype(BF16)
        hx_ref[...] = hx
        qx = (_dot(hx, wq_ref[...]) * (1.0 / 16)).astype(BF16)
        qx_ref[...] = qx
        for h in range(XH):
            p = _xatt_head(qx, kv_ref, h)
            o_ref[:, 256 * h:256 * (h + 1)] = _dot(p.astype(BF16), kv_ref[:, D + 256 * h:D + 256 * (h + 1)]).astype(BF16)
        x2_ref[...] = xv + _dot(o_ref[...], wo_ref[...])

    return pl.pallas_call(
        body, name="xattn_fwd", grid=(S // tm,),
        out_shape=(jax.ShapeDtypeStruct((S, D), F32),) + (jax.ShapeDtypeStruct((S, D), BF16),) * 3,
        in_specs=[_rows(tm, D), _resident((1, D)), _resident((D, D)), _resident((MEM, 2 * D)), _resident((D, D))],
        out_specs=(_rows(tm, D),) * 4,
        compiler_params=_cparams(1),
    )(x1, g2, w_xq, kv, w_xo)


FF_CHUNK = 1408


def _ffn_fwd(x2, tgt, g3, w_ffn_in_t, w_ffn_out, g4, tm=256):
    S = x2.shape[0]

    def body(x_ref, t_ref, g3_ref, wi_ref, wo_ref, g4_ref, dx3_ref, gu_ref, hf_ref, act_ref, loss_ref, dg4_ref):
        i = pl.program_id(0)
        xv = x_ref[...]
        hf = (xv * _rstd(xv) * g3_ref[...]).astype(BF16)
        hf_ref[...] = hf
        acc = xv
        for c in range(DFF // FF_CHUNK):
            cs = slice(FF_CHUNK * c, FF_CHUNK * (c + 1))
            us = slice(DFF + FF_CHUNK * c, DFF + FF_CHUNK * (c + 1))
            gate = _dot_nt(hf, wi_ref[cs, :])
            up = _dot_nt(hf, wi_ref[us, :])
            gu_ref[:, cs] = gate.astype(BF16)
            gu_ref[:, us] = up.astype(BF16)
            act = ((gate * _sigmoid(gate)) * up).astype(BF16)
            act_ref[:, cs] = act
            acc = acc + _dot(act, wo_ref[cs, :])
        r4 = _rstd(acc)
        g4 = g4_ref[...]
        diff = acc * r4 * g4 - t_ref[...]
        dy = diff * (1.0 / D)
        dx3, dg4 = _rms_bwd(dy, acc, r4, g4)
        dx3_ref[...] = dx3
        part = 0.5 * jnp.sum(jnp.mean(diff * diff, axis=-1, keepdims=True))

        @pl.when(i == 0)
        def _():
            loss_ref[...] = jnp.zeros_like(loss_ref)
            dg4_ref[...] = jnp.zeros_like(dg4_ref)

        loss_ref[...] += jnp.full(loss_ref.shape, part, F32)
        dg4_ref[...] += dg4

    return pl.pallas_call(
        body, name="ffn_fwd", grid=(S // tm,),
        out_shape=(jax.ShapeDtypeStruct((S, D), F32), jax.ShapeDtypeStruct((S, 2 * DFF), BF16),
                   jax.ShapeDtypeStruct((S, D), BF16), jax.ShapeDtypeStruct((S, DFF), BF16),
                   jax.ShapeDtypeStruct((8, LANES), F32), jax.ShapeDtypeStruct((1, D), F32)),
        in_specs=[_rows(tm, D), _rows(tm, D), _resident((1, D)), _resident((2 * DFF, D)), _resident((DFF, D)),
                  _resident((1, D))],
        out_specs=(_rows(tm, D), _rows(tm, 2 * DFF), _rows(tm, D), _rows(tm, DFF),
                   pl.BlockSpec((8, LANES), lambda i: (0, 0)), pl.BlockSpec((1, D), lambda i: (0, 0))),
        compiler_params=_cparams(1),
    )(x2, tgt, g3, w_ffn_in_t, w_ffn_out, g4)


def _ffn_bwd(dx3, gu, x2, g3, w_ffn_out, w_ffn_in_t, tm=256):
    S = x2.shape[0]

    def body(d_ref, gu_ref, x_ref, g3_ref, wo_ref, wit_ref, dx2_ref, dgu_ref, dg3_ref):
        i = pl.program_id(0)
        d3 = d_ref[...]
        d3b = d3.astype(BF16)
        for c in range(DFF // FF_CHUNK):
            cs = slice(FF_CHUNK * c, FF_CHUNK * (c + 1))
            us = slice(DFF + FF_CHUNK * c, DFF + FF_CHUNK * (c + 1))
            da = _dot_nt(d3b, wo_ref[cs, :])
            gate = gu_ref[:, cs].astype(F32)
            up = gu_ref[:, us].astype(F32)
            sg = _sigmoid(gate)
            dgate = (da * up * (sg * (1.0 + gate * (1.0 - sg)))).astype(BF16)
            dup = (da * (gate * sg)).astype(BF16)
            dgu_ref[:, cs] = dgate
            dgu_ref[:, us] = dup
        dhf = _dot(dgu_ref[...], wit_ref[...])
        xv = x_ref[...]
        dx, dg3 = _rms_bwd(dhf, xv, _rstd(xv), g3_ref[...])
        dx2_ref[...] = d3 + dx

        @pl.when(i == 0)
        def _():
            dg3_ref[...] = jnp.zeros_like(dg3_ref)

        dg3_ref[...] += dg3

    return pl.pallas_call(
        body, name="ffn_bwd", grid=(S // tm,),
        out_shape=(jax.ShapeDtypeStruct((S, D), F32), jax.ShapeDtypeStruct((S, 2 * DFF), BF16),
                   jax.ShapeDtypeStruct((1, D), F32)),
        in_specs=[_rows(tm, D), _rows(tm, 2 * DFF), _rows(tm, D), _resident((1, D)),
                  _resident((DFF, D)), _resident((2 * DFF, D))],
        out_specs=(_rows(tm, D), _rows(tm, 2 * DFF), pl.BlockSpec((1, D), lambda i: (0, 0))),
        compiler_params=_cparams(1),
    )(dx3, gu, x2, g3, w_ffn_out, w_ffn_in_t)


def _dw(a, b, tmm, tn, ts, name, out_dtype=BF16):
    S, M = a.shape
    N = b.shape[1]
    nk = S // ts

    def body(a_ref, b_ref, o_ref, acc_ref):
        k = pl.program_id(2)

        @pl.when(k == 0)
        def _():
            acc_ref[...] = jnp.zeros_like(acc_ref)

        acc_ref[...] += _dot_tn(a_ref[...].astype(BF16), b_ref[...].astype(BF16))

        @pl.when(k == nk - 1)
        def _():
            o_ref[...] = acc_ref[...].astype(out_dtype)

    return pl.pallas_call(
        body, name=name, grid=(M // tmm, N // tn, nk),
        out_shape=jax.ShapeDtypeStruct((M, N), out_dtype),
        in_specs=[pl.BlockSpec((ts, tmm), lambda m, n, k: (k, m)), pl.BlockSpec((ts, tn), lambda m, n, k: (k, n))],
        out_specs=pl.BlockSpec((tmm, tn), lambda m, n, k: (m, n)),
        scratch_shapes=[pltpu.VMEM((tmm, tn), F32)],
        compiler_params=_cparams(3),
    )(a, b)


def _xattn_bwd(dx2, x1, qx, g2, w_xq, w_xo, kv, plans=(), tm=512):
    S = x1.shape[0]

    def body(d_ref, x_ref, qx_ref, g_ref, wq_ref, wo_ref, kv_ref, dx1_ref, dq_ref, dkv_ref, dg2_ref):
        i = pl.program_id(0)

        @pl.when(i == 0)
        def _():
            dkv_ref[...] = jnp.zeros_like(dkv_ref)
            dg2_ref[...] = jnp.zeros_like(dg2_ref)

        d2 = d_ref[...]
        qx = qx_ref[...]
        do = _dot_nt(d2.astype(BF16), wo_ref[...]).astype(BF16)
        for h in range(XH):
            hs = slice(256 * h, 256 * (h + 1))
            vs = slice(D + 256 * h, D + 256 * (h + 1))
            p = _xatt_head(qx, kv_ref, h)
            dp = _dot_nt(do[:, hs], kv_ref[:, vs])
            ds = (p * (dp - jnp.sum(dp * p, axis=-1, keepdims=True))).astype(BF16)
            dq_ref[:, hs] = (_dot(ds, kv_ref[:, hs]) * (1.0 / 16)).astype(BF16)
            dkv_ref[:, hs] += _dot_tn(ds, qx[:, hs])
            dkv_ref[:, vs] += _dot_tn(p.astype(BF16), do[:, hs])
        dhx = _dot_nt(dq_ref[...], wq_ref[...])
        xv = x_ref[...]
        dx, dg2 = _rms_bwd(dhx, xv, _rstd(xv), g_ref[...])
        dx1_ref[...] = d2 + dx
        dg2_ref[...] += dg2

    return _call(
        body, name="xattn_bwd", grid=(S // tm,),
        out_shape=(jax.ShapeDtypeStruct((S, D), F32), jax.ShapeDtypeStruct((S, D), BF16),
                   jax.ShapeDtypeStruct((MEM, 2 * D), F32), jax.ShapeDtypeStruct((1, D), F32)),
        in_specs=[_rows(tm, D), _rows(tm, D), _rows(tm, D), _resident((1, D)), _resident((D, D)), _resident((D, D)),
                  _resident((MEM, 2 * D))],
        out_specs=(_rows(tm, D), _rows(tm, D),
                   pl.BlockSpec((MEM, 2 * D), lambda i: (0, 0)), pl.BlockSpec((1, D), lambda i: (0, 0))),
        operands=(dx2, x1, qx, g2, w_xq, w_xo, kv), plans=plans)


def _mem_kv_bwd(dkv, mem, g_mem, mn, w_xkv_t):
    def body(dkv_ref, m_ref, g_ref, mn_ref, wt_ref, dw_ref, dg_ref):
        dkvb = dkv_ref[...].astype(BF16)
        dw_ref[...] = _dot_tn(dkvb, mn_ref[...]).astype(BF16)
        dmn = _dot(dkvb, wt_ref[...])
        mv = m_ref[...]
        dg_ref[...] = jnp.sum(dmn * (mv * _rstd(mv)), axis=0, keepdims=True)

    vm = pl.BlockSpec(memory_space=pltpu.VMEM)
    return pl.pallas_call(
        body, name="mem_kv_bwd",
        out_shape=(jax.ShapeDtypeStruct((2 * D, D), BF16), jax.ShapeDtypeStruct((1, D), F32)),
        in_specs=[vm] * 5, out_specs=(vm, vm),
        compiler_params=pltpu.CompilerParams(vmem_limit_bytes=VMEM_LIMIT),
    )(dkv, mem, g_mem, mn, w_xkv_t)


def _merge_bwd(dx1, y_att, y_sg, gates, wba_t, wbs_t, w_out, tm=512):
    S = dx1.shape[0]

    def body(d_ref, ya_ref, ys_ref, g_ref, wbat_ref, wbst_ref, wo_ref,
             mg_ref, da_ref, db_ref, dya_ref, dys_ref, dg_ref):
        dm = _dot_nt(d_ref[...].astype(BF16), wo_ref[...])
        a = _dot_nt(ya_ref[...], wbat_ref[...])
        b = _dot_nt(ys_ref[...], wbst_ref[...])
        sa = _sigmoid(g_ref[:, 0:D].astype(F32))
        sb = _sigmoid(g_ref[:, D:2 * D].astype(F32))
        mg_ref[...] = (sa * a + sb * b).astype(BF16)
        da = (dm * sa).astype(BF16)
        db = (dm * sb).astype(BF16)
        da_ref[...] = da
        db_ref[...] = db
        dg_ref[:, 0:D] = (dm * a * sa * (1.0 - sa)).astype(BF16)
        dg_ref[:, D:2 * D] = (dm * b * sb * (1.0 - sb)).astype(BF16)
        dya_ref[...] = _dot(da, wbat_ref[...]).astype(BF16)
        dys_ref[...] = _dot(db, wbst_ref[...]).astype(BF16)

    return pl.pallas_call(
        body, name="merge_bwd", grid=(S // tm,),
        out_shape=(jax.ShapeDtypeStruct((S, D), BF16), jax.ShapeDtypeStruct((S, D), BF16),
                   jax.ShapeDtypeStruct((S, D), BF16), jax.ShapeDtypeStruct((S, ATT_W), BF16),
                   jax.ShapeDtypeStruct((S, SG_W), BF16), jax.ShapeDtypeStruct((S, 2 * D), BF16)),
        in_specs=[_rows(tm, D), _rows(tm, ATT_W), _rows(tm, SG_W), _rows(tm, 2 * D),
                  _resident((D, ATT_W)), _resident((D, SG_W)), _resident((D, D))],
        out_specs=(_rows(tm, D), _rows(tm, D), _rows(tm, D), _rows(tm, ATT_W), _rows(tm, SG_W), _rows(tm, 2 * D)),
        compiler_params=_cparams(1),
    )(dx1, y_att, y_sg, gates, wba_t, wbs_t, w_out)


def _sgu_bwd(uv, dy_sg, lng, lnb, sg_w, b_exp, plans=(), tm=512):
    S = uv.shape[0]
    n_steps = S // tm

    def body(uv_ref, dy_ref, lng_ref, lnb_ref, w_ref, b_ref, duv_ref, dw_ref, dbx_ref, dlng_ref, dlnb_ref, dvln_ref):
        i = pl.program_id(0)

        @pl.when(i == 0)
        def _():
            dw_ref[...] = jnp.zeros_like(dw_ref)
            dbx_ref[...] = jnp.zeros_like(dbx_ref)
            dlng_ref[...] = jnp.zeros_like(dlng_ref)
            dlnb_ref[...] = jnp.zeros_like(dlnb_ref)

        lane = lax.broadcasted_iota(jnp.int32, (1, LANES), 1)
        lo = lane < 64
        mask = _sg_mask()
        lng = lng_ref[...]
        gu, dgelu_u = _gelu_and_grad(uv_ref[:, 0:SG_W].astype(F32))
        gv, dgelu_v = _gelu_and_grad(uv_ref[:, SG_W:2 * SG_W].astype(F32))
        vln, vhat, rstd = _sg_layernorm(gv, lng, lnb_ref[...])
        dy = dy_ref[...].astype(F32)
        dsv_all = dy * gu
        for gp in range(4):
            wf0 = jnp.where(mask, w_ref[2 * gp], 0)
            wf1 = jnp.where(mask, w_ref[2 * gp + 1], 0)
            w0 = wf0.astype(BF16)
            w1 = wf1.astype(BF16)
            cols = slice(128 * gp, 128 * (gp + 1))
            dw0 = jnp.zeros((128, 128), F32)
            dw1 = jnp.zeros((128, 128), F32)
            dbx = jnp.zeros((128, LANES), F32)
            for n in range(tm // 128):
                rows = slice(128 * n, 128 * (n + 1))
                vl = vln[rows, cols]
                vl0 = jnp.where(lo, vl, 0).astype(BF16)
                vl1 = jnp.where(lo, 0, vl).astype(BF16)
                sv = _dot(w0, vl0) + _dot(w1, vl1) + b_ref[:, cols]
                duv_ref[rows, cols] = (dy[rows, cols] * sv * dgelu_u[rows, cols]).astype(BF16)
                dsv = dsv_all[rows, cols]
                dbx = dbx + dsv
                ds0 = jnp.where(lo, dsv, 0).astype(BF16)
                ds1 = jnp.where(lo, 0, dsv).astype(BF16)
                dw0 = dw0 + _dot_nt(ds0, vl0)
                dw1 = dw1 + _dot_nt(ds1, vl1)
                dvln_ref[rows, cols] = _dot_tn(w0, ds0) + _dot_tn(w1, ds1)
            dw_ref[2 * gp] += jnp.where(mask, dw0, 0)
            dw_ref[2 * gp + 1] += jnp.where(mask, dw1, 0)
            dbx_ref[:, cols] += dbx
        dvln = dvln_ref[...]
        dlng_ref[...] += jnp.sum(dvln * vhat, axis=0, keepdims=True)
        dlnb_ref[...] += jnp.sum(dvln, axis=0, keepdims=True)
        dvh = dvln * lng
        dgv = rstd * (dvh - _group_sum64(dvh) * (1.0 / 64) - vhat * (_group_sum64(dvh * vhat) * (1.0 / 64)))
        duv_ref[:, SG_W:2 * SG_W] = (dgv * dgelu_v).astype(BF16)

        @pl.when(i == n_steps - 1)
        def _():
            dbx_ref[...] = _group_sum64(dbx_ref[...])

    return _call(
        body, name="sgu_bwd", grid=(n_steps,),
        out_shape=(jax.ShapeDtypeStruct((S, 2 * SG_W), BF16), jax.ShapeDtypeStruct((8, 128, 128), F32),
                   jax.ShapeDtypeStruct((128, SG_W), F32), jax.ShapeDtypeStruct((1, SG_W), F32),
                   jax.ShapeDtypeStruct((1, SG_W), F32)),
        in_specs=[_rows(tm, 2 * SG_W), _rows(tm, SG_W), _resident((1, SG_W)), _resident((1, SG_W)),
                  _resident((8, 128, 128)), _resident((128, SG_W))],
        out_specs=(_rows(tm, 2 * SG_W), pl.BlockSpec((8, 128, 128), lambda i: (0, 0, 0)),
                   pl.BlockSpec((128, SG_W), lambda i: (0, 0)), pl.BlockSpec((1, SG_W), lambda i: (0, 0)),
                   pl.BlockSpec((1, SG_W), lambda i: (0, 0))),
        scratch_shapes=[pltpu.VMEM((tm, SG_W), F32)],
        operands=(uv, dy_sg, lng, lnb, sg_w, b_exp), plans=plans)


def _attn_bwd(qkv, dy_att, y_att, lse, bias, plans=()):
    S = qkv.shape[0]
    R = ATT_R

    def body(q_ref, kp_ref, kc_ref, vp_ref, vc_ref, b_ref, dy_ref, y_ref, lse_ref, dq_ref, dk_ref, dv_ref, dss_ref):
        i = pl.program_id(1)

        @pl.when(i == 0)
        def _():
            dk_ref[...] = jnp.zeros_like(dk_ref)
            dv_ref[...] = jnp.zeros_like(dv_ref)
            dss_ref[...] = jnp.zeros_like(dss_ref)

        lane = lax.broadcasted_iota(jnp.int32, (1, LANES), 1)
        kwin = jnp.concatenate([kp_ref[...], kc_ref[...]], axis=0)
        vwin = jnp.concatenate([vp_ref[...], vc_ref[...]], axis=0)
        for sub in range(ATT_SUBS):
            rows = slice(128 * sub, 128 * (sub + 1))
            q2 = q_ref[rows, :]
            do2 = dy_ref[rows, :]
            kw = kwin[128 * sub:128 * sub + BAND_KEYS]
            vw = vwin[128 * sub:128 * sub + BAND_KEYS]
            table = jnp.where(i == 0, 1 + sub, 0)
            dyy = do2.astype(F32) * y_ref[rows, :].astype(F32)
            lse2 = lse_ref[rows, :]
            dq = jnp.zeros((128, LANES), F32)
            dkw = jnp.zeros((BAND_KEYS, LANES), F32)
            dvw = jnp.zeros((BAND_KEYS, LANES), F32)
            for hh in range(2):
                mine = (lane >= 64 * hh) & (lane < 64 * (hh + 1))
                qm = jnp.where(mine, q2, 0)
                dom = jnp.where(mine, do2, 0)
                p = jnp.exp(_dot_nt(qm, kw) + b_ref[table, hh] - lse2[:, 64 * hh:64 * hh + 1])
                dp = _dot_nt(dom, vw)
                ds = p * (dp - jnp.sum(jnp.where(mine, dyy, 0.0), axis=-1, keepdims=True))
                dss_ref[hh] += ds
                dsb = ds.astype(BF16)
                dq = dq + _dot(dsb, jnp.where(mine, kw, 0))
                dkw = dkw + _dot_tn(dsb, qm)
                dvw = dvw + _dot_tn(p.astype(BF16), dom)
            dq_ref[rows, :] = dq.astype(BF16)
            start = pl.multiple_of(i * R + 128 * sub, 128)
            dk_ref[pl.ds(start, BAND_KEYS), :] += dkw
            dv_ref[pl.ds(start, BAND_KEYS), :] += dvw

    acc_spec = pl.BlockSpec((S + 8 * CHUNK, LANES), lambda j, i: (0, j))
    return _call(
        body, name="attn_bwd", grid=(4, S // R),
        out_shape=(jax.ShapeDtypeStruct((S, ATT_W), BF16), jax.ShapeDtypeStruct((S + 8 * CHUNK, ATT_W), F32),
                   jax.ShapeDtypeStruct((S + 8 * CHUNK, ATT_W), F32), jax.ShapeDtypeStruct((8, 128, BAND_KEYS), F32)),
        in_specs=_att_specs() + [pl.BlockSpec((R, LANES), lambda j, i: (i, j))] * 3,
        out_specs=(pl.BlockSpec((R, LANES), lambda j, i: (i, j)), acc_spec, acc_spec,
                   pl.BlockSpec((2, 128, BAND_KEYS), lambda j, i: (j, 0, 0))),
        operands=(qkv, qkv, qkv, qkv, qkv, bias, dy_att, y_att, lse), plans=plans)


def _in_bwd(dq, dk, dv, duv, dgates, x, dx1, g1, w_in_t, plans=(), tm=512):
    S = x.shape[0]
    pad_blocks = (8 * CHUNK) // tm

    def body(dq_ref, dk_ref, dv_ref, duv_ref, dg_ref, x_ref, d1_ref, g_ref, wt_ref, dx_ref, dz_ref, dg1_ref):
        i = pl.program_id(0)
        dz_ref[:, 0:ATT_W] = (dq_ref[...].astype(F32) * 0.125).astype(BF16)
        dz_ref[:, ATT_W:2 * ATT_W] = dk_ref[...].astype(BF16)
        dz_ref[:, 2 * ATT_W:3 * ATT_W] = dv_ref[...].astype(BF16)
        dz_ref[:, 3 * ATT_W:3 * ATT_W + 2 * SG_W] = duv_ref[...]
        dz_ref[:, 3 * ATT_W + 2 * SG_W:IN_COLS] = dg_ref[...]
        dh = _dot(dz_ref[...], wt_ref[...])
        xv = x_ref[...]
        dx, dg1 = _rms_bwd(dh, xv, _rstd(xv), g_ref[...])
        dx_ref[...] = d1_ref[...] + dx

        @pl.when(i == 0)
        def _():
            dg1_ref[...] = jnp.zeros_like(dg1_ref)

        dg1_ref[...] += dg1

    shifted = pl.BlockSpec((tm, ATT_W), lambda i: (i + pad_blocks, 0))
    return _call(
        body, name="in_bwd", grid=(S // tm,),
        out_shape=(jax.ShapeDtypeStruct((S, D), F32), jax.ShapeDtypeStruct((S, IN_COLS), BF16),
                   jax.ShapeDtypeStruct((1, D), F32)),
        in_specs=[_rows(tm, ATT_W), shifted, shifted, _rows(tm, 2 * SG_W), _rows(tm, 2 * D), _rows(tm, D),
                  _rows(tm, D), _resident((1, D)), _resident((IN_COLS, D))],
        out_specs=(_rows(tm, D), _rows(tm, IN_COLS), pl.BlockSpec((1, D), lambda i: (0, 0))),
        operands=(dq, dk, dv, duv, dgates, x, dx1, g1, w_in_t), plans=plans)


def _adam_math(w, g, m, v):
    m = ADAM_B1 * m + (1.0 - ADAM_B1) * g
    v = ADAM_B2 * v + (1.0 - ADAM_B2) * (g * g)
    m_hat = m / (1.0 - ADAM_B1 ** ADAM_STEP)
    v_hat = v / (1.0 - ADAM_B2 ** ADAM_STEP)
    delta = -ADAM_LR * (m_hat / (jnp.sqrt(v_hat) + ADAM_EPS) + ADAM_WD * w)
    return delta, m, v


def _adam(parts, w, m, v, tr, name, transposed):
    P = parts.shape[0]
    R, C = w.shape

    def body(p_ref, w_ref, m_ref, v_ref, g_ref, d_ref, mo_ref, vo_ref):
        if transposed:
            eye = (lax.broadcasted_iota(jnp.int32, (C, C), 0) == lax.broadcasted_iota(jnp.int32, (C, C), 1)).astype(BF16)
            part = lambda k: _dot_tn(p_ref[k], eye)
        else:
            part = lambda k: p_ref[k].astype(F32)
        g = part(0)
        for k in range(1, P):
            g = g + part(k)
        g_ref[...] = g
        d_ref[...], mo_ref[...], vo_ref[...] = _adam_math(w_ref[...], g, m_ref[...], v_ref[...])

    row = pl.BlockSpec((tr, C), lambda i: (i, 0))
    p_spec = pl.BlockSpec((P, C, tr), lambda i: (0, 0, i)) if transposed else pl.BlockSpec((P, tr, C), lambda i: (0, i, 0))
    return pl.pallas_call(
        body, name=name, grid=(R // tr,),
        out_shape=tuple(jax.ShapeDtypeStruct((R, C), F32) for _ in range(4)),
        in_specs=[p_spec, row, row, row],
        out_specs=(row, row, row, row),
        compiler_params=_cparams(1),
    )(parts, w, m, v)


def _my_place():
    return lax.axis_index("x"), lax.axis_index("y"), lax.axis_index("c")


def _other_chips(x, y):
    return [(1 - x, y), (x, 1 - y), (1 - x, 1 - y)]


class _Plan:
    def __init__(self, arrays, out_shapes, sems, start, finish, forward=None, forward_at=0.7):
        self.arrays, self.out_shapes, self.sems = list(arrays), list(out_shapes), list(sems)
        self.start, self.finish, self.forward, self.forward_at = start, finish, forward, forward_at


def _call(body, *, name, grid, in_specs, out_specs, out_shape, operands, scratch_shapes=(), plans=()):
    n_in, n_out, n_scr = len(operands), len(out_shape), len(scratch_shapes)
    p_in = [a for p in plans for a in p.arrays]
    p_out = [s for p in plans for s in p.out_shapes]
    p_sem = [s for p in plans for s in p.sems]
    steps = math.prod(grid)

    def wrapped(*refs):
        ins, refs = refs[:n_in], refs[n_in:]
        pins, refs = refs[:len(p_in)], refs[len(p_in):]
        outs, refs = refs[:n_out], refs[n_out:]
        pouts, refs = refs[:len(p_out)], refs[len(p_out):]
        scr, psems = refs[:n_scr], refs[n_scr:]
        step = 0
        for axis, size in enumerate(grid):
            step = step * size + pl.program_id(axis)
        bound = []
        for p in plans:
            bound.append((p, pins[:len(p.arrays)], pouts[:len(p.out_shapes)], psems[:len(p.sems)]))
            pins, pouts, psems = pins[len(p.arrays):], pouts[len(p.out_shapes):], psems[len(p.sems):]

        @pl.when(step == 0)
        def _():
            for p, a, b, s in bound:
                p.start(a, b, s)

        body(*ins, *outs, *scr)

        for p, a, b, s in bound:
            if p.forward is not None:
                @pl.when(step == min(int(p.forward_at * steps), steps - 1))
                def _(p=p, a=a, b=b, s=s):
                    p.forward(a, b, s)

        @pl.when(step == steps - 1)
        def _():
            for p, a, b, s in bound:
                p.finish(a, b, s)

    hbm = pl.BlockSpec(memory_space=pl.ANY)
    return pl.pallas_call(
        wrapped if plans else body, name=name, grid=grid,
        out_shape=tuple(out_shape) + tuple(p_out),
        in_specs=list(in_specs) + [hbm] * len(p_in),
        out_specs=tuple(out_specs) + tuple([hbm] * len(p_out)),
        scratch_shapes=list(scratch_shapes) + p_sem,
        compiler_params=_cparams(len(grid)),
    )(*operands, *p_in)


def _run_plan(plan, name):
    n_in, n_out = len(plan.arrays), len(plan.out_shapes)

    def body(*refs):
        a, b, s = refs[:n_in], refs[n_in:n_in + n_out], refs[n_in + n_out:]
        plan.start(a, b, s)
        if plan.forward is not None:
            plan.forward(a, b, s)
        plan.finish(a, b, s)

    hbm = pl.BlockSpec(memory_space=pl.ANY)
    return pl.pallas_call(
        body, name=name, out_shape=tuple(plan.out_shapes),
        in_specs=[hbm] * n_in, out_specs=tuple([hbm] * n_out), scratch_shapes=plan.sems,
    )(*plan.arrays)


def _gather_plan(shards, forward_at=0.7):
    n = len(shards)

    def copies(ins, outs, sems):
        send_sems, recv_sems, local_sems = sems
        x, y, c = _my_place()
        me, sibling = (x, y, c), (x, y, 1 - c)
        chips = _other_chips(x, y)

        def idx(p):
            return 4 * p[0] + 2 * p[1] + p[2]

        def copy(a, k, block, to, src=None):
            return pltpu.make_async_remote_copy(
                src_ref=outs[a].at[idx(block)] if src is None else src, dst_ref=outs[a].at[idx(block)],
                send_sem=send_sems.at[a, k], recv_sem=recv_sems.at[a, k], device_id=to, device_id_type=MESH)

        mine = [pltpu.make_async_copy(ins[a], outs[a].at[idx(me)], local_sems.at[a]) for a in range(n)]
        first = []
        for a in range(n):
            first.append(copy(a, 0, me, sibling, src=ins[a]))
            first += [copy(a, 1 + j, me, (*chip, c), src=ins[a]) for j, chip in enumerate(chips)]
        arrived = [copy(a, 1 + j, (*chip, c), me) for j, chip in enumerate(chips) for a in range(n)]
        passed = [copy(a, 4 + j, (*chip, c), sibling) for j, chip in enumerate(chips) for a in range(n)]
        from_sibling = []
        for a in range(n):
            from_sibling.append(copy(a, 0, sibling, me))
            from_sibling += [copy(a, 4 + j, (*chip, 1 - c), me) for j, chip in enumerate(chips)]
        return mine, first, arrived, passed, from_sibling

    def start(ins, outs, sems):
        mine, first, _, _, _ = copies(ins, outs, sems)
        for cp in mine + first:
            cp.start()

    def forward(ins, outs, sems):
        _, _, arrived, passed, _ = copies(ins, outs, sems)
        for landed, onward in zip(arrived, passed):
            landed.wait_recv()
            onward.start()

    def finish(ins, outs, sems):
        mine, first, _, passed, from_sibling = copies(ins, outs, sems)
        for cp in from_sibling:
            cp.wait_recv()
        for cp in first + passed:
            cp.wait_send()
        for cp in mine:
            cp.wait()

    return _Plan(shards, [jax.ShapeDtypeStruct((N_DEV,) + s.shape, s.dtype) for s in shards],
                 [pltpu.SemaphoreType.DMA((n, 7)), pltpu.SemaphoreType.DMA((n, 7)), pltpu.SemaphoreType.DMA((n,))],
                 start, finish, forward, forward_at)


def _sibling_plan(scatter, whole=()):
    ns = len(scatter)
    n = ns + len(whole)

    def copies(ins, outs, sems):
        send_sems, recv_sems = sems
        x, y, c = _my_place()
        out = []
        for a in range(n):
            for k in range(4 if a < ns else 1):
                src = ins[a].at[2 * k + (1 - c)] if a < ns else ins[a]
                dst = outs[a].at[k] if a < ns else outs[a]
                out.append(pltpu.make_async_remote_copy(
                    src_ref=src, dst_ref=dst, send_sem=send_sems.at[a, k], recv_sem=recv_sems.at[a, k],
                    device_id=(x, y, 1 - c), device_id_type=MESH))
        return out

    def start(ins, outs, sems):
        for cp in copies(ins, outs, sems):
            cp.start()

    def finish(ins, outs, sems):
        for cp in copies(ins, outs, sems):
            cp.wait()

    shapes = [jax.ShapeDtypeStruct((4,) + p.shape[1:], p.dtype) for p in scatter]
    shapes += [jax.ShapeDtypeStruct(p.shape, p.dtype) for p in whole]
    return _Plan(list(scatter) + list(whole), shapes,
                 [pltpu.SemaphoreType.DMA((n, 4)), pltpu.SemaphoreType.DMA((n, 4))], start, finish)


def _pair_sum(mine, theirs, c_idx, name):
    _, R, C = mine.shape

    def body(c_ref, a_ref, b_ref, o_ref):
        o_ref[...] = (a_ref[...].astype(F32) + b_ref[...].astype(F32)).astype(o_ref.dtype)

    grid_spec = pltpu.PrefetchScalarGridSpec(
        num_scalar_prefetch=1, grid=(4,),
        in_specs=[pl.BlockSpec((1, R, C), lambda k, c_ref: (2 * k + c_ref[0], 0, 0)),
                  pl.BlockSpec((1, R, C), lambda k, c_ref: (k, 0, 0))],
        out_specs=pl.BlockSpec((1, R, C), lambda k, c_ref: (k, 0, 0)))
    return pl.pallas_call(
        body, name=name, grid_spec=grid_spec,
        out_shape=jax.ShapeDtypeStruct((4, R, C), mine.dtype),
        compiler_params=_cparams(1),
    )(c_idx, mine, theirs)


def _pair_sum_whole(mine, theirs, name):
    n = len(mine)

    def body(*refs):
        for a in range(n):
            refs[2 * n + a][...] = refs[a][...] + refs[n + a][...]

    vm = pl.BlockSpec(memory_space=pltpu.VMEM)
    return pl.pallas_call(
        body, name=name, out_shape=tuple(jax.ShapeDtypeStruct(p.shape, p.dtype) for p in mine),
        in_specs=[vm] * (2 * n), out_specs=tuple([vm] * n),
    )(*mine, *theirs)


def _peers_plan(arrays):
    n = len(arrays)

    def copies(ins, outs, sems):
        send_sems, recv_sems, local_sems = sems
        x, y, c = _my_place()
        me = 4 * x + 2 * y + c
        out = [pltpu.make_async_copy(ins[a], outs[a].at[me], local_sems.at[a]) for a in range(n)]
        for a in range(n):
            for k in range(N_DEV - 1):
                bits = k + 1
                peer = (x ^ (bits >> 2), y ^ ((bits >> 1) & 1), c ^ (bits & 1))
                out.append(pltpu.make_async_remote_copy(
                    src_ref=ins[a], dst_ref=outs[a].at[me], send_sem=send_sems.at[a, k], recv_sem=recv_sems.at[a, k],
                    device_id=peer, device_id_type=MESH))
        return out

    def start(ins, outs, sems):
        for cp in copies(ins, outs, sems):
            cp.start()

    def finish(ins, outs, sems):
        for cp in copies(ins, outs, sems):
            cp.wait()

    return _Plan(list(arrays), [jax.ShapeDtypeStruct((N_DEV,) + a.shape, a.dtype) for a in arrays],
                 [pltpu.SemaphoreType.DMA((n, N_DEV - 1)), pltpu.SemaphoreType.DMA((n, N_DEV - 1)),
                  pltpu.SemaphoreType.DMA((n,))], start, finish)


def _chips_plan(scatter, whole=()):
    ns = len(scatter)
    n = ns + len(whole)

    def copies(ins, outs, sems):
        send_sems, recv_sems, local_sems = sems
        x, y, c = _my_place()
        my_chip = 2 * x + y

        def src(a, k):
            return ins[a].at[k] if a < ns else ins[a]

        local = [pltpu.make_async_copy(src(a, my_chip), outs[a].at[my_chip], local_sems.at[a]) for a in range(n)]
        remote = []
        for a in range(n):
            for j, (px, py) in enumerate(_other_chips(x, y)):
                remote.append(pltpu.make_async_remote_copy(
                    src_ref=src(a, 2 * px + py), dst_ref=outs[a].at[my_chip],
                    send_sem=send_sems.at[a, j], recv_sem=recv_sems.at[a, j],
                    device_id=(px, py, c), device_id_type=MESH))
        return local + remote

    def start(ins, outs, sems):
        for cp in copies(ins, outs, sems):
            cp.start()

    def finish(ins, outs, sems):
        for cp in copies(ins, outs, sems):
            cp.wait()

    shapes = [jax.ShapeDtypeStruct(s.shape, s.dtype) for s in scatter]
    shapes += [jax.ShapeDtypeStruct((4,) + s.shape, s.dtype) for s in whole]
    return _Plan(list(scatter) + list(whole), shapes,
                 [pltpu.SemaphoreType.DMA((n, 3)), pltpu.SemaphoreType.DMA((n, 3)), pltpu.SemaphoreType.DMA((n,))],
                 start, finish)


def _small_update(parts, w, m, v, loss_parts, name):
    n = len(parts)

    def total(ref):
        acc = ref[0]
        for k in range(1, ref.shape[0]):
            acc = acc + ref[k]
        return acc

    def body(*refs):
        p_refs, w_refs, m_refs, v_refs = (refs[i * n:(i + 1) * n] for i in range(4))
        lp_ref = refs[4 * n]
        outs = refs[4 * n + 1:]
        g_refs, d_refs, mo_refs, vo_refs = (outs[i * n:(i + 1) * n] for i in range(4))
        for a in range(n):
            g = total(p_refs[a])
            g_refs[a][...] = g
            d_refs[a][...], mo_refs[a][...], vo_refs[a][...] = _adam_math(w_refs[a][...], g, m_refs[a][...],
                                                                          v_refs[a][...])
        outs[4 * n][...] = total(lp_ref)

    vm = pl.BlockSpec(memory_space=pltpu.VMEM)
    shapes = [jax.ShapeDtypeStruct(t.shape, F32) for _ in range(4) for t in w]
    shapes.append(jax.ShapeDtypeStruct(loss_parts.shape[1:], F32))
    outs = pl.pallas_call(
        body, name=name, out_shape=tuple(shapes),
        in_specs=[vm] * (4 * n + 1), out_specs=tuple([vm] * (4 * n + 1)),
        compiler_params=pltpu.CompilerParams(vmem_limit_bytes=VMEM_LIMIT),
    )(*parts, *w, *m, *v, loss_parts)
    return outs[0:n], outs[n:2 * n], outs[2 * n:3 * n], outs[3 * n:4 * n], outs[4 * n]


BIG = [("w_in", 1), ("w_branch_att", 1), ("w_branch_sg", 1), ("w_out", 0), ("w_xq", 0), ("w_xkv", 1), ("w_xo", 0),
       ("w_ffn_in", 1), ("w_ffn_out", 0)]
SMALL = [("norm_mix_g", (1, D)), ("rel_bias", (8, NREL)), ("sg_ln_g", (8, 64)), ("sg_ln_b", (8, 64)),
         ("sg_w", (8, 128, 128)), ("sg_b", (8, 128)), ("norm_xattn_g", (1, D)), ("norm_mem_g", (1, D)),
         ("norm_ffn_g", (1, D)), ("norm_final_g", (1, D))]
ADAM_ROWS = {"w_in": 256, "w_branch_att": 512, "w_branch_sg": 512, "w_xkv": 1024, "w_ffn_in": 256}


def _full(gathered):
    return gathered.reshape(N_DEV * gathered.shape[1], gathered.shape[2])


def _blocks(grad):
    return grad.reshape(N_DEV, grad.shape[0] // N_DEV, grad.shape[1])


def kernel(x, mem, norm_mix_g, w_in, rel_bias, sg_ln_g, sg_ln_b, sg_w, sg_b, w_branch_att, w_branch_sg, w_out, norm_xattn_g, norm_mem_g, w_xq, w_xkv, w_xo, norm_ffn_g, w_ffn_in, w_ffn_out, norm_final_g, loss_target, m_norm_mix_g, m_w_in, m_rel_bias, m_sg_ln_g, m_sg_ln_b, m_sg_w, m_sg_b, m_w_branch_att, m_w_branch_sg, m_w_out, m_norm_xattn_g, m_norm_mem_g, m_w_xq, m_w_xkv, m_w_xo, m_norm_ffn_g, m_w_ffn_in, m_w_ffn_out, m_norm_final_g, v_norm_mix_g, v_w_in, v_rel_bias, v_sg_ln_g, v_sg_ln_b, v_sg_w, v_sg_b, v_w_branch_att, v_w_branch_sg, v_w_out, v_norm_xattn_g, v_norm_mem_g, v_w_xq, v_w_xkv, v_w_xo, v_norm_ffn_g, v_w_ffn_in, v_w_ffn_out, v_norm_final_g):
    args = dict(locals())
    big_names = [n for n, _ in BIG]
    small_names = [n for n, _ in SMALL]
    S = x.shape[1]

    x, mem, tgt = x.reshape(S, D), mem.reshape(MEM, D), loss_target.reshape(S, D)
    small = {n: args[n].reshape(shape) for n, shape in SMALL}
    g1, g2, g3 = small["norm_mix_g"], small["norm_xattn_g"], small["norm_ffn_g"]
    g_mem, g4 = small["norm_mem_g"], small["norm_final_g"]
    lng = small["sg_ln_g"].reshape(1, SG_W)
    lnb = small["sg_ln_b"].reshape(1, SG_W)
    b_exp = jnp.broadcast_to(small["sg_b"].T[:, :, None], (128, 8, 64)).reshape(128, SG_W)
    rel_pad = jnp.pad(small["rel_bias"], ((0, 0), (0, 384 - NREL)))
    c_idx = lax.axis_index("c").astype(jnp.int32).reshape(1)

    shard = {n: (args[n][0].T if axis == 1 else args[n][0]).astype(BF16) for n, axis in BIG}
    w_in_t = _full(_run_plan(_gather_plan([shard["w_in"]]), "ag_w_in")[0])
    bias = _bias_table(rel_pad)
    mix_names = ["w_branch_att", "w_branch_sg", "w_out", "w_xq", "w_xkv", "w_xo"]
    qkv, uv, gates, h, *got = _in_proj(x, g1, w_in_t, plans=[_gather_plan([shard[n] for n in mix_names])])
    wba_t, wbs_t, w_out_f, w_xq_f, w_xkv_t, w_xo_f = (_full(g) for g in got)
    y_att, lse, *got = _attn_fwd(qkv, bias, plans=[_gather_plan([shard["w_ffn_in"], shard["w_ffn_out"]])])
    w_ffn_in_t, w_ffn_out_f = (_full(g) for g in got)
    y_sg = _sgu_fwd(uv, lng, lnb, small["sg_w"], b_exp)
    x1 = _merge_fwd(x, y_att, y_sg, gates, wba_t, wbs_t, w_out_f)
    kv, mn = _mem_kv(mem, g_mem, w_xkv_t)
    x2, hx, qx, o_x = _xattn_fwd(x1, g2, w_xq_f, kv, w_xo_f)
    dx3, gu, hf, act, loss_part, dg4 = _ffn_fwd(x2, tgt, g3, w_ffn_in_t, w_ffn_out_f, g4)

    def pair_sums(names, mine, theirs):
        return [_pair_sum(a, b, c_idx, "rs_pair_" + n) for n, a, b in zip(names, mine, theirs)]

    dx2, dgu, dg3 = _ffn_bwd(dx3, gu, x2, g3, w_ffn_out_f, w_ffn_in_t)
    ffn_names = ["w_ffn_out", "w_ffn_in"]
    ffn_mine = [_blocks(_dw(act, dx3, 1408, 1024, DW_TOKENS, "dw_ffn_out")),
                _blocks(_dw(dgu, hf, 1408, 1024, DW_TOKENS, "dw_ffn_in"))]
    dx1, dq_x, dkv, dg2, *ffn_theirs = _xattn_bwd(dx2, x1, qx, g2, w_xq_f, w_xo_f, kv,
                                                  plans=[_sibling_plan(ffn_mine)])
    ffn_chip = pair_sums(ffn_names, ffn_mine, ffn_theirs)
    d_xkv, dg_mem = _mem_kv_bwd(dkv, mem, g_mem, mn, w_xkv_t)
    merged, d_a, d_b, dy_att, dy_sg, dgates = _merge_bwd(dx1, y_att, y_sg, gates, wba_t, wbs_t, w_out_f)
    mid_names = ["w_xo", "w_xq", "w_xkv", "w_out", "w_branch_att", "w_branch_sg"]
    mid_mine = [_blocks(g) for g in (
        _dw(o_x, dx2, 1024, 1024, DW_TOKENS, "dw_xo"), _dw(hx, dq_x, 1024, 1024, DW_TOKENS, "dw_xq"), d_xkv,
        _dw(merged, dx1, 1024, 1024, DW_TOKENS, "dw_out"), _dw(d_a, y_att, 1024, 512, DW_TOKENS, "dw_branch_att"),
        _dw(d_b, y_sg, 1024, 512, DW_TOKENS, "dw_branch_sg"))]
    duv, d_sgw, d_bx, d_lng, d_lnb, *got = _sgu_bwd(uv, dy_sg, lng, lnb, small["sg_w"], b_exp,
                                                    plans=[_chips_plan(ffn_chip), _sibling_plan(mid_mine)])
    ffn_all, mid_theirs = got[:2], got[2:]
    mid_chip = pair_sums(mid_names, mid_mine, mid_theirs)
    dq, dk, dv, ds_sum, *mid_all = _attn_bwd(qkv, dy_att, y_att, lse, bias, plans=[_chips_plan(mid_chip)])
    d_rel = _bias_grad(ds_sum)

    gs = {"rel_bias": d_rel[:, :NREL], "sg_ln_g": d_lng.reshape(8, 64),
          "sg_ln_b": d_lnb.reshape(8, 64), "sg_w": d_sgw, "sg_b": d_bx.reshape(128, 8, 64)[:, :, 0].T,
          "norm_xattn_g": dg2, "norm_mem_g": dg_mem, "norm_ffn_g": dg3, "norm_final_g": dg4}
    early = [n for n in small_names if n != "norm_mix_g"]
    grad_x, dz, dg1, *everyone = _in_bwd(dq, dk, dv, duv, dgates, x, dx1, g1, w_in_t,
                                         plans=[_peers_plan([gs[n] for n in early] + [loss_part])])
    small_parts = dict(zip(early, everyone[:-1]))

    in_mine = [_blocks(_dw(dz, h, 1152, 1024, DW_TOKENS, "dw_in"))]
    in_theirs, dg1_theirs = _run_plan(_sibling_plan(in_mine, [dg1]), "rs_sibling")
    in_chip = pair_sums(["w_in"], in_mine, [in_theirs])
    (dg1_chip,) = _pair_sum_whole([dg1], [dg1_theirs], "rs_pair_small")
    in_all, small_parts["norm_mix_g"] = _run_plan(_chips_plan(in_chip, [dg1_chip]), "rs_chips")
    all_parts = dict(zip(ffn_names + mid_names + ["w_in"], list(ffn_all) + list(mid_all) + [in_all]))

    res = {}
    for n, axis in BIG:
        parts = all_parts[n]
        wmv = [args[p + n][0] for p in ("", "m_", "v_")]
        tr = ADAM_ROWS[n] if axis == 1 else wmv[0].shape[0]
        res[n] = [t[None] for t in _adam(parts, *wmv, tr, "adam_" + n, transposed=(axis == 1))]
    small_res = _small_update(
        [small_parts[n] for n in small_names], [small[n] for n in small_names],
        [args["m_" + n].reshape(s) for n, s in SMALL], [args["v_" + n].reshape(s) for n, s in SMALL],
        everyone[-1], "adam_small")
    for i, n in enumerate(small_names):
        res[n] = [small_res[k][i].reshape(args[n].shape) for k in range(4)]
    loss = small_res[4][0, 0]

    order = ["norm_mix_g", "w_in", "rel_bias", "sg_ln_g", "sg_ln_b", "sg_w", "sg_b", "w_branch_att", "w_branch_sg",
             "w_out", "norm_xattn_g", "norm_mem_g", "w_xq", "w_xkv", "w_xo", "norm_ffn_g", "w_ffn_in", "w_ffn_out",
             "norm_final_g"]
    outs = [loss, grad_x.reshape(1, S, D)]
    for k in range(4):
        outs += [res[n][k] for n in order]
    return tuple(outs)
```

```python
import math

import jax
import jax.numpy as jnp
from jax import lax
from jax.experimental import pallas as pl
from jax.experimental.pallas import tpu as pltpu

F32 = jnp.float32
BF16 = jnp.bfloat16

D = 1024
ATT_W = 512
SG_W = 512
IN_COLS = 4608
DFF = 2816
MEM = 256
XH = 4
CHUNK = 64
BAND_KEYS = 640
ATT_R = 512
ATT_SUBS = ATT_R // 128
DW_TOKENS = 1024
REL_CLIP = 128
NREL = 2 * REL_CLIP + 1
EPS = 1e-6
NEG = -1e30
N_DEV = 8

ADAM_LR = 0.001
ADAM_B1 = 0.9
ADAM_B2 = 0.999
ADAM_EPS = 1e-08
ADAM_WD = 0.01
ADAM_STEP = 10

LANES = 128
VMEM_LIMIT = 56 * 1024 * 1024
MESH = pl.DeviceIdType.MESH


def _cparams(n_axes):
    return pltpu.CompilerParams(dimension_semantics=("arbitrary",) * n_axes, vmem_limit_bytes=VMEM_LIMIT)


def _resident(shape):
    zeros = (0,) * len(shape)
    return pl.BlockSpec(shape, lambda *_: zeros, pipeline_mode=pl.Buffered(1))


def _rows(tm, cols, col_block=0):
    return pl.BlockSpec((tm, cols), lambda i: (i, col_block))


def _sigmoid(x):
    return 1.0 / (1.0 + jnp.exp(-x))


_GELU_C = math.sqrt(2.0 / math.pi)


def _gelu(x):
    t = jnp.tanh(_GELU_C * (x + 0.044715 * (x * x * x)))
    return x * (0.5 * (1.0 + t))


def _gelu_and_grad(x):
    x2 = x * x
    t = jnp.tanh(_GELU_C * (x + 0.044715 * (x2 * x)))
    cdf = 0.5 * (1.0 + t)
    dcdf = 0.5 * (1.0 - t * t) * (_GELU_C * (1.0 + 3.0 * 0.044715 * x2))
    return x * cdf, cdf + x * dcdf


def _rstd(x):
    return lax.rsqrt(jnp.mean(x * x, axis=-1, keepdims=True) + EPS)


def _rms_bwd(dh, x, r, g):
    xh = x * r
    dxh = dh * g
    dx = r * (dxh - xh * jnp.mean(dxh * xh, axis=-1, keepdims=True))
    dg = jnp.sum(dh * xh, axis=0, keepdims=True)
    return dx, dg


def _group_sum64(x):
    r = lax.broadcasted_iota(jnp.int32, (LANES, LANES), 0) // 64
    c = lax.broadcasted_iota(jnp.int32, (LANES, LANES), 1) // 64
    same_group = (r == c).astype(BF16)

    def one(v):
        hi = v.astype(BF16)
        rest = v - hi.astype(F32)
        mid = rest.astype(BF16)
        lo = (rest - mid.astype(F32)).astype(BF16)
        return _dot(hi, same_group) + _dot(mid, same_group) + _dot(lo, same_group)

    pieces = [one(x[:, LANES * j:LANES * (j + 1)]) for j in range(x.shape[1] // LANES)]
    return pieces[0] if len(pieces) == 1 else jnp.concatenate(pieces, axis=1)


def _dot(a, b):
    return jnp.dot(a, b, preferred_element_type=F32)


def _dot_nt(a, b):
    return lax.dot_general(a, b, (((1,), (1,)), ((), ())), preferred_element_type=F32)


def _dot_tn(a, b):
    return lax.dot_general(a, b, (((0,), (0,)), ((), ())), preferred_element_type=F32)


DIAGS = 768


def _diag_onehot():
    r_idx = lax.broadcasted_iota(jnp.int32, (384, DIAGS), 0)
    t_idx = lax.broadcasted_iota(jnp.int32, (384, DIAGS), 1)
    dist = (8 * CHUNK + 127) - t_idx
    return (jnp.clip(dist, -REL_CLIP, REL_CLIP) + REL_CLIP == r_idx).astype(F32)


def _shift_rows(x, reverse):
    row = lax.broadcasted_iota(jnp.int32, x.shape, 0)
    for k in range(7):
        amt = (DIAGS - (1 << k)) if reverse else (1 << k)
        x = jnp.where(((row >> k) & 1) == 1, pltpu.roll(x, amt, 1), x)
    return x


N_TABLES = 1 + ATT_SUBS


def _bias_table(rel_bias_pad):
    def body(rb_ref, out_ref):
        per_diag = jnp.dot(rb_ref[...], _diag_onehot(), preferred_element_type=F32,
                           precision=lax.Precision.HIGHEST)
        a = lax.broadcasted_iota(jnp.int32, (128, BAND_KEYS), 0)
        b = lax.broadcasted_iota(jnp.int32, (128, BAND_KEYS), 1)
        band = (b // CHUNK >= a // CHUNK) & (b // CHUNK <= a // CHUNK + 8)
        for h in range(8):
            rows = jnp.broadcast_to(per_diag[h:h + 1, :], (128, DIAGS))
            table = _shift_rows(pltpu.roll(rows, DIAGS - 127, 1), reverse=False)[:, :BAND_KEYS]
            out_ref[0, h] = jnp.where(band, table, NEG)
            for s in range(ATT_SUBS):
                out_ref[1 + s, h] = jnp.where(band & (b >= 8 * CHUNK - 128 * s), table, NEG)

    return pl.pallas_call(
        body, name="bias_table",
        out_shape=jax.ShapeDtypeStruct((N_TABLES, 8, 128, BAND_KEYS), F32),
        in_specs=[pl.BlockSpec(memory_space=pltpu.VMEM)],
        out_specs=pl.BlockSpec(memory_space=pltpu.VMEM),
        compiler_params=pltpu.CompilerParams(vmem_limit_bytes=VMEM_LIMIT),
    )(rel_bias_pad)


def _bias_grad(ds_sum):
    def body(ds_ref, out_ref):
        sums = []
        for h in range(8):
            padded = jnp.concatenate([ds_ref[h], jnp.zeros((128, DIAGS - BAND_KEYS), F32)], axis=1)
            skewed = pltpu.roll(_shift_rows(padded, reverse=True), 127, 1)
            sums.append(jnp.sum(skewed, axis=0, keepdims=True))
        per_diag = jnp.concatenate(sums, axis=0)
        out_ref[...] = lax.dot_general(per_diag, _diag_onehot(), (((1,), (1,)), ((), ())),
                                       preferred_element_type=F32, precision=lax.Precision.HIGHEST)

    return pl.pallas_call(
        body, name="bias_grad",
        out_shape=jax.ShapeDtypeStruct((8, 384), F32),
        in_specs=[pl.BlockSpec(memory_space=pltpu.VMEM)],
        out_specs=pl.BlockSpec(memory_space=pltpu.VMEM),
    )(ds_sum)


def _in_proj(x, g1, w_in_t, plans=(), tm=512):
    S = x.shape[0]

    def body(x_ref, g_ref, w_ref, qkv_ref, uv_ref, gate_ref, h_ref):
        xv = x_ref[...]
        h = (xv * _rstd(xv) * g_ref[...]).astype(BF16)
        h_ref[...] = h
        for c in range(IN_COLS // 512):
            zc = _dot_nt(h, w_ref[512 * c:512 * (c + 1), :])
            if c == 0:
                qkv_ref[:, 0:512] = (zc * 0.125).astype(BF16)
            elif c < 3:
                qkv_ref[:, 512 * c:512 * (c + 1)] = zc.astype(BF16)
            elif c < 5:
                uv_ref[:, 512 * (c - 3):512 * (c - 2)] = zc.astype(BF16)
            else:
                gate_ref[:, 512 * (c - 5):512 * (c - 4)] = zc.astype(BF16)

    return _call(
        body, name="in_proj", grid=(S // tm,),
        out_shape=(jax.ShapeDtypeStruct((S, 3 * ATT_W), BF16), jax.ShapeDtypeStruct((S, 2 * SG_W), BF16),
                   jax.ShapeDtypeStruct((S, 2 * D), BF16), jax.ShapeDtypeStruct((S, D), BF16)),
        in_specs=[_rows(tm, D), _resident((1, D)), _resident((IN_COLS, D))],
        out_specs=(_rows(tm, 3 * ATT_W), _rows(tm, 2 * SG_W), _rows(tm, 2 * D), _rows(tm, D)),
        operands=(x, g1, w_in_t), plans=plans)


def _two_heads(a, lo):
    return jnp.concatenate([jnp.where(lo, a, 0), jnp.where(lo, 0, a)], axis=0)


def _att_specs():
    R = ATT_R
    q = pl.BlockSpec((R, LANES), lambda j, i: (i, j))
    kp = pl.BlockSpec((R, LANES), lambda j, i: (jnp.maximum(i - 1, 0), 4 + j))
    kc = pl.BlockSpec((R, LANES), lambda j, i: (i, 4 + j))
    vp = pl.BlockSpec((R, LANES), lambda j, i: (jnp.maximum(i - 1, 0), 8 + j))
    vc = pl.BlockSpec((R, LANES), lambda j, i: (i, 8 + j))
    bias = pl.BlockSpec((N_TABLES, 2, 128, BAND_KEYS), lambda j, i: (0, j, 0, 0))
    return [q, kp, kc, vp, vc, bias]


def _attn_fwd(qkv, bias, plans=()):
    S = qkv.shape[0]
    R = ATT_R

    def body(q_ref, kp_ref, kc_ref, vp_ref, vc_ref, b_ref, o_ref, lse_ref):
        i = pl.program_id(1)
        lo = lax.broadcasted_iota(jnp.int32, (1, LANES), 1) < 64
        kwin = jnp.concatenate([kp_ref[...], kc_ref[...]], axis=0)
        vwin = jnp.concatenate([vp_ref[...], vc_ref[...]], axis=0)
        for sub in range(ATT_SUBS):
            q2 = q_ref[128 * sub:128 * (sub + 1), :]
            kw = kwin[128 * sub:128 * sub + BAND_KEYS]
            vw = vwin[128 * sub:128 * sub + BAND_KEYS]
            table = jnp.where(i == 0, 1 + sub, 0)
            s = _dot_nt(_two_heads(q2, lo), kw) + b_ref[table].reshape(256, BAND_KEYS)
            top = jnp.max(s, axis=-1, keepdims=True)
            p = jnp.exp(s - top)
            total = jnp.sum(p, axis=-1, keepdims=True)
            p = (p / total).astype(BF16)
            o = _dot(jnp.concatenate([p[:128], p[128:]], axis=1), _two_heads(vw, lo))
            lse = top + jnp.log(total)
            o_ref[128 * sub:128 * (sub + 1), :] = o.astype(BF16)
            lse_ref[128 * sub:128 * (sub + 1), :] = jnp.where(lo, lse[:128], lse[128:])

    blk = pl.BlockSpec((R, LANES), lambda j, i: (i, j))
    return _call(
        body, name="attn_fwd", grid=(4, S // R),
        out_shape=(jax.ShapeDtypeStruct((S, ATT_W), BF16), jax.ShapeDtypeStruct((S, ATT_W), F32)),
        in_specs=_att_specs(), out_specs=(blk, blk),
        operands=(qkv, qkv, qkv, qkv, qkv, bias), plans=plans)


def _sg_mask():
    t = lax.broadcasted_iota(jnp.int32, (128, 128), 0)
    s = lax.broadcasted_iota(jnp.int32, (128, 128), 1)
    return (s // CHUNK) <= (t // CHUNK)


def _sg_layernorm(gv, lng, lnb):
    mu = _group_sum64(gv) * (1.0 / 64)
    xc = gv - mu
    var = _group_sum64(xc * xc) * (1.0 / 64)
    rstd = lax.rsqrt(var + EPS)
    vhat = xc * rstd
    return vhat * lng + lnb, vhat, rstd


def _sgu_fwd(uv, lng, lnb, sg_w, b_exp, tm=512):
    S = uv.shape[0]

    def body(uv_ref, lng_ref, lnb_ref, w_ref, b_ref, y_ref):
        lane = lax.broadcasted_iota(jnp.int32, (1, LANES), 1)
        lo = lane < 64
        mask = _sg_mask()
        gu = _gelu(uv_ref[:, 0:SG_W].astype(F32))
        vln, _, _ = _sg_layernorm(_gelu(uv_ref[:, SG_W:2 * SG_W].astype(F32)), lng_ref[...], lnb_ref[...])
        for gp in range(4):
            w0 = jnp.where(mask, w_ref[2 * gp], 0).astype(BF16)
            w1 = jnp.where(mask, w_ref[2 * gp + 1], 0).astype(BF16)
            cols = slice(128 * gp, 128 * (gp + 1))
            for n in range(tm // 128):
                rows = slice(128 * n, 128 * (n + 1))
                vl = vln[rows, cols]
                sv = (_dot(w0, jnp.where(lo, vl, 0).astype(BF16)) + _dot(w1, jnp.where(lo, 0, vl).astype(BF16))
                      + b_ref[:, cols])
                y_ref[rows, cols] = (gu[rows, cols] * sv).astype(BF16)

    return pl.pallas_call(
        body, name="sgu_fwd", grid=(S // tm,),
        out_shape=jax.ShapeDtypeStruct((S, SG_W), BF16),
        in_specs=[_rows(tm, 2 * SG_W), _resident((1, SG_W)), _resident((1, SG_W)),
                  _resident((8, 128, 128)), _resident((128, SG_W))],
        out_specs=_rows(tm, SG_W),
        compiler_params=_cparams(1),
    )(uv, lng, lnb, sg_w, b_exp)


def _merge_fwd(x, y_att, y_sg, gates, wba_t, wbs_t, w_out, tm=512):
    S = x.shape[0]

    def body(x_ref, ya_ref, ys_ref, g_ref, wba_ref, wbs_ref, wo_ref, x1_ref):
        a = _dot_nt(ya_ref[...], wba_ref[...])
        b = _dot_nt(ys_ref[...], wbs_ref[...])
        merged = _sigmoid(g_ref[:, 0:D].astype(F32)) * a + _sigmoid(g_ref[:, D:2 * D].astype(F32)) * b
        x1_ref[...] = x_ref[...] + _dot(merged.astype(BF16), wo_ref[...])

    return pl.pallas_call(
        body, name="merge_fwd", grid=(S // tm,),
        out_shape=jax.ShapeDtypeStruct((S, D), F32),
        in_specs=[_rows(tm, D), _rows(tm, ATT_W), _rows(tm, SG_W), _rows(tm, 2 * D),
                  _resident((D, ATT_W)), _resident((D, SG_W)), _resident((D, D))],
        out_specs=_rows(tm, D),
        compiler_params=_cparams(1),
    )(x, y_att, y_sg, gates, wba_t, wbs_t, w_out)


def _mem_kv(mem, g_mem, w_xkv_t):
    def body(m_ref, g_ref, w_ref, kv_ref, mn_ref):
        mv = m_ref[...]
        mn = (mv * _rstd(mv) * g_ref[...]).astype(BF16)
        mn_ref[...] = mn
        kv_ref[...] = _dot_nt(mn, w_ref[...]).astype(BF16)

    vm = pl.BlockSpec(memory_space=pltpu.VMEM)
    return pl.pallas_call(
        body, name="mem_kv",
        out_shape=(jax.ShapeDtypeStruct((MEM, 2 * D), BF16), jax.ShapeDtypeStruct((MEM, D), BF16)),
        in_specs=[vm, vm, vm], out_specs=(vm, vm),
        compiler_params=pltpu.CompilerParams(vmem_limit_bytes=VMEM_LIMIT),
    )(mem, g_mem, w_xkv_t)


def _xatt_head(qx, kv_ref, h):
    hs = slice(256 * h, 256 * (h + 1))
    s = _dot_nt(qx[:, hs], kv_ref[:, hs])
    p = jnp.exp(s - jnp.max(s, axis=-1, keepdims=True))
    return p / jnp.sum(p, axis=-1, keepdims=True)


def _xattn_fwd(x1, g2, w_xq, kv, w_xo, tm=512):
    S = x1.shape[0]

    def body(x_ref, g_ref, wq_ref, kv_ref, wo_ref, x2_ref, hx_ref, qx_ref, o_ref):
        xv = x_ref[...]
        hx = (xv * _rstd(xv) * g_ref[...]).astype(BF16)
        hx_ref[...] = hx
        qx = (_dot(hx, wq_ref[...]) * (1.0 / 16)).astype(BF16)
        qx_ref[...] = qx
        for h in range(XH):
            p = _xatt_head(qx, kv_ref, h)
            o_ref[:, 256 * h:256 * (h + 1)] = _dot(p.astype(BF16), kv_ref[:, D + 256 * h:D + 256 * (h + 1)]).astype(BF16)
        x2_ref[...] = xv + _dot(o_ref[...], wo_ref[...])

    return pl.pallas_call(
        body, name="xattn_fwd", grid=(S // tm,),
        out_shape=(jax.ShapeDtypeStruct((S, D), F32),) + (jax.ShapeDtypeStruct((S, D), BF16),) * 3,
        in_specs=[_rows(tm, D), _resident((1, D)), _resident((D, D)), _resident((MEM, 2 * D)), _resident((D, D))],
        out_specs=(_rows(tm, D),) * 4,
        compiler_params=_cparams(1),
    )(x1, g2, w_xq, kv, w_xo)


FF_CHUNK = 1408


def _ffn_fwd(x2, tgt, g3, w_ffn_in_t, w_ffn_out, g4, tm=256):
    S = x2.shape[0]

    def body(x_ref, t_ref, g3_ref, wi_ref, wo_ref, g4_ref, dx3_ref, gu_ref, hf_ref, act_ref, loss_ref, dg4_ref):
        i = pl.program_id(0)
        xv = x_ref[...]
        hf = (xv * _rstd(xv) * g3_ref[...]).astype(BF16)
        hf_ref[...] = hf
        acc = xv
        for c in range(DFF // FF_CHUNK):
            cs = slice(FF_CHUNK * c, FF_CHUNK * (c + 1))
            us = slice(DFF + FF_CHUNK * c, DFF + FF_CHUNK * (c + 1))
            gate = _dot_nt(hf, wi_ref[cs, :])
            up = _dot_nt(hf, wi_ref[us, :])
            gu_ref[:, cs] = gate.astype(BF16)
            gu_ref[:, us] = up.astype(BF16)
            act = ((gate * _sigmoid(gate)) * up).astype(BF16)
            act_ref[:, cs] = act
            acc = acc + _dot(act, wo_ref[cs, :])
        r4 = _rstd(acc)
        g4 = g4_ref[...]
        diff = acc * r4 * g4 - t_ref[...]
        dy = diff * (1.0 / D)
        dx3, dg4 = _rms_bwd(dy, acc, r4, g4)
        dx3_ref[...] = dx3
        part = 0.5 * jnp.sum(jnp.mean(diff * diff, axis=-1, keepdims=True))

        @pl.when(i == 0)
        def _():
            loss_ref[...] = jnp.zeros_like(loss_ref)
            dg4_ref[...] = jnp.zeros_like(dg4_ref)

        loss_ref[...] += jnp.full(loss_ref.shape, part, F32)
        dg4_ref[...] += dg4

    return pl.pallas_call(
        body, name="ffn_fwd", grid=(S // tm,),
        out_shape=(jax.ShapeDtypeStruct((S, D), F32), jax.ShapeDtypeStruct((S, 2 * DFF), BF16),
                   jax.ShapeDtypeStruct((S, D), BF16), jax.ShapeDtypeStruct((S, DFF), BF16),
                   jax.ShapeDtypeStruct((8, LANES), F32), jax.ShapeDtypeStruct((1, D), F32)),
        in_specs=[_rows(tm, D), _rows(tm, D), _resident((1, D)), _resident((2 * DFF, D)), _resident((DFF, D)),
                  _resident((1, D))],
        out_specs=(_rows(tm, D), _rows(tm, 2 * DFF), _rows(tm, D), _rows(tm, DFF),
                   pl.BlockSpec((8, LANES), lambda i: (0, 0)), pl.BlockSpec((1, D), lambda i: (0, 0))),
        compiler_params=_cparams(1),
    )(x2, tgt, g3, w_ffn_in_t, w_ffn_out, g4)


def _ffn_bwd(dx3, gu, x2, g3, w_ffn_out, w_ffn_in_t, tm=256):
    S = x2.shape[0]

    def body(d_ref, gu_ref, x_ref, g3_ref, wo_ref, wit_ref, dx2_ref, dgu_ref, dg3_ref):
        i = pl.program_id(0)
        d3 = d_ref[...]
        d3b = d3.astype(BF16)
        for c in range(DFF // FF_CHUNK):
            cs = slice(FF_CHUNK * c, FF_CHUNK * (c + 1))
            us = slice(DFF + FF_CHUNK * c, DFF + FF_CHUNK * (c + 1))
            da = _dot_nt(d3b, wo_ref[cs, :])
            gate = gu_ref[:, cs].astype(F32)
            up = gu_ref[:, us].astype(F32)
            sg = _sigmoid(gate)
            dgate = (da * up * (sg * (1.0 + gate * (1.0 - sg)))).astype(BF16)
            dup = (da * (gate * sg)).astype(BF16)
            dgu_ref[:, cs] = dgate
            dgu_ref[:, us] = dup
        dhf = _dot(dgu_ref[...], wit_ref[...])
        xv = x_ref[...]
        dx, dg3 = _rms_bwd(dhf, xv, _rstd(xv), g3_ref[...])
        dx2_ref[...] = d3 + dx

        @pl.when(i == 0)
        def _():
            dg3_ref[...] = jnp.zeros_like(dg3_ref)

        dg3_ref[...] += dg3

    return pl.pallas_call(
        body, name="ffn_bwd", grid=(S // tm,),
        out_shape=(jax.ShapeDtypeStruct((S, D), F32), jax.ShapeDtypeStruct((S, 2 * DFF), BF16),
                   jax.ShapeDtypeStruct((1, D), F32)),
        in_specs=[_rows(tm, D), _rows(tm, 2 * DFF), _rows(tm, D), _resident((1, D)),
                  _resident((DFF, D)), _resident((2 * DFF, D))],
        out_specs=(_rows(tm, D), _rows(tm, 2 * DFF), pl.BlockSpec((1, D), lambda i: (0, 0))),
        compiler_params=_cparams(1),
    )(dx3, gu, x2, g3, w_ffn_out, w_ffn_in_t)


def _dw(a, b, tmm, tn, ts, name, out_dtype=BF16, plans=()):
    S, M = a.shape
    N = b.shape[1]
    nk = S // ts

    def body(a_ref, b_ref, o_ref, acc_ref):
        k = pl.program_id(2)

        @pl.when(k == 0)
        def _():
            acc_ref[...] = jnp.zeros_like(acc_ref)

        acc_ref[...] += _dot_tn(a_ref[...].astype(BF16), b_ref[...].astype(BF16))

        @pl.when(k == nk - 1)
        def _():
            o_ref[...] = acc_ref[...].astype(out_dtype)

    out = _call(
        body, name=name, grid=(M // tmm, N // tn, nk),
        out_shape=(jax.ShapeDtypeStruct((M, N), out_dtype),),
        in_specs=[pl.BlockSpec((ts, tmm), lambda m, n, k: (k, m)), pl.BlockSpec((ts, tn), lambda m, n, k: (k, n))],
        out_specs=(pl.BlockSpec((tmm, tn), lambda m, n, k: (m, n)),),
        scratch_shapes=[pltpu.VMEM((tmm, tn), F32)],
        operands=(a, b), plans=plans)
    return out if plans else out[0]


def _xattn_bwd(dx2, x1, qx, g2, w_xq, w_xo, kv, plans=(), tm=512):
    S = x1.shape[0]

    def body(d_ref, x_ref, qx_ref, g_ref, wq_ref, wo_ref, kv_ref, dx1_ref, dq_ref, dkv_ref, dg2_ref):
        i = pl.program_id(0)

        @pl.when(i == 0)
        def _():
            dkv_ref[...] = jnp.zeros_like(dkv_ref)
            dg2_ref[...] = jnp.zeros_like(dg2_ref)

        d2 = d_ref[...]
        qx = qx_ref[...]
        do = _dot_nt(d2.astype(BF16), wo_ref[...]).astype(BF16)
        for h in range(XH):
            hs = slice(256 * h, 256 * (h + 1))
            vs = slice(D + 256 * h, D + 256 * (h + 1))
            p = _xatt_head(qx, kv_ref, h)
            dp = _dot_nt(do[:, hs], kv_ref[:, vs])
            ds = (p * (dp - jnp.sum(dp * p, axis=-1, keepdims=True))).astype(BF16)
            dq_ref[:, hs] = (_dot(ds, kv_ref[:, hs]) * (1.0 / 16)).astype(BF16)
            dkv_ref[:, hs] += _dot_tn(ds, qx[:, hs])
            dkv_ref[:, vs] += _dot_tn(p.astype(BF16), do[:, hs])
        dhx = _dot_nt(dq_ref[...], wq_ref[...])
        xv = x_ref[...]
        dx, dg2 = _rms_bwd(dhx, xv, _rstd(xv), g_ref[...])
        dx1_ref[...] = d2 + dx
        dg2_ref[...] += dg2

    return _call(
        body, name="xattn_bwd", grid=(S // tm,),
        out_shape=(jax.ShapeDtypeStruct((S, D), F32), jax.ShapeDtypeStruct((S, D), BF16),
                   jax.ShapeDtypeStruct((MEM, 2 * D), F32), jax.ShapeDtypeStruct((1, D), F32)),
        in_specs=[_rows(tm, D), _rows(tm, D), _rows(tm, D), _resident((1, D)), _resident((D, D)), _resident((D, D)),
                  _resident((MEM, 2 * D))],
        out_specs=(_rows(tm, D), _rows(tm, D),
                   pl.BlockSpec((MEM, 2 * D), lambda i: (0, 0)), pl.BlockSpec((1, D), lambda i: (0, 0))),
        operands=(dx2, x1, qx, g2, w_xq, w_xo, kv), plans=plans)


def _mem_kv_bwd(dkv, mem, g_mem, mn, w_xkv_t):
    def body(dkv_ref, m_ref, g_ref, mn_ref, wt_ref, dw_ref, dg_ref):
        dkvb = dkv_ref[...].astype(BF16)
        dw_ref[...] = _dot_tn(dkvb, mn_ref[...]).astype(BF16)
        dmn = _dot(dkvb, wt_ref[...])
        mv = m_ref[...]
        dg_ref[...] = jnp.sum(dmn * (mv * _rstd(mv)), axis=0, keepdims=True)

    vm = pl.BlockSpec(memory_space=pltpu.VMEM)
    return pl.pallas_call(
        body, name="mem_kv_bwd",
        out_shape=(jax.ShapeDtypeStruct((2 * D, D), BF16), jax.ShapeDtypeStruct((1, D), F32)),
        in_specs=[vm] * 5, out_specs=(vm, vm),
        compiler_params=pltpu.CompilerParams(vmem_limit_bytes=VMEM_LIMIT),
    )(dkv, mem, g_mem, mn, w_xkv_t)


def _merge_bwd(dx1, y_att, y_sg, gates, wba_t, wbs_t, w_out, tm=512):
    S = dx1.shape[0]

    def body(d_ref, ya_ref, ys_ref, g_ref, wbat_ref, wbst_ref, wo_ref,
             mg_ref, da_ref, db_ref, dya_ref, dys_ref, dg_ref):
        dm = _dot_nt(d_ref[...].astype(BF16), wo_ref[...])
        a = _dot_nt(ya_ref[...], wbat_ref[...])
        b = _dot_nt(ys_ref[...], wbst_ref[...])
        sa = _sigmoid(g_ref[:, 0:D].astype(F32))
        sb = _sigmoid(g_ref[:, D:2 * D].astype(F32))
        mg_ref[...] = (sa * a + sb * b).astype(BF16)
        da = (dm * sa).astype(BF16)
        db = (dm * sb).astype(BF16)
        da_ref[...] = da
        db_ref[...] = db
        dg_ref[:, 0:D] = (dm * a * sa * (1.0 - sa)).astype(BF16)
        dg_ref[:, D:2 * D] = (dm * b * sb * (1.0 - sb)).astype(BF16)
        dya_ref[...] = _dot(da, wbat_ref[...]).astype(BF16)
        dys_ref[...] = _dot(db, wbst_ref[...]).astype(BF16)

    return pl.pallas_call(
        body, name="merge_bwd", grid=(S // tm,),
        out_shape=(jax.ShapeDtypeStruct((S, D), BF16), jax.ShapeDtypeStruct((S, D), BF16),
                   jax.ShapeDtypeStruct((S, D), BF16), jax.ShapeDtypeStruct((S, ATT_W), BF16),
                   jax.ShapeDtypeStruct((S, SG_W), BF16), jax.ShapeDtypeStruct((S, 2 * D), BF16)),
        in_specs=[_rows(tm, D), _rows(tm, ATT_W), _rows(tm, SG_W), _rows(tm, 2 * D),
                  _resident((D, ATT_W)), _resident((D, SG_W)), _resident((D, D))],
        out_specs=(_rows(tm, D), _rows(tm, D), _rows(tm, D), _rows(tm, ATT_W), _rows(tm, SG_W), _rows(tm, 2 * D)),
        compiler_params=_cparams(1),
    )(dx1, y_att, y_sg, gates, wba_t, wbs_t, w_out)


def _sgu_bwd(uv, dy_sg, lng, lnb, sg_w, b_exp, plans=(), tm=512):
    S = uv.shape[0]
    n_steps = S // tm

    def body(uv_ref, dy_ref, lng_ref, lnb_ref, w_ref, b_ref, duv_ref, dw_ref, dbx_ref, dlng_ref, dlnb_ref, dvln_ref):
        i = pl.program_id(0)

        @pl.when(i == 0)
        def _():
            dw_ref[...] = jnp.zeros_like(dw_ref)
            dbx_ref[...] = jnp.zeros_like(dbx_ref)
            dlng_ref[...] = jnp.zeros_like(dlng_ref)
            dlnb_ref[...] = jnp.zeros_like(dlnb_ref)

        lane = lax.broadcasted_iota(jnp.int32, (1, LANES), 1)
        lo = lane < 64
        mask = _sg_mask()
        lng = lng_ref[...]
        gu, dgelu_u = _gelu_and_grad(uv_ref[:, 0:SG_W].astype(F32))
        gv, dgelu_v = _gelu_and_grad(uv_ref[:, SG_W:2 * SG_W].astype(F32))
        vln, vhat, rstd = _sg_layernorm(gv, lng, lnb_ref[...])
        dy = dy_ref[...].astype(F32)
        dsv_all = dy * gu
        for gp in range(4):
            wf0 = jnp.where(mask, w_ref[2 * gp], 0)
            wf1 = jnp.where(mask, w_ref[2 * gp + 1], 0)
            w0 = wf0.astype(BF16)
            w1 = wf1.astype(BF16)
            cols = slice(128 * gp, 128 * (gp + 1))
            dw0 = jnp.zeros((128, 128), F32)
            dw1 = jnp.zeros((128, 128), F32)
            dbx = jnp.zeros((128, LANES), F32)
            for n in range(tm // 128):
                rows = slice(128 * n, 128 * (n + 1))
                vl = vln[rows, cols]
                vl0 = jnp.where(lo, vl, 0).astype(BF16)
                vl1 = jnp.where(lo, 0, vl).astype(BF16)
                sv = _dot(w0, vl0) + _dot(w1, vl1) + b_ref[:, cols]
                duv_ref[rows, cols] = (dy[rows, cols] * sv * dgelu_u[rows, cols]).astype(BF16)
                dsv = dsv_all[rows, cols]
                dbx = dbx + dsv
                ds0 = jnp.where(lo, dsv, 0).astype(BF16)
                ds1 = jnp.where(lo, 0, dsv).astype(BF16)
                dw0 = dw0 + _dot_nt(ds0, vl0)
                dw1 = dw1 + _dot_nt(ds1, vl1)
                dvln_ref[rows, cols] = _dot_tn(w0, ds0) + _dot_tn(w1, ds1)
            dw_ref[2 * gp] += jnp.where(mask, dw0, 0)
            dw_ref[2 * gp + 1] += jnp.where(mask, dw1, 0)
            dbx_ref[:, cols] += dbx
        dvln = dvln_ref[...]
        dlng_ref[...] += jnp.sum(dvln * vhat, axis=0, keepdims=True)
        dlnb_ref[...] += jnp.sum(dvln, axis=0, keepdims=True)
        dvh = dvln * lng
        dgv = rstd * (dvh - _group_sum64(dvh) * (1.0 / 64) - vhat * (_group_sum64(dvh * vhat) * (1.0 / 64)))
        duv_ref[:, SG_W:2 * SG_W] = (dgv * dgelu_v).astype(BF16)

        @pl.when(i == n_steps - 1)
        def _():
            dbx_ref[...] = _group_sum64(dbx_ref[...])

    return _call(
        body, name="sgu_bwd", grid=(n_steps,),
        out_shape=(jax.ShapeDtypeStruct((S, 2 * SG_W), BF16), jax.ShapeDtypeStruct((8, 128, 128), F32),
                   jax.ShapeDtypeStruct((128, SG_W), F32), jax.ShapeDtypeStruct((1, SG_W), F32),
                   jax.ShapeDtypeStruct((1, SG_W), F32)),
        in_specs=[_rows(tm, 2 * SG_W), _rows(tm, SG_W), _resident((1, SG_W)), _resident((1, SG_W)),
                  _resident((8, 128, 128)), _resident((128, SG_W))],
        out_specs=(_rows(tm, 2 * SG_W), pl.BlockSpec((8, 128, 128), lambda i: (0, 0, 0)),
                   pl.BlockSpec((128, SG_W), lambda i: (0, 0)), pl.BlockSpec((1, SG_W), lambda i: (0, 0)),
                   pl.BlockSpec((1, SG_W), lambda i: (0, 0))),
        scratch_shapes=[pltpu.VMEM((tm, SG_W), F32)],
        operands=(uv, dy_sg, lng, lnb, sg_w, b_exp), plans=plans)


def _attn_bwd(qkv, dy_att, y_att, lse, bias, plans=()):
    S = qkv.shape[0]
    R = ATT_R

    def body(q_ref, kp_ref, kc_ref, vp_ref, vc_ref, b_ref, dy_ref, y_ref, lse_ref, dq_ref, dk_ref, dv_ref, dss_ref):
        i = pl.program_id(1)

        @pl.when(i == 0)
        def _():
            dk_ref[...] = jnp.zeros_like(dk_ref)
            dv_ref[...] = jnp.zeros_like(dv_ref)
            dss_ref[...] = jnp.zeros_like(dss_ref)

        lane = lax.broadcasted_iota(jnp.int32, (1, LANES), 1)
        kwin = jnp.concatenate([kp_ref[...], kc_ref[...]], axis=0)
        vwin = jnp.concatenate([vp_ref[...], vc_ref[...]], axis=0)
        for sub in range(ATT_SUBS):
            rows = slice(128 * sub, 128 * (sub + 1))
            q2 = q_ref[rows, :]
            do2 = dy_ref[rows, :]
            kw = kwin[128 * sub:128 * sub + BAND_KEYS]
            vw = vwin[128 * sub:128 * sub + BAND_KEYS]
            table = jnp.where(i == 0, 1 + sub, 0)
            dyy = do2.astype(F32) * y_ref[rows, :].astype(F32)
            lse2 = lse_ref[rows, :]
            dq = jnp.zeros((128, LANES), F32)
            dkw = jnp.zeros((BAND_KEYS, LANES), F32)
            dvw = jnp.zeros((BAND_KEYS, LANES), F32)
            for hh in range(2):
                mine = (lane >= 64 * hh) & (lane < 64 * (hh + 1))
                qm = jnp.where(mine, q2, 0)
                dom = jnp.where(mine, do2, 0)
                p = jnp.exp(_dot_nt(qm, kw) + b_ref[table, hh] - lse2[:, 64 * hh:64 * hh + 1])
                dp = _dot_nt(dom, vw)
                ds = p * (dp - jnp.sum(jnp.where(mine, dyy, 0.0), axis=-1, keepdims=True))
                dss_ref[hh] += ds
                dsb = ds.astype(BF16)
                dq = dq + _dot(dsb, jnp.where(mine, kw, 0))
                dkw = dkw + _dot_tn(dsb, qm)
                dvw = dvw + _dot_tn(p.astype(BF16), dom)
            dq_ref[rows, :] = dq.astype(BF16)
            start = pl.multiple_of(i * R + 128 * sub, 128)
            dk_ref[pl.ds(start, BAND_KEYS), :] += dkw
            dv_ref[pl.ds(start, BAND_KEYS), :] += dvw

    acc_spec = pl.BlockSpec((S + 8 * CHUNK, LANES), lambda j, i: (0, j))
    return _call(
        body, name="attn_bwd", grid=(4, S // R),
        out_shape=(jax.ShapeDtypeStruct((S, ATT_W), BF16), jax.ShapeDtypeStruct((S + 8 * CHUNK, ATT_W), F32),
                   jax.ShapeDtypeStruct((S + 8 * CHUNK, ATT_W), F32), jax.ShapeDtypeStruct((8, 128, BAND_KEYS), F32)),
        in_specs=_att_specs() + [pl.BlockSpec((R, LANES), lambda j, i: (i, j))] * 3,
        out_specs=(pl.BlockSpec((R, LANES), lambda j, i: (i, j)), acc_spec, acc_spec,
                   pl.BlockSpec((2, 128, BAND_KEYS), lambda j, i: (j, 0, 0))),
        operands=(qkv, qkv, qkv, qkv, qkv, bias, dy_att, y_att, lse), plans=plans)


def _in_bwd(dq, dk, dv, duv, dgates, x, dx1, g1, w_in_t, plans=(), tm=512):
    S = x.shape[0]
    pad_blocks = (8 * CHUNK) // tm

    def body(dq_ref, dk_ref, dv_ref, duv_ref, dg_ref, x_ref, d1_ref, g_ref, wt_ref, dx_ref, dz_ref, dg1_ref):
        i = pl.program_id(0)
        dz_ref[:, 0:ATT_W] = (dq_ref[...].astype(F32) * 0.125).astype(BF16)
        dz_ref[:, ATT_W:2 * ATT_W] = dk_ref[...].astype(BF16)
        dz_ref[:, 2 * ATT_W:3 * ATT_W] = dv_ref[...].astype(BF16)
        dz_ref[:, 3 * ATT_W:3 * ATT_W + 2 * SG_W] = duv_ref[...]
        dz_ref[:, 3 * ATT_W + 2 * SG_W:IN_COLS] = dg_ref[...]
        dh = _dot(dz_ref[...], wt_ref[...])
        xv = x_ref[...]
        dx, dg1 = _rms_bwd(dh, xv, _rstd(xv), g_ref[...])
        dx_ref[...] = d1_ref[...] + dx

        @pl.when(i == 0)
        def _():
            dg1_ref[...] = jnp.zeros_like(dg1_ref)

        dg1_ref[...] += dg1

    shifted = pl.BlockSpec((tm, ATT_W), lambda i: (i + pad_blocks, 0))
    return _call(
        body, name="in_bwd", grid=(S // tm,),
        out_shape=(jax.ShapeDtypeStruct((S, D), F32), jax.ShapeDtypeStruct((S, IN_COLS), BF16),
                   jax.ShapeDtypeStruct((1, D), F32)),
        in_specs=[_rows(tm, ATT_W), shifted, shifted, _rows(tm, 2 * SG_W), _rows(tm, 2 * D), _rows(tm, D),
                  _rows(tm, D), _resident((1, D)), _resident((IN_COLS, D))],
        out_specs=(_rows(tm, D), _rows(tm, IN_COLS), pl.BlockSpec((1, D), lambda i: (0, 0))),
        operands=(dq, dk, dv, duv, dgates, x, dx1, g1, w_in_t), plans=plans)


def _adam_math(w, g, m, v):
    m = ADAM_B1 * m + (1.0 - ADAM_B1) * g
    v = ADAM_B2 * v + (1.0 - ADAM_B2) * (g * g)
    m_hat = m / (1.0 - ADAM_B1 ** ADAM_STEP)
    v_hat = v / (1.0 - ADAM_B2 ** ADAM_STEP)
    delta = -ADAM_LR * (m_hat / (jnp.sqrt(v_hat) + ADAM_EPS) + ADAM_WD * w)
    return delta, m, v


def _adam(parts, w, m, v, tr, name, transposed):
    P = parts.shape[0]
    R, C = w.shape

    def body(p_ref, w_ref, m_ref, v_ref, g_ref, d_ref, mo_ref, vo_ref):
        if transposed:
            eye = (lax.broadcasted_iota(jnp.int32, (C, C), 0) == lax.broadcasted_iota(jnp.int32, (C, C), 1)).astype(BF16)
            part = lambda k: _dot_tn(p_ref[k], eye)
        else:
            part = lambda k: p_ref[k].astype(F32)
        g = part(0)
        for k in range(1, P):
            g = g + part(k)
        g_ref[...] = g
        d_ref[...], mo_ref[...], vo_ref[...] = _adam_math(w_ref[...], g, m_ref[...], v_ref[...])

    row = pl.BlockSpec((tr, C), lambda i: (i, 0))
    p_spec = pl.BlockSpec((P, C, tr), lambda i: (0, 0, i)) if transposed else pl.BlockSpec((P, tr, C), lambda i: (0, i, 0))
    return pl.pallas_call(
        body, name=name, grid=(R // tr,),
        out_shape=tuple(jax.ShapeDtypeStruct((R, C), F32) for _ in range(4)),
        in_specs=[p_spec, row, row, row],
        out_specs=(row, row, row, row),
        compiler_params=_cparams(1),
    )(parts, w, m, v)


def _my_place():
    return lax.axis_index("x"), lax.axis_index("y"), lax.axis_index("c")


def _other_chips(x, y):
    return [(1 - x, y), (x, 1 - y), (1 - x, 1 - y)]


class _Plan:
    def __init__(self, arrays, out_shapes, sems, start, finish, forward=None, forward_at=0.7):
        self.arrays, self.out_shapes, self.sems = list(arrays), list(out_shapes), list(sems)
        self.start, self.finish, self.forward, self.forward_at = start, finish, forward, forward_at


def _call(body, *, name, grid, in_specs, out_specs, out_shape, operands, scratch_shapes=(), plans=()):
    n_in, n_out, n_scr = len(operands), len(out_shape), len(scratch_shapes)
    p_in = [a for p in plans for a in p.arrays]
    p_out = [s for p in plans for s in p.out_shapes]
    p_sem = [s for p in plans for s in p.sems]
    steps = math.prod(grid)

    def wrapped(*refs):
        ins, refs = refs[:n_in], refs[n_in:]
        pins, refs = refs[:len(p_in)], refs[len(p_in):]
        outs, refs = refs[:n_out], refs[n_out:]
        pouts, refs = refs[:len(p_out)], refs[len(p_out):]
        scr, psems = refs[:n_scr], refs[n_scr:]
        step = 0
        for axis, size in enumerate(grid):
            step = step * size + pl.program_id(axis)
        bound = []
        for p in plans:
            bound.append((p, pins[:len(p.arrays)], pouts[:len(p.out_shapes)], psems[:len(p.sems)]))
            pins, pouts, psems = pins[len(p.arrays):], pouts[len(p.out_shapes):], psems[len(p.sems):]

        @pl.when(step == 0)
        def _():
            for p, a, b, s in bound:
                p.start(a, b, s)

        body(*ins, *outs, *scr)

        for p, a, b, s in bound:
            if p.forward is not None:
                @pl.when(step == min(int(p.forward_at * steps), steps - 1))
                def _(p=p, a=a, b=b, s=s):
                    p.forward(a, b, s)

        @pl.when(step == steps - 1)
        def _():
            for p, a, b, s in bound:
                p.finish(a, b, s)

    hbm = pl.BlockSpec(memory_space=pl.ANY)
    return pl.pallas_call(
        wrapped if plans else body, name=name, grid=grid,
        out_shape=tuple(out_shape) + tuple(p_out),
        in_specs=list(in_specs) + [hbm] * len(p_in),
        out_specs=tuple(out_specs) + tuple([hbm] * len(p_out)),
        scratch_shapes=list(scratch_shapes) + p_sem,
        compiler_params=_cparams(len(grid)),
    )(*operands, *p_in)


def _run_plan(plan, name):
    n_in, n_out = len(plan.arrays), len(plan.out_shapes)

    def body(*refs):
        a, b, s = refs[:n_in], refs[n_in:n_in + n_out], refs[n_in + n_out:]
        plan.start(a, b, s)
        if plan.forward is not None:
            plan.forward(a, b, s)
        plan.finish(a, b, s)

    hbm = pl.BlockSpec(memory_space=pl.ANY)
    return pl.pallas_call(
        body, name=name, out_shape=tuple(plan.out_shapes),
        in_specs=[hbm] * n_in, out_specs=tuple([hbm] * n_out), scratch_shapes=plan.sems,
    )(*plan.arrays)


def _gather_plan(shards, forward_at=0.7):
    n = len(shards)

    def copies(ins, outs, sems):
        send_sems, recv_sems, local_sems = sems
        x, y, c = _my_place()
        me, sibling = (x, y, c), (x, y, 1 - c)
        chips = _other_chips(x, y)

        def idx(p):
            return 4 * p[0] + 2 * p[1] + p[2]

        def copy(a, k, block, to, src=None):
            return pltpu.make_async_remote_copy(
                src_ref=outs[a].at[idx(block)] if src is None else src, dst_ref=outs[a].at[idx(block)],
                send_sem=send_sems.at[a, k], recv_sem=recv_sems.at[a, k], device_id=to, device_id_type=MESH)

        mine = [pltpu.make_async_copy(ins[a], outs[a].at[idx(me)], local_sems.at[a]) for a in range(n)]
        first = []
        for a in range(n):
            first.append(copy(a, 0, me, sibling, src=ins[a]))
            first += [copy(a, 1 + j, me, (*chip, c), src=ins[a]) for j, chip in enumerate(chips)]
        arrived = [copy(a, 1 + j, (*chip, c), me) for j, chip in enumerate(chips) for a in range(n)]
        passed = [copy(a, 4 + j, (*chip, c), sibling) for j, chip in enumerate(chips) for a in range(n)]
        from_sibling = []
        for a in range(n):
            from_sibling.append(copy(a, 0, sibling, me))
            from_sibling += [copy(a, 4 + j, (*chip, 1 - c), me) for j, chip in enumerate(chips)]
        return mine, first, arrived, passed, from_sibling

    def start(ins, outs, sems):
        mine, first, _, _, _ = copies(ins, outs, sems)
        for cp in mine + first:
            cp.start()

    def forward(ins, outs, sems):
        _, _, arrived, passed, _ = copies(ins, outs, sems)
        for landed, onward in zip(arrived, passed):
            landed.wait_recv()
            onward.start()

    def finish(ins, outs, sems):
        mine, first, _, passed, from_sibling = copies(ins, outs, sems)
        for cp in from_sibling:
            cp.wait_recv()
        for cp in first + passed:
            cp.wait_send()
        for cp in mine:
            cp.wait()

    return _Plan(shards, [jax.ShapeDtypeStruct((N_DEV,) + s.shape, s.dtype) for s in shards],
                 [pltpu.SemaphoreType.DMA((n, 7)), pltpu.SemaphoreType.DMA((n, 7)), pltpu.SemaphoreType.DMA((n,))],
                 start, finish, forward, forward_at)


def _sibling_plan(scatter, whole=()):
    ns = len(scatter)
    n = ns + len(whole)

    def copies(ins, outs, sems):
        send_sems, recv_sems = sems
        x, y, c = _my_place()
        out = []
        for a in range(n):
            for k in range(4 if a < ns else 1):
                src = ins[a].at[2 * k + (1 - c)] if a < ns else ins[a]
                dst = outs[a].at[k] if a < ns else outs[a]
                out.append(pltpu.make_async_remote_copy(
                    src_ref=src, dst_ref=dst, send_sem=send_sems.at[a, k], recv_sem=recv_sems.at[a, k],
                    device_id=(x, y, 1 - c), device_id_type=MESH))
        return out

    def start(ins, outs, sems):
        for cp in copies(ins, outs, sems):
            cp.start()

    def finish(ins, outs, sems):
        for cp in copies(ins, outs, sems):
            cp.wait()

    shapes = [jax.ShapeDtypeStruct((4,) + p.shape[1:], p.dtype) for p in scatter]
    shapes += [jax.ShapeDtypeStruct(p.shape, p.dtype) for p in whole]
    return _Plan(list(scatter) + list(whole), shapes,
                 [pltpu.SemaphoreType.DMA((n, 4)), pltpu.SemaphoreType.DMA((n, 4))], start, finish)


def _pair_sum(mine, theirs, c_idx, name):
    _, R, C = mine.shape

    def body(c_ref, a_ref, b_ref, o_ref):
        o_ref[...] = (a_ref[...].astype(F32) + b_ref[...].astype(F32)).astype(o_ref.dtype)

    grid_spec = pltpu.PrefetchScalarGridSpec(
        num_scalar_prefetch=1, grid=(4,),
        in_specs=[pl.BlockSpec((1, R, C), lambda k, c_ref: (2 * k + c_ref[0], 0, 0)),
                  pl.BlockSpec((1, R, C), lambda k, c_ref: (k, 0, 0))],
        out_specs=pl.BlockSpec((1, R, C), lambda k, c_ref: (k, 0, 0)))
    return pl.pallas_call(
        body, name=name, grid_spec=grid_spec,
        out_shape=jax.ShapeDtypeStruct((4, R, C), mine.dtype),
        compiler_params=_cparams(1),
    )(c_idx, mine, theirs)


def _peers_plan(arrays):
    n = len(arrays)

    def copies(ins, outs, sems):
        send_sems, recv_sems, local_sems = sems
        x, y, c = _my_place()
        me = 4 * x + 2 * y + c
        out = [pltpu.make_async_copy(ins[a], outs[a].at[me], local_sems.at[a]) for a in range(n)]
        for a in range(n):
            for k in range(N_DEV - 1):
                bits = k + 1
                peer = (x ^ (bits >> 2), y ^ ((bits >> 1) & 1), c ^ (bits & 1))
                out.append(pltpu.make_async_remote_copy(
                    src_ref=ins[a], dst_ref=outs[a].at[me], send_sem=send_sems.at[a, k], recv_sem=recv_sems.at[a, k],
                    device_id=peer, device_id_type=MESH))
        return out

    def start(ins, outs, sems):
        for cp in copies(ins, outs, sems):
            cp.start()

    def finish(ins, outs, sems):
        for cp in copies(ins, outs, sems):
            cp.wait()

    return _Plan(list(arrays), [jax.ShapeDtypeStruct((N_DEV,) + a.shape, a.dtype) for a in arrays],
                 [pltpu.SemaphoreType.DMA((n, N_DEV - 1)), pltpu.SemaphoreType.DMA((n, N_DEV - 1)),
                  pltpu.SemaphoreType.DMA((n,))], start, finish)


def _chips_plan(scatter, whole=()):
    ns = len(scatter)
    n = ns + len(whole)

    def copies(ins, outs, sems):
        send_sems, recv_sems, local_sems = sems
        x, y, c = _my_place()
        my_chip = 2 * x + y

        def src(a, k):
            return ins[a].at[k] if a < ns else ins[a]

        local = [pltpu.make_async_copy(src(a, my_chip), outs[a].at[my_chip], local_sems.at[a]) for a in range(n)]
        remote = []
        for a in range(n):
            for j, (px, py) in enumerate(_other_chips(x, y)):
                remote.append(pltpu.make_async_remote_copy(
                    src_ref=src(a, 2 * px + py), dst_ref=outs[a].at[my_chip],
                    send_sem=send_sems.at[a, j], recv_sem=recv_sems.at[a, j],
                    device_id=(px, py, c), device_id_type=MESH))
        return local + remote

    def start(ins, outs, sems):
        for cp in copies(ins, outs, sems):
            cp.start()

    def finish(ins, outs, sems):
        for cp in copies(ins, outs, sems):
            cp.wait()

    shapes = [jax.ShapeDtypeStruct(s.shape, s.dtype) for s in scatter]
    shapes += [jax.ShapeDtypeStruct((4,) + s.shape, s.dtype) for s in whole]
    return _Plan(list(scatter) + list(whole), shapes,
                 [pltpu.SemaphoreType.DMA((n, 3)), pltpu.SemaphoreType.DMA((n, 3)), pltpu.SemaphoreType.DMA((n,))],
                 start, finish)


def _small_update(parts, w, m, v, loss_parts, name):
    n = len(parts)

    def total(ref):
        acc = ref[0]
        for k in range(1, ref.shape[0]):
            acc = acc + ref[k]
        return acc

    def body(*refs):
        p_refs, w_refs, m_refs, v_refs = (refs[i * n:(i + 1) * n] for i in range(4))
        lp_ref = refs[4 * n]
        outs = refs[4 * n + 1:]
        g_refs, d_refs, mo_refs, vo_refs = (outs[i * n:(i + 1) * n] for i in range(4))
        for a in range(n):
            g = total(p_refs[a])
            g_refs[a][...] = g
            d_refs[a][...], mo_refs[a][...], vo_refs[a][...] = _adam_math(w_refs[a][...], g, m_refs[a][...],
                                                                          v_refs[a][...])
        outs[4 * n][...] = total(lp_ref)

    vm = pl.BlockSpec(memory_space=pltpu.VMEM)
    shapes = [jax.ShapeDtypeStruct(t.shape, F32) for _ in range(4) for t in w]
    shapes.append(jax.ShapeDtypeStruct(loss_parts.shape[1:], F32))
    outs = pl.pallas_call(
        body, name=name, out_shape=tuple(shapes),
        in_specs=[vm] * (4 * n + 1), out_specs=tuple([vm] * (4 * n + 1)),
        compiler_params=pltpu.CompilerParams(vmem_limit_bytes=VMEM_LIMIT),
    )(*parts, *w, *m, *v, loss_parts)
    return outs[0:n], outs[n:2 * n], outs[2 * n:3 * n], outs[3 * n:4 * n], outs[4 * n]


BIG = [("w_in", 1), ("w_branch_att", 1), ("w_branch_sg", 1), ("w_out", 0), ("w_xq", 0), ("w_xkv", 1), ("w_xo", 0),
       ("w_ffn_in", 1), ("w_ffn_out", 0)]
SMALL = [("norm_mix_g", (1, D)), ("rel_bias", (8, NREL)), ("sg_ln_g", (8, 64)), ("sg_ln_b", (8, 64)),
         ("sg_w", (8, 128, 128)), ("sg_b", (8, 128)), ("norm_xattn_g", (1, D)), ("norm_mem_g", (1, D)),
         ("norm_ffn_g", (1, D)), ("norm_final_g", (1, D))]
ADAM_ROWS = {"w_in": 192, "w_branch_att": 512, "w_branch_sg": 512, "w_xkv": 1024, "w_ffn_in": 176}


def _full(gathered):
    return gathered.reshape(N_DEV * gathered.shape[1], gathered.shape[2])


def _blocks(grad):
    return grad.reshape(N_DEV, grad.shape[0] // N_DEV, grad.shape[1])


def kernel(x, mem, norm_mix_g, w_in, rel_bias, sg_ln_g, sg_ln_b, sg_w, sg_b, w_branch_att, w_branch_sg, w_out, norm_xattn_g, norm_mem_g, w_xq, w_xkv, w_xo, norm_ffn_g, w_ffn_in, w_ffn_out, norm_final_g, loss_target, m_norm_mix_g, m_w_in, m_rel_bias, m_sg_ln_g, m_sg_ln_b, m_sg_w, m_sg_b, m_w_branch_att, m_w_branch_sg, m_w_out, m_norm_xattn_g, m_norm_mem_g, m_w_xq, m_w_xkv, m_w_xo, m_norm_ffn_g, m_w_ffn_in, m_w_ffn_out, m_norm_final_g, v_norm_mix_g, v_w_in, v_rel_bias, v_sg_ln_g, v_sg_ln_b, v_sg_w, v_sg_b, v_w_branch_att, v_w_branch_sg, v_w_out, v_norm_xattn_g, v_norm_mem_g, v_w_xq, v_w_xkv, v_w_xo, v_norm_ffn_g, v_w_ffn_in, v_w_ffn_out, v_norm_final_g):
    args = dict(locals())
    big_names = [n for n, _ in BIG]
    small_names = [n for n, _ in SMALL]
    S = x.shape[1]

    x, mem, tgt = x.reshape(S, D), mem.reshape(MEM, D), loss_target.reshape(S, D)
    small = {n: args[n].reshape(shape) for n, shape in SMALL}
    g1, g2, g3 = small["norm_mix_g"], small["norm_xattn_g"], small["norm_ffn_g"]
    g_mem, g4 = small["norm_mem_g"], small["norm_final_g"]
    lng = small["sg_ln_g"].reshape(1, SG_W)
    lnb = small["sg_ln_b"].reshape(1, SG_W)
    b_exp = jnp.broadcast_to(small["sg_b"].T[:, :, None], (128, 8, 64)).reshape(128, SG_W)
    rel_pad = jnp.pad(small["rel_bias"], ((0, 0), (0, 384 - NREL)))
    c_idx = lax.axis_index("c").astype(jnp.int32).reshape(1)

    shard = {n: (args[n][0].T if axis == 1 else args[n][0]).astype(BF16) for n, axis in BIG}
    w_in_t = _full(_run_plan(_gather_plan([shard["w_in"]]), "ag_w_in")[0])
    bias = _bias_table(rel_pad)
    mix_names = ["w_branch_att", "w_branch_sg", "w_out", "w_xq", "w_xkv", "w_xo"]
    qkv, uv, gates, h, *got = _in_proj(x, g1, w_in_t, plans=[_gather_plan([shard[n] for n in mix_names])])
    wba_t, wbs_t, w_out_f, w_xq_f, w_xkv_t, w_xo_f = (_full(g) for g in got)
    y_att, lse, *got = _attn_fwd(qkv, bias, plans=[_gather_plan([shard["w_ffn_in"], shard["w_ffn_out"]])])
    w_ffn_in_t, w_ffn_out_f = (_full(g) for g in got)
    y_sg = _sgu_fwd(uv, lng, lnb, small["sg_w"], b_exp)
    x1 = _merge_fwd(x, y_att, y_sg, gates, wba_t, wbs_t, w_out_f)
    kv, mn = _mem_kv(mem, g_mem, w_xkv_t)
    x2, hx, qx, o_x = _xattn_fwd(x1, g2, w_xq_f, kv, w_xo_f)
    dx3, gu, hf, act, loss_part, dg4 = _ffn_fwd(x2, tgt, g3, w_ffn_in_t, w_ffn_out_f, g4)

    def pair_sums(names, mine, theirs):
        return [_pair_sum(a, b, c_idx, "rs_pair_" + n) for n, a, b in zip(names, mine, theirs)]

    dx2, dgu, dg3 = _ffn_bwd(dx3, gu, x2, g3, w_ffn_out_f, w_ffn_in_t)
    ffn_names = ["w_ffn_out", "w_ffn_in"]
    ffn_mine = [_blocks(_dw(act, dx3, 1408, 1024, DW_TOKENS, "dw_ffn_out")),
                _blocks(_dw(dgu, hf, 1408, 1024, DW_TOKENS, "dw_ffn_in"))]
    dx1, dq_x, dkv, dg2, *ffn_theirs = _xattn_bwd(dx2, x1, qx, g2, w_xq_f, w_xo_f, kv,
                                                  plans=[_sibling_plan(ffn_mine)])
    ffn_chip = pair_sums(ffn_names, ffn_mine, ffn_theirs)
    d_xkv, dg_mem = _mem_kv_bwd(dkv, mem, g_mem, mn, w_xkv_t)
    merged, d_a, d_b, dy_att, dy_sg, dgates = _merge_bwd(dx1, y_att, y_sg, gates, wba_t, wbs_t, w_out_f)
    mid_names = ["w_xo", "w_xq", "w_xkv", "w_out", "w_branch_att", "w_branch_sg"]
    mid_mine = [_blocks(g) for g in (
        _dw(o_x, dx2, 1024, 1024, DW_TOKENS, "dw_xo"), _dw(hx, dq_x, 1024, 1024, DW_TOKENS, "dw_xq"), d_xkv,
        _dw(merged, dx1, 1024, 1024, DW_TOKENS, "dw_out"), _dw(d_a, y_att, 1024, 512, DW_TOKENS, "dw_branch_att"),
        _dw(d_b, y_sg, 1024, 512, DW_TOKENS, "dw_branch_sg"))]
    duv, d_sgw, d_bx, d_lng, d_lnb, *got = _sgu_bwd(uv, dy_sg, lng, lnb, small["sg_w"], b_exp,
                                                    plans=[_chips_plan(ffn_chip), _sibling_plan(mid_mine)])
    ffn_all, mid_theirs = got[:2], got[2:]
    mid_chip = pair_sums(mid_names, mid_mine, mid_theirs)
    dq, dk, dv, ds_sum, *mid_all = _attn_bwd(qkv, dy_att, y_att, lse, bias, plans=[_chips_plan(mid_chip)])
    d_rel = _bias_grad(ds_sum)

    grad_x, dz, dg1 = _in_bwd(dq, dk, dv, duv, dgates, x, dx1, g1, w_in_t)

    gs = {"norm_mix_g": dg1, "rel_bias": d_rel[:, :NREL], "sg_ln_g": d_lng.reshape(8, 64),
          "sg_ln_b": d_lnb.reshape(8, 64), "sg_w": d_sgw, "sg_b": d_bx.reshape(128, 8, 64)[:, :, 0].T,
          "norm_xattn_g": dg2, "norm_mem_g": dg_mem, "norm_ffn_g": dg3, "norm_final_g": dg4}
    d_in, *everyone = _dw(dz, h, 1152, 1024, DW_TOKENS, "dw_in",
                          plans=[_peers_plan([gs[n] for n in small_names] + [loss_part])])

    in_mine = [_blocks(d_in)]
    (in_theirs,) = _run_plan(_sibling_plan(in_mine), "rs_sibling")
    in_chip = pair_sums(["w_in"], in_mine, [in_theirs])
    (in_all,) = _run_plan(_chips_plan(in_chip), "rs_chips")
    all_parts = dict(zip(ffn_names + mid_names + ["w_in"], list(ffn_all) + list(mid_all) + [in_all]))

    res = {}
    for n, axis in BIG:
        parts = all_parts[n]
        wmv = [args[p + n][0] for p in ("", "m_", "v_")]
        if axis == 1 and wmv[0].shape[1] % LANES != 0:
            outs = _adam(parts, *(t.T for t in wmv), ADAM_ROWS[n], "adam_" + n, transposed=False)
            res[n] = [t.T[None] for t in outs]
        else:
            tr = ADAM_ROWS[n] if axis == 1 else wmv[0].shape[0]
            res[n] = [t[None] for t in _adam(parts, *wmv, tr, "adam_" + n, transposed=(axis == 1))]
    small_res = _small_update(
        everyone[:-1], [small[n] for n in small_names],
        [args["m_" + n].reshape(s) for n, s in SMALL], [args["v_" + n].reshape(s) for n, s in SMALL],
        everyone[-1], "adam_small")
    for i, n in enumerate(small_names):
        res[n] = [small_res[k][i].reshape(args[n].shape) for k in range(4)]
    loss = small_res[4][0, 0]

    order = ["norm_mix_g", "w_in", "rel_bias", "sg_ln_g", "sg_ln_b", "sg_w", "sg_b", "w_branch_att", "w_branch_sg",
             "w_out", "norm_xattn_g", "norm_mem_g", "w_xq", "w_xkv", "w_xo", "norm_ffn_g", "w_ffn_in", "w_ffn_out",
             "norm_final_g"]
    outs = [loss, grad_x.reshape(1, S, D)]
    for k in range(4):
        outs += [res[n][k] for n in order]
    return tuple(outs)
```

```python
import math

import jax
import jax.numpy as jnp
from jax import lax
from jax.experimental import pallas as pl
from jax.experimental.pallas import tpu as pltpu

F32 = jnp.float32
BF16 = jnp.bfloat16

D = 1024
ATT_W = 512
SG_W = 512
IN_COLS = 4608
DFF = 2816
MEM = 256
XH = 4
CHUNK = 64
BAND_KEYS = 640
ATT_R = 512
ATT_SUBS = ATT_R // 128
DW_TOKENS = 2048
REL_CLIP = 128
NREL = 2 * REL_CLIP + 1
EPS = 1e-6
NEG = -1e30
N_DEV = 8

ADAM_LR = 0.001
ADAM_B1 = 0.9
ADAM_B2 = 0.999
ADAM_EPS = 1e-08
ADAM_WD = 0.01
ADAM_STEP = 10

LANES = 128
VMEM_LIMIT = 56 * 1024 * 1024
MESH = pl.DeviceIdType.MESH


def _cparams(n_axes):
    return pltpu.CompilerParams(dimension_semantics=("arbitrary",) * n_axes, vmem_limit_bytes=VMEM_LIMIT)


def _resident(shape):
    zeros = (0,) * len(shape)
    return pl.BlockSpec(shape, lambda *_: zeros, pipeline_mode=pl.Buffered(1))


def _rows(tm, cols, col_block=0):
    return pl.BlockSpec((tm, cols), lambda i: (i, col_block))


def _sigmoid(x):
    return 1.0 / (1.0 + jnp.exp(-x))


_GELU_C = math.sqrt(2.0 / math.pi)


def _gelu(x):
    t = jnp.tanh(_GELU_C * (x + 0.044715 * (x * x * x)))
    return x * (0.5 * (1.0 + t))


def _gelu_and_grad(x):
    x2 = x * x
    t = jnp.tanh(_GELU_C * (x + 0.044715 * (x2 * x)))
    cdf = 0.5 * (1.0 + t)
    dcdf = 0.5 * (1.0 - t * t) * (_GELU_C * (1.0 + 3.0 * 0.044715 * x2))
    return x * cdf, cdf + x * dcdf


def _rstd(x):
    return lax.rsqrt(jnp.mean(x * x, axis=-1, keepdims=True) + EPS)


def _rms_bwd(dh, x, r, g):
    xh = x * r
    dxh = dh * g
    dx = r * (dxh - xh * jnp.mean(dxh * xh, axis=-1, keepdims=True))
    dg = jnp.sum(dh * xh, axis=0, keepdims=True)
    return dx, dg


def _group_sum64(x):
    r = lax.broadcasted_iota(jnp.int32, (LANES, LANES), 0) // 64
    c = lax.broadcasted_iota(jnp.int32, (LANES, LANES), 1) // 64
    same_group = (r == c).astype(BF16)

    def one(v):
        hi = v.astype(BF16)
        rest = v - hi.astype(F32)
        mid = rest.astype(BF16)
        lo = (rest - mid.astype(F32)).astype(BF16)
        return _dot(hi, same_group) + _dot(mid, same_group) + _dot(lo, same_group)

    pieces = [one(x[:, LANES * j:LANES * (j + 1)]) for j in range(x.shape[1] // LANES)]
    return pieces[0] if len(pieces) == 1 else jnp.concatenate(pieces, axis=1)


def _dot(a, b):
    return jnp.dot(a, b, preferred_element_type=F32)


def _dot_nt(a, b):
    return lax.dot_general(a, b, (((1,), (1,)), ((), ())), preferred_element_type=F32)


def _dot_tn(a, b):
    return lax.dot_general(a, b, (((0,), (0,)), ((), ())), preferred_element_type=F32)


DIAGS = 768


def _diag_onehot():
    r_idx = lax.broadcasted_iota(jnp.int32, (384, DIAGS), 0)
    t_idx = lax.broadcasted_iota(jnp.int32, (384, DIAGS), 1)
    dist = (8 * CHUNK + 127) - t_idx
    return (jnp.clip(dist, -REL_CLIP, REL_CLIP) + REL_CLIP == r_idx).astype(F32)


def _shift_rows(x, reverse):
    row = lax.broadcasted_iota(jnp.int32, x.shape, 0)
    for k in range(7):
        amt = (DIAGS - (1 << k)) if reverse else (1 << k)
        x = jnp.where(((row >> k) & 1) == 1, pltpu.roll(x, amt, 1), x)
    return x


N_TABLES = 1 + ATT_SUBS


def _bias_table(rel_bias_pad):
    def body(rb_ref, out_ref):
        per_diag = jnp.dot(rb_ref[...], _diag_onehot(), preferred_element_type=F32,
                           precision=lax.Precision.HIGHEST)
        a = lax.broadcasted_iota(jnp.int32, (128, BAND_KEYS), 0)
        b = lax.broadcasted_iota(jnp.int32, (128, BAND_KEYS), 1)
        band = (b // CHUNK >= a // CHUNK) & (b // CHUNK <= a // CHUNK + 8)
        for h in range(8):
            rows = jnp.broadcast_to(per_diag[h:h + 1, :], (128, DIAGS))
            table = _shift_rows(pltpu.roll(rows, DIAGS - 127, 1), reverse=False)[:, :BAND_KEYS]
            out_ref[0, h] = jnp.where(band, table, NEG)
            for s in range(ATT_SUBS):
                out_ref[1 + s, h] = jnp.where(band & (b >= 8 * CHUNK - 128 * s), table, NEG)

    return pl.pallas_call(
        body, name="bias_table",
        out_shape=jax.ShapeDtypeStruct((N_TABLES, 8, 128, BAND_KEYS), F32),
        in_specs=[pl.BlockSpec(memory_space=pltpu.VMEM)],
        out_specs=pl.BlockSpec(memory_space=pltpu.VMEM),
        compiler_params=pltpu.CompilerParams(vmem_limit_bytes=VMEM_LIMIT),
    )(rel_bias_pad)


def _bias_grad(ds_sum):
    def body(ds_ref, out_ref):
        sums = []
        for h in range(8):
            padded = jnp.concatenate([ds_ref[h], jnp.zeros((128, DIAGS - BAND_KEYS), F32)], axis=1)
            skewed = pltpu.roll(_shift_rows(padded, reverse=True), 127, 1)
            sums.append(jnp.sum(skewed, axis=0, keepdims=True))
        per_diag = jnp.concatenate(sums, axis=0)
        out_ref[...] = lax.dot_general(per_diag, _diag_onehot(), (((1,), (1,)), ((), ())),
                                       preferred_element_type=F32, precision=lax.Precision.HIGHEST)

    return pl.pallas_call(
        body, name="bias_grad",
        out_shape=jax.ShapeDtypeStruct((8, 384), F32),
        in_specs=[pl.BlockSpec(memory_space=pltpu.VMEM)],
        out_specs=pl.BlockSpec(memory_space=pltpu.VMEM),
    )(ds_sum)


def _norm_in(x, g1, plans=(), tm=512):
    S = x.shape[0]

    def body(x_ref, g_ref, h_ref):
        xv = x_ref[...]
        h_ref[...] = (xv * _rstd(xv) * g_ref[...]).astype(BF16)

    return _call(
        body, name="norm_in", grid=(S // tm,),
        out_shape=(jax.ShapeDtypeStruct((S, D), BF16),),
        in_specs=[_rows(tm, D), _resident((1, D))], out_specs=(_rows(tm, D),),
        operands=(x, g1), plans=plans)


def _in_proj(h, w_in_t, plans=(), tm=512):
    S = h.shape[0]

    def body(h_ref, w_ref, qkv_ref, uv_ref, gate_ref):
        h = h_ref[...]
        for c in range(IN_COLS // 512):
            zc = _dot_nt(h, w_ref[512 * c:512 * (c + 1), :])
            if c == 0:
                qkv_ref[:, 0:512] = (zc * 0.125).astype(BF16)
            elif c < 3:
                qkv_ref[:, 512 * c:512 * (c + 1)] = zc.astype(BF16)
            elif c < 5:
                uv_ref[:, 512 * (c - 3):512 * (c - 2)] = zc.astype(BF16)
            else:
                gate_ref[:, 512 * (c - 5):512 * (c - 4)] = zc.astype(BF16)

    return _call(
        body, name="in_proj", grid=(S // tm,),
        out_shape=(jax.ShapeDtypeStruct((S, 3 * ATT_W), BF16), jax.ShapeDtypeStruct((S, 2 * SG_W), BF16),
                   jax.ShapeDtypeStruct((S, 2 * D), BF16)),
        in_specs=[_rows(tm, D), _resident((IN_COLS, D))],
        out_specs=(_rows(tm, 3 * ATT_W), _rows(tm, 2 * SG_W), _rows(tm, 2 * D)),
        operands=(h, w_in_t), plans=plans)


def _two_heads(a, lo):
    return jnp.concatenate([jnp.where(lo, a, 0), jnp.where(lo, 0, a)], axis=0)


def _att_specs():
    R = ATT_R
    q = pl.BlockSpec((R, LANES), lambda j, i: (i, j))
    kp = pl.BlockSpec((R, LANES), lambda j, i: (jnp.maximum(i - 1, 0), 4 + j))
    kc = pl.BlockSpec((R, LANES), lambda j, i: (i, 4 + j))
    vp = pl.BlockSpec((R, LANES), lambda j, i: (jnp.maximum(i - 1, 0), 8 + j))
    vc = pl.BlockSpec((R, LANES), lambda j, i: (i, 8 + j))
    bias = pl.BlockSpec((N_TABLES, 2, 128, BAND_KEYS), lambda j, i: (0, j, 0, 0))
    return [q, kp, kc, vp, vc, bias]


def _attn_fwd(qkv, bias, plans=()):
    S = qkv.shape[0]
    R = ATT_R

    def body(q_ref, kp_ref, kc_ref, vp_ref, vc_ref, b_ref, o_ref, lse_ref):
        i = pl.program_id(1)
        lo = lax.broadcasted_iota(jnp.int32, (1, LANES), 1) < 64
        kwin = jnp.concatenate([kp_ref[...], kc_ref[...]], axis=0)
        vwin = jnp.concatenate([vp_ref[...], vc_ref[...]], axis=0)
        for sub in range(ATT_SUBS):
            q2 = q_ref[128 * sub:128 * (sub + 1), :]
            kw = kwin[128 * sub:128 * sub + BAND_KEYS]
            vw = vwin[128 * sub:128 * sub + BAND_KEYS]
            table = jnp.where(i == 0, 1 + sub, 0)
            s = _dot_nt(_two_heads(q2, lo), kw) + b_ref[table].reshape(256, BAND_KEYS)
            top = jnp.max(s, axis=-1, keepdims=True)
            p = jnp.exp(s - top)
            total = jnp.sum(p, axis=-1, keepdims=True)
            p = (p / total).astype(BF16)
            o = _dot(jnp.concatenate([p[:128], p[128:]], axis=1), _two_heads(vw, lo))
            lse = top + jnp.log(total)
            o_ref[128 * sub:128 * (sub + 1), :] = o.astype(BF16)
            lse_ref[128 * sub:128 * (sub + 1), :] = jnp.where(lo, lse[:128], lse[128:])

    blk = pl.BlockSpec((R, LANES), lambda j, i: (i, j))
    return _call(
        body, name="attn_fwd", grid=(4, S // R),
        out_shape=(jax.ShapeDtypeStruct((S, ATT_W), BF16), jax.ShapeDtypeStruct((S, ATT_W), F32)),
        in_specs=_att_specs(), out_specs=(blk, blk),
        operands=(qkv, qkv, qkv, qkv, qkv, bias), plans=plans)


def _sg_mask():
    t = lax.broadcasted_iota(jnp.int32, (128, 128), 0)
    s = lax.broadcasted_iota(jnp.int32, (128, 128), 1)
    return (s // CHUNK) <= (t // CHUNK)


def _sg_layernorm(gv, lng, lnb):
    mu = _group_sum64(gv) * (1.0 / 64)
    xc = gv - mu
    var = _group_sum64(xc * xc) * (1.0 / 64)
    rstd = lax.rsqrt(var + EPS)
    vhat = xc * rstd
    return vhat * lng + lnb, vhat, rstd


def _sgu_fwd(uv, lng, lnb, sg_w, b_exp, tm=512):
    S = uv.shape[0]

    def body(uv_ref, lng_ref, lnb_ref, w_ref, b_ref, y_ref):
        lane = lax.broadcasted_iota(jnp.int32, (1, LANES), 1)
        lo = lane < 64
        mask = _sg_mask()
        gu = _gelu(uv_ref[:, 0:SG_W].astype(F32))
        vln, _, _ = _sg_layernorm(_gelu(uv_ref[:, SG_W:2 * SG_W].astype(F32)), lng_ref[...], lnb_ref[...])
        for gp in range(4):
            w0 = jnp.where(mask, w_ref[2 * gp], 0).astype(BF16)
            w1 = jnp.where(mask, w_ref[2 * gp + 1], 0).astype(BF16)
            cols = slice(128 * gp, 128 * (gp + 1))
            for n in range(tm // 128):
                rows = slice(128 * n, 128 * (n + 1))
                vl = vln[rows, cols]
                sv = (_dot(w0, jnp.where(lo, vl, 0).astype(BF16)) + _dot(w1, jnp.where(lo, 0, vl).astype(BF16))
                      + b_ref[:, cols])
                y_ref[rows, cols] = (gu[rows, cols] * sv).astype(BF16)

    return pl.pallas_call(
        body, name="sgu_fwd", grid=(S // tm,),
        out_shape=jax.ShapeDtypeStruct((S, SG_W), BF16),
        in_specs=[_rows(tm, 2 * SG_W), _resident((1, SG_W)), _resident((1, SG_W)),
                  _resident((8, 128, 128)), _resident((128, SG_W))],
        out_specs=_rows(tm, SG_W),
        compiler_params=_cparams(1),
    )(uv, lng, lnb, sg_w, b_exp)


def _merge_fwd(x, y_att, y_sg, gates, wba_t, wbs_t, w_out, tm=512):
    S = x.shape[0]

    def body(x_ref, ya_ref, ys_ref, g_ref, wba_ref, wbs_ref, wo_ref, x1_ref):
        a = _dot_nt(ya_ref[...], wba_ref[...])
        b = _dot_nt(ys_ref[...], wbs_ref[...])
        merged = _sigmoid(g_ref[:, 0:D].astype(F32)) * a + _sigmoid(g_ref[:, D:2 * D].astype(F32)) * b
        x1_ref[...] = x_ref[...] + _dot(merged.astype(BF16), wo_ref[...])

    return pl.pallas_call(
        body, name="merge_fwd", grid=(S // tm,),
        out_shape=jax.ShapeDtypeStruct((S, D), F32),
        in_specs=[_rows(tm, D), _rows(tm, ATT_W), _rows(tm, SG_W), _rows(tm, 2 * D),
                  _resident((D, ATT_W)), _resident((D, SG_W)), _resident((D, D))],
        out_specs=_rows(tm, D),
        compiler_params=_cparams(1),
    )(x, y_att, y_sg, gates, wba_t, wbs_t, w_out)


def _mem_kv(mem, g_mem, w_xkv_t):
    def body(m_ref, g_ref, w_ref, kv_ref, mn_ref):
        mv = m_ref[...]
        mn = (mv * _rstd(mv) * g_ref[...]).astype(BF16)
        mn_ref[...] = mn
        kv_ref[...] = _dot_nt(mn, w_ref[...]).astype(BF16)

    vm = pl.BlockSpec(memory_space=pltpu.VMEM)
    return pl.pallas_call(
        body, name="mem_kv",
        out_shape=(jax.ShapeDtypeStruct((MEM, 2 * D), BF16), jax.ShapeDtypeStruct((MEM, D), BF16)),
        in_specs=[vm, vm, vm], out_specs=(vm, vm),
        compiler_params=pltpu.CompilerParams(vmem_limit_bytes=VMEM_LIMIT),
    )(mem, g_mem, w_xkv_t)


def _xatt_head(qx, kv_ref, h):
    hs = slice(256 * h, 256 * (h + 1))
    s = _dot_nt(qx[:, hs], kv_ref[:, hs])
    p = jnp.exp(s - jnp.max(s, axis=-1, keepdims=True))
    return p / jnp.sum(p, axis=-1, keepdims=True)


def _xattn_fwd(x1, g2, w_xq, kv, w_xo, tm=512):
    S = x1.shape[0]

    def body(x_ref, g_ref, wq_ref, kv_ref, wo_ref, x2_ref, hx_ref, qx_ref, o_ref):
        xv = x_ref[...]
        hx = (xv * _rstd(xv) * g_ref[...]).astype(BF16)
        hx_ref[...] = hx
        qx = (_dot(hx, wq_ref[...]) * (1.0 / 16)).astype(BF16)
        qx_ref[...] = qx
        for h in range(XH):
            p = _xatt_head(qx, kv_ref, h)
            o_ref[:, 256 * h:256 * (h + 1)] = _dot(p.astype(BF16), kv_ref[:, D + 256 * h:D + 256 * (h + 1)]).astype(BF16)
        x2_ref[...] = xv + _dot(o_ref[...], wo_ref[...])

    return pl.pallas_call(
        body, name="xattn_fwd", grid=(S // tm,),
        out_shape=(jax.ShapeDtypeStruct((S, D), F32),) + (jax.ShapeDtypeStruct((S, D), BF16),) * 3,
        in_specs=[_rows(tm, D), _resident((1, D)), _resident((D, D)), _resident((MEM, 2 * D)), _resident((D, D))],
        out_specs=(_rows(tm, D),) * 4,
        compiler_params=_cparams(1),
    )(x1, g2, w_xq, kv, w_xo)


FF_CHUNK = 1408


def _ffn_fwd(x2, tgt, g3, w_ffn_in_t, w_ffn_out, g4, tm=256):
    S = x2.shape[0]

    def body(x_ref, t_ref, g3_ref, wi_ref, wo_ref, g4_ref, dx3_ref, gu_ref, hf_ref, act_ref, loss_ref, dg4_ref):
        i = pl.program_id(0)
        xv = x_ref[...]
        hf = (xv * _rstd(xv) * g3_ref[...]).astype(BF16)
        hf_ref[...] = hf
        acc = xv
        for c in range(DFF // FF_CHUNK):
            cs = slice(FF_CHUNK * c, FF_CHUNK * (c + 1))
            us = slice(DFF + FF_CHUNK * c, DFF + FF_CHUNK * (c + 1))
            gate = _dot_nt(hf, wi_ref[cs, :])
            up = _dot_nt(hf, wi_ref[us, :])
            gu_ref[:, cs] = gate.astype(BF16)
            gu_ref[:, us] = up.astype(BF16)
            act = ((gate * _sigmoid(gate)) * up).astype(BF16)
            act_ref[:, cs] = act
            acc = acc + _dot(act, wo_ref[cs, :])
        r4 = _rstd(acc)
        g4 = g4_ref[...]
        diff = acc * r4 * g4 - t_ref[...]
        dy = diff * (1.0 / D)
        dx3, dg4 = _rms_bwd(dy, acc, r4, g4)
        dx3_ref[...] = dx3
        part = 0.5 * jnp.sum(jnp.mean(diff * diff, axis=-1, keepdims=True))

        @pl.when(i == 0)
        def _():
            loss_ref[...] = jnp.zeros_like(loss_ref)
            dg4_ref[...] = jnp.zeros_like(dg4_ref)

        loss_ref[...] += jnp.full(loss_ref.shape, part, F32)
        dg4_ref[...] += dg4

    return pl.pallas_call(
        body, name="ffn_fwd", grid=(S // tm,),
        out_shape=(jax.ShapeDtypeStruct((S, D), F32), jax.ShapeDtypeStruct((S, 2 * DFF), BF16),
                   jax.ShapeDtypeStruct((S, D), BF16), jax.ShapeDtypeStruct((S, DFF), BF16),
                   jax.ShapeDtypeStruct((8, LANES), F32), jax.ShapeDtypeStruct((1, D), F32)),
        in_specs=[_rows(tm, D), _rows(tm, D), _resident((1, D)), _resident((2 * DFF, D)), _resident((DFF, D)),
                  _resident((1, D))],
        out_specs=(_rows(tm, D), _rows(tm, 2 * DFF), _rows(tm, D), _rows(tm, DFF),
                   pl.BlockSpec((8, LANES), lambda i: (0, 0)), pl.BlockSpec((1, D), lambda i: (0, 0))),
        compiler_params=_cparams(1),
    )(x2, tgt, g3, w_ffn_in_t, w_ffn_out, g4)


def _ffn_bwd(dx3, gu, x2, g3, w_ffn_out, w_ffn_in_t, tm=256):
    S = x2.shape[0]

    def body(d_ref, gu_ref, x_ref, g3_ref, wo_ref, wit_ref, dx2_ref, dgu_ref, dg3_ref):
        i = pl.program_id(0)
        d3 = d_ref[...]
        d3b = d3.astype(BF16)
        for c in range(DFF // FF_CHUNK):
            cs = slice(FF_CHUNK * c, FF_CHUNK * (c + 1))
            us = slice(DFF + FF_CHUNK * c, DFF + FF_CHUNK * (c + 1))
            da = _dot_nt(d3b, wo_ref[cs, :])
            gate = gu_ref[:, cs].astype(F32)
            up = gu_ref[:, us].astype(F32)
            sg = _sigmoid(gate)
            dgate = (da * up * (sg * (1.0 + gate * (1.0 - sg)))).astype(BF16)
            dup = (da * (gate * sg)).astype(BF16)
            dgu_ref[:, cs] = dgate
            dgu_ref[:, us] = dup
        dhf = _dot(dgu_ref[...], wit_ref[...])
        xv = x_ref[...]
        dx, dg3 = _rms_bwd(dhf, xv, _rstd(xv), g3_ref[...])
        dx2_ref[...] = d3 + dx

        @pl.when(i == 0)
        def _():
            dg3_ref[...] = jnp.zeros_like(dg3_ref)

        dg3_ref[...] += dg3

    return pl.pallas_call(
        body, name="ffn_bwd", grid=(S // tm,),
        out_shape=(jax.ShapeDtypeStruct((S, D), F32), jax.ShapeDtypeStruct((S, 2 * DFF), BF16),
                   jax.ShapeDtypeStruct((1, D), F32)),
        in_specs=[_rows(tm, D), _rows(tm, 2 * DFF), _rows(tm, D), _resident((1, D)),
                  _resident((DFF, D)), _resident((2 * DFF, D))],
        out_specs=(_rows(tm, D), _rows(tm, 2 * DFF), pl.BlockSpec((1, D), lambda i: (0, 0))),
        compiler_params=_cparams(1),
    )(dx3, gu, x2, g3, w_ffn_out, w_ffn_in_t)


def _dw(a, b, tmm, tn, ts, name, out_dtype=BF16, plans=()):
    S, M = a.shape
    N = b.shape[1]
    ts = min(ts, S)
    nk = S // ts

    def body(a_ref, b_ref, o_ref, acc_ref):
        k = pl.program_id(2)

        @pl.when(k == 0)
        def _():
            acc_ref[...] = jnp.zeros_like(acc_ref)

        acc_ref[...] += _dot_tn(a_ref[...].astype(BF16), b_ref[...].astype(BF16))

        @pl.when(k == nk - 1)
        def _():
            o_ref[...] = acc_ref[...].astype(out_dtype)

    out = _call(
        body, name=name, grid=(M // tmm, N // tn, nk),
        out_shape=(jax.ShapeDtypeStruct((M, N), out_dtype),),
        in_specs=[pl.BlockSpec((ts, tmm), lambda m, n, k: (k, m)), pl.BlockSpec((ts, tn), lambda m, n, k: (k, n))],
        out_specs=(pl.BlockSpec((tmm, tn), lambda m, n, k: (m, n)),),
        scratch_shapes=[pltpu.VMEM((tmm, tn), F32)],
        operands=(a, b), plans=plans)
    return out if plans else out[0]


def _xattn_bwd(dx2, x1, qx, g2, w_xq, w_xo, kv, plans=(), tm=512):
    S = x1.shape[0]

    def body(d_ref, x_ref, qx_ref, g_ref, wq_ref, wo_ref, kv_ref, dx1_ref, dq_ref, dkv_ref, dg2_ref):
        i = pl.program_id(0)

        @pl.when(i == 0)
        def _():
            dkv_ref[...] = jnp.zeros_like(dkv_ref)
            dg2_ref[...] = jnp.zeros_like(dg2_ref)

        d2 = d_ref[...]
        qx = qx_ref[...]
        do = _dot_nt(d2.astype(BF16), wo_ref[...]).astype(BF16)
        for h in range(XH):
            hs = slice(256 * h, 256 * (h + 1))
            vs = slice(D + 256 * h, D + 256 * (h + 1))
            p = _xatt_head(qx, kv_ref, h)
            dp = _dot_nt(do[:, hs], kv_ref[:, vs])
            ds = (p * (dp - jnp.sum(dp * p, axis=-1, keepdims=True))).astype(BF16)
            dq_ref[:, hs] = (_dot(ds, kv_ref[:, hs]) * (1.0 / 16)).astype(BF16)
            dkv_ref[:, hs] += _dot_tn(ds, qx[:, hs])
            dkv_ref[:, vs] += _dot_tn(p.astype(BF16), do[:, hs])
        dhx = _dot_nt(dq_ref[...], wq_ref[...])
        xv = x_ref[...]
        dx, dg2 = _rms_bwd(dhx, xv, _rstd(xv), g_ref[...])
        dx1_ref[...] = d2 + dx
        dg2_ref[...] += dg2

    return _call(
        body, name="xattn_bwd", grid=(S // tm,),
        out_shape=(jax.ShapeDtypeStruct((S, D), F32), jax.ShapeDtypeStruct((S, D), BF16),
                   jax.ShapeDtypeStruct((MEM, 2 * D), F32), jax.ShapeDtypeStruct((1, D), F32)),
        in_specs=[_rows(tm, D), _rows(tm, D), _rows(tm, D), _resident((1, D)), _resident((D, D)), _resident((D, D)),
                  _resident((MEM, 2 * D))],
        out_specs=(_rows(tm, D), _rows(tm, D),
                   pl.BlockSpec((MEM, 2 * D), lambda i: (0, 0)), pl.BlockSpec((1, D), lambda i: (0, 0))),
        operands=(dx2, x1, qx, g2, w_xq, w_xo, kv), plans=plans)


def _mem_kv_bwd(dkv, mem, g_mem, mn, w_xkv_t):
    def body(dkv_ref, m_ref, g_ref, mn_ref, wt_ref, dw_ref, dg_ref):
        dkvb = dkv_ref[...].astype(BF16)
        dw_ref[...] = _dot_tn(dkvb, mn_ref[...]).astype(BF16)
        dmn = _dot(dkvb, wt_ref[...])
        mv = m_ref[...]
        dg_ref[...] = jnp.sum(dmn * (mv * _rstd(mv)), axis=0, keepdims=True)

    vm = pl.BlockSpec(memory_space=pltpu.VMEM)
    return pl.pallas_call(
        body, name="mem_kv_bwd",
        out_shape=(jax.ShapeDtypeStruct((2 * D, D), BF16), jax.ShapeDtypeStruct((1, D), F32)),
        in_specs=[vm] * 5, out_specs=(vm, vm),
        compiler_params=pltpu.CompilerParams(vmem_limit_bytes=VMEM_LIMIT),
    )(dkv, mem, g_mem, mn, w_xkv_t)


def _merge_bwd(dx1, y_att, y_sg, gates, wba_t, wbs_t, w_out, tm=512):
    S = dx1.shape[0]

    def body(d_ref, ya_ref, ys_ref, g_ref, wbat_ref, wbst_ref, wo_ref,
             mg_ref, da_ref, db_ref, dya_ref, dys_ref, dg_ref):
        dm = _dot_nt(d_ref[...].astype(BF16), wo_ref[...])
        a = _dot_nt(ya_ref[...], wbat_ref[...])
        b = _dot_nt(ys_ref[...], wbst_ref[...])
        sa = _sigmoid(g_ref[:, 0:D].astype(F32))
        sb = _sigmoid(g_ref[:, D:2 * D].astype(F32))
        mg_ref[...] = (sa * a + sb * b).astype(BF16)
        da = (dm * sa).astype(BF16)
        db = (dm * sb).astype(BF16)
        da_ref[...] = da
        db_ref[...] = db
        dg_ref[:, 0:D] = (dm * a * sa * (1.0 - sa)).astype(BF16)
        dg_ref[:, D:2 * D] = (dm * b * sb * (1.0 - sb)).astype(BF16)
        dya_ref[...] = _dot(da, wbat_ref[...]).astype(BF16)
        dys_ref[...] = _dot(db, wbst_ref[...]).astype(BF16)

    return pl.pallas_call(
        body, name="merge_bwd", grid=(S // tm,),
        out_shape=(jax.ShapeDtypeStruct((S, D), BF16), jax.ShapeDtypeStruct((S, D), BF16),
                   jax.ShapeDtypeStruct((S, D), BF16), jax.ShapeDtypeStruct((S, ATT_W), BF16),
                   jax.ShapeDtypeStruct((S, SG_W), BF16), jax.ShapeDtypeStruct((S, 2 * D), BF16)),
        in_specs=[_rows(tm, D), _rows(tm, ATT_W), _rows(tm, SG_W), _rows(tm, 2 * D),
                  _resident((D, ATT_W)), _resident((D, SG_W)), _resident((D, D))],
        out_specs=(_rows(tm, D), _rows(tm, D), _rows(tm, D), _rows(tm, ATT_W), _rows(tm, SG_W), _rows(tm, 2 * D)),
        compiler_params=_cparams(1),
    )(dx1, y_att, y_sg, gates, wba_t, wbs_t, w_out)


def _sgu_bwd(uv, dy_sg, lng, lnb, sg_w, b_exp, plans=(), tm=512):
    S = uv.shape[0]
    n_steps = S // tm

    def body(uv_ref, dy_ref, lng_ref, lnb_ref, w_ref, b_ref, duv_ref, dw_ref, dbx_ref, dlng_ref, dlnb_ref, dvln_ref):
        i = pl.program_id(0)

        @pl.when(i == 0)
        def _():
            dw_ref[...] = jnp.zeros_like(dw_ref)
            dbx_ref[...] = jnp.zeros_like(dbx_ref)
            dlng_ref[...] = jnp.zeros_like(dlng_ref)
            dlnb_ref[...] = jnp.zeros_like(dlnb_ref)

        lane = lax.broadcasted_iota(jnp.int32, (1, LANES), 1)
        lo = lane < 64
        mask = _sg_mask()
        lng = lng_ref[...]
        gu, dgelu_u = _gelu_and_grad(uv_ref[:, 0:SG_W].astype(F32))
        gv, dgelu_v = _gelu_and_grad(uv_ref[:, SG_W:2 * SG_W].astype(F32))
        vln, vhat, rstd = _sg_layernorm(gv, lng, lnb_ref[...])
        dy = dy_ref[...].astype(F32)
        dsv_all = dy * gu
        for gp in range(4):
            wf0 = jnp.where(mask, w_ref[2 * gp], 0)
            wf1 = jnp.where(mask, w_ref[2 * gp + 1], 0)
            w0 = wf0.astype(BF16)
            w1 = wf1.astype(BF16)
            cols = slice(128 * gp, 128 * (gp + 1))
            dw0 = jnp.zeros((128, 128), F32)
            dw1 = jnp.zeros((128, 128), F32)
            dbx = jnp.zeros((128, LANES), F32)
            for n in range(tm // 128):
                rows = slice(128 * n, 128 * (n + 1))
                vl = vln[rows, cols]
                vl0 = jnp.where(lo, vl, 0).astype(BF16)
                vl1 = jnp.where(lo, 0, vl).astype(BF16)
                sv = _dot(w0, vl0) + _dot(w1, vl1) + b_ref[:, cols]
                duv_ref[rows, cols] = (dy[rows, cols] * sv * dgelu_u[rows, cols]).astype(BF16)
                dsv = dsv_all[rows, cols]
                dbx = dbx + dsv
                ds0 = jnp.where(lo, dsv, 0).astype(BF16)
                ds1 = jnp.where(lo, 0, dsv).astype(BF16)
                dw0 = dw0 + _dot_nt(ds0, vl0)
                dw1 = dw1 + _dot_nt(ds1, vl1)
                dvln_ref[rows, cols] = _dot_tn(w0, ds0) + _dot_tn(w1, ds1)
            dw_ref[2 * gp] += jnp.where(mask, dw0, 0)
            dw_ref[2 * gp + 1] += jnp.where(mask, dw1, 0)
            dbx_ref[:, cols] += dbx
        dvln = dvln_ref[...]
        dlng_ref[...] += jnp.sum(dvln * vhat, axis=0, keepdims=True)
        dlnb_ref[...] += jnp.sum(dvln, axis=0, keepdims=True)
        dvh = dvln * lng
        dgv = rstd * (dvh - _group_sum64(dvh) * (1.0 / 64) - vhat * (_group_sum64(dvh * vhat) * (1.0 / 64)))
        duv_ref[:, SG_W:2 * SG_W] = (dgv * dgelu_v).astype(BF16)

        @pl.when(i == n_steps - 1)
        def _():
            dbx_ref[...] = _group_sum64(dbx_ref[...])

    return _call(
        body, name="sgu_bwd", grid=(n_steps,),
        out_shape=(jax.ShapeDtypeStruct((S, 2 * SG_W), BF16), jax.ShapeDtypeStruct((8, 128, 128), F32),
                   jax.ShapeDtypeStruct((128, SG_W), F32), jax.ShapeDtypeStruct((1, SG_W), F32),
                   jax.ShapeDtypeStruct((1, SG_W), F32)),
        in_specs=[_rows(tm, 2 * SG_W), _rows(tm, SG_W), _resident((1, SG_W)), _resident((1, SG_W)),
                  _resident((8, 128, 128)), _resident((128, SG_W))],
        out_specs=(_rows(tm, 2 * SG_W), pl.BlockSpec((8, 128, 128), lambda i: (0, 0, 0)),
                   pl.BlockSpec((128, SG_W), lambda i: (0, 0)), pl.BlockSpec((1, SG_W), lambda i: (0, 0)),
                   pl.BlockSpec((1, SG_W), lambda i: (0, 0))),
        scratch_shapes=[pltpu.VMEM((tm, SG_W), F32)],
        operands=(uv, dy_sg, lng, lnb, sg_w, b_exp), plans=plans)


def _attn_bwd(qkv, dy_att, y_att, lse, bias, plans=()):
    S = qkv.shape[0]
    R = ATT_R

    def body(q_ref, kp_ref, kc_ref, vp_ref, vc_ref, b_ref, dy_ref, y_ref, lse_ref, dq_ref, dk_ref, dv_ref, dss_ref):
        i = pl.program_id(1)

        @pl.when(i == 0)
        def _():
            dk_ref[...] = jnp.zeros_like(dk_ref)
            dv_ref[...] = jnp.zeros_like(dv_ref)
            dss_ref[...] = jnp.zeros_like(dss_ref)

        lane = lax.broadcasted_iota(jnp.int32, (1, LANES), 1)
        kwin = jnp.concatenate([kp_ref[...], kc_ref[...]], axis=0)
        vwin = jnp.concatenate([vp_ref[...], vc_ref[...]], axis=0)
        for sub in range(ATT_SUBS):
            rows = slice(128 * sub, 128 * (sub + 1))
            q2 = q_ref[rows, :]
            do2 = dy_ref[rows, :]
            kw = kwin[128 * sub:128 * sub + BAND_KEYS]
            vw = vwin[128 * sub:128 * sub + BAND_KEYS]
            table = jnp.where(i == 0, 1 + sub, 0)
            dyy = do2.astype(F32) * y_ref[rows, :].astype(F32)
            lse2 = lse_ref[rows, :]
            dq = jnp.zeros((128, LANES), F32)
            dkw = jnp.zeros((BAND_KEYS, LANES), F32)
            dvw = jnp.zeros((BAND_KEYS, LANES), F32)
            for hh in range(2):
                mine = (lane >= 64 * hh) & (lane < 64 * (hh + 1))
                qm = jnp.where(mine, q2, 0)
                dom = jnp.where(mine, do2, 0)
                p = jnp.exp(_dot_nt(qm, kw) + b_ref[table, hh] - lse2[:, 64 * hh:64 * hh + 1])
                dp = _dot_nt(dom, vw)
                ds = p * (dp - jnp.sum(jnp.where(mine, dyy, 0.0), axis=-1, keepdims=True))
                dss_ref[hh] += ds
                dsb = ds.astype(BF16)
                dq = dq + _dot(dsb, jnp.where(mine, kw, 0))
                dkw = dkw + _dot_tn(dsb, qm)
                dvw = dvw + _dot_tn(p.astype(BF16), dom)
            dq_ref[rows, :] = dq.astype(BF16)
            start = pl.multiple_of(i * R + 128 * sub, 128)
            dk_ref[pl.ds(start, BAND_KEYS), :] += dkw
            dv_ref[pl.ds(start, BAND_KEYS), :] += dvw

    acc_spec = pl.BlockSpec((S + 8 * CHUNK, LANES), lambda j, i: (0, j))
    return _call(
        body, name="attn_bwd", grid=(4, S // R),
        out_shape=(jax.ShapeDtypeStruct((S, ATT_W), BF16), jax.ShapeDtypeStruct((S + 8 * CHUNK, ATT_W), F32),
                   jax.ShapeDtypeStruct((S + 8 * CHUNK, ATT_W), F32), jax.ShapeDtypeStruct((8, 128, BAND_KEYS), F32)),
        in_specs=_att_specs() + [pl.BlockSpec((R, LANES), lambda j, i: (i, j))] * 3,
        out_specs=(pl.BlockSpec((R, LANES), lambda j, i: (i, j)), acc_spec, acc_spec,
                   pl.BlockSpec((2, 128, BAND_KEYS), lambda j, i: (j, 0, 0))),
        operands=(qkv, qkv, qkv, qkv, qkv, bias, dy_att, y_att, lse), plans=plans)


def _in_bwd(dq, dk, dv, duv, dgates, x, dx1, g1, w_in_t, plans=(), tm=512):
    S = x.shape[0]
    pad_blocks = (8 * CHUNK) // tm

    def body(dq_ref, dk_ref, dv_ref, duv_ref, dg_ref, x_ref, d1_ref, g_ref, wt_ref, dx_ref, dz_ref, dg1_ref):
        i = pl.program_id(0)
        dz_ref[:, 0:ATT_W] = (dq_ref[...].astype(F32) * 0.125).astype(BF16)
        dz_ref[:, ATT_W:2 * ATT_W] = dk_ref[...].astype(BF16)
        dz_ref[:, 2 * ATT_W:3 * ATT_W] = dv_ref[...].astype(BF16)
        dz_ref[:, 3 * ATT_W:3 * ATT_W + 2 * SG_W] = duv_ref[...]
        dz_ref[:, 3 * ATT_W + 2 * SG_W:IN_COLS] = dg_ref[...]
        dh = _dot(dz_ref[...], wt_ref[...])
        xv = x_ref[...]
        dx, dg1 = _rms_bwd(dh, xv, _rstd(xv), g_ref[...])
        dx_ref[...] = d1_ref[...] + dx

        @pl.when(i == 0)
        def _():
            dg1_ref[...] = jnp.zeros_like(dg1_ref)

        dg1_ref[...] += dg1

    shifted = pl.BlockSpec((tm, ATT_W), lambda i: (i + pad_blocks, 0))
    return _call(
        body, name="in_bwd", grid=(S // tm,),
        out_shape=(jax.ShapeDtypeStruct((S, D), F32), jax.ShapeDtypeStruct((S, IN_COLS), BF16),
                   jax.ShapeDtypeStruct((1, D), F32)),
        in_specs=[_rows(tm, ATT_W), shifted, shifted, _rows(tm, 2 * SG_W), _rows(tm, 2 * D), _rows(tm, D),
                  _rows(tm, D), _resident((1, D)), _resident((IN_COLS, D))],
        out_specs=(_rows(tm, D), _rows(tm, IN_COLS), pl.BlockSpec((1, D), lambda i: (0, 0))),
        operands=(dq, dk, dv, duv, dgates, x, dx1, g1, w_in_t), plans=plans)


def _adam_math(w, g, m, v):
    m = ADAM_B1 * m + (1.0 - ADAM_B1) * g
    v = ADAM_B2 * v + (1.0 - ADAM_B2) * (g * g)
    m_hat = m / (1.0 - ADAM_B1 ** ADAM_STEP)
    v_hat = v / (1.0 - ADAM_B2 ** ADAM_STEP)
    delta = -ADAM_LR * (m_hat / (jnp.sqrt(v_hat) + ADAM_EPS) + ADAM_WD * w)
    return delta, m, v


def _adam(parts, w, m, v, tr, name, transposed):
    P = parts.shape[0]
    R, C = w.shape

    def body(p_ref, w_ref, m_ref, v_ref, g_ref, d_ref, mo_ref, vo_ref):
        if transposed:
            eye = (lax.broadcasted_iota(jnp.int32, (C, C), 0) == lax.broadcasted_iota(jnp.int32, (C, C), 1)).astype(BF16)
            part = lambda k: _dot_tn(p_ref[k], eye)
        else:
            part = lambda k: p_ref[k].astype(F32)
        g = part(0)
        for k in range(1, P):
            g = g + part(k)
        g_ref[...] = g
        d_ref[...], mo_ref[...], vo_ref[...] = _adam_math(w_ref[...], g, m_ref[...], v_ref[...])

    row = pl.BlockSpec((tr, C), lambda i: (i, 0))
    p_spec = pl.BlockSpec((P, C, tr), lambda i: (0, 0, i)) if transposed else pl.BlockSpec((P, tr, C), lambda i: (0, i, 0))
    return pl.pallas_call(
        body, name=name, grid=(R // tr,),
        out_shape=tuple(jax.ShapeDtypeStruct((R, C), F32) for _ in range(4)),
        in_specs=[p_spec, row, row, row],
        out_specs=(row, row, row, row),
        compiler_params=_cparams(1),
    )(parts, w, m, v)


def _my_place():
    return lax.axis_index("x"), lax.axis_index("y"), lax.axis_index("c")


def _other_chips(x, y):
    return [(1 - x, y), (x, 1 - y), (1 - x, 1 - y)]


class _Plan:
    def __init__(self, arrays, out_shapes, sems, start, finish, forward=None, forward_at=0.7):
        self.arrays, self.out_shapes, self.sems = list(arrays), list(out_shapes), list(sems)
        self.start, self.finish, self.forward, self.forward_at = start, finish, forward, forward_at


def _call(body, *, name, grid, in_specs, out_specs, out_shape, operands, scratch_shapes=(), plans=()):
    n_in, n_out, n_scr = len(operands), len(out_shape), len(scratch_shapes)
    p_in = [a for p in plans for a in p.arrays]
    p_out = [s for p in plans for s in p.out_shapes]
    p_sem = [s for p in plans for s in p.sems]
    steps = math.prod(grid)

    def wrapped(*refs):
        ins, refs = refs[:n_in], refs[n_in:]
        pins, refs = refs[:len(p_in)], refs[len(p_in):]
        outs, refs = refs[:n_out], refs[n_out:]
        pouts, refs = refs[:len(p_out)], refs[len(p_out):]
        scr, psems = refs[:n_scr], refs[n_scr:]
        step = 0
        for axis, size in enumerate(grid):
            step = step * size + pl.program_id(axis)
        bound = []
        for p in plans:
            bound.append((p, pins[:len(p.arrays)], pouts[:len(p.out_shapes)], psems[:len(p.sems)]))
            pins, pouts, psems = pins[len(p.arrays):], pouts[len(p.out_shapes):], psems[len(p.sems):]

        @pl.when(step == 0)
        def _():
            for p, a, b, s in bound:
                p.start(a, b, s)

        body(*ins, *outs, *scr)

        for p, a, b, s in bound:
            if p.forward is not None:
                @pl.when(step == min(int(p.forward_at * steps), steps - 1))
                def _(p=p, a=a, b=b, s=s):
                    p.forward(a, b, s)

        @pl.when(step == steps - 1)
        def _():
            for p, a, b, s in bound:
                p.finish(a, b, s)

    hbm = pl.BlockSpec(memory_space=pl.ANY)
    return pl.pallas_call(
        wrapped if plans else body, name=name, grid=grid,
        out_shape=tuple(out_shape) + tuple(p_out),
        in_specs=list(in_specs) + [hbm] * len(p_in),
        out_specs=tuple(out_specs) + tuple([hbm] * len(p_out)),
        scratch_shapes=list(scratch_shapes) + p_sem,
        compiler_params=_cparams(len(grid)),
    )(*operands, *p_in)


def _run_plan(plan, name):
    n_in, n_out = len(plan.arrays), len(plan.out_shapes)

    def body(*refs):
        a, b, s = refs[:n_in], refs[n_in:n_in + n_out], refs[n_in + n_out:]
        plan.start(a, b, s)
        if plan.forward is not None:
            plan.forward(a, b, s)
        plan.finish(a, b, s)

    hbm = pl.BlockSpec(memory_space=pl.ANY)
    return pl.pallas_call(
        body, name=name, out_shape=tuple(plan.out_shapes),
        in_specs=[hbm] * n_in, out_specs=tuple([hbm] * n_out), scratch_shapes=plan.sems,
    )(*plan.arrays)


def _gather_plan(shards, forward_at=0.7):
    n = len(shards)

    def copies(ins, outs, sems):
        send_sems, recv_sems, local_sems = sems
        x, y, c = _my_place()
        me, sibling = (x, y, c), (x, y, 1 - c)
        chips = _other_chips(x, y)

        def idx(p):
            return 4 * p[0] + 2 * p[1] + p[2]

        def copy(a, k, block, to, src=None):
            return pltpu.make_async_remote_copy(
                src_ref=outs[a].at[idx(block)] if src is None else src, dst_ref=outs[a].at[idx(block)],
                send_sem=send_sems.at[a, k], recv_sem=recv_sems.at[a, k], device_id=to, device_id_type=MESH)

        mine = [pltpu.make_async_copy(ins[a], outs[a].at[idx(me)], local_sems.at[a]) for a in range(n)]
        first = []
        for a in range(n):
            first.append(copy(a, 0, me, sibling, src=ins[a]))
            first += [copy(a, 1 + j, me, (*chip, c), src=ins[a]) for j, chip in enumerate(chips)]
        arrived = [copy(a, 1 + j, (*chip, c), me) for j, chip in enumerate(chips) for a in range(n)]
        passed = [copy(a, 4 + j, (*chip, c), sibling) for j, chip in enumerate(chips) for a in range(n)]
        from_sibling = []
        for a in range(n):
            from_sibling.append(copy(a, 0, sibling, me))
            from_sibling += [copy(a, 4 + j, (*chip, 1 - c), me) for j, chip in enumerate(chips)]
        return mine, first, arrived, passed, from_sibling

    def start(ins, outs, sems):
        mine, first, _, _, _ = copies(ins, outs, sems)
        for cp in mine + first:
            cp.start()

    def forward(ins, outs, sems):
        _, _, arrived, passed, _ = copies(ins, outs, sems)
        for landed, onward in zip(arrived, passed):
            landed.wait_recv()
            onward.start()

    def finish(ins, outs, sems):
        mine, first, _, passed, from_sibling = copies(ins, outs, sems)
        for cp in from_sibling:
            cp.wait_recv()
        for cp in first + passed:
            cp.wait_send()
        for cp in mine:
            cp.wait()

    return _Plan(shards, [jax.ShapeDtypeStruct((N_DEV,) + s.shape, s.dtype) for s in shards],
                 [pltpu.SemaphoreType.DMA((n, 7)), pltpu.SemaphoreType.DMA((n, 7)), pltpu.SemaphoreType.DMA((n,))],
                 start, finish, forward, forward_at)


def _sibling_plan(scatter, whole=()):
    ns = len(scatter)
    n = ns + len(whole)

    def copies(ins, outs, sems):
        send_sems, recv_sems = sems
        x, y, c = _my_place()
        out = []
        for a in range(n):
            for k in range(4 if a < ns else 1):
                src = ins[a].at[2 * k + (1 - c)] if a < ns else ins[a]
                dst = outs[a].at[k] if a < ns else outs[a]
                out.append(pltpu.make_async_remote_copy(
                    src_ref=src, dst_ref=dst, send_sem=send_sems.at[a, k], recv_sem=recv_sems.at[a, k],
                    device_id=(x, y, 1 - c), device_id_type=MESH))
        return out

    def start(ins, outs, sems):
        for cp in copies(ins, outs, sems):
            cp.start()

    def finish(ins, outs, sems):
        for cp in copies(ins, outs, sems):
            cp.wait()

    shapes = [jax.ShapeDtypeStruct((4,) + p.shape[1:], p.dtype) for p in scatter]
    shapes += [jax.ShapeDtypeStruct(p.shape, p.dtype) for p in whole]
    return _Plan(list(scatter) + list(whole), shapes,
                 [pltpu.SemaphoreType.DMA((n, 4)), pltpu.SemaphoreType.DMA((n, 4))], start, finish)


def _pair_sum(mine, theirs, c_idx, name):
    _, R, C = mine.shape

    def body(c_ref, a_ref, b_ref, o_ref):
        o_ref[...] = (a_ref[...].astype(F32) + b_ref[...].astype(F32)).astype(o_ref.dtype)

    grid_spec = pltpu.PrefetchScalarGridSpec(
        num_scalar_prefetch=1, grid=(4,),
        in_specs=[pl.BlockSpec((1, R, C), lambda k, c_ref: (2 * k + c_ref[0], 0, 0)),
                  pl.BlockSpec((1, R, C), lambda k, c_ref: (k, 0, 0))],
        out_specs=pl.BlockSpec((1, R, C), lambda k, c_ref: (k, 0, 0)))
    return pl.pallas_call(
        body, name=name, grid_spec=grid_spec,
        out_shape=jax.ShapeDtypeStruct((4, R, C), mine.dtype),
        compiler_params=_cparams(1),
    )(c_idx, mine, theirs)


def _peers_plan(arrays):
    n = len(arrays)

    def copies(ins, outs, sems):
        send_sems, recv_sems, local_sems = sems
        x, y, c = _my_place()
        me = 4 * x + 2 * y + c
        out = [pltpu.make_async_copy(ins[a], outs[a].at[me], local_sems.at[a]) for a in range(n)]
        for a in range(n):
            for k in range(N_DEV - 1):
                bits = k + 1
                peer = (x ^ (bits >> 2), y ^ ((bits >> 1) & 1), c ^ (bits & 1))
                out.append(pltpu.make_async_remote_copy(
                    src_ref=ins[a], dst_ref=outs[a].at[me], send_sem=send_sems.at[a, k], recv_sem=recv_sems.at[a, k],
                    device_id=peer, device_id_type=MESH))
        return out

    def start(ins, outs, sems):
        for cp in copies(ins, outs, sems):
            cp.start()

    def finish(ins, outs, sems):
        for cp in copies(ins, outs, sems):
            cp.wait()

    return _Plan(list(arrays), [jax.ShapeDtypeStruct((N_DEV,) + a.shape, a.dtype) for a in arrays],
                 [pltpu.SemaphoreType.DMA((n, N_DEV - 1)), pltpu.SemaphoreType.DMA((n, N_DEV - 1)),
                  pltpu.SemaphoreType.DMA((n,))], start, finish)


def _chips_plan(scatter, whole=()):
    ns = len(scatter)
    n = ns + len(whole)

    def copies(ins, outs, sems):
        send_sems, recv_sems, local_sems = sems
        x, y, c = _my_place()
        my_chip = 2 * x + y

        def src(a, k):
            return ins[a].at[k] if a < ns else ins[a]

        local = [pltpu.make_async_copy(src(a, my_chip), outs[a].at[my_chip], local_sems.at[a]) for a in range(n)]
        remote = []
        for a in range(n):
            for j, (px, py) in enumerate(_other_chips(x, y)):
                remote.append(pltpu.make_async_remote_copy(
                    src_ref=src(a, 2 * px + py), dst_ref=outs[a].at[my_chip],
                    send_sem=send_sems.at[a, j], recv_sem=recv_sems.at[a, j],
                    device_id=(px, py, c), device_id_type=MESH))
        return local + remote

    def start(ins, outs, sems):
        for cp in copies(ins, outs, sems):
            cp.start()

    def finish(ins, outs, sems):
        for cp in copies(ins, outs, sems):
            cp.wait()

    shapes = [jax.ShapeDtypeStruct(s.shape, s.dtype) for s in scatter]
    shapes += [jax.ShapeDtypeStruct((4,) + s.shape, s.dtype) for s in whole]
    return _Plan(list(scatter) + list(whole), shapes,
                 [pltpu.SemaphoreType.DMA((n, 3)), pltpu.SemaphoreType.DMA((n, 3)), pltpu.SemaphoreType.DMA((n,))],
                 start, finish)


def _small_update(parts, w, m, v, loss_parts, name):
    n = len(parts)

    def total(ref):
        acc = ref[0]
        for k in range(1, ref.shape[0]):
            acc = acc + ref[k]
        return acc

    def body(*refs):
        p_refs, w_refs, m_refs, v_refs = (refs[i * n:(i + 1) * n] for i in range(4))
        lp_ref = refs[4 * n]
        outs = refs[4 * n + 1:]
        g_refs, d_refs, mo_refs, vo_refs = (outs[i * n:(i + 1) * n] for i in range(4))
        for a in range(n):
            g = total(p_refs[a])
            g_refs[a][...] = g
            d_refs[a][...], mo_refs[a][...], vo_refs[a][...] = _adam_math(w_refs[a][...], g, m_refs[a][...],
                                                                          v_refs[a][...])
        outs[4 * n][...] = total(lp_ref)

    vm = pl.BlockSpec(memory_space=pltpu.VMEM)
    shapes = [jax.ShapeDtypeStruct(t.shape, F32) for _ in range(4) for t in w]
    shapes.append(jax.ShapeDtypeStruct(loss_parts.shape[1:], F32))
    outs = pl.pallas_call(
        body, name=name, out_shape=tuple(shapes),
        in_specs=[vm] * (4 * n + 1), out_specs=tuple([vm] * (4 * n + 1)),
        compiler_params=pltpu.CompilerParams(vmem_limit_bytes=VMEM_LIMIT),
    )(*parts, *w, *m, *v, loss_parts)
    return outs[0:n], outs[n:2 * n], outs[2 * n:3 * n], outs[3 * n:4 * n], outs[4 * n]


BIG = [("w_in", 1), ("w_branch_att", 1), ("w_branch_sg", 1), ("w_out", 0), ("w_xq", 0), ("w_xkv", 1), ("w_xo", 0),
       ("w_ffn_in", 1), ("w_ffn_out", 0)]
SMALL = [("norm_mix_g", (1, D)), ("rel_bias", (8, NREL)), ("sg_ln_g", (8, 64)), ("sg_ln_b", (8, 64)),
         ("sg_w", (8, 128, 128)), ("sg_b", (8, 128)), ("norm_xattn_g", (1, D)), ("norm_mem_g", (1, D)),
         ("norm_ffn_g", (1, D)), ("norm_final_g", (1, D))]
ADAM_ROWS = {"w_in": 192, "w_branch_att": 512, "w_branch_sg": 512, "w_xkv": 1024, "w_ffn_in": 176}


def _full(gathered):
    return gathered.reshape(N_DEV * gathered.shape[1], gathered.shape[2])


def _blocks(grad):
    return grad.reshape(N_DEV, grad.shape[0] // N_DEV, grad.shape[1])


def kernel(x, mem, norm_mix_g, w_in, rel_bias, sg_ln_g, sg_ln_b, sg_w, sg_b, w_branch_att, w_branch_sg, w_out, norm_xattn_g, norm_mem_g, w_xq, w_xkv, w_xo, norm_ffn_g, w_ffn_in, w_ffn_out, norm_final_g, loss_target, m_norm_mix_g, m_w_in, m_rel_bias, m_sg_ln_g, m_sg_ln_b, m_sg_w, m_sg_b, m_w_branch_att, m_w_branch_sg, m_w_out, m_norm_xattn_g, m_norm_mem_g, m_w_xq, m_w_xkv, m_w_xo, m_norm_ffn_g, m_w_ffn_in, m_w_ffn_out, m_norm_final_g, v_norm_mix_g, v_w_in, v_rel_bias, v_sg_ln_g, v_sg_ln_b, v_sg_w, v_sg_b, v_w_branch_att, v_w_branch_sg, v_w_out, v_norm_xattn_g, v_norm_mem_g, v_w_xq, v_w_xkv, v_w_xo, v_norm_ffn_g, v_w_ffn_in, v_w_ffn_out, v_norm_final_g):
    args = dict(locals())
    big_names = [n for n, _ in BIG]
    small_names = [n for n, _ in SMALL]
    S = x.shape[1]

    x, mem, tgt = x.reshape(S, D), mem.reshape(MEM, D), loss_target.reshape(S, D)
    small = {n: args[n].reshape(shape) for n, shape in SMALL}
    g1, g2, g3 = small["norm_mix_g"], small["norm_xattn_g"], small["norm_ffn_g"]
    g_mem, g4 = small["norm_mem_g"], small["norm_final_g"]
    lng = small["sg_ln_g"].reshape(1, SG_W)
    lnb = small["sg_ln_b"].reshape(1, SG_W)
    b_exp = jnp.broadcast_to(small["sg_b"].T[:, :, None], (128, 8, 64)).reshape(128, SG_W)
    rel_pad = jnp.pad(small["rel_bias"], ((0, 0), (0, 384 - NREL)))
    c_idx = lax.axis_index("c").astype(jnp.int32).reshape(1)

    shard = {n: (args[n][0].T if axis == 1 else args[n][0]).astype(BF16) for n, axis in BIG}
    h, w_in_gathered = _norm_in(x, g1, plans=[_gather_plan([shard["w_in"]])])
    w_in_t = _full(w_in_gathered)
    bias = _bias_table(rel_pad)
    mix_names = ["w_branch_att", "w_branch_sg", "w_out", "w_xq", "w_xkv", "w_xo"]
    qkv, uv, gates, *got = _in_proj(h, w_in_t, plans=[_gather_plan([shard[n] for n in mix_names])])
    wba_t, wbs_t, w_out_f, w_xq_f, w_xkv_t, w_xo_f = (_full(g) for g in got)
    y_att, lse, *got = _attn_fwd(qkv, bias, plans=[_gather_plan([shard["w_ffn_in"], shard["w_ffn_out"]])])
    w_ffn_in_t, w_ffn_out_f = (_full(g) for g in got)
    y_sg = _sgu_fwd(uv, lng, lnb, small["sg_w"], b_exp)
    x1 = _merge_fwd(x, y_att, y_sg, gates, wba_t, wbs_t, w_out_f)
    kv, mn = _mem_kv(mem, g_mem, w_xkv_t)
    x2, hx, qx, o_x = _xattn_fwd(x1, g2, w_xq_f, kv, w_xo_f)
    dx3, gu, hf, act, loss_part, dg4 = _ffn_fwd(x2, tgt, g3, w_ffn_in_t, w_ffn_out_f, g4)

    def pair_sums(names, mine, theirs):
        return [_pair_sum(a, b, c_idx, "rs_pair_" + n) for n, a, b in zip(names, mine, theirs)]

    dx2, dgu, dg3 = _ffn_bwd(dx3, gu, x2, g3, w_ffn_out_f, w_ffn_in_t)
    ffn_names = ["w_ffn_out", "w_ffn_in"]
    ffn_mine = [_blocks(_dw(act, dx3, 1408, 1024, DW_TOKENS, "dw_ffn_out")),
                _blocks(_dw(dgu, hf, 1408, 1024, DW_TOKENS, "dw_ffn_in"))]
    dx1, dq_x, dkv, dg2, *ffn_theirs = _xattn_bwd(dx2, x1, qx, g2, w_xq_f, w_xo_f, kv,
                                                  plans=[_sibling_plan(ffn_mine)])
    ffn_chip = pair_sums(ffn_names, ffn_mine, ffn_theirs)
    d_xkv, dg_mem = _mem_kv_bwd(dkv, mem, g_mem, mn, w_xkv_t)
    merged, d_a, d_b, dy_att, dy_sg, dgates = _merge_bwd(dx1, y_att, y_sg, gates, wba_t, wbs_t, w_out_f)
    mid_names = ["w_xo", "w_xq", "w_xkv", "w_out", "w_branch_att", "w_branch_sg"]
    mid_mine = [_blocks(g) for g in (
        _dw(o_x, dx2, 1024, 1024, DW_TOKENS, "dw_xo"), _dw(hx, dq_x, 1024, 1024, DW_TOKENS, "dw_xq"), d_xkv,
        _dw(merged, dx1, 1024, 1024, DW_TOKENS, "dw_out"), _dw(d_a, y_att, 1024, 512, DW_TOKENS, "dw_branch_att"),
        _dw(d_b, y_sg, 1024, 512, DW_TOKENS, "dw_branch_sg"))]
    duv, d_sgw, d_bx, d_lng, d_lnb, *got = _sgu_bwd(uv, dy_sg, lng, lnb, small["sg_w"], b_exp,
                                                    plans=[_chips_plan(ffn_chip), _sibling_plan(mid_mine)])
    ffn_all, mid_theirs = got[:2], got[2:]
    mid_chip = pair_sums(mid_names, mid_mine, mid_theirs)
    dq, dk, dv, ds_sum, *mid_all = _attn_bwd(qkv, dy_att, y_att, lse, bias, plans=[_chips_plan(mid_chip)])
    d_rel = _bias_grad(ds_sum)

    grad_x, dz, dg1 = _in_bwd(dq, dk, dv, duv, dgates, x, dx1, g1, w_in_t)

    gs = {"norm_mix_g": dg1, "rel_bias": d_rel[:, :NREL], "sg_ln_g": d_lng.reshape(8, 64),
          "sg_ln_b": d_lnb.reshape(8, 64), "sg_w": d_sgw, "sg_b": d_bx.reshape(128, 8, 64)[:, :, 0].T,
          "norm_xattn_g": dg2, "norm_mem_g": dg_mem, "norm_ffn_g": dg3, "norm_final_g": dg4}
    d_in, *everyone = _dw(dz, h, 1152, 1024, DW_TOKENS, "dw_in",
                          plans=[_peers_plan([gs[n] for n in small_names] + [loss_part])])

    in_mine = [_blocks(d_in)]
    (in_theirs,) = _run_plan(_sibling_plan(in_mine), "rs_sibling")
    in_chip = pair_sums(["w_in"], in_mine, [in_theirs])
    (in_all,) = _run_plan(_chips_plan(in_chip), "rs_chips")
    all_parts = dict(zip(ffn_names + mid_names + ["w_in"], list(ffn_all) + list(mid_all) + [in_all]))

    res = {}
    for n, axis in BIG:
        parts = all_parts[n]
        wmv = [args[p + n][0] for p in ("", "m_", "v_")]
        if axis == 1 and wmv[0].shape[1] % LANES != 0:
            outs = _adam(parts, *(t.T for t in wmv), ADAM_ROWS[n], "adam_" + n, transposed=False)
            res[n] = [t.T[None] for t in outs]
        else:
            tr = ADAM_ROWS[n] if axis == 1 else wmv[0].shape[0]
            res[n] = [t[None] for t in _adam(parts, *wmv, tr, "adam_" + n, transposed=(axis == 1))]
    small_res = _small_update(
        everyone[:-1], [small[n] for n in small_names],
        [args["m_" + n].reshape(s) for n, s in SMALL], [args["v_" + n].reshape(s) for n, s in SMALL],
        everyone[-1], "adam_small")
    for i, n in enumerate(small_names):
        res[n] = [small_res[k][i].reshape(args[n].shape) for k in range(4)]
    loss = small_res[4][0, 0]

    order = ["norm_mix_g", "w_in", "rel_bias", "sg_ln_g", "sg_ln_b", "sg_w", "sg_b", "w_branch_att", "w_branch_sg",
             "w_out", "norm_xattn_g", "norm_mem_g", "w_xq", "w_xkv", "w_xo", "norm_ffn_g", "w_ffn_in", "w_ffn_out",
             "norm_final_g"]
    outs = [loss, grad_x.reshape(1, S, D)]
    for k in range(4):
        outs += [res[n][k] for n in order]
    return tuple(outs)
```

```python
import math

import jax
import jax.numpy as jnp
from jax import lax
from jax.experimental import pallas as pl
from jax.experimental.pallas import tpu as pltpu

F32 = jnp.float32
BF16 = jnp.bfloat16

D = 1024
ATT_W = 512
SG_W = 512
IN_COLS = 4608
DFF = 2816
MEM = 256
XH = 4
CHUNK = 64
BAND_KEYS = 640
ATT_R = 512
ATT_SUBS = ATT_R // 128
DW_TOKENS = 2048
REL_CLIP = 128
NREL = 2 * REL_CLIP + 1
EPS = 1e-6
NEG = -1e30
N_DEV = 8

ADAM_LR = 0.001
ADAM_B1 = 0.9
ADAM_B2 = 0.999
ADAM_EPS = 1e-08
ADAM_WD = 0.01
ADAM_STEP = 10

LANES = 128
VMEM_LIMIT = 56 * 1024 * 1024
MESH = pl.DeviceIdType.MESH


def _cparams(n_axes):
    return pltpu.CompilerParams(dimension_semantics=("arbitrary",) * n_axes, vmem_limit_bytes=VMEM_LIMIT)


def _resident(shape):
    zeros = (0,) * len(shape)
    return pl.BlockSpec(shape, lambda *_: zeros, pipeline_mode=pl.Buffered(1))


def _rows(tm, cols, col_block=0):
    return pl.BlockSpec((tm, cols), lambda i: (i, col_block))


def _sigmoid(x):
    return pl.reciprocal(1.0 + jnp.exp(-x), approx=True)


_GELU_C = math.sqrt(2.0 / math.pi)


def _gelu(x):
    t = jnp.tanh(_GELU_C * (x + 0.044715 * (x * x * x)))
    return x * (0.5 * (1.0 + t))


def _gelu_and_grad(x):
    x2 = x * x
    t = jnp.tanh(_GELU_C * (x + 0.044715 * (x2 * x)))
    cdf = 0.5 * (1.0 + t)
    dcdf = 0.5 * (1.0 - t * t) * (_GELU_C * (1.0 + 3.0 * 0.044715 * x2))
    return x * cdf, cdf + x * dcdf


def _rstd(x):
    return lax.rsqrt(jnp.mean(x * x, axis=-1, keepdims=True) + EPS)


def _rms_bwd(dh, x, r, g):
    xh = x * r
    dxh = dh * g
    dx = r * (dxh - xh * jnp.mean(dxh * xh, axis=-1, keepdims=True))
    dg = jnp.sum(dh * xh, axis=0, keepdims=True)
    return dx, dg


def _group_sum64(x):
    r = lax.broadcasted_iota(jnp.int32, (LANES, LANES), 0) // 64
    c = lax.broadcasted_iota(jnp.int32, (LANES, LANES), 1) // 64
    same_group = (r == c).astype(BF16)

    def one(v):
        hi = v.astype(BF16)
        rest = v - hi.astype(F32)
        mid = rest.astype(BF16)
        lo = (rest - mid.astype(F32)).astype(BF16)
        return _dot(hi, same_group) + _dot(mid, same_group) + _dot(lo, same_group)

    pieces = [one(x[:, LANES * j:LANES * (j + 1)]) for j in range(x.shape[1] // LANES)]
    return pieces[0] if len(pieces) == 1 else jnp.concatenate(pieces, axis=1)


def _dot(a, b):
    return jnp.dot(a, b, preferred_element_type=F32)


def _dot_nt(a, b):
    return lax.dot_general(a, b, (((1,), (1,)), ((), ())), preferred_element_type=F32)


def _dot_tn(a, b):
    return lax.dot_general(a, b, (((0,), (0,)), ((), ())), preferred_element_type=F32)


DIAGS = 768


def _diag_onehot():
    r_idx = lax.broadcasted_iota(jnp.int32, (384, DIAGS), 0)
    t_idx = lax.broadcasted_iota(jnp.int32, (384, DIAGS), 1)
    dist = (8 * CHUNK + 127) - t_idx
    return (jnp.clip(dist, -REL_CLIP, REL_CLIP) + REL_CLIP == r_idx).astype(F32)


def _shift_rows(x, reverse):
    row = lax.broadcasted_iota(jnp.int32, x.shape, 0)
    for k in range(7):
        amt = (DIAGS - (1 << k)) if reverse else (1 << k)
        x = jnp.where(((row >> k) & 1) == 1, pltpu.roll(x, amt, 1), x)
    return x


N_TABLES = 1 + ATT_SUBS


def _bias_table(rel_bias_pad):
    def body(rb_ref, out_ref):
        per_diag = jnp.dot(rb_ref[...], _diag_onehot(), preferred_element_type=F32,
                           precision=lax.Precision.HIGHEST)
        a = lax.broadcasted_iota(jnp.int32, (128, BAND_KEYS), 0)
        b = lax.broadcasted_iota(jnp.int32, (128, BAND_KEYS), 1)
        band = (b // CHUNK >= a // CHUNK) & (b // CHUNK <= a // CHUNK + 8)
        for h in range(8):
            rows = jnp.broadcast_to(per_diag[h:h + 1, :], (128, DIAGS))
            table = _shift_rows(pltpu.roll(rows, DIAGS - 127, 1), reverse=False)[:, :BAND_KEYS]
            out_ref[0, h] = jnp.where(band, table, NEG)
            for s in range(ATT_SUBS):
                out_ref[1 + s, h] = jnp.where(band & (b >= 8 * CHUNK - 128 * s), table, NEG)

    return pl.pallas_call(
        body, name="bias_table",
        out_shape=jax.ShapeDtypeStruct((N_TABLES, 8, 128, BAND_KEYS), F32),
        in_specs=[pl.BlockSpec(memory_space=pltpu.VMEM)],
        out_specs=pl.BlockSpec(memory_space=pltpu.VMEM),
        compiler_params=pltpu.CompilerParams(vmem_limit_bytes=VMEM_LIMIT),
    )(rel_bias_pad)


def _bias_grad(ds_sum):
    def body(ds_ref, out_ref):
        sums = []
        for h in range(8):
            padded = jnp.concatenate([ds_ref[h], jnp.zeros((128, DIAGS - BAND_KEYS), F32)], axis=1)
            skewed = pltpu.roll(_shift_rows(padded, reverse=True), 127, 1)
            sums.append(jnp.sum(skewed, axis=0, keepdims=True))
        per_diag = jnp.concatenate(sums, axis=0)
        out_ref[...] = lax.dot_general(per_diag, _diag_onehot(), (((1,), (1,)), ((), ())),
                                       preferred_element_type=F32, precision=lax.Precision.HIGHEST)

    return pl.pallas_call(
        body, name="bias_grad",
        out_shape=jax.ShapeDtypeStruct((8, 384), F32),
        in_specs=[pl.BlockSpec(memory_space=pltpu.VMEM)],
        out_specs=pl.BlockSpec(memory_space=pltpu.VMEM),
    )(ds_sum)


def _norm_in(x, g1, plans=(), tm=512):
    S = x.shape[0]

    def body(x_ref, g_ref, h_ref):
        xv = x_ref[...]
        h_ref[...] = (xv * _rstd(xv) * g_ref[...]).astype(BF16)

    return _call(
        body, name="norm_in", grid=(S // tm,),
        out_shape=(jax.ShapeDtypeStruct((S, D), BF16),),
        in_specs=[_rows(tm, D), _resident((1, D))], out_specs=(_rows(tm, D),),
        operands=(x, g1), plans=plans)


def _in_proj(h, w_in_t, plans=(), tm=512):
    S = h.shape[0]

    def body(h_ref, w_ref, qkv_ref, uv_ref, gate_ref):
        h = h_ref[...]
        for c in range(IN_COLS // 512):
            zc = _dot_nt(h, w_ref[512 * c:512 * (c + 1), :])
            if c == 0:
                qkv_ref[:, 0:512] = (zc * 0.125).astype(BF16)
            elif c < 3:
                qkv_ref[:, 512 * c:512 * (c + 1)] = zc.astype(BF16)
            elif c < 5:
                uv_ref[:, 512 * (c - 3):512 * (c - 2)] = zc.astype(BF16)
            else:
                gate_ref[:, 512 * (c - 5):512 * (c - 4)] = zc.astype(BF16)

    return _call(
        body, name="in_proj", grid=(S // tm,),
        out_shape=(jax.ShapeDtypeStruct((S, 3 * ATT_W), BF16), jax.ShapeDtypeStruct((S, 2 * SG_W), BF16),
                   jax.ShapeDtypeStruct((S, 2 * D), BF16)),
        in_specs=[_rows(tm, D), _resident((IN_COLS, D))],
        out_specs=(_rows(tm, 3 * ATT_W), _rows(tm, 2 * SG_W), _rows(tm, 2 * D)),
        operands=(h, w_in_t), plans=plans)


def _two_heads(a, lo):
    return jnp.concatenate([jnp.where(lo, a, 0), jnp.where(lo, 0, a)], axis=0)


def _att_specs():
    R = ATT_R
    q = pl.BlockSpec((R, LANES), lambda j, i: (i, j))
    kp = pl.BlockSpec((R, LANES), lambda j, i: (jnp.maximum(i - 1, 0), 4 + j))
    kc = pl.BlockSpec((R, LANES), lambda j, i: (i, 4 + j))
    vp = pl.BlockSpec((R, LANES), lambda j, i: (jnp.maximum(i - 1, 0), 8 + j))
    vc = pl.BlockSpec((R, LANES), lambda j, i: (i, 8 + j))
    bias = pl.BlockSpec((N_TABLES, 2, 128, BAND_KEYS), lambda j, i: (0, j, 0, 0))
    return [q, kp, kc, vp, vc, bias]


def _attn_fwd(qkv, bias, plans=()):
    S = qkv.shape[0]
    R = ATT_R

    def body(q_ref, kp_ref, kc_ref, vp_ref, vc_ref, b_ref, o_ref, lse_ref):
        i = pl.program_id(1)
        lo = lax.broadcasted_iota(jnp.int32, (1, LANES), 1) < 64
        kwin = jnp.concatenate([kp_ref[...], kc_ref[...]], axis=0)
        vwin = jnp.concatenate([vp_ref[...], vc_ref[...]], axis=0)
        for sub in range(ATT_SUBS):
            q2 = q_ref[128 * sub:128 * (sub + 1), :]
            kw = kwin[128 * sub:128 * sub + BAND_KEYS]
            vw = vwin[128 * sub:128 * sub + BAND_KEYS]
            table = jnp.where(i == 0, 1 + sub, 0)
            s = _dot_nt(_two_heads(q2, lo), kw) + b_ref[table].reshape(256, BAND_KEYS)
            top = jnp.max(s, axis=-1, keepdims=True)
            p = jnp.exp(s - top)
            total = jnp.sum(p, axis=-1, keepdims=True)
            p = (p / total).astype(BF16)
            o = _dot(jnp.concatenate([p[:128], p[128:]], axis=1), _two_heads(vw, lo))
            lse = top + jnp.log(total)
            o_ref[128 * sub:128 * (sub + 1), :] = o.astype(BF16)
            lse_ref[128 * sub:128 * (sub + 1), :] = jnp.where(lo, lse[:128], lse[128:])

    blk = pl.BlockSpec((R, LANES), lambda j, i: (i, j))
    return _call(
        body, name="attn_fwd", grid=(4, S // R),
        out_shape=(jax.ShapeDtypeStruct((S, ATT_W), BF16), jax.ShapeDtypeStruct((S, ATT_W), F32)),
        in_specs=_att_specs(), out_specs=(blk, blk),
        operands=(qkv, qkv, qkv, qkv, qkv, bias), plans=plans)


def _sg_mask():
    t = lax.broadcasted_iota(jnp.int32, (128, 128), 0)
    s = lax.broadcasted_iota(jnp.int32, (128, 128), 1)
    return (s // CHUNK) <= (t // CHUNK)


def _sg_layernorm(gv, lng, lnb):
    mu = _group_sum64(gv) * (1.0 / 64)
    xc = gv - mu
    var = _group_sum64(xc * xc) * (1.0 / 64)
    rstd = lax.rsqrt(var + EPS)
    vhat = xc * rstd
    return vhat * lng + lnb, vhat, rstd


def _sgu_fwd(uv, lng, lnb, sg_w, b_exp, tm=512):
    S = uv.shape[0]

    def body(uv_ref, lng_ref, lnb_ref, w_ref, b_ref, y_ref):
        lane = lax.broadcasted_iota(jnp.int32, (1, LANES), 1)
        lo = lane < 64
        mask = _sg_mask()
        gu = _gelu(uv_ref[:, 0:SG_W].astype(F32))
        vln, _, _ = _sg_layernorm(_gelu(uv_ref[:, SG_W:2 * SG_W].astype(F32)), lng_ref[...], lnb_ref[...])
        for gp in range(4):
            w0 = jnp.where(mask, w_ref[2 * gp], 0).astype(BF16)
            w1 = jnp.where(mask, w_ref[2 * gp + 1], 0).astype(BF16)
            cols = slice(128 * gp, 128 * (gp + 1))
            for n in range(tm // 128):
                rows = slice(128 * n, 128 * (n + 1))
                vl = vln[rows, cols]
                sv = (_dot(w0, jnp.where(lo, vl, 0).astype(BF16)) + _dot(w1, jnp.where(lo, 0, vl).astype(BF16))
                      + b_ref[:, cols])
                y_ref[rows, cols] = (gu[rows, cols] * sv).astype(BF16)

    return pl.pallas_call(
        body, name="sgu_fwd", grid=(S // tm,),
        out_shape=jax.ShapeDtypeStruct((S, SG_W), BF16),
        in_specs=[_rows(tm, 2 * SG_W), _resident((1, SG_W)), _resident((1, SG_W)),
                  _resident((8, 128, 128)), _resident((128, SG_W))],
        out_specs=_rows(tm, SG_W),
        compiler_params=_cparams(1),
    )(uv, lng, lnb, sg_w, b_exp)


def _merge_fwd(x, y_att, y_sg, gates, wba_t, wbs_t, w_out, tm=512):
    S = x.shape[0]

    def body(x_ref, ya_ref, ys_ref, g_ref, wba_ref, wbs_ref, wo_ref, x1_ref):
        a = _dot_nt(ya_ref[...], wba_ref[...])
        b = _dot_nt(ys_ref[...], wbs_ref[...])
        merged = _sigmoid(g_ref[:, 0:D].astype(F32)) * a + _sigmoid(g_ref[:, D:2 * D].astype(F32)) * b
        x1_ref[...] = x_ref[...] + _dot(merged.astype(BF16), wo_ref[...])

    return pl.pallas_call(
        body, name="merge_fwd", grid=(S // tm,),
        out_shape=jax.ShapeDtypeStruct((S, D), F32),
        in_specs=[_rows(tm, D), _rows(tm, ATT_W), _rows(tm, SG_W), _rows(tm, 2 * D),
                  _resident((D, ATT_W)), _resident((D, SG_W)), _resident((D, D))],
        out_specs=_rows(tm, D),
        compiler_params=_cparams(1),
    )(x, y_att, y_sg, gates, wba_t, wbs_t, w_out)


def _mem_kv(mem, g_mem, w_xkv_t):
    def body(m_ref, g_ref, w_ref, kv_ref, mn_ref):
        mv = m_ref[...]
        mn = (mv * _rstd(mv) * g_ref[...]).astype(BF16)
        mn_ref[...] = mn
        kv_ref[...] = _dot_nt(mn, w_ref[...]).astype(BF16)

    vm = pl.BlockSpec(memory_space=pltpu.VMEM)
    return pl.pallas_call(
        body, name="mem_kv",
        out_shape=(jax.ShapeDtypeStruct((MEM, 2 * D), BF16), jax.ShapeDtypeStruct((MEM, D), BF16)),
        in_specs=[vm, vm, vm], out_specs=(vm, vm),
        compiler_params=pltpu.CompilerParams(vmem_limit_bytes=VMEM_LIMIT),
    )(mem, g_mem, w_xkv_t)


def _xatt_head(qx, kv_ref, h):
    hs = slice(256 * h, 256 * (h + 1))
    s = _dot_nt(qx[:, hs], kv_ref[:, hs])
    p = jnp.exp(s - jnp.max(s, axis=-1, keepdims=True))
    return p / jnp.sum(p, axis=-1, keepdims=True)


def _xattn_fwd(x1, g2, w_xq, kv, w_xo, tm=512):
    S = x1.shape[0]

    def body(x_ref, g_ref, wq_ref, kv_ref, wo_ref, x2_ref, hx_ref, qx_ref, o_ref):
        xv = x_ref[...]
        hx = (xv * _rstd(xv) * g_ref[...]).astype(BF16)
        hx_ref[...] = hx
        qx = (_dot(hx, wq_ref[...]) * (1.0 / 16)).astype(BF16)
        qx_ref[...] = qx
        for h in range(XH):
            p = _xatt_head(qx, kv_ref, h)
            o_ref[:, 256 * h:256 * (h + 1)] = _dot(p.astype(BF16), kv_ref[:, D + 256 * h:D + 256 * (h + 1)]).astype(BF16)
        x2_ref[...] = xv + _dot(o_ref[...], wo_ref[...])

    return pl.pallas_call(
        body, name="xattn_fwd", grid=(S // tm,),
        out_shape=(jax.ShapeDtypeStruct((S, D), F32),) + (jax.ShapeDtypeStruct((S, D), BF16),) * 3,
        in_specs=[_rows(tm, D), _resident((1, D)), _resident((D, D)), _resident((MEM, 2 * D)), _resident((D, D))],
        out_specs=(_rows(tm, D),) * 4,
        compiler_params=_cparams(1),
    )(x1, g2, w_xq, kv, w_xo)


FF_CHUNK = 1408


def _ffn_fwd(x2, tgt, g3, w_ffn_in_t, w_ffn_out, g4, tm=256):
    S = x2.shape[0]

    def body(x_ref, t_ref, g3_ref, wi_ref, wo_ref, g4_ref, dx3_ref, gu_ref, hf_ref, act_ref, loss_ref, dg4_ref):
        i = pl.program_id(0)
        xv = x_ref[...]
        hf = (xv * _rstd(xv) * g3_ref[...]).astype(BF16)
        hf_ref[...] = hf
        acc = xv
        for c in range(DFF // FF_CHUNK):
            cs = slice(FF_CHUNK * c, FF_CHUNK * (c + 1))
            us = slice(DFF + FF_CHUNK * c, DFF + FF_CHUNK * (c + 1))
            gate = _dot_nt(hf, wi_ref[cs, :])
            up = _dot_nt(hf, wi_ref[us, :])
            gu_ref[:, cs] = gate.astype(BF16)
            gu_ref[:, us] = up.astype(BF16)
            act = ((gate * _sigmoid(gate)) * up).astype(BF16)
            act_ref[:, cs] = act
            acc = acc + _dot(act, wo_ref[cs, :])
        r4 = _rstd(acc)
        g4 = g4_ref[...]
        diff = acc * r4 * g4 - t_ref[...]
        dy = diff * (1.0 / D)
        dx3, dg4 = _rms_bwd(dy, acc, r4, g4)
        dx3_ref[...] = dx3
        part = 0.5 * jnp.sum(jnp.mean(diff * diff, axis=-1, keepdims=True))

        @pl.when(i == 0)
        def _():
            loss_ref[...] = jnp.zeros_like(loss_ref)
            dg4_ref[...] = jnp.zeros_like(dg4_ref)

        loss_ref[...] += jnp.full(loss_ref.shape, part, F32)
        dg4_ref[...] += dg4

    return pl.pallas_call(
        body, name="ffn_fwd", grid=(S // tm,),
        out_shape=(jax.ShapeDtypeStruct((S, D), F32), jax.ShapeDtypeStruct((S, 2 * DFF), BF16),
                   jax.ShapeDtypeStruct((S, D), BF16), jax.ShapeDtypeStruct((S, DFF), BF16),
                   jax.ShapeDtypeStruct((8, LANES), F32), jax.ShapeDtypeStruct((1, D), F32)),
        in_specs=[_rows(tm, D), _rows(tm, D), _resident((1, D)), _resident((2 * DFF, D)), _resident((DFF, D)),
                  _resident((1, D))],
        out_specs=(_rows(tm, D), _rows(tm, 2 * DFF), _rows(tm, D), _rows(tm, DFF),
                   pl.BlockSpec((8, LANES), lambda i: (0, 0)), pl.BlockSpec((1, D), lambda i: (0, 0))),
        compiler_params=_cparams(1),
    )(x2, tgt, g3, w_ffn_in_t, w_ffn_out, g4)


def _ffn_bwd(dx3, gu, x2, g3, w_ffn_out, w_ffn_in_t, tm=256):
    S = x2.shape[0]

    def body(d_ref, gu_ref, x_ref, g3_ref, wo_ref, wit_ref, dx2_ref, dgu_ref, dg3_ref):
        i = pl.program_id(0)
        d3 = d_ref[...]
        d3b = d3.astype(BF16)
        for c in range(DFF // FF_CHUNK):
            cs = slice(FF_CHUNK * c, FF_CHUNK * (c + 1))
            us = slice(DFF + FF_CHUNK * c, DFF + FF_CHUNK * (c + 1))
            da = _dot_nt(d3b, wo_ref[cs, :])
            gate = gu_ref[:, cs].astype(F32)
            up = gu_ref[:, us].astype(F32)
            sg = _sigmoid(gate)
            dgate = (da * up * (sg * (1.0 + gate * (1.0 - sg)))).astype(BF16)
            dup = (da * (gate * sg)).astype(BF16)
            dgu_ref[:, cs] = dgate
            dgu_ref[:, us] = dup
        dhf = _dot(dgu_ref[...], wit_ref[...])
        xv = x_ref[...]
        dx, dg3 = _rms_bwd(dhf, xv, _rstd(xv), g3_ref[...])
        dx2_ref[...] = d3 + dx

        @pl.when(i == 0)
        def _():
            dg3_ref[...] = jnp.zeros_like(dg3_ref)

        dg3_ref[...] += dg3

    return pl.pallas_call(
        body, name="ffn_bwd", grid=(S // tm,),
        out_shape=(jax.ShapeDtypeStruct((S, D), F32), jax.ShapeDtypeStruct((S, 2 * DFF), BF16),
                   jax.ShapeDtypeStruct((1, D), F32)),
        in_specs=[_rows(tm, D), _rows(tm, 2 * DFF), _rows(tm, D), _resident((1, D)),
                  _resident((DFF, D)), _resident((2 * DFF, D))],
        out_specs=(_rows(tm, D), _rows(tm, 2 * DFF), pl.BlockSpec((1, D), lambda i: (0, 0))),
        compiler_params=_cparams(1),
    )(dx3, gu, x2, g3, w_ffn_out, w_ffn_in_t)


def _dw(a, b, tmm, tn, ts, name, out_dtype=BF16, plans=()):
    S, M = a.shape
    N = b.shape[1]
    ts = min(ts, S)
    nk = S // ts

    def body(a_ref, b_ref, o_ref, acc_ref):
        k = pl.program_id(2)

        @pl.when(k == 0)
        def _():
            acc_ref[...] = jnp.zeros_like(acc_ref)

        acc_ref[...] += _dot_tn(a_ref[...].astype(BF16), b_ref[...].astype(BF16))

        @pl.when(k == nk - 1)
        def _():
            o_ref[...] = acc_ref[...].astype(out_dtype)

    out = _call(
        body, name=name, grid=(M // tmm, N // tn, nk),
        out_shape=(jax.ShapeDtypeStruct((M, N), out_dtype),),
        in_specs=[pl.BlockSpec((ts, tmm), lambda m, n, k: (k, m)), pl.BlockSpec((ts, tn), lambda m, n, k: (k, n))],
        out_specs=(pl.BlockSpec((tmm, tn), lambda m, n, k: (m, n)),),
        scratch_shapes=[pltpu.VMEM((tmm, tn), F32)],
        operands=(a, b), plans=plans)
    return out if plans else out[0]


def _xattn_bwd(dx2, x1, qx, g2, w_xq, w_xo, kv, plans=(), tm=512):
    S = x1.shape[0]

    def body(d_ref, x_ref, qx_ref, g_ref, wq_ref, wo_ref, kv_ref, dx1_ref, dq_ref, dkv_ref, dg2_ref):
        i = pl.program_id(0)

        @pl.when(i == 0)
        def _():
            dkv_ref[...] = jnp.zeros_like(dkv_ref)
            dg2_ref[...] = jnp.zeros_like(dg2_ref)

        d2 = d_ref[...]
        qx = qx_ref[...]
        do = _dot_nt(d2.astype(BF16), wo_ref[...]).astype(BF16)
        for h in range(XH):
            hs = slice(256 * h, 256 * (h + 1))
            vs = slice(D + 256 * h, D + 256 * (h + 1))
            p = _xatt_head(qx, kv_ref, h)
            dp = _dot_nt(do[:, hs], kv_ref[:, vs])
            ds = (p * (dp - jnp.sum(dp * p, axis=-1, keepdims=True))).astype(BF16)
            dq_ref[:, hs] = (_dot(ds, kv_ref[:, hs]) * (1.0 / 16)).astype(BF16)
            dkv_ref[:, hs] += _dot_tn(ds, qx[:, hs])
            dkv_ref[:, vs] += _dot_tn(p.astype(BF16), do[:, hs])
        dhx = _dot_nt(dq_ref[...], wq_ref[...])
        xv = x_ref[...]
        dx, dg2 = _rms_bwd(dhx, xv, _rstd(xv), g_ref[...])
        dx1_ref[...] = d2 + dx
        dg2_ref[...] += dg2

    return _call(
        body, name="xattn_bwd", grid=(S // tm,),
        out_shape=(jax.ShapeDtypeStruct((S, D), F32), jax.ShapeDtypeStruct((S, D), BF16),
                   jax.ShapeDtypeStruct((MEM, 2 * D), F32), jax.ShapeDtypeStruct((1, D), F32)),
        in_specs=[_rows(tm, D), _rows(tm, D), _rows(tm, D), _resident((1, D)), _resident((D, D)), _resident((D, D)),
                  _resident((MEM, 2 * D))],
        out_specs=(_rows(tm, D), _rows(tm, D),
                   pl.BlockSpec((MEM, 2 * D), lambda i: (0, 0)), pl.BlockSpec((1, D), lambda i: (0, 0))),
        operands=(dx2, x1, qx, g2, w_xq, w_xo, kv), plans=plans)


def _mem_kv_bwd(dkv, mem, g_mem, mn, w_xkv_t):
    def body(dkv_ref, m_ref, g_ref, mn_ref, wt_ref, dw_ref, dg_ref):
        dkvb = dkv_ref[...].astype(BF16)
        dw_ref[...] = _dot_tn(dkvb, mn_ref[...]).astype(BF16)
        dmn = _dot(dkvb, wt_ref[...])
        mv = m_ref[...]
        dg_ref[...] = jnp.sum(dmn * (mv * _rstd(mv)), axis=0, keepdims=True)

    vm = pl.BlockSpec(memory_space=pltpu.VMEM)
    return pl.pallas_call(
        body, name="mem_kv_bwd",
        out_shape=(jax.ShapeDtypeStruct((2 * D, D), BF16), jax.ShapeDtypeStruct((1, D), F32)),
        in_specs=[vm] * 5, out_specs=(vm, vm),
        compiler_params=pltpu.CompilerParams(vmem_limit_bytes=VMEM_LIMIT),
    )(dkv, mem, g_mem, mn, w_xkv_t)


def _merge_bwd(dx1, y_att, y_sg, gates, wba_t, wbs_t, w_out, tm=512):
    S = dx1.shape[0]

    def body(d_ref, ya_ref, ys_ref, g_ref, wbat_ref, wbst_ref, wo_ref,
             mg_ref, da_ref, db_ref, dya_ref, dys_ref, dg_ref):
        dm = _dot_nt(d_ref[...].astype(BF16), wo_ref[...])
        a = _dot_nt(ya_ref[...], wbat_ref[...])
        b = _dot_nt(ys_ref[...], wbst_ref[...])
        sa = _sigmoid(g_ref[:, 0:D].astype(F32))
        sb = _sigmoid(g_ref[:, D:2 * D].astype(F32))
        mg_ref[...] = (sa * a + sb * b).astype(BF16)
        da = (dm * sa).astype(BF16)
        db = (dm * sb).astype(BF16)
        da_ref[...] = da
        db_ref[...] = db
        dg_ref[:, 0:D] = (dm * a * sa * (1.0 - sa)).astype(BF16)
        dg_ref[:, D:2 * D] = (dm * b * sb * (1.0 - sb)).astype(BF16)
        dya_ref[...] = _dot(da, wbat_ref[...]).astype(BF16)
        dys_ref[...] = _dot(db, wbst_ref[...]).astype(BF16)

    return pl.pallas_call(
        body, name="merge_bwd", grid=(S // tm,),
        out_shape=(jax.ShapeDtypeStruct((S, D), BF16), jax.ShapeDtypeStruct((S, D), BF16),
                   jax.ShapeDtypeStruct((S, D), BF16), jax.ShapeDtypeStruct((S, ATT_W), BF16),
                   jax.ShapeDtypeStruct((S, SG_W), BF16), jax.ShapeDtypeStruct((S, 2 * D), BF16)),
        in_specs=[_rows(tm, D), _rows(tm, ATT_W), _rows(tm, SG_W), _rows(tm, 2 * D),
                  _resident((D, ATT_W)), _resident((D, SG_W)), _resident((D, D))],
        out_specs=(_rows(tm, D), _rows(tm, D), _rows(tm, D), _rows(tm, ATT_W), _rows(tm, SG_W), _rows(tm, 2 * D)),
        compiler_params=_cparams(1),
    )(dx1, y_att, y_sg, gates, wba_t, wbs_t, w_out)


def _sgu_bwd(uv, dy_sg, lng, lnb, sg_w, b_exp, plans=(), tm=512):
    S = uv.shape[0]
    n_steps = S // tm

    def body(uv_ref, dy_ref, lng_ref, lnb_ref, w_ref, b_ref, duv_ref, dw_ref, dbx_ref, dlng_ref, dlnb_ref, dvln_ref):
        i = pl.program_id(0)

        @pl.when(i == 0)
        def _():
            dw_ref[...] = jnp.zeros_like(dw_ref)
            dbx_ref[...] = jnp.zeros_like(dbx_ref)
            dlng_ref[...] = jnp.zeros_like(dlng_ref)
            dlnb_ref[...] = jnp.zeros_like(dlnb_ref)

        lane = lax.broadcasted_iota(jnp.int32, (1, LANES), 1)
        lo = lane < 64
        mask = _sg_mask()
        lng = lng_ref[...]
        gu, dgelu_u = _gelu_and_grad(uv_ref[:, 0:SG_W].astype(F32))
        gv, dgelu_v = _gelu_and_grad(uv_ref[:, SG_W:2 * SG_W].astype(F32))
        vln, vhat, rstd = _sg_layernorm(gv, lng, lnb_ref[...])
        dy = dy_ref[...].astype(F32)
        dsv_all = dy * gu
        for gp in range(4):
            wf0 = jnp.where(mask, w_ref[2 * gp], 0)
            wf1 = jnp.where(mask, w_ref[2 * gp + 1], 0)
            w0 = wf0.astype(BF16)
            w1 = wf1.astype(BF16)
            cols = slice(128 * gp, 128 * (gp + 1))
            dw0 = jnp.zeros((128, 128), F32)
            dw1 = jnp.zeros((128, 128), F32)
            dbx = jnp.zeros((128, LANES), F32)
            for n in range(tm // 128):
                rows = slice(128 * n, 128 * (n + 1))
                vl = vln[rows, cols]
                vl0 = jnp.where(lo, vl, 0).astype(BF16)
                vl1 = jnp.where(lo, 0, vl).astype(BF16)
                sv = _dot(w0, vl0) + _dot(w1, vl1) + b_ref[:, cols]
                duv_ref[rows, cols] = (dy[rows, cols] * sv * dgelu_u[rows, cols]).astype(BF16)
                dsv = dsv_all[rows, cols]
                dbx = dbx + dsv
                ds0 = jnp.where(lo, dsv, 0).astype(BF16)
                ds1 = jnp.where(lo, 0, dsv).astype(BF16)
                dw0 = dw0 + _dot_nt(ds0, vl0)
                dw1 = dw1 + _dot_nt(ds1, vl1)
                dvln_ref[rows, cols] = _dot_tn(w0, ds0) + _dot_tn(w1, ds1)
            dw_ref[2 * gp] += jnp.where(mask, dw0, 0)
            dw_ref[2 * gp + 1] += jnp.where(mask, dw1, 0)
            dbx_ref[:, cols] += dbx
        dvln = dvln_ref[...]
        dlng_ref[...] += jnp.sum(dvln * vhat, axis=0, keepdims=True)
        dlnb_ref[...] += jnp.sum(dvln, axis=0, keepdims=True)
        dvh = dvln * lng
        dgv = rstd * (dvh - _group_sum64(dvh) * (1.0 / 64) - vhat * (_group_sum64(dvh * vhat) * (1.0 / 64)))
        duv_ref[:, SG_W:2 * SG_W] = (dgv * dgelu_v).astype(BF16)

        @pl.when(i == n_steps - 1)
        def _():
            dbx_ref[...] = _group_sum64(dbx_ref[...])

    return _call(
        body, name="sgu_bwd", grid=(n_steps,),
        out_shape=(jax.ShapeDtypeStruct((S, 2 * SG_W), BF16), jax.ShapeDtypeStruct((8, 128, 128), F32),
                   jax.ShapeDtypeStruct((128, SG_W), F32), jax.ShapeDtypeStruct((1, SG_W), F32),
                   jax.ShapeDtypeStruct((1, SG_W), F32)),
        in_specs=[_rows(tm, 2 * SG_W), _rows(tm, SG_W), _resident((1, SG_W)), _resident((1, SG_W)),
                  _resident((8, 128, 128)), _resident((128, SG_W))],
        out_specs=(_rows(tm, 2 * SG_W), pl.BlockSpec((8, 128, 128), lambda i: (0, 0, 0)),
                   pl.BlockSpec((128, SG_W), lambda i: (0, 0)), pl.BlockSpec((1, SG_W), lambda i: (0, 0)),
                   pl.BlockSpec((1, SG_W), lambda i: (0, 0))),
        scratch_shapes=[pltpu.VMEM((tm, SG_W), F32)],
        operands=(uv, dy_sg, lng, lnb, sg_w, b_exp), plans=plans)


def _attn_bwd(qkv, dy_att, y_att, lse, bias, plans=()):
    S = qkv.shape[0]
    R = ATT_R

    def body(q_ref, kp_ref, kc_ref, vp_ref, vc_ref, b_ref, dy_ref, y_ref, lse_ref, dq_ref, dk_ref, dv_ref, dss_ref):
        i = pl.program_id(1)

        @pl.when(i == 0)
        def _():
            dk_ref[...] = jnp.zeros_like(dk_ref)
            dv_ref[...] = jnp.zeros_like(dv_ref)
            dss_ref[...] = jnp.zeros_like(dss_ref)

        lane = lax.broadcasted_iota(jnp.int32, (1, LANES), 1)
        kwin = jnp.concatenate([kp_ref[...], kc_ref[...]], axis=0)
        vwin = jnp.concatenate([vp_ref[...], vc_ref[...]], axis=0)
        for sub in range(ATT_SUBS):
            rows = slice(128 * sub, 128 * (sub + 1))
            q2 = q_ref[rows, :]
            do2 = dy_ref[rows, :]
            kw = kwin[128 * sub:128 * sub + BAND_KEYS]
            vw = vwin[128 * sub:128 * sub + BAND_KEYS]
            table = jnp.where(i == 0, 1 + sub, 0)
            dyy = do2.astype(F32) * y_ref[rows, :].astype(F32)
            lse2 = lse_ref[rows, :]
            dq = jnp.zeros((128, LANES), F32)
            dkw = jnp.zeros((BAND_KEYS, LANES), F32)
            dvw = jnp.zeros((BAND_KEYS, LANES), F32)
            for hh in range(2):
                mine = (lane >= 64 * hh) & (lane < 64 * (hh + 1))
                qm = jnp.where(mine, q2, 0)
                dom = jnp.where(mine, do2, 0)
                p = jnp.exp(_dot_nt(qm, kw) + b_ref[table, hh] - lse2[:, 64 * hh:64 * hh + 1])
                dp = _dot_nt(dom, vw)
                ds = p * (dp - jnp.sum(jnp.where(mine, dyy, 0.0), axis=-1, keepdims=True))
                dss_ref[hh] += ds
                dsb = ds.astype(BF16)
                dq = dq + _dot(dsb, jnp.where(mine, kw, 0))
                dkw = dkw + _dot_tn(dsb, qm)
                dvw = dvw + _dot_tn(p.astype(BF16), dom)
            dq_ref[rows, :] = dq.astype(BF16)
            start = pl.multiple_of(i * R + 128 * sub, 128)
            dk_ref[pl.ds(start, BAND_KEYS), :] += dkw
            dv_ref[pl.ds(start, BAND_KEYS), :] += dvw

    acc_spec = pl.BlockSpec((S + 8 * CHUNK, LANES), lambda j, i: (0, j))
    return _call(
        body, name="attn_bwd", grid=(4, S // R),
        out_shape=(jax.ShapeDtypeStruct((S, ATT_W), BF16), jax.ShapeDtypeStruct((S + 8 * CHUNK, ATT_W), F32),
                   jax.ShapeDtypeStruct((S + 8 * CHUNK, ATT_W), F32), jax.ShapeDtypeStruct((8, 128, BAND_KEYS), F32)),
        in_specs=_att_specs() + [pl.BlockSpec((R, LANES), lambda j, i: (i, j))] * 3,
        out_specs=(pl.BlockSpec((R, LANES), lambda j, i: (i, j)), acc_spec, acc_spec,
                   pl.BlockSpec((2, 128, BAND_KEYS), lambda j, i: (j, 0, 0))),
        operands=(qkv, qkv, qkv, qkv, qkv, bias, dy_att, y_att, lse), plans=plans)


def _in_bwd(dq, dk, dv, duv, dgates, x, dx1, g1, w_in_t, plans=(), tm=512):
    S = x.shape[0]
    pad_blocks = (8 * CHUNK) // tm

    def body(dq_ref, dk_ref, dv_ref, duv_ref, dg_ref, x_ref, d1_ref, g_ref, wt_ref, dx_ref, dz_ref, dg1_ref):
        i = pl.program_id(0)
        dz_ref[:, 0:ATT_W] = (dq_ref[...].astype(F32) * 0.125).astype(BF16)
        dz_ref[:, ATT_W:2 * ATT_W] = dk_ref[...].astype(BF16)
        dz_ref[:, 2 * ATT_W:3 * ATT_W] = dv_ref[...].astype(BF16)
        dz_ref[:, 3 * ATT_W:3 * ATT_W + 2 * SG_W] = duv_ref[...]
        dz_ref[:, 3 * ATT_W + 2 * SG_W:IN_COLS] = dg_ref[...]
        dh = _dot(dz_ref[...], wt_ref[...])
        xv = x_ref[...]
        dx, dg1 = _rms_bwd(dh, xv, _rstd(xv), g_ref[...])
        dx_ref[...] = d1_ref[...] + dx

        @pl.when(i == 0)
        def _():
            dg1_ref[...] = jnp.zeros_like(dg1_ref)

        dg1_ref[...] += dg1

    shifted = pl.BlockSpec((tm, ATT_W), lambda i: (i + pad_blocks, 0))
    return _call(
        body, name="in_bwd", grid=(S // tm,),
        out_shape=(jax.ShapeDtypeStruct((S, D), F32), jax.ShapeDtypeStruct((S, IN_COLS), BF16),
                   jax.ShapeDtypeStruct((1, D), F32)),
        in_specs=[_rows(tm, ATT_W), shifted, shifted, _rows(tm, 2 * SG_W), _rows(tm, 2 * D), _rows(tm, D),
                  _rows(tm, D), _resident((1, D)), _resident((IN_COLS, D))],
        out_specs=(_rows(tm, D), _rows(tm, IN_COLS), pl.BlockSpec((1, D), lambda i: (0, 0))),
        operands=(dq, dk, dv, duv, dgates, x, dx1, g1, w_in_t), plans=plans)


def _adam_math(w, g, m, v):
    m = ADAM_B1 * m + (1.0 - ADAM_B1) * g
    v = ADAM_B2 * v + (1.0 - ADAM_B2) * (g * g)
    m_hat = m / (1.0 - ADAM_B1 ** ADAM_STEP)
    v_hat = v / (1.0 - ADAM_B2 ** ADAM_STEP)
    delta = -ADAM_LR * (m_hat / (jnp.sqrt(v_hat) + ADAM_EPS) + ADAM_WD * w)
    return delta, m, v


def _adam(parts, w, m, v, tr, name, transposed, after=None):
    P = parts.shape[0]
    R, C = w.shape

    def body(p_ref, w_ref, m_ref, v_ref, *rest):
        g_ref, d_ref, mo_ref, vo_ref = rest[-4:]
        if transposed:
            eye = (lax.broadcasted_iota(jnp.int32, (C, C), 0) == lax.broadcasted_iota(jnp.int32, (C, C), 1)).astype(BF16)
            part = lambda k: _dot_tn(p_ref[k], eye)
        else:
            part = lambda k: p_ref[k].astype(F32)
        g = part(0)
        for k in range(1, P):
            g = g + part(k)
        g_ref[...] = g
        d_ref[...], mo_ref[...], vo_ref[...] = _adam_math(w_ref[...], g, m_ref[...], v_ref[...])

    row = pl.BlockSpec((tr, C), lambda i: (i, 0))
    p_spec = pl.BlockSpec((P, C, tr), lambda i: (0, 0, i)) if transposed else pl.BlockSpec((P, tr, C), lambda i: (0, i, 0))
    extra = [] if after is None else [after]
    return pl.pallas_call(
        body, name=name, grid=(R // tr,),
        out_shape=tuple(jax.ShapeDtypeStruct((R, C), F32) for _ in range(4)),
        in_specs=[p_spec, row, row, row] + [pl.BlockSpec(memory_space=pl.ANY)] * len(extra),
        out_specs=(row, row, row, row),
        compiler_params=_cparams(1),
    )(parts, w, m, v, *extra)


def _my_place():
    return lax.axis_index("x"), lax.axis_index("y"), lax.axis_index("c")


def _other_chips(x, y):
    return [(1 - x, y), (x, 1 - y), (1 - x, 1 - y)]


class _Plan:
    def __init__(self, arrays, out_shapes, sems, start, finish, forward=None, forward_at=0.7):
        self.arrays, self.out_shapes, self.sems = list(arrays), list(out_shapes), list(sems)
        self.start, self.finish, self.forward, self.forward_at = start, finish, forward, forward_at


def _call(body, *, name, grid, in_specs, out_specs, out_shape, operands, scratch_shapes=(), plans=()):
    n_in, n_out, n_scr = len(operands), len(out_shape), len(scratch_shapes)
    p_in = [a for p in plans for a in p.arrays]
    p_out = [s for p in plans for s in p.out_shapes]
    p_sem = [s for p in plans for s in p.sems]
    steps = math.prod(grid)

    def wrapped(*refs):
        ins, refs = refs[:n_in], refs[n_in:]
        pins, refs = refs[:len(p_in)], refs[len(p_in):]
        outs, refs = refs[:n_out], refs[n_out:]
        pouts, refs = refs[:len(p_out)], refs[len(p_out):]
        scr, psems = refs[:n_scr], refs[n_scr:]
        step = 0
        for axis, size in enumerate(grid):
            step = step * size + pl.program_id(axis)
        bound = []
        for p in plans:
            bound.append((p, pins[:len(p.arrays)], pouts[:len(p.out_shapes)], psems[:len(p.sems)]))
            pins, pouts, psems = pins[len(p.arrays):], pouts[len(p.out_shapes):], psems[len(p.sems):]

        @pl.when(step == 0)
        def _():
            for p, a, b, s in bound:
                p.start(a, b, s)

        body(*ins, *outs, *scr)

        for p, a, b, s in bound:
            if p.forward is not None:
                @pl.when(step == min(int(p.forward_at * steps), steps - 1))
                def _(p=p, a=a, b=b, s=s):
                    p.forward(a, b, s)

        @pl.when(step == steps - 1)
        def _():
            for p, a, b, s in bound:
                p.finish(a, b, s)

    hbm = pl.BlockSpec(memory_space=pl.ANY)
    return pl.pallas_call(
        wrapped if plans else body, name=name, grid=grid,
        out_shape=tuple(out_shape) + tuple(p_out),
        in_specs=list(in_specs) + [hbm] * len(p_in),
        out_specs=tuple(out_specs) + tuple([hbm] * len(p_out)),
        scratch_shapes=list(scratch_shapes) + p_sem,
        compiler_params=_cparams(len(grid)),
    )(*operands, *p_in)


def _run_plan(plan, name):
    n_in, n_out = len(plan.arrays), len(plan.out_shapes)

    def body(*refs):
        a, b, s = refs[:n_in], refs[n_in:n_in + n_out], refs[n_in + n_out:]
        plan.start(a, b, s)
        if plan.forward is not None:
            plan.forward(a, b, s)
        plan.finish(a, b, s)

    hbm = pl.BlockSpec(memory_space=pl.ANY)
    return pl.pallas_call(
        body, name=name, out_shape=tuple(plan.out_shapes),
        in_specs=[hbm] * n_in, out_specs=tuple([hbm] * n_out), scratch_shapes=plan.sems,
    )(*plan.arrays)


def _gather_plan(shards, forward_at=0.7):
    n = len(shards)

    def copies(ins, outs, sems):
        send_sems, recv_sems, local_sems = sems
        x, y, c = _my_place()
        me, sibling = (x, y, c), (x, y, 1 - c)
        chips = _other_chips(x, y)

        def idx(p):
            return 4 * p[0] + 2 * p[1] + p[2]

        def copy(a, k, block, to, src=None):
            return pltpu.make_async_remote_copy(
                src_ref=outs[a].at[idx(block)] if src is None else src, dst_ref=outs[a].at[idx(block)],
                send_sem=send_sems.at[a, k], recv_sem=recv_sems.at[a, k], device_id=to, device_id_type=MESH)

        mine = [pltpu.make_async_copy(ins[a], outs[a].at[idx(me)], local_sems.at[a]) for a in range(n)]
        first = []
        for a in range(n):
            first.append(copy(a, 0, me, sibling, src=ins[a]))
            first += [copy(a, 1 + j, me, (*chip, c), src=ins[a]) for j, chip in enumerate(chips)]
        arrived = [copy(a, 1 + j, (*chip, c), me) for j, chip in enumerate(chips) for a in range(n)]
        passed = [copy(a, 4 + j, (*chip, c), sibling) for j, chip in enumerate(chips) for a in range(n)]
        from_sibling = []
        for a in range(n):
            from_sibling.append(copy(a, 0, sibling, me))
            from_sibling += [copy(a, 4 + j, (*chip, 1 - c), me) for j, chip in enumerate(chips)]
        return mine, first, arrived, passed, from_sibling

    def start(ins, outs, sems):
        mine, first, _, _, _ = copies(ins, outs, sems)
        for cp in mine + first:
            cp.start()

    def forward(ins, outs, sems):
        _, _, arrived, passed, _ = copies(ins, outs, sems)
        for landed, onward in zip(arrived, passed):
            landed.wait_recv()
            onward.start()

    def finish(ins, outs, sems):
        mine, first, _, passed, from_sibling = copies(ins, outs, sems)
        for cp in from_sibling:
            cp.wait_recv()
        for cp in first + passed:
            cp.wait_send()
        for cp in mine:
            cp.wait()

    return _Plan(shards, [jax.ShapeDtypeStruct((N_DEV,) + s.shape, s.dtype) for s in shards],
                 [pltpu.SemaphoreType.DMA((n, 7)), pltpu.SemaphoreType.DMA((n, 7)), pltpu.SemaphoreType.DMA((n,))],
                 start, finish, forward, forward_at)


def _sibling_plan(scatter, whole=()):
    ns = len(scatter)
    n = ns + len(whole)

    def copies(ins, outs, sems):
        send_sems, recv_sems = sems
        x, y, c = _my_place()
        out = []
        for a in range(n):
            for k in range(4 if a < ns else 1):
                src = ins[a].at[2 * k + (1 - c)] if a < ns else ins[a]
                dst = outs[a].at[k] if a < ns else outs[a]
                out.append(pltpu.make_async_remote_copy(
                    src_ref=src, dst_ref=dst, send_sem=send_sems.at[a, k], recv_sem=recv_sems.at[a, k],
                    device_id=(x, y, 1 - c), device_id_type=MESH))
        return out

    def start(ins, outs, sems):
        for cp in copies(ins, outs, sems):
            cp.start()

    def finish(ins, outs, sems):
        for cp in copies(ins, outs, sems):
            cp.wait()

    shapes = [jax.ShapeDtypeStruct((4,) + p.shape[1:], p.dtype) for p in scatter]
    shapes += [jax.ShapeDtypeStruct(p.shape, p.dtype) for p in whole]
    return _Plan(list(scatter) + list(whole), shapes,
                 [pltpu.SemaphoreType.DMA((n, 4)), pltpu.SemaphoreType.DMA((n, 4))], start, finish)


def _pair_sum(mine, theirs, c_idx, name):
    _, R, C = mine.shape

    def body(c_ref, a_ref, b_ref, o_ref):
        o_ref[...] = (a_ref[...].astype(F32) + b_ref[...].astype(F32)).astype(o_ref.dtype)

    grid_spec = pltpu.PrefetchScalarGridSpec(
        num_scalar_prefetch=1, grid=(4,),
        in_specs=[pl.BlockSpec((1, R, C), lambda k, c_ref: (2 * k + c_ref[0], 0, 0)),
                  pl.BlockSpec((1, R, C), lambda k, c_ref: (k, 0, 0))],
        out_specs=pl.BlockSpec((1, R, C), lambda k, c_ref: (k, 0, 0)))
    return pl.pallas_call(
        body, name=name, grid_spec=grid_spec,
        out_shape=jax.ShapeDtypeStruct((4, R, C), mine.dtype),
        compiler_params=_cparams(1),
    )(c_idx, mine, theirs)


def _peers_plan(arrays):
    n = len(arrays)

    def copies(ins, outs, sems):
        send_sems, recv_sems, local_sems = sems
        x, y, c = _my_place()
        me = 4 * x + 2 * y + c
        out = [pltpu.make_async_copy(ins[a], outs[a].at[me], local_sems.at[a]) for a in range(n)]
        for a in range(n):
            for k in range(N_DEV - 1):
                bits = k + 1
                peer = (x ^ (bits >> 2), y ^ ((bits >> 1) & 1), c ^ (bits & 1))
                out.append(pltpu.make_async_remote_copy(
                    src_ref=ins[a], dst_ref=outs[a].at[me], send_sem=send_sems.at[a, k], recv_sem=recv_sems.at[a, k],
                    device_id=peer, device_id_type=MESH))
        return out

    def start(ins, outs, sems):
        for cp in copies(ins, outs, sems):
            cp.start()

    def finish(ins, outs, sems):
        for cp in copies(ins, outs, sems):
            cp.wait()

    return _Plan(list(arrays), [jax.ShapeDtypeStruct((N_DEV,) + a.shape, a.dtype) for a in arrays],
                 [pltpu.SemaphoreType.DMA((n, N_DEV - 1)), pltpu.SemaphoreType.DMA((n, N_DEV - 1)),
                  pltpu.SemaphoreType.DMA((n,))], start, finish)


def _chips_plan(scatter, whole=()):
    ns = len(scatter)
    n = ns + len(whole)

    def copies(ins, outs, sems):
        send_sems, recv_sems, local_sems = sems
        x, y, c = _my_place()
        my_chip = 2 * x + y

        def src(a, k):
            return ins[a].at[k] if a < ns else ins[a]

        local = [pltpu.make_async_copy(src(a, my_chip), outs[a].at[my_chip], local_sems.at[a]) for a in range(n)]
        remote = []
        for a in range(n):
            for j, (px, py) in enumerate(_other_chips(x, y)):
                remote.append(pltpu.make_async_remote_copy(
                    src_ref=src(a, 2 * px + py), dst_ref=outs[a].at[my_chip],
                    send_sem=send_sems.at[a, j], recv_sem=recv_sems.at[a, j],
                    device_id=(px, py, c), device_id_type=MESH))
        return local + remote

    def start(ins, outs, sems):
        for cp in copies(ins, outs, sems):
            cp.start()

    def finish(ins, outs, sems):
        for cp in copies(ins, outs, sems):
            cp.wait()

    shapes = [jax.ShapeDtypeStruct(s.shape, s.dtype) for s in scatter]
    shapes += [jax.ShapeDtypeStruct((4,) + s.shape, s.dtype) for s in whole]
    return _Plan(list(scatter) + list(whole), shapes,
                 [pltpu.SemaphoreType.DMA((n, 3)), pltpu.SemaphoreType.DMA((n, 3)), pltpu.SemaphoreType.DMA((n,))],
                 start, finish)


def _chip_copies(src_ref, land_ref, send_sems, recv_sems):
    x, y, c = _my_place()
    return [pltpu.make_async_remote_copy(
        src_ref=src_ref.at[2 * px + py], dst_ref=land_ref.at[2 * x + y], send_sem=send_sems.at[j],
        recv_sem=recv_sems.at[j], device_id=(px, py, c), device_id_type=MESH)
        for j, (px, py) in enumerate(_other_chips(x, y))]


def _chips_start(src, name):
    def body(src_ref, land_ref, send_sems, recv_sems, src_thru, land_thru, token):
        for cp in _chip_copies(src_ref, land_ref, send_sems, recv_sems):
            cp.start()
        token[...] = jnp.zeros_like(token)

    hbm = pl.BlockSpec(memory_space=pltpu.HBM)
    sem = pl.BlockSpec(memory_space=pltpu.SEMAPHORE)
    return pl.pallas_call(
        body, name=name,
        out_shape=(pltpu.SemaphoreType.DMA((3,)), pltpu.SemaphoreType.DMA((3,)), pltpu.HBM(src.shape, src.dtype),
                   pltpu.HBM(src.shape, src.dtype), jax.ShapeDtypeStruct((8, LANES), F32)),
        in_specs=(hbm, hbm), out_specs=(sem, sem, hbm, hbm, pl.BlockSpec(memory_space=pltpu.VMEM)),
        input_output_aliases={0: 2, 1: 3},
        compiler_params=pltpu.CompilerParams(has_side_effects=pltpu.SideEffectType.DATAFLOW_SIDE_EFFECTING),
    )(pltpu.with_memory_space_constraint(src, pltpu.HBM),
      pltpu.with_memory_space_constraint(jnp.zeros(src.shape, src.dtype), pltpu.HBM))


def _chips_wait(send_sems, recv_sems, src_thru, land_thru, after, name):
    def body(src_ref, land_ref, send_sems, recv_sems, after_ref, src_dead, got_ref):
        for cp in _chip_copies(src_ref, land_ref, send_sems, recv_sems):
            cp.wait_send()
            cp.wait_recv()

    hbm = pl.BlockSpec(memory_space=pltpu.HBM)
    sem = pl.BlockSpec(memory_space=pltpu.SEMAPHORE)
    return pl.pallas_call(
        body, name=name,
        out_shape=(pltpu.HBM(src_thru.shape, src_thru.dtype), pltpu.HBM(land_thru.shape, land_thru.dtype)),
        in_specs=(hbm, hbm, sem, sem, pl.BlockSpec(memory_space=pl.ANY)), out_specs=(hbm, hbm),
        input_output_aliases={0: 0, 1: 1},
        compiler_params=pltpu.CompilerParams(has_side_effects=pltpu.SideEffectType.DATAFLOW_SIDE_EFFECTING),
    )(src_thru, land_thru, send_sems, recv_sems, after)


def _small_update(parts, w, m, v, loss_parts, name):
    n = len(parts)

    def total(ref):
        acc = ref[0]
        for k in range(1, ref.shape[0]):
            acc = acc + ref[k]
        return acc

    def body(*refs):
        p_refs, w_refs, m_refs, v_refs = (refs[i * n:(i + 1) * n] for i in range(4))
        lp_ref = refs[4 * n]
        outs = refs[4 * n + 1:]
        g_refs, d_refs, mo_refs, vo_refs = (outs[i * n:(i + 1) * n] for i in range(4))
        for a in range(n):
            g = total(p_refs[a])
            g_refs[a][...] = g
            d_refs[a][...], mo_refs[a][...], vo_refs[a][...] = _adam_math(w_refs[a][...], g, m_refs[a][...],
                                                                          v_refs[a][...])
        outs[4 * n][...] = total(lp_ref)

    vm = pl.BlockSpec(memory_space=pltpu.VMEM)
    shapes = [jax.ShapeDtypeStruct(t.shape, F32) for _ in range(4) for t in w]
    shapes.append(jax.ShapeDtypeStruct(loss_parts.shape[1:], F32))
    outs = pl.pallas_call(
        body, name=name, out_shape=tuple(shapes),
        in_specs=[vm] * (4 * n + 1), out_specs=tuple([vm] * (4 * n + 1)),
        compiler_params=pltpu.CompilerParams(vmem_limit_bytes=VMEM_LIMIT),
    )(*parts, *w, *m, *v, loss_parts)
    return outs[0:n], outs[n:2 * n], outs[2 * n:3 * n], outs[3 * n:4 * n], outs[4 * n]


BIG = [("w_in", 1), ("w_branch_att", 1), ("w_branch_sg", 1), ("w_out", 0), ("w_xq", 0), ("w_xkv", 1), ("w_xo", 0),
       ("w_ffn_in", 1), ("w_ffn_out", 0)]
SMALL = [("norm_mix_g", (1, D)), ("rel_bias", (8, NREL)), ("sg_ln_g", (8, 64)), ("sg_ln_b", (8, 64)),
         ("sg_w", (8, 128, 128)), ("sg_b", (8, 128)), ("norm_xattn_g", (1, D)), ("norm_mem_g", (1, D)),
         ("norm_ffn_g", (1, D)), ("norm_final_g", (1, D))]
ADAM_ROWS = {"w_in": 192, "w_branch_att": 512, "w_branch_sg": 512, "w_xkv": 1024, "w_ffn_in": 176}


def _full(gathered):
    return gathered.reshape(N_DEV * gathered.shape[1], gathered.shape[2])


def _blocks(grad):
    return grad.reshape(N_DEV, grad.shape[0] // N_DEV, grad.shape[1])


def kernel(x, mem, norm_mix_g, w_in, rel_bias, sg_ln_g, sg_ln_b, sg_w, sg_b, w_branch_att, w_branch_sg, w_out, norm_xattn_g, norm_mem_g, w_xq, w_xkv, w_xo, norm_ffn_g, w_ffn_in, w_ffn_out, norm_final_g, loss_target, m_norm_mix_g, m_w_in, m_rel_bias, m_sg_ln_g, m_sg_ln_b, m_sg_w, m_sg_b, m_w_branch_att, m_w_branch_sg, m_w_out, m_norm_xattn_g, m_norm_mem_g, m_w_xq, m_w_xkv, m_w_xo, m_norm_ffn_g, m_w_ffn_in, m_w_ffn_out, m_norm_final_g, v_norm_mix_g, v_w_in, v_rel_bias, v_sg_ln_g, v_sg_ln_b, v_sg_w, v_sg_b, v_w_branch_att, v_w_branch_sg, v_w_out, v_norm_xattn_g, v_norm_mem_g, v_w_xq, v_w_xkv, v_w_xo, v_norm_ffn_g, v_w_ffn_in, v_w_ffn_out, v_norm_final_g):
    args = dict(locals())
    big_names = [n for n, _ in BIG]
    small_names = [n for n, _ in SMALL]
    S = x.shape[1]

    x, mem, tgt = x.reshape(S, D), mem.reshape(MEM, D), loss_target.reshape(S, D)
    small = {n: args[n].reshape(shape) for n, shape in SMALL}
    g1, g2, g3 = small["norm_mix_g"], small["norm_xattn_g"], small["norm_ffn_g"]
    g_mem, g4 = small["norm_mem_g"], small["norm_final_g"]
    lng = small["sg_ln_g"].reshape(1, SG_W)
    lnb = small["sg_ln_b"].reshape(1, SG_W)
    b_exp = jnp.broadcast_to(small["sg_b"].T[:, :, None], (128, 8, 64)).reshape(128, SG_W)
    rel_pad = jnp.pad(small["rel_bias"], ((0, 0), (0, 384 - NREL)))
    c_idx = lax.axis_index("c").astype(jnp.int32).reshape(1)

    shard = {n: (args[n][0].T if axis == 1 else args[n][0]).astype(BF16) for n, axis in BIG}
    h, w_in_gathered = _norm_in(x, g1, plans=[_gather_plan([shard["w_in"]])])
    w_in_t = _full(w_in_gathered)
    bias = _bias_table(rel_pad)
    mix_names = ["w_branch_att", "w_branch_sg", "w_out", "w_xq", "w_xkv", "w_xo"]
    qkv, uv, gates, *got = _in_proj(h, w_in_t, plans=[_gather_plan([shard[n] for n in mix_names])])
    wba_t, wbs_t, w_out_f, w_xq_f, w_xkv_t, w_xo_f = (_full(g) for g in got)
    y_att, lse, *got = _attn_fwd(qkv, bias, plans=[_gather_plan([shard["w_ffn_in"], shard["w_ffn_out"]])])
    w_ffn_in_t, w_ffn_out_f = (_full(g) for g in got)
    y_sg = _sgu_fwd(uv, lng, lnb, small["sg_w"], b_exp)
    x1 = _merge_fwd(x, y_att, y_sg, gates, wba_t, wbs_t, w_out_f)
    kv, mn = _mem_kv(mem, g_mem, w_xkv_t)
    x2, hx, qx, o_x = _xattn_fwd(x1, g2, w_xq_f, kv, w_xo_f)
    dx3, gu, hf, act, loss_part, dg4 = _ffn_fwd(x2, tgt, g3, w_ffn_in_t, w_ffn_out_f, g4)

    def pair_sums(names, mine, theirs):
        return [_pair_sum(a, b, c_idx, "rs_pair_" + n) for n, a, b in zip(names, mine, theirs)]

    dx2, dgu, dg3 = _ffn_bwd(dx3, gu, x2, g3, w_ffn_out_f, w_ffn_in_t)
    ffn_names = ["w_ffn_out", "w_ffn_in"]
    ffn_mine = [_blocks(_dw(act, dx3, 1408, 1024, DW_TOKENS, "dw_ffn_out")),
                _blocks(_dw(dgu, hf, 1408, 1024, DW_TOKENS, "dw_ffn_in"))]
    dx1, dq_x, dkv, dg2, *ffn_theirs = _xattn_bwd(dx2, x1, qx, g2, w_xq_f, w_xo_f, kv,
                                                  plans=[_sibling_plan(ffn_mine)])
    ffn_chip = pair_sums(ffn_names, ffn_mine, ffn_theirs)
    d_xkv, dg_mem = _mem_kv_bwd(dkv, mem, g_mem, mn, w_xkv_t)
    merged, d_a, d_b, dy_att, dy_sg, dgates = _merge_bwd(dx1, y_att, y_sg, gates, wba_t, wbs_t, w_out_f)
    mid_names = ["w_xo", "w_xq", "w_xkv", "w_out", "w_branch_att", "w_branch_sg"]
    mid_mine = [_blocks(g) for g in (
        _dw(o_x, dx2, 1024, 1024, DW_TOKENS, "dw_xo"), _dw(hx, dq_x, 1024, 1024, DW_TOKENS, "dw_xq"), d_xkv,
        _dw(merged, dx1, 1024, 1024, DW_TOKENS, "dw_out"), _dw(d_a, y_att, 1024, 512, DW_TOKENS, "dw_branch_att"),
        _dw(d_b, y_sg, 1024, 512, DW_TOKENS, "dw_branch_sg"))]
    duv, d_sgw, d_bx, d_lng, d_lnb, *got = _sgu_bwd(uv, dy_sg, lng, lnb, small["sg_w"], b_exp,
                                                    plans=[_chips_plan(ffn_chip), _sibling_plan(mid_mine)])
    ffn_all, mid_theirs = got[:2], got[2:]
    mid_chip = pair_sums(mid_names, mid_mine, mid_theirs)
    dq, dk, dv, ds_sum, *mid_all = _attn_bwd(qkv, dy_att, y_att, lse, bias, plans=[_chips_plan(mid_chip)])
    d_rel = _bias_grad(ds_sum)

    grad_x, dz, dg1 = _in_bwd(dq, dk, dv, duv, dgates, x, dx1, g1, w_in_t)

    gs = {"norm_mix_g": dg1, "rel_bias": d_rel[:, :NREL], "sg_ln_g": d_lng.reshape(8, 64),
          "sg_ln_b": d_lnb.reshape(8, 64), "sg_w": d_sgw, "sg_b": d_bx.reshape(128, 8, 64)[:, :, 0].T,
          "norm_xattn_g": dg2, "norm_mem_g": dg_mem, "norm_ffn_g": dg3, "norm_final_g": dg4}
    d_in, *everyone = _dw(dz, h, 1152, 1024, DW_TOKENS, "dw_in",
                          plans=[_peers_plan([gs[n] for n in small_names] + [loss_part])])

    in_mine = [_blocks(d_in)]
    (in_theirs,) = _run_plan(_sibling_plan(in_mine), "rs_sibling")
    (in_chip,) = pair_sums(["w_in"], in_mine, [in_theirs])
    send_sems, recv_sems, in_chip_thru, landing, token = _chips_start(in_chip, "rs_chips_start")
    all_parts = dict(zip(ffn_names + mid_names, list(ffn_all) + list(mid_all)))

    def adam(n, axis, parts, after):
        wmv = [args[p + n][0] for p in ("", "m_", "v_")]
        if axis == 1 and wmv[0].shape[1] % LANES != 0:
            outs = _adam(parts, *(t.T for t in wmv), ADAM_ROWS[n], "adam_" + n, transposed=False, after=after)
            return [t.T[None] for t in outs]
        tr = ADAM_ROWS[n] if axis == 1 else wmv[0].shape[0]
        return [t[None] for t in _adam(parts, *wmv, tr, "adam_" + n, transposed=(axis == 1), after=after)]

    res = {n: adam(n, axis, all_parts[n], token) for n, axis in BIG if n != "w_in"}
    in_chip, landing = _chips_wait(send_sems, recv_sems, in_chip_thru, landing, res["w_ffn_in"][1], "rs_chips_wait")
    my_chip = 2 * lax.axis_index("x") + lax.axis_index("y")
    own = lax.dynamic_slice_in_dim(in_chip, my_chip, 1, axis=0)
    res["w_in"] = adam("w_in", 1, lax.dynamic_update_slice_in_dim(landing, own, my_chip, axis=0), None)
    small_res = _small_update(
        everyone[:-1], [small[n] for n in small_names],
        [args["m_" + n].reshape(s) for n, s in SMALL], [args["v_" + n].reshape(s) for n, s in SMALL],
        everyone[-1], "adam_small")
    for i, n in enumerate(small_names):
        res[n] = [small_res[k][i].reshape(args[n].shape) for k in range(4)]
    loss = small_res[4][0, 0]

    order = ["norm_mix_g", "w_in", "rel_bias", "sg_ln_g", "sg_ln_b", "sg_w", "sg_b", "w_branch_att", "w_branch_sg",
             "w_out", "norm_xattn_g", "norm_mem_g", "w_xq", "w_xkv", "w_xo", "norm_ffn_g", "w_ffn_in", "w_ffn_out",
             "norm_final_g"]
    outs = [loss, grad_x.reshape(1, S, D)]
    for k in range(4):
        outs += [res[n][k] for n in order]
    return tuple(outs)
```

```python
import math

import jax
import jax.numpy as jnp
from jax import lax
from jax.experimental import pallas as pl
from jax.experimental.pallas import tpu as pltpu

F32 = jnp.float32
BF16 = jnp.bfloat16

D = 1024
ATT_W = 512
SG_W = 512
IN_COLS = 4608
DFF = 2816
MEM = 256
XH = 4
CHUNK = 64
BAND_KEYS = 640
ATT_R = 512
ATT_SUBS = ATT_R // 128
DW_TOKENS = 2048
REL_CLIP = 128
NREL = 2 * REL_CLIP + 1
EPS = 1e-6
NEG = -1e30
N_DEV = 8

ADAM_LR = 0.001
ADAM_B1 = 0.9
ADAM_B2 = 0.999
ADAM_EPS = 1e-08
ADAM_WD = 0.01
ADAM_STEP = 10

LANES = 128
VMEM_LIMIT = 56 * 1024 * 1024
MESH = pl.DeviceIdType.MESH


def _cparams(n_axes):
    return pltpu.CompilerParams(dimension_semantics=("arbitrary",) * n_axes, vmem_limit_bytes=VMEM_LIMIT)


def _resident(shape):
    zeros = (0,) * len(shape)
    return pl.BlockSpec(shape, lambda *_: zeros, pipeline_mode=pl.Buffered(1))


def _rows(tm, cols, col_block=0):
    return pl.BlockSpec((tm, cols), lambda i: (i, col_block))


def _sigmoid(x):
    return pl.reciprocal(1.0 + jnp.exp(-x), approx=True)


_GELU_C = math.sqrt(2.0 / math.pi)


def _gelu(x):
    t = jnp.tanh(_GELU_C * (x + 0.044715 * (x * x * x)))
    return x * (0.5 * (1.0 + t))


def _gelu_and_grad(x):
    x2 = x * x
    t = jnp.tanh(_GELU_C * (x + 0.044715 * (x2 * x)))
    cdf = 0.5 * (1.0 + t)
    dcdf = 0.5 * (1.0 - t * t) * (_GELU_C * (1.0 + 3.0 * 0.044715 * x2))
    return x * cdf, cdf + x * dcdf


def _rstd(x):
    return lax.rsqrt(jnp.mean(x * x, axis=-1, keepdims=True) + EPS)


def _rms_bwd(dh, x, r, g):
    xh = x * r
    dxh = dh * g
    dx = r * (dxh - xh * jnp.mean(dxh * xh, axis=-1, keepdims=True))
    dg = jnp.sum(dh * xh, axis=0, keepdims=True)
    return dx, dg


def _group_sum64(x):
    r = lax.broadcasted_iota(jnp.int32, (LANES, LANES), 0) // 64
    c = lax.broadcasted_iota(jnp.int32, (LANES, LANES), 1) // 64
    same_group = (r == c).astype(BF16)

    def one(v):
        hi = v.astype(BF16)
        rest = v - hi.astype(F32)
        mid = rest.astype(BF16)
        lo = (rest - mid.astype(F32)).astype(BF16)
        return _dot(hi, same_group) + _dot(mid, same_group) + _dot(lo, same_group)

    pieces = [one(x[:, LANES * j:LANES * (j + 1)]) for j in range(x.shape[1] // LANES)]
    return pieces[0] if len(pieces) == 1 else jnp.concatenate(pieces, axis=1)


def _dot(a, b):
    return jnp.dot(a, b, preferred_element_type=F32)


def _dot_nt(a, b):
    return lax.dot_general(a, b, (((1,), (1,)), ((), ())), preferred_element_type=F32)


def _dot_tn(a, b):
    return lax.dot_general(a, b, (((0,), (0,)), ((), ())), preferred_element_type=F32)


DIAGS = 768


def _diag_onehot():
    r_idx = lax.broadcasted_iota(jnp.int32, (384, DIAGS), 0)
    t_idx = lax.broadcasted_iota(jnp.int32, (384, DIAGS), 1)
    dist = (8 * CHUNK + 127) - t_idx
    return (jnp.clip(dist, -REL_CLIP, REL_CLIP) + REL_CLIP == r_idx).astype(F32)


def _shift_rows(x, reverse):
    row = lax.broadcasted_iota(jnp.int32, x.shape, 0)
    for k in range(7):
        amt = (DIAGS - (1 << k)) if reverse else (1 << k)
        x = jnp.where(((row >> k) & 1) == 1, pltpu.roll(x, amt, 1), x)
    return x


N_TABLES = 1 + ATT_SUBS


def _bias_table(rel_bias_pad):
    def body(rb_ref, out_ref):
        per_diag = jnp.dot(rb_ref[...], _diag_onehot(), preferred_element_type=F32,
                           precision=lax.Precision.HIGHEST)
        a = lax.broadcasted_iota(jnp.int32, (128, BAND_KEYS), 0)
        b = lax.broadcasted_iota(jnp.int32, (128, BAND_KEYS), 1)
        band = (b // CHUNK >= a // CHUNK) & (b // CHUNK <= a // CHUNK + 8)
        for h in range(8):
            rows = jnp.broadcast_to(per_diag[h:h + 1, :], (128, DIAGS))
            table = _shift_rows(pltpu.roll(rows, DIAGS - 127, 1), reverse=False)[:, :BAND_KEYS]
            out_ref[0, h] = jnp.where(band, table, NEG)
            for s in range(ATT_SUBS):
                out_ref[1 + s, h] = jnp.where(band & (b >= 8 * CHUNK - 128 * s), table, NEG)

    return pl.pallas_call(
        body, name="bias_table",
        out_shape=jax.ShapeDtypeStruct((N_TABLES, 8, 128, BAND_KEYS), F32),
        in_specs=[pl.BlockSpec(memory_space=pltpu.VMEM)],
        out_specs=pl.BlockSpec(memory_space=pltpu.VMEM),
        compiler_params=pltpu.CompilerParams(vmem_limit_bytes=VMEM_LIMIT),
    )(rel_bias_pad)


def _bias_grad(ds_sum):
    def body(ds_ref, out_ref):
        sums = []
        for h in range(8):
            padded = jnp.concatenate([ds_ref[h], jnp.zeros((128, DIAGS - BAND_KEYS), F32)], axis=1)
            skewed = pltpu.roll(_shift_rows(padded, reverse=True), 127, 1)
            sums.append(jnp.sum(skewed, axis=0, keepdims=True))
        per_diag = jnp.concatenate(sums, axis=0)
        out_ref[...] = lax.dot_general(per_diag, _diag_onehot(), (((1,), (1,)), ((), ())),
                                       preferred_element_type=F32, precision=lax.Precision.HIGHEST)

    return pl.pallas_call(
        body, name="bias_grad",
        out_shape=jax.ShapeDtypeStruct((8, 384), F32),
        in_specs=[pl.BlockSpec(memory_space=pltpu.VMEM)],
        out_specs=pl.BlockSpec(memory_space=pltpu.VMEM),
    )(ds_sum)


def _norm_in(x, g1, plans=(), tm=512):
    S = x.shape[0]

    def body(x_ref, g_ref, h_ref):
        xv = x_ref[...]
        h_ref[...] = (xv * _rstd(xv) * g_ref[...]).astype(BF16)

    return _call(
        body, name="norm_in", grid=(S // tm,),
        out_shape=(jax.ShapeDtypeStruct((S, D), BF16),),
        in_specs=[_rows(tm, D), _resident((1, D))], out_specs=(_rows(tm, D),),
        operands=(x, g1), plans=plans)


def _group_chip(g, my_chip):
    return my_chip ^ jnp.where(g == 0, 0, jnp.where(g == 1, 2, jnp.where(g == 2, 1, 3)))


def _in_proj(h, w_shard, my_chip, plans=(), tm=512):
    S = h.shape[0]
    r, C = w_shard.shape
    n_tiles = S // tm

    def body(chip_ref, h_ref, shard_ref, z_ref, gathered_ref, wbuf, send_sems, recv_sems, local_sems):
        g, i = pl.program_id(0), pl.program_id(1)
        x, y, c = _my_place()
        me, sibling = (x, y, c), (x, y, 1 - c)
        chips = _other_chips(x, y)

        def idx(p):
            return 4 * p[0] + 2 * p[1] + p[2]

        def copy(k, block, to, src=None):
            return pltpu.make_async_remote_copy(
                src_ref=gathered_ref.at[idx(block)] if src is None else src, dst_ref=gathered_ref.at[idx(block)],
                send_sem=send_sems.at[k], recv_sem=recv_sems.at[k], device_id=to, device_id_type=MESH)

        mine = pltpu.make_async_copy(shard_ref, gathered_ref.at[idx(me)], local_sems.at[0])
        first = [copy(0, me, sibling, src=shard_ref)]
        first += [copy(1 + j, me, (*chip, c), src=shard_ref) for j, chip in enumerate(chips)]
        passed = [copy(4 + j, (*chip, c), sibling) for j, chip in enumerate(chips)]

        @pl.when((g == 0) & (i == 0))
        def _():
            mine.start()
            for cp in first:
                cp.start()
            mine.wait()
            copy(0, sibling, me).wait_recv()

        for j, chip in enumerate(chips):
            @pl.when((g == j + 1) & (i == 0))
            def _(j=j, chip=chip):
                copy(1 + j, (*chip, c), me).wait_recv()
                passed[j].start()
                copy(4 + j, (*chip, 1 - c), me).wait_recv()

        chip_now = _group_chip(g, 2 * x + y)

        @pl.when(i == 0)
        def _():
            loads = [pltpu.make_async_copy(gathered_ref.at[2 * chip_now + cc], wbuf.at[pl.ds(r * cc, r)],
                                           local_sems.at[1 + cc]) for cc in range(2)]
            for cp in loads:
                cp.start()
            for cp in loads:
                cp.wait()

        col = lax.broadcasted_iota(jnp.int32, (1, 2 * r), 1)
        scale = jnp.where((chip_now == 0) & (col < ATT_W), 0.125, 1.0)
        z_ref[...] = (_dot_nt(h_ref[...], wbuf[...]) * scale).astype(BF16)

        @pl.when((g == 3) & (i == n_tiles - 1))
        def _():
            for cp in first + passed:
                cp.wait_send()

    hbm = pl.BlockSpec(memory_space=pl.ANY)
    return _call(
        body, name="in_proj", grid=(4, n_tiles),
        out_shape=(jax.ShapeDtypeStruct((S, IN_COLS), BF16), jax.ShapeDtypeStruct((N_DEV, r, C), BF16)),
        in_specs=[pl.BlockSpec((tm, D), lambda g, i, chip: (i, 0)), hbm],
        out_specs=(pl.BlockSpec((tm, 2 * r), lambda g, i, chip: (i, _group_chip(g, chip[0]))), hbm),
        scratch_shapes=[pltpu.VMEM((2 * r, C), BF16), pltpu.SemaphoreType.DMA((7,)), pltpu.SemaphoreType.DMA((7,)),
                        pltpu.SemaphoreType.DMA((3,))],
        operands=(h, w_shard), plans=plans, prefetch=(my_chip,))


def _zcols(tm, width, start):
    return pl.BlockSpec((pl.Element(tm), pl.Element(width)), lambda i: (i * tm, start))


def _two_heads(a, lo):
    return jnp.concatenate([jnp.where(lo, a, 0), jnp.where(lo, 0, a)], axis=0)


def _att_specs():
    R = ATT_R
    q = pl.BlockSpec((R, LANES), lambda j, i: (i, j))
    kp = pl.BlockSpec((R, LANES), lambda j, i: (jnp.maximum(i - 1, 0), 4 + j))
    kc = pl.BlockSpec((R, LANES), lambda j, i: (i, 4 + j))
    vp = pl.BlockSpec((R, LANES), lambda j, i: (jnp.maximum(i - 1, 0), 8 + j))
    vc = pl.BlockSpec((R, LANES), lambda j, i: (i, 8 + j))
    bias = pl.BlockSpec((N_TABLES, 2, 128, BAND_KEYS), lambda j, i: (0, j, 0, 0))
    return [q, kp, kc, vp, vc, bias]


def _attn_fwd(qkv, bias, plans=()):
    S = qkv.shape[0]
    R = ATT_R

    def body(q_ref, kp_ref, kc_ref, vp_ref, vc_ref, b_ref, o_ref, lse_ref):
        i = pl.program_id(1)
        lo = lax.broadcasted_iota(jnp.int32, (1, LANES), 1) < 64
        kwin = jnp.concatenate([kp_ref[...], kc_ref[...]], axis=0)
        vwin = jnp.concatenate([vp_ref[...], vc_ref[...]], axis=0)
        for sub in range(ATT_SUBS):
            q2 = q_ref[128 * sub:128 * (sub + 1), :]
            kw = kwin[128 * sub:128 * sub + BAND_KEYS]
            vw = vwin[128 * sub:128 * sub + BAND_KEYS]
            table = jnp.where(i == 0, 1 + sub, 0)
            s = _dot_nt(_two_heads(q2, lo), kw) + b_ref[table].reshape(256, BAND_KEYS)
            top = jnp.max(s, axis=-1, keepdims=True)
            p = jnp.exp(s - top)
            total = jnp.sum(p, axis=-1, keepdims=True)
            p = (p / total).astype(BF16)
            o = _dot(jnp.concatenate([p[:128], p[128:]], axis=1), _two_heads(vw, lo))
            lse = top + jnp.log(total)
            o_ref[128 * sub:128 * (sub + 1), :] = o.astype(BF16)
            lse_ref[128 * sub:128 * (sub + 1), :] = jnp.where(lo, lse[:128], lse[128:])

    blk = pl.BlockSpec((R, LANES), lambda j, i: (i, j))
    return _call(
        body, name="attn_fwd", grid=(4, S // R),
        out_shape=(jax.ShapeDtypeStruct((S, ATT_W), BF16), jax.ShapeDtypeStruct((S, ATT_W), F32)),
        in_specs=_att_specs(), out_specs=(blk, blk),
        operands=(qkv, qkv, qkv, qkv, qkv, bias), plans=plans)


def _sg_mask():
    t = lax.broadcasted_iota(jnp.int32, (128, 128), 0)
    s = lax.broadcasted_iota(jnp.int32, (128, 128), 1)
    return (s // CHUNK) <= (t // CHUNK)


def _sg_layernorm(gv, lng, lnb):
    mu = _group_sum64(gv) * (1.0 / 64)
    xc = gv - mu
    var = _group_sum64(xc * xc) * (1.0 / 64)
    rstd = lax.rsqrt(var + EPS)
    vhat = xc * rstd
    return vhat * lng + lnb, vhat, rstd


def _sgu_fwd(uv, lng, lnb, sg_w, b_exp, tm=512):
    S = uv.shape[0]

    def body(uv_ref, lng_ref, lnb_ref, w_ref, b_ref, y_ref):
        lane = lax.broadcasted_iota(jnp.int32, (1, LANES), 1)
        lo = lane < 64
        mask = _sg_mask()
        gu = _gelu(uv_ref[:, 0:SG_W].astype(F32))
        vln, _, _ = _sg_layernorm(_gelu(uv_ref[:, SG_W:2 * SG_W].astype(F32)), lng_ref[...], lnb_ref[...])
        for gp in range(4):
            w0 = jnp.where(mask, w_ref[2 * gp], 0).astype(BF16)
            w1 = jnp.where(mask, w_ref[2 * gp + 1], 0).astype(BF16)
            cols = slice(128 * gp, 128 * (gp + 1))
            for n in range(tm // 128):
                rows = slice(128 * n, 128 * (n + 1))
                vl = vln[rows, cols]
                sv = (_dot(w0, jnp.where(lo, vl, 0).astype(BF16)) + _dot(w1, jnp.where(lo, 0, vl).astype(BF16))
                      + b_ref[:, cols])
                y_ref[rows, cols] = (gu[rows, cols] * sv).astype(BF16)

    return pl.pallas_call(
        body, name="sgu_fwd", grid=(S // tm,),
        out_shape=jax.ShapeDtypeStruct((S, SG_W), BF16),
        in_specs=[_zcols(tm, 2 * SG_W, 3 * ATT_W), _resident((1, SG_W)), _resident((1, SG_W)),
                  _resident((8, 128, 128)), _resident((128, SG_W))],
        out_specs=_rows(tm, SG_W),
        compiler_params=_cparams(1),
    )(uv, lng, lnb, sg_w, b_exp)


def _merge_fwd(x, y_att, y_sg, gates, wba_t, wbs_t, w_out, tm=512):
    S = x.shape[0]

    def body(x_ref, ya_ref, ys_ref, g_ref, wba_ref, wbs_ref, wo_ref, x1_ref):
        a = _dot_nt(ya_ref[...], wba_ref[...])
        b = _dot_nt(ys_ref[...], wbs_ref[...])
        merged = _sigmoid(g_ref[:, 0:D].astype(F32)) * a + _sigmoid(g_ref[:, D:2 * D].astype(F32)) * b
        x1_ref[...] = x_ref[...] + _dot(merged.astype(BF16), wo_ref[...])

    return pl.pallas_call(
        body, name="merge_fwd", grid=(S // tm,),
        out_shape=jax.ShapeDtypeStruct((S, D), F32),
        in_specs=[_rows(tm, D), _rows(tm, ATT_W), _rows(tm, SG_W), _zcols(tm, 2 * D, 3 * ATT_W + 2 * SG_W),
                  _resident((D, ATT_W)), _resident((D, SG_W)), _resident((D, D))],
        out_specs=_rows(tm, D),
        compiler_params=_cparams(1),
    )(x, y_att, y_sg, gates, wba_t, wbs_t, w_out)


def _mem_kv(mem, g_mem, w_xkv_t):
    def body(m_ref, g_ref, w_ref, kv_ref, mn_ref):
        mv = m_ref[...]
        mn = (mv * _rstd(mv) * g_ref[...]).astype(BF16)
        mn_ref[...] = mn
        kv_ref[...] = _dot_nt(mn, w_ref[...]).astype(BF16)

    vm = pl.BlockSpec(memory_space=pltpu.VMEM)
    return pl.pallas_call(
        body, name="mem_kv",
        out_shape=(jax.ShapeDtypeStruct((MEM, 2 * D), BF16), jax.ShapeDtypeStruct((MEM, D), BF16)),
        in_specs=[vm, vm, vm], out_specs=(vm, vm),
        compiler_params=pltpu.CompilerParams(vmem_limit_bytes=VMEM_LIMIT),
    )(mem, g_mem, w_xkv_t)


def _xatt_head(qx, kv_ref, h):
    hs = slice(256 * h, 256 * (h + 1))
    s = _dot_nt(qx[:, hs], kv_ref[:, hs])
    p = jnp.exp(s - jnp.max(s, axis=-1, keepdims=True))
    return p / jnp.sum(p, axis=-1, keepdims=True)


def _xattn_fwd(x1, g2, w_xq, kv, w_xo, tm=512):
    S = x1.shape[0]

    def body(x_ref, g_ref, wq_ref, kv_ref, wo_ref, x2_ref, hx_ref, qx_ref, o_ref):
        xv = x_ref[...]
        hx = (xv * _rstd(xv) * g_ref[...]).astype(BF16)
        hx_ref[...] = hx
        qx = (_dot(hx, wq_ref[...]) * (1.0 / 16)).astype(BF16)
        qx_ref[...] = qx
        for h in range(XH):
            p = _xatt_head(qx, kv_ref, h)
            o_ref[:, 256 * h:256 * (h + 1)] = _dot(p.astype(BF16), kv_ref[:, D + 256 * h:D + 256 * (h + 1)]).astype(BF16)
        x2_ref[...] = xv + _dot(o_ref[...], wo_ref[...])

    return pl.pallas_call(
        body, name="xattn_fwd", grid=(S // tm,),
        out_shape=(jax.ShapeDtypeStruct((S, D), F32),) + (jax.ShapeDtypeStruct((S, D), BF16),) * 3,
        in_specs=[_rows(tm, D), _resident((1, D)), _resident((D, D)), _resident((MEM, 2 * D)), _resident((D, D))],
        out_specs=(_rows(tm, D),) * 4,
        compiler_params=_cparams(1),
    )(x1, g2, w_xq, kv, w_xo)


FF_CHUNK = 1408


def _ffn_fwd(x2, tgt, g3, w_ffn_in_t, w_ffn_out, g4, tm=256):
    S = x2.shape[0]

    def body(x_ref, t_ref, g3_ref, wi_ref, wo_ref, g4_ref, dx3_ref, gu_ref, hf_ref, act_ref, loss_ref, dg4_ref):
        i = pl.program_id(0)
        xv = x_ref[...]
        hf = (xv * _rstd(xv) * g3_ref[...]).astype(BF16)
        hf_ref[...] = hf
        acc = xv
        for c in range(DFF // FF_CHUNK):
            cs = slice(FF_CHUNK * c, FF_CHUNK * (c + 1))
            us = slice(DFF + FF_CHUNK * c, DFF + FF_CHUNK * (c + 1))
            gate = _dot_nt(hf, wi_ref[cs, :])
            up = _dot_nt(hf, wi_ref[us, :])
            gu_ref[:, cs] = gate.astype(BF16)
            gu_ref[:, us] = up.astype(BF16)
            act = ((gate * _sigmoid(gate)) * up).astype(BF16)
            act_ref[:, cs] = act
            acc = acc + _dot(act, wo_ref[cs, :])
        r4 = _rstd(acc)
        g4 = g4_ref[...]
        diff = acc * r4 * g4 - t_ref[...]
        dy = diff * (1.0 / D)
        dx3, dg4 = _rms_bwd(dy, acc, r4, g4)
        dx3_ref[...] = dx3
        part = 0.5 * jnp.sum(jnp.mean(diff * diff, axis=-1, keepdims=True))

        @pl.when(i == 0)
        def _():
            loss_ref[...] = jnp.zeros_like(loss_ref)
            dg4_ref[...] = jnp.zeros_like(dg4_ref)

        loss_ref[...] += jnp.full(loss_ref.shape, part, F32)
        dg4_ref[...] += dg4

    return pl.pallas_call(
        body, name="ffn_fwd", grid=(S // tm,),
        out_shape=(jax.ShapeDtypeStruct((S, D), F32), jax.ShapeDtypeStruct((S, 2 * DFF), BF16),
                   jax.ShapeDtypeStruct((S, D), BF16), jax.ShapeDtypeStruct((S, DFF), BF16),
                   jax.ShapeDtypeStruct((8, LANES), F32), jax.ShapeDtypeStruct((1, D), F32)),
        in_specs=[_rows(tm, D), _rows(tm, D), _resident((1, D)), _resident((2 * DFF, D)), _resident((DFF, D)),
                  _resident((1, D))],
        out_specs=(_rows(tm, D), _rows(tm, 2 * DFF), _rows(tm, D), _rows(tm, DFF),
                   pl.BlockSpec((8, LANES), lambda i: (0, 0)), pl.BlockSpec((1, D), lambda i: (0, 0))),
        compiler_params=_cparams(1),
    )(x2, tgt, g3, w_ffn_in_t, w_ffn_out, g4)


def _ffn_bwd(dx3, gu, x2, g3, w_ffn_out, w_ffn_in_t, tm=256):
    S = x2.shape[0]

    def body(d_ref, gu_ref, x_ref, g3_ref, wo_ref, wit_ref, dx2_ref, dgu_ref, dg3_ref):
        i = pl.program_id(0)
        d3 = d_ref[...]
        d3b = d3.astype(BF16)
        for c in range(DFF // FF_CHUNK):
            cs = slice(FF_CHUNK * c, FF_CHUNK * (c + 1))
            us = slice(DFF + FF_CHUNK * c, DFF + FF_CHUNK * (c + 1))
            da = _dot_nt(d3b, wo_ref[cs, :])
            gate = gu_ref[:, cs].astype(F32)
            up = gu_ref[:, us].astype(F32)
            sg = _sigmoid(gate)
            dgate = (da * up * (sg * (1.0 + gate * (1.0 - sg)))).astype(BF16)
            dup = (da * (gate * sg)).astype(BF16)
            dgu_ref[:, cs] = dgate
            dgu_ref[:, us] = dup
        dhf = _dot(dgu_ref[...], wit_ref[...])
        xv = x_ref[...]
        dx, dg3 = _rms_bwd(dhf, xv, _rstd(xv), g3_ref[...])
        dx2_ref[...] = d3 + dx

        @pl.when(i == 0)
        def _():
            dg3_ref[...] = jnp.zeros_like(dg3_ref)

        dg3_ref[...] += dg3

    return pl.pallas_call(
        body, name="ffn_bwd", grid=(S // tm,),
        out_shape=(jax.ShapeDtypeStruct((S, D), F32), jax.ShapeDtypeStruct((S, 2 * DFF), BF16),
                   jax.ShapeDtypeStruct((1, D), F32)),
        in_specs=[_rows(tm, D), _rows(tm, 2 * DFF), _rows(tm, D), _resident((1, D)),
                  _resident((DFF, D)), _resident((2 * DFF, D))],
        out_specs=(_rows(tm, D), _rows(tm, 2 * DFF), pl.BlockSpec((1, D), lambda i: (0, 0))),
        compiler_params=_cparams(1),
    )(dx3, gu, x2, g3, w_ffn_out, w_ffn_in_t)


def _dw(a, b, tmm, tn, ts, name, out_dtype=BF16, plans=()):
    S, M = a.shape
    N = b.shape[1]
    ts = min(ts, S)
    nk = S // ts

    def body(a_ref, b_ref, o_ref, acc_ref):
        k = pl.program_id(2)

        @pl.when(k == 0)
        def _():
            acc_ref[...] = jnp.zeros_like(acc_ref)

        acc_ref[...] += _dot_tn(a_ref[...].astype(BF16), b_ref[...].astype(BF16))

        @pl.when(k == nk - 1)
        def _():
            o_ref[...] = acc_ref[...].astype(out_dtype)

    out = _call(
        body, name=name, grid=(M // tmm, N // tn, nk),
        out_shape=(jax.ShapeDtypeStruct((M, N), out_dtype),),
        in_specs=[pl.BlockSpec((ts, tmm), lambda m, n, k: (k, m)), pl.BlockSpec((ts, tn), lambda m, n, k: (k, n))],
        out_specs=(pl.BlockSpec((tmm, tn), lambda m, n, k: (m, n)),),
        scratch_shapes=[pltpu.VMEM((tmm, tn), F32)],
        operands=(a, b), plans=plans)
    return out if plans else out[0]


def _xattn_bwd(dx2, x1, qx, g2, w_xq, w_xo, kv, plans=(), tm=512):
    S = x1.shape[0]

    def body(d_ref, x_ref, qx_ref, g_ref, wq_ref, wo_ref, kv_ref, dx1_ref, dq_ref, dkv_ref, dg2_ref):
        i = pl.program_id(0)

        @pl.when(i == 0)
        def _():
            dkv_ref[...] = jnp.zeros_like(dkv_ref)
            dg2_ref[...] = jnp.zeros_like(dg2_ref)

        d2 = d_ref[...]
        qx = qx_ref[...]
        do = _dot_nt(d2.astype(BF16), wo_ref[...]).astype(BF16)
        for h in range(XH):
            hs = slice(256 * h, 256 * (h + 1))
            vs = slice(D + 256 * h, D + 256 * (h + 1))
            p = _xatt_head(qx, kv_ref, h)
            dp = _dot_nt(do[:, hs], kv_ref[:, vs])
            ds = (p * (dp - jnp.sum(dp * p, axis=-1, keepdims=True))).astype(BF16)
            dq_ref[:, hs] = (_dot(ds, kv_ref[:, hs]) * (1.0 / 16)).astype(BF16)
            dkv_ref[:, hs] += _dot_tn(ds, qx[:, hs])
            dkv_ref[:, vs] += _dot_tn(p.astype(BF16), do[:, hs])
        dhx = _dot_nt(dq_ref[...], wq_ref[...])
        xv = x_ref[...]
        dx, dg2 = _rms_bwd(dhx, xv, _rstd(xv), g_ref[...])
        dx1_ref[...] = d2 + dx
        dg2_ref[...] += dg2

    return _call(
        body, name="xattn_bwd", grid=(S // tm,),
        out_shape=(jax.ShapeDtypeStruct((S, D), F32), jax.ShapeDtypeStruct((S, D), BF16),
                   jax.ShapeDtypeStruct((MEM, 2 * D), F32), jax.ShapeDtypeStruct((1, D), F32)),
        in_specs=[_rows(tm, D), _rows(tm, D), _rows(tm, D), _resident((1, D)), _resident((D, D)), _resident((D, D)),
                  _resident((MEM, 2 * D))],
        out_specs=(_rows(tm, D), _rows(tm, D),
                   pl.BlockSpec((MEM, 2 * D), lambda i: (0, 0)), pl.BlockSpec((1, D), lambda i: (0, 0))),
        operands=(dx2, x1, qx, g2, w_xq, w_xo, kv), plans=plans)


def _mem_kv_bwd(dkv, mem, g_mem, mn, w_xkv_t):
    def body(dkv_ref, m_ref, g_ref, mn_ref, wt_ref, dw_ref, dg_ref):
        dkvb = dkv_ref[...].astype(BF16)
        dw_ref[...] = _dot_tn(dkvb, mn_ref[...]).astype(BF16)
        dmn = _dot(dkvb, wt_ref[...])
        mv = m_ref[...]
        dg_ref[...] = jnp.sum(dmn * (mv * _rstd(mv)), axis=0, keepdims=True)

    vm = pl.BlockSpec(memory_space=pltpu.VMEM)
    return pl.pallas_call(
        body, name="mem_kv_bwd",
        out_shape=(jax.ShapeDtypeStruct((2 * D, D), BF16), jax.ShapeDtypeStruct((1, D), F32)),
        in_specs=[vm] * 5, out_specs=(vm, vm),
        compiler_params=pltpu.CompilerParams(vmem_limit_bytes=VMEM_LIMIT),
    )(dkv, mem, g_mem, mn, w_xkv_t)


def _merge_bwd(dx1, y_att, y_sg, gates, wba_t, wbs_t, w_out, tm=512):
    S = dx1.shape[0]

    def body(d_ref, ya_ref, ys_ref, g_ref, wbat_ref, wbst_ref, wo_ref,
             mg_ref, da_ref, db_ref, dya_ref, dys_ref, dg_ref):
        dm = _dot_nt(d_ref[...].astype(BF16), wo_ref[...])
        a = _dot_nt(ya_ref[...], wbat_ref[...])
        b = _dot_nt(ys_ref[...], wbst_ref[...])
        sa = _sigmoid(g_ref[:, 0:D].astype(F32))
        sb = _sigmoid(g_ref[:, D:2 * D].astype(F32))
        mg_ref[...] = (sa * a + sb * b).astype(BF16)
        da = (dm * sa).astype(BF16)
        db = (dm * sb).astype(BF16)
        da_ref[...] = da
        db_ref[...] = db
        dg_ref[:, 0:D] = (dm * a * sa * (1.0 - sa)).astype(BF16)
        dg_ref[:, D:2 * D] = (dm * b * sb * (1.0 - sb)).astype(BF16)
        dya_ref[...] = _dot(da, wbat_ref[...]).astype(BF16)
        dys_ref[...] = _dot(db, wbst_ref[...]).astype(BF16)

    return pl.pallas_call(
        body, name="merge_bwd", grid=(S // tm,),
        out_shape=(jax.ShapeDtypeStruct((S, D), BF16), jax.ShapeDtypeStruct((S, D), BF16),
                   jax.ShapeDtypeStruct((S, D), BF16), jax.ShapeDtypeStruct((S, ATT_W), BF16),
                   jax.ShapeDtypeStruct((S, SG_W), BF16), jax.ShapeDtypeStruct((S, 2 * D), BF16)),
        in_specs=[_rows(tm, D), _rows(tm, ATT_W), _rows(tm, SG_W), _zcols(tm, 2 * D, 3 * ATT_W + 2 * SG_W),
                  _resident((D, ATT_W)), _resident((D, SG_W)), _resident((D, D))],
        out_specs=(_rows(tm, D), _rows(tm, D), _rows(tm, D), _rows(tm, ATT_W), _rows(tm, SG_W), _rows(tm, 2 * D)),
        compiler_params=_cparams(1),
    )(dx1, y_att, y_sg, gates, wba_t, wbs_t, w_out)


def _sgu_bwd(uv, dy_sg, lng, lnb, sg_w, b_exp, plans=(), tm=512):
    S = uv.shape[0]
    n_steps = S // tm

    def body(uv_ref, dy_ref, lng_ref, lnb_ref, w_ref, b_ref, duv_ref, dw_ref, dbx_ref, dlng_ref, dlnb_ref, dvln_ref):
        i = pl.program_id(0)

        @pl.when(i == 0)
        def _():
            dw_ref[...] = jnp.zeros_like(dw_ref)
            dbx_ref[...] = jnp.zeros_like(dbx_ref)
            dlng_ref[...] = jnp.zeros_like(dlng_ref)
            dlnb_ref[...] = jnp.zeros_like(dlnb_ref)

        lane = lax.broadcasted_iota(jnp.int32, (1, LANES), 1)
        lo = lane < 64
        mask = _sg_mask()
        lng = lng_ref[...]
        gu, dgelu_u = _gelu_and_grad(uv_ref[:, 0:SG_W].astype(F32))
        gv, dgelu_v = _gelu_and_grad(uv_ref[:, SG_W:2 * SG_W].astype(F32))
        vln, vhat, rstd = _sg_layernorm(gv, lng, lnb_ref[...])
        dy = dy_ref[...].astype(F32)
        dsv_all = dy * gu
        for gp in range(4):
            wf0 = jnp.where(mask, w_ref[2 * gp], 0)
            wf1 = jnp.where(mask, w_ref[2 * gp + 1], 0)
            w0 = wf0.astype(BF16)
            w1 = wf1.astype(BF16)
            cols = slice(128 * gp, 128 * (gp + 1))
            dw0 = jnp.zeros((128, 128), F32)
            dw1 = jnp.zeros((128, 128), F32)
            dbx = jnp.zeros((128, LANES), F32)
            for n in range(tm // 128):
                rows = slice(128 * n, 128 * (n + 1))
                vl = vln[rows, cols]
                vl0 = jnp.where(lo, vl, 0).astype(BF16)
                vl1 = jnp.where(lo, 0, vl).astype(BF16)
                sv = _dot(w0, vl0) + _dot(w1, vl1) + b_ref[:, cols]
                duv_ref[rows, cols] = (dy[rows, cols] * sv * dgelu_u[rows, cols]).astype(BF16)
                dsv = dsv_all[rows, cols]
                dbx = dbx + dsv
                ds0 = jnp.where(lo, dsv, 0).astype(BF16)
                ds1 = jnp.where(lo, 0, dsv).astype(BF16)
                dw0 = dw0 + _dot_nt(ds0, vl0)
                dw1 = dw1 + _dot_nt(ds1, vl1)
                dvln_ref[rows, cols] = _dot_tn(w0, ds0) + _dot_tn(w1, ds1)
            dw_ref[2 * gp] += jnp.where(mask, dw0, 0)
            dw_ref[2 * gp + 1] += jnp.where(mask, dw1, 0)
            dbx_ref[:, cols] += dbx
        dvln = dvln_ref[...]
        dlng_ref[...] += jnp.sum(dvln * vhat, axis=0, keepdims=True)
        dlnb_ref[...] += jnp.sum(dvln, axis=0, keepdims=True)
        dvh = dvln * lng
        dgv = rstd * (dvh - _group_sum64(dvh) * (1.0 / 64) - vhat * (_group_sum64(dvh * vhat) * (1.0 / 64)))
        duv_ref[:, SG_W:2 * SG_W] = (dgv * dgelu_v).astype(BF16)

        @pl.when(i == n_steps - 1)
        def _():
            dbx_ref[...] = _group_sum64(dbx_ref[...])

    return _call(
        body, name="sgu_bwd", grid=(n_steps,),
        out_shape=(jax.ShapeDtypeStruct((S, 2 * SG_W), BF16), jax.ShapeDtypeStruct((8, 128, 128), F32),
                   jax.ShapeDtypeStruct((128, SG_W), F32), jax.ShapeDtypeStruct((1, SG_W), F32),
                   jax.ShapeDtypeStruct((1, SG_W), F32)),
        in_specs=[_zcols(tm, 2 * SG_W, 3 * ATT_W), _rows(tm, SG_W), _resident((1, SG_W)), _resident((1, SG_W)),
                  _resident((8, 128, 128)), _resident((128, SG_W))],
        out_specs=(_rows(tm, 2 * SG_W), pl.BlockSpec((8, 128, 128), lambda i: (0, 0, 0)),
                   pl.BlockSpec((128, SG_W), lambda i: (0, 0)), pl.BlockSpec((1, SG_W), lambda i: (0, 0)),
                   pl.BlockSpec((1, SG_W), lambda i: (0, 0))),
        scratch_shapes=[pltpu.VMEM((tm, SG_W), F32)],
        operands=(uv, dy_sg, lng, lnb, sg_w, b_exp), plans=plans)


def _attn_bwd(qkv, dy_att, y_att, lse, bias, plans=()):
    S = qkv.shape[0]
    R = ATT_R

    def body(q_ref, kp_ref, kc_ref, vp_ref, vc_ref, b_ref, dy_ref, y_ref, lse_ref, dq_ref, dk_ref, dv_ref, dss_ref):
        i = pl.program_id(1)

        @pl.when(i == 0)
        def _():
            dk_ref[...] = jnp.zeros_like(dk_ref)
            dv_ref[...] = jnp.zeros_like(dv_ref)
            dss_ref[...] = jnp.zeros_like(dss_ref)

        lane = lax.broadcasted_iota(jnp.int32, (1, LANES), 1)
        kwin = jnp.concatenate([kp_ref[...], kc_ref[...]], axis=0)
        vwin = jnp.concatenate([vp_ref[...], vc_ref[...]], axis=0)
        for sub in range(ATT_SUBS):
            rows = slice(128 * sub, 128 * (sub + 1))
            q2 = q_ref[rows, :]
            do2 = dy_ref[rows, :]
            kw = kwin[128 * sub:128 * sub + BAND_KEYS]
            vw = vwin[128 * sub:128 * sub + BAND_KEYS]
            table = jnp.where(i == 0, 1 + sub, 0)
            dyy = do2.astype(F32) * y_ref[rows, :].astype(F32)
            lse2 = lse_ref[rows, :]
            dq = jnp.zeros((128, LANES), F32)
            dkw = jnp.zeros((BAND_KEYS, LANES), F32)
            dvw = jnp.zeros((BAND_KEYS, LANES), F32)
            for hh in range(2):
                mine = (lane >= 64 * hh) & (lane < 64 * (hh + 1))
                qm = jnp.where(mine, q2, 0)
                dom = jnp.where(mine, do2, 0)
                p = jnp.exp(_dot_nt(qm, kw) + b_ref[table, hh] - lse2[:, 64 * hh:64 * hh + 1])
                dp = _dot_nt(dom, vw)
                ds = p * (dp - jnp.sum(jnp.where(mine, dyy, 0.0), axis=-1, keepdims=True))
                dss_ref[hh] += ds
                dsb = ds.astype(BF16)
                dq = dq + _dot(dsb, jnp.where(mine, kw, 0))
                dkw = dkw + _dot_tn(dsb, qm)
                dvw = dvw + _dot_tn(p.astype(BF16), dom)
            dq_ref[rows, :] = dq.astype(BF16)
            start = pl.multiple_of(i * R + 128 * sub, 128)
            dk_ref[pl.ds(start, BAND_KEYS), :] += dkw
            dv_ref[pl.ds(start, BAND_KEYS), :] += dvw

    acc_spec = pl.BlockSpec((S + 8 * CHUNK, LANES), lambda j, i: (0, j))
    return _call(
        body, name="attn_bwd", grid=(4, S // R),
        out_shape=(jax.ShapeDtypeStruct((S, ATT_W), BF16), jax.ShapeDtypeStruct((S + 8 * CHUNK, ATT_W), F32),
                   jax.ShapeDtypeStruct((S + 8 * CHUNK, ATT_W), F32), jax.ShapeDtypeStruct((8, 128, BAND_KEYS), F32)),
        in_specs=_att_specs() + [pl.BlockSpec((R, LANES), lambda j, i: (i, j))] * 3,
        out_specs=(pl.BlockSpec((R, LANES), lambda j, i: (i, j)), acc_spec, acc_spec,
                   pl.BlockSpec((2, 128, BAND_KEYS), lambda j, i: (j, 0, 0))),
        operands=(qkv, qkv, qkv, qkv, qkv, bias, dy_att, y_att, lse), plans=plans)


def _in_bwd(dq, dk, dv, duv, dgates, x, dx1, g1, w_in_t, plans=(), tm=512):
    S = x.shape[0]
    pad_blocks = (8 * CHUNK) // tm

    def body(dq_ref, dk_ref, dv_ref, duv_ref, dg_ref, x_ref, d1_ref, g_ref, wt_ref, dx_ref, dz_ref, dg1_ref):
        i = pl.program_id(0)
        dz_ref[:, 0:ATT_W] = (dq_ref[...].astype(F32) * 0.125).astype(BF16)
        dz_ref[:, ATT_W:2 * ATT_W] = dk_ref[...].astype(BF16)
        dz_ref[:, 2 * ATT_W:3 * ATT_W] = dv_ref[...].astype(BF16)
        dz_ref[:, 3 * ATT_W:3 * ATT_W + 2 * SG_W] = duv_ref[...]
        dz_ref[:, 3 * ATT_W + 2 * SG_W:IN_COLS] = dg_ref[...]
        dh = _dot(dz_ref[...], wt_ref[...])
        xv = x_ref[...]
        dx, dg1 = _rms_bwd(dh, xv, _rstd(xv), g_ref[...])
        dx_ref[...] = d1_ref[...] + dx

        @pl.when(i == 0)
        def _():
            dg1_ref[...] = jnp.zeros_like(dg1_ref)

        dg1_ref[...] += dg1

    shifted = pl.BlockSpec((tm, ATT_W), lambda i: (i + pad_blocks, 0))
    return _call(
        body, name="in_bwd", grid=(S // tm,),
        out_shape=(jax.ShapeDtypeStruct((S, D), F32), jax.ShapeDtypeStruct((S, IN_COLS), BF16),
                   jax.ShapeDtypeStruct((1, D), F32)),
        in_specs=[_rows(tm, ATT_W), shifted, shifted, _rows(tm, 2 * SG_W), _rows(tm, 2 * D), _rows(tm, D),
                  _rows(tm, D), _resident((1, D)), _resident((IN_COLS, D))],
        out_specs=(_rows(tm, D), _rows(tm, IN_COLS), pl.BlockSpec((1, D), lambda i: (0, 0))),
        operands=(dq, dk, dv, duv, dgates, x, dx1, g1, w_in_t), plans=plans)


def _adam_math(w, g, m, v):
    m = ADAM_B1 * m + (1.0 - ADAM_B1) * g
    v = ADAM_B2 * v + (1.0 - ADAM_B2) * (g * g)
    m_hat = m / (1.0 - ADAM_B1 ** ADAM_STEP)
    v_hat = v / (1.0 - ADAM_B2 ** ADAM_STEP)
    delta = -ADAM_LR * (m_hat / (jnp.sqrt(v_hat) + ADAM_EPS) + ADAM_WD * w)
    return delta, m, v


def _adam(parts, w, m, v, tr, name, transposed, after=None):
    P = parts.shape[0]
    R, C = w.shape

    def body(p_ref, w_ref, m_ref, v_ref, *rest):
        g_ref, d_ref, mo_ref, vo_ref = rest[-4:]
        if transposed:
            eye = (lax.broadcasted_iota(jnp.int32, (C, C), 0) == lax.broadcasted_iota(jnp.int32, (C, C), 1)).astype(BF16)
            part = lambda k: _dot_tn(p_ref[k], eye)
        else:
            part = lambda k: p_ref[k].astype(F32)
        g = part(0)
        for k in range(1, P):
            g = g + part(k)
        g_ref[...] = g
        d_ref[...], mo_ref[...], vo_ref[...] = _adam_math(w_ref[...], g, m_ref[...], v_ref[...])

    row = pl.BlockSpec((tr, C), lambda i: (i, 0))
    p_spec = pl.BlockSpec((P, C, tr), lambda i: (0, 0, i)) if transposed else pl.BlockSpec((P, tr, C), lambda i: (0, i, 0))
    extra = [] if after is None else [after]
    return pl.pallas_call(
        body, name=name, grid=(R // tr,),
        out_shape=tuple(jax.ShapeDtypeStruct((R, C), F32) for _ in range(4)),
        in_specs=[p_spec, row, row, row] + [pl.BlockSpec(memory_space=pl.ANY)] * len(extra),
        out_specs=(row, row, row, row),
        compiler_params=_cparams(1),
    )(parts, w, m, v, *extra)


def _my_place():
    return lax.axis_index("x"), lax.axis_index("y"), lax.axis_index("c")


def _other_chips(x, y):
    return [(1 - x, y), (x, 1 - y), (1 - x, 1 - y)]


class _Plan:
    def __init__(self, arrays, out_shapes, sems, start, finish, forward=None, forward_at=0.7):
        self.arrays, self.out_shapes, self.sems = list(arrays), list(out_shapes), list(sems)
        self.start, self.finish, self.forward, self.forward_at = start, finish, forward, forward_at


def _call(body, *, name, grid, in_specs, out_specs, out_shape, operands, scratch_shapes=(), plans=(), prefetch=()):
    n_in, n_out, n_scr, n_pre = len(operands), len(out_shape), len(scratch_shapes), len(prefetch)
    p_in = [a for p in plans for a in p.arrays]
    p_out = [s for p in plans for s in p.out_shapes]
    p_sem = [s for p in plans for s in p.sems]
    steps = math.prod(grid)

    def wrapped(*refs):
        pre, refs = refs[:n_pre], refs[n_pre:]
        ins, refs = refs[:n_in], refs[n_in:]
        pins, refs = refs[:len(p_in)], refs[len(p_in):]
        outs, refs = refs[:n_out], refs[n_out:]
        pouts, refs = refs[:len(p_out)], refs[len(p_out):]
        scr, psems = refs[:n_scr], refs[n_scr:]
        step = 0
        for axis, size in enumerate(grid):
            step = step * size + pl.program_id(axis)
        bound = []
        for p in plans:
            bound.append((p, pins[:len(p.arrays)], pouts[:len(p.out_shapes)], psems[:len(p.sems)]))
            pins, pouts, psems = pins[len(p.arrays):], pouts[len(p.out_shapes):], psems[len(p.sems):]

        body(*pre, *ins, *outs, *scr)

        @pl.when(step == 0)
        def _():
            for p, a, b, s in bound:
                p.start(a, b, s)

        for p, a, b, s in bound:
            if p.forward is not None:
                @pl.when(step == min(int(p.forward_at * steps), steps - 1))
                def _(p=p, a=a, b=b, s=s):
                    p.forward(a, b, s)

        @pl.when(step == steps - 1)
        def _():
            for p, a, b, s in bound:
                p.finish(a, b, s)

    hbm = pl.BlockSpec(memory_space=pl.ANY)
    specs = dict(in_specs=list(in_specs) + [hbm] * len(p_in), out_specs=tuple(out_specs) + tuple([hbm] * len(p_out)),
                 scratch_shapes=list(scratch_shapes) + p_sem)
    if n_pre:
        specs = dict(grid_spec=pltpu.PrefetchScalarGridSpec(num_scalar_prefetch=n_pre, grid=grid, **specs))
    else:
        specs["grid"] = grid
    return pl.pallas_call(
        wrapped if plans else body, name=name, out_shape=tuple(out_shape) + tuple(p_out),
        compiler_params=_cparams(len(grid)), **specs,
    )(*prefetch, *operands, *p_in)


def _run_plan(plan, name):
    n_in, n_out = len(plan.arrays), len(plan.out_shapes)

    def body(*refs):
        a, b, s = refs[:n_in], refs[n_in:n_in + n_out], refs[n_in + n_out:]
        plan.start(a, b, s)
        if plan.forward is not None:
            plan.forward(a, b, s)
        plan.finish(a, b, s)

    hbm = pl.BlockSpec(memory_space=pl.ANY)
    return pl.pallas_call(
        body, name=name, out_shape=tuple(plan.out_shapes),
        in_specs=[hbm] * n_in, out_specs=tuple([hbm] * n_out), scratch_shapes=plan.sems,
    )(*plan.arrays)


def _gather_plan(shards, forward_at=0.7):
    n = len(shards)

    def copies(ins, outs, sems):
        send_sems, recv_sems, local_sems = sems
        x, y, c = _my_place()
        me, sibling = (x, y, c), (x, y, 1 - c)
        chips = _other_chips(x, y)

        def idx(p):
            return 4 * p[0] + 2 * p[1] + p[2]

        def copy(a, k, block, to, src=None):
            return pltpu.make_async_remote_copy(
                src_ref=outs[a].at[idx(block)] if src is None else src, dst_ref=outs[a].at[idx(block)],
                send_sem=send_sems.at[a, k], recv_sem=recv_sems.at[a, k], device_id=to, device_id_type=MESH)

        mine = [pltpu.make_async_copy(ins[a], outs[a].at[idx(me)], local_sems.at[a]) for a in range(n)]
        first = []
        for a in range(n):
            first.append(copy(a, 0, me, sibling, src=ins[a]))
            first += [copy(a, 1 + j, me, (*chip, c), src=ins[a]) for j, chip in enumerate(chips)]
        arrived = [copy(a, 1 + j, (*chip, c), me) for j, chip in enumerate(chips) for a in range(n)]
        passed = [copy(a, 4 + j, (*chip, c), sibling) for j, chip in enumerate(chips) for a in range(n)]
        from_sibling = []
        for a in range(n):
            from_sibling.append(copy(a, 0, sibling, me))
            from_sibling += [copy(a, 4 + j, (*chip, 1 - c), me) for j, chip in enumerate(chips)]
        return mine, first, arrived, passed, from_sibling

    def start(ins, outs, sems):
        mine, first, _, _, _ = copies(ins, outs, sems)
        for cp in mine + first:
            cp.start()

    def forward(ins, outs, sems):
        _, _, arrived, passed, _ = copies(ins, outs, sems)
        for landed, onward in zip(arrived, passed):
            landed.wait_recv()
            onward.start()

    def finish(ins, outs, sems):
        mine, first, _, passed, from_sibling = copies(ins, outs, sems)
        for cp in from_sibling:
            cp.wait_recv()
        for cp in first + passed:
            cp.wait_send()
        for cp in mine:
            cp.wait()

    return _Plan(shards, [jax.ShapeDtypeStruct((N_DEV,) + s.shape, s.dtype) for s in shards],
                 [pltpu.SemaphoreType.DMA((n, 7)), pltpu.SemaphoreType.DMA((n, 7)), pltpu.SemaphoreType.DMA((n,))],
                 start, finish, forward, forward_at)


def _sibling_plan(scatter, whole=()):
    ns = len(scatter)
    n = ns + len(whole)

    def copies(ins, outs, sems):
        send_sems, recv_sems = sems
        x, y, c = _my_place()
        out = []
        for a in range(n):
            for k in range(4 if a < ns else 1):
                src = ins[a].at[2 * k + (1 - c)] if a < ns else ins[a]
                dst = outs[a].at[k] if a < ns else outs[a]
                out.append(pltpu.make_async_remote_copy(
                    src_ref=src, dst_ref=dst, send_sem=send_sems.at[a, k], recv_sem=recv_sems.at[a, k],
                    device_id=(x, y, 1 - c), device_id_type=MESH))
        return out

    def start(ins, outs, sems):
        for cp in copies(ins, outs, sems):
            cp.start()

    def finish(ins, outs, sems):
        for cp in copies(ins, outs, sems):
            cp.wait()

    shapes = [jax.ShapeDtypeStruct((4,) + p.shape[1:], p.dtype) for p in scatter]
    shapes += [jax.ShapeDtypeStruct(p.shape, p.dtype) for p in whole]
    return _Plan(list(scatter) + list(whole), shapes,
                 [pltpu.SemaphoreType.DMA((n, 4)), pltpu.SemaphoreType.DMA((n, 4))], start, finish)


def _pair_sum(mine, theirs, c_idx, name):
    _, R, C = mine.shape

    def body(c_ref, a_ref, b_ref, o_ref):
        o_ref[...] = (a_ref[...].astype(F32) + b_ref[...].astype(F32)).astype(o_ref.dtype)

    grid_spec = pltpu.PrefetchScalarGridSpec(
        num_scalar_prefetch=1, grid=(4,),
        in_specs=[pl.BlockSpec((1, R, C), lambda k, c_ref: (2 * k + c_ref[0], 0, 0)),
                  pl.BlockSpec((1, R, C), lambda k, c_ref: (k, 0, 0))],
        out_specs=pl.BlockSpec((1, R, C), lambda k, c_ref: (k, 0, 0)))
    return pl.pallas_call(
        body, name=name, grid_spec=grid_spec,
        out_shape=jax.ShapeDtypeStruct((4, R, C), mine.dtype),
        compiler_params=_cparams(1),
    )(c_idx, mine, theirs)


def _peers_plan(arrays):
    n = len(arrays)

    def copies(ins, outs, sems):
        send_sems, recv_sems, local_sems = sems
        x, y, c = _my_place()
        me = 4 * x + 2 * y + c
        out = [pltpu.make_async_copy(ins[a], outs[a].at[me], local_sems.at[a]) for a in range(n)]
        for a in range(n):
            for k in range(N_DEV - 1):
                bits = k + 1
                peer = (x ^ (bits >> 2), y ^ ((bits >> 1) & 1), c ^ (bits & 1))
                out.append(pltpu.make_async_remote_copy(
                    src_ref=ins[a], dst_ref=outs[a].at[me], send_sem=send_sems.at[a, k], recv_sem=recv_sems.at[a, k],
                    device_id=peer, device_id_type=MESH))
        return out

    def start(ins, outs, sems):
        for cp in copies(ins, outs, sems):
            cp.start()

    def finish(ins, outs, sems):
        for cp in copies(ins, outs, sems):
            cp.wait()

    return _Plan(list(arrays), [jax.ShapeDtypeStruct((N_DEV,) + a.shape, a.dtype) for a in arrays],
                 [pltpu.SemaphoreType.DMA((n, N_DEV - 1)), pltpu.SemaphoreType.DMA((n, N_DEV - 1)),
                  pltpu.SemaphoreType.DMA((n,))], start, finish)


def _chips_plan(scatter, whole=()):
    ns = len(scatter)
    n = ns + len(whole)

    def copies(ins, outs, sems):
        send_sems, recv_sems, local_sems = sems
        x, y, c = _my_place()
        my_chip = 2 * x + y

        def src(a, k):
            return ins[a].at[k] if a < ns else ins[a]

        local = [pltpu.make_async_copy(src(a, my_chip), outs[a].at[my_chip], local_sems.at[a]) for a in range(n)]
        remote = []
        for a in range(n):
            for j, (px, py) in enumerate(_other_chips(x, y)):
                remote.append(pltpu.make_async_remote_copy(
                    src_ref=src(a, 2 * px + py), dst_ref=outs[a].at[my_chip],
                    send_sem=send_sems.at[a, j], recv_sem=recv_sems.at[a, j],
                    device_id=(px, py, c), device_id_type=MESH))
        return local + remote

    def start(ins, outs, sems):
        for cp in copies(ins, outs, sems):
            cp.start()

    def finish(ins, outs, sems):
        for cp in copies(ins, outs, sems):
            cp.wait()

    shapes = [jax.ShapeDtypeStruct(s.shape, s.dtype) for s in scatter]
    shapes += [jax.ShapeDtypeStruct((4,) + s.shape, s.dtype) for s in whole]
    return _Plan(list(scatter) + list(whole), shapes,
                 [pltpu.SemaphoreType.DMA((n, 3)), pltpu.SemaphoreType.DMA((n, 3)), pltpu.SemaphoreType.DMA((n,))],
                 start, finish)


def _chip_copies(src_ref, land_ref, send_sems, recv_sems):
    x, y, c = _my_place()
    return [pltpu.make_async_remote_copy(
        src_ref=src_ref.at[2 * px + py], dst_ref=land_ref.at[2 * x + y], send_sem=send_sems.at[j],
        recv_sem=recv_sems.at[j], device_id=(px, py, c), device_id_type=MESH)
        for j, (px, py) in enumerate(_other_chips(x, y))]


def _chips_start(src, name):
    def body(src_ref, land_ref, send_sems, recv_sems, src_thru, land_thru, token):
        for cp in _chip_copies(src_ref, land_ref, send_sems, recv_sems):
            cp.start()
        token[...] = jnp.zeros_like(token)

    hbm = pl.BlockSpec(memory_space=pltpu.HBM)
    sem = pl.BlockSpec(memory_space=pltpu.SEMAPHORE)
    return pl.pallas_call(
        body, name=name,
        out_shape=(pltpu.SemaphoreType.DMA((3,)), pltpu.SemaphoreType.DMA((3,)), pltpu.HBM(src.shape, src.dtype),
                   pltpu.HBM(src.shape, src.dtype), jax.ShapeDtypeStruct((8, LANES), F32)),
        in_specs=(hbm, hbm), out_specs=(sem, sem, hbm, hbm, pl.BlockSpec(memory_space=pltpu.VMEM)),
        input_output_aliases={0: 2, 1: 3},
        compiler_params=pltpu.CompilerParams(has_side_effects=pltpu.SideEffectType.DATAFLOW_SIDE_EFFECTING),
    )(pltpu.with_memory_space_constraint(src, pltpu.HBM),
      pltpu.with_memory_space_constraint(jnp.zeros(src.shape, src.dtype), pltpu.HBM))


def _chips_wait(send_sems, recv_sems, src_thru, land_thru, after, name):
    def body(src_ref, land_ref, send_sems, recv_sems, after_ref, src_dead, got_ref):
        for cp in _chip_copies(src_ref, land_ref, send_sems, recv_sems):
            cp.wait_send()
            cp.wait_recv()

    hbm = pl.BlockSpec(memory_space=pltpu.HBM)
    sem = pl.BlockSpec(memory_space=pltpu.SEMAPHORE)
    return pl.pallas_call(
        body, name=name,
        out_shape=(pltpu.HBM(src_thru.shape, src_thru.dtype), pltpu.HBM(land_thru.shape, land_thru.dtype)),
        in_specs=(hbm, hbm, sem, sem, pl.BlockSpec(memory_space=pl.ANY)), out_specs=(hbm, hbm),
        input_output_aliases={0: 0, 1: 1},
        compiler_params=pltpu.CompilerParams(has_side_effects=pltpu.SideEffectType.DATAFLOW_SIDE_EFFECTING),
    )(src_thru, land_thru, send_sems, recv_sems, after)


def _small_update(parts, w, m, v, loss_parts, name):
    n = len(parts)

    def total(ref):
        acc = ref[0]
        for k in range(1, ref.shape[0]):
            acc = acc + ref[k]
        return acc

    def body(*refs):
        p_refs, w_refs, m_refs, v_refs = (refs[i * n:(i + 1) * n] for i in range(4))
        lp_ref = refs[4 * n]
        outs = refs[4 * n + 1:]
        g_refs, d_refs, mo_refs, vo_refs = (outs[i * n:(i + 1) * n] for i in range(4))
        for a in range(n):
            g = total(p_refs[a])
            g_refs[a][...] = g
            d_refs[a][...], mo_refs[a][...], vo_refs[a][...] = _adam_math(w_refs[a][...], g, m_refs[a][...],
                                                                          v_refs[a][...])
        outs[4 * n][...] = total(lp_ref)

    vm = pl.BlockSpec(memory_space=pltpu.VMEM)
    shapes = [jax.ShapeDtypeStruct(t.shape, F32) for _ in range(4) for t in w]
    shapes.append(jax.ShapeDtypeStruct(loss_parts.shape[1:], F32))
    outs = pl.pallas_call(
        body, name=name, out_shape=tuple(shapes),
        in_specs=[vm] * (4 * n + 1), out_specs=tuple([vm] * (4 * n + 1)),
        compiler_params=pltpu.CompilerParams(vmem_limit_bytes=VMEM_LIMIT),
    )(*parts, *w, *m, *v, loss_parts)
    return outs[0:n], outs[n:2 * n], outs[2 * n:3 * n], outs[3 * n:4 * n], outs[4 * n]


BIG = [("w_in", 1), ("w_branch_att", 1), ("w_branch_sg", 1), ("w_out", 0), ("w_xq", 0), ("w_xkv", 1), ("w_xo", 0),
       ("w_ffn_in", 1), ("w_ffn_out", 0)]
SMALL = [("norm_mix_g", (1, D)), ("rel_bias", (8, NREL)), ("sg_ln_g", (8, 64)), ("sg_ln_b", (8, 64)),
         ("sg_w", (8, 128, 128)), ("sg_b", (8, 128)), ("norm_xattn_g", (1, D)), ("norm_mem_g", (1, D)),
         ("norm_ffn_g", (1, D)), ("norm_final_g", (1, D))]
ADAM_ROWS = {"w_in": 192, "w_branch_att": 512, "w_branch_sg": 512, "w_xkv": 1024, "w_ffn_in": 176}


def _full(gathered):
    return gathered.reshape(N_DEV * gathered.shape[1], gathered.shape[2])


def _blocks(grad):
    return grad.reshape(N_DEV, grad.shape[0] // N_DEV, grad.shape[1])


def kernel(x, mem, norm_mix_g, w_in, rel_bias, sg_ln_g, sg_ln_b, sg_w, sg_b, w_branch_att, w_branch_sg, w_out, norm_xattn_g, norm_mem_g, w_xq, w_xkv, w_xo, norm_ffn_g, w_ffn_in, w_ffn_out, norm_final_g, loss_target, m_norm_mix_g, m_w_in, m_rel_bias, m_sg_ln_g, m_sg_ln_b, m_sg_w, m_sg_b, m_w_branch_att, m_w_branch_sg, m_w_out, m_norm_xattn_g, m_norm_mem_g, m_w_xq, m_w_xkv, m_w_xo, m_norm_ffn_g, m_w_ffn_in, m_w_ffn_out, m_norm_final_g, v_norm_mix_g, v_w_in, v_rel_bias, v_sg_ln_g, v_sg_ln_b, v_sg_w, v_sg_b, v_w_branch_att, v_w_branch_sg, v_w_out, v_norm_xattn_g, v_norm_mem_g, v_w_xq, v_w_xkv, v_w_xo, v_norm_ffn_g, v_w_ffn_in, v_w_ffn_out, v_norm_final_g):
    args = dict(locals())
    big_names = [n for n, _ in BIG]
    small_names = [n for n, _ in SMALL]
    S = x.shape[1]

    x, mem, tgt = x.reshape(S, D), mem.reshape(MEM, D), loss_target.reshape(S, D)
    small = {n: args[n].reshape(shape) for n, shape in SMALL}
    g1, g2, g3 = small["norm_mix_g"], small["norm_xattn_g"], small["norm_ffn_g"]
    g_mem, g4 = small["norm_mem_g"], small["norm_final_g"]
    lng = small["sg_ln_g"].reshape(1, SG_W)
    lnb = small["sg_ln_b"].reshape(1, SG_W)
    b_exp = jnp.broadcast_to(small["sg_b"].T[:, :, None], (128, 8, 64)).reshape(128, SG_W)
    rel_pad = jnp.pad(small["rel_bias"], ((0, 0), (0, 384 - NREL)))
    c_idx = lax.axis_index("c").astype(jnp.int32).reshape(1)

    shard = {n: (args[n][0].T if axis == 1 else args[n][0]).astype(BF16) for n, axis in BIG}
    my_chip = (2 * lax.axis_index("x") + lax.axis_index("y")).astype(jnp.int32).reshape(1)
    (h,) = _norm_in(x, g1)
    bias = _bias_table(rel_pad)
    merge_names = ["w_branch_att", "w_branch_sg", "w_out"]
    z, w_in_gathered, *got = _in_proj(h, shard["w_in"], my_chip, plans=[_gather_plan([shard[n] for n in merge_names])])
    w_in_t = _full(w_in_gathered)
    wba_t, wbs_t, w_out_f = (_full(g) for g in got)
    late_names = ["w_xq", "w_xkv", "w_xo", "w_ffn_in", "w_ffn_out"]
    y_att, lse, *got = _attn_fwd(z, bias, plans=[_gather_plan([shard[n] for n in late_names])])
    w_xq_f, w_xkv_t, w_xo_f, w_ffn_in_t, w_ffn_out_f = (_full(g) for g in got)
    y_sg = _sgu_fwd(z, lng, lnb, small["sg_w"], b_exp)
    x1 = _merge_fwd(x, y_att, y_sg, z, wba_t, wbs_t, w_out_f)
    kv, mn = _mem_kv(mem, g_mem, w_xkv_t)
    x2, hx, qx, o_x = _xattn_fwd(x1, g2, w_xq_f, kv, w_xo_f)
    dx3, gu, hf, act, loss_part, dg4 = _ffn_fwd(x2, tgt, g3, w_ffn_in_t, w_ffn_out_f, g4)

    def pair_sums(names, mine, theirs):
        return [_pair_sum(a, b, c_idx, "rs_pair_" + n) for n, a, b in zip(names, mine, theirs)]

    dx2, dgu, dg3 = _ffn_bwd(dx3, gu, x2, g3, w_ffn_out_f, w_ffn_in_t)
    ffn_names = ["w_ffn_out", "w_ffn_in"]
    ffn_mine = [_blocks(_dw(act, dx3, 1408, 1024, DW_TOKENS, "dw_ffn_out")),
                _blocks(_dw(dgu, hf, 1408, 1024, DW_TOKENS, "dw_ffn_in"))]
    dx1, dq_x, dkv, dg2, *ffn_theirs = _xattn_bwd(dx2, x1, qx, g2, w_xq_f, w_xo_f, kv,
                                                  plans=[_sibling_plan(ffn_mine)])
    ffn_chip = pair_sums(ffn_names, ffn_mine, ffn_theirs)
    d_xkv, dg_mem = _mem_kv_bwd(dkv, mem, g_mem, mn, w_xkv_t)
    merged, d_a, d_b, dy_att, dy_sg, dgates = _merge_bwd(dx1, y_att, y_sg, z, wba_t, wbs_t, w_out_f)
    mid_names = ["w_xo", "w_xq", "w_xkv", "w_out", "w_branch_att", "w_branch_sg"]
    mid_mine = [_blocks(g) for g in (
        _dw(o_x, dx2, 1024, 1024, DW_TOKENS, "dw_xo"), _dw(hx, dq_x, 1024, 1024, DW_TOKENS, "dw_xq"), d_xkv,
        _dw(merged, dx1, 1024, 1024, DW_TOKENS, "dw_out"), _dw(d_a, y_att, 1024, 512, DW_TOKENS, "dw_branch_att"),
        _dw(d_b, y_sg, 1024, 512, DW_TOKENS, "dw_branch_sg"))]
    duv, d_sgw, d_bx, d_lng, d_lnb, *got = _sgu_bwd(z, dy_sg, lng, lnb, small["sg_w"], b_exp,
                                                    plans=[_chips_plan(ffn_chip), _sibling_plan(mid_mine)])
    ffn_all, mid_theirs = got[:2], got[2:]
    mid_chip = pair_sums(mid_names, mid_mine, mid_theirs)
    dq, dk, dv, ds_sum, *mid_all = _attn_bwd(z, dy_att, y_att, lse, bias, plans=[_chips_plan(mid_chip)])
    d_rel = _bias_grad(ds_sum)

    grad_x, dz, dg1 = _in_bwd(dq, dk, dv, duv, dgates, x, dx1, g1, w_in_t)

    gs = {"norm_mix_g": dg1, "rel_bias": d_rel[:, :NREL], "sg_ln_g": d_lng.reshape(8, 64),
          "sg_ln_b": d_lnb.reshape(8, 64), "sg_w": d_sgw, "sg_b": d_bx.reshape(128, 8, 64)[:, :, 0].T,
          "norm_xattn_g": dg2, "norm_mem_g": dg_mem, "norm_ffn_g": dg3, "norm_final_g": dg4}
    d_in, *everyone = _dw(dz, h, 1152, 1024, DW_TOKENS, "dw_in",
                          plans=[_peers_plan([gs[n] for n in small_names] + [loss_part])])

    in_mine = [_blocks(d_in)]
    (in_theirs,) = _run_plan(_sibling_plan(in_mine), "rs_sibling")
    (in_chip,) = pair_sums(["w_in"], in_mine, [in_theirs])
    send_sems, recv_sems, in_chip_thru, landing, token = _chips_start(in_chip, "rs_chips_start")
    all_parts = dict(zip(ffn_names + mid_names, list(ffn_all) + list(mid_all)))

    def adam(n, axis, parts, after):
        wmv = [args[p + n][0] for p in ("", "m_", "v_")]
        if axis == 1 and wmv[0].shape[1] % LANES != 0:
            outs = _adam(parts, *(t.T for t in wmv), ADAM_ROWS[n], "adam_" + n, transposed=False, after=after)
            return [t.T[None] for t in outs]
        tr = ADAM_ROWS[n] if axis == 1 else wmv[0].shape[0]
        return [t[None] for t in _adam(parts, *wmv, tr, "adam_" + n, transposed=(axis == 1), after=after)]

    res = {n: adam(n, axis, all_parts[n], token) for n, axis in BIG if n != "w_in"}
    in_chip, landing = _chips_wait(send_sems, recv_sems, in_chip_thru, landing, res["w_ffn_in"][1], "rs_chips_wait")
    my_chip = 2 * lax.axis_index("x") + lax.axis_index("y")
    own = lax.dynamic_slice_in_dim(in_chip, my_chip, 1, axis=0)
    res["w_in"] = adam("w_in", 1, lax.dynamic_update_slice_in_dim(landing, own, my_chip, axis=0), None)
    small_res = _small_update(
        everyone[:-1], [small[n] for n in small_names],
        [args["m_" + n].reshape(s) for n, s in SMALL], [args["v_" + n].reshape(s) for n, s in SMALL],
        everyone[-1], "adam_small")
    for i, n in enumerate(small_names):
        res[n] = [small_res[k][i].reshape(args[n].shape) for k in range(4)]
    loss = small_res[4][0, 0]

    order = ["norm_mix_g", "w_in", "rel_bias", "sg_ln_g", "sg_ln_b", "sg_w", "sg_b", "w_branch_att", "w_branch_sg",
             "w_out", "norm_xattn_g", "norm_mem_g", "w_xq", "w_xkv", "w_xo", "norm_ffn_g", "w_ffn_in", "w_ffn_out",
             "norm_final_g"]
    outs = [loss, grad_x.reshape(1, S, D)]
    for k in range(4):
        outs += [res[n][k] for n in order]
    return tuple(outs)
```

```python
import math

import jax
import jax.numpy as jnp
from jax import lax
from jax.experimental import pallas as pl
from jax.experimental.pallas import tpu as pltpu

F32 = jnp.float32
BF16 = jnp.bfloat16

D = 1024
ATT_W = 512
SG_W = 512
IN_COLS = 4608
DFF = 2816
MEM = 256
XH = 4
CHUNK = 64
BAND_KEYS = 640
ATT_R = 512
ATT_SUBS = ATT_R // 128
ATT_ROWS = 32
ATT_ROWS_BWD = 16
DW_TOKENS = 2048
REL_CLIP = 128
NREL = 2 * REL_CLIP + 1
EPS = 1e-6
NEG = -1e30
N_DEV = 8

ADAM_LR = 0.001
ADAM_B1 = 0.9
ADAM_B2 = 0.999
ADAM_EPS = 1e-08
ADAM_WD = 0.01
ADAM_STEP = 10

LANES = 128
VMEM_LIMIT = 56 * 1024 * 1024
MESH = pl.DeviceIdType.MESH


def _cparams(n_axes):
    return pltpu.CompilerParams(dimension_semantics=("arbitrary",) * n_axes, vmem_limit_bytes=VMEM_LIMIT)


def _resident(shape):
    zeros = (0,) * len(shape)
    return pl.BlockSpec(shape, lambda *_: zeros, pipeline_mode=pl.Buffered(1))


def _rows(tm, cols, col_block=0):
    return pl.BlockSpec((tm, cols), lambda i: (i, col_block))


def _sigmoid(x):
    return pl.reciprocal(1.0 + jnp.exp(-x), approx=True)


_GELU_C = math.sqrt(2.0 / math.pi)


def _gelu(x):
    t = jnp.tanh(_GELU_C * (x + 0.044715 * (x * x * x)))
    return x * (0.5 * (1.0 + t))


def _gelu_and_grad(x):
    x2 = x * x
    t = jnp.tanh(_GELU_C * (x + 0.044715 * (x2 * x)))
    cdf = 0.5 * (1.0 + t)
    dcdf = 0.5 * (1.0 - t * t) * (_GELU_C * (1.0 + 3.0 * 0.044715 * x2))
    return x * cdf, cdf + x * dcdf


def _rstd(x):
    return lax.rsqrt(jnp.mean(x * x, axis=-1, keepdims=True) + EPS)


def _rms_bwd(dh, x, r, g):
    xh = x * r
    dxh = dh * g
    dx = r * (dxh - xh * jnp.mean(dxh * xh, axis=-1, keepdims=True))
    dg = jnp.sum(dh * xh, axis=0, keepdims=True)
    return dx, dg


def _group_sum64(x):
    r = lax.broadcasted_iota(jnp.int32, (LANES, LANES), 0) // 64
    c = lax.broadcasted_iota(jnp.int32, (LANES, LANES), 1) // 64
    same_group = (r == c).astype(BF16)

    def one(v):
        hi = v.astype(BF16)
        rest = v - hi.astype(F32)
        mid = rest.astype(BF16)
        lo = (rest - mid.astype(F32)).astype(BF16)
        return _dot(hi, same_group) + _dot(mid, same_group) + _dot(lo, same_group)

    pieces = [one(x[:, LANES * j:LANES * (j + 1)]) for j in range(x.shape[1] // LANES)]
    return pieces[0] if len(pieces) == 1 else jnp.concatenate(pieces, axis=1)


def _dot(a, b):
    return jnp.dot(a, b, preferred_element_type=F32)


def _dot_nt(a, b):
    return lax.dot_general(a, b, (((1,), (1,)), ((), ())), preferred_element_type=F32)


def _dot_tn(a, b):
    return lax.dot_general(a, b, (((0,), (0,)), ((), ())), preferred_element_type=F32)


DIAGS = 768


def _diag_onehot():
    r_idx = lax.broadcasted_iota(jnp.int32, (384, DIAGS), 0)
    t_idx = lax.broadcasted_iota(jnp.int32, (384, DIAGS), 1)
    dist = (8 * CHUNK + 127) - t_idx
    return (jnp.clip(dist, -REL_CLIP, REL_CLIP) + REL_CLIP == r_idx).astype(F32)


def _shift_rows(x, reverse):
    row = lax.broadcasted_iota(jnp.int32, x.shape, 0)
    for k in range(7):
        amt = (DIAGS - (1 << k)) if reverse else (1 << k)
        x = jnp.where(((row >> k) & 1) == 1, pltpu.roll(x, amt, 1), x)
    return x


N_TABLES = 1 + ATT_SUBS


def _bias_table(rel_bias_pad):
    def body(rb_ref, out_ref):
        per_diag = jnp.dot(rb_ref[...], _diag_onehot(), preferred_element_type=F32,
                           precision=lax.Precision.HIGHEST)
        a = lax.broadcasted_iota(jnp.int32, (128, BAND_KEYS), 0)
        b = lax.broadcasted_iota(jnp.int32, (128, BAND_KEYS), 1)
        band = (b // CHUNK >= a // CHUNK) & (b // CHUNK <= a // CHUNK + 8)
        for h in range(8):
            rows = jnp.broadcast_to(per_diag[h:h + 1, :], (128, DIAGS))
            table = _shift_rows(pltpu.roll(rows, DIAGS - 127, 1), reverse=False)[:, :BAND_KEYS]
            out_ref[0, h] = jnp.where(band, table, NEG)
            for s in range(ATT_SUBS):
                out_ref[1 + s, h] = jnp.where(band & (b >= 8 * CHUNK - 128 * s), table, NEG)

    return pl.pallas_call(
        body, name="bias_table",
        out_shape=jax.ShapeDtypeStruct((N_TABLES, 8, 128, BAND_KEYS), F32),
        in_specs=[pl.BlockSpec(memory_space=pltpu.VMEM)],
        out_specs=pl.BlockSpec(memory_space=pltpu.VMEM),
        compiler_params=pltpu.CompilerParams(vmem_limit_bytes=VMEM_LIMIT),
    )(rel_bias_pad)


def _bias_grad(ds_sum):
    def body(ds_ref, out_ref):
        sums = []
        for h in range(8):
            padded = jnp.concatenate([ds_ref[h], jnp.zeros((128, DIAGS - BAND_KEYS), F32)], axis=1)
            skewed = pltpu.roll(_shift_rows(padded, reverse=True), 127, 1)
            sums.append(jnp.sum(skewed, axis=0, keepdims=True))
        per_diag = jnp.concatenate(sums, axis=0)
        out_ref[...] = lax.dot_general(per_diag, _diag_onehot(), (((1,), (1,)), ((), ())),
                                       preferred_element_type=F32, precision=lax.Precision.HIGHEST)

    return pl.pallas_call(
        body, name="bias_grad",
        out_shape=jax.ShapeDtypeStruct((8, 384), F32),
        in_specs=[pl.BlockSpec(memory_space=pltpu.VMEM)],
        out_specs=pl.BlockSpec(memory_space=pltpu.VMEM),
    )(ds_sum)


def _norm_in(x, g1, plans=(), tm=512):
    S = x.shape[0]

    def body(x_ref, g_ref, h_ref):
        xv = x_ref[...]
        h_ref[...] = (xv * _rstd(xv) * g_ref[...]).astype(BF16)

    return _call(
        body, name="norm_in", grid=(S // tm,),
        out_shape=(jax.ShapeDtypeStruct((S, D), BF16),),
        in_specs=[_rows(tm, D), _resident((1, D))], out_specs=(_rows(tm, D),),
        operands=(x, g1), plans=plans)


def _in_proj(h, w_in_t, plans=(), tm=512):
    S = h.shape[0]

    def body(h_ref, w_ref, qkv_ref, uv_ref, gate_ref):
        h = h_ref[...]
        for c in range(IN_COLS // 512):
            zc = _dot_nt(h, w_ref[512 * c:512 * (c + 1), :])
            if c == 0:
                qkv_ref[:, 0:512] = (zc * 0.125).astype(BF16)
            elif c < 3:
                qkv_ref[:, 512 * c:512 * (c + 1)] = zc.astype(BF16)
            elif c < 5:
                uv_ref[:, 512 * (c - 3):512 * (c - 2)] = zc.astype(BF16)
            else:
                gate_ref[:, 512 * (c - 5):512 * (c - 4)] = zc.astype(BF16)

    return _call(
        body, name="in_proj", grid=(S // tm,),
        out_shape=(jax.ShapeDtypeStruct((S, 3 * ATT_W), BF16), jax.ShapeDtypeStruct((S, 2 * SG_W), BF16),
                   jax.ShapeDtypeStruct((S, 2 * D), BF16)),
        in_specs=[_rows(tm, D), _resident((IN_COLS, D))],
        out_specs=(_rows(tm, 3 * ATT_W), _rows(tm, 2 * SG_W), _rows(tm, 2 * D)),
        operands=(h, w_in_t), plans=plans)


def _two_heads(a, lo):
    return jnp.concatenate([jnp.where(lo, a, 0), jnp.where(lo, 0, a)], axis=0)


def _att_specs():
    R = ATT_R
    q = pl.BlockSpec((R, LANES), lambda j, i: (i, j))
    kp = pl.BlockSpec((R, LANES), lambda j, i: (jnp.maximum(i - 1, 0), 4 + j))
    kc = pl.BlockSpec((R, LANES), lambda j, i: (i, 4 + j))
    vp = pl.BlockSpec((R, LANES), lambda j, i: (jnp.maximum(i - 1, 0), 8 + j))
    vc = pl.BlockSpec((R, LANES), lambda j, i: (i, 8 + j))
    bias = pl.BlockSpec((N_TABLES, 2, 128, BAND_KEYS), lambda j, i: (0, j, 0, 0))
    return [q, kp, kc, vp, vc, bias]


def _attn_fwd(qkv, bias, plans=()):
    S = qkv.shape[0]
    R = ATT_R

    def body(q_ref, kp_ref, kc_ref, vp_ref, vc_ref, b_ref, o_ref, lse_ref, s_ref, p_ref, l_ref):
        i = pl.program_id(1)
        lo = lax.broadcasted_iota(jnp.int32, (1, LANES), 1) < 64
        kwin = jnp.concatenate([kp_ref[...], kc_ref[...]], axis=0)
        vwin = jnp.concatenate([vp_ref[...], vc_ref[...]], axis=0)
        for sub in range(ATT_SUBS):
            q2 = q_ref[128 * sub:128 * (sub + 1), :]
            kw = kwin[128 * sub:128 * sub + BAND_KEYS]
            vw = vwin[128 * sub:128 * sub + BAND_KEYS]
            table = jnp.where(i == 0, 1 + sub, 0)
            s_ref[...] = _dot_nt(_two_heads(q2, lo), kw)
            for hh in range(2):
                for r0 in range(0, 128, ATT_ROWS):
                    rows = slice(128 * hh + r0, 128 * hh + r0 + ATT_ROWS)
                    s = s_ref[rows, :] + b_ref[table, hh, r0:r0 + ATT_ROWS, :]
                    top = jnp.max(s, axis=-1, keepdims=True)
                    p = jnp.exp(s - top)
                    total = jnp.sum(p, axis=-1, keepdims=True)
                    p_ref[rows, :] = (p / total).astype(BF16)
                    l_ref[rows, :] = jnp.broadcast_to(top + jnp.log(total), (ATT_ROWS, LANES))
            o = _dot(jnp.concatenate([p_ref[0:128, :], p_ref[128:256, :]], axis=1), _two_heads(vw, lo))
            o_ref[128 * sub:128 * (sub + 1), :] = o.astype(BF16)
            lse_ref[128 * sub:128 * (sub + 1), :] = jnp.where(lo, l_ref[0:128, :], l_ref[128:256, :])

    blk = pl.BlockSpec((R, LANES), lambda j, i: (i, j))
    return _call(
        body, name="attn_fwd", grid=(4, S // R),
        out_shape=(jax.ShapeDtypeStruct((S, ATT_W), BF16), jax.ShapeDtypeStruct((S, ATT_W), F32)),
        in_specs=_att_specs(), out_specs=(blk, blk),
        scratch_shapes=[pltpu.VMEM((256, BAND_KEYS), F32), pltpu.VMEM((256, BAND_KEYS), BF16),
                        pltpu.VMEM((256, LANES), F32)],
        operands=(qkv, qkv, qkv, qkv, qkv, bias), plans=plans)


def _sg_mask():
    t = lax.broadcasted_iota(jnp.int32, (128, 128), 0)
    s = lax.broadcasted_iota(jnp.int32, (128, 128), 1)
    return (s // CHUNK) <= (t // CHUNK)


def _sg_layernorm(gv, lng, lnb):
    mu = _group_sum64(gv) * (1.0 / 64)
    xc = gv - mu
    var = _group_sum64(xc * xc) * (1.0 / 64)
    rstd = lax.rsqrt(var + EPS)
    vhat = xc * rstd
    return vhat * lng + lnb, vhat, rstd


def _sgu_fwd(uv, lng, lnb, sg_w, b_exp, tm=512):
    S = uv.shape[0]

    def body(uv_ref, lng_ref, lnb_ref, w_ref, b_ref, y_ref):
        lane = lax.broadcasted_iota(jnp.int32, (1, LANES), 1)
        lo = lane < 64
        mask = _sg_mask()
        gu = _gelu(uv_ref[:, 0:SG_W].astype(F32))
        vln, _, _ = _sg_layernorm(_gelu(uv_ref[:, SG_W:2 * SG_W].astype(F32)), lng_ref[...], lnb_ref[...])
        for gp in range(4):
            w0 = jnp.where(mask, w_ref[2 * gp], 0).astype(BF16)
            w1 = jnp.where(mask, w_ref[2 * gp + 1], 0).astype(BF16)
            cols = slice(128 * gp, 128 * (gp + 1))
            for n in range(tm // 128):
                rows = slice(128 * n, 128 * (n + 1))
                vl = vln[rows, cols]
                sv = (_dot(w0, jnp.where(lo, vl, 0).astype(BF16)) + _dot(w1, jnp.where(lo, 0, vl).astype(BF16))
                      + b_ref[:, cols])
                y_ref[rows, cols] = (gu[rows, cols] * sv).astype(BF16)

    return pl.pallas_call(
        body, name="sgu_fwd", grid=(S // tm,),
        out_shape=jax.ShapeDtypeStruct((S, SG_W), BF16),
        in_specs=[_rows(tm, 2 * SG_W), _resident((1, SG_W)), _resident((1, SG_W)),
                  _resident((8, 128, 128)), _resident((128, SG_W))],
        out_specs=_rows(tm, SG_W),
        compiler_params=_cparams(1),
    )(uv, lng, lnb, sg_w, b_exp)


def _merge_fwd(x, y_att, y_sg, gates, wba_t, wbs_t, w_out, tm=512):
    S = x.shape[0]

    def body(x_ref, ya_ref, ys_ref, g_ref, wba_ref, wbs_ref, wo_ref, x1_ref):
        a = _dot_nt(ya_ref[...], wba_ref[...])
        b = _dot_nt(ys_ref[...], wbs_ref[...])
        merged = _sigmoid(g_ref[:, 0:D].astype(F32)) * a + _sigmoid(g_ref[:, D:2 * D].astype(F32)) * b
        x1_ref[...] = x_ref[...] + _dot(merged.astype(BF16), wo_ref[...])

    return pl.pallas_call(
        body, name="merge_fwd", grid=(S // tm,),
        out_shape=jax.ShapeDtypeStruct((S, D), F32),
        in_specs=[_rows(tm, D), _rows(tm, ATT_W), _rows(tm, SG_W), _rows(tm, 2 * D),
                  _resident((D, ATT_W)), _resident((D, SG_W)), _resident((D, D))],
        out_specs=_rows(tm, D),
        compiler_params=_cparams(1),
    )(x, y_att, y_sg, gates, wba_t, wbs_t, w_out)


def _mem_kv(mem, g_mem, w_xkv_t):
    def body(m_ref, g_ref, w_ref, kv_ref, mn_ref):
        mv = m_ref[...]
        mn = (mv * _rstd(mv) * g_ref[...]).astype(BF16)
        mn_ref[...] = mn
        kv_ref[...] = _dot_nt(mn, w_ref[...]).astype(BF16)

    vm = pl.BlockSpec(memory_space=pltpu.VMEM)
    return pl.pallas_call(
        body, name="mem_kv",
        out_shape=(jax.ShapeDtypeStruct((MEM, 2 * D), BF16), jax.ShapeDtypeStruct((MEM, D), BF16)),
        in_specs=[vm, vm, vm], out_specs=(vm, vm),
        compiler_params=pltpu.CompilerParams(vmem_limit_bytes=VMEM_LIMIT),
    )(mem, g_mem, w_xkv_t)


def _xatt_head(qx, kv_ref, h):
    hs = slice(256 * h, 256 * (h + 1))
    s = _dot_nt(qx[:, hs], kv_ref[:, hs])
    p = jnp.exp(s - jnp.max(s, axis=-1, keepdims=True))
    return p / jnp.sum(p, axis=-1, keepdims=True)


def _xattn_fwd(x1, g2, w_xq, kv, w_xo, tm=512):
    S = x1.shape[0]

    def body(x_ref, g_ref, wq_ref, kv_ref, wo_ref, x2_ref, hx_ref, qx_ref, o_ref):
        xv = x_ref[...]
        hx = (xv * _rstd(xv) * g_ref[...]).astype(BF16)
        hx_ref[...] = hx
        qx = (_dot(hx, wq_ref[...]) * (1.0 / 16)).astype(BF16)
        qx_ref[...] = qx
        for h in range(XH):
            p = _xatt_head(qx, kv_ref, h)
            o_ref[:, 256 * h:256 * (h + 1)] = _dot(p.astype(BF16), kv_ref[:, D + 256 * h:D + 256 * (h + 1)]).astype(BF16)
        x2_ref[...] = xv + _dot(o_ref[...], wo_ref[...])

    return pl.pallas_call(
        body, name="xattn_fwd", grid=(S // tm,),
        out_shape=(jax.ShapeDtypeStruct((S, D), F32),) + (jax.ShapeDtypeStruct((S, D), BF16),) * 3,
        in_specs=[_rows(tm, D), _resident((1, D)), _resident((D, D)), _resident((MEM, 2 * D)), _resident((D, D))],
        out_specs=(_rows(tm, D),) * 4,
        compiler_params=_cparams(1),
    )(x1, g2, w_xq, kv, w_xo)


FF_CHUNK = 1408


def _ffn_fwd(x2, tgt, g3, w_ffn_in_t, w_ffn_out, g4, tm=256):
    S = x2.shape[0]

    def body(x_ref, t_ref, g3_ref, wi_ref, wo_ref, g4_ref, dx3_ref, gu_ref, hf_ref, act_ref, loss_ref, dg4_ref):
        i = pl.program_id(0)
        xv = x_ref[...]
        hf = (xv * _rstd(xv) * g3_ref[...]).astype(BF16)
        hf_ref[...] = hf
        acc = xv
        for c in range(DFF // FF_CHUNK):
            cs = slice(FF_CHUNK * c, FF_CHUNK * (c + 1))
            us = slice(DFF + FF_CHUNK * c, DFF + FF_CHUNK * (c + 1))
            gate = _dot_nt(hf, wi_ref[cs, :])
            up = _dot_nt(hf, wi_ref[us, :])
            gu_ref[:, cs] = gate.astype(BF16)
            gu_ref[:, us] = up.astype(BF16)
            act = ((gate * _sigmoid(gate)) * up).astype(BF16)
            act_ref[:, cs] = act
            acc = acc + _dot(act, wo_ref[cs, :])
        r4 = _rstd(acc)
        g4 = g4_ref[...]
        diff = acc * r4 * g4 - t_ref[...]
        dy = diff * (1.0 / D)
        dx3, dg4 = _rms_bwd(dy, acc, r4, g4)
        dx3_ref[...] = dx3
        part = 0.5 * jnp.sum(jnp.mean(diff * diff, axis=-1, keepdims=True))

        @pl.when(i == 0)
        def _():
            loss_ref[...] = jnp.zeros_like(loss_ref)
            dg4_ref[...] = jnp.zeros_like(dg4_ref)

        loss_ref[...] += jnp.full(loss_ref.shape, part, F32)
        dg4_ref[...] += dg4

    return pl.pallas_call(
        body, name="ffn_fwd", grid=(S // tm,),
        out_shape=(jax.ShapeDtypeStruct((S, D), F32), jax.ShapeDtypeStruct((S, 2 * DFF), BF16),
                   jax.ShapeDtypeStruct((S, D), BF16), jax.ShapeDtypeStruct((S, DFF), BF16),
                   jax.ShapeDtypeStruct((8, LANES), F32), jax.ShapeDtypeStruct((1, D), F32)),
        in_specs=[_rows(tm, D), _rows(tm, D), _resident((1, D)), _resident((2 * DFF, D)), _resident((DFF, D)),
                  _resident((1, D))],
        out_specs=(_rows(tm, D), _rows(tm, 2 * DFF), _rows(tm, D), _rows(tm, DFF),
                   pl.BlockSpec((8, LANES), lambda i: (0, 0)), pl.BlockSpec((1, D), lambda i: (0, 0))),
        compiler_params=_cparams(1),
    )(x2, tgt, g3, w_ffn_in_t, w_ffn_out, g4)


def _ffn_bwd(dx3, gu, x2, g3, w_ffn_out, w_ffn_in_t, tm=256):
    S = x2.shape[0]

    def body(d_ref, gu_ref, x_ref, g3_ref, wo_ref, wit_ref, dx2_ref, dgu_ref, dg3_ref):
        i = pl.program_id(0)
        d3 = d_ref[...]
        d3b = d3.astype(BF16)
        for c in range(DFF // FF_CHUNK):
            cs = slice(FF_CHUNK * c, FF_CHUNK * (c + 1))
            us = slice(DFF + FF_CHUNK * c, DFF + FF_CHUNK * (c + 1))
            da = _dot_nt(d3b, wo_ref[cs, :])
            gate = gu_ref[:, cs].astype(F32)
            up = gu_ref[:, us].astype(F32)
            sg = _sigmoid(gate)
            dgate = (da * up * (sg * (1.0 + gate * (1.0 - sg)))).astype(BF16)
            dup = (da * (gate * sg)).astype(BF16)
            dgu_ref[:, cs] = dgate
            dgu_ref[:, us] = dup
        dhf = _dot(dgu_ref[...], wit_ref[...])
        xv = x_ref[...]
        dx, dg3 = _rms_bwd(dhf, xv, _rstd(xv), g3_ref[...])
        dx2_ref[...] = d3 + dx

        @pl.when(i == 0)
        def _():
            dg3_ref[...] = jnp.zeros_like(dg3_ref)

        dg3_ref[...] += dg3

    return pl.pallas_call(
        body, name="ffn_bwd", grid=(S // tm,),
        out_shape=(jax.ShapeDtypeStruct((S, D), F32), jax.ShapeDtypeStruct((S, 2 * DFF), BF16),
                   jax.ShapeDtypeStruct((1, D), F32)),
        in_specs=[_rows(tm, D), _rows(tm, 2 * DFF), _rows(tm, D), _resident((1, D)),
                  _resident((DFF, D)), _resident((2 * DFF, D))],
        out_specs=(_rows(tm, D), _rows(tm, 2 * DFF), pl.BlockSpec((1, D), lambda i: (0, 0))),
        compiler_params=_cparams(1),
    )(dx3, gu, x2, g3, w_ffn_out, w_ffn_in_t)


def _dw(a, b, tmm, tn, ts, name, out_dtype=BF16, plans=()):
    S, M = a.shape
    N = b.shape[1]
    ts = min(ts, S)
    nk = S // ts

    def body(a_ref, b_ref, o_ref, acc_ref):
        k = pl.program_id(2)

        @pl.when(k == 0)
        def _():
            acc_ref[...] = jnp.zeros_like(acc_ref)

        acc_ref[...] += _dot_tn(a_ref[...].astype(BF16), b_ref[...].astype(BF16))

        @pl.when(k == nk - 1)
        def _():
            o_ref[...] = acc_ref[...].astype(out_dtype)

    out = _call(
        body, name=name, grid=(M // tmm, N // tn, nk),
        out_shape=(jax.ShapeDtypeStruct((M, N), out_dtype),),
        in_specs=[pl.BlockSpec((ts, tmm), lambda m, n, k: (k, m)), pl.BlockSpec((ts, tn), lambda m, n, k: (k, n))],
        out_specs=(pl.BlockSpec((tmm, tn), lambda m, n, k: (m, n)),),
        scratch_shapes=[pltpu.VMEM((tmm, tn), F32)],
        operands=(a, b), plans=plans)
    return out if plans else out[0]


def _xattn_bwd(dx2, x1, qx, g2, w_xq, w_xo, kv, plans=(), tm=512):
    S = x1.shape[0]

    def body(d_ref, x_ref, qx_ref, g_ref, wq_ref, wo_ref, kv_ref, dx1_ref, dq_ref, dkv_ref, dg2_ref):
        i = pl.program_id(0)

        @pl.when(i == 0)
        def _():
            dkv_ref[...] = jnp.zeros_like(dkv_ref)
            dg2_ref[...] = jnp.zeros_like(dg2_ref)

        d2 = d_ref[...]
        qx = qx_ref[...]
        do = _dot_nt(d2.astype(BF16), wo_ref[...]).astype(BF16)
        for h in range(XH):
            hs = slice(256 * h, 256 * (h + 1))
            vs = slice(D + 256 * h, D + 256 * (h + 1))
            p = _xatt_head(qx, kv_ref, h)
            dp = _dot_nt(do[:, hs], kv_ref[:, vs])
            ds = (p * (dp - jnp.sum(dp * p, axis=-1, keepdims=True))).astype(BF16)
            dq_ref[:, hs] = (_dot(ds, kv_ref[:, hs]) * (1.0 / 16)).astype(BF16)
            dkv_ref[:, hs] += _dot_tn(ds, qx[:, hs])
            dkv_ref[:, vs] += _dot_tn(p.astype(BF16), do[:, hs])
        dhx = _dot_nt(dq_ref[...], wq_ref[...])
        xv = x_ref[...]
        dx, dg2 = _rms_bwd(dhx, xv, _rstd(xv), g_ref[...])
        dx1_ref[...] = d2 + dx
        dg2_ref[...] += dg2

    return _call(
        body, name="xattn_bwd", grid=(S // tm,),
        out_shape=(jax.ShapeDtypeStruct((S, D), F32), jax.ShapeDtypeStruct((S, D), BF16),
                   jax.ShapeDtypeStruct((MEM, 2 * D), F32), jax.ShapeDtypeStruct((1, D), F32)),
        in_specs=[_rows(tm, D), _rows(tm, D), _rows(tm, D), _resident((1, D)), _resident((D, D)), _resident((D, D)),
                  _resident((MEM, 2 * D))],
        out_specs=(_rows(tm, D), _rows(tm, D),
                   pl.BlockSpec((MEM, 2 * D), lambda i: (0, 0)), pl.BlockSpec((1, D), lambda i: (0, 0))),
        operands=(dx2, x1, qx, g2, w_xq, w_xo, kv), plans=plans)


def _mem_kv_bwd(dkv, mem, g_mem, mn, w_xkv_t):
    def body(dkv_ref, m_ref, g_ref, mn_ref, wt_ref, dw_ref, dg_ref):
        dkvb = dkv_ref[...].astype(BF16)
        dw_ref[...] = _dot_tn(dkvb, mn_ref[...]).astype(BF16)
        dmn = _dot(dkvb, wt_ref[...])
        mv = m_ref[...]
        dg_ref[...] = jnp.sum(dmn * (mv * _rstd(mv)), axis=0, keepdims=True)

    vm = pl.BlockSpec(memory_space=pltpu.VMEM)
    return pl.pallas_call(
        body, name="mem_kv_bwd",
        out_shape=(jax.ShapeDtypeStruct((2 * D, D), BF16), jax.ShapeDtypeStruct((1, D), F32)),
        in_specs=[vm] * 5, out_specs=(vm, vm),
        compiler_params=pltpu.CompilerParams(vmem_limit_bytes=VMEM_LIMIT),
    )(dkv, mem, g_mem, mn, w_xkv_t)


def _merge_bwd(dx1, y_att, y_sg, gates, wba_t, wbs_t, w_out, tm=512):
    S = dx1.shape[0]

    def body(d_ref, ya_ref, ys_ref, g_ref, wbat_ref, wbst_ref, wo_ref,
             mg_ref, da_ref, db_ref, dya_ref, dys_ref, dg_ref):
        dm = _dot_nt(d_ref[...].astype(BF16), wo_ref[...])
        a = _dot_nt(ya_ref[...], wbat_ref[...])
        b = _dot_nt(ys_ref[...], wbst_ref[...])
        sa = _sigmoid(g_ref[:, 0:D].astype(F32))
        sb = _sigmoid(g_ref[:, D:2 * D].astype(F32))
        mg_ref[...] = (sa * a + sb * b).astype(BF16)
        da = (dm * sa).astype(BF16)
        db = (dm * sb).astype(BF16)
        da_ref[...] = da
        db_ref[...] = db
        dg_ref[:, 0:D] = (dm * a * sa * (1.0 - sa)).astype(BF16)
        dg_ref[:, D:2 * D] = (dm * b * sb * (1.0 - sb)).astype(BF16)
        dya_ref[...] = _dot(da, wbat_ref[...]).astype(BF16)
        dys_ref[...] = _dot(db, wbst_ref[...]).astype(BF16)

    return pl.pallas_call(
        body, name="merge_bwd", grid=(S // tm,),
        out_shape=(jax.ShapeDtypeStruct((S, D), BF16), jax.ShapeDtypeStruct((S, D), BF16),
                   jax.ShapeDtypeStruct((S, D), BF16), jax.ShapeDtypeStruct((S, ATT_W), BF16),
                   jax.ShapeDtypeStruct((S, SG_W), BF16), jax.ShapeDtypeStruct((S, 2 * D), BF16)),
        in_specs=[_rows(tm, D), _rows(tm, ATT_W), _rows(tm, SG_W), _rows(tm, 2 * D),
                  _resident((D, ATT_W)), _resident((D, SG_W)), _resident((D, D))],
        out_specs=(_rows(tm, D), _rows(tm, D), _rows(tm, D), _rows(tm, ATT_W), _rows(tm, SG_W), _rows(tm, 2 * D)),
        compiler_params=_cparams(1),
    )(dx1, y_att, y_sg, gates, wba_t, wbs_t, w_out)


def _sgu_bwd(uv, dy_sg, lng, lnb, sg_w, b_exp, plans=(), tm=512):
    S = uv.shape[0]
    n_steps = S // tm

    def body(uv_ref, dy_ref, lng_ref, lnb_ref, w_ref, b_ref, duv_ref, dw_ref, dbx_ref, dlng_ref, dlnb_ref, dvln_ref):
        i = pl.program_id(0)

        @pl.when(i == 0)
        def _():
            dw_ref[...] = jnp.zeros_like(dw_ref)
            dbx_ref[...] = jnp.zeros_like(dbx_ref)
            dlng_ref[...] = jnp.zeros_like(dlng_ref)
            dlnb_ref[...] = jnp.zeros_like(dlnb_ref)

        lane = lax.broadcasted_iota(jnp.int32, (1, LANES), 1)
        lo = lane < 64
        mask = _sg_mask()
        lng = lng_ref[...]
        gu, dgelu_u = _gelu_and_grad(uv_ref[:, 0:SG_W].astype(F32))
        gv, dgelu_v = _gelu_and_grad(uv_ref[:, SG_W:2 * SG_W].astype(F32))
        vln, vhat, rstd = _sg_layernorm(gv, lng, lnb_ref[...])
        dy = dy_ref[...].astype(F32)
        dsv_all = dy * gu
        for gp in range(4):
            wf0 = jnp.where(mask, w_ref[2 * gp], 0)
            wf1 = jnp.where(mask, w_ref[2 * gp + 1], 0)
            w0 = wf0.astype(BF16)
            w1 = wf1.astype(BF16)
            cols = slice(128 * gp, 128 * (gp + 1))
            dw0 = jnp.zeros((128, 128), F32)
            dw1 = jnp.zeros((128, 128), F32)
            dbx = jnp.zeros((128, LANES), F32)
            for n in range(tm // 128):
                rows = slice(128 * n, 128 * (n + 1))
                vl = vln[rows, cols]
                vl0 = jnp.where(lo, vl, 0).astype(BF16)
                vl1 = jnp.where(lo, 0, vl).astype(BF16)
                sv = _dot(w0, vl0) + _dot(w1, vl1) + b_ref[:, cols]
                duv_ref[rows, cols] = (dy[rows, cols] * sv * dgelu_u[rows, cols]).astype(BF16)
                dsv = dsv_all[rows, cols]
                dbx = dbx + dsv
                ds0 = jnp.where(lo, dsv, 0).astype(BF16)
                ds1 = jnp.where(lo, 0, dsv).astype(BF16)
                dw0 = dw0 + _dot_nt(ds0, vl0)
                dw1 = dw1 + _dot_nt(ds1, vl1)
                dvln_ref[rows, cols] = _dot_tn(w0, ds0) + _dot_tn(w1, ds1)
            dw_ref[2 * gp] += jnp.where(mask, dw0, 0)
            dw_ref[2 * gp + 1] += jnp.where(mask, dw1, 0)
            dbx_ref[:, cols] += dbx
        dvln = dvln_ref[...]
        dlng_ref[...] += jnp.sum(dvln * vhat, axis=0, keepdims=True)
        dlnb_ref[...] += jnp.sum(dvln, axis=0, keepdims=True)
        dvh = dvln * lng
        dgv = rstd * (dvh - _group_sum64(dvh) * (1.0 / 64) - vhat * (_group_sum64(dvh * vhat) * (1.0 / 64)))
        duv_ref[:, SG_W:2 * SG_W] = (dgv * dgelu_v).astype(BF16)

        @pl.when(i == n_steps - 1)
        def _():
            dbx_ref[...] = _group_sum64(dbx_ref[...])

    return _call(
        body, name="sgu_bwd", grid=(n_steps,),
        out_shape=(jax.ShapeDtypeStruct((S, 2 * SG_W), BF16), jax.ShapeDtypeStruct((8, 128, 128), F32),
                   jax.ShapeDtypeStruct((128, SG_W), F32), jax.ShapeDtypeStruct((1, SG_W), F32),
                   jax.ShapeDtypeStruct((1, SG_W), F32)),
        in_specs=[_rows(tm, 2 * SG_W), _rows(tm, SG_W), _resident((1, SG_W)), _resident((1, SG_W)),
                  _resident((8, 128, 128)), _resident((128, SG_W))],
        out_specs=(_rows(tm, 2 * SG_W), pl.BlockSpec((8, 128, 128), lambda i: (0, 0, 0)),
                   pl.BlockSpec((128, SG_W), lambda i: (0, 0)), pl.BlockSpec((1, SG_W), lambda i: (0, 0)),
                   pl.BlockSpec((1, SG_W), lambda i: (0, 0))),
        scratch_shapes=[pltpu.VMEM((tm, SG_W), F32)],
        operands=(uv, dy_sg, lng, lnb, sg_w, b_exp), plans=plans)


def _attn_bwd(qkv, dy_att, y_att, lse, bias, plans=()):
    S = qkv.shape[0]
    R = ATT_R

    def body(q_ref, kp_ref, kc_ref, vp_ref, vc_ref, b_ref, dy_ref, y_ref, lse_ref, dq_ref, dk_ref, dv_ref, dss_ref,
             s_ref, dp_ref, pb_ref, dsb_ref):
        i = pl.program_id(1)

        @pl.when(i == 0)
        def _():
            dk_ref[...] = jnp.zeros_like(dk_ref)
            dv_ref[...] = jnp.zeros_like(dv_ref)
            dss_ref[...] = jnp.zeros_like(dss_ref)

        lane = lax.broadcasted_iota(jnp.int32, (1, LANES), 1)
        kwin = jnp.concatenate([kp_ref[...], kc_ref[...]], axis=0)
        vwin = jnp.concatenate([vp_ref[...], vc_ref[...]], axis=0)
        for sub in range(ATT_SUBS):
            rows = slice(128 * sub, 128 * (sub + 1))
            q2 = q_ref[rows, :]
            do2 = dy_ref[rows, :]
            kw = kwin[128 * sub:128 * sub + BAND_KEYS]
            vw = vwin[128 * sub:128 * sub + BAND_KEYS]
            table = jnp.where(i == 0, 1 + sub, 0)
            dyy = do2.astype(F32) * y_ref[rows, :].astype(F32)
            lse2 = lse_ref[rows, :]
            dq = jnp.zeros((128, LANES), F32)
            dkw = jnp.zeros((BAND_KEYS, LANES), F32)
            dvw = jnp.zeros((BAND_KEYS, LANES), F32)
            for hh in range(2):
                mine = (lane >= 64 * hh) & (lane < 64 * (hh + 1))
                qm = jnp.where(mine, q2, 0)
                dom = jnp.where(mine, do2, 0)
                s_ref[...] = _dot_nt(qm, kw)
                dp_ref[...] = _dot_nt(dom, vw)
                delta = jnp.sum(jnp.where(mine, dyy, 0.0), axis=-1, keepdims=True)
                lse = lse2[:, 64 * hh:64 * hh + 1]
                for r0 in range(0, 128, ATT_ROWS_BWD):
                    rr = slice(r0, r0 + ATT_ROWS_BWD)
                    p = jnp.exp(s_ref[rr, :] + b_ref[table, hh, rr, :] - lse[rr])
                    ds = p * (dp_ref[rr, :] - delta[rr])
                    dss_ref[hh, rr, :] += ds
                    pb_ref[rr, :] = p.astype(BF16)
                    dsb_ref[rr, :] = ds.astype(BF16)
                dsb = dsb_ref[...]
                dq = dq + _dot(dsb, jnp.where(mine, kw, 0))
                dkw = dkw + _dot_tn(dsb, qm)
                dvw = dvw + _dot_tn(pb_ref[...], dom)
            dq_ref[rows, :] = dq.astype(BF16)
            start = pl.multiple_of(i * R + 128 * sub, 128)
            dk_ref[pl.ds(start, BAND_KEYS), :] += dkw
            dv_ref[pl.ds(start, BAND_KEYS), :] += dvw

    acc_spec = pl.BlockSpec((S + 8 * CHUNK, LANES), lambda j, i: (0, j))
    return _call(
        body, name="attn_bwd", grid=(4, S // R),
        out_shape=(jax.ShapeDtypeStruct((S, ATT_W), BF16), jax.ShapeDtypeStruct((S + 8 * CHUNK, ATT_W), F32),
                   jax.ShapeDtypeStruct((S + 8 * CHUNK, ATT_W), F32), jax.ShapeDtypeStruct((8, 128, BAND_KEYS), F32)),
        in_specs=_att_specs() + [pl.BlockSpec((R, LANES), lambda j, i: (i, j))] * 3,
        out_specs=(pl.BlockSpec((R, LANES), lambda j, i: (i, j)), acc_spec, acc_spec,
                   pl.BlockSpec((2, 128, BAND_KEYS), lambda j, i: (j, 0, 0))),
        scratch_shapes=[pltpu.VMEM((128, BAND_KEYS), F32), pltpu.VMEM((128, BAND_KEYS), F32),
                        pltpu.VMEM((128, BAND_KEYS), BF16), pltpu.VMEM((128, BAND_KEYS), BF16)],
        operands=(qkv, qkv, qkv, qkv, qkv, bias, dy_att, y_att, lse), plans=plans)


def _in_bwd(dq, dk, dv, duv, dgates, x, dx1, g1, w_in_t, plans=(), tm=512):
    S = x.shape[0]
    pad_blocks = (8 * CHUNK) // tm

    def body(dq_ref, dk_ref, dv_ref, duv_ref, dg_ref, x_ref, d1_ref, g_ref, wt_ref, dx_ref, dz_ref, dg1_ref):
        i = pl.program_id(0)
        dz_ref[:, 0:ATT_W] = (dq_ref[...].astype(F32) * 0.125).astype(BF16)
        dz_ref[:, ATT_W:2 * ATT_W] = dk_ref[...].astype(BF16)
        dz_ref[:, 2 * ATT_W:3 * ATT_W] = dv_ref[...].astype(BF16)
        dz_ref[:, 3 * ATT_W:3 * ATT_W + 2 * SG_W] = duv_ref[...]
        dz_ref[:, 3 * ATT_W + 2 * SG_W:IN_COLS] = dg_ref[...]
        dh = _dot(dz_ref[...], wt_ref[...])
        xv = x_ref[...]
        dx, dg1 = _rms_bwd(dh, xv, _rstd(xv), g_ref[...])
        dx_ref[...] = d1_ref[...] + dx

        @pl.when(i == 0)
        def _():
            dg1_ref[...] = jnp.zeros_like(dg1_ref)

        dg1_ref[...] += dg1

    shifted = pl.BlockSpec((tm, ATT_W), lambda i: (i + pad_blocks, 0))
    return _call(
        body, name="in_bwd", grid=(S // tm,),
        out_shape=(jax.ShapeDtypeStruct((S, D), F32), jax.ShapeDtypeStruct((S, IN_COLS), BF16),
                   jax.ShapeDtypeStruct((1, D), F32)),
        in_specs=[_rows(tm, ATT_W), shifted, shifted, _rows(tm, 2 * SG_W), _rows(tm, 2 * D), _rows(tm, D),
                  _rows(tm, D), _resident((1, D)), _resident((IN_COLS, D))],
        out_specs=(_rows(tm, D), _rows(tm, IN_COLS), pl.BlockSpec((1, D), lambda i: (0, 0))),
        operands=(dq, dk, dv, duv, dgates, x, dx1, g1, w_in_t), plans=plans)


def _adam_math(w, g, m, v):
    m = ADAM_B1 * m + (1.0 - ADAM_B1) * g
    v = ADAM_B2 * v + (1.0 - ADAM_B2) * (g * g)
    m_hat = m / (1.0 - ADAM_B1 ** ADAM_STEP)
    v_hat = v / (1.0 - ADAM_B2 ** ADAM_STEP)
    delta = -ADAM_LR * (m_hat / (jnp.sqrt(v_hat) + ADAM_EPS) + ADAM_WD * w)
    return delta, m, v


def _adam(parts, w, m, v, tr, name, transposed, after=None):
    P = parts.shape[0]
    R, C = w.shape

    def body(p_ref, w_ref, m_ref, v_ref, *rest):
        g_ref, d_ref, mo_ref, vo_ref = rest[-4:]
        if transposed:
            eye = (lax.broadcasted_iota(jnp.int32, (C, C), 0) == lax.broadcasted_iota(jnp.int32, (C, C), 1)).astype(BF16)
            part = lambda k: _dot_tn(p_ref[k], eye)
        else:
            part = lambda k: p_ref[k].astype(F32)
        g = part(0)
        for k in range(1, P):
            g = g + part(k)
        g_ref[...] = g
        d_ref[...], mo_ref[...], vo_ref[...] = _adam_math(w_ref[...], g, m_ref[...], v_ref[...])

    row = pl.BlockSpec((tr, C), lambda i: (i, 0))
    p_spec = pl.BlockSpec((P, C, tr), lambda i: (0, 0, i)) if transposed else pl.BlockSpec((P, tr, C), lambda i: (0, i, 0))
    extra = [] if after is None else [after]
    return pl.pallas_call(
        body, name=name, grid=(R // tr,),
        out_shape=tuple(jax.ShapeDtypeStruct((R, C), F32) for _ in range(4)),
        in_specs=[p_spec, row, row, row] + [pl.BlockSpec(memory_space=pl.ANY)] * len(extra),
        out_specs=(row, row, row, row),
        compiler_params=_cparams(1),
    )(parts, w, m, v, *extra)


def _my_place():
    return lax.axis_index("x"), lax.axis_index("y"), lax.axis_index("c")


def _other_chips(x, y):
    return [(1 - x, y), (x, 1 - y), (1 - x, 1 - y)]


class _Plan:
    def __init__(self, arrays, out_shapes, sems, start, finish, forward=None, forward_at=0.7):
        self.arrays, self.out_shapes, self.sems = list(arrays), list(out_shapes), list(sems)
        self.start, self.finish, self.forward, self.forward_at = start, finish, forward, forward_at


def _call(body, *, name, grid, in_specs, out_specs, out_shape, operands, scratch_shapes=(), plans=()):
    n_in, n_out, n_scr = len(operands), len(out_shape), len(scratch_shapes)
    p_in = [a for p in plans for a in p.arrays]
    p_out = [s for p in plans for s in p.out_shapes]
    p_sem = [s for p in plans for s in p.sems]
    steps = math.prod(grid)

    def wrapped(*refs):
        ins, refs = refs[:n_in], refs[n_in:]
        pins, refs = refs[:len(p_in)], refs[len(p_in):]
        outs, refs = refs[:n_out], refs[n_out:]
        pouts, refs = refs[:len(p_out)], refs[len(p_out):]
        scr, psems = refs[:n_scr], refs[n_scr:]
        step = 0
        for axis, size in enumerate(grid):
            step = step * size + pl.program_id(axis)
        bound = []
        for p in plans:
            bound.append((p, pins[:len(p.arrays)], pouts[:len(p.out_shapes)], psems[:len(p.sems)]))
            pins, pouts, psems = pins[len(p.arrays):], pouts[len(p.out_shapes):], psems[len(p.sems):]

        @pl.when(step == 0)
        def _():
            for p, a, b, s in bound:
                p.start(a, b, s)

        body(*ins, *outs, *scr)

        for p, a, b, s in bound:
            if p.forward is not None:
                @pl.when(step == min(int(p.forward_at * steps), steps - 1))
                def _(p=p, a=a, b=b, s=s):
                    p.forward(a, b, s)

        @pl.when(step == steps - 1)
        def _():
            for p, a, b, s in bound:
                p.finish(a, b, s)

    hbm = pl.BlockSpec(memory_space=pl.ANY)
    return pl.pallas_call(
        wrapped if plans else body, name=name, grid=grid,
        out_shape=tuple(out_shape) + tuple(p_out),
        in_specs=list(in_specs) + [hbm] * len(p_in),
        out_specs=tuple(out_specs) + tuple([hbm] * len(p_out)),
        scratch_shapes=list(scratch_shapes) + p_sem,
        compiler_params=_cparams(len(grid)),
    )(*operands, *p_in)


def _run_plan(plan, name):
    n_in, n_out = len(plan.arrays), len(plan.out_shapes)

    def body(*refs):
        a, b, s = refs[:n_in], refs[n_in:n_in + n_out], refs[n_in + n_out:]
        plan.start(a, b, s)
        if plan.forward is not None:
            plan.forward(a, b, s)
        plan.finish(a, b, s)

    hbm = pl.BlockSpec(memory_space=pl.ANY)
    return pl.pallas_call(
        body, name=name, out_shape=tuple(plan.out_shapes),
        in_specs=[hbm] * n_in, out_specs=tuple([hbm] * n_out), scratch_shapes=plan.sems,
    )(*plan.arrays)


def _gather_plan(shards, forward_at=0.7):
    n = len(shards)

    def copies(ins, outs, sems):
        send_sems, recv_sems, local_sems = sems
        x, y, c = _my_place()
        me, sibling = (x, y, c), (x, y, 1 - c)
        chips = _other_chips(x, y)

        def idx(p):
            return 4 * p[0] + 2 * p[1] + p[2]

        def copy(a, k, block, to, src=None):
            return pltpu.make_async_remote_copy(
                src_ref=outs[a].at[idx(block)] if src is None else src, dst_ref=outs[a].at[idx(block)],
                send_sem=send_sems.at[a, k], recv_sem=recv_sems.at[a, k], device_id=to, device_id_type=MESH)

        mine = [pltpu.make_async_copy(ins[a], outs[a].at[idx(me)], local_sems.at[a]) for a in range(n)]
        first = []
        for a in range(n):
            first.append(copy(a, 0, me, sibling, src=ins[a]))
            first += [copy(a, 1 + j, me, (*chip, c), src=ins[a]) for j, chip in enumerate(chips)]
        arrived = [copy(a, 1 + j, (*chip, c), me) for j, chip in enumerate(chips) for a in range(n)]
        passed = [copy(a, 4 + j, (*chip, c), sibling) for j, chip in enumerate(chips) for a in range(n)]
        from_sibling = []
        for a in range(n):
            from_sibling.append(copy(a, 0, sibling, me))
            from_sibling += [copy(a, 4 + j, (*chip, 1 - c), me) for j, chip in enumerate(chips)]
        return mine, first, arrived, passed, from_sibling

    def start(ins, outs, sems):
        mine, first, _, _, _ = copies(ins, outs, sems)
        for cp in mine + first:
            cp.start()

    def forward(ins, outs, sems):
        _, _, arrived, passed, _ = copies(ins, outs, sems)
        for landed, onward in zip(arrived, passed):
            landed.wait_recv()
            onward.start()

    def finish(ins, outs, sems):
        mine, first, _, passed, from_sibling = copies(ins, outs, sems)
        for cp in from_sibling:
            cp.wait_recv()
        for cp in first + passed:
            cp.wait_send()
        for cp in mine:
            cp.wait()

    return _Plan(shards, [jax.ShapeDtypeStruct((N_DEV,) + s.shape, s.dtype) for s in shards],
                 [pltpu.SemaphoreType.DMA((n, 7)), pltpu.SemaphoreType.DMA((n, 7)), pltpu.SemaphoreType.DMA((n,))],
                 start, finish, forward, forward_at)


def _sibling_plan(scatter, whole=()):
    ns = len(scatter)
    n = ns + len(whole)

    def copies(ins, outs, sems):
        send_sems, recv_sems = sems
        x, y, c = _my_place()
        out = []
        for a in range(n):
            for k in range(4 if a < ns else 1):
                src = ins[a].at[2 * k + (1 - c)] if a < ns else ins[a]
                dst = outs[a].at[k] if a < ns else outs[a]
                out.append(pltpu.make_async_remote_copy(
                    src_ref=src, dst_ref=dst, send_sem=send_sems.at[a, k], recv_sem=recv_sems.at[a, k],
                    device_id=(x, y, 1 - c), device_id_type=MESH))
        return out

    def start(ins, outs, sems):
        for cp in copies(ins, outs, sems):
            cp.start()

    def finish(ins, outs, sems):
        for cp in copies(ins, outs, sems):
            cp.wait()

    shapes = [jax.ShapeDtypeStruct((4,) + p.shape[1:], p.dtype) for p in scatter]
    shapes += [jax.ShapeDtypeStruct(p.shape, p.dtype) for p in whole]
    return _Plan(list(scatter) + list(whole), shapes,
                 [pltpu.SemaphoreType.DMA((n, 4)), pltpu.SemaphoreType.DMA((n, 4))], start, finish)


def _pair_sum(mine, theirs, c_idx, name):
    _, R, C = mine.shape

    def body(c_ref, a_ref, b_ref, o_ref):
        o_ref[...] = (a_ref[...].astype(F32) + b_ref[...].astype(F32)).astype(o_ref.dtype)

    grid_spec = pltpu.PrefetchScalarGridSpec(
        num_scalar_prefetch=1, grid=(4,),
        in_specs=[pl.BlockSpec((1, R, C), lambda k, c_ref: (2 * k + c_ref[0], 0, 0)),
                  pl.BlockSpec((1, R, C), lambda k, c_ref: (k, 0, 0))],
        out_specs=pl.BlockSpec((1, R, C), lambda k, c_ref: (k, 0, 0)))
    return pl.pallas_call(
        body, name=name, grid_spec=grid_spec,
        out_shape=jax.ShapeDtypeStruct((4, R, C), mine.dtype),
        compiler_params=_cparams(1),
    )(c_idx, mine, theirs)


def _peers_plan(arrays):
    n = len(arrays)

    def copies(ins, outs, sems):
        send_sems, recv_sems, local_sems = sems
        x, y, c = _my_place()
        me = 4 * x + 2 * y + c
        out = [pltpu.make_async_copy(ins[a], outs[a].at[me], local_sems.at[a]) for a in range(n)]
        for a in range(n):
            for k in range(N_DEV - 1):
                bits = k + 1
                peer = (x ^ (bits >> 2), y ^ ((bits >> 1) & 1), c ^ (bits & 1))
                out.append(pltpu.make_async_remote_copy(
                    src_ref=ins[a], dst_ref=outs[a].at[me], send_sem=send_sems.at[a, k], recv_sem=recv_sems.at[a, k],
                    device_id=peer, device_id_type=MESH))
        return out

    def start(ins, outs, sems):
        for cp in copies(ins, outs, sems):
            cp.start()

    def finish(ins, outs, sems):
        for cp in copies(ins, outs, sems):
            cp.wait()

    return _Plan(list(arrays), [jax.ShapeDtypeStruct((N_DEV,) + a.shape, a.dtype) for a in arrays],
                 [pltpu.SemaphoreType.DMA((n, N_DEV - 1)), pltpu.SemaphoreType.DMA((n, N_DEV - 1)),
                  pltpu.SemaphoreType.DMA((n,))], start, finish)


def _chips_plan(scatter, whole=()):
    ns = len(scatter)
    n = ns + len(whole)

    def copies(ins, outs, sems):
        send_sems, recv_sems, local_sems = sems
        x, y, c = _my_place()
        my_chip = 2 * x + y

        def src(a, k):
            return ins[a].at[k] if a < ns else ins[a]

        local = [pltpu.make_async_copy(src(a, my_chip), outs[a].at[my_chip], local_sems.at[a]) for a in range(n)]
        remote = []
        for a in range(n):
            for j, (px, py) in enumerate(_other_chips(x, y)):
                remote.append(pltpu.make_async_remote_copy(
                    src_ref=src(a, 2 * px + py), dst_ref=outs[a].at[my_chip],
                    send_sem=send_sems.at[a, j], recv_sem=recv_sems.at[a, j],
                    device_id=(px, py, c), device_id_type=MESH))
        return local + remote

    def start(ins, outs, sems):
        for cp in copies(ins, outs, sems):
            cp.start()

    def finish(ins, outs, sems):
        for cp in copies(ins, outs, sems):
            cp.wait()

    shapes = [jax.ShapeDtypeStruct(s.shape, s.dtype) for s in scatter]
    shapes += [jax.ShapeDtypeStruct((4,) + s.shape, s.dtype) for s in whole]
    return _Plan(list(scatter) + list(whole), shapes,
                 [pltpu.SemaphoreType.DMA((n, 3)), pltpu.SemaphoreType.DMA((n, 3)), pltpu.SemaphoreType.DMA((n,))],
                 start, finish)


def _chip_copies(src_ref, land_ref, send_sems, recv_sems):
    x, y, c = _my_place()
    return [pltpu.make_async_remote_copy(
        src_ref=src_ref.at[2 * px + py], dst_ref=land_ref.at[2 * x + y], send_sem=send_sems.at[j],
        recv_sem=recv_sems.at[j], device_id=(px, py, c), device_id_type=MESH)
        for j, (px, py) in enumerate(_other_chips(x, y))]


def _chips_start(src, name):
    def body(src_ref, land_ref, send_sems, recv_sems, src_thru, land_thru, token):
        for cp in _chip_copies(src_ref, land_ref, send_sems, recv_sems):
            cp.start()
        token[...] = jnp.zeros_like(token)

    hbm = pl.BlockSpec(memory_space=pltpu.HBM)
    sem = pl.BlockSpec(memory_space=pltpu.SEMAPHORE)
    return pl.pallas_call(
        body, name=name,
        out_shape=(pltpu.SemaphoreType.DMA((3,)), pltpu.SemaphoreType.DMA((3,)), pltpu.HBM(src.shape, src.dtype),
                   pltpu.HBM(src.shape, src.dtype), jax.ShapeDtypeStruct((8, LANES), F32)),
        in_specs=(hbm, hbm), out_specs=(sem, sem, hbm, hbm, pl.BlockSpec(memory_space=pltpu.VMEM)),
        input_output_aliases={0: 2, 1: 3},
        compiler_params=pltpu.CompilerParams(has_side_effects=pltpu.SideEffectType.DATAFLOW_SIDE_EFFECTING),
    )(pltpu.with_memory_space_constraint(src, pltpu.HBM),
      pltpu.with_memory_space_constraint(jnp.zeros(src.shape, src.dtype), pltpu.HBM))


def _chips_wait(send_sems, recv_sems, src_thru, land_thru, after, name):
    def body(src_ref, land_ref, send_sems, recv_sems, after_ref, src_dead, got_ref):
        for cp in _chip_copies(src_ref, land_ref, send_sems, recv_sems):
            cp.wait_send()
            cp.wait_recv()

    hbm = pl.BlockSpec(memory_space=pltpu.HBM)
    sem = pl.BlockSpec(memory_space=pltpu.SEMAPHORE)
    return pl.pallas_call(
        body, name=name,
        out_shape=(pltpu.HBM(src_thru.shape, src_thru.dtype), pltpu.HBM(land_thru.shape, land_thru.dtype)),
        in_specs=(hbm, hbm, sem, sem, pl.BlockSpec(memory_space=pl.ANY)), out_specs=(hbm, hbm),
        input_output_aliases={0: 0, 1: 1},
        compiler_params=pltpu.CompilerParams(has_side_effects=pltpu.SideEffectType.DATAFLOW_SIDE_EFFECTING),
    )(src_thru, land_thru, send_sems, recv_sems, after)


def _small_update(parts, w, m, v, loss_parts, name):
    n = len(parts)

    def total(ref):
        acc = ref[0]
        for k in range(1, ref.shape[0]):
            acc = acc + ref[k]
        return acc

    def body(*refs):
        p_refs, w_refs, m_refs, v_refs = (refs[i * n:(i + 1) * n] for i in range(4))
        lp_ref = refs[4 * n]
        outs = refs[4 * n + 1:]
        g_refs, d_refs, mo_refs, vo_refs = (outs[i * n:(i + 1) * n] for i in range(4))
        for a in range(n):
            g = total(p_refs[a])
            g_refs[a][...] = g
            d_refs[a][...], mo_refs[a][...], vo_refs[a][...] = _adam_math(w_refs[a][...], g, m_refs[a][...],
                                                                          v_refs[a][...])
        outs[4 * n][...] = total(lp_ref)

    vm = pl.BlockSpec(memory_space=pltpu.VMEM)
    shapes = [jax.ShapeDtypeStruct(t.shape, F32) for _ in range(4) for t in w]
    shapes.append(jax.ShapeDtypeStruct(loss_parts.shape[1:], F32))
    outs = pl.pallas_call(
        body, name=name, out_shape=tuple(shapes),
        in_specs=[vm] * (4 * n + 1), out_specs=tuple([vm] * (4 * n + 1)),
        compiler_params=pltpu.CompilerParams(vmem_limit_bytes=VMEM_LIMIT),
    )(*parts, *w, *m, *v, loss_parts)
    return outs[0:n], outs[n:2 * n], outs[2 * n:3 * n], outs[3 * n:4 * n], outs[4 * n]


BIG = [("w_in", 1), ("w_branch_att", 1), ("w_branch_sg", 1), ("w_out", 0), ("w_xq", 0), ("w_xkv", 1), ("w_xo", 0),
       ("w_ffn_in", 1), ("w_ffn_out", 0)]
SMALL = [("norm_mix_g", (1, D)), ("rel_bias", (8, NREL)), ("sg_ln_g", (8, 64)), ("sg_ln_b", (8, 64)),
         ("sg_w", (8, 128, 128)), ("sg_b", (8, 128)), ("norm_xattn_g", (1, D)), ("norm_mem_g", (1, D)),
         ("norm_ffn_g", (1, D)), ("norm_final_g", (1, D))]
ADAM_ROWS = {"w_in": 192, "w_branch_att": 512, "w_branch_sg": 512, "w_xkv": 1024, "w_ffn_in": 176}


def _full(gathered):
    return gathered.reshape(N_DEV * gathered.shape[1], gathered.shape[2])


def _blocks(grad):
    return grad.reshape(N_DEV, grad.shape[0] // N_DEV, grad.shape[1])


def kernel(x, mem, norm_mix_g, w_in, rel_bias, sg_ln_g, sg_ln_b, sg_w, sg_b, w_branch_att, w_branch_sg, w_out, norm_xattn_g, norm_mem_g, w_xq, w_xkv, w_xo, norm_ffn_g, w_ffn_in, w_ffn_out, norm_final_g, loss_target, m_norm_mix_g, m_w_in, m_rel_bias, m_sg_ln_g, m_sg_ln_b, m_sg_w, m_sg_b, m_w_branch_att, m_w_branch_sg, m_w_out, m_norm_xattn_g, m_norm_mem_g, m_w_xq, m_w_xkv, m_w_xo, m_norm_ffn_g, m_w_ffn_in, m_w_ffn_out, m_norm_final_g, v_norm_mix_g, v_w_in, v_rel_bias, v_sg_ln_g, v_sg_ln_b, v_sg_w, v_sg_b, v_w_branch_att, v_w_branch_sg, v_w_out, v_norm_xattn_g, v_norm_mem_g, v_w_xq, v_w_xkv, v_w_xo, v_norm_ffn_g, v_w_ffn_in, v_w_ffn_out, v_norm_final_g):
    args = dict(locals())
    big_names = [n for n, _ in BIG]
    small_names = [n for n, _ in SMALL]
    S = x.shape[1]

    x, mem, tgt = x.reshape(S, D), mem.reshape(MEM, D), loss_target.reshape(S, D)
    small = {n: args[n].reshape(shape) for n, shape in SMALL}
    g1, g2, g3 = small["norm_mix_g"], small["norm_xattn_g"], small["norm_ffn_g"]
    g_mem, g4 = small["norm_mem_g"], small["norm_final_g"]
    lng = small["sg_ln_g"].reshape(1, SG_W)
    lnb = small["sg_ln_b"].reshape(1, SG_W)
    b_exp = jnp.broadcast_to(small["sg_b"].T[:, :, None], (128, 8, 64)).reshape(128, SG_W)
    rel_pad = jnp.pad(small["rel_bias"], ((0, 0), (0, 384 - NREL)))
    c_idx = lax.axis_index("c").astype(jnp.int32).reshape(1)

    shard = {n: (args[n][0].T if axis == 1 else args[n][0]).astype(BF16) for n, axis in BIG}
    h, w_in_gathered = _norm_in(x, g1, plans=[_gather_plan([shard["w_in"]])])
    w_in_t = _full(w_in_gathered)
    bias = _bias_table(rel_pad)
    mix_names = ["w_branch_att", "w_branch_sg", "w_out", "w_xq", "w_xkv", "w_xo"]
    qkv, uv, gates, *got = _in_proj(h, w_in_t, plans=[_gather_plan([shard[n] for n in mix_names])])
    wba_t, wbs_t, w_out_f, w_xq_f, w_xkv_t, w_xo_f = (_full(g) for g in got)
    y_att, lse, *got = _attn_fwd(qkv, bias, plans=[_gather_plan([shard["w_ffn_in"], shard["w_ffn_out"]])])
    w_ffn_in_t, w_ffn_out_f = (_full(g) for g in got)
    y_sg = _sgu_fwd(uv, lng, lnb, small["sg_w"], b_exp)
    x1 = _merge_fwd(x, y_att, y_sg, gates, wba_t, wbs_t, w_out_f)
    kv, mn = _mem_kv(mem, g_mem, w_xkv_t)
    x2, hx, qx, o_x = _xattn_fwd(x1, g2, w_xq_f, kv, w_xo_f)
    dx3, gu, hf, act, loss_part, dg4 = _ffn_fwd(x2, tgt, g3, w_ffn_in_t, w_ffn_out_f, g4)

    def pair_sums(names, mine, theirs):
        return [_pair_sum(a, b, c_idx, "rs_pair_" + n) for n, a, b in zip(names, mine, theirs)]

    dx2, dgu, dg3 = _ffn_bwd(dx3, gu, x2, g3, w_ffn_out_f, w_ffn_in_t)
    ffn_names = ["w_ffn_out", "w_ffn_in"]
    ffn_mine = [_blocks(_dw(act, dx3, 1408, 1024, DW_TOKENS, "dw_ffn_out")),
                _blocks(_dw(dgu, hf, 1408, 1024, DW_TOKENS, "dw_ffn_in"))]
    dx1, dq_x, dkv, dg2, *ffn_theirs = _xattn_bwd(dx2, x1, qx, g2, w_xq_f, w_xo_f, kv,
                                                  plans=[_sibling_plan(ffn_mine)])
    ffn_chip = pair_sums(ffn_names, ffn_mine, ffn_theirs)
    d_xkv, dg_mem = _mem_kv_bwd(dkv, mem, g_mem, mn, w_xkv_t)
    merged, d_a, d_b, dy_att, dy_sg, dgates = _merge_bwd(dx1, y_att, y_sg, gates, wba_t, wbs_t, w_out_f)
    mid_names = ["w_xo", "w_xq", "w_xkv", "w_out", "w_branch_att", "w_branch_sg"]
    mid_mine = [_blocks(g) for g in (
        _dw(o_x, dx2, 1024, 1024, DW_TOKENS, "dw_xo"), _dw(hx, dq_x, 1024, 1024, DW_TOKENS, "dw_xq"), d_xkv,
        _dw(merged, dx1, 1024, 1024, DW_TOKENS, "dw_out"), _dw(d_a, y_att, 1024, 512, DW_TOKENS, "dw_branch_att"),
        _dw(d_b, y_sg, 1024, 512, DW_TOKENS, "dw_branch_sg"))]
    duv, d_sgw, d_bx, d_lng, d_lnb, *got = _sgu_bwd(uv, dy_sg, lng, lnb, small["sg_w"], b_exp,
                                                    plans=[_chips_plan(ffn_chip), _sibling_plan(mid_mine)])
    ffn_all, mid_theirs = got[:2], got[2:]
    mid_chip = pair_sums(mid_names, mid_mine, mid_theirs)
    dq, dk, dv, ds_sum, *mid_all = _attn_bwd(qkv, dy_att, y_att, lse, bias, plans=[_chips_plan(mid_chip)])
    d_rel = _bias_grad(ds_sum)

    grad_x, dz, dg1 = _in_bwd(dq, dk, dv, duv, dgates, x, dx1, g1, w_in_t)

    gs = {"norm_mix_g": dg1, "rel_bias": d_rel[:, :NREL], "sg_ln_g": d_lng.reshape(8, 64),
          "sg_ln_b": d_lnb.reshape(8, 64), "sg_w": d_sgw, "sg_b": d_bx.reshape(128, 8, 64)[:, :, 0].T,
          "norm_xattn_g": dg2, "norm_mem_g": dg_mem, "norm_ffn_g": dg3, "norm_final_g": dg4}
    d_in, *everyone = _dw(dz, h, 1152, 1024, DW_TOKENS, "dw_in",
                          plans=[_peers_plan([gs[n] for n in small_names] + [loss_part])])

    in_mine = [_blocks(d_in)]
    (in_theirs,) = _run_plan(_sibling_plan(in_mine), "rs_sibling")
    (in_chip,) = pair_sums(["w_in"], in_mine, [in_theirs])
    send_sems, recv_sems, in_chip_thru, landing, token = _chips_start(in_chip, "rs_chips_start")
    all_parts = dict(zip(ffn_names + mid_names, list(ffn_all) + list(mid_all)))

    def adam(n, axis, parts, after):
        wmv = [args[p + n][0] for p in ("", "m_", "v_")]
        if axis == 1 and wmv[0].shape[1] % LANES != 0:
            outs = _adam(parts, *(t.T for t in wmv), ADAM_ROWS[n], "adam_" + n, transposed=False, after=after)
            return [t.T[None] for t in outs]
        tr = ADAM_ROWS[n] if axis == 1 else wmv[0].shape[0]
        return [t[None] for t in _adam(parts, *wmv, tr, "adam_" + n, transposed=(axis == 1), after=after)]

    res = {n: adam(n, axis, all_parts[n], token) for n, axis in BIG if n != "w_in"}
    in_chip, landing = _chips_wait(send_sems, recv_sems, in_chip_thru, landing, res["w_ffn_in"][1], "rs_chips_wait")
    my_chip = 2 * lax.axis_index("x") + lax.axis_index("y")
    own = lax.dynamic_slice_in_dim(in_chip, my_chip, 1, axis=0)
    res["w_in"] = adam("w_in", 1, lax.dynamic_update_slice_in_dim(landing, own, my_chip, axis=0), None)
    small_res = _small_update(
        everyone[:-1], [small[n] for n in small_names],
        [args["m_" + n].reshape(s) for n, s in SMALL], [args["v_" + n].reshape(s) for n, s in SMALL],
        everyone[-1], "adam_small")
    for i, n in enumerate(small_names):
        res[n] = [small_res[k][i].reshape(args[n].shape) for k in range(4)]
    loss = small_res[4][0, 0]

    order = ["norm_mix_g", "w_in", "rel_bias", "sg_ln_g", "sg_ln_b", "sg_w", "sg_b", "w_branch_att", "w_branch_sg",
             "w_out", "norm_xattn_g", "norm_mem_g", "w_xq", "w_xkv", "w_xo", "norm_ffn_g", "w_ffn_in", "w_ffn_out",
             "norm_final_g"]
    outs = [loss, grad_x.reshape(1, S, D)]
    for k in range(4):
        outs += [res[n][k] for n in order]
    return tuple(outs)
```

```python
import math

import jax
import jax.numpy as jnp
from jax import lax
from jax.experimental import pallas as pl
from jax.experimental.pallas import tpu as pltpu

F32 = jnp.float32
BF16 = jnp.bfloat16

D = 1024
ATT_W = 512
SG_W = 512
IN_COLS = 4608
DFF = 2816
MEM = 256
XH = 4
CHUNK = 64
BAND_KEYS = 640
ATT_R = 512
ATT_SUBS = ATT_R // 128
ATT_ROWS = 32
ATT_ROWS_BWD = 16
DW_TOKENS = 2048
REL_CLIP = 128
NREL = 2 * REL_CLIP + 1
EPS = 1e-6
NEG = -1e30
N_DEV = 8

ADAM_LR = 0.001
ADAM_B1 = 0.9
ADAM_B2 = 0.999
ADAM_EPS = 1e-08
ADAM_WD = 0.01
ADAM_STEP = 10

LANES = 128
VMEM_LIMIT = 56 * 1024 * 1024
MESH = pl.DeviceIdType.MESH


def _cparams(n_axes):
    return pltpu.CompilerParams(dimension_semantics=("arbitrary",) * n_axes, vmem_limit_bytes=VMEM_LIMIT)


def _resident(shape):
    zeros = (0,) * len(shape)
    return pl.BlockSpec(shape, lambda *_: zeros, pipeline_mode=pl.Buffered(1))


def _rows(tm, cols, col_block=0):
    return pl.BlockSpec((tm, cols), lambda i: (i, col_block))


def _sigmoid(x):
    return pl.reciprocal(1.0 + jnp.exp(-x), approx=True)


_GELU_C = math.sqrt(2.0 / math.pi)


def _gelu(x):
    t = jnp.tanh(_GELU_C * (x + 0.044715 * (x * x * x)))
    return x * (0.5 * (1.0 + t))


def _gelu_and_grad(x):
    x2 = x * x
    t = jnp.tanh(_GELU_C * (x + 0.044715 * (x2 * x)))
    cdf = 0.5 * (1.0 + t)
    dcdf = 0.5 * (1.0 - t * t) * (_GELU_C * (1.0 + 3.0 * 0.044715 * x2))
    return x * cdf, cdf + x * dcdf


def _rstd(x):
    return lax.rsqrt(jnp.mean(x * x, axis=-1, keepdims=True) + EPS)


def _rms_bwd(dh, x, r, g):
    xh = x * r
    dxh = dh * g
    dx = r * (dxh - xh * jnp.mean(dxh * xh, axis=-1, keepdims=True))
    dg = jnp.sum(dh * xh, axis=0, keepdims=True)
    return dx, dg


def _group_sum64(x):
    r = lax.broadcasted_iota(jnp.int32, (LANES, LANES), 0) // 64
    c = lax.broadcasted_iota(jnp.int32, (LANES, LANES), 1) // 64
    same_group = (r == c).astype(BF16)

    def one(v):
        hi = v.astype(BF16)
        rest = v - hi.astype(F32)
        mid = rest.astype(BF16)
        lo = (rest - mid.astype(F32)).astype(BF16)
        return _dot(hi, same_group) + _dot(mid, same_group) + _dot(lo, same_group)

    pieces = [one(x[:, LANES * j:LANES * (j + 1)]) for j in range(x.shape[1] // LANES)]
    return pieces[0] if len(pieces) == 1 else jnp.concatenate(pieces, axis=1)


def _dot(a, b):
    return jnp.dot(a, b, preferred_element_type=F32)


def _dot_nt(a, b):
    return lax.dot_general(a, b, (((1,), (1,)), ((), ())), preferred_element_type=F32)


def _dot_tn(a, b):
    return lax.dot_general(a, b, (((0,), (0,)), ((), ())), preferred_element_type=F32)


DIAGS = 768


def _diag_onehot():
    r_idx = lax.broadcasted_iota(jnp.int32, (384, DIAGS), 0)
    t_idx = lax.broadcasted_iota(jnp.int32, (384, DIAGS), 1)
    dist = (8 * CHUNK + 127) - t_idx
    return (jnp.clip(dist, -REL_CLIP, REL_CLIP) + REL_CLIP == r_idx).astype(F32)


def _shift_rows(x, reverse):
    row = lax.broadcasted_iota(jnp.int32, x.shape, 0)
    for k in range(7):
        amt = (DIAGS - (1 << k)) if reverse else (1 << k)
        x = jnp.where(((row >> k) & 1) == 1, pltpu.roll(x, amt, 1), x)
    return x


N_TABLES = 1 + ATT_SUBS


def _bias_table(rel_bias_pad):
    def body(rb_ref, out_ref):
        per_diag = jnp.dot(rb_ref[...], _diag_onehot(), preferred_element_type=F32,
                           precision=lax.Precision.HIGHEST)
        a = lax.broadcasted_iota(jnp.int32, (128, BAND_KEYS), 0)
        b = lax.broadcasted_iota(jnp.int32, (128, BAND_KEYS), 1)
        band = (b // CHUNK >= a // CHUNK) & (b // CHUNK <= a // CHUNK + 8)
        for h in range(8):
            rows = jnp.broadcast_to(per_diag[h:h + 1, :], (128, DIAGS))
            table = _shift_rows(pltpu.roll(rows, DIAGS - 127, 1), reverse=False)[:, :BAND_KEYS]
            out_ref[0, h] = jnp.where(band, table, NEG)
            for s in range(ATT_SUBS):
                out_ref[1 + s, h] = jnp.where(band & (b >= 8 * CHUNK - 128 * s), table, NEG)

    return pl.pallas_call(
        body, name="bias_table",
        out_shape=jax.ShapeDtypeStruct((N_TABLES, 8, 128, BAND_KEYS), F32),
        in_specs=[pl.BlockSpec(memory_space=pltpu.VMEM)],
        out_specs=pl.BlockSpec(memory_space=pltpu.VMEM),
        compiler_params=pltpu.CompilerParams(vmem_limit_bytes=VMEM_LIMIT),
    )(rel_bias_pad)


def _bias_grad(ds_sum):
    def body(ds_ref, out_ref):
        sums = []
        for h in range(8):
            padded = jnp.concatenate([ds_ref[h], jnp.zeros((128, DIAGS - BAND_KEYS), F32)], axis=1)
            skewed = pltpu.roll(_shift_rows(padded, reverse=True), 127, 1)
            sums.append(jnp.sum(skewed, axis=0, keepdims=True))
        per_diag = jnp.concatenate(sums, axis=0)
        out_ref[...] = lax.dot_general(per_diag, _diag_onehot(), (((1,), (1,)), ((), ())),
                                       preferred_element_type=F32, precision=lax.Precision.HIGHEST)

    return pl.pallas_call(
        body, name="bias_grad",
        out_shape=jax.ShapeDtypeStruct((8, 384), F32),
        in_specs=[pl.BlockSpec(memory_space=pltpu.VMEM)],
        out_specs=pl.BlockSpec(memory_space=pltpu.VMEM),
    )(ds_sum)


def _norm_in(x, g1, plans=(), tm=512):
    S = x.shape[0]

    def body(x_ref, g_ref, h_ref):
        xv = x_ref[...]
        h_ref[...] = (xv * _rstd(xv) * g_ref[...]).astype(BF16)

    return _call(
        body, name="norm_in", grid=(S // tm,),
        out_shape=(jax.ShapeDtypeStruct((S, D), BF16),),
        in_specs=[_rows(tm, D), _resident((1, D))], out_specs=(_rows(tm, D),),
        operands=(x, g1), plans=plans)


def _in_proj(h, w_in_t, plans=(), tm=512):
    S = h.shape[0]

    def body(h_ref, w_ref, qkv_ref, uv_ref, gate_ref):
        h = h_ref[...]
        for c in range(IN_COLS // 512):
            zc = _dot_nt(h, w_ref[512 * c:512 * (c + 1), :])
            if c == 0:
                qkv_ref[:, 0:512] = (zc * 0.125).astype(BF16)
            elif c < 3:
                qkv_ref[:, 512 * c:512 * (c + 1)] = zc.astype(BF16)
            elif c < 5:
                uv_ref[:, 512 * (c - 3):512 * (c - 2)] = zc.astype(BF16)
            else:
                gate_ref[:, 512 * (c - 5):512 * (c - 4)] = zc.astype(BF16)

    return _call(
        body, name="in_proj", grid=(S // tm,),
        out_shape=(jax.ShapeDtypeStruct((S, 3 * ATT_W), BF16), jax.ShapeDtypeStruct((S, 2 * SG_W), BF16),
                   jax.ShapeDtypeStruct((S, 2 * D), BF16)),
        in_specs=[_rows(tm, D), _resident((IN_COLS, D))],
        out_specs=(_rows(tm, 3 * ATT_W), _rows(tm, 2 * SG_W), _rows(tm, 2 * D)),
        operands=(h, w_in_t), plans=plans)


def _two_heads(a, lo):
    return jnp.concatenate([jnp.where(lo, a, 0), jnp.where(lo, 0, a)], axis=0)


def _att_specs():
    R = ATT_R
    q = pl.BlockSpec((R, LANES), lambda j, i: (i, j))
    kp = pl.BlockSpec((R, LANES), lambda j, i: (jnp.maximum(i - 1, 0), 4 + j))
    kc = pl.BlockSpec((R, LANES), lambda j, i: (i, 4 + j))
    vp = pl.BlockSpec((R, LANES), lambda j, i: (jnp.maximum(i - 1, 0), 8 + j))
    vc = pl.BlockSpec((R, LANES), lambda j, i: (i, 8 + j))
    bias = pl.BlockSpec((N_TABLES, 2, 128, BAND_KEYS), lambda j, i: (0, j, 0, 0))
    return [q, kp, kc, vp, vc, bias]


def _attn_fwd(qkv, bias, plans=()):
    S = qkv.shape[0]
    R = ATT_R

    def body(q_ref, kp_ref, kc_ref, vp_ref, vc_ref, b_ref, o_ref, lse_ref, s_ref, p_ref, l_ref):
        i = pl.program_id(1)
        lo = lax.broadcasted_iota(jnp.int32, (1, LANES), 1) < 64
        kwin = jnp.concatenate([kp_ref[...], kc_ref[...]], axis=0)
        vwin = jnp.concatenate([vp_ref[...], vc_ref[...]], axis=0)
        for sub in range(ATT_SUBS):
            q2 = q_ref[128 * sub:128 * (sub + 1), :]
            kw = kwin[128 * sub:128 * sub + BAND_KEYS]
            vw = vwin[128 * sub:128 * sub + BAND_KEYS]
            table = jnp.where(i == 0, 1 + sub, 0)
            s_ref[...] = _dot_nt(_two_heads(q2, lo), kw)
            for hh in range(2):
                for r0 in range(0, 128, ATT_ROWS):
                    rows = slice(128 * hh + r0, 128 * hh + r0 + ATT_ROWS)
                    s = s_ref[rows, :] + b_ref[table, hh, r0:r0 + ATT_ROWS, :]
                    top = jnp.max(s, axis=-1, keepdims=True)
                    p = jnp.exp(s - top)
                    total = jnp.sum(p, axis=-1, keepdims=True)
                    p_ref[rows, :] = (p / total).astype(BF16)
                    l_ref[rows, :] = jnp.broadcast_to(top + jnp.log(total), (ATT_ROWS, LANES))
            o = _dot(jnp.concatenate([p_ref[0:128, :], p_ref[128:256, :]], axis=1), _two_heads(vw, lo))
            o_ref[128 * sub:128 * (sub + 1), :] = o.astype(BF16)
            lse_ref[128 * sub:128 * (sub + 1), :] = jnp.where(lo, l_ref[0:128, :], l_ref[128:256, :])

    blk = pl.BlockSpec((R, LANES), lambda j, i: (i, j))
    return _call(
        body, name="attn_fwd", grid=(4, S // R),
        out_shape=(jax.ShapeDtypeStruct((S, ATT_W), BF16), jax.ShapeDtypeStruct((S, ATT_W), F32)),
        in_specs=_att_specs(), out_specs=(blk, blk),
        scratch_shapes=[pltpu.VMEM((256, BAND_KEYS), F32), pltpu.VMEM((256, BAND_KEYS), BF16),
                        pltpu.VMEM((256, LANES), F32)],
        operands=(qkv, qkv, qkv, qkv, qkv, bias), plans=plans)


def _sg_mask():
    t = lax.broadcasted_iota(jnp.int32, (128, 128), 0)
    s = lax.broadcasted_iota(jnp.int32, (128, 128), 1)
    return (s // CHUNK) <= (t // CHUNK)


def _sg_layernorm(gv, lng, lnb):
    mu = _group_sum64(gv) * (1.0 / 64)
    xc = gv - mu
    var = _group_sum64(xc * xc) * (1.0 / 64)
    rstd = lax.rsqrt(var + EPS)
    vhat = xc * rstd
    return vhat * lng + lnb, vhat, rstd


def _sgu_fwd(uv, lng, lnb, sg_w, b_exp, tm=512):
    S = uv.shape[0]

    def body(uv_ref, lng_ref, lnb_ref, w_ref, b_ref, y_ref):
        lane = lax.broadcasted_iota(jnp.int32, (1, LANES), 1)
        lo = lane < 64
        mask = _sg_mask()
        gu = _gelu(uv_ref[:, 0:SG_W].astype(F32))
        vln, _, _ = _sg_layernorm(_gelu(uv_ref[:, SG_W:2 * SG_W].astype(F32)), lng_ref[...], lnb_ref[...])
        for gp in range(4):
            w0 = jnp.where(mask, w_ref[2 * gp], 0).astype(BF16)
            w1 = jnp.where(mask, w_ref[2 * gp + 1], 0).astype(BF16)
            cols = slice(128 * gp, 128 * (gp + 1))
            for n in range(tm // 128):
                rows = slice(128 * n, 128 * (n + 1))
                vl = vln[rows, cols]
                sv = (_dot(w0, jnp.where(lo, vl, 0).astype(BF16)) + _dot(w1, jnp.where(lo, 0, vl).astype(BF16))
                      + b_ref[:, cols])
                y_ref[rows, cols] = (gu[rows, cols] * sv).astype(BF16)

    return pl.pallas_call(
        body, name="sgu_fwd", grid=(S // tm,),
        out_shape=jax.ShapeDtypeStruct((S, SG_W), BF16),
        in_specs=[_rows(tm, 2 * SG_W), _resident((1, SG_W)), _resident((1, SG_W)),
                  _resident((8, 128, 128)), _resident((128, SG_W))],
        out_specs=_rows(tm, SG_W),
        compiler_params=_cparams(1),
    )(uv, lng, lnb, sg_w, b_exp)


def _merge_fwd(x, y_att, y_sg, gates, wba_t, wbs_t, w_out, tm=512):
    S = x.shape[0]

    def body(x_ref, ya_ref, ys_ref, g_ref, wba_ref, wbs_ref, wo_ref, x1_ref):
        a = _dot_nt(ya_ref[...], wba_ref[...])
        b = _dot_nt(ys_ref[...], wbs_ref[...])
        merged = _sigmoid(g_ref[:, 0:D].astype(F32)) * a + _sigmoid(g_ref[:, D:2 * D].astype(F32)) * b
        x1_ref[...] = x_ref[...] + _dot(merged.astype(BF16), wo_ref[...])

    return pl.pallas_call(
        body, name="merge_fwd", grid=(S // tm,),
        out_shape=jax.ShapeDtypeStruct((S, D), F32),
        in_specs=[_rows(tm, D), _rows(tm, ATT_W), _rows(tm, SG_W), _rows(tm, 2 * D),
                  _resident((D, ATT_W)), _resident((D, SG_W)), _resident((D, D))],
        out_specs=_rows(tm, D),
        compiler_params=_cparams(1),
    )(x, y_att, y_sg, gates, wba_t, wbs_t, w_out)


def _mem_kv(mem, g_mem, w_xkv_t):
    def body(m_ref, g_ref, w_ref, kv_ref, mn_ref):
        mv = m_ref[...]
        mn = (mv * _rstd(mv) * g_ref[...]).astype(BF16)
        mn_ref[...] = mn
        kv_ref[...] = _dot_nt(mn, w_ref[...]).astype(BF16)

    vm = pl.BlockSpec(memory_space=pltpu.VMEM)
    return pl.pallas_call(
        body, name="mem_kv",
        out_shape=(jax.ShapeDtypeStruct((MEM, 2 * D), BF16), jax.ShapeDtypeStruct((MEM, D), BF16)),
        in_specs=[vm, vm, vm], out_specs=(vm, vm),
        compiler_params=pltpu.CompilerParams(vmem_limit_bytes=VMEM_LIMIT),
    )(mem, g_mem, w_xkv_t)


def _xatt_head(qx, kv_ref, h):
    hs = slice(256 * h, 256 * (h + 1))
    s = _dot_nt(qx[:, hs], kv_ref[:, hs])
    p = jnp.exp(s - jnp.max(s, axis=-1, keepdims=True))
    return p / jnp.sum(p, axis=-1, keepdims=True)


def _xattn_fwd(x1, g2, w_xq, kv, w_xo, tm=512):
    S = x1.shape[0]

    def body(x_ref, g_ref, wq_ref, kv_ref, wo_ref, x2_ref, hx_ref, qx_ref, o_ref):
        xv = x_ref[...]
        hx = (xv * _rstd(xv) * g_ref[...]).astype(BF16)
        hx_ref[...] = hx
        qx = (_dot(hx, wq_ref[...]) * (1.0 / 16)).astype(BF16)
        qx_ref[...] = qx
        for h in range(XH):
            p = _xatt_head(qx, kv_ref, h)
            o_ref[:, 256 * h:256 * (h + 1)] = _dot(p.astype(BF16), kv_ref[:, D + 256 * h:D + 256 * (h + 1)]).astype(BF16)
        x2_ref[...] = xv + _dot(o_ref[...], wo_ref[...])

    return pl.pallas_call(
        body, name="xattn_fwd", grid=(S // tm,),
        out_shape=(jax.ShapeDtypeStruct((S, D), F32),) + (jax.ShapeDtypeStruct((S, D), BF16),) * 3,
        in_specs=[_rows(tm, D), _resident((1, D)), _resident((D, D)), _resident((MEM, 2 * D)), _resident((D, D))],
        out_specs=(_rows(tm, D),) * 4,
        compiler_params=_cparams(1),
    )(x1, g2, w_xq, kv, w_xo)


FF_CHUNK = 1408


def _ffn_fwd(x2, tgt, g3, w_ffn_in_t, w_ffn_out, g4, tm=256):
    S = x2.shape[0]

    def body(x_ref, t_ref, g3_ref, wi_ref, wo_ref, g4_ref, dx3_ref, gu_ref, hf_ref, act_ref, loss_ref, dg4_ref):
        i = pl.program_id(0)
        xv = x_ref[...]
        hf = (xv * _rstd(xv) * g3_ref[...]).astype(BF16)
        hf_ref[...] = hf
        acc = xv
        for c in range(DFF // FF_CHUNK):
            cs = slice(FF_CHUNK * c, FF_CHUNK * (c + 1))
            us = slice(DFF + FF_CHUNK * c, DFF + FF_CHUNK * (c + 1))
            gate = _dot_nt(hf, wi_ref[cs, :])
            up = _dot_nt(hf, wi_ref[us, :])
            gu_ref[:, cs] = gate.astype(BF16)
            gu_ref[:, us] = up.astype(BF16)
            act = ((gate * _sigmoid(gate)) * up).astype(BF16)
            act_ref[:, cs] = act
            acc = acc + _dot(act, wo_ref[cs, :])
        r4 = _rstd(acc)
        g4 = g4_ref[...]
        diff = acc * r4 * g4 - t_ref[...]
        dy = diff * (1.0 / D)
        dx3, dg4 = _rms_bwd(dy, acc, r4, g4)
        dx3_ref[...] = dx3
        part = 0.5 * jnp.sum(jnp.mean(diff * diff, axis=-1, keepdims=True))

        @pl.when(i == 0)
        def _():
            loss_ref[...] = jnp.zeros_like(loss_ref)
            dg4_ref[...] = jnp.zeros_like(dg4_ref)

        loss_ref[...] += jnp.full(loss_ref.shape, part, F32)
        dg4_ref[...] += dg4

    return pl.pallas_call(
        body, name="ffn_fwd", grid=(S // tm,),
        out_shape=(jax.ShapeDtypeStruct((S, D), F32), jax.ShapeDtypeStruct((S, 2 * DFF), BF16),
                   jax.ShapeDtypeStruct((S, D), BF16), jax.ShapeDtypeStruct((S, DFF), BF16),
                   jax.ShapeDtypeStruct((8, LANES), F32), jax.ShapeDtypeStruct((1, D), F32)),
        in_specs=[_rows(tm, D), _rows(tm, D), _resident((1, D)), _resident((2 * DFF, D)), _resident((DFF, D)),
                  _resident((1, D))],
        out_specs=(_rows(tm, D), _rows(tm, 2 * DFF), _rows(tm, D), _rows(tm, DFF),
                   pl.BlockSpec((8, LANES), lambda i: (0, 0)), pl.BlockSpec((1, D), lambda i: (0, 0))),
        compiler_params=_cparams(1),
    )(x2, tgt, g3, w_ffn_in_t, w_ffn_out, g4)


def _ffn_bwd(dx3, gu, x2, g3, w_ffn_out, w_ffn_in_t, tm=256):
    S = x2.shape[0]

    def body(d_ref, gu_ref, x_ref, g3_ref, wo_ref, wit_ref, dx2_ref, dgu_ref, dg3_ref):
        i = pl.program_id(0)
        d3 = d_ref[...]
        d3b = d3.astype(BF16)
        for c in range(DFF // FF_CHUNK):
            cs = slice(FF_CHUNK * c, FF_CHUNK * (c + 1))
            us = slice(DFF + FF_CHUNK * c, DFF + FF_CHUNK * (c + 1))
            da = _dot_nt(d3b, wo_ref[cs, :])
            gate = gu_ref[:, cs].astype(F32)
            up = gu_ref[:, us].astype(F32)
            sg = _sigmoid(gate)
            dgate = (da * up * (sg * (1.0 + gate * (1.0 - sg)))).astype(BF16)
            dup = (da * (gate * sg)).astype(BF16)
            dgu_ref[:, cs] = dgate
            dgu_ref[:, us] = dup
        dhf = _dot(dgu_ref[...], wit_ref[...])
        xv = x_ref[...]
        dx, dg3 = _rms_bwd(dhf, xv, _rstd(xv), g3_ref[...])
        dx2_ref[...] = d3 + dx

        @pl.when(i == 0)
        def _():
            dg3_ref[...] = jnp.zeros_like(dg3_ref)

        dg3_ref[...] += dg3

    return pl.pallas_call(
        body, name="ffn_bwd", grid=(S // tm,),
        out_shape=(jax.ShapeDtypeStruct((S, D), F32), jax.ShapeDtypeStruct((S, 2 * DFF), BF16),
                   jax.ShapeDtypeStruct((1, D), F32)),
        in_specs=[_rows(tm, D), _rows(tm, 2 * DFF), _rows(tm, D), _resident((1, D)),
                  _resident((DFF, D)), _resident((2 * DFF, D))],
        out_specs=(_rows(tm, D), _rows(tm, 2 * DFF), pl.BlockSpec((1, D), lambda i: (0, 0))),
        compiler_params=_cparams(1),
    )(dx3, gu, x2, g3, w_ffn_out, w_ffn_in_t)


def _dw(a, b, tmm, tn, ts, name, out_dtype=BF16, plans=()):
    S, M = a.shape
    N = b.shape[1]
    ts = min(ts, S)
    nk = S // ts

    def body(a_ref, b_ref, o_ref, acc_ref):
        k = pl.program_id(2)

        @pl.when(k == 0)
        def _():
            acc_ref[...] = jnp.zeros_like(acc_ref)

        acc_ref[...] += _dot_tn(a_ref[...].astype(BF16), b_ref[...].astype(BF16))

        @pl.when(k == nk - 1)
        def _():
            o_ref[...] = acc_ref[...].astype(out_dtype)

    out = _call(
        body, name=name, grid=(M // tmm, N // tn, nk),
        out_shape=(jax.ShapeDtypeStruct((M, N), out_dtype),),
        in_specs=[pl.BlockSpec((ts, tmm), lambda m, n, k: (k, m)), pl.BlockSpec((ts, tn), lambda m, n, k: (k, n))],
        out_specs=(pl.BlockSpec((tmm, tn), lambda m, n, k: (m, n)),),
        scratch_shapes=[pltpu.VMEM((tmm, tn), F32)],
        operands=(a, b), plans=plans)
    return out if plans else out[0]


def _xattn_bwd(dx2, x1, qx, g2, w_xq, w_xo, kv, plans=(), tm=512):
    S = x1.shape[0]

    def body(d_ref, x_ref, qx_ref, g_ref, wq_ref, wo_ref, kv_ref, dx1_ref, dq_ref, dkv_ref, dg2_ref):
        i = pl.program_id(0)

        @pl.when(i == 0)
        def _():
            dkv_ref[...] = jnp.zeros_like(dkv_ref)
            dg2_ref[...] = jnp.zeros_like(dg2_ref)

        d2 = d_ref[...]
        qx = qx_ref[...]
        do = _dot_nt(d2.astype(BF16), wo_ref[...]).astype(BF16)
        for h in range(XH):
            hs = slice(256 * h, 256 * (h + 1))
            vs = slice(D + 256 * h, D + 256 * (h + 1))
            p = _xatt_head(qx, kv_ref, h)
            dp = _dot_nt(do[:, hs], kv_ref[:, vs])
            ds = (p * (dp - jnp.sum(dp * p, axis=-1, keepdims=True))).astype(BF16)
            dq_ref[:, hs] = (_dot(ds, kv_ref[:, hs]) * (1.0 / 16)).astype(BF16)
            dkv_ref[:, hs] += _dot_tn(ds, qx[:, hs])
            dkv_ref[:, vs] += _dot_tn(p.astype(BF16), do[:, hs])
        dhx = _dot_nt(dq_ref[...], wq_ref[...])
        xv = x_ref[...]
        dx, dg2 = _rms_bwd(dhx, xv, _rstd(xv), g_ref[...])
        dx1_ref[...] = d2 + dx
        dg2_ref[...] += dg2

    return _call(
        body, name="xattn_bwd", grid=(S // tm,),
        out_shape=(jax.ShapeDtypeStruct((S, D), F32), jax.ShapeDtypeStruct((S, D), BF16),
                   jax.ShapeDtypeStruct((MEM, 2 * D), F32), jax.ShapeDtypeStruct((1, D), F32)),
        in_specs=[_rows(tm, D), _rows(tm, D), _rows(tm, D), _resident((1, D)), _resident((D, D)), _resident((D, D)),
                  _resident((MEM, 2 * D))],
        out_specs=(_rows(tm, D), _rows(tm, D),
                   pl.BlockSpec((MEM, 2 * D), lambda i: (0, 0)), pl.BlockSpec((1, D), lambda i: (0, 0))),
        operands=(dx2, x1, qx, g2, w_xq, w_xo, kv), plans=plans)


def _mem_kv_bwd(dkv, mem, g_mem, mn, w_xkv_t):
    def body(dkv_ref, m_ref, g_ref, mn_ref, wt_ref, dw_ref, dg_ref):
        dkvb = dkv_ref[...].astype(BF16)
        dw_ref[...] = _dot_tn(dkvb, mn_ref[...]).astype(BF16)
        dmn = _dot(dkvb, wt_ref[...])
        mv = m_ref[...]
        dg_ref[...] = jnp.sum(dmn * (mv * _rstd(mv)), axis=0, keepdims=True)

    vm = pl.BlockSpec(memory_space=pltpu.VMEM)
    return pl.pallas_call(
        body, name="mem_kv_bwd",
        out_shape=(jax.ShapeDtypeStruct((2 * D, D), BF16), jax.ShapeDtypeStruct((1, D), F32)),
        in_specs=[vm] * 5, out_specs=(vm, vm),
        compiler_params=pltpu.CompilerParams(vmem_limit_bytes=VMEM_LIMIT),
    )(dkv, mem, g_mem, mn, w_xkv_t)


def _merge_bwd(dx1, y_att, y_sg, gates, wba_t, wbs_t, w_out, tm=512):
    S = dx1.shape[0]

    def body(d_ref, ya_ref, ys_ref, g_ref, wbat_ref, wbst_ref, wo_ref,
             mg_ref, da_ref, db_ref, dya_ref, dys_ref, dg_ref):
        dm = _dot_nt(d_ref[...].astype(BF16), wo_ref[...])
        a = _dot_nt(ya_ref[...], wbat_ref[...])
        b = _dot_nt(ys_ref[...], wbst_ref[...])
        sa = _sigmoid(g_ref[:, 0:D].astype(F32))
        sb = _sigmoid(g_ref[:, D:2 * D].astype(F32))
        mg_ref[...] = (sa * a + sb * b).astype(BF16)
        da = (dm * sa).astype(BF16)
        db = (dm * sb).astype(BF16)
        da_ref[...] = da
        db_ref[...] = db
        dg_ref[:, 0:D] = (dm * a * sa * (1.0 - sa)).astype(BF16)
        dg_ref[:, D:2 * D] = (dm * b * sb * (1.0 - sb)).astype(BF16)
        dya_ref[...] = _dot(da, wbat_ref[...]).astype(BF16)
        dys_ref[...] = _dot(db, wbst_ref[...]).astype(BF16)

    return pl.pallas_call(
        body, name="merge_bwd", grid=(S // tm,),
        out_shape=(jax.ShapeDtypeStruct((S, D), BF16), jax.ShapeDtypeStruct((S, D), BF16),
                   jax.ShapeDtypeStruct((S, D), BF16), jax.ShapeDtypeStruct((S, ATT_W), BF16),
                   jax.ShapeDtypeStruct((S, SG_W), BF16), jax.ShapeDtypeStruct((S, 2 * D), BF16)),
        in_specs=[_rows(tm, D), _rows(tm, ATT_W), _rows(tm, SG_W), _rows(tm, 2 * D),
                  _resident((D, ATT_W)), _resident((D, SG_W)), _resident((D, D))],
        out_specs=(_rows(tm, D), _rows(tm, D), _rows(tm, D), _rows(tm, ATT_W), _rows(tm, SG_W), _rows(tm, 2 * D)),
        compiler_params=_cparams(1),
    )(dx1, y_att, y_sg, gates, wba_t, wbs_t, w_out)


def _sgu_bwd(uv, dy_sg, lng, lnb, sg_w, b_exp, plans=(), tm=512):
    S = uv.shape[0]
    n_steps = S // tm

    def body(uv_ref, dy_ref, lng_ref, lnb_ref, w_ref, b_ref, duv_ref, dw_ref, dbx_ref, dlng_ref, dlnb_ref, dvln_ref):
        i = pl.program_id(0)

        @pl.when(i == 0)
        def _():
            dw_ref[...] = jnp.zeros_like(dw_ref)
            dbx_ref[...] = jnp.zeros_like(dbx_ref)
            dlng_ref[...] = jnp.zeros_like(dlng_ref)
            dlnb_ref[...] = jnp.zeros_like(dlnb_ref)

        lane = lax.broadcasted_iota(jnp.int32, (1, LANES), 1)
        lo = lane < 64
        mask = _sg_mask()
        lng = lng_ref[...]
        gu, dgelu_u = _gelu_and_grad(uv_ref[:, 0:SG_W].astype(F32))
        gv, dgelu_v = _gelu_and_grad(uv_ref[:, SG_W:2 * SG_W].astype(F32))
        vln, vhat, rstd = _sg_layernorm(gv, lng, lnb_ref[...])
        dy = dy_ref[...].astype(F32)
        dsv_all = dy * gu
        for gp in range(4):
            wf0 = jnp.where(mask, w_ref[2 * gp], 0)
            wf1 = jnp.where(mask, w_ref[2 * gp + 1], 0)
            w0 = wf0.astype(BF16)
            w1 = wf1.astype(BF16)
            cols = slice(128 * gp, 128 * (gp + 1))
            dw0 = jnp.zeros((128, 128), F32)
            dw1 = jnp.zeros((128, 128), F32)
            dbx = jnp.zeros((128, LANES), F32)
            for n in range(tm // 128):
                rows = slice(128 * n, 128 * (n + 1))
                vl = vln[rows, cols]
                vl0 = jnp.where(lo, vl, 0).astype(BF16)
                vl1 = jnp.where(lo, 0, vl).astype(BF16)
                sv = _dot(w0, vl0) + _dot(w1, vl1) + b_ref[:, cols]
                duv_ref[rows, cols] = (dy[rows, cols] * sv * dgelu_u[rows, cols]).astype(BF16)
                dsv = dsv_all[rows, cols]
                dbx = dbx + dsv
                ds0 = jnp.where(lo, dsv, 0).astype(BF16)
                ds1 = jnp.where(lo, 0, dsv).astype(BF16)
                dw0 = dw0 + _dot_nt(ds0, vl0)
                dw1 = dw1 + _dot_nt(ds1, vl1)
                dvln_ref[rows, cols] = _dot_tn(w0, ds0) + _dot_tn(w1, ds1)
            dw_ref[2 * gp] += jnp.where(mask, dw0, 0)
            dw_ref[2 * gp + 1] += jnp.where(mask, dw1, 0)
            dbx_ref[:, cols] += dbx
        dvln = dvln_ref[...]
        dlng_ref[...] += jnp.sum(dvln * vhat, axis=0, keepdims=True)
        dlnb_ref[...] += jnp.sum(dvln, axis=0, keepdims=True)
        dvh = dvln * lng
        dgv = rstd * (dvh - _group_sum64(dvh) * (1.0 / 64) - vhat * (_group_sum64(dvh * vhat) * (1.0 / 64)))
        duv_ref[:, SG_W:2 * SG_W] = (dgv * dgelu_v).astype(BF16)

        @pl.when(i == n_steps - 1)
        def _():
            dbx_ref[...] = _group_sum64(dbx_ref[...])

    return _call(
        body, name="sgu_bwd", grid=(n_steps,),
        out_shape=(jax.ShapeDtypeStruct((S, 2 * SG_W), BF16), jax.ShapeDtypeStruct((8, 128, 128), F32),
                   jax.ShapeDtypeStruct((128, SG_W), F32), jax.ShapeDtypeStruct((1, SG_W), F32),
                   jax.ShapeDtypeStruct((1, SG_W), F32)),
        in_specs=[_rows(tm, 2 * SG_W), _rows(tm, SG_W), _resident((1, SG_W)), _resident((1, SG_W)),
                  _resident((8, 128, 128)), _resident((128, SG_W))],
        out_specs=(_rows(tm, 2 * SG_W), pl.BlockSpec((8, 128, 128), lambda i: (0, 0, 0)),
                   pl.BlockSpec((128, SG_W), lambda i: (0, 0)), pl.BlockSpec((1, SG_W), lambda i: (0, 0)),
                   pl.BlockSpec((1, SG_W), lambda i: (0, 0))),
        scratch_shapes=[pltpu.VMEM((tm, SG_W), F32)],
        operands=(uv, dy_sg, lng, lnb, sg_w, b_exp), plans=plans)


def _attn_bwd(qkv, dy_att, y_att, lse, bias, plans=()):
    S = qkv.shape[0]
    R = ATT_R

    def body(q_ref, kp_ref, kc_ref, vp_ref, vc_ref, b_ref, dy_ref, y_ref, lse_ref, dq_ref, dk_ref, dv_ref, dss_ref,
             s_ref, dp_ref, pb_ref, dsb_ref):
        i = pl.program_id(1)

        @pl.when(i == 0)
        def _():
            dk_ref[...] = jnp.zeros_like(dk_ref)
            dv_ref[...] = jnp.zeros_like(dv_ref)
            dss_ref[...] = jnp.zeros_like(dss_ref)

        lo = lax.broadcasted_iota(jnp.int32, (1, LANES), 1) < 64
        kwin = jnp.concatenate([kp_ref[...], kc_ref[...]], axis=0)
        vwin = jnp.concatenate([vp_ref[...], vc_ref[...]], axis=0)
        for sub in range(ATT_SUBS):
            rows = slice(128 * sub, 128 * (sub + 1))
            kw = kwin[128 * sub:128 * sub + BAND_KEYS]
            vw = vwin[128 * sub:128 * sub + BAND_KEYS]
            table = jnp.where(i == 0, 1 + sub, 0)
            qs = _two_heads(q_ref[rows, :], lo)
            dos = _two_heads(dy_ref[rows, :], lo)
            dyy = dy_ref[rows, :].astype(F32) * y_ref[rows, :].astype(F32)
            delta = (jnp.sum(jnp.where(lo, dyy, 0.0), axis=-1, keepdims=True),
                     jnp.sum(jnp.where(lo, 0.0, dyy), axis=-1, keepdims=True))
            lse2 = lse_ref[rows, :]
            s_ref[...] = _dot_nt(qs, kw)
            dp_ref[...] = _dot_nt(dos, vw)
            for hh in range(2):
                lse = lse2[:, 64 * hh:64 * hh + 1]
                for r0 in range(0, 128, ATT_ROWS_BWD):
                    rr = slice(r0, r0 + ATT_ROWS_BWD)
                    both = slice(128 * hh + r0, 128 * hh + r0 + ATT_ROWS_BWD)
                    p = jnp.exp(s_ref[both, :] + b_ref[table, hh, rr, :] - lse[rr])
                    ds = p * (dp_ref[both, :] - delta[hh][rr])
                    dss_ref[hh, rr, :] += ds
                    pb_ref[both, :] = p.astype(BF16)
                    dsb_ref[both, :] = ds.astype(BF16)
            dq = _dot(jnp.concatenate([dsb_ref[0:128, :], dsb_ref[128:256, :]], axis=1), _two_heads(kw, lo))
            dq_ref[rows, :] = dq.astype(BF16)
            start = pl.multiple_of(i * R + 128 * sub, 128)
            dk_ref[pl.ds(start, BAND_KEYS), :] += _dot_tn(dsb_ref[...], qs)
            dv_ref[pl.ds(start, BAND_KEYS), :] += _dot_tn(pb_ref[...], dos)

    acc_spec = pl.BlockSpec((S + 8 * CHUNK, LANES), lambda j, i: (0, j))
    return _call(
        body, name="attn_bwd", grid=(4, S // R),
        out_shape=(jax.ShapeDtypeStruct((S, ATT_W), BF16), jax.ShapeDtypeStruct((S + 8 * CHUNK, ATT_W), F32),
                   jax.ShapeDtypeStruct((S + 8 * CHUNK, ATT_W), F32), jax.ShapeDtypeStruct((8, 128, BAND_KEYS), F32)),
        in_specs=_att_specs() + [pl.BlockSpec((R, LANES), lambda j, i: (i, j))] * 3,
        out_specs=(pl.BlockSpec((R, LANES), lambda j, i: (i, j)), acc_spec, acc_spec,
                   pl.BlockSpec((2, 128, BAND_KEYS), lambda j, i: (j, 0, 0))),
        scratch_shapes=[pltpu.VMEM((256, BAND_KEYS), F32), pltpu.VMEM((256, BAND_KEYS), F32),
                        pltpu.VMEM((256, BAND_KEYS), BF16), pltpu.VMEM((256, BAND_KEYS), BF16)],
        operands=(qkv, qkv, qkv, qkv, qkv, bias, dy_att, y_att, lse), plans=plans)


def _in_bwd(dq, dk, dv, duv, dgates, x, dx1, g1, w_in_t, plans=(), tm=512):
    S = x.shape[0]
    pad_blocks = (8 * CHUNK) // tm

    def body(dq_ref, dk_ref, dv_ref, duv_ref, dg_ref, x_ref, d1_ref, g_ref, wt_ref, dx_ref, dz_ref, dg1_ref):
        i = pl.program_id(0)
        dz_ref[:, 0:ATT_W] = (dq_ref[...].astype(F32) * 0.125).astype(BF16)
        dz_ref[:, ATT_W:2 * ATT_W] = dk_ref[...].astype(BF16)
        dz_ref[:, 2 * ATT_W:3 * ATT_W] = dv_ref[...].astype(BF16)
        dz_ref[:, 3 * ATT_W:3 * ATT_W + 2 * SG_W] = duv_ref[...]
        dz_ref[:, 3 * ATT_W + 2 * SG_W:IN_COLS] = dg_ref[...]
        dh = _dot(dz_ref[...], wt_ref[...])
        xv = x_ref[...]
        dx, dg1 = _rms_bwd(dh, xv, _rstd(xv), g_ref[...])
        dx_ref[...] = d1_ref[...] + dx

        @pl.when(i == 0)
        def _():
            dg1_ref[...] = jnp.zeros_like(dg1_ref)

        dg1_ref[...] += dg1

    shifted = pl.BlockSpec((tm, ATT_W), lambda i: (i + pad_blocks, 0))
    return _call(
        body, name="in_bwd", grid=(S // tm,),
        out_shape=(jax.ShapeDtypeStruct((S, D), F32), jax.ShapeDtypeStruct((S, IN_COLS), BF16),
                   jax.ShapeDtypeStruct((1, D), F32)),
        in_specs=[_rows(tm, ATT_W), shifted, shifted, _rows(tm, 2 * SG_W), _rows(tm, 2 * D), _rows(tm, D),
                  _rows(tm, D), _resident((1, D)), _resident((IN_COLS, D))],
        out_specs=(_rows(tm, D), _rows(tm, IN_COLS), pl.BlockSpec((1, D), lambda i: (0, 0))),
        operands=(dq, dk, dv, duv, dgates, x, dx1, g1, w_in_t), plans=plans)


def _adam_math(w, g, m, v):
    m = ADAM_B1 * m + (1.0 - ADAM_B1) * g
    v = ADAM_B2 * v + (1.0 - ADAM_B2) * (g * g)
    m_hat = m / (1.0 - ADAM_B1 ** ADAM_STEP)
    v_hat = v / (1.0 - ADAM_B2 ** ADAM_STEP)
    delta = -ADAM_LR * (m_hat / (jnp.sqrt(v_hat) + ADAM_EPS) + ADAM_WD * w)
    return delta, m, v


def _adam(parts, w, m, v, tr, name, transposed, after=None):
    P = parts.shape[0]
    R, C = w.shape

    def body(p_ref, w_ref, m_ref, v_ref, *rest):
        g_ref, d_ref, mo_ref, vo_ref = rest[-4:]
        if transposed:
            eye = (lax.broadcasted_iota(jnp.int32, (C, C), 0) == lax.broadcasted_iota(jnp.int32, (C, C), 1)).astype(BF16)
            part = lambda k: _dot_tn(p_ref[k], eye)
        else:
            part = lambda k: p_ref[k].astype(F32)
        g = part(0)
        for k in range(1, P):
            g = g + part(k)
        g_ref[...] = g
        d_ref[...], mo_ref[...], vo_ref[...] = _adam_math(w_ref[...], g, m_ref[...], v_ref[...])

    row = pl.BlockSpec((tr, C), lambda i: (i, 0))
    p_spec = pl.BlockSpec((P, C, tr), lambda i: (0, 0, i)) if transposed else pl.BlockSpec((P, tr, C), lambda i: (0, i, 0))
    extra = [] if after is None else [after]
    return pl.pallas_call(
        body, name=name, grid=(R // tr,),
        out_shape=tuple(jax.ShapeDtypeStruct((R, C), F32) for _ in range(4)),
        in_specs=[p_spec, row, row, row] + [pl.BlockSpec(memory_space=pl.ANY)] * len(extra),
        out_specs=(row, row, row, row),
        compiler_params=_cparams(1),
    )(parts, w, m, v, *extra)


def _my_place():
    return lax.axis_index("x"), lax.axis_index("y"), lax.axis_index("c")


def _other_chips(x, y):
    return [(1 - x, y), (x, 1 - y), (1 - x, 1 - y)]


class _Plan:
    def __init__(self, arrays, out_shapes, sems, start, finish, forward=None, forward_at=0.7):
        self.arrays, self.out_shapes, self.sems = list(arrays), list(out_shapes), list(sems)
        self.start, self.finish, self.forward, self.forward_at = start, finish, forward, forward_at


def _call(body, *, name, grid, in_specs, out_specs, out_shape, operands, scratch_shapes=(), plans=()):
    n_in, n_out, n_scr = len(operands), len(out_shape), len(scratch_shapes)
    p_in = [a for p in plans for a in p.arrays]
    p_out = [s for p in plans for s in p.out_shapes]
    p_sem = [s for p in plans for s in p.sems]
    steps = math.prod(grid)

    def wrapped(*refs):
        ins, refs = refs[:n_in], refs[n_in:]
        pins, refs = refs[:len(p_in)], refs[len(p_in):]
        outs, refs = refs[:n_out], refs[n_out:]
        pouts, refs = refs[:len(p_out)], refs[len(p_out):]
        scr, psems = refs[:n_scr], refs[n_scr:]
        step = 0
        for axis, size in enumerate(grid):
            step = step * size + pl.program_id(axis)
        bound = []
        for p in plans:
            bound.append((p, pins[:len(p.arrays)], pouts[:len(p.out_shapes)], psems[:len(p.sems)]))
            pins, pouts, psems = pins[len(p.arrays):], pouts[len(p.out_shapes):], psems[len(p.sems):]

        @pl.when(step == 0)
        def _():
            for p, a, b, s in bound:
                p.start(a, b, s)

        body(*ins, *outs, *scr)

        for p, a, b, s in bound:
            if p.forward is not None:
                @pl.when(step == min(int(p.forward_at * steps), steps - 1))
                def _(p=p, a=a, b=b, s=s):
                    p.forward(a, b, s)

        @pl.when(step == steps - 1)
        def _():
            for p, a, b, s in bound:
                p.finish(a, b, s)

    hbm = pl.BlockSpec(memory_space=pl.ANY)
    return pl.pallas_call(
        wrapped if plans else body, name=name, grid=grid,
        out_shape=tuple(out_shape) + tuple(p_out),
        in_specs=list(in_specs) + [hbm] * len(p_in),
        out_specs=tuple(out_specs) + tuple([hbm] * len(p_out)),
        scratch_shapes=list(scratch_shapes) + p_sem,
        compiler_params=_cparams(len(grid)),
    )(*operands, *p_in)


def _run_plan(plan, name):
    n_in, n_out = len(plan.arrays), len(plan.out_shapes)

    def body(*refs):
        a, b, s = refs[:n_in], refs[n_in:n_in + n_out], refs[n_in + n_out:]
        plan.start(a, b, s)
        if plan.forward is not None:
            plan.forward(a, b, s)
        plan.finish(a, b, s)

    hbm = pl.BlockSpec(memory_space=pl.ANY)
    return pl.pallas_call(
        body, name=name, out_shape=tuple(plan.out_shapes),
        in_specs=[hbm] * n_in, out_specs=tuple([hbm] * n_out), scratch_shapes=plan.sems,
    )(*plan.arrays)


def _gather_plan(shards, forward_at=0.7):
    n = len(shards)

    def copies(ins, outs, sems):
        send_sems, recv_sems, local_sems = sems
        x, y, c = _my_place()
        me, sibling = (x, y, c), (x, y, 1 - c)
        chips = _other_chips(x, y)

        def idx(p):
            return 4 * p[0] + 2 * p[1] + p[2]

        def copy(a, k, block, to, src=None):
            return pltpu.make_async_remote_copy(
                src_ref=outs[a].at[idx(block)] if src is None else src, dst_ref=outs[a].at[idx(block)],
                send_sem=send_sems.at[a, k], recv_sem=recv_sems.at[a, k], device_id=to, device_id_type=MESH)

        mine = [pltpu.make_async_copy(ins[a], outs[a].at[idx(me)], local_sems.at[a]) for a in range(n)]
        first = []
        for a in range(n):
            first.append(copy(a, 0, me, sibling, src=ins[a]))
            first += [copy(a, 1 + j, me, (*chip, c), src=ins[a]) for j, chip in enumerate(chips)]
        arrived = [copy(a, 1 + j, (*chip, c), me) for j, chip in enumerate(chips) for a in range(n)]
        passed = [copy(a, 4 + j, (*chip, c), sibling) for j, chip in enumerate(chips) for a in range(n)]
        from_sibling = []
        for a in range(n):
            from_sibling.append(copy(a, 0, sibling, me))
            from_sibling += [copy(a, 4 + j, (*chip, 1 - c), me) for j, chip in enumerate(chips)]
        return mine, first, arrived, passed, from_sibling

    def start(ins, outs, sems):
        mine, first, _, _, _ = copies(ins, outs, sems)
        for cp in mine + first:
            cp.start()

    def forward(ins, outs, sems):
        _, _, arrived, passed, _ = copies(ins, outs, sems)
        for landed, onward in zip(arrived, passed):
            landed.wait_recv()
            onward.start()

    def finish(ins, outs, sems):
        mine, first, _, passed, from_sibling = copies(ins, outs, sems)
        for cp in from_sibling:
            cp.wait_recv()
        for cp in first + passed:
            cp.wait_send()
        for cp in mine:
            cp.wait()

    return _Plan(shards, [jax.ShapeDtypeStruct((N_DEV,) + s.shape, s.dtype) for s in shards],
                 [pltpu.SemaphoreType.DMA((n, 7)), pltpu.SemaphoreType.DMA((n, 7)), pltpu.SemaphoreType.DMA((n,))],
                 start, finish, forward, forward_at)


def _sibling_plan(scatter, whole=()):
    ns = len(scatter)
    n = ns + len(whole)

    def copies(ins, outs, sems):
        send_sems, recv_sems = sems
        x, y, c = _my_place()
        out = []
        for a in range(n):
            for k in range(4 if a < ns else 1):
                src = ins[a].at[2 * k + (1 - c)] if a < ns else ins[a]
                dst = outs[a].at[k] if a < ns else outs[a]
                out.append(pltpu.make_async_remote_copy(
                    src_ref=src, dst_ref=dst, send_sem=send_sems.at[a, k], recv_sem=recv_sems.at[a, k],
                    device_id=(x, y, 1 - c), device_id_type=MESH))
        return out

    def start(ins, outs, sems):
        for cp in copies(ins, outs, sems):
            cp.start()

    def finish(ins, outs, sems):
        for cp in copies(ins, outs, sems):
            cp.wait()

    shapes = [jax.ShapeDtypeStruct((4,) + p.shape[1:], p.dtype) for p in scatter]
    shapes += [jax.ShapeDtypeStruct(p.shape, p.dtype) for p in whole]
    return _Plan(list(scatter) + list(whole), shapes,
                 [pltpu.SemaphoreType.DMA((n, 4)), pltpu.SemaphoreType.DMA((n, 4))], start, finish)


def _pair_sums(mine, theirs, c_idx, name):
    n = len(mine)

    def body(c_ref, *refs):
        for a in range(n):
            out = refs[2 * n + a]
            out[...] = (refs[a][...].astype(F32) + refs[n + a][...].astype(F32)).astype(out.dtype)

    def block(arr):
        return (1,) + arr.shape[1:]

    grid_spec = pltpu.PrefetchScalarGridSpec(
        num_scalar_prefetch=1, grid=(4,),
        in_specs=[pl.BlockSpec(block(m), lambda k, c_ref: (2 * k + c_ref[0], 0, 0)) for m in mine]
        + [pl.BlockSpec(block(t), lambda k, c_ref: (k, 0, 0)) for t in theirs],
        out_specs=tuple(pl.BlockSpec(block(t), lambda k, c_ref: (k, 0, 0)) for t in theirs))
    return pl.pallas_call(
        body, name=name, grid_spec=grid_spec,
        out_shape=tuple(jax.ShapeDtypeStruct(t.shape, m.dtype) for m, t in zip(mine, theirs)),
        compiler_params=_cparams(1),
    )(c_idx, *mine, *theirs)


def _peers_plan(arrays):
    n = len(arrays)

    def copies(ins, outs, sems):
        send_sems, recv_sems, local_sems = sems
        x, y, c = _my_place()
        me = 4 * x + 2 * y + c
        out = [pltpu.make_async_copy(ins[a], outs[a].at[me], local_sems.at[a]) for a in range(n)]
        for a in range(n):
            for k in range(N_DEV - 1):
                bits = k + 1
                peer = (x ^ (bits >> 2), y ^ ((bits >> 1) & 1), c ^ (bits & 1))
                out.append(pltpu.make_async_remote_copy(
                    src_ref=ins[a], dst_ref=outs[a].at[me], send_sem=send_sems.at[a, k], recv_sem=recv_sems.at[a, k],
                    device_id=peer, device_id_type=MESH))
        return out

    def start(ins, outs, sems):
        for cp in copies(ins, outs, sems):
            cp.start()

    def finish(ins, outs, sems):
        for cp in copies(ins, outs, sems):
            cp.wait()

    return _Plan(list(arrays), [jax.ShapeDtypeStruct((N_DEV,) + a.shape, a.dtype) for a in arrays],
                 [pltpu.SemaphoreType.DMA((n, N_DEV - 1)), pltpu.SemaphoreType.DMA((n, N_DEV - 1)),
                  pltpu.SemaphoreType.DMA((n,))], start, finish)


def _chips_plan(scatter, whole=()):
    ns = len(scatter)
    n = ns + len(whole)

    def copies(ins, outs, sems):
        send_sems, recv_sems, local_sems = sems
        x, y, c = _my_place()
        my_chip = 2 * x + y

        def src(a, k):
            return ins[a].at[k] if a < ns else ins[a]

        local = [pltpu.make_async_copy(src(a, my_chip), outs[a].at[my_chip], local_sems.at[a]) for a in range(n)]
        remote = []
        for a in range(n):
            for j, (px, py) in enumerate(_other_chips(x, y)):
                remote.append(pltpu.make_async_remote_copy(
                    src_ref=src(a, 2 * px + py), dst_ref=outs[a].at[my_chip],
                    send_sem=send_sems.at[a, j], recv_sem=recv_sems.at[a, j],
                    device_id=(px, py, c), device_id_type=MESH))
        return local + remote

    def start(ins, outs, sems):
        for cp in copies(ins, outs, sems):
            cp.start()

    def finish(ins, outs, sems):
        for cp in copies(ins, outs, sems):
            cp.wait()

    shapes = [jax.ShapeDtypeStruct(s.shape, s.dtype) for s in scatter]
    shapes += [jax.ShapeDtypeStruct((4,) + s.shape, s.dtype) for s in whole]
    return _Plan(list(scatter) + list(whole), shapes,
                 [pltpu.SemaphoreType.DMA((n, 3)), pltpu.SemaphoreType.DMA((n, 3)), pltpu.SemaphoreType.DMA((n,))],
                 start, finish)


def _chip_copies(src_ref, land_ref, send_sems, recv_sems):
    x, y, c = _my_place()
    return [pltpu.make_async_remote_copy(
        src_ref=src_ref.at[2 * px + py], dst_ref=land_ref.at[2 * x + y], send_sem=send_sems.at[j],
        recv_sem=recv_sems.at[j], device_id=(px, py, c), device_id_type=MESH)
        for j, (px, py) in enumerate(_other_chips(x, y))]


def _chips_start(src, name):
    def body(src_ref, land_ref, send_sems, recv_sems, src_thru, land_thru, token):
        for cp in _chip_copies(src_ref, land_ref, send_sems, recv_sems):
            cp.start()
        token[...] = jnp.zeros_like(token)

    hbm = pl.BlockSpec(memory_space=pltpu.HBM)
    sem = pl.BlockSpec(memory_space=pltpu.SEMAPHORE)
    return pl.pallas_call(
        body, name=name,
        out_shape=(pltpu.SemaphoreType.DMA((3,)), pltpu.SemaphoreType.DMA((3,)), pltpu.HBM(src.shape, src.dtype),
                   pltpu.HBM(src.shape, src.dtype), jax.ShapeDtypeStruct((8, LANES), F32)),
        in_specs=(hbm, hbm), out_specs=(sem, sem, hbm, hbm, pl.BlockSpec(memory_space=pltpu.VMEM)),
        input_output_aliases={0: 2, 1: 3},
        compiler_params=pltpu.CompilerParams(has_side_effects=pltpu.SideEffectType.DATAFLOW_SIDE_EFFECTING),
    )(pltpu.with_memory_space_constraint(src, pltpu.HBM),
      pltpu.with_memory_space_constraint(jnp.zeros(src.shape, src.dtype), pltpu.HBM))


def _chips_wait(send_sems, recv_sems, src_thru, land_thru, after, name):
    def body(src_ref, land_ref, send_sems, recv_sems, after_ref, src_dead, got_ref):
        for cp in _chip_copies(src_ref, land_ref, send_sems, recv_sems):
            cp.wait_send()
            cp.wait_recv()

    hbm = pl.BlockSpec(memory_space=pltpu.HBM)
    sem = pl.BlockSpec(memory_space=pltpu.SEMAPHORE)
    return pl.pallas_call(
        body, name=name,
        out_shape=(pltpu.HBM(src_thru.shape, src_thru.dtype), pltpu.HBM(land_thru.shape, land_thru.dtype)),
        in_specs=(hbm, hbm, sem, sem, pl.BlockSpec(memory_space=pl.ANY)), out_specs=(hbm, hbm),
        input_output_aliases={0: 0, 1: 1},
        compiler_params=pltpu.CompilerParams(has_side_effects=pltpu.SideEffectType.DATAFLOW_SIDE_EFFECTING),
    )(src_thru, land_thru, send_sems, recv_sems, after)


def _small_update(parts, w, m, v, loss_parts, name):
    n = len(parts)

    def total(ref):
        acc = ref[0]
        for k in range(1, ref.shape[0]):
            acc = acc + ref[k]
        return acc

    def body(*refs):
        p_refs, w_refs, m_refs, v_refs = (refs[i * n:(i + 1) * n] for i in range(4))
        lp_ref = refs[4 * n]
        outs = refs[4 * n + 1:]
        g_refs, d_refs, mo_refs, vo_refs = (outs[i * n:(i + 1) * n] for i in range(4))
        for a in range(n):
            g = total(p_refs[a])
            g_refs[a][...] = g
            d_refs[a][...], mo_refs[a][...], vo_refs[a][...] = _adam_math(w_refs[a][...], g, m_refs[a][...],
                                                                          v_refs[a][...])
        outs[4 * n][...] = total(lp_ref)

    vm = pl.BlockSpec(memory_space=pltpu.VMEM)
    shapes = [jax.ShapeDtypeStruct(t.shape, F32) for _ in range(4) for t in w]
    shapes.append(jax.ShapeDtypeStruct(loss_parts.shape[1:], F32))
    outs = pl.pallas_call(
        body, name=name, out_shape=tuple(shapes),
        in_specs=[vm] * (4 * n + 1), out_specs=tuple([vm] * (4 * n + 1)),
        compiler_params=pltpu.CompilerParams(vmem_limit_bytes=VMEM_LIMIT),
    )(*parts, *w, *m, *v, loss_parts)
    return outs[0:n], outs[n:2 * n], outs[2 * n:3 * n], outs[3 * n:4 * n], outs[4 * n]


BIG = [("w_in", 1), ("w_branch_att", 1), ("w_branch_sg", 1), ("w_out", 0), ("w_xq", 0), ("w_xkv", 1), ("w_xo", 0),
       ("w_ffn_in", 1), ("w_ffn_out", 0)]
SMALL = [("norm_mix_g", (1, D)), ("rel_bias", (8, NREL)), ("sg_ln_g", (8, 64)), ("sg_ln_b", (8, 64)),
         ("sg_w", (8, 128, 128)), ("sg_b", (8, 128)), ("norm_xattn_g", (1, D)), ("norm_mem_g", (1, D)),
         ("norm_ffn_g", (1, D)), ("norm_final_g", (1, D))]
ADAM_ROWS = {"w_in": 192, "w_branch_att": 512, "w_branch_sg": 512, "w_xkv": 1024, "w_ffn_in": 176}


def _full(gathered):
    return gathered.reshape(N_DEV * gathered.shape[1], gathered.shape[2])


def _blocks(grad):
    return grad.reshape(N_DEV, grad.shape[0] // N_DEV, grad.shape[1])


def kernel(x, mem, norm_mix_g, w_in, rel_bias, sg_ln_g, sg_ln_b, sg_w, sg_b, w_branch_att, w_branch_sg, w_out, norm_xattn_g, norm_mem_g, w_xq, w_xkv, w_xo, norm_ffn_g, w_ffn_in, w_ffn_out, norm_final_g, loss_target, m_norm_mix_g, m_w_in, m_rel_bias, m_sg_ln_g, m_sg_ln_b, m_sg_w, m_sg_b, m_w_branch_att, m_w_branch_sg, m_w_out, m_norm_xattn_g, m_norm_mem_g, m_w_xq, m_w_xkv, m_w_xo, m_norm_ffn_g, m_w_ffn_in, m_w_ffn_out, m_norm_final_g, v_norm_mix_g, v_w_in, v_rel_bias, v_sg_ln_g, v_sg_ln_b, v_sg_w, v_sg_b, v_w_branch_att, v_w_branch_sg, v_w_out, v_norm_xattn_g, v_norm_mem_g, v_w_xq, v_w_xkv, v_w_xo, v_norm_ffn_g, v_w_ffn_in, v_w_ffn_out, v_norm_final_g):
    args = dict(locals())
    big_names = [n for n, _ in BIG]
    small_names = [n for n, _ in SMALL]
    S = x.shape[1]

    x, mem, tgt = x.reshape(S, D), mem.reshape(MEM, D), loss_target.reshape(S, D)
    small = {n: args[n].reshape(shape) for n, shape in SMALL}
    g1, g2, g3 = small["norm_mix_g"], small["norm_xattn_g"], small["norm_ffn_g"]
    g_mem, g4 = small["norm_mem_g"], small["norm_final_g"]
    lng = small["sg_ln_g"].reshape(1, SG_W)
    lnb = small["sg_ln_b"].reshape(1, SG_W)
    b_exp = jnp.broadcast_to(small["sg_b"].T[:, :, None], (128, 8, 64)).reshape(128, SG_W)
    rel_pad = jnp.pad(small["rel_bias"], ((0, 0), (0, 384 - NREL)))
    c_idx = lax.axis_index("c").astype(jnp.int32).reshape(1)

    shard = {n: (args[n][0].T if axis == 1 else args[n][0]).astype(BF16) for n, axis in BIG}
    h, w_in_gathered = _norm_in(x, g1, plans=[_gather_plan([shard["w_in"]])])
    w_in_t = _full(w_in_gathered)
    bias = _bias_table(rel_pad)
    mix_names = ["w_branch_att", "w_branch_sg", "w_out", "w_xq", "w_xkv", "w_xo"]
    qkv, uv, gates, *got = _in_proj(h, w_in_t, plans=[_gather_plan([shard[n] for n in mix_names])])
    wba_t, wbs_t, w_out_f, w_xq_f, w_xkv_t, w_xo_f = (_full(g) for g in got)
    y_att, lse, *got = _attn_fwd(qkv, bias, plans=[_gather_plan([shard["w_ffn_in"], shard["w_ffn_out"]])])
    w_ffn_in_t, w_ffn_out_f = (_full(g) for g in got)
    y_sg = _sgu_fwd(uv, lng, lnb, small["sg_w"], b_exp)
    x1 = _merge_fwd(x, y_att, y_sg, gates, wba_t, wbs_t, w_out_f)
    kv, mn = _mem_kv(mem, g_mem, w_xkv_t)
    x2, hx, qx, o_x = _xattn_fwd(x1, g2, w_xq_f, kv, w_xo_f)
    dx3, gu, hf, act, loss_part, dg4 = _ffn_fwd(x2, tgt, g3, w_ffn_in_t, w_ffn_out_f, g4)


    dx2, dgu, dg3 = _ffn_bwd(dx3, gu, x2, g3, w_ffn_out_f, w_ffn_in_t)
    ffn_names = ["w_ffn_out", "w_ffn_in"]
    ffn_mine = [_blocks(_dw(act, dx3, 1408, 1024, DW_TOKENS, "dw_ffn_out")),
                _blocks(_dw(dgu, hf, 1408, 1024, DW_TOKENS, "dw_ffn_in"))]
    dx1, dq_x, dkv, dg2, *ffn_theirs = _xattn_bwd(dx2, x1, qx, g2, w_xq_f, w_xo_f, kv,
                                                  plans=[_sibling_plan(ffn_mine)])
    ffn_chip = _pair_sums(ffn_mine, ffn_theirs, c_idx, "rs_pair_ffn")
    d_xkv, dg_mem = _mem_kv_bwd(dkv, mem, g_mem, mn, w_xkv_t)
    merged, d_a, d_b, dy_att, dy_sg, dgates = _merge_bwd(dx1, y_att, y_sg, gates, wba_t, wbs_t, w_out_f)
    mid_names = ["w_xo", "w_xq", "w_xkv", "w_out", "w_branch_att", "w_branch_sg"]
    mid_mine = [_blocks(g) for g in (
        _dw(o_x, dx2, 1024, 1024, DW_TOKENS, "dw_xo"), _dw(hx, dq_x, 1024, 1024, DW_TOKENS, "dw_xq"), d_xkv,
        _dw(merged, dx1, 1024, 1024, DW_TOKENS, "dw_out"), _dw(d_a, y_att, 1024, 512, DW_TOKENS, "dw_branch_att"),
        _dw(d_b, y_sg, 1024, 512, DW_TOKENS, "dw_branch_sg"))]
    duv, d_sgw, d_bx, d_lng, d_lnb, *got = _sgu_bwd(uv, dy_sg, lng, lnb, small["sg_w"], b_exp,
                                                    plans=[_chips_plan(ffn_chip), _sibling_plan(mid_mine)])
    ffn_all, mid_theirs = got[:2], got[2:]
    mid_chip = _pair_sums(mid_mine, mid_theirs, c_idx, "rs_pair_mid")
    dq, dk, dv, ds_sum, *mid_all = _attn_bwd(qkv, dy_att, y_att, lse, bias, plans=[_chips_plan(mid_chip)])
    d_rel = _bias_grad(ds_sum)

    grad_x, dz, dg1 = _in_bwd(dq, dk, dv, duv, dgates, x, dx1, g1, w_in_t)

    gs = {"norm_mix_g": dg1, "rel_bias": d_rel[:, :NREL], "sg_ln_g": d_lng.reshape(8, 64),
          "sg_ln_b": d_lnb.reshape(8, 64), "sg_w": d_sgw, "sg_b": d_bx.reshape(128, 8, 64)[:, :, 0].T,
          "norm_xattn_g": dg2, "norm_mem_g": dg_mem, "norm_ffn_g": dg3, "norm_final_g": dg4}
    d_in, *everyone = _dw(dz, h, 1152, 1024, DW_TOKENS, "dw_in",
                          plans=[_peers_plan([gs[n] for n in small_names] + [loss_part])])

    in_mine = [_blocks(d_in)]
    (in_theirs,) = _run_plan(_sibling_plan(in_mine), "rs_sibling")
    (in_chip,) = _pair_sums(in_mine, [in_theirs], c_idx, "rs_pair_w_in")
    send_sems, recv_sems, in_chip_thru, landing, token = _chips_start(in_chip, "rs_chips_start")
    all_parts = dict(zip(ffn_names + mid_names, list(ffn_all) + list(mid_all)))

    def adam(n, axis, parts, after):
        wmv = [args[p + n][0] for p in ("", "m_", "v_")]
        if axis == 1 and wmv[0].shape[1] % LANES != 0:
            outs = _adam(parts, *(t.T for t in wmv), ADAM_ROWS[n], "adam_" + n, transposed=False, after=after)
            return [t.T[None] for t in outs]
        tr = ADAM_ROWS[n] if axis == 1 else wmv[0].shape[0]
        return [t[None] for t in _adam(parts, *wmv, tr, "adam_" + n, transposed=(axis == 1), after=after)]

    res = {n: adam(n, axis, all_parts[n], token) for n, axis in BIG if n != "w_in"}
    in_chip, landing = _chips_wait(send_sems, recv_sems, in_chip_thru, landing, res["w_ffn_in"][1], "rs_chips_wait")
    my_chip = 2 * lax.axis_index("x") + lax.axis_index("y")
    own = lax.dynamic_slice_in_dim(in_chip, my_chip, 1, axis=0)
    res["w_in"] = adam("w_in", 1, lax.dynamic_update_slice_in_dim(landing, own, my_chip, axis=0), None)
    small_res = _small_update(
        everyone[:-1], [small[n] for n in small_names],
        [args["m_" + n].reshape(s) for n, s in SMALL], [args["v_" + n].reshape(s) for n, s in SMALL],
        everyone[-1], "adam_small")
    for i, n in enumerate(small_names):
        res[n] = [small_res[k][i].reshape(args[n].shape) for k in range(4)]
    loss = small_res[4][0, 0]

    order = ["norm_mix_g", "w_in", "rel_bias", "sg_ln_g", "sg_ln_b", "sg_w", "sg_b", "w_branch_att", "w_branch_sg",
             "w_out", "norm_xattn_g", "norm_mem_g", "w_xq", "w_xkv", "w_xo", "norm_ffn_g", "w_ffn_in", "w_ffn_out",
             "norm_final_g"]
    outs = [loss, grad_x.reshape(1, S, D)]
    for k in range(4):
        outs += [res[n][k] for n in order]
    return tuple(outs)
```

```python
import math

import jax
import jax.numpy as jnp
from jax import lax
from jax.experimental import pallas as pl
from jax.experimental.pallas import tpu as pltpu

F32 = jnp.float32
BF16 = jnp.bfloat16

D = 1024
ATT_W = 512
SG_W = 512
IN_COLS = 4608
DFF = 2816
MEM = 256
XH = 4
CHUNK = 64
BAND_KEYS = 640
ATT_R = 512
ATT_SUBS = ATT_R // 128
ATT_ROWS = 32
ATT_ROWS_BWD = 16
DW_TOKENS = 2048
REL_CLIP = 128
NREL = 2 * REL_CLIP + 1
EPS = 1e-6
NEG = -1e30
N_DEV = 8

ADAM_LR = 0.001
ADAM_B1 = 0.9
ADAM_B2 = 0.999
ADAM_EPS = 1e-08
ADAM_WD = 0.01
ADAM_STEP = 10

LANES = 128
VMEM_LIMIT = 56 * 1024 * 1024
MESH = pl.DeviceIdType.MESH


def _cparams(n_axes):
    return pltpu.CompilerParams(dimension_semantics=("arbitrary",) * n_axes, vmem_limit_bytes=VMEM_LIMIT)


def _resident(shape):
    zeros = (0,) * len(shape)
    return pl.BlockSpec(shape, lambda *_: zeros, pipeline_mode=pl.Buffered(1))


def _rows(tm, cols, col_block=0):
    return pl.BlockSpec((tm, cols), lambda i: (i, col_block))


def _sigmoid(x):
    return pl.reciprocal(1.0 + jnp.exp(-x), approx=True)


_GELU_C = math.sqrt(2.0 / math.pi)


def _gelu(x):
    t = jnp.tanh(_GELU_C * (x + 0.044715 * (x * x * x)))
    return x * (0.5 * (1.0 + t))


def _gelu_and_grad(x):
    x2 = x * x
    t = jnp.tanh(_GELU_C * (x + 0.044715 * (x2 * x)))
    cdf = 0.5 * (1.0 + t)
    dcdf = 0.5 * (1.0 - t * t) * (_GELU_C * (1.0 + 3.0 * 0.044715 * x2))
    return x * cdf, cdf + x * dcdf


def _rstd(x):
    return lax.rsqrt(jnp.mean(x * x, axis=-1, keepdims=True) + EPS)


def _rms_bwd(dh, x, r, g):
    xh = x * r
    dxh = dh * g
    dx = r * (dxh - xh * jnp.mean(dxh * xh, axis=-1, keepdims=True))
    dg = jnp.sum(dh * xh, axis=0, keepdims=True)
    return dx, dg


def _group_sum64(x):
    r = lax.broadcasted_iota(jnp.int32, (LANES, LANES), 0) // 64
    c = lax.broadcasted_iota(jnp.int32, (LANES, LANES), 1) // 64
    same_group = (r == c).astype(BF16)

    def one(v):
        hi = v.astype(BF16)
        rest = v - hi.astype(F32)
        mid = rest.astype(BF16)
        lo = (rest - mid.astype(F32)).astype(BF16)
        return _dot(hi, same_group) + _dot(mid, same_group) + _dot(lo, same_group)

    pieces = [one(x[:, LANES * j:LANES * (j + 1)]) for j in range(x.shape[1] // LANES)]
    return pieces[0] if len(pieces) == 1 else jnp.concatenate(pieces, axis=1)


def _dot(a, b):
    return jnp.dot(a, b, preferred_element_type=F32)


def _dot_nt(a, b):
    return lax.dot_general(a, b, (((1,), (1,)), ((), ())), preferred_element_type=F32)


def _dot_tn(a, b):
    return lax.dot_general(a, b, (((0,), (0,)), ((), ())), preferred_element_type=F32)


DIAGS = 768


def _diag_onehot():
    r_idx = lax.broadcasted_iota(jnp.int32, (384, DIAGS), 0)
    t_idx = lax.broadcasted_iota(jnp.int32, (384, DIAGS), 1)
    dist = (8 * CHUNK + 127) - t_idx
    return (jnp.clip(dist, -REL_CLIP, REL_CLIP) + REL_CLIP == r_idx).astype(F32)


def _shift_rows(x, reverse):
    row = lax.broadcasted_iota(jnp.int32, x.shape, 0)
    for k in range(7):
        amt = (DIAGS - (1 << k)) if reverse else (1 << k)
        x = jnp.where(((row >> k) & 1) == 1, pltpu.roll(x, amt, 1), x)
    return x


N_TABLES = 1 + ATT_SUBS


def _bias_table(rel_bias_pad):
    def body(rb_ref, out_ref):
        per_diag = jnp.dot(rb_ref[...], _diag_onehot(), preferred_element_type=F32,
                           precision=lax.Precision.HIGHEST)
        a = lax.broadcasted_iota(jnp.int32, (128, BAND_KEYS), 0)
        b = lax.broadcasted_iota(jnp.int32, (128, BAND_KEYS), 1)
        band = (b // CHUNK >= a // CHUNK) & (b // CHUNK <= a // CHUNK + 8)
        for h in range(8):
            rows = jnp.broadcast_to(per_diag[h:h + 1, :], (128, DIAGS))
            table = _shift_rows(pltpu.roll(rows, DIAGS - 127, 1), reverse=False)[:, :BAND_KEYS]
            out_ref[0, h] = jnp.where(band, table, NEG)
            for s in range(ATT_SUBS):
                out_ref[1 + s, h] = jnp.where(band & (b >= 8 * CHUNK - 128 * s), table, NEG)

    return pl.pallas_call(
        body, name="bias_table",
        out_shape=jax.ShapeDtypeStruct((N_TABLES, 8, 128, BAND_KEYS), F32),
        in_specs=[pl.BlockSpec(memory_space=pltpu.VMEM)],
        out_specs=pl.BlockSpec(memory_space=pltpu.VMEM),
        compiler_params=pltpu.CompilerParams(vmem_limit_bytes=VMEM_LIMIT),
    )(rel_bias_pad)


def _bias_grad(ds_sum):
    def body(ds_ref, out_ref):
        sums = []
        for h in range(8):
            padded = jnp.concatenate([ds_ref[h], jnp.zeros((128, DIAGS - BAND_KEYS), F32)], axis=1)
            skewed = pltpu.roll(_shift_rows(padded, reverse=True), 127, 1)
            sums.append(jnp.sum(skewed, axis=0, keepdims=True))
        per_diag = jnp.concatenate(sums, axis=0)
        out_ref[...] = lax.dot_general(per_diag, _diag_onehot(), (((1,), (1,)), ((), ())),
                                       preferred_element_type=F32, precision=lax.Precision.HIGHEST)

    return pl.pallas_call(
        body, name="bias_grad",
        out_shape=jax.ShapeDtypeStruct((8, 384), F32),
        in_specs=[pl.BlockSpec(memory_space=pltpu.VMEM)],
        out_specs=pl.BlockSpec(memory_space=pltpu.VMEM),
    )(ds_sum)


def _norm_in(x, g1, plans=(), tm=512):
    S = x.shape[0]

    def body(x_ref, g_ref, h_ref):
        xv = x_ref[...]
        h_ref[...] = (xv * _rstd(xv) * g_ref[...]).astype(BF16)

    return _call(
        body, name="norm_in", grid=(S // tm,),
        out_shape=(jax.ShapeDtypeStruct((S, D), BF16),),
        in_specs=[_rows(tm, D), _resident((1, D))], out_specs=(_rows(tm, D),),
        operands=(x, g1), plans=plans)


def _in_proj(h, w_in_t, plans=(), tm=512):
    S = h.shape[0]

    def body(h_ref, w_ref, qkv_ref, uv_ref, gate_ref):
        h = h_ref[...]
        for c in range(IN_COLS // 512):
            zc = _dot_nt(h, w_ref[512 * c:512 * (c + 1), :])
            if c == 0:
                qkv_ref[:, 0:512] = (zc * 0.125).astype(BF16)
            elif c < 3:
                qkv_ref[:, 512 * c:512 * (c + 1)] = zc.astype(BF16)
            elif c < 5:
                uv_ref[:, 512 * (c - 3):512 * (c - 2)] = zc.astype(BF16)
            else:
                gate_ref[:, 512 * (c - 5):512 * (c - 4)] = zc.astype(BF16)

    return _call(
        body, name="in_proj", grid=(S // tm,),
        out_shape=(jax.ShapeDtypeStruct((S, 3 * ATT_W), BF16), jax.ShapeDtypeStruct((S, 2 * SG_W), BF16),
                   jax.ShapeDtypeStruct((S, 2 * D), BF16)),
        in_specs=[_rows(tm, D), _resident((IN_COLS, D))],
        out_specs=(_rows(tm, 3 * ATT_W), _rows(tm, 2 * SG_W), _rows(tm, 2 * D)),
        operands=(h, w_in_t), plans=plans)


def _two_heads(a, lo):
    return jnp.concatenate([jnp.where(lo, a, 0), jnp.where(lo, 0, a)], axis=0)


def _att_specs():
    R = ATT_R
    q = pl.BlockSpec((R, LANES), lambda j, i: (i, j))
    kp = pl.BlockSpec((R, LANES), lambda j, i: (jnp.maximum(i - 1, 0), 4 + j))
    kc = pl.BlockSpec((R, LANES), lambda j, i: (i, 4 + j))
    vp = pl.BlockSpec((R, LANES), lambda j, i: (jnp.maximum(i - 1, 0), 8 + j))
    vc = pl.BlockSpec((R, LANES), lambda j, i: (i, 8 + j))
    bias = pl.BlockSpec((N_TABLES, 2, 128, BAND_KEYS), lambda j, i: (0, j, 0, 0))
    return [q, kp, kc, vp, vc, bias]


def _attn_fwd(qkv, bias, plans=()):
    S = qkv.shape[0]
    R = ATT_R

    def body(q_ref, kp_ref, kc_ref, vp_ref, vc_ref, b_ref, o_ref, lse_ref, s_ref, p_ref, l_ref):
        i = pl.program_id(1)
        lo = lax.broadcasted_iota(jnp.int32, (1, LANES), 1) < 64
        kwin = jnp.concatenate([kp_ref[...], kc_ref[...]], axis=0)
        vwin = jnp.concatenate([vp_ref[...], vc_ref[...]], axis=0)
        for sub in range(ATT_SUBS):
            q2 = q_ref[128 * sub:128 * (sub + 1), :]
            kw = kwin[128 * sub:128 * sub + BAND_KEYS]
            vw = vwin[128 * sub:128 * sub + BAND_KEYS]
            table = jnp.where(i == 0, 1 + sub, 0)
            s_ref[...] = _dot_nt(_two_heads(q2, lo), kw)
            for hh in range(2):
                for r0 in range(0, 128, ATT_ROWS):
                    rows = slice(128 * hh + r0, 128 * hh + r0 + ATT_ROWS)
                    s = s_ref[rows, :] + b_ref[table, hh, r0:r0 + ATT_ROWS, :]
                    top = jnp.max(s, axis=-1, keepdims=True)
                    p = jnp.exp(s - top)
                    total = jnp.sum(p, axis=-1, keepdims=True)
                    p_ref[rows, :] = (p / total).astype(BF16)
                    l_ref[rows, :] = jnp.broadcast_to(top + jnp.log(total), (ATT_ROWS, LANES))
            o = _dot(jnp.concatenate([p_ref[0:128, :], p_ref[128:256, :]], axis=1), _two_heads(vw, lo))
            o_ref[128 * sub:128 * (sub + 1), :] = o.astype(BF16)
            lse_ref[128 * sub:128 * (sub + 1), :] = jnp.where(lo, l_ref[0:128, :], l_ref[128:256, :])

    blk = pl.BlockSpec((R, LANES), lambda j, i: (i, j))
    return _call(
        body, name="attn_fwd", grid=(4, S // R),
        out_shape=(jax.ShapeDtypeStruct((S, ATT_W), BF16), jax.ShapeDtypeStruct((S, ATT_W), F32)),
        in_specs=_att_specs(), out_specs=(blk, blk),
        scratch_shapes=[pltpu.VMEM((256, BAND_KEYS), F32), pltpu.VMEM((256, BAND_KEYS), BF16),
                        pltpu.VMEM((256, LANES), F32)],
        operands=(qkv, qkv, qkv, qkv, qkv, bias), plans=plans)


def _sg_mask():
    t = lax.broadcasted_iota(jnp.int32, (128, 128), 0)
    s = lax.broadcasted_iota(jnp.int32, (128, 128), 1)
    return (s // CHUNK) <= (t // CHUNK)


def _sg_layernorm(gv, lng, lnb):
    mu = _group_sum64(gv) * (1.0 / 64)
    xc = gv - mu
    var = _group_sum64(xc * xc) * (1.0 / 64)
    rstd = lax.rsqrt(var + EPS)
    vhat = xc * rstd
    return vhat * lng + lnb, vhat, rstd


def _sgu_fwd(uv, lng, lnb, sg_w, b_exp, tm=512):
    S = uv.shape[0]

    def body(uv_ref, lng_ref, lnb_ref, w_ref, b_ref, y_ref):
        lane = lax.broadcasted_iota(jnp.int32, (1, LANES), 1)
        lo = lane < 64
        mask = _sg_mask()
        gu = _gelu(uv_ref[:, 0:SG_W].astype(F32))
        vln, _, _ = _sg_layernorm(_gelu(uv_ref[:, SG_W:2 * SG_W].astype(F32)), lng_ref[...], lnb_ref[...])
        for gp in range(4):
            w0 = jnp.where(mask, w_ref[2 * gp], 0).astype(BF16)
            w1 = jnp.where(mask, w_ref[2 * gp + 1], 0).astype(BF16)
            cols = slice(128 * gp, 128 * (gp + 1))
            for n in range(tm // 128):
                rows = slice(128 * n, 128 * (n + 1))
                vl = vln[rows, cols]
                sv = (_dot(w0, jnp.where(lo, vl, 0).astype(BF16)) + _dot(w1, jnp.where(lo, 0, vl).astype(BF16))
                      + b_ref[:, cols])
                y_ref[rows, cols] = (gu[rows, cols] * sv).astype(BF16)

    return pl.pallas_call(
        body, name="sgu_fwd", grid=(S // tm,),
        out_shape=jax.ShapeDtypeStruct((S, SG_W), BF16),
        in_specs=[_rows(tm, 2 * SG_W), _resident((1, SG_W)), _resident((1, SG_W)),
                  _resident((8, 128, 128)), _resident((128, SG_W))],
        out_specs=_rows(tm, SG_W),
        compiler_params=_cparams(1),
    )(uv, lng, lnb, sg_w, b_exp)


def _merge_fwd(x, y_att, y_sg, gates, wba_t, wbs_t, w_out, tm=512):
    S = x.shape[0]

    def body(x_ref, ya_ref, ys_ref, g_ref, wba_ref, wbs_ref, wo_ref, x1_ref):
        a = _dot_nt(ya_ref[...], wba_ref[...])
        b = _dot_nt(ys_ref[...], wbs_ref[...])
        merged = _sigmoid(g_ref[:, 0:D].astype(F32)) * a + _sigmoid(g_ref[:, D:2 * D].astype(F32)) * b
        x1_ref[...] = x_ref[...] + _dot(merged.astype(BF16), wo_ref[...])

    return pl.pallas_call(
        body, name="merge_fwd", grid=(S // tm,),
        out_shape=jax.ShapeDtypeStruct((S, D), F32),
        in_specs=[_rows(tm, D), _rows(tm, ATT_W), _rows(tm, SG_W), _rows(tm, 2 * D),
                  _resident((D, ATT_W)), _resident((D, SG_W)), _resident((D, D))],
        out_specs=_rows(tm, D),
        compiler_params=_cparams(1),
    )(x, y_att, y_sg, gates, wba_t, wbs_t, w_out)


def _mem_kv(mem, g_mem, w_xkv_t):
    def body(m_ref, g_ref, w_ref, kv_ref, mn_ref):
        mv = m_ref[...]
        mn = (mv * _rstd(mv) * g_ref[...]).astype(BF16)
        mn_ref[...] = mn
        kv_ref[...] = _dot_nt(mn, w_ref[...]).astype(BF16)

    vm = pl.BlockSpec(memory_space=pltpu.VMEM)
    return pl.pallas_call(
        body, name="mem_kv",
        out_shape=(jax.ShapeDtypeStruct((MEM, 2 * D), BF16), jax.ShapeDtypeStruct((MEM, D), BF16)),
        in_specs=[vm, vm, vm], out_specs=(vm, vm),
        compiler_params=pltpu.CompilerParams(vmem_limit_bytes=VMEM_LIMIT),
    )(mem, g_mem, w_xkv_t)


def _xatt_head(qx, kv_ref, h):
    hs = slice(256 * h, 256 * (h + 1))
    s = _dot_nt(qx[:, hs], kv_ref[:, hs])
    p = jnp.exp(s - jnp.max(s, axis=-1, keepdims=True))
    return p / jnp.sum(p, axis=-1, keepdims=True)


def _xattn_fwd(x1, g2, w_xq, kv, w_xo, tm=512):
    S = x1.shape[0]

    def body(x_ref, g_ref, wq_ref, kv_ref, wo_ref, x2_ref, hx_ref, qx_ref, o_ref):
        xv = x_ref[...]
        hx = (xv * _rstd(xv) * g_ref[...]).astype(BF16)
        hx_ref[...] = hx
        qx = (_dot(hx, wq_ref[...]) * (1.0 / 16)).astype(BF16)
        qx_ref[...] = qx
        for h in range(XH):
            p = _xatt_head(qx, kv_ref, h)
            o_ref[:, 256 * h:256 * (h + 1)] = _dot(p.astype(BF16), kv_ref[:, D + 256 * h:D + 256 * (h + 1)]).astype(BF16)
        x2_ref[...] = xv + _dot(o_ref[...], wo_ref[...])

    return pl.pallas_call(
        body, name="xattn_fwd", grid=(S // tm,),
        out_shape=(jax.ShapeDtypeStruct((S, D), F32),) + (jax.ShapeDtypeStruct((S, D), BF16),) * 3,
        in_specs=[_rows(tm, D), _resident((1, D)), _resident((D, D)), _resident((MEM, 2 * D)), _resident((D, D))],
        out_specs=(_rows(tm, D),) * 4,
        compiler_params=_cparams(1),
    )(x1, g2, w_xq, kv, w_xo)


FF_CHUNK = 256


def _ffn_fwd(x2, tgt, g3, w_ffn_in_t, w_ffn_out, g4, tm=256):
    S = x2.shape[0]

    def body(x_ref, t_ref, g3_ref, wi_ref, wo_ref, g4_ref, dx3_ref, gu_ref, hf_ref, act_ref, loss_ref, dg4_ref):
        i = pl.program_id(0)
        xv = x_ref[...]
        hf = (xv * _rstd(xv) * g3_ref[...]).astype(BF16)
        hf_ref[...] = hf
        for c in range(DFF // FF_CHUNK):
            cs = slice(FF_CHUNK * c, FF_CHUNK * (c + 1))
            us = slice(DFF + FF_CHUNK * c, DFF + FF_CHUNK * (c + 1))
            gate = _dot_nt(hf, wi_ref[cs, :])
            up = _dot_nt(hf, wi_ref[us, :])
            gu_ref[:, cs] = gate.astype(BF16)
            gu_ref[:, us] = up.astype(BF16)
            act_ref[:, cs] = ((gate * _sigmoid(gate)) * up).astype(BF16)
        acc = xv + _dot(act_ref[...], wo_ref[...])
        r4 = _rstd(acc)
        g4 = g4_ref[...]
        diff = acc * r4 * g4 - t_ref[...]
        dy = diff * (1.0 / D)
        dx3, dg4 = _rms_bwd(dy, acc, r4, g4)
        dx3_ref[...] = dx3
        part = 0.5 * jnp.sum(jnp.mean(diff * diff, axis=-1, keepdims=True))

        @pl.when(i == 0)
        def _():
            loss_ref[...] = jnp.zeros_like(loss_ref)
            dg4_ref[...] = jnp.zeros_like(dg4_ref)

        loss_ref[...] += jnp.full(loss_ref.shape, part, F32)
        dg4_ref[...] += dg4

    return pl.pallas_call(
        body, name="ffn_fwd", grid=(S // tm,),
        out_shape=(jax.ShapeDtypeStruct((S, D), F32), jax.ShapeDtypeStruct((S, 2 * DFF), BF16),
                   jax.ShapeDtypeStruct((S, D), BF16), jax.ShapeDtypeStruct((S, DFF), BF16),
                   jax.ShapeDtypeStruct((8, LANES), F32), jax.ShapeDtypeStruct((1, D), F32)),
        in_specs=[_rows(tm, D), _rows(tm, D), _resident((1, D)), _resident((2 * DFF, D)), _resident((DFF, D)),
                  _resident((1, D))],
        out_specs=(_rows(tm, D), _rows(tm, 2 * DFF), _rows(tm, D), _rows(tm, DFF),
                   pl.BlockSpec((8, LANES), lambda i: (0, 0)), pl.BlockSpec((1, D), lambda i: (0, 0))),
        compiler_params=_cparams(1),
    )(x2, tgt, g3, w_ffn_in_t, w_ffn_out, g4)


def _ffn_bwd(dx3, gu, x2, g3, w_ffn_out, w_ffn_in_t, tm=256):
    S = x2.shape[0]

    def body(d_ref, gu_ref, x_ref, g3_ref, wo_ref, wit_ref, dx2_ref, dgu_ref, dg3_ref):
        i = pl.program_id(0)
        d3 = d_ref[...]
        d3b = d3.astype(BF16)
        for c in range(DFF // FF_CHUNK):
            cs = slice(FF_CHUNK * c, FF_CHUNK * (c + 1))
            us = slice(DFF + FF_CHUNK * c, DFF + FF_CHUNK * (c + 1))
            da = _dot_nt(d3b, wo_ref[cs, :])
            gate = gu_ref[:, cs].astype(F32)
            up = gu_ref[:, us].astype(F32)
            sg = _sigmoid(gate)
            dgate = (da * up * (sg * (1.0 + gate * (1.0 - sg)))).astype(BF16)
            dup = (da * (gate * sg)).astype(BF16)
            dgu_ref[:, cs] = dgate
            dgu_ref[:, us] = dup
        dhf = _dot(dgu_ref[...], wit_ref[...])
        xv = x_ref[...]
        dx, dg3 = _rms_bwd(dhf, xv, _rstd(xv), g3_ref[...])
        dx2_ref[...] = d3 + dx

        @pl.when(i == 0)
        def _():
            dg3_ref[...] = jnp.zeros_like(dg3_ref)

        dg3_ref[...] += dg3

    return pl.pallas_call(
        body, name="ffn_bwd", grid=(S // tm,),
        out_shape=(jax.ShapeDtypeStruct((S, D), F32), jax.ShapeDtypeStruct((S, 2 * DFF), BF16),
                   jax.ShapeDtypeStruct((1, D), F32)),
        in_specs=[_rows(tm, D), _rows(tm, 2 * DFF), _rows(tm, D), _resident((1, D)),
                  _resident((DFF, D)), _resident((2 * DFF, D))],
        out_specs=(_rows(tm, D), _rows(tm, 2 * DFF), pl.BlockSpec((1, D), lambda i: (0, 0))),
        compiler_params=_cparams(1),
    )(dx3, gu, x2, g3, w_ffn_out, w_ffn_in_t)


def _dw(a, b, tmm, tn, ts, name, out_dtype=BF16, plans=()):
    S, M = a.shape
    N = b.shape[1]
    ts = min(ts, S)
    nk = S // ts

    def body(a_ref, b_ref, o_ref, acc_ref):
        k = pl.program_id(2)

        @pl.when(k == 0)
        def _():
            acc_ref[...] = jnp.zeros_like(acc_ref)

        acc_ref[...] += _dot_tn(a_ref[...].astype(BF16), b_ref[...].astype(BF16))

        @pl.when(k == nk - 1)
        def _():
            o_ref[...] = acc_ref[...].astype(out_dtype)

    out = _call(
        body, name=name, grid=(M // tmm, N // tn, nk),
        out_shape=(jax.ShapeDtypeStruct((M, N), out_dtype),),
        in_specs=[pl.BlockSpec((ts, tmm), lambda m, n, k: (k, m)), pl.BlockSpec((ts, tn), lambda m, n, k: (k, n))],
        out_specs=(pl.BlockSpec((tmm, tn), lambda m, n, k: (m, n)),),
        scratch_shapes=[pltpu.VMEM((tmm, tn), F32)],
        operands=(a, b), plans=plans)
    return out if plans else out[0]


def _xattn_bwd(dx2, x1, qx, g2, w_xq, w_xo, kv, plans=(), tm=512):
    S = x1.shape[0]

    def body(d_ref, x_ref, qx_ref, g_ref, wq_ref, wo_ref, kv_ref, dx1_ref, dq_ref, dkv_ref, dg2_ref):
        i = pl.program_id(0)

        @pl.when(i == 0)
        def _():
            dkv_ref[...] = jnp.zeros_like(dkv_ref)
            dg2_ref[...] = jnp.zeros_like(dg2_ref)

        d2 = d_ref[...]
        qx = qx_ref[...]
        do = _dot_nt(d2.astype(BF16), wo_ref[...]).astype(BF16)
        for h in range(XH):
            hs = slice(256 * h, 256 * (h + 1))
            vs = slice(D + 256 * h, D + 256 * (h + 1))
            p = _xatt_head(qx, kv_ref, h)
            dp = _dot_nt(do[:, hs], kv_ref[:, vs])
            ds = (p * (dp - jnp.sum(dp * p, axis=-1, keepdims=True))).astype(BF16)
            dq_ref[:, hs] = (_dot(ds, kv_ref[:, hs]) * (1.0 / 16)).astype(BF16)
            dkv_ref[:, hs] += _dot_tn(ds, qx[:, hs])
            dkv_ref[:, vs] += _dot_tn(p.astype(BF16), do[:, hs])
        dhx = _dot_nt(dq_ref[...], wq_ref[...])
        xv = x_ref[...]
        dx, dg2 = _rms_bwd(dhx, xv, _rstd(xv), g_ref[...])
        dx1_ref[...] = d2 + dx
        dg2_ref[...] += dg2

    return _call(
        body, name="xattn_bwd", grid=(S // tm,),
        out_shape=(jax.ShapeDtypeStruct((S, D), F32), jax.ShapeDtypeStruct((S, D), BF16),
                   jax.ShapeDtypeStruct((MEM, 2 * D), F32), jax.ShapeDtypeStruct((1, D), F32)),
        in_specs=[_rows(tm, D), _rows(tm, D), _rows(tm, D), _resident((1, D)), _resident((D, D)), _resident((D, D)),
                  _resident((MEM, 2 * D))],
        out_specs=(_rows(tm, D), _rows(tm, D),
                   pl.BlockSpec((MEM, 2 * D), lambda i: (0, 0)), pl.BlockSpec((1, D), lambda i: (0, 0))),
        operands=(dx2, x1, qx, g2, w_xq, w_xo, kv), plans=plans)


def _mem_kv_bwd(dkv, mem, g_mem, mn, w_xkv_t):
    def body(dkv_ref, m_ref, g_ref, mn_ref, wt_ref, dw_ref, dg_ref):
        dkvb = dkv_ref[...].astype(BF16)
        dw_ref[...] = _dot_tn(dkvb, mn_ref[...]).astype(BF16)
        dmn = _dot(dkvb, wt_ref[...])
        mv = m_ref[...]
        dg_ref[...] = jnp.sum(dmn * (mv * _rstd(mv)), axis=0, keepdims=True)

    vm = pl.BlockSpec(memory_space=pltpu.VMEM)
    return pl.pallas_call(
        body, name="mem_kv_bwd",
        out_shape=(jax.ShapeDtypeStruct((2 * D, D), BF16), jax.ShapeDtypeStruct((1, D), F32)),
        in_specs=[vm] * 5, out_specs=(vm, vm),
        compiler_params=pltpu.CompilerParams(vmem_limit_bytes=VMEM_LIMIT),
    )(dkv, mem, g_mem, mn, w_xkv_t)


def _merge_bwd(dx1, y_att, y_sg, gates, wba_t, wbs_t, w_out, tm=512):
    S = dx1.shape[0]

    def body(d_ref, ya_ref, ys_ref, g_ref, wbat_ref, wbst_ref, wo_ref,
             mg_ref, da_ref, db_ref, dya_ref, dys_ref, dg_ref):
        dm = _dot_nt(d_ref[...].astype(BF16), wo_ref[...])
        a = _dot_nt(ya_ref[...], wbat_ref[...])
        b = _dot_nt(ys_ref[...], wbst_ref[...])
        sa = _sigmoid(g_ref[:, 0:D].astype(F32))
        sb = _sigmoid(g_ref[:, D:2 * D].astype(F32))
        mg_ref[...] = (sa * a + sb * b).astype(BF16)
        da = (dm * sa).astype(BF16)
        db = (dm * sb).astype(BF16)
        da_ref[...] = da
        db_ref[...] = db
        dg_ref[:, 0:D] = (dm * a * sa * (1.0 - sa)).astype(BF16)
        dg_ref[:, D:2 * D] = (dm * b * sb * (1.0 - sb)).astype(BF16)
        dya_ref[...] = _dot(da, wbat_ref[...]).astype(BF16)
        dys_ref[...] = _dot(db, wbst_ref[...]).astype(BF16)

    return pl.pallas_call(
        body, name="merge_bwd", grid=(S // tm,),
        out_shape=(jax.ShapeDtypeStruct((S, D), BF16), jax.ShapeDtypeStruct((S, D), BF16),
                   jax.ShapeDtypeStruct((S, D), BF16), jax.ShapeDtypeStruct((S, ATT_W), BF16),
                   jax.ShapeDtypeStruct((S, SG_W), BF16), jax.ShapeDtypeStruct((S, 2 * D), BF16)),
        in_specs=[_rows(tm, D), _rows(tm, ATT_W), _rows(tm, SG_W), _rows(tm, 2 * D),
                  _resident((D, ATT_W)), _resident((D, SG_W)), _resident((D, D))],
        out_specs=(_rows(tm, D), _rows(tm, D), _rows(tm, D), _rows(tm, ATT_W), _rows(tm, SG_W), _rows(tm, 2 * D)),
        compiler_params=_cparams(1),
    )(dx1, y_att, y_sg, gates, wba_t, wbs_t, w_out)


def _sgu_bwd(uv, dy_sg, lng, lnb, sg_w, b_exp, plans=(), tm=512):
    S = uv.shape[0]
    n_steps = S // tm

    def body(uv_ref, dy_ref, lng_ref, lnb_ref, w_ref, b_ref, duv_ref, dw_ref, dbx_ref, dlng_ref, dlnb_ref, dvln_ref):
        i = pl.program_id(0)

        @pl.when(i == 0)
        def _():
            dw_ref[...] = jnp.zeros_like(dw_ref)
            dbx_ref[...] = jnp.zeros_like(dbx_ref)
            dlng_ref[...] = jnp.zeros_like(dlng_ref)
            dlnb_ref[...] = jnp.zeros_like(dlnb_ref)

        lane = lax.broadcasted_iota(jnp.int32, (1, LANES), 1)
        lo = lane < 64
        mask = _sg_mask()
        lng = lng_ref[...]
        gu, dgelu_u = _gelu_and_grad(uv_ref[:, 0:SG_W].astype(F32))
        gv, dgelu_v = _gelu_and_grad(uv_ref[:, SG_W:2 * SG_W].astype(F32))
        vln, vhat, rstd = _sg_layernorm(gv, lng, lnb_ref[...])
        dy = dy_ref[...].astype(F32)
        dsv_all = dy * gu
        for gp in range(4):
            wf0 = jnp.where(mask, w_ref[2 * gp], 0)
            wf1 = jnp.where(mask, w_ref[2 * gp + 1], 0)
            w0 = wf0.astype(BF16)
            w1 = wf1.astype(BF16)
            cols = slice(128 * gp, 128 * (gp + 1))
            dw0 = jnp.zeros((128, 128), F32)
            dw1 = jnp.zeros((128, 128), F32)
            dbx = jnp.zeros((128, LANES), F32)
            for n in range(tm // 128):
                rows = slice(128 * n, 128 * (n + 1))
                vl = vln[rows, cols]
                vl0 = jnp.where(lo, vl, 0).astype(BF16)
                vl1 = jnp.where(lo, 0, vl).astype(BF16)
                sv = _dot(w0, vl0) + _dot(w1, vl1) + b_ref[:, cols]
                duv_ref[rows, cols] = (dy[rows, cols] * sv * dgelu_u[rows, cols]).astype(BF16)
                dsv = dsv_all[rows, cols]
                dbx = dbx + dsv
                ds0 = jnp.where(lo, dsv, 0).astype(BF16)
                ds1 = jnp.where(lo, 0, dsv).astype(BF16)
                dw0 = dw0 + _dot_nt(ds0, vl0)
                dw1 = dw1 + _dot_nt(ds1, vl1)
                dvln_ref[rows, cols] = _dot_tn(w0, ds0) + _dot_tn(w1, ds1)
            dw_ref[2 * gp] += jnp.where(mask, dw0, 0)
            dw_ref[2 * gp + 1] += jnp.where(mask, dw1, 0)
            dbx_ref[:, cols] += dbx
        dvln = dvln_ref[...]
        dlng_ref[...] += jnp.sum(dvln * vhat, axis=0, keepdims=True)
        dlnb_ref[...] += jnp.sum(dvln, axis=0, keepdims=True)
        dvh = dvln * lng
        dgv = rstd * (dvh - _group_sum64(dvh) * (1.0 / 64) - vhat * (_group_sum64(dvh * vhat) * (1.0 / 64)))
        duv_ref[:, SG_W:2 * SG_W] = (dgv * dgelu_v).astype(BF16)

        @pl.when(i == n_steps - 1)
        def _():
            dbx_ref[...] = _group_sum64(dbx_ref[...])

    return _call(
        body, name="sgu_bwd", grid=(n_steps,),
        out_shape=(jax.ShapeDtypeStruct((S, 2 * SG_W), BF16), jax.ShapeDtypeStruct((8, 128, 128), F32),
                   jax.ShapeDtypeStruct((128, SG_W), F32), jax.ShapeDtypeStruct((1, SG_W), F32),
                   jax.ShapeDtypeStruct((1, SG_W), F32)),
        in_specs=[_rows(tm, 2 * SG_W), _rows(tm, SG_W), _resident((1, SG_W)), _resident((1, SG_W)),
                  _resident((8, 128, 128)), _resident((128, SG_W))],
        out_specs=(_rows(tm, 2 * SG_W), pl.BlockSpec((8, 128, 128), lambda i: (0, 0, 0)),
                   pl.BlockSpec((128, SG_W), lambda i: (0, 0)), pl.BlockSpec((1, SG_W), lambda i: (0, 0)),
                   pl.BlockSpec((1, SG_W), lambda i: (0, 0))),
        scratch_shapes=[pltpu.VMEM((tm, SG_W), F32)],
        operands=(uv, dy_sg, lng, lnb, sg_w, b_exp), plans=plans)


def _attn_bwd(qkv, dy_att, y_att, lse, bias, plans=()):
    S = qkv.shape[0]
    R = ATT_R

    def body(q_ref, kp_ref, kc_ref, vp_ref, vc_ref, b_ref, dy_ref, y_ref, lse_ref, dq_ref, dk_ref, dv_ref, dss_ref,
             s_ref, dp_ref, pb_ref, dsb_ref):
        i = pl.program_id(1)

        @pl.when(i == 0)
        def _():
            dk_ref[...] = jnp.zeros_like(dk_ref)
            dv_ref[...] = jnp.zeros_like(dv_ref)
            dss_ref[...] = jnp.zeros_like(dss_ref)

        lo = lax.broadcasted_iota(jnp.int32, (1, LANES), 1) < 64
        kwin = jnp.concatenate([kp_ref[...], kc_ref[...]], axis=0)
        vwin = jnp.concatenate([vp_ref[...], vc_ref[...]], axis=0)
        for sub in range(ATT_SUBS):
            rows = slice(128 * sub, 128 * (sub + 1))
            kw = kwin[128 * sub:128 * sub + BAND_KEYS]
            vw = vwin[128 * sub:128 * sub + BAND_KEYS]
            table = jnp.where(i == 0, 1 + sub, 0)
            qs = _two_heads(q_ref[rows, :], lo)
            dos = _two_heads(dy_ref[rows, :], lo)
            dyy = dy_ref[rows, :].astype(F32) * y_ref[rows, :].astype(F32)
            delta = (jnp.sum(jnp.where(lo, dyy, 0.0), axis=-1, keepdims=True),
                     jnp.sum(jnp.where(lo, 0.0, dyy), axis=-1, keepdims=True))
            lse2 = lse_ref[rows, :]
            s_ref[...] = _dot_nt(qs, kw)
            dp_ref[...] = _dot_nt(dos, vw)
            for hh in range(2):
                lse = lse2[:, 64 * hh:64 * hh + 1]
                for r0 in range(0, 128, ATT_ROWS_BWD):
                    rr = slice(r0, r0 + ATT_ROWS_BWD)
                    both = slice(128 * hh + r0, 128 * hh + r0 + ATT_ROWS_BWD)
                    p = jnp.exp(s_ref[both, :] + b_ref[table, hh, rr, :] - lse[rr])
                    ds = p * (dp_ref[both, :] - delta[hh][rr])
                    dss_ref[hh, rr, :] += ds
                    pb_ref[both, :] = p.astype(BF16)
                    dsb_ref[both, :] = ds.astype(BF16)
            dq = _dot(jnp.concatenate([dsb_ref[0:128, :], dsb_ref[128:256, :]], axis=1), _two_heads(kw, lo))
            dq_ref[rows, :] = dq.astype(BF16)
            start = pl.multiple_of(i * R + 128 * sub, 128)
            dk_ref[pl.ds(start, BAND_KEYS), :] += _dot_tn(dsb_ref[...], qs)
            dv_ref[pl.ds(start, BAND_KEYS), :] += _dot_tn(pb_ref[...], dos)

    acc_spec = pl.BlockSpec((S + 8 * CHUNK, LANES), lambda j, i: (0, j))
    return _call(
        body, name="attn_bwd", grid=(4, S // R),
        out_shape=(jax.ShapeDtypeStruct((S, ATT_W), BF16), jax.ShapeDtypeStruct((S + 8 * CHUNK, ATT_W), F32),
                   jax.ShapeDtypeStruct((S + 8 * CHUNK, ATT_W), F32), jax.ShapeDtypeStruct((8, 128, BAND_KEYS), F32)),
        in_specs=_att_specs() + [pl.BlockSpec((R, LANES), lambda j, i: (i, j))] * 3,
        out_specs=(pl.BlockSpec((R, LANES), lambda j, i: (i, j)), acc_spec, acc_spec,
                   pl.BlockSpec((2, 128, BAND_KEYS), lambda j, i: (j, 0, 0))),
        scratch_shapes=[pltpu.VMEM((256, BAND_KEYS), F32), pltpu.VMEM((256, BAND_KEYS), F32),
                        pltpu.VMEM((256, BAND_KEYS), BF16), pltpu.VMEM((256, BAND_KEYS), BF16)],
        operands=(qkv, qkv, qkv, qkv, qkv, bias, dy_att, y_att, lse), plans=plans)


def _in_bwd(dq, dk, dv, duv, dgates, x, dx1, g1, w_in_t, plans=(), tm=512):
    S = x.shape[0]
    pad_blocks = (8 * CHUNK) // tm

    def body(dq_ref, dk_ref, dv_ref, duv_ref, dg_ref, x_ref, d1_ref, g_ref, wt_ref, dx_ref, dz_ref, dg1_ref):
        i = pl.program_id(0)
        dz_ref[:, 0:ATT_W] = (dq_ref[...].astype(F32) * 0.125).astype(BF16)
        dz_ref[:, ATT_W:2 * ATT_W] = dk_ref[...].astype(BF16)
        dz_ref[:, 2 * ATT_W:3 * ATT_W] = dv_ref[...].astype(BF16)
        dz_ref[:, 3 * ATT_W:3 * ATT_W + 2 * SG_W] = duv_ref[...]
        dz_ref[:, 3 * ATT_W + 2 * SG_W:IN_COLS] = dg_ref[...]
        dh = _dot(dz_ref[...], wt_ref[...])
        xv = x_ref[...]
        dx, dg1 = _rms_bwd(dh, xv, _rstd(xv), g_ref[...])
        dx_ref[...] = d1_ref[...] + dx

        @pl.when(i == 0)
        def _():
            dg1_ref[...] = jnp.zeros_like(dg1_ref)

        dg1_ref[...] += dg1

    shifted = pl.BlockSpec((tm, ATT_W), lambda i: (i + pad_blocks, 0))
    return _call(
        body, name="in_bwd", grid=(S // tm,),
        out_shape=(jax.ShapeDtypeStruct((S, D), F32), jax.ShapeDtypeStruct((S, IN_COLS), BF16),
                   jax.ShapeDtypeStruct((1, D), F32)),
        in_specs=[_rows(tm, ATT_W), shifted, shifted, _rows(tm, 2 * SG_W), _rows(tm, 2 * D), _rows(tm, D),
                  _rows(tm, D), _resident((1, D)), _resident((IN_COLS, D))],
        out_specs=(_rows(tm, D), _rows(tm, IN_COLS), pl.BlockSpec((1, D), lambda i: (0, 0))),
        operands=(dq, dk, dv, duv, dgates, x, dx1, g1, w_in_t), plans=plans)


def _adam_math(w, g, m, v):
    m = ADAM_B1 * m + (1.0 - ADAM_B1) * g
    v = ADAM_B2 * v + (1.0 - ADAM_B2) * (g * g)
    m_hat = m / (1.0 - ADAM_B1 ** ADAM_STEP)
    v_hat = v / (1.0 - ADAM_B2 ** ADAM_STEP)
    delta = -ADAM_LR * (m_hat / (jnp.sqrt(v_hat) + ADAM_EPS) + ADAM_WD * w)
    return delta, m, v


def _adam(parts, w, m, v, tr, name, transposed, after=None):
    P = parts.shape[0]
    R, C = w.shape

    def body(p_ref, w_ref, m_ref, v_ref, *rest):
        g_ref, d_ref, mo_ref, vo_ref = rest[-4:]
        if transposed:
            eye = (lax.broadcasted_iota(jnp.int32, (C, C), 0) == lax.broadcasted_iota(jnp.int32, (C, C), 1)).astype(BF16)
            part = lambda k: _dot_tn(p_ref[k], eye)
        else:
            part = lambda k: p_ref[k].astype(F32)
        g = part(0)
        for k in range(1, P):
            g = g + part(k)
        g_ref[...] = g
        d_ref[...], mo_ref[...], vo_ref[...] = _adam_math(w_ref[...], g, m_ref[...], v_ref[...])

    row = pl.BlockSpec((tr, C), lambda i: (i, 0))
    p_spec = pl.BlockSpec((P, C, tr), lambda i: (0, 0, i)) if transposed else pl.BlockSpec((P, tr, C), lambda i: (0, i, 0))
    extra = [] if after is None else [after]
    return pl.pallas_call(
        body, name=name, grid=(R // tr,),
        out_shape=tuple(jax.ShapeDtypeStruct((R, C), F32) for _ in range(4)),
        in_specs=[p_spec, row, row, row] + [pl.BlockSpec(memory_space=pl.ANY)] * len(extra),
        out_specs=(row, row, row, row),
        compiler_params=_cparams(1),
    )(parts, w, m, v, *extra)


def _my_place():
    return lax.axis_index("x"), lax.axis_index("y"), lax.axis_index("c")


def _other_chips(x, y):
    return [(1 - x, y), (x, 1 - y), (1 - x, 1 - y)]


class _Plan:
    def __init__(self, arrays, out_shapes, sems, start, finish, forward=None, forward_at=0.7):
        self.arrays, self.out_shapes, self.sems = list(arrays), list(out_shapes), list(sems)
        self.start, self.finish, self.forward, self.forward_at = start, finish, forward, forward_at


def _call(body, *, name, grid, in_specs, out_specs, out_shape, operands, scratch_shapes=(), plans=()):
    n_in, n_out, n_scr = len(operands), len(out_shape), len(scratch_shapes)
    p_in = [a for p in plans for a in p.arrays]
    p_out = [s for p in plans for s in p.out_shapes]
    p_sem = [s for p in plans for s in p.sems]
    steps = math.prod(grid)

    def wrapped(*refs):
        ins, refs = refs[:n_in], refs[n_in:]
        pins, refs = refs[:len(p_in)], refs[len(p_in):]
        outs, refs = refs[:n_out], refs[n_out:]
        pouts, refs = refs[:len(p_out)], refs[len(p_out):]
        scr, psems = refs[:n_scr], refs[n_scr:]
        step = 0
        for axis, size in enumerate(grid):
            step = step * size + pl.program_id(axis)
        bound = []
        for p in plans:
            bound.append((p, pins[:len(p.arrays)], pouts[:len(p.out_shapes)], psems[:len(p.sems)]))
            pins, pouts, psems = pins[len(p.arrays):], pouts[len(p.out_shapes):], psems[len(p.sems):]

        @pl.when(step == 0)
        def _():
            for p, a, b, s in bound:
                p.start(a, b, s)

        body(*ins, *outs, *scr)

        for p, a, b, s in bound:
            if p.forward is not None:
                @pl.when(step == min(int(p.forward_at * steps), steps - 1))
                def _(p=p, a=a, b=b, s=s):
                    p.forward(a, b, s)

        @pl.when(step == steps - 1)
        def _():
            for p, a, b, s in bound:
                p.finish(a, b, s)

    hbm = pl.BlockSpec(memory_space=pl.ANY)
    return pl.pallas_call(
        wrapped if plans else body, name=name, grid=grid,
        out_shape=tuple(out_shape) + tuple(p_out),
        in_specs=list(in_specs) + [hbm] * len(p_in),
        out_specs=tuple(out_specs) + tuple([hbm] * len(p_out)),
        scratch_shapes=list(scratch_shapes) + p_sem,
        compiler_params=_cparams(len(grid)),
    )(*operands, *p_in)


def _run_plan(plan, name):
    n_in, n_out = len(plan.arrays), len(plan.out_shapes)

    def body(*refs):
        a, b, s = refs[:n_in], refs[n_in:n_in + n_out], refs[n_in + n_out:]
        plan.start(a, b, s)
        if plan.forward is not None:
            plan.forward(a, b, s)
        plan.finish(a, b, s)

    hbm = pl.BlockSpec(memory_space=pl.ANY)
    return pl.pallas_call(
        body, name=name, out_shape=tuple(plan.out_shapes),
        in_specs=[hbm] * n_in, out_specs=tuple([hbm] * n_out), scratch_shapes=plan.sems,
    )(*plan.arrays)


def _gather_plan(shards, forward_at=0.7):
    n = len(shards)

    def copies(ins, outs, sems):
        send_sems, recv_sems, local_sems = sems
        x, y, c = _my_place()
        me, sibling = (x, y, c), (x, y, 1 - c)
        chips = _other_chips(x, y)

        def idx(p):
            return 4 * p[0] + 2 * p[1] + p[2]

        def copy(a, k, block, to, src=None):
            return pltpu.make_async_remote_copy(
                src_ref=outs[a].at[idx(block)] if src is None else src, dst_ref=outs[a].at[idx(block)],
                send_sem=send_sems.at[a, k], recv_sem=recv_sems.at[a, k], device_id=to, device_id_type=MESH)

        mine = [pltpu.make_async_copy(ins[a], outs[a].at[idx(me)], local_sems.at[a]) for a in range(n)]
        first = []
        for a in range(n):
            first.append(copy(a, 0, me, sibling, src=ins[a]))
            first += [copy(a, 1 + j, me, (*chip, c), src=ins[a]) for j, chip in enumerate(chips)]
        arrived = [copy(a, 1 + j, (*chip, c), me) for j, chip in enumerate(chips) for a in range(n)]
        passed = [copy(a, 4 + j, (*chip, c), sibling) for j, chip in enumerate(chips) for a in range(n)]
        from_sibling = []
        for a in range(n):
            from_sibling.append(copy(a, 0, sibling, me))
            from_sibling += [copy(a, 4 + j, (*chip, 1 - c), me) for j, chip in enumerate(chips)]
        return mine, first, arrived, passed, from_sibling

    def start(ins, outs, sems):
        mine, first, _, _, _ = copies(ins, outs, sems)
        for cp in mine + first:
            cp.start()

    def forward(ins, outs, sems):
        _, _, arrived, passed, _ = copies(ins, outs, sems)
        for landed, onward in zip(arrived, passed):
            landed.wait_recv()
            onward.start()

    def finish(ins, outs, sems):
        mine, first, _, passed, from_sibling = copies(ins, outs, sems)
        for cp in from_sibling:
            cp.wait_recv()
        for cp in first + passed:
            cp.wait_send()
        for cp in mine:
            cp.wait()

    return _Plan(shards, [jax.ShapeDtypeStruct((N_DEV,) + s.shape, s.dtype) for s in shards],
                 [pltpu.SemaphoreType.DMA((n, 7)), pltpu.SemaphoreType.DMA((n, 7)), pltpu.SemaphoreType.DMA((n,))],
                 start, finish, forward, forward_at)


def _sibling_plan(scatter, whole=()):
    ns = len(scatter)
    n = ns + len(whole)

    def copies(ins, outs, sems):
        send_sems, recv_sems = sems
        x, y, c = _my_place()
        out = []
        for a in range(n):
            for k in range(4 if a < ns else 1):
                src = ins[a].at[2 * k + (1 - c)] if a < ns else ins[a]
                dst = outs[a].at[k] if a < ns else outs[a]
                out.append(pltpu.make_async_remote_copy(
                    src_ref=src, dst_ref=dst, send_sem=send_sems.at[a, k], recv_sem=recv_sems.at[a, k],
                    device_id=(x, y, 1 - c), device_id_type=MESH))
        return out

    def start(ins, outs, sems):
        for cp in copies(ins, outs, sems):
            cp.start()

    def finish(ins, outs, sems):
        for cp in copies(ins, outs, sems):
            cp.wait()

    shapes = [jax.ShapeDtypeStruct((4,) + p.shape[1:], p.dtype) for p in scatter]
    shapes += [jax.ShapeDtypeStruct(p.shape, p.dtype) for p in whole]
    return _Plan(list(scatter) + list(whole), shapes,
                 [pltpu.SemaphoreType.DMA((n, 4)), pltpu.SemaphoreType.DMA((n, 4))], start, finish)


def _pair_sums(mine, theirs, c_idx, name):
    n = len(mine)

    def body(c_ref, *refs):
        for a in range(n):
            out = refs[2 * n + a]
            out[...] = (refs[a][...].astype(F32) + refs[n + a][...].astype(F32)).astype(out.dtype)

    def block(arr):
        return (1,) + arr.shape[1:]

    grid_spec = pltpu.PrefetchScalarGridSpec(
        num_scalar_prefetch=1, grid=(4,),
        in_specs=[pl.BlockSpec(block(m), lambda k, c_ref: (2 * k + c_ref[0], 0, 0)) for m in mine]
        + [pl.BlockSpec(block(t), lambda k, c_ref: (k, 0, 0)) for t in theirs],
        out_specs=tuple(pl.BlockSpec(block(t), lambda k, c_ref: (k, 0, 0)) for t in theirs))
    return pl.pallas_call(
        body, name=name, grid_spec=grid_spec,
        out_shape=tuple(jax.ShapeDtypeStruct(t.shape, m.dtype) for m, t in zip(mine, theirs)),
        compiler_params=_cparams(1),
    )(c_idx, *mine, *theirs)


def _peers_plan(arrays):
    n = len(arrays)

    def copies(ins, outs, sems):
        send_sems, recv_sems, local_sems = sems
        x, y, c = _my_place()
        me = 4 * x + 2 * y + c
        out = [pltpu.make_async_copy(ins[a], outs[a].at[me], local_sems.at[a]) for a in range(n)]
        for a in range(n):
            for k in range(N_DEV - 1):
                bits = k + 1
                peer = (x ^ (bits >> 2), y ^ ((bits >> 1) & 1), c ^ (bits & 1))
                out.append(pltpu.make_async_remote_copy(
                    src_ref=ins[a], dst_ref=outs[a].at[me], send_sem=send_sems.at[a, k], recv_sem=recv_sems.at[a, k],
                    device_id=peer, device_id_type=MESH))
        return out

    def start(ins, outs, sems):
        for cp in copies(ins, outs, sems):
            cp.start()

    def finish(ins, outs, sems):
        for cp in copies(ins, outs, sems):
            cp.wait()

    return _Plan(list(arrays), [jax.ShapeDtypeStruct((N_DEV,) + a.shape, a.dtype) for a in arrays],
                 [pltpu.SemaphoreType.DMA((n, N_DEV - 1)), pltpu.SemaphoreType.DMA((n, N_DEV - 1)),
                  pltpu.SemaphoreType.DMA((n,))], start, finish)


def _chips_plan(scatter, whole=()):
    ns = len(scatter)
    n = ns + len(whole)

    def copies(ins, outs, sems):
        send_sems, recv_sems, local_sems = sems
        x, y, c = _my_place()
        my_chip = 2 * x + y

        def src(a, k):
            return ins[a].at[k] if a < ns else ins[a]

        local = [pltpu.make_async_copy(src(a, my_chip), outs[a].at[my_chip], local_sems.at[a]) for a in range(n)]
        remote = []
        for a in range(n):
            for j, (px, py) in enumerate(_other_chips(x, y)):
                remote.append(pltpu.make_async_remote_copy(
                    src_ref=src(a, 2 * px + py), dst_ref=outs[a].at[my_chip],
                    send_sem=send_sems.at[a, j], recv_sem=recv_sems.at[a, j],
                    device_id=(px, py, c), device_id_type=MESH))
        return local + remote

    def start(ins, outs, sems):
        for cp in copies(ins, outs, sems):
            cp.start()

    def finish(ins, outs, sems):
        for cp in copies(ins, outs, sems):
            cp.wait()

    shapes = [jax.ShapeDtypeStruct(s.shape, s.dtype) for s in scatter]
    shapes += [jax.ShapeDtypeStruct((4,) + s.shape, s.dtype) for s in whole]
    return _Plan(list(scatter) + list(whole), shapes,
                 [pltpu.SemaphoreType.DMA((n, 3)), pltpu.SemaphoreType.DMA((n, 3)), pltpu.SemaphoreType.DMA((n,))],
                 start, finish)


def _chip_copies(src_ref, land_ref, send_sems, recv_sems):
    x, y, c = _my_place()
    return [pltpu.make_async_remote_copy(
        src_ref=src_ref.at[2 * px + py], dst_ref=land_ref.at[2 * x + y], send_sem=send_sems.at[j],
        recv_sem=recv_sems.at[j], device_id=(px, py, c), device_id_type=MESH)
        for j, (px, py) in enumerate(_other_chips(x, y))]


def _chips_start(src, name):
    def body(src_ref, land_ref, send_sems, recv_sems, src_thru, land_thru, token):
        for cp in _chip_copies(src_ref, land_ref, send_sems, recv_sems):
            cp.start()
        token[...] = jnp.zeros_like(token)

    hbm = pl.BlockSpec(memory_space=pltpu.HBM)
    sem = pl.BlockSpec(memory_space=pltpu.SEMAPHORE)
    return pl.pallas_call(
        body, name=name,
        out_shape=(pltpu.SemaphoreType.DMA((3,)), pltpu.SemaphoreType.DMA((3,)), pltpu.HBM(src.shape, src.dtype),
                   pltpu.HBM(src.shape, src.dtype), jax.ShapeDtypeStruct((8, LANES), F32)),
        in_specs=(hbm, hbm), out_specs=(sem, sem, hbm, hbm, pl.BlockSpec(memory_space=pltpu.VMEM)),
        input_output_aliases={0: 2, 1: 3},
        compiler_params=pltpu.CompilerParams(has_side_effects=pltpu.SideEffectType.DATAFLOW_SIDE_EFFECTING),
    )(pltpu.with_memory_space_constraint(src, pltpu.HBM),
      pltpu.with_memory_space_constraint(jnp.zeros(src.shape, src.dtype), pltpu.HBM))


def _chips_wait(send_sems, recv_sems, src_thru, land_thru, after, name):
    def body(src_ref, land_ref, send_sems, recv_sems, after_ref, src_dead, got_ref):
        for cp in _chip_copies(src_ref, land_ref, send_sems, recv_sems):
            cp.wait_send()
            cp.wait_recv()

    hbm = pl.BlockSpec(memory_space=pltpu.HBM)
    sem = pl.BlockSpec(memory_space=pltpu.SEMAPHORE)
    return pl.pallas_call(
        body, name=name,
        out_shape=(pltpu.HBM(src_thru.shape, src_thru.dtype), pltpu.HBM(land_thru.shape, land_thru.dtype)),
        in_specs=(hbm, hbm, sem, sem, pl.BlockSpec(memory_space=pl.ANY)), out_specs=(hbm, hbm),
        input_output_aliases={0: 0, 1: 1},
        compiler_params=pltpu.CompilerParams(has_side_effects=pltpu.SideEffectType.DATAFLOW_SIDE_EFFECTING),
    )(src_thru, land_thru, send_sems, recv_sems, after)


def _small_update(parts, w, m, v, loss_parts, name):
    n = len(parts)

    def total(ref):
        acc = ref[0]
        for k in range(1, ref.shape[0]):
            acc = acc + ref[k]
        return acc

    def body(*refs):
        p_refs, w_refs, m_refs, v_refs = (refs[i * n:(i + 1) * n] for i in range(4))
        lp_ref = refs[4 * n]
        outs = refs[4 * n + 1:]
        g_refs, d_refs, mo_refs, vo_refs = (outs[i * n:(i + 1) * n] for i in range(4))
        for a in range(n):
            g = total(p_refs[a])
            g_refs[a][...] = g
            d_refs[a][...], mo_refs[a][...], vo_refs[a][...] = _adam_math(w_refs[a][...], g, m_refs[a][...],
                                                                          v_refs[a][...])
        outs[4 * n][...] = total(lp_ref)

    vm = pl.BlockSpec(memory_space=pltpu.VMEM)
    shapes = [jax.ShapeDtypeStruct(t.shape, F32) for _ in range(4) for t in w]
    shapes.append(jax.ShapeDtypeStruct(loss_parts.shape[1:], F32))
    outs = pl.pallas_call(
        body, name=name, out_shape=tuple(shapes),
        in_specs=[vm] * (4 * n + 1), out_specs=tuple([vm] * (4 * n + 1)),
        compiler_params=pltpu.CompilerParams(vmem_limit_bytes=VMEM_LIMIT),
    )(*parts, *w, *m, *v, loss_parts)
    return outs[0:n], outs[n:2 * n], outs[2 * n:3 * n], outs[3 * n:4 * n], outs[4 * n]


BIG = [("w_in", 1), ("w_branch_att", 1), ("w_branch_sg", 1), ("w_out", 0), ("w_xq", 0), ("w_xkv", 1), ("w_xo", 0),
       ("w_ffn_in", 1), ("w_ffn_out", 0)]
SMALL = [("norm_mix_g", (1, D)), ("rel_bias", (8, NREL)), ("sg_ln_g", (8, 64)), ("sg_ln_b", (8, 64)),
         ("sg_w", (8, 128, 128)), ("sg_b", (8, 128)), ("norm_xattn_g", (1, D)), ("norm_mem_g", (1, D)),
         ("norm_ffn_g", (1, D)), ("norm_final_g", (1, D))]
ADAM_ROWS = {"w_in": 192, "w_branch_att": 512, "w_branch_sg": 512, "w_xkv": 1024, "w_ffn_in": 176}


def _full(gathered):
    return gathered.reshape(N_DEV * gathered.shape[1], gathered.shape[2])


def _blocks(grad):
    return grad.reshape(N_DEV, grad.shape[0] // N_DEV, grad.shape[1])


def kernel(x, mem, norm_mix_g, w_in, rel_bias, sg_ln_g, sg_ln_b, sg_w, sg_b, w_branch_att, w_branch_sg, w_out, norm_xattn_g, norm_mem_g, w_xq, w_xkv, w_xo, norm_ffn_g, w_ffn_in, w_ffn_out, norm_final_g, loss_target, m_norm_mix_g, m_w_in, m_rel_bias, m_sg_ln_g, m_sg_ln_b, m_sg_w, m_sg_b, m_w_branch_att, m_w_branch_sg, m_w_out, m_norm_xattn_g, m_norm_mem_g, m_w_xq, m_w_xkv, m_w_xo, m_norm_ffn_g, m_w_ffn_in, m_w_ffn_out, m_norm_final_g, v_norm_mix_g, v_w_in, v_rel_bias, v_sg_ln_g, v_sg_ln_b, v_sg_w, v_sg_b, v_w_branch_att, v_w_branch_sg, v_w_out, v_norm_xattn_g, v_norm_mem_g, v_w_xq, v_w_xkv, v_w_xo, v_norm_ffn_g, v_w_ffn_in, v_w_ffn_out, v_norm_final_g):
    args = dict(locals())
    big_names = [n for n, _ in BIG]
    small_names = [n for n, _ in SMALL]
    S = x.shape[1]

    x, mem, tgt = x.reshape(S, D), mem.reshape(MEM, D), loss_target.reshape(S, D)
    small = {n: args[n].reshape(shape) for n, shape in SMALL}
    g1, g2, g3 = small["norm_mix_g"], small["norm_xattn_g"], small["norm_ffn_g"]
    g_mem, g4 = small["norm_mem_g"], small["norm_final_g"]
    lng = small["sg_ln_g"].reshape(1, SG_W)
    lnb = small["sg_ln_b"].reshape(1, SG_W)
    b_exp = jnp.broadcast_to(small["sg_b"].T[:, :, None], (128, 8, 64)).reshape(128, SG_W)
    rel_pad = jnp.pad(small["rel_bias"], ((0, 0), (0, 384 - NREL)))
    c_idx = lax.axis_index("c").astype(jnp.int32).reshape(1)

    shard = {n: (args[n][0].T if axis == 1 else args[n][0]).astype(BF16) for n, axis in BIG}
    h, w_in_gathered = _norm_in(x, g1, plans=[_gather_plan([shard["w_in"]])])
    w_in_t = _full(w_in_gathered)
    bias = _bias_table(rel_pad)
    mix_names = ["w_branch_att", "w_branch_sg", "w_out", "w_xq", "w_xkv", "w_xo"]
    qkv, uv, gates, *got = _in_proj(h, w_in_t, plans=[_gather_plan([shard[n] for n in mix_names])])
    wba_t, wbs_t, w_out_f, w_xq_f, w_xkv_t, w_xo_f = (_full(g) for g in got)
    y_att, lse, *got = _attn_fwd(qkv, bias, plans=[_gather_plan([shard["w_ffn_in"], shard["w_ffn_out"]])])
    w_ffn_in_t, w_ffn_out_f = (_full(g) for g in got)
    y_sg = _sgu_fwd(uv, lng, lnb, small["sg_w"], b_exp)
    x1 = _merge_fwd(x, y_att, y_sg, gates, wba_t, wbs_t, w_out_f)
    kv, mn = _mem_kv(mem, g_mem, w_xkv_t)
    x2, hx, qx, o_x = _xattn_fwd(x1, g2, w_xq_f, kv, w_xo_f)
    dx3, gu, hf, act, loss_part, dg4 = _ffn_fwd(x2, tgt, g3, w_ffn_in_t, w_ffn_out_f, g4)


    dx2, dgu, dg3 = _ffn_bwd(dx3, gu, x2, g3, w_ffn_out_f, w_ffn_in_t)
    ffn_names = ["w_ffn_out", "w_ffn_in"]
    ffn_mine = [_blocks(_dw(act, dx3, 1408, 1024, DW_TOKENS, "dw_ffn_out")),
                _blocks(_dw(dgu, hf, 1408, 1024, DW_TOKENS, "dw_ffn_in"))]
    dx1, dq_x, dkv, dg2, *ffn_theirs = _xattn_bwd(dx2, x1, qx, g2, w_xq_f, w_xo_f, kv,
                                                  plans=[_sibling_plan(ffn_mine)])
    ffn_chip = _pair_sums(ffn_mine, ffn_theirs, c_idx, "rs_pair_ffn")
    d_xkv, dg_mem = _mem_kv_bwd(dkv, mem, g_mem, mn, w_xkv_t)
    merged, d_a, d_b, dy_att, dy_sg, dgates = _merge_bwd(dx1, y_att, y_sg, gates, wba_t, wbs_t, w_out_f)
    mid_names = ["w_xo", "w_xq", "w_xkv", "w_out", "w_branch_att", "w_branch_sg"]
    mid_mine = [_blocks(g) for g in (
        _dw(o_x, dx2, 1024, 1024, DW_TOKENS, "dw_xo"), _dw(hx, dq_x, 1024, 1024, DW_TOKENS, "dw_xq"), d_xkv,
        _dw(merged, dx1, 1024, 1024, DW_TOKENS, "dw_out"), _dw(d_a, y_att, 1024, 512, DW_TOKENS, "dw_branch_att"),
        _dw(d_b, y_sg, 1024, 512, DW_TOKENS, "dw_branch_sg"))]
    duv, d_sgw, d_bx, d_lng, d_lnb, *got = _sgu_bwd(uv, dy_sg, lng, lnb, small["sg_w"], b_exp,
                                                    plans=[_chips_plan(ffn_chip), _sibling_plan(mid_mine)])
    ffn_all, mid_theirs = got[:2], got[2:]
    mid_chip = _pair_sums(mid_mine, mid_theirs, c_idx, "rs_pair_mid")
    dq, dk, dv, ds_sum, *mid_all = _attn_bwd(qkv, dy_att, y_att, lse, bias, plans=[_chips_plan(mid_chip)])
    d_rel = _bias_grad(ds_sum)

    grad_x, dz, dg1 = _in_bwd(dq, dk, dv, duv, dgates, x, dx1, g1, w_in_t)

    gs = {"norm_mix_g": dg1, "rel_bias": d_rel[:, :NREL], "sg_ln_g": d_lng.reshape(8, 64),
          "sg_ln_b": d_lnb.reshape(8, 64), "sg_w": d_sgw, "sg_b": d_bx.reshape(128, 8, 64)[:, :, 0].T,
          "norm_xattn_g": dg2, "norm_mem_g": dg_mem, "norm_ffn_g": dg3, "norm_final_g": dg4}
    d_in, *everyone = _dw(dz, h, 1152, 1024, DW_TOKENS, "dw_in",
                          plans=[_peers_plan([gs[n] for n in small_names] + [loss_part])])

    in_mine = [_blocks(d_in)]
    (in_theirs,) = _run_plan(_sibling_plan(in_mine), "rs_sibling")
    (in_chip,) = _pair_sums(in_mine, [in_theirs], c_idx, "rs_pair_w_in")
    send_sems, recv_sems, in_chip_thru, landing, token = _chips_start(in_chip, "rs_chips_start")
    all_parts = dict(zip(ffn_names + mid_names, list(ffn_all) + list(mid_all)))

    def adam(n, axis, parts, after):
        wmv = [args[p + n][0] for p in ("", "m_", "v_")]
        if axis == 1 and wmv[0].shape[1] % LANES != 0:
            outs = _adam(parts, *(t.T for t in wmv), ADAM_ROWS[n], "adam_" + n, transposed=False, after=after)
            return [t.T[None] for t in outs]
        tr = ADAM_ROWS[n] if axis == 1 else wmv[0].shape[0]
        return [t[None] for t in _adam(parts, *wmv, tr, "adam_" + n, transposed=(axis == 1), after=after)]

    res = {n: adam(n, axis, all_parts[n], token) for n, axis in BIG if n != "w_in"}
    in_chip, landing = _chips_wait(send_sems, recv_sems, in_chip_thru, landing, res["w_ffn_in"][1], "rs_chips_wait")
    my_chip = 2 * lax.axis_index("x") + lax.axis_index("y")
    own = lax.dynamic_slice_in_dim(in_chip, my_chip, 1, axis=0)
    res["w_in"] = adam("w_in", 1, lax.dynamic_update_slice_in_dim(landing, own, my_chip, axis=0), None)
    small_res = _small_update(
        everyone[:-1], [small[n] for n in small_names],
        [args["m_" + n].reshape(s) for n, s in SMALL], [args["v_" + n].reshape(s) for n, s in SMALL],
        everyone[-1], "adam_small")
    for i, n in enumerate(small_names):
        res[n] = [small_res[k][i].reshape(args[n].shape) for k in range(4)]
    loss = small_res[4][0, 0]

    order = ["norm_mix_g", "w_in", "rel_bias", "sg_ln_g", "sg_ln_b", "sg_w", "sg_b", "w_branch_att", "w_branch_sg",
             "w_out", "norm_xattn_g", "norm_mem_g", "w_xq", "w_xkv", "w_xo", "norm_ffn_g", "w_ffn_in", "w_ffn_out",
             "norm_final_g"]
    outs = [loss, grad_x.reshape(1, S, D)]
    for k in range(4):
        outs += [res[n][k] for n in order]
    return tuple(outs)
```

```python
import math

import jax
import jax.numpy as jnp
from jax import lax
from jax.experimental import pallas as pl
from jax.experimental.pallas import tpu as pltpu

F32 = jnp.float32
BF16 = jnp.bfloat16

D = 1024
ATT_W = 512
SG_W = 512
IN_COLS = 4608
DFF = 2816
MEM = 256
XH = 4
CHUNK = 64
BAND_KEYS = 640
ATT_R = 512
ATT_SUBS = ATT_R // 128
ATT_ROWS = 32
ATT_ROWS_BWD = 16
DW_TOKENS = 2048
IN_CHUNK = 256
FF_CHUNK = 256
MERGE_CHUNK = 256
REL_CLIP = 128
NREL = 2 * REL_CLIP + 1
EPS = 1e-6
NEG = -1e30
N_DEV = 8

ADAM_LR = 0.001
ADAM_B1 = 0.9
ADAM_B2 = 0.999
ADAM_EPS = 1e-08
ADAM_WD = 0.01
ADAM_STEP = 10

LANES = 128
VMEM_LIMIT = 56 * 1024 * 1024
MESH = pl.DeviceIdType.MESH


def _cparams(n_axes):
    return pltpu.CompilerParams(dimension_semantics=("arbitrary",) * n_axes, vmem_limit_bytes=VMEM_LIMIT)


def _resident(shape):
    zeros = (0,) * len(shape)
    return pl.BlockSpec(shape, lambda *_: zeros, pipeline_mode=pl.Buffered(1))


def _rows(tm, cols, col_block=0):
    return pl.BlockSpec((tm, cols), lambda i: (i, col_block))


def _sigmoid(x):
    return pl.reciprocal(1.0 + jnp.exp(-x), approx=True)


_GELU_C = math.sqrt(2.0 / math.pi)


def _gelu(x):
    t = jnp.tanh(_GELU_C * (x + 0.044715 * (x * x * x)))
    return x * (0.5 * (1.0 + t))


def _gelu_and_grad(x):
    x2 = x * x
    t = jnp.tanh(_GELU_C * (x + 0.044715 * (x2 * x)))
    cdf = 0.5 * (1.0 + t)
    dcdf = 0.5 * (1.0 - t * t) * (_GELU_C * (1.0 + 3.0 * 0.044715 * x2))
    return x * cdf, cdf + x * dcdf


def _rstd(x):
    return lax.rsqrt(jnp.mean(x * x, axis=-1, keepdims=True) + EPS)


def _rms_bwd(dh, x, r, g):
    xh = x * r
    dxh = dh * g
    dx = r * (dxh - xh * jnp.mean(dxh * xh, axis=-1, keepdims=True))
    dg = jnp.sum(dh * xh, axis=0, keepdims=True)
    return dx, dg


def _group_sum64(x):
    r = lax.broadcasted_iota(jnp.int32, (LANES, LANES), 0) // 64
    c = lax.broadcasted_iota(jnp.int32, (LANES, LANES), 1) // 64
    same_group = (r == c).astype(BF16)

    def one(v):
        hi = v.astype(BF16)
        rest = v - hi.astype(F32)
        mid = rest.astype(BF16)
        lo = (rest - mid.astype(F32)).astype(BF16)
        return _dot(hi, same_group) + _dot(mid, same_group) + _dot(lo, same_group)

    pieces = [one(x[:, LANES * j:LANES * (j + 1)]) for j in range(x.shape[1] // LANES)]
    return pieces[0] if len(pieces) == 1 else jnp.concatenate(pieces, axis=1)


def _dot(a, b):
    return jnp.dot(a, b, preferred_element_type=F32)


def _dot_nt(a, b):
    return lax.dot_general(a, b, (((1,), (1,)), ((), ())), preferred_element_type=F32)


def _dot_tn(a, b):
    return lax.dot_general(a, b, (((0,), (0,)), ((), ())), preferred_element_type=F32)


DIAGS = 768


def _diag_onehot():
    r_idx = lax.broadcasted_iota(jnp.int32, (384, DIAGS), 0)
    t_idx = lax.broadcasted_iota(jnp.int32, (384, DIAGS), 1)
    dist = (8 * CHUNK + 127) - t_idx
    return (jnp.clip(dist, -REL_CLIP, REL_CLIP) + REL_CLIP == r_idx).astype(F32)


def _shift_rows(x, reverse):
    row = lax.broadcasted_iota(jnp.int32, x.shape, 0)
    for k in range(7):
        amt = (DIAGS - (1 << k)) if reverse else (1 << k)
        x = jnp.where(((row >> k) & 1) == 1, pltpu.roll(x, amt, 1), x)
    return x


N_TABLES = 1 + ATT_SUBS


def _bias_table(rel_bias_pad):
    def body(rb_ref, out_ref):
        per_diag = jnp.dot(rb_ref[...], _diag_onehot(), preferred_element_type=F32,
                           precision=lax.Precision.HIGHEST)
        a = lax.broadcasted_iota(jnp.int32, (128, BAND_KEYS), 0)
        b = lax.broadcasted_iota(jnp.int32, (128, BAND_KEYS), 1)
        band = (b // CHUNK >= a // CHUNK) & (b // CHUNK <= a // CHUNK + 8)
        for h in range(8):
            rows = jnp.broadcast_to(per_diag[h:h + 1, :], (128, DIAGS))
            table = _shift_rows(pltpu.roll(rows, DIAGS - 127, 1), reverse=False)[:, :BAND_KEYS]
            out_ref[0, h] = jnp.where(band, table, NEG)
            for s in range(ATT_SUBS):
                out_ref[1 + s, h] = jnp.where(band & (b >= 8 * CHUNK - 128 * s), table, NEG)

    return pl.pallas_call(
        body, name="bias_table",
        out_shape=jax.ShapeDtypeStruct((N_TABLES, 8, 128, BAND_KEYS), F32),
        in_specs=[pl.BlockSpec(memory_space=pltpu.VMEM)],
        out_specs=pl.BlockSpec(memory_space=pltpu.VMEM),
        compiler_params=pltpu.CompilerParams(vmem_limit_bytes=VMEM_LIMIT),
    )(rel_bias_pad)


def _bias_grad(ds_sum):
    def body(ds_ref, out_ref):
        sums = []
        for h in range(8):
            padded = jnp.concatenate([ds_ref[h], jnp.zeros((128, DIAGS - BAND_KEYS), F32)], axis=1)
            skewed = pltpu.roll(_shift_rows(padded, reverse=True), 127, 1)
            sums.append(jnp.sum(skewed, axis=0, keepdims=True))
        per_diag = jnp.concatenate(sums, axis=0)
        out_ref[...] = lax.dot_general(per_diag, _diag_onehot(), (((1,), (1,)), ((), ())),
                                       preferred_element_type=F32, precision=lax.Precision.HIGHEST)

    return pl.pallas_call(
        body, name="bias_grad",
        out_shape=jax.ShapeDtypeStruct((8, 384), F32),
        in_specs=[pl.BlockSpec(memory_space=pltpu.VMEM)],
        out_specs=pl.BlockSpec(memory_space=pltpu.VMEM),
    )(ds_sum)


def _norm_in(x, g1, plans=(), tm=512):
    S = x.shape[0]

    def body(x_ref, g_ref, h_ref):
        xv = x_ref[...]
        h_ref[...] = (xv * _rstd(xv) * g_ref[...]).astype(BF16)

    return _call(
        body, name="norm_in", grid=(S // tm,),
        out_shape=(jax.ShapeDtypeStruct((S, D), BF16),),
        in_specs=[_rows(tm, D), _resident((1, D))], out_specs=(_rows(tm, D),),
        operands=(x, g1), plans=plans)


def _in_proj(h, w_in_t, plans=(), tm=512):
    S = h.shape[0]

    def body(h_ref, w_ref, qkv_ref, uv_ref, gate_ref):
        h = h_ref[...]
        w = IN_CHUNK
        for c in range(IN_COLS // w):
            zc = _dot_nt(h, w_ref[w * c:w * (c + 1), :])
            start = w * c
            if start < ATT_W:
                qkv_ref[:, start:start + w] = (zc * 0.125).astype(BF16)
            elif start < 3 * ATT_W:
                qkv_ref[:, start:start + w] = zc.astype(BF16)
            elif start < 3 * ATT_W + 2 * SG_W:
                uv_ref[:, start - 3 * ATT_W:start - 3 * ATT_W + w] = zc.astype(BF16)
            else:
                gate_ref[:, start - 3 * ATT_W - 2 * SG_W:start - 3 * ATT_W - 2 * SG_W + w] = zc.astype(BF16)

    return _call(
        body, name="in_proj", grid=(S // tm,),
        out_shape=(jax.ShapeDtypeStruct((S, 3 * ATT_W), BF16), jax.ShapeDtypeStruct((S, 2 * SG_W), BF16),
                   jax.ShapeDtypeStruct((S, 2 * D), BF16)),
        in_specs=[_rows(tm, D), _resident((IN_COLS, D))],
        out_specs=(_rows(tm, 3 * ATT_W), _rows(tm, 2 * SG_W), _rows(tm, 2 * D)),
        operands=(h, w_in_t), plans=plans)


def _two_heads(a, lo):
    return jnp.concatenate([jnp.where(lo, a, 0), jnp.where(lo, 0, a)], axis=0)


def _att_specs():
    R = ATT_R
    q = pl.BlockSpec((R, LANES), lambda j, i: (i, j))
    kp = pl.BlockSpec((R, LANES), lambda j, i: (jnp.maximum(i - 1, 0), 4 + j))
    kc = pl.BlockSpec((R, LANES), lambda j, i: (i, 4 + j))
    vp = pl.BlockSpec((R, LANES), lambda j, i: (jnp.maximum(i - 1, 0), 8 + j))
    vc = pl.BlockSpec((R, LANES), lambda j, i: (i, 8 + j))
    bias = pl.BlockSpec((N_TABLES, 2, 128, BAND_KEYS), lambda j, i: (0, j, 0, 0))
    return [q, kp, kc, vp, vc, bias]


def _attn_fwd(qkv, bias, plans=()):
    S = qkv.shape[0]
    R = ATT_R

    def body(q_ref, kp_ref, kc_ref, vp_ref, vc_ref, b_ref, o_ref, lse_ref, s_ref, p_ref, l_ref):
        i = pl.program_id(1)
        lo = lax.broadcasted_iota(jnp.int32, (1, LANES), 1) < 64
        kwin = jnp.concatenate([kp_ref[...], kc_ref[...]], axis=0)
        vwin = jnp.concatenate([vp_ref[...], vc_ref[...]], axis=0)
        for sub in range(ATT_SUBS):
            q2 = q_ref[128 * sub:128 * (sub + 1), :]
            kw = kwin[128 * sub:128 * sub + BAND_KEYS]
            vw = vwin[128 * sub:128 * sub + BAND_KEYS]
            table = jnp.where(i == 0, 1 + sub, 0)
            s_ref[...] = _dot_nt(_two_heads(q2, lo), kw)
            for hh in range(2):
                for r0 in range(0, 128, ATT_ROWS):
                    rows = slice(128 * hh + r0, 128 * hh + r0 + ATT_ROWS)
                    s = s_ref[rows, :] + b_ref[table, hh, r0:r0 + ATT_ROWS, :]
                    top = jnp.max(s, axis=-1, keepdims=True)
                    p = jnp.exp(s - top)
                    total = jnp.sum(p, axis=-1, keepdims=True)
                    p_ref[rows, :] = (p / total).astype(BF16)
                    l_ref[rows, :] = jnp.broadcast_to(top + jnp.log(total), (ATT_ROWS, LANES))
            o = _dot(jnp.concatenate([p_ref[0:128, :], p_ref[128:256, :]], axis=1), _two_heads(vw, lo))
            o_ref[128 * sub:128 * (sub + 1), :] = o.astype(BF16)
            lse_ref[128 * sub:128 * (sub + 1), :] = jnp.where(lo, l_ref[0:128, :], l_ref[128:256, :])

    blk = pl.BlockSpec((R, LANES), lambda j, i: (i, j))
    return _call(
        body, name="attn_fwd", grid=(4, S // R),
        out_shape=(jax.ShapeDtypeStruct((S, ATT_W), BF16), jax.ShapeDtypeStruct((S, ATT_W), F32)),
        in_specs=_att_specs(), out_specs=(blk, blk),
        scratch_shapes=[pltpu.VMEM((256, BAND_KEYS), F32), pltpu.VMEM((256, BAND_KEYS), BF16),
                        pltpu.VMEM((256, LANES), F32)],
        operands=(qkv, qkv, qkv, qkv, qkv, bias), plans=plans)


def _sg_mask():
    t = lax.broadcasted_iota(jnp.int32, (128, 128), 0)
    s = lax.broadcasted_iota(jnp.int32, (128, 128), 1)
    return (s // CHUNK) <= (t // CHUNK)


def _sg_layernorm(gv, lng, lnb):
    mu = _group_sum64(gv) * (1.0 / 64)
    xc = gv - mu
    var = _group_sum64(xc * xc) * (1.0 / 64)
    rstd = lax.rsqrt(var + EPS)
    vhat = xc * rstd
    return vhat * lng + lnb, vhat, rstd


def _sgu_fwd(uv, lng, lnb, sg_w, b_exp, tm=512):
    S = uv.shape[0]

    def body(uv_ref, lng_ref, lnb_ref, w_ref, b_ref, y_ref):
        lane = lax.broadcasted_iota(jnp.int32, (1, LANES), 1)
        lo = lane < 64
        mask = _sg_mask()
        gu = _gelu(uv_ref[:, 0:SG_W].astype(F32))
        vln, _, _ = _sg_layernorm(_gelu(uv_ref[:, SG_W:2 * SG_W].astype(F32)), lng_ref[...], lnb_ref[...])
        for gp in range(4):
            w0 = jnp.where(mask, w_ref[2 * gp], 0).astype(BF16)
            w1 = jnp.where(mask, w_ref[2 * gp + 1], 0).astype(BF16)
            cols = slice(128 * gp, 128 * (gp + 1))
            for n in range(tm // 128):
                rows = slice(128 * n, 128 * (n + 1))
                vl = vln[rows, cols]
                sv = (_dot(w0, jnp.where(lo, vl, 0).astype(BF16)) + _dot(w1, jnp.where(lo, 0, vl).astype(BF16))
                      + b_ref[:, cols])
                y_ref[rows, cols] = (gu[rows, cols] * sv).astype(BF16)

    return pl.pallas_call(
        body, name="sgu_fwd", grid=(S // tm,),
        out_shape=jax.ShapeDtypeStruct((S, SG_W), BF16),
        in_specs=[_rows(tm, 2 * SG_W), _resident((1, SG_W)), _resident((1, SG_W)),
                  _resident((8, 128, 128)), _resident((128, SG_W))],
        out_specs=_rows(tm, SG_W),
        compiler_params=_cparams(1),
    )(uv, lng, lnb, sg_w, b_exp)


def _merge_fwd(x, y_att, y_sg, gates, wba_t, wbs_t, w_out, tm=512):
    S = x.shape[0]

    def body(x_ref, ya_ref, ys_ref, g_ref, wba_ref, wbs_ref, wo_ref, x1_ref, mg_ref):
        ya, ys = ya_ref[...], ys_ref[...]
        for c in range(D // MERGE_CHUNK):
            cs = slice(MERGE_CHUNK * c, MERGE_CHUNK * (c + 1))
            gs = slice(D + MERGE_CHUNK * c, D + MERGE_CHUNK * (c + 1))
            a = _dot_nt(ya, wba_ref[cs, :])
            b = _dot_nt(ys, wbs_ref[cs, :])
            merged = _sigmoid(g_ref[:, cs].astype(F32)) * a + _sigmoid(g_ref[:, gs].astype(F32)) * b
            mg_ref[:, cs] = merged.astype(BF16)
        x1_ref[...] = x_ref[...] + _dot(mg_ref[...], wo_ref[...])

    return pl.pallas_call(
        body, name="merge_fwd", grid=(S // tm,),
        out_shape=jax.ShapeDtypeStruct((S, D), F32),
        in_specs=[_rows(tm, D), _rows(tm, ATT_W), _rows(tm, SG_W), _rows(tm, 2 * D),
                  _resident((D, ATT_W)), _resident((D, SG_W)), _resident((D, D))],
        out_specs=_rows(tm, D),
        scratch_shapes=[pltpu.VMEM((tm, D), BF16)],
        compiler_params=_cparams(1),
    )(x, y_att, y_sg, gates, wba_t, wbs_t, w_out)


def _mem_kv(mem, g_mem, w_xkv_t):
    def body(m_ref, g_ref, w_ref, kv_ref, mn_ref):
        mv = m_ref[...]
        mn = (mv * _rstd(mv) * g_ref[...]).astype(BF16)
        mn_ref[...] = mn
        kv_ref[...] = _dot_nt(mn, w_ref[...]).astype(BF16)

    vm = pl.BlockSpec(memory_space=pltpu.VMEM)
    return pl.pallas_call(
        body, name="mem_kv",
        out_shape=(jax.ShapeDtypeStruct((MEM, 2 * D), BF16), jax.ShapeDtypeStruct((MEM, D), BF16)),
        in_specs=[vm, vm, vm], out_specs=(vm, vm),
        compiler_params=pltpu.CompilerParams(vmem_limit_bytes=VMEM_LIMIT),
    )(mem, g_mem, w_xkv_t)


def _xatt_head(qx, kv_ref, h):
    hs = slice(256 * h, 256 * (h + 1))
    s = _dot_nt(qx[:, hs], kv_ref[:, hs])
    p = jnp.exp(s - jnp.max(s, axis=-1, keepdims=True))
    return p / jnp.sum(p, axis=-1, keepdims=True)


def _xattn_fwd(x1, g2, w_xq, kv, w_xo, tm=512):
    S = x1.shape[0]

    def body(x_ref, g_ref, wq_ref, kv_ref, wo_ref, x2_ref, hx_ref, qx_ref, o_ref):
        xv = x_ref[...]
        hx = (xv * _rstd(xv) * g_ref[...]).astype(BF16)
        hx_ref[...] = hx
        qx = (_dot(hx, wq_ref[...]) * (1.0 / 16)).astype(BF16)
        qx_ref[...] = qx
        for h in range(XH):
            p = _xatt_head(qx, kv_ref, h)
            o_ref[:, 256 * h:256 * (h + 1)] = _dot(p.astype(BF16), kv_ref[:, D + 256 * h:D + 256 * (h + 1)]).astype(BF16)
        x2_ref[...] = xv + _dot(o_ref[...], wo_ref[...])

    return pl.pallas_call(
        body, name="xattn_fwd", grid=(S // tm,),
        out_shape=(jax.ShapeDtypeStruct((S, D), F32),) + (jax.ShapeDtypeStruct((S, D), BF16),) * 3,
        in_specs=[_rows(tm, D), _resident((1, D)), _resident((D, D)), _resident((MEM, 2 * D)), _resident((D, D))],
        out_specs=(_rows(tm, D),) * 4,
        compiler_params=_cparams(1),
    )(x1, g2, w_xq, kv, w_xo)


def _ffn_fwd(x2, tgt, g3, w_ffn_in_t, w_ffn_out, g4, tm=256):
    S = x2.shape[0]

    def body(x_ref, t_ref, g3_ref, wi_ref, wo_ref, g4_ref, dx3_ref, gu_ref, hf_ref, act_ref, loss_ref, dg4_ref):
        i = pl.program_id(0)
        xv = x_ref[...]
        hf = (xv * _rstd(xv) * g3_ref[...]).astype(BF16)
        hf_ref[...] = hf
        for c in range(DFF // FF_CHUNK):
            cs = slice(FF_CHUNK * c, FF_CHUNK * (c + 1))
            us = slice(DFF + FF_CHUNK * c, DFF + FF_CHUNK * (c + 1))
            gate = _dot_nt(hf, wi_ref[cs, :])
            up = _dot_nt(hf, wi_ref[us, :])
            gu_ref[:, cs] = gate.astype(BF16)
            gu_ref[:, us] = up.astype(BF16)
            act_ref[:, cs] = ((gate * _sigmoid(gate)) * up).astype(BF16)
        acc = xv + _dot(act_ref[...], wo_ref[...])
        r4 = _rstd(acc)
        g4 = g4_ref[...]
        diff = acc * r4 * g4 - t_ref[...]
        dy = diff * (1.0 / D)
        dx3, dg4 = _rms_bwd(dy, acc, r4, g4)
        dx3_ref[...] = dx3
        part = 0.5 * jnp.sum(jnp.mean(diff * diff, axis=-1, keepdims=True))

        @pl.when(i == 0)
        def _():
            loss_ref[...] = jnp.zeros_like(loss_ref)
            dg4_ref[...] = jnp.zeros_like(dg4_ref)

        loss_ref[...] += jnp.full(loss_ref.shape, part, F32)
        dg4_ref[...] += dg4

    return pl.pallas_call(
        body, name="ffn_fwd", grid=(S // tm,),
        out_shape=(jax.ShapeDtypeStruct((S, D), F32), jax.ShapeDtypeStruct((S, 2 * DFF), BF16),
                   jax.ShapeDtypeStruct((S, D), BF16), jax.ShapeDtypeStruct((S, DFF), BF16),
                   jax.ShapeDtypeStruct((8, LANES), F32), jax.ShapeDtypeStruct((1, D), F32)),
        in_specs=[_rows(tm, D), _rows(tm, D), _resident((1, D)), _resident((2 * DFF, D)), _resident((DFF, D)),
                  _resident((1, D))],
        out_specs=(_rows(tm, D), _rows(tm, 2 * DFF), _rows(tm, D), _rows(tm, DFF),
                   pl.BlockSpec((8, LANES), lambda i: (0, 0)), pl.BlockSpec((1, D), lambda i: (0, 0))),
        compiler_params=_cparams(1),
    )(x2, tgt, g3, w_ffn_in_t, w_ffn_out, g4)


def _ffn_bwd(dx3, gu, x2, g3, w_ffn_out, w_ffn_in_t, tm=256):
    S = x2.shape[0]

    def body(d_ref, gu_ref, x_ref, g3_ref, wo_ref, wit_ref, dx2_ref, dgu_ref, dg3_ref):
        i = pl.program_id(0)
        d3 = d_ref[...]
        d3b = d3.astype(BF16)
        for c in range(DFF // FF_CHUNK):
            cs = slice(FF_CHUNK * c, FF_CHUNK * (c + 1))
            us = slice(DFF + FF_CHUNK * c, DFF + FF_CHUNK * (c + 1))
            da = _dot_nt(d3b, wo_ref[cs, :])
            gate = gu_ref[:, cs].astype(F32)
            up = gu_ref[:, us].astype(F32)
            sg = _sigmoid(gate)
            dgate = (da * up * (sg * (1.0 + gate * (1.0 - sg)))).astype(BF16)
            dup = (da * (gate * sg)).astype(BF16)
            dgu_ref[:, cs] = dgate
            dgu_ref[:, us] = dup
        dhf = _dot(dgu_ref[...], wit_ref[...])
        xv = x_ref[...]
        dx, dg3 = _rms_bwd(dhf, xv, _rstd(xv), g3_ref[...])
        dx2_ref[...] = d3 + dx

        @pl.when(i == 0)
        def _():
            dg3_ref[...] = jnp.zeros_like(dg3_ref)

        dg3_ref[...] += dg3

    return pl.pallas_call(
        body, name="ffn_bwd", grid=(S // tm,),
        out_shape=(jax.ShapeDtypeStruct((S, D), F32), jax.ShapeDtypeStruct((S, 2 * DFF), BF16),
                   jax.ShapeDtypeStruct((1, D), F32)),
        in_specs=[_rows(tm, D), _rows(tm, 2 * DFF), _rows(tm, D), _resident((1, D)),
                  _resident((DFF, D)), _resident((2 * DFF, D))],
        out_specs=(_rows(tm, D), _rows(tm, 2 * DFF), pl.BlockSpec((1, D), lambda i: (0, 0))),
        compiler_params=_cparams(1),
    )(dx3, gu, x2, g3, w_ffn_out, w_ffn_in_t)


def _dw(a, b, tmm, tn, ts, name, out_dtype=BF16, plans=()):
    S, M = a.shape
    N = b.shape[1]
    ts = min(ts, S)
    nk = S // ts

    def body(a_ref, b_ref, o_ref, acc_ref):
        k = pl.program_id(2)

        @pl.when(k == 0)
        def _():
            acc_ref[...] = jnp.zeros_like(acc_ref)

        acc_ref[...] += _dot_tn(a_ref[...].astype(BF16), b_ref[...].astype(BF16))

        @pl.when(k == nk - 1)
        def _():
            o_ref[...] = acc_ref[...].astype(out_dtype)

    out = _call(
        body, name=name, grid=(M // tmm, N // tn, nk),
        out_shape=(jax.ShapeDtypeStruct((M, N), out_dtype),),
        in_specs=[pl.BlockSpec((ts, tmm), lambda m, n, k: (k, m)), pl.BlockSpec((ts, tn), lambda m, n, k: (k, n))],
        out_specs=(pl.BlockSpec((tmm, tn), lambda m, n, k: (m, n)),),
        scratch_shapes=[pltpu.VMEM((tmm, tn), F32)],
        operands=(a, b), plans=plans)
    return out if plans else out[0]


def _xattn_bwd(dx2, x1, qx, g2, w_xq, w_xo, kv, plans=(), tm=512):
    S = x1.shape[0]

    def body(d_ref, x_ref, qx_ref, g_ref, wq_ref, wo_ref, kv_ref, dx1_ref, dq_ref, dkv_ref, dg2_ref):
        i = pl.program_id(0)

        @pl.when(i == 0)
        def _():
            dkv_ref[...] = jnp.zeros_like(dkv_ref)
            dg2_ref[...] = jnp.zeros_like(dg2_ref)

        d2 = d_ref[...]
        qx = qx_ref[...]
        do = _dot_nt(d2.astype(BF16), wo_ref[...]).astype(BF16)
        for h in range(XH):
            hs = slice(256 * h, 256 * (h + 1))
            vs = slice(D + 256 * h, D + 256 * (h + 1))
            p = _xatt_head(qx, kv_ref, h)
            dp = _dot_nt(do[:, hs], kv_ref[:, vs])
            ds = (p * (dp - jnp.sum(dp * p, axis=-1, keepdims=True))).astype(BF16)
            dq_ref[:, hs] = (_dot(ds, kv_ref[:, hs]) * (1.0 / 16)).astype(BF16)
            dkv_ref[:, hs] += _dot_tn(ds, qx[:, hs])
            dkv_ref[:, vs] += _dot_tn(p.astype(BF16), do[:, hs])
        dhx = _dot_nt(dq_ref[...], wq_ref[...])
        xv = x_ref[...]
        dx, dg2 = _rms_bwd(dhx, xv, _rstd(xv), g_ref[...])
        dx1_ref[...] = d2 + dx
        dg2_ref[...] += dg2

    return _call(
        body, name="xattn_bwd", grid=(S // tm,),
        out_shape=(jax.ShapeDtypeStruct((S, D), F32), jax.ShapeDtypeStruct((S, D), BF16),
                   jax.ShapeDtypeStruct((MEM, 2 * D), F32), jax.ShapeDtypeStruct((1, D), F32)),
        in_specs=[_rows(tm, D), _rows(tm, D), _rows(tm, D), _resident((1, D)), _resident((D, D)), _resident((D, D)),
                  _resident((MEM, 2 * D))],
        out_specs=(_rows(tm, D), _rows(tm, D),
                   pl.BlockSpec((MEM, 2 * D), lambda i: (0, 0)), pl.BlockSpec((1, D), lambda i: (0, 0))),
        operands=(dx2, x1, qx, g2, w_xq, w_xo, kv), plans=plans)


def _mem_kv_bwd(dkv, mem, g_mem, mn, w_xkv_t):
    def body(dkv_ref, m_ref, g_ref, mn_ref, wt_ref, dw_ref, dg_ref):
        dkvb = dkv_ref[...].astype(BF16)
        dw_ref[...] = _dot_tn(dkvb, mn_ref[...]).astype(BF16)
        dmn = _dot(dkvb, wt_ref[...])
        mv = m_ref[...]
        dg_ref[...] = jnp.sum(dmn * (mv * _rstd(mv)), axis=0, keepdims=True)

    vm = pl.BlockSpec(memory_space=pltpu.VMEM)
    return pl.pallas_call(
        body, name="mem_kv_bwd",
        out_shape=(jax.ShapeDtypeStruct((2 * D, D), BF16), jax.ShapeDtypeStruct((1, D), F32)),
        in_specs=[vm] * 5, out_specs=(vm, vm),
        compiler_params=pltpu.CompilerParams(vmem_limit_bytes=VMEM_LIMIT),
    )(dkv, mem, g_mem, mn, w_xkv_t)


def _merge_bwd(dx1, y_att, y_sg, gates, wba_t, wbs_t, w_out, tm=512):
    S = dx1.shape[0]

    def body(d_ref, ya_ref, ys_ref, g_ref, wbat_ref, wbst_ref, wo_ref,
             mg_ref, da_ref, db_ref, dya_ref, dys_ref, dg_ref):
        d, ya, ys = d_ref[...].astype(BF16), ya_ref[...], ys_ref[...]
        for c in range(D // MERGE_CHUNK):
            cs = slice(MERGE_CHUNK * c, MERGE_CHUNK * (c + 1))
            gs = slice(D + MERGE_CHUNK * c, D + MERGE_CHUNK * (c + 1))
            dm = _dot_nt(d, wo_ref[cs, :])
            a = _dot_nt(ya, wbat_ref[cs, :])
            b = _dot_nt(ys, wbst_ref[cs, :])
            sa = _sigmoid(g_ref[:, cs].astype(F32))
            sb = _sigmoid(g_ref[:, gs].astype(F32))
            mg_ref[:, cs] = (sa * a + sb * b).astype(BF16)
            da_ref[:, cs] = (dm * sa).astype(BF16)
            db_ref[:, cs] = (dm * sb).astype(BF16)
            dg_ref[:, cs] = (dm * a * sa * (1.0 - sa)).astype(BF16)
            dg_ref[:, gs] = (dm * b * sb * (1.0 - sb)).astype(BF16)
        dya_ref[...] = _dot(da_ref[...], wbat_ref[...]).astype(BF16)
        dys_ref[...] = _dot(db_ref[...], wbst_ref[...]).astype(BF16)

    return pl.pallas_call(
        body, name="merge_bwd", grid=(S // tm,),
        out_shape=(jax.ShapeDtypeStruct((S, D), BF16), jax.ShapeDtypeStruct((S, D), BF16),
                   jax.ShapeDtypeStruct((S, D), BF16), jax.ShapeDtypeStruct((S, ATT_W), BF16),
                   jax.ShapeDtypeStruct((S, SG_W), BF16), jax.ShapeDtypeStruct((S, 2 * D), BF16)),
        in_specs=[_rows(tm, D), _rows(tm, ATT_W), _rows(tm, SG_W), _rows(tm, 2 * D),
                  _resident((D, ATT_W)), _resident((D, SG_W)), _resident((D, D))],
        out_specs=(_rows(tm, D), _rows(tm, D), _rows(tm, D), _rows(tm, ATT_W), _rows(tm, SG_W), _rows(tm, 2 * D)),
        compiler_params=_cparams(1),
    )(dx1, y_att, y_sg, gates, wba_t, wbs_t, w_out)


def _sgu_bwd(uv, dy_sg, lng, lnb, sg_w, b_exp, plans=(), tm=512):
    S = uv.shape[0]
    n_steps = S // tm

    def body(uv_ref, dy_ref, lng_ref, lnb_ref, w_ref, b_ref, duv_ref, dw_ref, dbx_ref, dlng_ref, dlnb_ref, dvln_ref):
        i = pl.program_id(0)

        @pl.when(i == 0)
        def _():
            dw_ref[...] = jnp.zeros_like(dw_ref)
            dbx_ref[...] = jnp.zeros_like(dbx_ref)
            dlng_ref[...] = jnp.zeros_like(dlng_ref)
            dlnb_ref[...] = jnp.zeros_like(dlnb_ref)

        lane = lax.broadcasted_iota(jnp.int32, (1, LANES), 1)
        lo = lane < 64
        mask = _sg_mask()
        lng = lng_ref[...]
        gu, dgelu_u = _gelu_and_grad(uv_ref[:, 0:SG_W].astype(F32))
        gv, dgelu_v = _gelu_and_grad(uv_ref[:, SG_W:2 * SG_W].astype(F32))
        vln, vhat, rstd = _sg_layernorm(gv, lng, lnb_ref[...])
        dy = dy_ref[...].astype(F32)
        dsv_all = dy * gu
        for gp in range(4):
            wf0 = jnp.where(mask, w_ref[2 * gp], 0)
            wf1 = jnp.where(mask, w_ref[2 * gp + 1], 0)
            w0 = wf0.astype(BF16)
            w1 = wf1.astype(BF16)
            cols = slice(128 * gp, 128 * (gp + 1))
            dw0 = jnp.zeros((128, 128), F32)
            dw1 = jnp.zeros((128, 128), F32)
            dbx = jnp.zeros((128, LANES), F32)
            for n in range(tm // 128):
                rows = slice(128 * n, 128 * (n + 1))
                vl = vln[rows, cols]
                vl0 = jnp.where(lo, vl, 0).astype(BF16)
                vl1 = jnp.where(lo, 0, vl).astype(BF16)
                sv = _dot(w0, vl0) + _dot(w1, vl1) + b_ref[:, cols]
                duv_ref[rows, cols] = (dy[rows, cols] * sv * dgelu_u[rows, cols]).astype(BF16)
                dsv = dsv_all[rows, cols]
                dbx = dbx + dsv
                ds0 = jnp.where(lo, dsv, 0).astype(BF16)
                ds1 = jnp.where(lo, 0, dsv).astype(BF16)
                dw0 = dw0 + _dot_nt(ds0, vl0)
                dw1 = dw1 + _dot_nt(ds1, vl1)
                dvln_ref[rows, cols] = _dot_tn(w0, ds0) + _dot_tn(w1, ds1)
            dw_ref[2 * gp] += jnp.where(mask, dw0, 0)
            dw_ref[2 * gp + 1] += jnp.where(mask, dw1, 0)
            dbx_ref[:, cols] += dbx
        dvln = dvln_ref[...]
        dlng_ref[...] += jnp.sum(dvln * vhat, axis=0, keepdims=True)
        dlnb_ref[...] += jnp.sum(dvln, axis=0, keepdims=True)
        dvh = dvln * lng
        dgv = rstd * (dvh - _group_sum64(dvh) * (1.0 / 64) - vhat * (_group_sum64(dvh * vhat) * (1.0 / 64)))
        duv_ref[:, SG_W:2 * SG_W] = (dgv * dgelu_v).astype(BF16)

        @pl.when(i == n_steps - 1)
        def _():
            dbx_ref[...] = _group_sum64(dbx_ref[...])

    return _call(
        body, name="sgu_bwd", grid=(n_steps,),
        out_shape=(jax.ShapeDtypeStruct((S, 2 * SG_W), BF16), jax.ShapeDtypeStruct((8, 128, 128), F32),
                   jax.ShapeDtypeStruct((128, SG_W), F32), jax.ShapeDtypeStruct((1, SG_W), F32),
                   jax.ShapeDtypeStruct((1, SG_W), F32)),
        in_specs=[_rows(tm, 2 * SG_W), _rows(tm, SG_W), _resident((1, SG_W)), _resident((1, SG_W)),
                  _resident((8, 128, 128)), _resident((128, SG_W))],
        out_specs=(_rows(tm, 2 * SG_W), pl.BlockSpec((8, 128, 128), lambda i: (0, 0, 0)),
                   pl.BlockSpec((128, SG_W), lambda i: (0, 0)), pl.BlockSpec((1, SG_W), lambda i: (0, 0)),
                   pl.BlockSpec((1, SG_W), lambda i: (0, 0))),
        scratch_shapes=[pltpu.VMEM((tm, SG_W), F32)],
        operands=(uv, dy_sg, lng, lnb, sg_w, b_exp), plans=plans)


def _attn_bwd(qkv, dy_att, y_att, lse, bias, plans=()):
    S = qkv.shape[0]
    R = ATT_R

    def body(q_ref, kp_ref, kc_ref, vp_ref, vc_ref, b_ref, dy_ref, y_ref, lse_ref, dq_ref, dk_ref, dv_ref, dss_ref,
             s_ref, dp_ref, pb_ref, dsb_ref):
        i = pl.program_id(1)

        @pl.when(i == 0)
        def _():
            dk_ref[...] = jnp.zeros_like(dk_ref)
            dv_ref[...] = jnp.zeros_like(dv_ref)
            dss_ref[...] = jnp.zeros_like(dss_ref)

        lo = lax.broadcasted_iota(jnp.int32, (1, LANES), 1) < 64
        kwin = jnp.concatenate([kp_ref[...], kc_ref[...]], axis=0)
        vwin = jnp.concatenate([vp_ref[...], vc_ref[...]], axis=0)
        for sub in range(ATT_SUBS):
            rows = slice(128 * sub, 128 * (sub + 1))
            kw = kwin[128 * sub:128 * sub + BAND_KEYS]
            vw = vwin[128 * sub:128 * sub + BAND_KEYS]
            table = jnp.where(i == 0, 1 + sub, 0)
            qs = _two_heads(q_ref[rows, :], lo)
            dos = _two_heads(dy_ref[rows, :], lo)
            dyy = dy_ref[rows, :].astype(F32) * y_ref[rows, :].astype(F32)
            delta = (jnp.sum(jnp.where(lo, dyy, 0.0), axis=-1, keepdims=True),
                     jnp.sum(jnp.where(lo, 0.0, dyy), axis=-1, keepdims=True))
            lse2 = lse_ref[rows, :]
            s_ref[...] = _dot_nt(qs, kw)
            dp_ref[...] = _dot_nt(dos, vw)
            for hh in range(2):
                lse = lse2[:, 64 * hh:64 * hh + 1]
                for r0 in range(0, 128, ATT_ROWS_BWD):
                    rr = slice(r0, r0 + ATT_ROWS_BWD)
                    both = slice(128 * hh + r0, 128 * hh + r0 + ATT_ROWS_BWD)
                    p = jnp.exp(s_ref[both, :] + b_ref[table, hh, rr, :] - lse[rr])
                    ds = p * (dp_ref[both, :] - delta[hh][rr])
                    dss_ref[hh, rr, :] += ds
                    pb_ref[both, :] = p.astype(BF16)
                    dsb_ref[both, :] = ds.astype(BF16)
            dq = _dot(jnp.concatenate([dsb_ref[0:128, :], dsb_ref[128:256, :]], axis=1), _two_heads(kw, lo))
            dq_ref[rows, :] = dq.astype(BF16)
            start = pl.multiple_of(i * R + 128 * sub, 128)
            dk_ref[pl.ds(start, BAND_KEYS), :] += _dot_tn(dsb_ref[...], qs)
            dv_ref[pl.ds(start, BAND_KEYS), :] += _dot_tn(pb_ref[...], dos)

    acc_spec = pl.BlockSpec((S + 8 * CHUNK, LANES), lambda j, i: (0, j))
    return _call(
        body, name="attn_bwd", grid=(4, S // R),
        out_shape=(jax.ShapeDtypeStruct((S, ATT_W), BF16), jax.ShapeDtypeStruct((S + 8 * CHUNK, ATT_W), F32),
                   jax.ShapeDtypeStruct((S + 8 * CHUNK, ATT_W), F32), jax.ShapeDtypeStruct((8, 128, BAND_KEYS), F32)),
        in_specs=_att_specs() + [pl.BlockSpec((R, LANES), lambda j, i: (i, j))] * 3,
        out_specs=(pl.BlockSpec((R, LANES), lambda j, i: (i, j)), acc_spec, acc_spec,
                   pl.BlockSpec((2, 128, BAND_KEYS), lambda j, i: (j, 0, 0))),
        scratch_shapes=[pltpu.VMEM((256, BAND_KEYS), F32), pltpu.VMEM((256, BAND_KEYS), F32),
                        pltpu.VMEM((256, BAND_KEYS), BF16), pltpu.VMEM((256, BAND_KEYS), BF16)],
        operands=(qkv, qkv, qkv, qkv, qkv, bias, dy_att, y_att, lse), plans=plans)


def _in_bwd(dq, dk, dv, duv, dgates, x, dx1, g1, w_in_t, plans=(), tm=512):
    S = x.shape[0]
    pad_blocks = (8 * CHUNK) // tm

    def body(dq_ref, dk_ref, dv_ref, duv_ref, dg_ref, x_ref, d1_ref, g_ref, wt_ref, dx_ref, dz_ref, dg1_ref):
        i = pl.program_id(0)
        dz_ref[:, 0:ATT_W] = (dq_ref[...].astype(F32) * 0.125).astype(BF16)
        dz_ref[:, ATT_W:2 * ATT_W] = dk_ref[...].astype(BF16)
        dz_ref[:, 2 * ATT_W:3 * ATT_W] = dv_ref[...].astype(BF16)
        dz_ref[:, 3 * ATT_W:3 * ATT_W + 2 * SG_W] = duv_ref[...]
        dz_ref[:, 3 * ATT_W + 2 * SG_W:IN_COLS] = dg_ref[...]
        dh = _dot(dz_ref[...], wt_ref[...])
        xv = x_ref[...]
        dx, dg1 = _rms_bwd(dh, xv, _rstd(xv), g_ref[...])
        dx_ref[...] = d1_ref[...] + dx

        @pl.when(i == 0)
        def _():
            dg1_ref[...] = jnp.zeros_like(dg1_ref)

        dg1_ref[...] += dg1

    shifted = pl.BlockSpec((tm, ATT_W), lambda i: (i + pad_blocks, 0))
    return _call(
        body, name="in_bwd", grid=(S // tm,),
        out_shape=(jax.ShapeDtypeStruct((S, D), F32), jax.ShapeDtypeStruct((S, IN_COLS), BF16),
                   jax.ShapeDtypeStruct((1, D), F32)),
        in_specs=[_rows(tm, ATT_W), shifted, shifted, _rows(tm, 2 * SG_W), _rows(tm, 2 * D), _rows(tm, D),
                  _rows(tm, D), _resident((1, D)), _resident((IN_COLS, D))],
        out_specs=(_rows(tm, D), _rows(tm, IN_COLS), pl.BlockSpec((1, D), lambda i: (0, 0))),
        operands=(dq, dk, dv, duv, dgates, x, dx1, g1, w_in_t), plans=plans)


def _adam_math(w, g, m, v):
    m = ADAM_B1 * m + (1.0 - ADAM_B1) * g
    v = ADAM_B2 * v + (1.0 - ADAM_B2) * (g * g)
    m_hat = m / (1.0 - ADAM_B1 ** ADAM_STEP)
    v_hat = v / (1.0 - ADAM_B2 ** ADAM_STEP)
    delta = -ADAM_LR * (m_hat / (jnp.sqrt(v_hat) + ADAM_EPS) + ADAM_WD * w)
    return delta, m, v


def _adam(parts, w, m, v, tr, name, transposed, after=None):
    P = parts.shape[0]
    R, C = w.shape

    def body(p_ref, w_ref, m_ref, v_ref, *rest):
        g_ref, d_ref, mo_ref, vo_ref = rest[-4:]
        if transposed:
            eye = (lax.broadcasted_iota(jnp.int32, (C, C), 0) == lax.broadcasted_iota(jnp.int32, (C, C), 1)).astype(BF16)
            part = lambda k: _dot_tn(p_ref[k], eye)
        else:
            part = lambda k: p_ref[k].astype(F32)
        g = part(0)
        for k in range(1, P):
            g = g + part(k)
        g_ref[...] = g
        d_ref[...], mo_ref[...], vo_ref[...] = _adam_math(w_ref[...], g, m_ref[...], v_ref[...])

    row = pl.BlockSpec((tr, C), lambda i: (i, 0))
    p_spec = pl.BlockSpec((P, C, tr), lambda i: (0, 0, i)) if transposed else pl.BlockSpec((P, tr, C), lambda i: (0, i, 0))
    extra = [] if after is None else [after]
    return pl.pallas_call(
        body, name=name, grid=(R // tr,),
        out_shape=tuple(jax.ShapeDtypeStruct((R, C), F32) for _ in range(4)),
        in_specs=[p_spec, row, row, row] + [pl.BlockSpec(memory_space=pl.ANY)] * len(extra),
        out_specs=(row, row, row, row),
        compiler_params=_cparams(1),
    )(parts, w, m, v, *extra)


def _my_place():
    return lax.axis_index("x"), lax.axis_index("y"), lax.axis_index("c")


def _other_chips(x, y):
    return [(1 - x, y), (x, 1 - y), (1 - x, 1 - y)]


class _Plan:
    def __init__(self, arrays, out_shapes, sems, start, finish, forward=None, forward_at=0.7):
        self.arrays, self.out_shapes, self.sems = list(arrays), list(out_shapes), list(sems)
        self.start, self.finish, self.forward, self.forward_at = start, finish, forward, forward_at


def _call(body, *, name, grid, in_specs, out_specs, out_shape, operands, scratch_shapes=(), plans=()):
    n_in, n_out, n_scr = len(operands), len(out_shape), len(scratch_shapes)
    p_in = [a for p in plans for a in p.arrays]
    p_out = [s for p in plans for s in p.out_shapes]
    p_sem = [s for p in plans for s in p.sems]
    steps = math.prod(grid)

    def wrapped(*refs):
        ins, refs = refs[:n_in], refs[n_in:]
        pins, refs = refs[:len(p_in)], refs[len(p_in):]
        outs, refs = refs[:n_out], refs[n_out:]
        pouts, refs = refs[:len(p_out)], refs[len(p_out):]
        scr, psems = refs[:n_scr], refs[n_scr:]
        step = 0
        for axis, size in enumerate(grid):
            step = step * size + pl.program_id(axis)
        bound = []
        for p in plans:
            bound.append((p, pins[:len(p.arrays)], pouts[:len(p.out_shapes)], psems[:len(p.sems)]))
            pins, pouts, psems = pins[len(p.arrays):], pouts[len(p.out_shapes):], psems[len(p.sems):]

        @pl.when(step == 0)
        def _():
            for p, a, b, s in bound:
                p.start(a, b, s)

        body(*ins, *outs, *scr)

        for p, a, b, s in bound:
            if p.forward is not None:
                @pl.when(step == min(int(p.forward_at * steps), steps - 1))
                def _(p=p, a=a, b=b, s=s):
                    p.forward(a, b, s)

        @pl.when(step == steps - 1)
        def _():
            for p, a, b, s in bound:
                p.finish(a, b, s)

    hbm = pl.BlockSpec(memory_space=pl.ANY)
    return pl.pallas_call(
        wrapped if plans else body, name=name, grid=grid,
        out_shape=tuple(out_shape) + tuple(p_out),
        in_specs=list(in_specs) + [hbm] * len(p_in),
        out_specs=tuple(out_specs) + tuple([hbm] * len(p_out)),
        scratch_shapes=list(scratch_shapes) + p_sem,
        compiler_params=_cparams(len(grid)),
    )(*operands, *p_in)


def _run_plan(plan, name):
    n_in, n_out = len(plan.arrays), len(plan.out_shapes)

    def body(*refs):
        a, b, s = refs[:n_in], refs[n_in:n_in + n_out], refs[n_in + n_out:]
        plan.start(a, b, s)
        if plan.forward is not None:
            plan.forward(a, b, s)
        plan.finish(a, b, s)

    hbm = pl.BlockSpec(memory_space=pl.ANY)
    return pl.pallas_call(
        body, name=name, out_shape=tuple(plan.out_shapes),
        in_specs=[hbm] * n_in, out_specs=tuple([hbm] * n_out), scratch_shapes=plan.sems,
    )(*plan.arrays)


def _gather_plan(shards, forward_at=0.7):
    n = len(shards)

    def copies(ins, outs, sems):
        send_sems, recv_sems, local_sems = sems
        x, y, c = _my_place()
        me, sibling = (x, y, c), (x, y, 1 - c)
        chips = _other_chips(x, y)

        def idx(p):
            return 4 * p[0] + 2 * p[1] + p[2]

        def copy(a, k, block, to, src=None):
            return pltpu.make_async_remote_copy(
                src_ref=outs[a].at[idx(block)] if src is None else src, dst_ref=outs[a].at[idx(block)],
                send_sem=send_sems.at[a, k], recv_sem=recv_sems.at[a, k], device_id=to, device_id_type=MESH)

        mine = [pltpu.make_async_copy(ins[a], outs[a].at[idx(me)], local_sems.at[a]) for a in range(n)]
        first = []
        for a in range(n):
            first.append(copy(a, 0, me, sibling, src=ins[a]))
            first += [copy(a, 1 + j, me, (*chip, c), src=ins[a]) for j, chip in enumerate(chips)]
        arrived = [copy(a, 1 + j, (*chip, c), me) for j, chip in enumerate(chips) for a in range(n)]
        passed = [copy(a, 4 + j, (*chip, c), sibling) for j, chip in enumerate(chips) for a in range(n)]
        from_sibling = []
        for a in range(n):
            from_sibling.append(copy(a, 0, sibling, me))
            from_sibling += [copy(a, 4 + j, (*chip, 1 - c), me) for j, chip in enumerate(chips)]
        return mine, first, arrived, passed, from_sibling

    def start(ins, outs, sems):
        mine, first, _, _, _ = copies(ins, outs, sems)
        for cp in mine + first:
            cp.start()

    def forward(ins, outs, sems):
        _, _, arrived, passed, _ = copies(ins, outs, sems)
        for landed, onward in zip(arrived, passed):
            landed.wait_recv()
            onward.start()

    def finish(ins, outs, sems):
        mine, first, _, passed, from_sibling = copies(ins, outs, sems)
        for cp in from_sibling:
            cp.wait_recv()
        for cp in first + passed:
            cp.wait_send()
        for cp in mine:
            cp.wait()

    return _Plan(shards, [jax.ShapeDtypeStruct((N_DEV,) + s.shape, s.dtype) for s in shards],
                 [pltpu.SemaphoreType.DMA((n, 7)), pltpu.SemaphoreType.DMA((n, 7)), pltpu.SemaphoreType.DMA((n,))],
                 start, finish, forward, forward_at)


def _sibling_plan(scatter, whole=()):
    ns = len(scatter)
    n = ns + len(whole)

    def copies(ins, outs, sems):
        send_sems, recv_sems = sems
        x, y, c = _my_place()
        out = []
        for a in range(n):
            for k in range(4 if a < ns else 1):
                src = ins[a].at[2 * k + (1 - c)] if a < ns else ins[a]
                dst = outs[a].at[k] if a < ns else outs[a]
                out.append(pltpu.make_async_remote_copy(
                    src_ref=src, dst_ref=dst, send_sem=send_sems.at[a, k], recv_sem=recv_sems.at[a, k],
                    device_id=(x, y, 1 - c), device_id_type=MESH))
        return out

    def start(ins, outs, sems):
        for cp in copies(ins, outs, sems):
            cp.start()

    def finish(ins, outs, sems):
        for cp in copies(ins, outs, sems):
            cp.wait()

    shapes = [jax.ShapeDtypeStruct((4,) + p.shape[1:], p.dtype) for p in scatter]
    shapes += [jax.ShapeDtypeStruct(p.shape, p.dtype) for p in whole]
    return _Plan(list(scatter) + list(whole), shapes,
                 [pltpu.SemaphoreType.DMA((n, 4)), pltpu.SemaphoreType.DMA((n, 4))], start, finish)


def _pair_sums(mine, theirs, c_idx, name):
    n = len(mine)

    def body(c_ref, *refs):
        for a in range(n):
            out = refs[2 * n + a]
            out[...] = (refs[a][...].astype(F32) + refs[n + a][...].astype(F32)).astype(out.dtype)

    def block(arr):
        return (1,) + arr.shape[1:]

    grid_spec = pltpu.PrefetchScalarGridSpec(
        num_scalar_prefetch=1, grid=(4,),
        in_specs=[pl.BlockSpec(block(m), lambda k, c_ref: (2 * k + c_ref[0], 0, 0)) for m in mine]
        + [pl.BlockSpec(block(t), lambda k, c_ref: (k, 0, 0)) for t in theirs],
        out_specs=tuple(pl.BlockSpec(block(t), lambda k, c_ref: (k, 0, 0)) for t in theirs))
    return pl.pallas_call(
        body, name=name, grid_spec=grid_spec,
        out_shape=tuple(jax.ShapeDtypeStruct(t.shape, m.dtype) for m, t in zip(mine, theirs)),
        compiler_params=_cparams(1),
    )(c_idx, *mine, *theirs)


def _peers_plan(arrays):
    n = len(arrays)

    def copies(ins, outs, sems):
        send_sems, recv_sems, local_sems = sems
        x, y, c = _my_place()
        me = 4 * x + 2 * y + c
        out = [pltpu.make_async_copy(ins[a], outs[a].at[me], local_sems.at[a]) for a in range(n)]
        for a in range(n):
            for k in range(N_DEV - 1):
                bits = k + 1
                peer = (x ^ (bits >> 2), y ^ ((bits >> 1) & 1), c ^ (bits & 1))
                out.append(pltpu.make_async_remote_copy(
                    src_ref=ins[a], dst_ref=outs[a].at[me], send_sem=send_sems.at[a, k], recv_sem=recv_sems.at[a, k],
                    device_id=peer, device_id_type=MESH))
        return out

    def start(ins, outs, sems):
        for cp in copies(ins, outs, sems):
            cp.start()

    def finish(ins, outs, sems):
        for cp in copies(ins, outs, sems):
            cp.wait()

    return _Plan(list(arrays), [jax.ShapeDtypeStruct((N_DEV,) + a.shape, a.dtype) for a in arrays],
                 [pltpu.SemaphoreType.DMA((n, N_DEV - 1)), pltpu.SemaphoreType.DMA((n, N_DEV - 1)),
                  pltpu.SemaphoreType.DMA((n,))], start, finish)


def _chips_plan(scatter, whole=()):
    ns = len(scatter)
    n = ns + len(whole)

    def copies(ins, outs, sems):
        send_sems, recv_sems, local_sems = sems
        x, y, c = _my_place()
        my_chip = 2 * x + y

        def src(a, k):
            return ins[a].at[k] if a < ns else ins[a]

        local = [pltpu.make_async_copy(src(a, my_chip), outs[a].at[my_chip], local_sems.at[a]) for a in range(n)]
        remote = []
        for a in range(n):
            for j, (px, py) in enumerate(_other_chips(x, y)):
                remote.append(pltpu.make_async_remote_copy(
                    src_ref=src(a, 2 * px + py), dst_ref=outs[a].at[my_chip],
                    send_sem=send_sems.at[a, j], recv_sem=recv_sems.at[a, j],
                    device_id=(px, py, c), device_id_type=MESH))
        return local + remote

    def start(ins, outs, sems):
        for cp in copies(ins, outs, sems):
            cp.start()

    def finish(ins, outs, sems):
        for cp in copies(ins, outs, sems):
            cp.wait()

    shapes = [jax.ShapeDtypeStruct(s.shape, s.dtype) for s in scatter]
    shapes += [jax.ShapeDtypeStruct((4,) + s.shape, s.dtype) for s in whole]
    return _Plan(list(scatter) + list(whole), shapes,
                 [pltpu.SemaphoreType.DMA((n, 3)), pltpu.SemaphoreType.DMA((n, 3)), pltpu.SemaphoreType.DMA((n,))],
                 start, finish)


def _chip_copies(src_ref, land_ref, send_sems, recv_sems):
    x, y, c = _my_place()
    return [pltpu.make_async_remote_copy(
        src_ref=src_ref.at[2 * px + py], dst_ref=land_ref.at[2 * x + y], send_sem=send_sems.at[j],
        recv_sem=recv_sems.at[j], device_id=(px, py, c), device_id_type=MESH)
        for j, (px, py) in enumerate(_other_chips(x, y))]


def _chips_start(src, name):
    def body(src_ref, land_ref, send_sems, recv_sems, src_thru, land_thru, token):
        for cp in _chip_copies(src_ref, land_ref, send_sems, recv_sems):
            cp.start()
        token[...] = jnp.zeros_like(token)

    hbm = pl.BlockSpec(memory_space=pltpu.HBM)
    sem = pl.BlockSpec(memory_space=pltpu.SEMAPHORE)
    return pl.pallas_call(
        body, name=name,
        out_shape=(pltpu.SemaphoreType.DMA((3,)), pltpu.SemaphoreType.DMA((3,)), pltpu.HBM(src.shape, src.dtype),
                   pltpu.HBM(src.shape, src.dtype), jax.ShapeDtypeStruct((8, LANES), F32)),
        in_specs=(hbm, hbm), out_specs=(sem, sem, hbm, hbm, pl.BlockSpec(memory_space=pltpu.VMEM)),
        input_output_aliases={0: 2, 1: 3},
        compiler_params=pltpu.CompilerParams(has_side_effects=pltpu.SideEffectType.DATAFLOW_SIDE_EFFECTING),
    )(pltpu.with_memory_space_constraint(src, pltpu.HBM),
      pltpu.with_memory_space_constraint(jnp.zeros(src.shape, src.dtype), pltpu.HBM))


def _chips_wait(send_sems, recv_sems, src_thru, land_thru, after, name):
    def body(src_ref, land_ref, send_sems, recv_sems, after_ref, src_dead, got_ref):
        for cp in _chip_copies(src_ref, land_ref, send_sems, recv_sems):
            cp.wait_send()
            cp.wait_recv()

    hbm = pl.BlockSpec(memory_space=pltpu.HBM)
    sem = pl.BlockSpec(memory_space=pltpu.SEMAPHORE)
    return pl.pallas_call(
        body, name=name,
        out_shape=(pltpu.HBM(src_thru.shape, src_thru.dtype), pltpu.HBM(land_thru.shape, land_thru.dtype)),
        in_specs=(hbm, hbm, sem, sem, pl.BlockSpec(memory_space=pl.ANY)), out_specs=(hbm, hbm),
        input_output_aliases={0: 0, 1: 1},
        compiler_params=pltpu.CompilerParams(has_side_effects=pltpu.SideEffectType.DATAFLOW_SIDE_EFFECTING),
    )(src_thru, land_thru, send_sems, recv_sems, after)


def _small_update(parts, w, m, v, loss_parts, name):
    n = len(parts)

    def total(ref):
        acc = ref[0]
        for k in range(1, ref.shape[0]):
            acc = acc + ref[k]
        return acc

    def body(*refs):
        p_refs, w_refs, m_refs, v_refs = (refs[i * n:(i + 1) * n] for i in range(4))
        lp_ref = refs[4 * n]
        outs = refs[4 * n + 1:]
        g_refs, d_refs, mo_refs, vo_refs = (outs[i * n:(i + 1) * n] for i in range(4))
        for a in range(n):
            g = total(p_refs[a])
            g_refs[a][...] = g
            d_refs[a][...], mo_refs[a][...], vo_refs[a][...] = _adam_math(w_refs[a][...], g, m_refs[a][...],
                                                                          v_refs[a][...])
        outs[4 * n][...] = total(lp_ref)

    vm = pl.BlockSpec(memory_space=pltpu.VMEM)
    shapes = [jax.ShapeDtypeStruct(t.shape, F32) for _ in range(4) for t in w]
    shapes.append(jax.ShapeDtypeStruct(loss_parts.shape[1:], F32))
    outs = pl.pallas_call(
        body, name=name, out_shape=tuple(shapes),
        in_specs=[vm] * (4 * n + 1), out_specs=tuple([vm] * (4 * n + 1)),
        compiler_params=pltpu.CompilerParams(vmem_limit_bytes=VMEM_LIMIT),
    )(*parts, *w, *m, *v, loss_parts)
    return outs[0:n], outs[n:2 * n], outs[2 * n:3 * n], outs[3 * n:4 * n], outs[4 * n]


BIG = [("w_in", 1), ("w_branch_att", 1), ("w_branch_sg", 1), ("w_out", 0), ("w_xq", 0), ("w_xkv", 1), ("w_xo", 0),
       ("w_ffn_in", 1), ("w_ffn_out", 0)]
SMALL = [("norm_mix_g", (1, D)), ("rel_bias", (8, NREL)), ("sg_ln_g", (8, 64)), ("sg_ln_b", (8, 64)),
         ("sg_w", (8, 128, 128)), ("sg_b", (8, 128)), ("norm_xattn_g", (1, D)), ("norm_mem_g", (1, D)),
         ("norm_ffn_g", (1, D)), ("norm_final_g", (1, D))]
ADAM_ROWS = {"w_in": 192, "w_branch_att": 512, "w_branch_sg": 512, "w_xkv": 1024, "w_ffn_in": 176}


def _full(gathered):
    return gathered.reshape(N_DEV * gathered.shape[1], gathered.shape[2])


def _blocks(grad):
    return grad.reshape(N_DEV, grad.shape[0] // N_DEV, grad.shape[1])


def kernel(x, mem, norm_mix_g, w_in, rel_bias, sg_ln_g, sg_ln_b, sg_w, sg_b, w_branch_att, w_branch_sg, w_out, norm_xattn_g, norm_mem_g, w_xq, w_xkv, w_xo, norm_ffn_g, w_ffn_in, w_ffn_out, norm_final_g, loss_target, m_norm_mix_g, m_w_in, m_rel_bias, m_sg_ln_g, m_sg_ln_b, m_sg_w, m_sg_b, m_w_branch_att, m_w_branch_sg, m_w_out, m_norm_xattn_g, m_norm_mem_g, m_w_xq, m_w_xkv, m_w_xo, m_norm_ffn_g, m_w_ffn_in, m_w_ffn_out, m_norm_final_g, v_norm_mix_g, v_w_in, v_rel_bias, v_sg_ln_g, v_sg_ln_b, v_sg_w, v_sg_b, v_w_branch_att, v_w_branch_sg, v_w_out, v_norm_xattn_g, v_norm_mem_g, v_w_xq, v_w_xkv, v_w_xo, v_norm_ffn_g, v_w_ffn_in, v_w_ffn_out, v_norm_final_g):
    args = dict(locals())
    big_names = [n for n, _ in BIG]
    small_names = [n for n, _ in SMALL]
    S = x.shape[1]

    x, mem, tgt = x.reshape(S, D), mem.reshape(MEM, D), loss_target.reshape(S, D)
    small = {n: args[n].reshape(shape) for n, shape in SMALL}
    g1, g2, g3 = small["norm_mix_g"], small["norm_xattn_g"], small["norm_ffn_g"]
    g_mem, g4 = small["norm_mem_g"], small["norm_final_g"]
    lng = small["sg_ln_g"].reshape(1, SG_W)
    lnb = small["sg_ln_b"].reshape(1, SG_W)
    b_exp = jnp.broadcast_to(small["sg_b"].T[:, :, None], (128, 8, 64)).reshape(128, SG_W)
    rel_pad = jnp.pad(small["rel_bias"], ((0, 0), (0, 384 - NREL)))
    c_idx = lax.axis_index("c").astype(jnp.int32).reshape(1)

    shard = {n: (args[n][0].T if axis == 1 else args[n][0]).astype(BF16) for n, axis in BIG}
    h, w_in_gathered = _norm_in(x, g1, plans=[_gather_plan([shard["w_in"]])])
    w_in_t = _full(w_in_gathered)
    bias = _bias_table(rel_pad)
    mix_names = ["w_branch_att", "w_branch_sg", "w_out", "w_xq", "w_xkv", "w_xo"]
    qkv, uv, gates, *got = _in_proj(h, w_in_t, plans=[_gather_plan([shard[n] for n in mix_names])])
    wba_t, wbs_t, w_out_f, w_xq_f, w_xkv_t, w_xo_f = (_full(g) for g in got)
    y_att, lse, *got = _attn_fwd(qkv, bias, plans=[_gather_plan([shard["w_ffn_in"], shard["w_ffn_out"]])])
    w_ffn_in_t, w_ffn_out_f = (_full(g) for g in got)
    y_sg = _sgu_fwd(uv, lng, lnb, small["sg_w"], b_exp)
    x1 = _merge_fwd(x, y_att, y_sg, gates, wba_t, wbs_t, w_out_f)
    kv, mn = _mem_kv(mem, g_mem, w_xkv_t)
    x2, hx, qx, o_x = _xattn_fwd(x1, g2, w_xq_f, kv, w_xo_f)
    dx3, gu, hf, act, loss_part, dg4 = _ffn_fwd(x2, tgt, g3, w_ffn_in_t, w_ffn_out_f, g4)


    dx2, dgu, dg3 = _ffn_bwd(dx3, gu, x2, g3, w_ffn_out_f, w_ffn_in_t)
    ffn_names = ["w_ffn_out", "w_ffn_in"]
    ffn_mine = [_blocks(_dw(act, dx3, 1408, 1024, DW_TOKENS, "dw_ffn_out")),
                _blocks(_dw(dgu, hf, 1408, 1024, DW_TOKENS, "dw_ffn_in"))]
    dx1, dq_x, dkv, dg2, *ffn_theirs = _xattn_bwd(dx2, x1, qx, g2, w_xq_f, w_xo_f, kv,
                                                  plans=[_sibling_plan(ffn_mine)])
    ffn_chip = _pair_sums(ffn_mine, ffn_theirs, c_idx, "rs_pair_ffn")
    d_xkv, dg_mem = _mem_kv_bwd(dkv, mem, g_mem, mn, w_xkv_t)
    merged, d_a, d_b, dy_att, dy_sg, dgates = _merge_bwd(dx1, y_att, y_sg, gates, wba_t, wbs_t, w_out_f)
    mid_names = ["w_xo", "w_xq", "w_xkv", "w_out", "w_branch_att", "w_branch_sg"]
    mid_mine = [_blocks(g) for g in (
        _dw(o_x, dx2, 1024, 1024, DW_TOKENS, "dw_xo"), _dw(hx, dq_x, 1024, 1024, DW_TOKENS, "dw_xq"), d_xkv,
        _dw(merged, dx1, 1024, 1024, DW_TOKENS, "dw_out"), _dw(d_a, y_att, 1024, 512, DW_TOKENS, "dw_branch_att"),
        _dw(d_b, y_sg, 1024, 512, DW_TOKENS, "dw_branch_sg"))]
    duv, d_sgw, d_bx, d_lng, d_lnb, *got = _sgu_bwd(uv, dy_sg, lng, lnb, small["sg_w"], b_exp,
                                                    plans=[_chips_plan(ffn_chip), _sibling_plan(mid_mine)])
    ffn_all, mid_theirs = got[:2], got[2:]
    mid_chip = _pair_sums(mid_mine, mid_theirs, c_idx, "rs_pair_mid")
    dq, dk, dv, ds_sum, *mid_all = _attn_bwd(qkv, dy_att, y_att, lse, bias, plans=[_chips_plan(mid_chip)])
    d_rel = _bias_grad(ds_sum)

    grad_x, dz, dg1 = _in_bwd(dq, dk, dv, duv, dgates, x, dx1, g1, w_in_t)

    gs = {"norm_mix_g": dg1, "rel_bias": d_rel[:, :NREL], "sg_ln_g": d_lng.reshape(8, 64),
          "sg_ln_b": d_lnb.reshape(8, 64), "sg_w": d_sgw, "sg_b": d_bx.reshape(128, 8, 64)[:, :, 0].T,
          "norm_xattn_g": dg2, "norm_mem_g": dg_mem, "norm_ffn_g": dg3, "norm_final_g": dg4}
    d_in, *everyone = _dw(dz, h, 1152, 1024, DW_TOKENS, "dw_in",
                          plans=[_peers_plan([gs[n] for n in small_names] + [loss_part])])

    in_mine = [_blocks(d_in)]
    (in_theirs,) = _run_plan(_sibling_plan(in_mine), "rs_sibling")
    (in_chip,) = _pair_sums(in_mine, [in_theirs], c_idx, "rs_pair_w_in")
    send_sems, recv_sems, in_chip_thru, landing, token = _chips_start(in_chip, "rs_chips_start")
    all_parts = dict(zip(ffn_names + mid_names, list(ffn_all) + list(mid_all)))

    def adam(n, axis, parts, after):
        wmv = [args[p + n][0] for p in ("", "m_", "v_")]
        if axis == 1 and wmv[0].shape[1] % LANES != 0:
            outs = _adam(parts, *(t.T for t in wmv), ADAM_ROWS[n], "adam_" + n, transposed=False, after=after)
            return [t.T[None] for t in outs]
        tr = ADAM_ROWS[n] if axis == 1 else wmv[0].shape[0]
        return [t[None] for t in _adam(parts, *wmv, tr, "adam_" + n, transposed=(axis == 1), after=after)]

    res = {n: adam(n, axis, all_parts[n], token) for n, axis in BIG if n != "w_in"}
    in_chip, landing = _chips_wait(send_sems, recv_sems, in_chip_thru, landing, res["w_ffn_in"][1], "rs_chips_wait")
    my_chip = 2 * lax.axis_index("x") + lax.axis_index("y")
    own = lax.dynamic_slice_in_dim(in_chip, my_chip, 1, axis=0)
    res["w_in"] = adam("w_in", 1, lax.dynamic_update_slice_in_dim(landing, own, my_chip, axis=0), None)
    small_res = _small_update(
        everyone[:-1], [small[n] for n in small_names],
        [args["m_" + n].reshape(s) for n, s in SMALL], [args["v_" + n].reshape(s) for n, s in SMALL],
        everyone[-1], "adam_small")
    for i, n in enumerate(small_names):
        res[n] = [small_res[k][i].reshape(args[n].shape) for k in range(4)]
    loss = small_res[4][0, 0]

    order = ["norm_mix_g", "w_in", "rel_bias", "sg_ln_g", "sg_ln_b", "sg_w", "sg_b", "w_branch_att", "w_branch_sg",
             "w_out", "norm_xattn_g", "norm_mem_g", "w_xq", "w_xkv", "w_xo", "norm_ffn_g", "w_ffn_in", "w_ffn_out",
             "norm_final_g"]
    outs = [loss, grad_x.reshape(1, S, D)]
    for k in range(4):
        outs += [res[n][k] for n in order]
    return tuple(outs)
```

```python
import math

import jax
import jax.numpy as jnp
from jax import lax
from jax.experimental import pallas as pl
from jax.experimental.pallas import tpu as pltpu

F32 = jnp.float32
BF16 = jnp.bfloat16

D = 1024
ATT_W = 512
SG_W = 512
IN_COLS = 4608
DFF = 2816
MEM = 256
XH = 4
CHUNK = 64
BAND_KEYS = 640
ATT_R = 1024
ATT_SUBS = ATT_R // 128
START_SUBS = 8 * CHUNK // 128
ATT_ROWS = 32
ATT_ROWS_BWD = 16
DW_TOKENS = 2048
IN_CHUNK = 512
FF_CHUNK = 256
REL_CLIP = 128
NREL = 2 * REL_CLIP + 1
EPS = 1e-6
NEG = -1e30
N_DEV = 8

ADAM_LR = 0.001
ADAM_B1 = 0.9
ADAM_B2 = 0.999
ADAM_EPS = 1e-08
ADAM_WD = 0.01
ADAM_STEP = 10

LANES = 128
VMEM_LIMIT = 56 * 1024 * 1024
MESH = pl.DeviceIdType.MESH


def _cparams(n_axes):
    return pltpu.CompilerParams(dimension_semantics=("arbitrary",) * n_axes, vmem_limit_bytes=VMEM_LIMIT)


def _resident(shape):
    zeros = (0,) * len(shape)
    return pl.BlockSpec(shape, lambda *_: zeros, pipeline_mode=pl.Buffered(1))


def _rows(tm, cols, col_block=0):
    return pl.BlockSpec((tm, cols), lambda i: (i, col_block))


def _sigmoid(x):
    return pl.reciprocal(1.0 + jnp.exp(-x), approx=True)


_GELU_C = math.sqrt(2.0 / math.pi)


def _gelu(x):
    t = jnp.tanh(_GELU_C * (x + 0.044715 * (x * x * x)))
    return x * (0.5 * (1.0 + t))


def _gelu_and_grad(x):
    x2 = x * x
    t = jnp.tanh(_GELU_C * (x + 0.044715 * (x2 * x)))
    cdf = 0.5 * (1.0 + t)
    dcdf = 0.5 * (1.0 - t * t) * (_GELU_C * (1.0 + 3.0 * 0.044715 * x2))
    return x * cdf, cdf + x * dcdf


def _rstd(x):
    return lax.rsqrt(jnp.mean(x * x, axis=-1, keepdims=True) + EPS)


def _rms_bwd(dh, x, r, g):
    xh = x * r
    dxh = dh * g
    dx = r * (dxh - xh * jnp.mean(dxh * xh, axis=-1, keepdims=True))
    dg = jnp.sum(dh * xh, axis=0, keepdims=True)
    return dx, dg


def _group_sum64(x):
    r = lax.broadcasted_iota(jnp.int32, (LANES, LANES), 0) // 64
    c = lax.broadcasted_iota(jnp.int32, (LANES, LANES), 1) // 64
    same_group = (r == c).astype(BF16)

    def one(v):
        hi = v.astype(BF16)
        rest = v - hi.astype(F32)
        mid = rest.astype(BF16)
        lo = (rest - mid.astype(F32)).astype(BF16)
        return _dot(hi, same_group) + _dot(mid, same_group) + _dot(lo, same_group)

    pieces = [one(x[:, LANES * j:LANES * (j + 1)]) for j in range(x.shape[1] // LANES)]
    return pieces[0] if len(pieces) == 1 else jnp.concatenate(pieces, axis=1)


def _dot(a, b):
    return jnp.dot(a, b, preferred_element_type=F32)


def _dot_nt(a, b):
    return lax.dot_general(a, b, (((1,), (1,)), ((), ())), preferred_element_type=F32)


def _dot_tn(a, b):
    return lax.dot_general(a, b, (((0,), (0,)), ((), ())), preferred_element_type=F32)


DIAGS = 768


def _diag_onehot():
    r_idx = lax.broadcasted_iota(jnp.int32, (384, DIAGS), 0)
    t_idx = lax.broadcasted_iota(jnp.int32, (384, DIAGS), 1)
    dist = (8 * CHUNK + 127) - t_idx
    return (jnp.clip(dist, -REL_CLIP, REL_CLIP) + REL_CLIP == r_idx).astype(F32)


def _shift_rows(x, reverse):
    row = lax.broadcasted_iota(jnp.int32, x.shape, 0)
    for k in range(7):
        amt = (DIAGS - (1 << k)) if reverse else (1 << k)
        x = jnp.where(((row >> k) & 1) == 1, pltpu.roll(x, amt, 1), x)
    return x


N_TABLES = 1 + START_SUBS


def _bias_table(rel_bias_pad):
    def body(rb_ref, out_ref):
        per_diag = jnp.dot(rb_ref[...], _diag_onehot(), preferred_element_type=F32,
                           precision=lax.Precision.HIGHEST)
        a = lax.broadcasted_iota(jnp.int32, (128, BAND_KEYS), 0)
        b = lax.broadcasted_iota(jnp.int32, (128, BAND_KEYS), 1)
        band = (b // CHUNK >= a // CHUNK) & (b // CHUNK <= a // CHUNK + 8)
        for h in range(8):
            rows = jnp.broadcast_to(per_diag[h:h + 1, :], (128, DIAGS))
            table = _shift_rows(pltpu.roll(rows, DIAGS - 127, 1), reverse=False)[:, :BAND_KEYS]
            out_ref[0, h] = jnp.where(band, table, NEG)
            for s in range(START_SUBS):
                out_ref[1 + s, h] = jnp.where(band & (b >= 8 * CHUNK - 128 * s), table, NEG)

    return pl.pallas_call(
        body, name="bias_table",
        out_shape=jax.ShapeDtypeStruct((N_TABLES, 8, 128, BAND_KEYS), F32),
        in_specs=[pl.BlockSpec(memory_space=pltpu.VMEM)],
        out_specs=pl.BlockSpec(memory_space=pltpu.VMEM),
        compiler_params=pltpu.CompilerParams(vmem_limit_bytes=VMEM_LIMIT),
    )(rel_bias_pad)


def _bias_grad(ds_sum):
    def body(ds_ref, out_ref):
        sums = []
        for h in range(8):
            padded = jnp.concatenate([ds_ref[h], jnp.zeros((128, DIAGS - BAND_KEYS), F32)], axis=1)
            skewed = pltpu.roll(_shift_rows(padded, reverse=True), 127, 1)
            sums.append(jnp.sum(skewed, axis=0, keepdims=True))
        per_diag = jnp.concatenate(sums, axis=0)
        out_ref[...] = lax.dot_general(per_diag, _diag_onehot(), (((1,), (1,)), ((), ())),
                                       preferred_element_type=F32, precision=lax.Precision.HIGHEST)

    return pl.pallas_call(
        body, name="bias_grad",
        out_shape=jax.ShapeDtypeStruct((8, 384), F32),
        in_specs=[pl.BlockSpec(memory_space=pltpu.VMEM)],
        out_specs=pl.BlockSpec(memory_space=pltpu.VMEM),
    )(ds_sum)


def _norm_in(x, g1, plans=(), tm=512):
    S = x.shape[0]

    def body(x_ref, g_ref, h_ref):
        xv = x_ref[...]
        h_ref[...] = (xv * _rstd(xv) * g_ref[...]).astype(BF16)

    return _call(
        body, name="norm_in", grid=(S // tm,),
        out_shape=(jax.ShapeDtypeStruct((S, D), BF16),),
        in_specs=[_rows(tm, D), _resident((1, D))], out_specs=(_rows(tm, D),),
        operands=(x, g1), plans=plans)


def _in_proj(h, w_in_t, plans=(), tm=512):
    S = h.shape[0]

    def body(h_ref, w_ref, qkv_ref, uv_ref, gate_ref):
        h = h_ref[...]
        w = IN_CHUNK
        for c in range(IN_COLS // w):
            zc = _dot_nt(h, w_ref[w * c:w * (c + 1), :])
            start = w * c
            if start < ATT_W:
                qkv_ref[:, start:start + w] = (zc * 0.125).astype(BF16)
            elif start < 3 * ATT_W:
                qkv_ref[:, start:start + w] = zc.astype(BF16)
            elif start < 3 * ATT_W + 2 * SG_W:
                uv_ref[:, start - 3 * ATT_W:start - 3 * ATT_W + w] = zc.astype(BF16)
            else:
                gate_ref[:, start - 3 * ATT_W - 2 * SG_W:start - 3 * ATT_W - 2 * SG_W + w] = zc.astype(BF16)

    return _call(
        body, name="in_proj", grid=(S // tm,),
        out_shape=(jax.ShapeDtypeStruct((S, 3 * ATT_W), BF16), jax.ShapeDtypeStruct((S, 2 * SG_W), BF16),
                   jax.ShapeDtypeStruct((S, 2 * D), BF16)),
        in_specs=[_rows(tm, D), _resident((IN_COLS, D))],
        out_specs=(_rows(tm, 3 * ATT_W), _rows(tm, 2 * SG_W), _rows(tm, 2 * D)),
        operands=(h, w_in_t), plans=plans)


def _two_heads(a, lo):
    return jnp.concatenate([jnp.where(lo, a, 0), jnp.where(lo, 0, a)], axis=0)


def _att_specs():
    R = ATT_R
    q = pl.BlockSpec((R, LANES), lambda j, i: (i, j))
    kp = pl.BlockSpec((R, LANES), lambda j, i: (jnp.maximum(i - 1, 0), 4 + j))
    kc = pl.BlockSpec((R, LANES), lambda j, i: (i, 4 + j))
    vp = pl.BlockSpec((R, LANES), lambda j, i: (jnp.maximum(i - 1, 0), 8 + j))
    vc = pl.BlockSpec((R, LANES), lambda j, i: (i, 8 + j))
    bias = pl.BlockSpec((N_TABLES, 2, 128, BAND_KEYS), lambda j, i: (0, j, 0, 0))
    return [q, kp, kc, vp, vc, bias]


def _attn_fwd(qkv, bias, plans=()):
    S = qkv.shape[0]
    R = ATT_R

    def body(q_ref, kp_ref, kc_ref, vp_ref, vc_ref, b_ref, o_ref, lse_ref, s_ref, p_ref, l_ref):
        i = pl.program_id(1)
        lo = lax.broadcasted_iota(jnp.int32, (1, LANES), 1) < 64
        kwin = jnp.concatenate([kp_ref[R - 8 * CHUNK:R, :], kc_ref[...]], axis=0)
        vwin = jnp.concatenate([vp_ref[R - 8 * CHUNK:R, :], vc_ref[...]], axis=0)
        for sub in range(ATT_SUBS):
            q2 = q_ref[128 * sub:128 * (sub + 1), :]
            kw = kwin[128 * sub:128 * sub + BAND_KEYS]
            vw = vwin[128 * sub:128 * sub + BAND_KEYS]
            table = jnp.where(i == 0, 1 + sub, 0) if sub < START_SUBS else 0
            s_ref[...] = _dot_nt(_two_heads(q2, lo), kw)
            for hh in range(2):
                for r0 in range(0, 128, ATT_ROWS):
                    rows = slice(128 * hh + r0, 128 * hh + r0 + ATT_ROWS)
                    s = s_ref[rows, :] + b_ref[table, hh, r0:r0 + ATT_ROWS, :]
                    top = jnp.max(s, axis=-1, keepdims=True)
                    p = jnp.exp(s - top)
                    total = jnp.sum(p, axis=-1, keepdims=True)
                    p_ref[rows, :] = (p / total).astype(BF16)
                    l_ref[rows, :] = jnp.broadcast_to(top + jnp.log(total), (ATT_ROWS, LANES))
            o = _dot(jnp.concatenate([p_ref[0:128, :], p_ref[128:256, :]], axis=1), _two_heads(vw, lo))
            o_ref[128 * sub:128 * (sub + 1), :] = o.astype(BF16)
            lse_ref[128 * sub:128 * (sub + 1), :] = jnp.where(lo, l_ref[0:128, :], l_ref[128:256, :])

    blk = pl.BlockSpec((R, LANES), lambda j, i: (i, j))
    return _call(
        body, name="attn_fwd", grid=(4, S // R),
        out_shape=(jax.ShapeDtypeStruct((S, ATT_W), BF16), jax.ShapeDtypeStruct((S, ATT_W), F32)),
        in_specs=_att_specs(), out_specs=(blk, blk),
        scratch_shapes=[pltpu.VMEM((256, BAND_KEYS), F32), pltpu.VMEM((256, BAND_KEYS), BF16),
                        pltpu.VMEM((256, LANES), F32)],
        operands=(qkv, qkv, qkv, qkv, qkv, bias), plans=plans)


def _sg_mask():
    t = lax.broadcasted_iota(jnp.int32, (128, 128), 0)
    s = lax.broadcasted_iota(jnp.int32, (128, 128), 1)
    return (s // CHUNK) <= (t // CHUNK)


def _sg_layernorm(gv, lng, lnb):
    mu = _group_sum64(gv) * (1.0 / 64)
    xc = gv - mu
    var = _group_sum64(xc * xc) * (1.0 / 64)
    rstd = lax.rsqrt(var + EPS)
    vhat = xc * rstd
    return vhat * lng + lnb, vhat, rstd


def _sgu_fwd(uv, lng, lnb, sg_w, b_exp, tm=512):
    S = uv.shape[0]

    def body(uv_ref, lng_ref, lnb_ref, w_ref, b_ref, y_ref):
        lane = lax.broadcasted_iota(jnp.int32, (1, LANES), 1)
        lo = lane < 64
        mask = _sg_mask()
        gu = _gelu(uv_ref[:, 0:SG_W].astype(F32))
        vln, _, _ = _sg_layernorm(_gelu(uv_ref[:, SG_W:2 * SG_W].astype(F32)), lng_ref[...], lnb_ref[...])
        for gp in range(4):
            w0 = jnp.where(mask, w_ref[2 * gp], 0).astype(BF16)
            w1 = jnp.where(mask, w_ref[2 * gp + 1], 0).astype(BF16)
            cols = slice(128 * gp, 128 * (gp + 1))
            for n in range(tm // 128):
                rows = slice(128 * n, 128 * (n + 1))
                vl = vln[rows, cols]
                sv = (_dot(w0, jnp.where(lo, vl, 0).astype(BF16)) + _dot(w1, jnp.where(lo, 0, vl).astype(BF16))
                      + b_ref[:, cols])
                y_ref[rows, cols] = (gu[rows, cols] * sv).astype(BF16)

    return pl.pallas_call(
        body, name="sgu_fwd", grid=(S // tm,),
        out_shape=jax.ShapeDtypeStruct((S, SG_W), BF16),
        in_specs=[_rows(tm, 2 * SG_W), _resident((1, SG_W)), _resident((1, SG_W)),
                  _resident((8, 128, 128)), _resident((128, SG_W))],
        out_specs=_rows(tm, SG_W),
        compiler_params=_cparams(1),
    )(uv, lng, lnb, sg_w, b_exp)


def _merge_fwd(x, y_att, y_sg, gates, wba_t, wbs_t, w_out, tm=512):
    S = x.shape[0]

    def body(x_ref, ya_ref, ys_ref, g_ref, wba_ref, wbs_ref, wo_ref, x1_ref):
        a = _dot_nt(ya_ref[...], wba_ref[...])
        b = _dot_nt(ys_ref[...], wbs_ref[...])
        merged = _sigmoid(g_ref[:, 0:D].astype(F32)) * a + _sigmoid(g_ref[:, D:2 * D].astype(F32)) * b
        x1_ref[...] = x_ref[...] + _dot(merged.astype(BF16), wo_ref[...])

    return pl.pallas_call(
        body, name="merge_fwd", grid=(S // tm,),
        out_shape=jax.ShapeDtypeStruct((S, D), F32),
        in_specs=[_rows(tm, D), _rows(tm, ATT_W), _rows(tm, SG_W), _rows(tm, 2 * D),
                  _resident((D, ATT_W)), _resident((D, SG_W)), _resident((D, D))],
        out_specs=_rows(tm, D),
        compiler_params=_cparams(1),
    )(x, y_att, y_sg, gates, wba_t, wbs_t, w_out)


def _mem_kv(mem, g_mem, w_xkv_t):
    def body(m_ref, g_ref, w_ref, kv_ref, mn_ref):
        mv = m_ref[...]
        mn = (mv * _rstd(mv) * g_ref[...]).astype(BF16)
        mn_ref[...] = mn
        kv_ref[...] = _dot_nt(mn, w_ref[...]).astype(BF16)

    vm = pl.BlockSpec(memory_space=pltpu.VMEM)
    return pl.pallas_call(
        body, name="mem_kv",
        out_shape=(jax.ShapeDtypeStruct((MEM, 2 * D), BF16), jax.ShapeDtypeStruct((MEM, D), BF16)),
        in_specs=[vm, vm, vm], out_specs=(vm, vm),
        compiler_params=pltpu.CompilerParams(vmem_limit_bytes=VMEM_LIMIT),
    )(mem, g_mem, w_xkv_t)


def _xatt_head(qx, kv_ref, h):
    hs = slice(256 * h, 256 * (h + 1))
    s = _dot_nt(qx[:, hs], kv_ref[:, hs])
    p = jnp.exp(s - jnp.max(s, axis=-1, keepdims=True))
    return p / jnp.sum(p, axis=-1, keepdims=True)


def _xattn_fwd(x1, g2, w_xq, kv, w_xo, tm=512):
    S = x1.shape[0]

    def body(x_ref, g_ref, wq_ref, kv_ref, wo_ref, x2_ref, hx_ref, qx_ref, o_ref):
        xv = x_ref[...]
        hx = (xv * _rstd(xv) * g_ref[...]).astype(BF16)
        hx_ref[...] = hx
        qx = (_dot(hx, wq_ref[...]) * (1.0 / 16)).astype(BF16)
        qx_ref[...] = qx
        for h in range(XH):
            p = _xatt_head(qx, kv_ref, h)
            o_ref[:, 256 * h:256 * (h + 1)] = _dot(p.astype(BF16), kv_ref[:, D + 256 * h:D + 256 * (h + 1)]).astype(BF16)
        x2_ref[...] = xv + _dot(o_ref[...], wo_ref[...])

    return pl.pallas_call(
        body, name="xattn_fwd", grid=(S // tm,),
        out_shape=(jax.ShapeDtypeStruct((S, D), F32),) + (jax.ShapeDtypeStruct((S, D), BF16),) * 3,
        in_specs=[_rows(tm, D), _resident((1, D)), _resident((D, D)), _resident((MEM, 2 * D)), _resident((D, D))],
        out_specs=(_rows(tm, D),) * 4,
        compiler_params=_cparams(1),
    )(x1, g2, w_xq, kv, w_xo)


def _ffn_fwd(x2, tgt, g3, w_ffn_in_t, w_ffn_out, g4, tm=256):
    S = x2.shape[0]

    def body(x_ref, t_ref, g3_ref, wi_ref, wo_ref, g4_ref, dx3_ref, gu_ref, hf_ref, act_ref, loss_ref, dg4_ref):
        i = pl.program_id(0)
        xv = x_ref[...]
        hf = (xv * _rstd(xv) * g3_ref[...]).astype(BF16)
        hf_ref[...] = hf
        for c in range(DFF // FF_CHUNK):
            cs = slice(FF_CHUNK * c, FF_CHUNK * (c + 1))
            us = slice(DFF + FF_CHUNK * c, DFF + FF_CHUNK * (c + 1))
            gate = _dot_nt(hf, wi_ref[cs, :])
            up = _dot_nt(hf, wi_ref[us, :])
            gu_ref[:, cs] = gate.astype(BF16)
            gu_ref[:, us] = up.astype(BF16)
            act_ref[:, cs] = ((gate * _sigmoid(gate)) * up).astype(BF16)
        acc = xv + _dot(act_ref[...], wo_ref[...])
        r4 = _rstd(acc)
        g4 = g4_ref[...]
        diff = acc * r4 * g4 - t_ref[...]
        dy = diff * (1.0 / D)
        dx3, dg4 = _rms_bwd(dy, acc, r4, g4)
        dx3_ref[...] = dx3
        part = 0.5 * jnp.sum(jnp.mean(diff * diff, axis=-1, keepdims=True))

        @pl.when(i == 0)
        def _():
            loss_ref[...] = jnp.zeros_like(loss_ref)
            dg4_ref[...] = jnp.zeros_like(dg4_ref)

        loss_ref[...] += jnp.full(loss_ref.shape, part, F32)
        dg4_ref[...] += dg4

    return pl.pallas_call(
        body, name="ffn_fwd", grid=(S // tm,),
        out_shape=(jax.ShapeDtypeStruct((S, D), F32), jax.ShapeDtypeStruct((S, 2 * DFF), BF16),
                   jax.ShapeDtypeStruct((S, D), BF16), jax.ShapeDtypeStruct((S, DFF), BF16),
                   jax.ShapeDtypeStruct((8, LANES), F32), jax.ShapeDtypeStruct((1, D), F32)),
        in_specs=[_rows(tm, D), _rows(tm, D), _resident((1, D)), _resident((2 * DFF, D)), _resident((DFF, D)),
                  _resident((1, D))],
        out_specs=(_rows(tm, D), _rows(tm, 2 * DFF), _rows(tm, D), _rows(tm, DFF),
                   pl.BlockSpec((8, LANES), lambda i: (0, 0)), pl.BlockSpec((1, D), lambda i: (0, 0))),
        compiler_params=_cparams(1),
    )(x2, tgt, g3, w_ffn_in_t, w_ffn_out, g4)


def _ffn_bwd(dx3, gu, x2, g3, w_ffn_out, w_ffn_in_t, tm=256):
    S = x2.shape[0]

    def body(d_ref, gu_ref, x_ref, g3_ref, wo_ref, wit_ref, dx2_ref, dgu_ref, dg3_ref):
        i = pl.program_id(0)
        d3 = d_ref[...]
        d3b = d3.astype(BF16)
        for c in range(DFF // FF_CHUNK):
            cs = slice(FF_CHUNK * c, FF_CHUNK * (c + 1))
            us = slice(DFF + FF_CHUNK * c, DFF + FF_CHUNK * (c + 1))
            da = _dot_nt(d3b, wo_ref[cs, :])
            gate = gu_ref[:, cs].astype(F32)
            up = gu_ref[:, us].astype(F32)
            sg = _sigmoid(gate)
            dgate = (da * up * (sg * (1.0 + gate * (1.0 - sg)))).astype(BF16)
            dup = (da * (gate * sg)).astype(BF16)
            dgu_ref[:, cs] = dgate
            dgu_ref[:, us] = dup
        dhf = _dot(dgu_ref[...], wit_ref[...])
        xv = x_ref[...]
        dx, dg3 = _rms_bwd(dhf, xv, _rstd(xv), g3_ref[...])
        dx2_ref[...] = d3 + dx

        @pl.when(i == 0)
        def _():
            dg3_ref[...] = jnp.zeros_like(dg3_ref)

        dg3_ref[...] += dg3

    return pl.pallas_call(
        body, name="ffn_bwd", grid=(S // tm,),
        out_shape=(jax.ShapeDtypeStruct((S, D), F32), jax.ShapeDtypeStruct((S, 2 * DFF), BF16),
                   jax.ShapeDtypeStruct((1, D), F32)),
        in_specs=[_rows(tm, D), _rows(tm, 2 * DFF), _rows(tm, D), _resident((1, D)),
                  _resident((DFF, D)), _resident((2 * DFF, D))],
        out_specs=(_rows(tm, D), _rows(tm, 2 * DFF), pl.BlockSpec((1, D), lambda i: (0, 0))),
        compiler_params=_cparams(1),
    )(dx3, gu, x2, g3, w_ffn_out, w_ffn_in_t)


def _dw(a, b, tmm, tn, ts, name, out_dtype=BF16, plans=()):
    S, M = a.shape
    N = b.shape[1]
    ts = min(ts, S)
    nk = S // ts

    def body(a_ref, b_ref, o_ref, acc_ref):
        k = pl.program_id(2)

        @pl.when(k == 0)
        def _():
            acc_ref[...] = jnp.zeros_like(acc_ref)

        acc_ref[...] += _dot_tn(a_ref[...].astype(BF16), b_ref[...].astype(BF16))

        @pl.when(k == nk - 1)
        def _():
            o_ref[...] = acc_ref[...].astype(out_dtype)

    out = _call(
        body, name=name, grid=(M // tmm, N // tn, nk),
        out_shape=(jax.ShapeDtypeStruct((M, N), out_dtype),),
        in_specs=[pl.BlockSpec((ts, tmm), lambda m, n, k: (k, m)), pl.BlockSpec((ts, tn), lambda m, n, k: (k, n))],
        out_specs=(pl.BlockSpec((tmm, tn), lambda m, n, k: (m, n)),),
        scratch_shapes=[pltpu.VMEM((tmm, tn), F32)],
        operands=(a, b), plans=plans)
    return out if plans else out[0]


def _xattn_bwd(dx2, x1, qx, g2, w_xq, w_xo, kv, plans=(), tm=512):
    S = x1.shape[0]

    def body(d_ref, x_ref, qx_ref, g_ref, wq_ref, wo_ref, kv_ref, dx1_ref, dq_ref, dkv_ref, dg2_ref):
        i = pl.program_id(0)

        @pl.when(i == 0)
        def _():
            dkv_ref[...] = jnp.zeros_like(dkv_ref)
            dg2_ref[...] = jnp.zeros_like(dg2_ref)

        d2 = d_ref[...]
        qx = qx_ref[...]
        do = _dot_nt(d2.astype(BF16), wo_ref[...]).astype(BF16)
        for h in range(XH):
            hs = slice(256 * h, 256 * (h + 1))
            vs = slice(D + 256 * h, D + 256 * (h + 1))
            p = _xatt_head(qx, kv_ref, h)
            dp = _dot_nt(do[:, hs], kv_ref[:, vs])
            ds = (p * (dp - jnp.sum(dp * p, axis=-1, keepdims=True))).astype(BF16)
            dq_ref[:, hs] = (_dot(ds, kv_ref[:, hs]) * (1.0 / 16)).astype(BF16)
            dkv_ref[:, hs] += _dot_tn(ds, qx[:, hs])
            dkv_ref[:, vs] += _dot_tn(p.astype(BF16), do[:, hs])
        dhx = _dot_nt(dq_ref[...], wq_ref[...])
        xv = x_ref[...]
        dx, dg2 = _rms_bwd(dhx, xv, _rstd(xv), g_ref[...])
        dx1_ref[...] = d2 + dx
        dg2_ref[...] += dg2

    return _call(
        body, name="xattn_bwd", grid=(S // tm,),
        out_shape=(jax.ShapeDtypeStruct((S, D), F32), jax.ShapeDtypeStruct((S, D), BF16),
                   jax.ShapeDtypeStruct((MEM, 2 * D), F32), jax.ShapeDtypeStruct((1, D), F32)),
        in_specs=[_rows(tm, D), _rows(tm, D), _rows(tm, D), _resident((1, D)), _resident((D, D)), _resident((D, D)),
                  _resident((MEM, 2 * D))],
        out_specs=(_rows(tm, D), _rows(tm, D),
                   pl.BlockSpec((MEM, 2 * D), lambda i: (0, 0)), pl.BlockSpec((1, D), lambda i: (0, 0))),
        operands=(dx2, x1, qx, g2, w_xq, w_xo, kv), plans=plans)


def _mem_kv_bwd(dkv, mem, g_mem, mn, w_xkv_t):
    def body(dkv_ref, m_ref, g_ref, mn_ref, wt_ref, dw_ref, dg_ref):
        dkvb = dkv_ref[...].astype(BF16)
        dw_ref[...] = _dot_tn(dkvb, mn_ref[...]).astype(BF16)
        dmn = _dot(dkvb, wt_ref[...])
        mv = m_ref[...]
        dg_ref[...] = jnp.sum(dmn * (mv * _rstd(mv)), axis=0, keepdims=True)

    vm = pl.BlockSpec(memory_space=pltpu.VMEM)
    return pl.pallas_call(
        body, name="mem_kv_bwd",
        out_shape=(jax.ShapeDtypeStruct((2 * D, D), BF16), jax.ShapeDtypeStruct((1, D), F32)),
        in_specs=[vm] * 5, out_specs=(vm, vm),
        compiler_params=pltpu.CompilerParams(vmem_limit_bytes=VMEM_LIMIT),
    )(dkv, mem, g_mem, mn, w_xkv_t)


def _merge_bwd(dx1, y_att, y_sg, gates, wba_t, wbs_t, w_out, tm=512):
    S = dx1.shape[0]

    def body(d_ref, ya_ref, ys_ref, g_ref, wbat_ref, wbst_ref, wo_ref,
             mg_ref, da_ref, db_ref, dya_ref, dys_ref, dg_ref):
        dm = _dot_nt(d_ref[...].astype(BF16), wo_ref[...])
        a = _dot_nt(ya_ref[...], wbat_ref[...])
        b = _dot_nt(ys_ref[...], wbst_ref[...])
        sa = _sigmoid(g_ref[:, 0:D].astype(F32))
        sb = _sigmoid(g_ref[:, D:2 * D].astype(F32))
        mg_ref[...] = (sa * a + sb * b).astype(BF16)
        da = (dm * sa).astype(BF16)
        db = (dm * sb).astype(BF16)
        da_ref[...] = da
        db_ref[...] = db
        dg_ref[:, 0:D] = (dm * a * sa * (1.0 - sa)).astype(BF16)
        dg_ref[:, D:2 * D] = (dm * b * sb * (1.0 - sb)).astype(BF16)
        dya_ref[...] = _dot(da, wbat_ref[...]).astype(BF16)
        dys_ref[...] = _dot(db, wbst_ref[...]).astype(BF16)

    return pl.pallas_call(
        body, name="merge_bwd", grid=(S // tm,),
        out_shape=(jax.ShapeDtypeStruct((S, D), BF16), jax.ShapeDtypeStruct((S, D), BF16),
                   jax.ShapeDtypeStruct((S, D), BF16), jax.ShapeDtypeStruct((S, ATT_W), BF16),
                   jax.ShapeDtypeStruct((S, SG_W), BF16), jax.ShapeDtypeStruct((S, 2 * D), BF16)),
        in_specs=[_rows(tm, D), _rows(tm, ATT_W), _rows(tm, SG_W), _rows(tm, 2 * D),
                  _resident((D, ATT_W)), _resident((D, SG_W)), _resident((D, D))],
        out_specs=(_rows(tm, D), _rows(tm, D), _rows(tm, D), _rows(tm, ATT_W), _rows(tm, SG_W), _rows(tm, 2 * D)),
        compiler_params=_cparams(1),
    )(dx1, y_att, y_sg, gates, wba_t, wbs_t, w_out)


def _sgu_bwd(uv, dy_sg, lng, lnb, sg_w, b_exp, plans=(), tm=512):
    S = uv.shape[0]
    n_steps = S // tm

    def body(uv_ref, dy_ref, lng_ref, lnb_ref, w_ref, b_ref, duv_ref, dw_ref, dbx_ref, dlng_ref, dlnb_ref, dvln_ref):
        i = pl.program_id(0)

        @pl.when(i == 0)
        def _():
            dw_ref[...] = jnp.zeros_like(dw_ref)
            dbx_ref[...] = jnp.zeros_like(dbx_ref)
            dlng_ref[...] = jnp.zeros_like(dlng_ref)
            dlnb_ref[...] = jnp.zeros_like(dlnb_ref)

        lane = lax.broadcasted_iota(jnp.int32, (1, LANES), 1)
        lo = lane < 64
        mask = _sg_mask()
        lng = lng_ref[...]
        gu, dgelu_u = _gelu_and_grad(uv_ref[:, 0:SG_W].astype(F32))
        gv, dgelu_v = _gelu_and_grad(uv_ref[:, SG_W:2 * SG_W].astype(F32))
        vln, vhat, rstd = _sg_layernorm(gv, lng, lnb_ref[...])
        dy = dy_ref[...].astype(F32)
        dsv_all = dy * gu
        for gp in range(4):
            wf0 = jnp.where(mask, w_ref[2 * gp], 0)
            wf1 = jnp.where(mask, w_ref[2 * gp + 1], 0)
            w0 = wf0.astype(BF16)
            w1 = wf1.astype(BF16)
            cols = slice(128 * gp, 128 * (gp + 1))
            dw0 = jnp.zeros((128, 128), F32)
            dw1 = jnp.zeros((128, 128), F32)
            dbx = jnp.zeros((128, LANES), F32)
            for n in range(tm // 128):
                rows = slice(128 * n, 128 * (n + 1))
                vl = vln[rows, cols]
                vl0 = jnp.where(lo, vl, 0).astype(BF16)
                vl1 = jnp.where(lo, 0, vl).astype(BF16)
                sv = _dot(w0, vl0) + _dot(w1, vl1) + b_ref[:, cols]
                duv_ref[rows, cols] = (dy[rows, cols] * sv * dgelu_u[rows, cols]).astype(BF16)
                dsv = dsv_all[rows, cols]
                dbx = dbx + dsv
                ds0 = jnp.where(lo, dsv, 0).astype(BF16)
                ds1 = jnp.where(lo, 0, dsv).astype(BF16)
                dw0 = dw0 + _dot_nt(ds0, vl0)
                dw1 = dw1 + _dot_nt(ds1, vl1)
                dvln_ref[rows, cols] = _dot_tn(w0, ds0) + _dot_tn(w1, ds1)
            dw_ref[2 * gp] += jnp.where(mask, dw0, 0)
            dw_ref[2 * gp + 1] += jnp.where(mask, dw1, 0)
            dbx_ref[:, cols] += dbx
        dvln = dvln_ref[...]
        dlng_ref[...] += jnp.sum(dvln * vhat, axis=0, keepdims=True)
        dlnb_ref[...] += jnp.sum(dvln, axis=0, keepdims=True)
        dvh = dvln * lng
        dgv = rstd * (dvh - _group_sum64(dvh) * (1.0 / 64) - vhat * (_group_sum64(dvh * vhat) * (1.0 / 64)))
        duv_ref[:, SG_W:2 * SG_W] = (dgv * dgelu_v).astype(BF16)

        @pl.when(i == n_steps - 1)
        def _():
            dbx_ref[...] = _group_sum64(dbx_ref[...])

    return _call(
        body, name="sgu_bwd", grid=(n_steps,),
        out_shape=(jax.ShapeDtypeStruct((S, 2 * SG_W), BF16), jax.ShapeDtypeStruct((8, 128, 128), F32),
                   jax.ShapeDtypeStruct((128, SG_W), F32), jax.ShapeDtypeStruct((1, SG_W), F32),
                   jax.ShapeDtypeStruct((1, SG_W), F32)),
        in_specs=[_rows(tm, 2 * SG_W), _rows(tm, SG_W), _resident((1, SG_W)), _resident((1, SG_W)),
                  _resident((8, 128, 128)), _resident((128, SG_W))],
        out_specs=(_rows(tm, 2 * SG_W), pl.BlockSpec((8, 128, 128), lambda i: (0, 0, 0)),
                   pl.BlockSpec((128, SG_W), lambda i: (0, 0)), pl.BlockSpec((1, SG_W), lambda i: (0, 0)),
                   pl.BlockSpec((1, SG_W), lambda i: (0, 0))),
        scratch_shapes=[pltpu.VMEM((tm, SG_W), F32)],
        operands=(uv, dy_sg, lng, lnb, sg_w, b_exp), plans=plans)


def _attn_bwd(qkv, dy_att, y_att, lse, bias, plans=()):
    S = qkv.shape[0]
    R = ATT_R

    def body(q_ref, kp_ref, kc_ref, vp_ref, vc_ref, b_ref, dy_ref, y_ref, lse_ref, dq_ref, dk_ref, dv_ref, dss_ref,
             s_ref, dp_ref, pb_ref, dsb_ref):
        i = pl.program_id(1)

        @pl.when(i == 0)
        def _():
            dk_ref[...] = jnp.zeros_like(dk_ref)
            dv_ref[...] = jnp.zeros_like(dv_ref)
            dss_ref[...] = jnp.zeros_like(dss_ref)

        lo = lax.broadcasted_iota(jnp.int32, (1, LANES), 1) < 64
        kwin = jnp.concatenate([kp_ref[R - 8 * CHUNK:R, :], kc_ref[...]], axis=0)
        vwin = jnp.concatenate([vp_ref[R - 8 * CHUNK:R, :], vc_ref[...]], axis=0)
        for sub in range(ATT_SUBS):
            rows = slice(128 * sub, 128 * (sub + 1))
            kw = kwin[128 * sub:128 * sub + BAND_KEYS]
            vw = vwin[128 * sub:128 * sub + BAND_KEYS]
            table = jnp.where(i == 0, 1 + sub, 0) if sub < START_SUBS else 0
            qs = _two_heads(q_ref[rows, :], lo)
            dos = _two_heads(dy_ref[rows, :], lo)
            dyy = dy_ref[rows, :].astype(F32) * y_ref[rows, :].astype(F32)
            delta = (jnp.sum(jnp.where(lo, dyy, 0.0), axis=-1, keepdims=True),
                     jnp.sum(jnp.where(lo, 0.0, dyy), axis=-1, keepdims=True))
            lse2 = lse_ref[rows, :]
            s_ref[...] = _dot_nt(qs, kw)
            dp_ref[...] = _dot_nt(dos, vw)
            for hh in range(2):
                lse = lse2[:, 64 * hh:64 * hh + 1]
                for r0 in range(0, 128, ATT_ROWS_BWD):
                    rr = slice(r0, r0 + ATT_ROWS_BWD)
                    both = slice(128 * hh + r0, 128 * hh + r0 + ATT_ROWS_BWD)
                    p = jnp.exp(s_ref[both, :] + b_ref[table, hh, rr, :] - lse[rr])
                    ds = p * (dp_ref[both, :] - delta[hh][rr])
                    dss_ref[hh, rr, :] += ds
                    pb_ref[both, :] = p.astype(BF16)
                    dsb_ref[both, :] = ds.astype(BF16)
            dq = _dot(jnp.concatenate([dsb_ref[0:128, :], dsb_ref[128:256, :]], axis=1), _two_heads(kw, lo))
            dq_ref[rows, :] = dq.astype(BF16)
            start = pl.multiple_of(i * R + 128 * sub, 128)
            dk_ref[pl.ds(start, BAND_KEYS), :] += _dot_tn(dsb_ref[...], qs)
            dv_ref[pl.ds(start, BAND_KEYS), :] += _dot_tn(pb_ref[...], dos)

    acc_spec = pl.BlockSpec((S + 8 * CHUNK, LANES), lambda j, i: (0, j))
    return _call(
        body, name="attn_bwd", grid=(4, S // R),
        out_shape=(jax.ShapeDtypeStruct((S, ATT_W), BF16), jax.ShapeDtypeStruct((S + 8 * CHUNK, ATT_W), F32),
                   jax.ShapeDtypeStruct((S + 8 * CHUNK, ATT_W), F32), jax.ShapeDtypeStruct((8, 128, BAND_KEYS), F32)),
        in_specs=_att_specs() + [pl.BlockSpec((R, LANES), lambda j, i: (i, j))] * 3,
        out_specs=(pl.BlockSpec((R, LANES), lambda j, i: (i, j)), acc_spec, acc_spec,
                   pl.BlockSpec((2, 128, BAND_KEYS), lambda j, i: (j, 0, 0))),
        scratch_shapes=[pltpu.VMEM((256, BAND_KEYS), F32), pltpu.VMEM((256, BAND_KEYS), F32),
                        pltpu.VMEM((256, BAND_KEYS), BF16), pltpu.VMEM((256, BAND_KEYS), BF16)],
        operands=(qkv, qkv, qkv, qkv, qkv, bias, dy_att, y_att, lse), plans=plans)


def _in_bwd(dq, dk, dv, duv, dgates, x, dx1, g1, w_in_t, plans=(), tm=512):
    S = x.shape[0]
    pad_blocks = (8 * CHUNK) // tm

    def body(dq_ref, dk_ref, dv_ref, duv_ref, dg_ref, x_ref, d1_ref, g_ref, wt_ref, dx_ref, dz_ref, dg1_ref):
        i = pl.program_id(0)
        dz_ref[:, 0:ATT_W] = (dq_ref[...].astype(F32) * 0.125).astype(BF16)
        dz_ref[:, ATT_W:2 * ATT_W] = dk_ref[...].astype(BF16)
        dz_ref[:, 2 * ATT_W:3 * ATT_W] = dv_ref[...].astype(BF16)
        dz_ref[:, 3 * ATT_W:3 * ATT_W + 2 * SG_W] = duv_ref[...]
        dz_ref[:, 3 * ATT_W + 2 * SG_W:IN_COLS] = dg_ref[...]
        dh = _dot(dz_ref[...], wt_ref[...])
        xv = x_ref[...]
        dx, dg1 = _rms_bwd(dh, xv, _rstd(xv), g_ref[...])
        dx_ref[...] = d1_ref[...] + dx

        @pl.when(i == 0)
        def _():
            dg1_ref[...] = jnp.zeros_like(dg1_ref)

        dg1_ref[...] += dg1

    shifted = pl.BlockSpec((tm, ATT_W), lambda i: (i + pad_blocks, 0))
    return _call(
        body, name="in_bwd", grid=(S // tm,),
        out_shape=(jax.ShapeDtypeStruct((S, D), F32), jax.ShapeDtypeStruct((S, IN_COLS), BF16),
                   jax.ShapeDtypeStruct((1, D), F32)),
        in_specs=[_rows(tm, ATT_W), shifted, shifted, _rows(tm, 2 * SG_W), _rows(tm, 2 * D), _rows(tm, D),
                  _rows(tm, D), _resident((1, D)), _resident((IN_COLS, D))],
        out_specs=(_rows(tm, D), _rows(tm, IN_COLS), pl.BlockSpec((1, D), lambda i: (0, 0))),
        operands=(dq, dk, dv, duv, dgates, x, dx1, g1, w_in_t), plans=plans)


def _adam_math(w, g, m, v):
    m = ADAM_B1 * m + (1.0 - ADAM_B1) * g
    v = ADAM_B2 * v + (1.0 - ADAM_B2) * (g * g)
    m_hat = m / (1.0 - ADAM_B1 ** ADAM_STEP)
    v_hat = v / (1.0 - ADAM_B2 ** ADAM_STEP)
    delta = -ADAM_LR * (m_hat / (jnp.sqrt(v_hat) + ADAM_EPS) + ADAM_WD * w)
    return delta, m, v


def _adam(parts, w, m, v, tr, name, transposed, after=None):
    P = parts.shape[0]
    R, C = w.shape

    def body(p_ref, w_ref, m_ref, v_ref, *rest):
        g_ref, d_ref, mo_ref, vo_ref = rest[-4:]
        if transposed:
            eye = (lax.broadcasted_iota(jnp.int32, (C, C), 0) == lax.broadcasted_iota(jnp.int32, (C, C), 1)).astype(BF16)
            part = lambda k: _dot_tn(p_ref[k], eye)
        else:
            part = lambda k: p_ref[k].astype(F32)
        g = part(0)
        for k in range(1, P):
            g = g + part(k)
        g_ref[...] = g
        d_ref[...], mo_ref[...], vo_ref[...] = _adam_math(w_ref[...], g, m_ref[...], v_ref[...])

    row = pl.BlockSpec((tr, C), lambda i: (i, 0))
    p_spec = pl.BlockSpec((P, C, tr), lambda i: (0, 0, i)) if transposed else pl.BlockSpec((P, tr, C), lambda i: (0, i, 0))
    extra = [] if after is None else [after]
    return pl.pallas_call(
        body, name=name, grid=(R // tr,),
        out_shape=tuple(jax.ShapeDtypeStruct((R, C), F32) for _ in range(4)),
        in_specs=[p_spec, row, row, row] + [pl.BlockSpec(memory_space=pl.ANY)] * len(extra),
        out_specs=(row, row, row, row),
        compiler_params=_cparams(1),
    )(parts, w, m, v, *extra)


def _my_place():
    return lax.axis_index("x"), lax.axis_index("y"), lax.axis_index("c")


def _other_chips(x, y):
    return [(1 - x, y), (x, 1 - y), (1 - x, 1 - y)]


class _Plan:
    def __init__(self, arrays, out_shapes, sems, start, finish, forward=None, forward_at=0.7):
        self.arrays, self.out_shapes, self.sems = list(arrays), list(out_shapes), list(sems)
        self.start, self.finish, self.forward, self.forward_at = start, finish, forward, forward_at


def _call(body, *, name, grid, in_specs, out_specs, out_shape, operands, scratch_shapes=(), plans=()):
    n_in, n_out, n_scr = len(operands), len(out_shape), len(scratch_shapes)
    p_in = [a for p in plans for a in p.arrays]
    p_out = [s for p in plans for s in p.out_shapes]
    p_sem = [s for p in plans for s in p.sems]
    steps = math.prod(grid)

    def wrapped(*refs):
        ins, refs = refs[:n_in], refs[n_in:]
        pins, refs = refs[:len(p_in)], refs[len(p_in):]
        outs, refs = refs[:n_out], refs[n_out:]
        pouts, refs = refs[:len(p_out)], refs[len(p_out):]
        scr, psems = refs[:n_scr], refs[n_scr:]
        step = 0
        for axis, size in enumerate(grid):
            step = step * size + pl.program_id(axis)
        bound = []
        for p in plans:
            bound.append((p, pins[:len(p.arrays)], pouts[:len(p.out_shapes)], psems[:len(p.sems)]))
            pins, pouts, psems = pins[len(p.arrays):], pouts[len(p.out_shapes):], psems[len(p.sems):]

        @pl.when(step == 0)
        def _():
            for p, a, b, s in bound:
                p.start(a, b, s)

        body(*ins, *outs, *scr)

        for p, a, b, s in bound:
            if p.forward is not None:
                @pl.when(step == min(int(p.forward_at * steps), steps - 1))
                def _(p=p, a=a, b=b, s=s):
                    p.forward(a, b, s)

        @pl.when(step == steps - 1)
        def _():
            for p, a, b, s in bound:
                p.finish(a, b, s)

    hbm = pl.BlockSpec(memory_space=pl.ANY)
    return pl.pallas_call(
        wrapped if plans else body, name=name, grid=grid,
        out_shape=tuple(out_shape) + tuple(p_out),
        in_specs=list(in_specs) + [hbm] * len(p_in),
        out_specs=tuple(out_specs) + tuple([hbm] * len(p_out)),
        scratch_shapes=list(scratch_shapes) + p_sem,
        compiler_params=_cparams(len(grid)),
    )(*operands, *p_in)


def _run_plan(plan, name):
    n_in, n_out = len(plan.arrays), len(plan.out_shapes)

    def body(*refs):
        a, b, s = refs[:n_in], refs[n_in:n_in + n_out], refs[n_in + n_out:]
        plan.start(a, b, s)
        if plan.forward is not None:
            plan.forward(a, b, s)
        plan.finish(a, b, s)

    hbm = pl.BlockSpec(memory_space=pl.ANY)
    return pl.pallas_call(
        body, name=name, out_shape=tuple(plan.out_shapes),
        in_specs=[hbm] * n_in, out_specs=tuple([hbm] * n_out), scratch_shapes=plan.sems,
    )(*plan.arrays)


def _gather_plan(shards, forward_at=0.7):
    n = len(shards)

    def copies(ins, outs, sems):
        send_sems, recv_sems, local_sems = sems
        x, y, c = _my_place()
        me, sibling = (x, y, c), (x, y, 1 - c)
        chips = _other_chips(x, y)

        def idx(p):
            return 4 * p[0] + 2 * p[1] + p[2]

        def copy(a, k, block, to, src=None):
            return pltpu.make_async_remote_copy(
                src_ref=outs[a].at[idx(block)] if src is None else src, dst_ref=outs[a].at[idx(block)],
                send_sem=send_sems.at[a, k], recv_sem=recv_sems.at[a, k], device_id=to, device_id_type=MESH)

        mine = [pltpu.make_async_copy(ins[a], outs[a].at[idx(me)], local_sems.at[a]) for a in range(n)]
        first = []
        for a in range(n):
            first.append(copy(a, 0, me, sibling, src=ins[a]))
            first += [copy(a, 1 + j, me, (*chip, c), src=ins[a]) for j, chip in enumerate(chips)]
        arrived = [copy(a, 1 + j, (*chip, c), me) for j, chip in enumerate(chips) for a in range(n)]
        passed = [copy(a, 4 + j, (*chip, c), sibling) for j, chip in enumerate(chips) for a in range(n)]
        from_sibling = []
        for a in range(n):
            from_sibling.append(copy(a, 0, sibling, me))
            from_sibling += [copy(a, 4 + j, (*chip, 1 - c), me) for j, chip in enumerate(chips)]
        return mine, first, arrived, passed, from_sibling

    def start(ins, outs, sems):
        mine, first, _, _, _ = copies(ins, outs, sems)
        for cp in mine + first:
            cp.start()

    def forward(ins, outs, sems):
        _, _, arrived, passed, _ = copies(ins, outs, sems)
        for landed, onward in zip(arrived, passed):
            landed.wait_recv()
            onward.start()

    def finish(ins, outs, sems):
        mine, first, _, passed, from_sibling = copies(ins, outs, sems)
        for cp in from_sibling:
            cp.wait_recv()
        for cp in first + passed:
            cp.wait_send()
        for cp in mine:
            cp.wait()

    return _Plan(shards, [jax.ShapeDtypeStruct((N_DEV,) + s.shape, s.dtype) for s in shards],
                 [pltpu.SemaphoreType.DMA((n, 7)), pltpu.SemaphoreType.DMA((n, 7)), pltpu.SemaphoreType.DMA((n,))],
                 start, finish, forward, forward_at)


def _sibling_plan(scatter, whole=()):
    ns = len(scatter)
    n = ns + len(whole)

    def copies(ins, outs, sems):
        send_sems, recv_sems = sems
        x, y, c = _my_place()
        out = []
        for a in range(n):
            for k in range(4 if a < ns else 1):
                src = ins[a].at[2 * k + (1 - c)] if a < ns else ins[a]
                dst = outs[a].at[k] if a < ns else outs[a]
                out.append(pltpu.make_async_remote_copy(
                    src_ref=src, dst_ref=dst, send_sem=send_sems.at[a, k], recv_sem=recv_sems.at[a, k],
                    device_id=(x, y, 1 - c), device_id_type=MESH))
        return out

    def start(ins, outs, sems):
        for cp in copies(ins, outs, sems):
            cp.start()

    def finish(ins, outs, sems):
        for cp in copies(ins, outs, sems):
            cp.wait()

    shapes = [jax.ShapeDtypeStruct((4,) + p.shape[1:], p.dtype) for p in scatter]
    shapes += [jax.ShapeDtypeStruct(p.shape, p.dtype) for p in whole]
    return _Plan(list(scatter) + list(whole), shapes,
                 [pltpu.SemaphoreType.DMA((n, 4)), pltpu.SemaphoreType.DMA((n, 4))], start, finish)


def _pair_sums(mine, theirs, c_idx, name):
    n = len(mine)

    def body(c_ref, *refs):
        for a in range(n):
            out = refs[2 * n + a]
            out[...] = (refs[a][...].astype(F32) + refs[n + a][...].astype(F32)).astype(out.dtype)

    def block(arr):
        return (1,) + arr.shape[1:]

    grid_spec = pltpu.PrefetchScalarGridSpec(
        num_scalar_prefetch=1, grid=(4,),
        in_specs=[pl.BlockSpec(block(m), lambda k, c_ref: (2 * k + c_ref[0], 0, 0)) for m in mine]
        + [pl.BlockSpec(block(t), lambda k, c_ref: (k, 0, 0)) for t in theirs],
        out_specs=tuple(pl.BlockSpec(block(t), lambda k, c_ref: (k, 0, 0)) for t in theirs))
    return pl.pallas_call(
        body, name=name, grid_spec=grid_spec,
        out_shape=tuple(jax.ShapeDtypeStruct(t.shape, m.dtype) for m, t in zip(mine, theirs)),
        compiler_params=_cparams(1),
    )(c_idx, *mine, *theirs)


def _peers_plan(arrays):
    n = len(arrays)

    def copies(ins, outs, sems):
        send_sems, recv_sems, local_sems = sems
        x, y, c = _my_place()
        me = 4 * x + 2 * y + c
        out = [pltpu.make_async_copy(ins[a], outs[a].at[me], local_sems.at[a]) for a in range(n)]
        for a in range(n):
            for k in range(N_DEV - 1):
                bits = k + 1
                peer = (x ^ (bits >> 2), y ^ ((bits >> 1) & 1), c ^ (bits & 1))
                out.append(pltpu.make_async_remote_copy(
                    src_ref=ins[a], dst_ref=outs[a].at[me], send_sem=send_sems.at[a, k], recv_sem=recv_sems.at[a, k],
                    device_id=peer, device_id_type=MESH))
        return out

    def start(ins, outs, sems):
        for cp in copies(ins, outs, sems):
            cp.start()

    def finish(ins, outs, sems):
        for cp in copies(ins, outs, sems):
            cp.wait()

    return _Plan(list(arrays), [jax.ShapeDtypeStruct((N_DEV,) + a.shape, a.dtype) for a in arrays],
                 [pltpu.SemaphoreType.DMA((n, N_DEV - 1)), pltpu.SemaphoreType.DMA((n, N_DEV - 1)),
                  pltpu.SemaphoreType.DMA((n,))], start, finish)


def _chips_plan(scatter, whole=()):
    ns = len(scatter)
    n = ns + len(whole)

    def copies(ins, outs, sems):
        send_sems, recv_sems, local_sems = sems
        x, y, c = _my_place()
        my_chip = 2 * x + y

        def src(a, k):
            return ins[a].at[k] if a < ns else ins[a]

        local = [pltpu.make_async_copy(src(a, my_chip), outs[a].at[my_chip], local_sems.at[a]) for a in range(n)]
        remote = []
        for a in range(n):
            for j, (px, py) in enumerate(_other_chips(x, y)):
                remote.append(pltpu.make_async_remote_copy(
                    src_ref=src(a, 2 * px + py), dst_ref=outs[a].at[my_chip],
                    send_sem=send_sems.at[a, j], recv_sem=recv_sems.at[a, j],
                    device_id=(px, py, c), device_id_type=MESH))
        return local + remote

    def start(ins, outs, sems):
        for cp in copies(ins, outs, sems):
            cp.start()

    def finish(ins, outs, sems):
        for cp in copies(ins, outs, sems):
            cp.wait()

    shapes = [jax.ShapeDtypeStruct(s.shape, s.dtype) for s in scatter]
    shapes += [jax.ShapeDtypeStruct((4,) + s.shape, s.dtype) for s in whole]
    return _Plan(list(scatter) + list(whole), shapes,
                 [pltpu.SemaphoreType.DMA((n, 3)), pltpu.SemaphoreType.DMA((n, 3)), pltpu.SemaphoreType.DMA((n,))],
                 start, finish)


def _chip_copies(src_ref, land_ref, send_sems, recv_sems):
    x, y, c = _my_place()
    return [pltpu.make_async_remote_copy(
        src_ref=src_ref.at[2 * px + py], dst_ref=land_ref.at[2 * x + y], send_sem=send_sems.at[j],
        recv_sem=recv_sems.at[j], device_id=(px, py, c), device_id_type=MESH)
        for j, (px, py) in enumerate(_other_chips(x, y))]


def _chips_start(src, name):
    def body(src_ref, land_ref, send_sems, recv_sems, src_thru, land_thru, token):
        for cp in _chip_copies(src_ref, land_ref, send_sems, recv_sems):
            cp.start()
        token[...] = jnp.zeros_like(token)

    hbm = pl.BlockSpec(memory_space=pltpu.HBM)
    sem = pl.BlockSpec(memory_space=pltpu.SEMAPHORE)
    return pl.pallas_call(
        body, name=name,
        out_shape=(pltpu.SemaphoreType.DMA((3,)), pltpu.SemaphoreType.DMA((3,)), pltpu.HBM(src.shape, src.dtype),
                   pltpu.HBM(src.shape, src.dtype), jax.ShapeDtypeStruct((8, LANES), F32)),
        in_specs=(hbm, hbm), out_specs=(sem, sem, hbm, hbm, pl.BlockSpec(memory_space=pltpu.VMEM)),
        input_output_aliases={0: 2, 1: 3},
        compiler_params=pltpu.CompilerParams(has_side_effects=pltpu.SideEffectType.DATAFLOW_SIDE_EFFECTING),
    )(pltpu.with_memory_space_constraint(src, pltpu.HBM),
      pltpu.with_memory_space_constraint(jnp.zeros(src.shape, src.dtype), pltpu.HBM))


def _chips_wait(send_sems, recv_sems, src_thru, land_thru, after, name):
    def body(src_ref, land_ref, send_sems, recv_sems, after_ref, src_dead, got_ref):
        for cp in _chip_copies(src_ref, land_ref, send_sems, recv_sems):
            cp.wait_send()
            cp.wait_recv()

    hbm = pl.BlockSpec(memory_space=pltpu.HBM)
    sem = pl.BlockSpec(memory_space=pltpu.SEMAPHORE)
    return pl.pallas_call(
        body, name=name,
        out_shape=(pltpu.HBM(src_thru.shape, src_thru.dtype), pltpu.HBM(land_thru.shape, land_thru.dtype)),
        in_specs=(hbm, hbm, sem, sem, pl.BlockSpec(memory_space=pl.ANY)), out_specs=(hbm, hbm),
        input_output_aliases={0: 0, 1: 1},
        compiler_params=pltpu.CompilerParams(has_side_effects=pltpu.SideEffectType.DATAFLOW_SIDE_EFFECTING),
    )(src_thru, land_thru, send_sems, recv_sems, after)


def _small_update(parts, w, m, v, loss_parts, name):
    n = len(parts)

    def total(ref):
        acc = ref[0]
        for k in range(1, ref.shape[0]):
            acc = acc + ref[k]
        return acc

    def body(*refs):
        p_refs, w_refs, m_refs, v_refs = (refs[i * n:(i + 1) * n] for i in range(4))
        lp_ref = refs[4 * n]
        outs = refs[4 * n + 1:]
        g_refs, d_refs, mo_refs, vo_refs = (outs[i * n:(i + 1) * n] for i in range(4))
        for a in range(n):
            g = total(p_refs[a])
            g_refs[a][...] = g
            d_refs[a][...], mo_refs[a][...], vo_refs[a][...] = _adam_math(w_refs[a][...], g, m_refs[a][...],
                                                                          v_refs[a][...])
        outs[4 * n][...] = total(lp_ref)

    vm = pl.BlockSpec(memory_space=pltpu.VMEM)
    shapes = [jax.ShapeDtypeStruct(t.shape, F32) for _ in range(4) for t in w]
    shapes.append(jax.ShapeDtypeStruct(loss_parts.shape[1:], F32))
    outs = pl.pallas_call(
        body, name=name, out_shape=tuple(shapes),
        in_specs=[vm] * (4 * n + 1), out_specs=tuple([vm] * (4 * n + 1)),
        compiler_params=pltpu.CompilerParams(vmem_limit_bytes=VMEM_LIMIT),
    )(*parts, *w, *m, *v, loss_parts)
    return outs[0:n], outs[n:2 * n], outs[2 * n:3 * n], outs[3 * n:4 * n], outs[4 * n]


BIG = [("w_in", 1), ("w_branch_att", 1), ("w_branch_sg", 1), ("w_out", 0), ("w_xq", 0), ("w_xkv", 1), ("w_xo", 0),
       ("w_ffn_in", 1), ("w_ffn_out", 0)]
SMALL = [("norm_mix_g", (1, D)), ("rel_bias", (8, NREL)), ("sg_ln_g", (8, 64)), ("sg_ln_b", (8, 64)),
         ("sg_w", (8, 128, 128)), ("sg_b", (8, 128)), ("norm_xattn_g", (1, D)), ("norm_mem_g", (1, D)),
         ("norm_ffn_g", (1, D)), ("norm_final_g", (1, D))]
ADAM_ROWS = {"w_in": 192, "w_branch_att": 512, "w_branch_sg": 512, "w_xkv": 1024, "w_ffn_in": 176}


def _full(gathered):
    return gathered.reshape(N_DEV * gathered.shape[1], gathered.shape[2])


def _blocks(grad):
    return grad.reshape(N_DEV, grad.shape[0] // N_DEV, grad.shape[1])


def kernel(x, mem, norm_mix_g, w_in, rel_bias, sg_ln_g, sg_ln_b, sg_w, sg_b, w_branch_att, w_branch_sg, w_out, norm_xattn_g, norm_mem_g, w_xq, w_xkv, w_xo, norm_ffn_g, w_ffn_in, w_ffn_out, norm_final_g, loss_target, m_norm_mix_g, m_w_in, m_rel_bias, m_sg_ln_g, m_sg_ln_b, m_sg_w, m_sg_b, m_w_branch_att, m_w_branch_sg, m_w_out, m_norm_xattn_g, m_norm_mem_g, m_w_xq, m_w_xkv, m_w_xo, m_norm_ffn_g, m_w_ffn_in, m_w_ffn_out, m_norm_final_g, v_norm_mix_g, v_w_in, v_rel_bias, v_sg_ln_g, v_sg_ln_b, v_sg_w, v_sg_b, v_w_branch_att, v_w_branch_sg, v_w_out, v_norm_xattn_g, v_norm_mem_g, v_w_xq, v_w_xkv, v_w_xo, v_norm_ffn_g, v_w_ffn_in, v_w_ffn_out, v_norm_final_g):
    args = dict(locals())
    big_names = [n for n, _ in BIG]
    small_names = [n for n, _ in SMALL]
    S = x.shape[1]

    x, mem, tgt = x.reshape(S, D), mem.reshape(MEM, D), loss_target.reshape(S, D)
    small = {n: args[n].reshape(shape) for n, shape in SMALL}
    g1, g2, g3 = small["norm_mix_g"], small["norm_xattn_g"], small["norm_ffn_g"]
    g_mem, g4 = small["norm_mem_g"], small["norm_final_g"]
    lng = small["sg_ln_g"].reshape(1, SG_W)
    lnb = small["sg_ln_b"].reshape(1, SG_W)
    b_exp = jnp.broadcast_to(small["sg_b"].T[:, :, None], (128, 8, 64)).reshape(128, SG_W)
    rel_pad = jnp.pad(small["rel_bias"], ((0, 0), (0, 384 - NREL)))
    c_idx = lax.axis_index("c").astype(jnp.int32).reshape(1)

    shard = {n: (args[n][0].T if axis == 1 else args[n][0]).astype(BF16) for n, axis in BIG}
    h, w_in_gathered = _norm_in(x, g1, plans=[_gather_plan([shard["w_in"]])])
    w_in_t = _full(w_in_gathered)
    bias = _bias_table(rel_pad)
    mix_names = ["w_branch_att", "w_branch_sg", "w_out", "w_xq", "w_xkv", "w_xo"]
    qkv, uv, gates, *got = _in_proj(h, w_in_t, plans=[_gather_plan([shard[n] for n in mix_names])])
    wba_t, wbs_t, w_out_f, w_xq_f, w_xkv_t, w_xo_f = (_full(g) for g in got)
    y_att, lse, *got = _attn_fwd(qkv, bias, plans=[_gather_plan([shard["w_ffn_in"], shard["w_ffn_out"]])])
    w_ffn_in_t, w_ffn_out_f = (_full(g) for g in got)
    y_sg = _sgu_fwd(uv, lng, lnb, small["sg_w"], b_exp)
    x1 = _merge_fwd(x, y_att, y_sg, gates, wba_t, wbs_t, w_out_f)
    kv, mn = _mem_kv(mem, g_mem, w_xkv_t)
    x2, hx, qx, o_x = _xattn_fwd(x1, g2, w_xq_f, kv, w_xo_f)
    dx3, gu, hf, act, loss_part, dg4 = _ffn_fwd(x2, tgt, g3, w_ffn_in_t, w_ffn_out_f, g4)


    dx2, dgu, dg3 = _ffn_bwd(dx3, gu, x2, g3, w_ffn_out_f, w_ffn_in_t)
    ffn_names = ["w_ffn_out", "w_ffn_in"]
    ffn_mine = [_blocks(_dw(act, dx3, 1408, 1024, DW_TOKENS, "dw_ffn_out")),
                _blocks(_dw(dgu, hf, 1408, 1024, DW_TOKENS, "dw_ffn_in"))]
    dx1, dq_x, dkv, dg2, *ffn_theirs = _xattn_bwd(dx2, x1, qx, g2, w_xq_f, w_xo_f, kv,
                                                  plans=[_sibling_plan(ffn_mine)])
    ffn_chip = _pair_sums(ffn_mine, ffn_theirs, c_idx, "rs_pair_ffn")
    d_xkv, dg_mem = _mem_kv_bwd(dkv, mem, g_mem, mn, w_xkv_t)
    merged, d_a, d_b, dy_att, dy_sg, dgates = _merge_bwd(dx1, y_att, y_sg, gates, wba_t, wbs_t, w_out_f)
    mid_names = ["w_xo", "w_xq", "w_xkv", "w_out", "w_branch_att", "w_branch_sg"]
    mid_mine = [_blocks(g) for g in (
        _dw(o_x, dx2, 1024, 1024, DW_TOKENS, "dw_xo"), _dw(hx, dq_x, 1024, 1024, DW_TOKENS, "dw_xq"), d_xkv,
        _dw(merged, dx1, 1024, 1024, DW_TOKENS, "dw_out"), _dw(d_a, y_att, 1024, 512, DW_TOKENS, "dw_branch_att"),
        _dw(d_b, y_sg, 1024, 512, DW_TOKENS, "dw_branch_sg"))]
    duv, d_sgw, d_bx, d_lng, d_lnb, *got = _sgu_bwd(uv, dy_sg, lng, lnb, small["sg_w"], b_exp,
                                                    plans=[_chips_plan(ffn_chip), _sibling_plan(mid_mine)])
    ffn_all, mid_theirs = got[:2], got[2:]
    mid_chip = _pair_sums(mid_mine, mid_theirs, c_idx, "rs_pair_mid")
    dq, dk, dv, ds_sum, *mid_all = _attn_bwd(qkv, dy_att, y_att, lse, bias, plans=[_chips_plan(mid_chip)])
    d_rel = _bias_grad(ds_sum)

    grad_x, dz, dg1 = _in_bwd(dq, dk, dv, duv, dgates, x, dx1, g1, w_in_t)

    gs = {"norm_mix_g": dg1, "rel_bias": d_rel[:, :NREL], "sg_ln_g": d_lng.reshape(8, 64),
          "sg_ln_b": d_lnb.reshape(8, 64), "sg_w": d_sgw, "sg_b": d_bx.reshape(128, 8, 64)[:, :, 0].T,
          "norm_xattn_g": dg2, "norm_mem_g": dg_mem, "norm_ffn_g": dg3, "norm_final_g": dg4}
    d_in, *everyone = _dw(dz, h, 1152, 1024, DW_TOKENS, "dw_in",
                          plans=[_peers_plan([gs[n] for n in small_names] + [loss_part])])

    in_mine = [_blocks(d_in)]
    (in_theirs,) = _run_plan(_sibling_plan(in_mine), "rs_sibling")
    (in_chip,) = _pair_sums(in_mine, [in_theirs], c_idx, "rs_pair_w_in")
    send_sems, recv_sems, in_chip_thru, landing, token = _chips_start(in_chip, "rs_chips_start")
    all_parts = dict(zip(ffn_names + mid_names, list(ffn_all) + list(mid_all)))

    def adam(n, axis, parts, after):
        wmv = [args[p + n][0] for p in ("", "m_", "v_")]
        if axis == 1 and wmv[0].shape[1] % LANES != 0:
            outs = _adam(parts, *(t.T for t in wmv), ADAM_ROWS[n], "adam_" + n, transposed=False, after=after)
            return [t.T[None] for t in outs]
        tr = ADAM_ROWS[n] if axis == 1 else wmv[0].shape[0]
        return [t[None] for t in _adam(parts, *wmv, tr, "adam_" + n, transposed=(axis == 1), after=after)]

    res = {n: adam(n, axis, all_parts[n], token) for n, axis in BIG if n != "w_in"}
    in_chip, landing = _chips_wait(send_sems, recv_sems, in_chip_thru, landing, res["w_ffn_in"][1], "rs_chips_wait")
    my_chip = 2 * lax.axis_index("x") + lax.axis_index("y")
    own = lax.dynamic_slice_in_dim(in_chip, my_chip, 1, axis=0)
    res["w_in"] = adam("w_in", 1, lax.dynamic_update_slice_in_dim(landing, own, my_chip, axis=0), None)
    small_res = _small_update(
        everyone[:-1], [small[n] for n in small_names],
        [args["m_" + n].reshape(s) for n, s in SMALL], [args["v_" + n].reshape(s) for n, s in SMALL],
        everyone[-1], "adam_small")
    for i, n in enumerate(small_names):
        res[n] = [small_res[k][i].reshape(args[n].shape) for k in range(4)]
    loss = small_res[4][0, 0]

    order = ["norm_mix_g", "w_in", "rel_bias", "sg_ln_g", "sg_ln_b", "sg_w", "sg_b", "w_branch_att", "w_branch_sg",
             "w_out", "norm_xattn_g", "norm_mem_g", "w_xq", "w_xkv", "w_xo", "norm_ffn_g", "w_ffn_in", "w_ffn_out",
             "norm_final_g"]
    outs = [loss, grad_x.reshape(1, S, D)]
    for k in range(4):
        outs += [res[n][k] for n in order]
    return tuple(outs)
```

```python
import math

import jax
import jax.numpy as jnp
from jax import lax
from jax.experimental import pallas as pl
from jax.experimental.pallas import tpu as pltpu

F32 = jnp.float32
BF16 = jnp.bfloat16

D = 1024
ATT_W = 512
SG_W = 512
IN_COLS = 4608
DFF = 2816
MEM = 256
XH = 4
CHUNK = 64
BAND_KEYS = 640
ATT_R = 2048
ATT_SUBS = ATT_R // 128
START_SUBS = 8 * CHUNK // 128
ATT_ROWS = 32
ATT_ROWS_BWD = 16
DW_TOKENS = 2048
IN_CHUNK = 512
FF_CHUNK = 256
REL_CLIP = 128
NREL = 2 * REL_CLIP + 1
EPS = 1e-6
NEG = -1e30
N_DEV = 8

ADAM_LR = 0.001
ADAM_B1 = 0.9
ADAM_B2 = 0.999
ADAM_EPS = 1e-08
ADAM_WD = 0.01
ADAM_STEP = 10

LANES = 128
VMEM_LIMIT = 56 * 1024 * 1024
MESH = pl.DeviceIdType.MESH


def _cparams(n_axes):
    return pltpu.CompilerParams(dimension_semantics=("arbitrary",) * n_axes, vmem_limit_bytes=VMEM_LIMIT)


def _resident(shape):
    zeros = (0,) * len(shape)
    return pl.BlockSpec(shape, lambda *_: zeros, pipeline_mode=pl.Buffered(1))


def _rows(tm, cols, col_block=0):
    return pl.BlockSpec((tm, cols), lambda i: (i, col_block))


def _sigmoid(x):
    return pl.reciprocal(1.0 + jnp.exp(-x), approx=True)


_GELU_C = math.sqrt(2.0 / math.pi)


def _gelu(x):
    t = jnp.tanh(_GELU_C * (x + 0.044715 * (x * x * x)))
    return x * (0.5 * (1.0 + t))


def _gelu_and_grad(x):
    x2 = x * x
    t = jnp.tanh(_GELU_C * (x + 0.044715 * (x2 * x)))
    cdf = 0.5 * (1.0 + t)
    dcdf = 0.5 * (1.0 - t * t) * (_GELU_C * (1.0 + 3.0 * 0.044715 * x2))
    return x * cdf, cdf + x * dcdf


def _rstd(x):
    return lax.rsqrt(jnp.mean(x * x, axis=-1, keepdims=True) + EPS)


def _rms_bwd(dh, x, r, g):
    xh = x * r
    dxh = dh * g
    dx = r * (dxh - xh * jnp.mean(dxh * xh, axis=-1, keepdims=True))
    dg = jnp.sum(dh * xh, axis=0, keepdims=True)
    return dx, dg


def _group_sum64(x):
    r = lax.broadcasted_iota(jnp.int32, (LANES, LANES), 0) // 64
    c = lax.broadcasted_iota(jnp.int32, (LANES, LANES), 1) // 64
    same_group = (r == c).astype(BF16)

    def one(v):
        hi = v.astype(BF16)
        rest = v - hi.astype(F32)
        mid = rest.astype(BF16)
        lo = (rest - mid.astype(F32)).astype(BF16)
        return _dot(hi, same_group) + _dot(mid, same_group) + _dot(lo, same_group)

    pieces = [one(x[:, LANES * j:LANES * (j + 1)]) for j in range(x.shape[1] // LANES)]
    return pieces[0] if len(pieces) == 1 else jnp.concatenate(pieces, axis=1)


def _dot(a, b):
    return jnp.dot(a, b, preferred_element_type=F32)


def _dot_nt(a, b):
    return lax.dot_general(a, b, (((1,), (1,)), ((), ())), preferred_element_type=F32)


def _dot_tn(a, b):
    return lax.dot_general(a, b, (((0,), (0,)), ((), ())), preferred_element_type=F32)


DIAGS = 768


def _diag_onehot():
    r_idx = lax.broadcasted_iota(jnp.int32, (384, DIAGS), 0)
    t_idx = lax.broadcasted_iota(jnp.int32, (384, DIAGS), 1)
    dist = (8 * CHUNK + 127) - t_idx
    return (jnp.clip(dist, -REL_CLIP, REL_CLIP) + REL_CLIP == r_idx).astype(F32)


def _shift_rows(x, reverse):
    row = lax.broadcasted_iota(jnp.int32, x.shape, 0)
    for k in range(7):
        amt = (DIAGS - (1 << k)) if reverse else (1 << k)
        x = jnp.where(((row >> k) & 1) == 1, pltpu.roll(x, amt, 1), x)
    return x


N_TABLES = 1 + START_SUBS


def _bias_table(rel_bias_pad):
    def body(rb_ref, out_ref):
        per_diag = jnp.dot(rb_ref[...], _diag_onehot(), preferred_element_type=F32,
                           precision=lax.Precision.HIGHEST)
        a = lax.broadcasted_iota(jnp.int32, (128, BAND_KEYS), 0)
        b = lax.broadcasted_iota(jnp.int32, (128, BAND_KEYS), 1)
        band = (b // CHUNK >= a // CHUNK) & (b // CHUNK <= a // CHUNK + 8)
        for h in range(8):
            rows = jnp.broadcast_to(per_diag[h:h + 1, :], (128, DIAGS))
            table = _shift_rows(pltpu.roll(rows, DIAGS - 127, 1), reverse=False)[:, :BAND_KEYS]
            out_ref[0, h] = jnp.where(band, table, NEG)
            for s in range(START_SUBS):
                out_ref[1 + s, h] = jnp.where(band & (b >= 8 * CHUNK - 128 * s), table, NEG)

    return pl.pallas_call(
        body, name="bias_table",
        out_shape=jax.ShapeDtypeStruct((N_TABLES, 8, 128, BAND_KEYS), F32),
        in_specs=[pl.BlockSpec(memory_space=pltpu.VMEM)],
        out_specs=pl.BlockSpec(memory_space=pltpu.VMEM),
        compiler_params=pltpu.CompilerParams(vmem_limit_bytes=VMEM_LIMIT),
    )(rel_bias_pad)


def _bias_grad(ds_sum):
    def body(ds_ref, out_ref):
        sums = []
        for h in range(8):
            padded = jnp.concatenate([ds_ref[h], jnp.zeros((128, DIAGS - BAND_KEYS), F32)], axis=1)
            skewed = pltpu.roll(_shift_rows(padded, reverse=True), 127, 1)
            sums.append(jnp.sum(skewed, axis=0, keepdims=True))
        per_diag = jnp.concatenate(sums, axis=0)
        out_ref[...] = lax.dot_general(per_diag, _diag_onehot(), (((1,), (1,)), ((), ())),
                                       preferred_element_type=F32, precision=lax.Precision.HIGHEST)

    return pl.pallas_call(
        body, name="bias_grad",
        out_shape=jax.ShapeDtypeStruct((8, 384), F32),
        in_specs=[pl.BlockSpec(memory_space=pltpu.VMEM)],
        out_specs=pl.BlockSpec(memory_space=pltpu.VMEM),
    )(ds_sum)


def _norm_in(x, g1, plans=(), tm=512):
    S = x.shape[0]

    def body(x_ref, g_ref, h_ref):
        xv = x_ref[...]
        h_ref[...] = (xv * _rstd(xv) * g_ref[...]).astype(BF16)

    return _call(
        body, name="norm_in", grid=(S // tm,),
        out_shape=(jax.ShapeDtypeStruct((S, D), BF16),),
        in_specs=[_rows(tm, D), _resident((1, D))], out_specs=(_rows(tm, D),),
        operands=(x, g1), plans=plans)


def _in_proj(h, w_in_t, plans=(), tm=512):
    S = h.shape[0]

    def body(h_ref, w_ref, qkv_ref, uv_ref, gate_ref):
        h = h_ref[...]
        w = IN_CHUNK
        for c in range(IN_COLS // w):
            zc = _dot_nt(h, w_ref[w * c:w * (c + 1), :])
            start = w * c
            if start < ATT_W:
                qkv_ref[:, start:start + w] = (zc * 0.125).astype(BF16)
            elif start < 3 * ATT_W:
                qkv_ref[:, start:start + w] = zc.astype(BF16)
            elif start < 3 * ATT_W + 2 * SG_W:
                uv_ref[:, start - 3 * ATT_W:start - 3 * ATT_W + w] = zc.astype(BF16)
            else:
                gate_ref[:, start - 3 * ATT_W - 2 * SG_W:start - 3 * ATT_W - 2 * SG_W + w] = zc.astype(BF16)

    return _call(
        body, name="in_proj", grid=(S // tm,),
        out_shape=(jax.ShapeDtypeStruct((S, 3 * ATT_W), BF16), jax.ShapeDtypeStruct((S, 2 * SG_W), BF16),
                   jax.ShapeDtypeStruct((S, 2 * D), BF16)),
        in_specs=[_rows(tm, D), _resident((IN_COLS, D))],
        out_specs=(_rows(tm, 3 * ATT_W), _rows(tm, 2 * SG_W), _rows(tm, 2 * D)),
        operands=(h, w_in_t), plans=plans)


def _two_heads(a, lo):
    return jnp.concatenate([jnp.where(lo, a, 0), jnp.where(lo, 0, a)], axis=0)


def _att_specs():
    R = ATT_R
    q = pl.BlockSpec((R, LANES), lambda j, i: (i, j))
    kp = pl.BlockSpec((R, LANES), lambda j, i: (jnp.maximum(i - 1, 0), 4 + j))
    kc = pl.BlockSpec((R, LANES), lambda j, i: (i, 4 + j))
    vp = pl.BlockSpec((R, LANES), lambda j, i: (jnp.maximum(i - 1, 0), 8 + j))
    vc = pl.BlockSpec((R, LANES), lambda j, i: (i, 8 + j))
    bias = pl.BlockSpec((N_TABLES, 2, 128, BAND_KEYS), lambda j, i: (0, j, 0, 0))
    return [q, kp, kc, vp, vc, bias]


def _attn_fwd(qkv, bias, plans=()):
    S = qkv.shape[0]
    R = ATT_R

    def body(q_ref, kp_ref, kc_ref, vp_ref, vc_ref, b_ref, o_ref, lse_ref, s_ref, p_ref, l_ref):
        i = pl.program_id(1)
        lo = lax.broadcasted_iota(jnp.int32, (1, LANES), 1) < 64
        kwin = jnp.concatenate([kp_ref[R - 8 * CHUNK:R, :], kc_ref[...]], axis=0)
        vwin = jnp.concatenate([vp_ref[R - 8 * CHUNK:R, :], vc_ref[...]], axis=0)
        for sub in range(ATT_SUBS):
            q2 = q_ref[128 * sub:128 * (sub + 1), :]
            kw = kwin[128 * sub:128 * sub + BAND_KEYS]
            vw = vwin[128 * sub:128 * sub + BAND_KEYS]
            table = jnp.where(i == 0, 1 + sub, 0) if sub < START_SUBS else 0
            s_ref[...] = _dot_nt(_two_heads(q2, lo), kw)
            for hh in range(2):
                for r0 in range(0, 128, ATT_ROWS):
                    rows = slice(128 * hh + r0, 128 * hh + r0 + ATT_ROWS)
                    s = s_ref[rows, :] + b_ref[table, hh, r0:r0 + ATT_ROWS, :]
                    top = jnp.max(s, axis=-1, keepdims=True)
                    p = jnp.exp(s - top)
                    total = jnp.sum(p, axis=-1, keepdims=True)
                    p_ref[rows, :] = (p / total).astype(BF16)
                    l_ref[rows, :] = jnp.broadcast_to(top + jnp.log(total), (ATT_ROWS, LANES))
            o = _dot(jnp.concatenate([p_ref[0:128, :], p_ref[128:256, :]], axis=1), _two_heads(vw, lo))
            o_ref[128 * sub:128 * (sub + 1), :] = o.astype(BF16)
            lse_ref[128 * sub:128 * (sub + 1), :] = jnp.where(lo, l_ref[0:128, :], l_ref[128:256, :])

    blk = pl.BlockSpec((R, LANES), lambda j, i: (i, j))
    return _call(
        body, name="attn_fwd", grid=(4, S // R),
        out_shape=(jax.ShapeDtypeStruct((S, ATT_W), BF16), jax.ShapeDtypeStruct((S, ATT_W), F32)),
        in_specs=_att_specs(), out_specs=(blk, blk),
        scratch_shapes=[pltpu.VMEM((256, BAND_KEYS), F32), pltpu.VMEM((256, BAND_KEYS), BF16),
                        pltpu.VMEM((256, LANES), F32)],
        operands=(qkv, qkv, qkv, qkv, qkv, bias), plans=plans)


def _sg_mask():
    t = lax.broadcasted_iota(jnp.int32, (128, 128), 0)
    s = lax.broadcasted_iota(jnp.int32, (128, 128), 1)
    return (s // CHUNK) <= (t // CHUNK)


def _sg_layernorm(gv, lng, lnb):
    mu = _group_sum64(gv) * (1.0 / 64)
    xc = gv - mu
    var = _group_sum64(xc * xc) * (1.0 / 64)
    rstd = lax.rsqrt(var + EPS)
    vhat = xc * rstd
    return vhat * lng + lnb, vhat, rstd


def _sgu_fwd(uv, lng, lnb, sg_w, b_exp, tm=512):
    S = uv.shape[0]

    def body(uv_ref, lng_ref, lnb_ref, w_ref, b_ref, y_ref):
        lane = lax.broadcasted_iota(jnp.int32, (1, LANES), 1)
        lo = lane < 64
        mask = _sg_mask()
        gu = _gelu(uv_ref[:, 0:SG_W].astype(F32))
        vln, _, _ = _sg_layernorm(_gelu(uv_ref[:, SG_W:2 * SG_W].astype(F32)), lng_ref[...], lnb_ref[...])
        for gp in range(4):
            w0 = jnp.where(mask, w_ref[2 * gp], 0).astype(BF16)
            w1 = jnp.where(mask, w_ref[2 * gp + 1], 0).astype(BF16)
            cols = slice(128 * gp, 128 * (gp + 1))
            for n in range(tm // 128):
                rows = slice(128 * n, 128 * (n + 1))
                vl = vln[rows, cols]
                sv = (_dot(w0, jnp.where(lo, vl, 0).astype(BF16)) + _dot(w1, jnp.where(lo, 0, vl).astype(BF16))
                      + b_ref[:, cols])
                y_ref[rows, cols] = (gu[rows, cols] * sv).astype(BF16)

    return pl.pallas_call(
        body, name="sgu_fwd", grid=(S // tm,),
        out_shape=jax.ShapeDtypeStruct((S, SG_W), BF16),
        in_specs=[_rows(tm, 2 * SG_W), _resident((1, SG_W)), _resident((1, SG_W)),
                  _resident((8, 128, 128)), _resident((128, SG_W))],
        out_specs=_rows(tm, SG_W),
        compiler_params=_cparams(1),
    )(uv, lng, lnb, sg_w, b_exp)


def _merge_fwd(x, y_att, y_sg, gates, wba_t, wbs_t, w_out, tm=512):
    S = x.shape[0]

    def body(x_ref, ya_ref, ys_ref, g_ref, wba_ref, wbs_ref, wo_ref, x1_ref):
        a = _dot_nt(ya_ref[...], wba_ref[...])
        b = _dot_nt(ys_ref[...], wbs_ref[...])
        merged = _sigmoid(g_ref[:, 0:D].astype(F32)) * a + _sigmoid(g_ref[:, D:2 * D].astype(F32)) * b
        x1_ref[...] = x_ref[...] + _dot(merged.astype(BF16), wo_ref[...])

    return pl.pallas_call(
        body, name="merge_fwd", grid=(S // tm,),
        out_shape=jax.ShapeDtypeStruct((S, D), F32),
        in_specs=[_rows(tm, D), _rows(tm, ATT_W), _rows(tm, SG_W), _rows(tm, 2 * D),
                  _resident((D, ATT_W)), _resident((D, SG_W)), _resident((D, D))],
        out_specs=_rows(tm, D),
        compiler_params=_cparams(1),
    )(x, y_att, y_sg, gates, wba_t, wbs_t, w_out)


def _mem_kv(mem, g_mem, w_xkv_t):
    def body(m_ref, g_ref, w_ref, kv_ref, mn_ref):
        mv = m_ref[...]
        mn = (mv * _rstd(mv) * g_ref[...]).astype(BF16)
        mn_ref[...] = mn
        kv_ref[...] = _dot_nt(mn, w_ref[...]).astype(BF16)

    vm = pl.BlockSpec(memory_space=pltpu.VMEM)
    return pl.pallas_call(
        body, name="mem_kv",
        out_shape=(jax.ShapeDtypeStruct((MEM, 2 * D), BF16), jax.ShapeDtypeStruct((MEM, D), BF16)),
        in_specs=[vm, vm, vm], out_specs=(vm, vm),
        compiler_params=pltpu.CompilerParams(vmem_limit_bytes=VMEM_LIMIT),
    )(mem, g_mem, w_xkv_t)


def _xatt_head(qx, kv_ref, h):
    hs = slice(256 * h, 256 * (h + 1))
    s = _dot_nt(qx[:, hs], kv_ref[:, hs])
    p = jnp.exp(s - jnp.max(s, axis=-1, keepdims=True))
    return p / jnp.sum(p, axis=-1, keepdims=True)


def _xattn_fwd(x1, g2, w_xq, kv, w_xo, tm=512):
    S = x1.shape[0]

    def body(x_ref, g_ref, wq_ref, kv_ref, wo_ref, x2_ref, hx_ref, qx_ref, o_ref):
        xv = x_ref[...]
        hx = (xv * _rstd(xv) * g_ref[...]).astype(BF16)
        hx_ref[...] = hx
        qx = (_dot(hx, wq_ref[...]) * (1.0 / 16)).astype(BF16)
        qx_ref[...] = qx
        for h in range(XH):
            p = _xatt_head(qx, kv_ref, h)
            o_ref[:, 256 * h:256 * (h + 1)] = _dot(p.astype(BF16), kv_ref[:, D + 256 * h:D + 256 * (h + 1)]).astype(BF16)
        x2_ref[...] = xv + _dot(o_ref[...], wo_ref[...])

    return pl.pallas_call(
        body, name="xattn_fwd", grid=(S // tm,),
        out_shape=(jax.ShapeDtypeStruct((S, D), F32),) + (jax.ShapeDtypeStruct((S, D), BF16),) * 3,
        in_specs=[_rows(tm, D), _resident((1, D)), _resident((D, D)), _resident((MEM, 2 * D)), _resident((D, D))],
        out_specs=(_rows(tm, D),) * 4,
        compiler_params=_cparams(1),
    )(x1, g2, w_xq, kv, w_xo)


def _ffn_fwd(x2, tgt, g3, w_ffn_in_t, w_ffn_out, g4, tm=256):
    S = x2.shape[0]

    def body(x_ref, t_ref, g3_ref, wi_ref, wo_ref, g4_ref, dx3_ref, gu_ref, hf_ref, act_ref, loss_ref, dg4_ref):
        i = pl.program_id(0)
        xv = x_ref[...]
        hf = (xv * _rstd(xv) * g3_ref[...]).astype(BF16)
        hf_ref[...] = hf
        for c in range(DFF // FF_CHUNK):
            cs = slice(FF_CHUNK * c, FF_CHUNK * (c + 1))
            us = slice(DFF + FF_CHUNK * c, DFF + FF_CHUNK * (c + 1))
            gate = _dot_nt(hf, wi_ref[cs, :])
            up = _dot_nt(hf, wi_ref[us, :])
            gu_ref[:, cs] = gate.astype(BF16)
            gu_ref[:, us] = up.astype(BF16)
            act_ref[:, cs] = ((gate * _sigmoid(gate)) * up).astype(BF16)
        acc = xv + _dot(act_ref[...], wo_ref[...])
        r4 = _rstd(acc)
        g4 = g4_ref[...]
        diff = acc * r4 * g4 - t_ref[...]
        dy = diff * (1.0 / D)
        dx3, dg4 = _rms_bwd(dy, acc, r4, g4)
        dx3_ref[...] = dx3
        part = 0.5 * jnp.sum(jnp.mean(diff * diff, axis=-1, keepdims=True))

        @pl.when(i == 0)
        def _():
            loss_ref[...] = jnp.zeros_like(loss_ref)
            dg4_ref[...] = jnp.zeros_like(dg4_ref)

        loss_ref[...] += jnp.full(loss_ref.shape, part, F32)
        dg4_ref[...] += dg4

    return pl.pallas_call(
        body, name="ffn_fwd", grid=(S // tm,),
        out_shape=(jax.ShapeDtypeStruct((S, D), F32), jax.ShapeDtypeStruct((S, 2 * DFF), BF16),
                   jax.ShapeDtypeStruct((S, D), BF16), jax.ShapeDtypeStruct((S, DFF), BF16),
                   jax.ShapeDtypeStruct((8, LANES), F32), jax.ShapeDtypeStruct((1, D), F32)),
        in_specs=[_rows(tm, D), _rows(tm, D), _resident((1, D)), _resident((2 * DFF, D)), _resident((DFF, D)),
                  _resident((1, D))],
        out_specs=(_rows(tm, D), _rows(tm, 2 * DFF), _rows(tm, D), _rows(tm, DFF),
                   pl.BlockSpec((8, LANES), lambda i: (0, 0)), pl.BlockSpec((1, D), lambda i: (0, 0))),
        compiler_params=_cparams(1),
    )(x2, tgt, g3, w_ffn_in_t, w_ffn_out, g4)


def _ffn_bwd(dx3, gu, x2, g3, w_ffn_out, w_ffn_in_t, tm=256):
    S = x2.shape[0]

    def body(d_ref, gu_ref, x_ref, g3_ref, wo_ref, wit_ref, dx2_ref, dgu_ref, dg3_ref):
        i = pl.program_id(0)
        d3 = d_ref[...]
        d3b = d3.astype(BF16)
        for c in range(DFF // FF_CHUNK):
            cs = slice(FF_CHUNK * c, FF_CHUNK * (c + 1))
            us = slice(DFF + FF_CHUNK * c, DFF + FF_CHUNK * (c + 1))
            da = _dot_nt(d3b, wo_ref[cs, :])
            gate = gu_ref[:, cs].astype(F32)
            up = gu_ref[:, us].astype(F32)
            sg = _sigmoid(gate)
            dgate = (da * up * (sg * (1.0 + gate * (1.0 - sg)))).astype(BF16)
            dup = (da * (gate * sg)).astype(BF16)
            dgu_ref[:, cs] = dgate
            dgu_ref[:, us] = dup
        dhf = _dot(dgu_ref[...], wit_ref[...])
        xv = x_ref[...]
        dx, dg3 = _rms_bwd(dhf, xv, _rstd(xv), g3_ref[...])
        dx2_ref[...] = d3 + dx

        @pl.when(i == 0)
        def _():
            dg3_ref[...] = jnp.zeros_like(dg3_ref)

        dg3_ref[...] += dg3

    return pl.pallas_call(
        body, name="ffn_bwd", grid=(S // tm,),
        out_shape=(jax.ShapeDtypeStruct((S, D), F32), jax.ShapeDtypeStruct((S, 2 * DFF), BF16),
                   jax.ShapeDtypeStruct((1, D), F32)),
        in_specs=[_rows(tm, D), _rows(tm, 2 * DFF), _rows(tm, D), _resident((1, D)),
                  _resident((DFF, D)), _resident((2 * DFF, D))],
        out_specs=(_rows(tm, D), _rows(tm, 2 * DFF), pl.BlockSpec((1, D), lambda i: (0, 0))),
        compiler_params=_cparams(1),
    )(dx3, gu, x2, g3, w_ffn_out, w_ffn_in_t)


def _dw(a, b, tmm, tn, ts, name, out_dtype=BF16, plans=()):
    S, M = a.shape
    N = b.shape[1]
    ts = min(ts, S)
    nk = S // ts

    def body(a_ref, b_ref, o_ref, acc_ref):
        k = pl.program_id(2)

        @pl.when(k == 0)
        def _():
            acc_ref[...] = jnp.zeros_like(acc_ref)

        acc_ref[...] += _dot_tn(a_ref[...].astype(BF16), b_ref[...].astype(BF16))

        @pl.when(k == nk - 1)
        def _():
            o_ref[...] = acc_ref[...].astype(out_dtype)

    out = _call(
        body, name=name, grid=(M // tmm, N // tn, nk),
        out_shape=(jax.ShapeDtypeStruct((M, N), out_dtype),),
        in_specs=[pl.BlockSpec((ts, tmm), lambda m, n, k: (k, m)), pl.BlockSpec((ts, tn), lambda m, n, k: (k, n))],
        out_specs=(pl.BlockSpec((tmm, tn), lambda m, n, k: (m, n)),),
        scratch_shapes=[pltpu.VMEM((tmm, tn), F32)],
        operands=(a, b), plans=plans)
    return out if plans else out[0]


def _xattn_bwd(dx2, x1, qx, g2, w_xq, w_xo, kv, plans=(), tm=512):
    S = x1.shape[0]

    def body(d_ref, x_ref, qx_ref, g_ref, wq_ref, wo_ref, kv_ref, dx1_ref, dq_ref, dkv_ref, dg2_ref):
        i = pl.program_id(0)

        @pl.when(i == 0)
        def _():
            dkv_ref[...] = jnp.zeros_like(dkv_ref)
            dg2_ref[...] = jnp.zeros_like(dg2_ref)

        d2 = d_ref[...]
        qx = qx_ref[...]
        do = _dot_nt(d2.astype(BF16), wo_ref[...]).astype(BF16)
        for h in range(XH):
            hs = slice(256 * h, 256 * (h + 1))
            vs = slice(D + 256 * h, D + 256 * (h + 1))
            p = _xatt_head(qx, kv_ref, h)
            dp = _dot_nt(do[:, hs], kv_ref[:, vs])
            ds = (p * (dp - jnp.sum(dp * p, axis=-1, keepdims=True))).astype(BF16)
            dq_ref[:, hs] = (_dot(ds, kv_ref[:, hs]) * (1.0 / 16)).astype(BF16)
            dkv_ref[:, hs] += _dot_tn(ds, qx[:, hs])
            dkv_ref[:, vs] += _dot_tn(p.astype(BF16), do[:, hs])
        dhx = _dot_nt(dq_ref[...], wq_ref[...])
        xv = x_ref[...]
        dx, dg2 = _rms_bwd(dhx, xv, _rstd(xv), g_ref[...])
        dx1_ref[...] = d2 + dx
        dg2_ref[...] += dg2

    return _call(
        body, name="xattn_bwd", grid=(S // tm,),
        out_shape=(jax.ShapeDtypeStruct((S, D), F32), jax.ShapeDtypeStruct((S, D), BF16),
                   jax.ShapeDtypeStruct((MEM, 2 * D), F32), jax.ShapeDtypeStruct((1, D), F32)),
        in_specs=[_rows(tm, D), _rows(tm, D), _rows(tm, D), _resident((1, D)), _resident((D, D)), _resident((D, D)),
                  _resident((MEM, 2 * D))],
        out_specs=(_rows(tm, D), _rows(tm, D),
                   pl.BlockSpec((MEM, 2 * D), lambda i: (0, 0)), pl.BlockSpec((1, D), lambda i: (0, 0))),
        operands=(dx2, x1, qx, g2, w_xq, w_xo, kv), plans=plans)


def _mem_kv_bwd(dkv, mem, g_mem, mn, w_xkv_t):
    def body(dkv_ref, m_ref, g_ref, mn_ref, wt_ref, dw_ref, dg_ref):
        dkvb = dkv_ref[...].astype(BF16)
        dw_ref[...] = _dot_tn(dkvb, mn_ref[...]).astype(BF16)
        dmn = _dot(dkvb, wt_ref[...])
        mv = m_ref[...]
        dg_ref[...] = jnp.sum(dmn * (mv * _rstd(mv)), axis=0, keepdims=True)

    vm = pl.BlockSpec(memory_space=pltpu.VMEM)
    return pl.pallas_call(
        body, name="mem_kv_bwd",
        out_shape=(jax.ShapeDtypeStruct((2 * D, D), BF16), jax.ShapeDtypeStruct((1, D), F32)),
        in_specs=[vm] * 5, out_specs=(vm, vm),
        compiler_params=pltpu.CompilerParams(vmem_limit_bytes=VMEM_LIMIT),
    )(dkv, mem, g_mem, mn, w_xkv_t)


def _merge_bwd(dx1, y_att, y_sg, gates, wba_t, wbs_t, w_out, tm=512):
    S = dx1.shape[0]

    def body(d_ref, ya_ref, ys_ref, g_ref, wbat_ref, wbst_ref, wo_ref,
             mg_ref, da_ref, db_ref, dya_ref, dys_ref, dg_ref):
        dm = _dot_nt(d_ref[...].astype(BF16), wo_ref[...])
        a = _dot_nt(ya_ref[...], wbat_ref[...])
        b = _dot_nt(ys_ref[...], wbst_ref[...])
        sa = _sigmoid(g_ref[:, 0:D].astype(F32))
        sb = _sigmoid(g_ref[:, D:2 * D].astype(F32))
        mg_ref[...] = (sa * a + sb * b).astype(BF16)
        da = (dm * sa).astype(BF16)
        db = (dm * sb).astype(BF16)
        da_ref[...] = da
        db_ref[...] = db
        dg_ref[:, 0:D] = (dm * a * sa * (1.0 - sa)).astype(BF16)
        dg_ref[:, D:2 * D] = (dm * b * sb * (1.0 - sb)).astype(BF16)
        dya_ref[...] = _dot(da, wbat_ref[...]).astype(BF16)
        dys_ref[...] = _dot(db, wbst_ref[...]).astype(BF16)

    return pl.pallas_call(
        body, name="merge_bwd", grid=(S // tm,),
        out_shape=(jax.ShapeDtypeStruct((S, D), BF16), jax.ShapeDtypeStruct((S, D), BF16),
                   jax.ShapeDtypeStruct((S, D), BF16), jax.ShapeDtypeStruct((S, ATT_W), BF16),
                   jax.ShapeDtypeStruct((S, SG_W), BF16), jax.ShapeDtypeStruct((S, 2 * D), BF16)),
        in_specs=[_rows(tm, D), _rows(tm, ATT_W), _rows(tm, SG_W), _rows(tm, 2 * D),
                  _resident((D, ATT_W)), _resident((D, SG_W)), _resident((D, D))],
        out_specs=(_rows(tm, D), _rows(tm, D), _rows(tm, D), _rows(tm, ATT_W), _rows(tm, SG_W), _rows(tm, 2 * D)),
        compiler_params=_cparams(1),
    )(dx1, y_att, y_sg, gates, wba_t, wbs_t, w_out)


def _sgu_bwd(uv, dy_sg, lng, lnb, sg_w, b_exp, plans=(), tm=512):
    S = uv.shape[0]
    n_steps = S // tm

    def body(uv_ref, dy_ref, lng_ref, lnb_ref, w_ref, b_ref, duv_ref, dw_ref, dbx_ref, dlng_ref, dlnb_ref, dvln_ref):
        i = pl.program_id(0)

        @pl.when(i == 0)
        def _():
            dw_ref[...] = jnp.zeros_like(dw_ref)
            dbx_ref[...] = jnp.zeros_like(dbx_ref)
            dlng_ref[...] = jnp.zeros_like(dlng_ref)
            dlnb_ref[...] = jnp.zeros_like(dlnb_ref)

        lane = lax.broadcasted_iota(jnp.int32, (1, LANES), 1)
        lo = lane < 64
        mask = _sg_mask()
        lng = lng_ref[...]
        gu, dgelu_u = _gelu_and_grad(uv_ref[:, 0:SG_W].astype(F32))
        gv, dgelu_v = _gelu_and_grad(uv_ref[:, SG_W:2 * SG_W].astype(F32))
        vln, vhat, rstd = _sg_layernorm(gv, lng, lnb_ref[...])
        dy = dy_ref[...].astype(F32)
        dsv_all = dy * gu
        for gp in range(4):
            wf0 = jnp.where(mask, w_ref[2 * gp], 0)
            wf1 = jnp.where(mask, w_ref[2 * gp + 1], 0)
            w0 = wf0.astype(BF16)
            w1 = wf1.astype(BF16)
            cols = slice(128 * gp, 128 * (gp + 1))
            dw0 = jnp.zeros((128, 128), F32)
            dw1 = jnp.zeros((128, 128), F32)
            dbx = jnp.zeros((128, LANES), F32)
            for n in range(tm // 128):
                rows = slice(128 * n, 128 * (n + 1))
                vl = vln[rows, cols]
                vl0 = jnp.where(lo, vl, 0).astype(BF16)
                vl1 = jnp.where(lo, 0, vl).astype(BF16)
                sv = _dot(w0, vl0) + _dot(w1, vl1) + b_ref[:, cols]
                duv_ref[rows, cols] = (dy[rows, cols] * sv * dgelu_u[rows, cols]).astype(BF16)
                dsv = dsv_all[rows, cols]
                dbx = dbx + dsv
                ds0 = jnp.where(lo, dsv, 0).astype(BF16)
                ds1 = jnp.where(lo, 0, dsv).astype(BF16)
                dw0 = dw0 + _dot_nt(ds0, vl0)
                dw1 = dw1 + _dot_nt(ds1, vl1)
                dvln_ref[rows, cols] = _dot_tn(w0, ds0) + _dot_tn(w1, ds1)
            dw_ref[2 * gp] += jnp.where(mask, dw0, 0)
            dw_ref[2 * gp + 1] += jnp.where(mask, dw1, 0)
            dbx_ref[:, cols] += dbx
        dvln = dvln_ref[...]
        dlng_ref[...] += jnp.sum(dvln * vhat, axis=0, keepdims=True)
        dlnb_ref[...] += jnp.sum(dvln, axis=0, keepdims=True)
        dvh = dvln * lng
        dgv = rstd * (dvh - _group_sum64(dvh) * (1.0 / 64) - vhat * (_group_sum64(dvh * vhat) * (1.0 / 64)))
        duv_ref[:, SG_W:2 * SG_W] = (dgv * dgelu_v).astype(BF16)

        @pl.when(i == n_steps - 1)
        def _():
            dbx_ref[...] = _group_sum64(dbx_ref[...])

    return _call(
        body, name="sgu_bwd", grid=(n_steps,),
        out_shape=(jax.ShapeDtypeStruct((S, 2 * SG_W), BF16), jax.ShapeDtypeStruct((8, 128, 128), F32),
                   jax.ShapeDtypeStruct((128, SG_W), F32), jax.ShapeDtypeStruct((1, SG_W), F32),
                   jax.ShapeDtypeStruct((1, SG_W), F32)),
        in_specs=[_rows(tm, 2 * SG_W), _rows(tm, SG_W), _resident((1, SG_W)), _resident((1, SG_W)),
                  _resident((8, 128, 128)), _resident((128, SG_W))],
        out_specs=(_rows(tm, 2 * SG_W), pl.BlockSpec((8, 128, 128), lambda i: (0, 0, 0)),
                   pl.BlockSpec((128, SG_W), lambda i: (0, 0)), pl.BlockSpec((1, SG_W), lambda i: (0, 0)),
                   pl.BlockSpec((1, SG_W), lambda i: (0, 0))),
        scratch_shapes=[pltpu.VMEM((tm, SG_W), F32)],
        operands=(uv, dy_sg, lng, lnb, sg_w, b_exp), plans=plans)


def _attn_bwd(qkv, dy_att, y_att, lse, bias, plans=()):
    S = qkv.shape[0]
    R = ATT_R

    def body(q_ref, kp_ref, kc_ref, vp_ref, vc_ref, b_ref, dy_ref, y_ref, lse_ref, dq_ref, dk_ref, dv_ref, dss_ref,
             s_ref, dp_ref, pb_ref, dsb_ref):
        i = pl.program_id(1)

        @pl.when(i == 0)
        def _():
            dk_ref[...] = jnp.zeros_like(dk_ref)
            dv_ref[...] = jnp.zeros_like(dv_ref)
            dss_ref[...] = jnp.zeros_like(dss_ref)

        lo = lax.broadcasted_iota(jnp.int32, (1, LANES), 1) < 64
        kwin = jnp.concatenate([kp_ref[R - 8 * CHUNK:R, :], kc_ref[...]], axis=0)
        vwin = jnp.concatenate([vp_ref[R - 8 * CHUNK:R, :], vc_ref[...]], axis=0)
        for sub in range(ATT_SUBS):
            rows = slice(128 * sub, 128 * (sub + 1))
            kw = kwin[128 * sub:128 * sub + BAND_KEYS]
            vw = vwin[128 * sub:128 * sub + BAND_KEYS]
            table = jnp.where(i == 0, 1 + sub, 0) if sub < START_SUBS else 0
            qs = _two_heads(q_ref[rows, :], lo)
            dos = _two_heads(dy_ref[rows, :], lo)
            dyy = dy_ref[rows, :].astype(F32) * y_ref[rows, :].astype(F32)
            delta = (jnp.sum(jnp.where(lo, dyy, 0.0), axis=-1, keepdims=True),
                     jnp.sum(jnp.where(lo, 0.0, dyy), axis=-1, keepdims=True))
            lse2 = lse_ref[rows, :]
            s_ref[...] = _dot_nt(qs, kw)
            dp_ref[...] = _dot_nt(dos, vw)
            for hh in range(2):
                lse = lse2[:, 64 * hh:64 * hh + 1]
                for r0 in range(0, 128, ATT_ROWS_BWD):
                    rr = slice(r0, r0 + ATT_ROWS_BWD)
                    both = slice(128 * hh + r0, 128 * hh + r0 + ATT_ROWS_BWD)
                    p = jnp.exp(s_ref[both, :] + b_ref[table, hh, rr, :] - lse[rr])
                    ds = p * (dp_ref[both, :] - delta[hh][rr])
                    dss_ref[hh, rr, :] += ds
                    pb_ref[both, :] = p.astype(BF16)
                    dsb_ref[both, :] = ds.astype(BF16)
            dq = _dot(jnp.concatenate([dsb_ref[0:128, :], dsb_ref[128:256, :]], axis=1), _two_heads(kw, lo))
            dq_ref[rows, :] = dq.astype(BF16)
            start = pl.multiple_of(i * R + 128 * sub, 128)
            dk_ref[pl.ds(start, BAND_KEYS), :] += _dot_tn(dsb_ref[...], qs)
            dv_ref[pl.ds(start, BAND_KEYS), :] += _dot_tn(pb_ref[...], dos)

    acc_spec = pl.BlockSpec((S + 8 * CHUNK, LANES), lambda j, i: (0, j))
    return _call(
        body, name="attn_bwd", grid=(4, S // R),
        out_shape=(jax.ShapeDtypeStruct((S, ATT_W), BF16), jax.ShapeDtypeStruct((S + 8 * CHUNK, ATT_W), F32),
                   jax.ShapeDtypeStruct((S + 8 * CHUNK, ATT_W), F32), jax.ShapeDtypeStruct((8, 128, BAND_KEYS), F32)),
        in_specs=_att_specs() + [pl.BlockSpec((R, LANES), lambda j, i: (i, j))] * 3,
        out_specs=(pl.BlockSpec((R, LANES), lambda j, i: (i, j)), acc_spec, acc_spec,
                   pl.BlockSpec((2, 128, BAND_KEYS), lambda j, i: (j, 0, 0))),
        scratch_shapes=[pltpu.VMEM((256, BAND_KEYS), F32), pltpu.VMEM((256, BAND_KEYS), F32),
                        pltpu.VMEM((256, BAND_KEYS), BF16), pltpu.VMEM((256, BAND_KEYS), BF16)],
        operands=(qkv, qkv, qkv, qkv, qkv, bias, dy_att, y_att, lse), plans=plans)


def _in_bwd(dq, dk, dv, duv, dgates, x, dx1, g1, w_in_t, plans=(), tm=512):
    S = x.shape[0]
    pad_blocks = (8 * CHUNK) // tm

    def body(dq_ref, dk_ref, dv_ref, duv_ref, dg_ref, x_ref, d1_ref, g_ref, wt_ref, dx_ref, dz_ref, dg1_ref):
        i = pl.program_id(0)
        dz_ref[:, 0:ATT_W] = (dq_ref[...].astype(F32) * 0.125).astype(BF16)
        dz_ref[:, ATT_W:2 * ATT_W] = dk_ref[...].astype(BF16)
        dz_ref[:, 2 * ATT_W:3 * ATT_W] = dv_ref[...].astype(BF16)
        dz_ref[:, 3 * ATT_W:3 * ATT_W + 2 * SG_W] = duv_ref[...]
        dz_ref[:, 3 * ATT_W + 2 * SG_W:IN_COLS] = dg_ref[...]
        dh = _dot(dz_ref[...], wt_ref[...])
        xv = x_ref[...]
        dx, dg1 = _rms_bwd(dh, xv, _rstd(xv), g_ref[...])
        dx_ref[...] = d1_ref[...] + dx

        @pl.when(i == 0)
        def _():
            dg1_ref[...] = jnp.zeros_like(dg1_ref)

        dg1_ref[...] += dg1

    shifted = pl.BlockSpec((tm, ATT_W), lambda i: (i + pad_blocks, 0))
    return _call(
        body, name="in_bwd", grid=(S // tm,),
        out_shape=(jax.ShapeDtypeStruct((S, D), F32), jax.ShapeDtypeStruct((S, IN_COLS), BF16),
                   jax.ShapeDtypeStruct((1, D), F32)),
        in_specs=[_rows(tm, ATT_W), shifted, shifted, _rows(tm, 2 * SG_W), _rows(tm, 2 * D), _rows(tm, D),
                  _rows(tm, D), _resident((1, D)), _resident((IN_COLS, D))],
        out_specs=(_rows(tm, D), _rows(tm, IN_COLS), pl.BlockSpec((1, D), lambda i: (0, 0))),
        operands=(dq, dk, dv, duv, dgates, x, dx1, g1, w_in_t), plans=plans)


def _adam_math(w, g, m, v):
    m = ADAM_B1 * m + (1.0 - ADAM_B1) * g
    v = ADAM_B2 * v + (1.0 - ADAM_B2) * (g * g)
    m_hat = m / (1.0 - ADAM_B1 ** ADAM_STEP)
    v_hat = v / (1.0 - ADAM_B2 ** ADAM_STEP)
    delta = -ADAM_LR * (m_hat / (jnp.sqrt(v_hat) + ADAM_EPS) + ADAM_WD * w)
    return delta, m, v


def _adam(parts, w, m, v, tr, name, transposed, after=None):
    P = parts.shape[0]
    R, C = w.shape

    def body(p_ref, w_ref, m_ref, v_ref, *rest):
        g_ref, d_ref, mo_ref, vo_ref = rest[-4:]
        if transposed:
            eye = (lax.broadcasted_iota(jnp.int32, (C, C), 0) == lax.broadcasted_iota(jnp.int32, (C, C), 1)).astype(BF16)
            part = lambda k: _dot_tn(p_ref[k], eye)
        else:
            part = lambda k: p_ref[k].astype(F32)
        g = part(0)
        for k in range(1, P):
            g = g + part(k)
        g_ref[...] = g
        d_ref[...], mo_ref[...], vo_ref[...] = _adam_math(w_ref[...], g, m_ref[...], v_ref[...])

    row = pl.BlockSpec((tr, C), lambda i: (i, 0))
    p_spec = pl.BlockSpec((P, C, tr), lambda i: (0, 0, i)) if transposed else pl.BlockSpec((P, tr, C), lambda i: (0, i, 0))
    extra = [] if after is None else [after]
    return pl.pallas_call(
        body, name=name, grid=(R // tr,),
        out_shape=tuple(jax.ShapeDtypeStruct((R, C), F32) for _ in range(4)),
        in_specs=[p_spec, row, row, row] + [pl.BlockSpec(memory_space=pl.ANY)] * len(extra),
        out_specs=(row, row, row, row),
        compiler_params=_cparams(1),
    )(parts, w, m, v, *extra)


def _my_place():
    return lax.axis_index("x"), lax.axis_index("y"), lax.axis_index("c")


def _other_chips(x, y):
    return [(1 - x, y), (x, 1 - y), (1 - x, 1 - y)]


class _Plan:
    def __init__(self, arrays, out_shapes, sems, start, finish, forward=None, forward_at=0.7):
        self.arrays, self.out_shapes, self.sems = list(arrays), list(out_shapes), list(sems)
        self.start, self.finish, self.forward, self.forward_at = start, finish, forward, forward_at


def _call(body, *, name, grid, in_specs, out_specs, out_shape, operands, scratch_shapes=(), plans=()):
    n_in, n_out, n_scr = len(operands), len(out_shape), len(scratch_shapes)
    p_in = [a for p in plans for a in p.arrays]
    p_out = [s for p in plans for s in p.out_shapes]
    p_sem = [s for p in plans for s in p.sems]
    steps = math.prod(grid)

    def wrapped(*refs):
        ins, refs = refs[:n_in], refs[n_in:]
        pins, refs = refs[:len(p_in)], refs[len(p_in):]
        outs, refs = refs[:n_out], refs[n_out:]
        pouts, refs = refs[:len(p_out)], refs[len(p_out):]
        scr, psems = refs[:n_scr], refs[n_scr:]
        step = 0
        for axis, size in enumerate(grid):
            step = step * size + pl.program_id(axis)
        bound = []
        for p in plans:
            bound.append((p, pins[:len(p.arrays)], pouts[:len(p.out_shapes)], psems[:len(p.sems)]))
            pins, pouts, psems = pins[len(p.arrays):], pouts[len(p.out_shapes):], psems[len(p.sems):]

        @pl.when(step == 0)
        def _():
            for p, a, b, s in bound:
                p.start(a, b, s)

        body(*ins, *outs, *scr)

        for p, a, b, s in bound:
            if p.forward is not None:
                @pl.when(step == min(int(p.forward_at * steps), steps - 1))
                def _(p=p, a=a, b=b, s=s):
                    p.forward(a, b, s)

        @pl.when(step == steps - 1)
        def _():
            for p, a, b, s in bound:
                p.finish(a, b, s)

    hbm = pl.BlockSpec(memory_space=pl.ANY)
    return pl.pallas_call(
        wrapped if plans else body, name=name, grid=grid,
        out_shape=tuple(out_shape) + tuple(p_out),
        in_specs=list(in_specs) + [hbm] * len(p_in),
        out_specs=tuple(out_specs) + tuple([hbm] * len(p_out)),
        scratch_shapes=list(scratch_shapes) + p_sem,
        compiler_params=_cparams(len(grid)),
    )(*operands, *p_in)


def _run_plan(plan, name):
    n_in, n_out = len(plan.arrays), len(plan.out_shapes)

    def body(*refs):
        a, b, s = refs[:n_in], refs[n_in:n_in + n_out], refs[n_in + n_out:]
        plan.start(a, b, s)
        if plan.forward is not None:
            plan.forward(a, b, s)
        plan.finish(a, b, s)

    hbm = pl.BlockSpec(memory_space=pl.ANY)
    return pl.pallas_call(
        body, name=name, out_shape=tuple(plan.out_shapes),
        in_specs=[hbm] * n_in, out_specs=tuple([hbm] * n_out), scratch_shapes=plan.sems,
    )(*plan.arrays)


def _gather_plan(shards, forward_at=0.7):
    n = len(shards)

    def copies(ins, outs, sems):
        send_sems, recv_sems, local_sems = sems
        x, y, c = _my_place()
        me, sibling = (x, y, c), (x, y, 1 - c)
        chips = _other_chips(x, y)

        def idx(p):
            return 4 * p[0] + 2 * p[1] + p[2]

        def copy(a, k, block, to, src=None):
            return pltpu.make_async_remote_copy(
                src_ref=outs[a].at[idx(block)] if src is None else src, dst_ref=outs[a].at[idx(block)],
                send_sem=send_sems.at[a, k], recv_sem=recv_sems.at[a, k], device_id=to, device_id_type=MESH)

        mine = [pltpu.make_async_copy(ins[a], outs[a].at[idx(me)], local_sems.at[a]) for a in range(n)]
        first = []
        for a in range(n):
            first.append(copy(a, 0, me, sibling, src=ins[a]))
            first += [copy(a, 1 + j, me, (*chip, c), src=ins[a]) for j, chip in enumerate(chips)]
        arrived = [copy(a, 1 + j, (*chip, c), me) for j, chip in enumerate(chips) for a in range(n)]
        passed = [copy(a, 4 + j, (*chip, c), sibling) for j, chip in enumerate(chips) for a in range(n)]
        from_sibling = []
        for a in range(n):
            from_sibling.append(copy(a, 0, sibling, me))
            from_sibling += [copy(a, 4 + j, (*chip, 1 - c), me) for j, chip in enumerate(chips)]
        return mine, first, arrived, passed, from_sibling

    def start(ins, outs, sems):
        mine, first, _, _, _ = copies(ins, outs, sems)
        for cp in mine + first:
            cp.start()

    def forward(ins, outs, sems):
        _, _, arrived, passed, _ = copies(ins, outs, sems)
        for landed, onward in zip(arrived, passed):
            landed.wait_recv()
            onward.start()

    def finish(ins, outs, sems):
        mine, first, _, passed, from_sibling = copies(ins, outs, sems)
        for cp in from_sibling:
            cp.wait_recv()
        for cp in first + passed:
            cp.wait_send()
        for cp in mine:
            cp.wait()

    return _Plan(shards, [jax.ShapeDtypeStruct((N_DEV,) + s.shape, s.dtype) for s in shards],
                 [pltpu.SemaphoreType.DMA((n, 7)), pltpu.SemaphoreType.DMA((n, 7)), pltpu.SemaphoreType.DMA((n,))],
                 start, finish, forward, forward_at)


def _sibling_plan(scatter, whole=()):
    ns = len(scatter)
    n = ns + len(whole)

    def copies(ins, outs, sems):
        send_sems, recv_sems = sems
        x, y, c = _my_place()
        out = []
        for a in range(n):
            for k in range(4 if a < ns else 1):
                src = ins[a].at[2 * k + (1 - c)] if a < ns else ins[a]
                dst = outs[a].at[k] if a < ns else outs[a]
                out.append(pltpu.make_async_remote_copy(
                    src_ref=src, dst_ref=dst, send_sem=send_sems.at[a, k], recv_sem=recv_sems.at[a, k],
                    device_id=(x, y, 1 - c), device_id_type=MESH))
        return out

    def start(ins, outs, sems):
        for cp in copies(ins, outs, sems):
            cp.start()

    def finish(ins, outs, sems):
        for cp in copies(ins, outs, sems):
            cp.wait()

    shapes = [jax.ShapeDtypeStruct((4,) + p.shape[1:], p.dtype) for p in scatter]
    shapes += [jax.ShapeDtypeStruct(p.shape, p.dtype) for p in whole]
    return _Plan(list(scatter) + list(whole), shapes,
                 [pltpu.SemaphoreType.DMA((n, 4)), pltpu.SemaphoreType.DMA((n, 4))], start, finish)


def _pair_sums(mine, theirs, c_idx, name):
    n = len(mine)

    def body(c_ref, *refs):
        for a in range(n):
            out = refs[2 * n + a]
            out[...] = (refs[a][...].astype(F32) + refs[n + a][...].astype(F32)).astype(out.dtype)

    def block(arr):
        return (1,) + arr.shape[1:]

    grid_spec = pltpu.PrefetchScalarGridSpec(
        num_scalar_prefetch=1, grid=(4,),
        in_specs=[pl.BlockSpec(block(m), lambda k, c_ref: (2 * k + c_ref[0], 0, 0)) for m in mine]
        + [pl.BlockSpec(block(t), lambda k, c_ref: (k, 0, 0)) for t in theirs],
        out_specs=tuple(pl.BlockSpec(block(t), lambda k, c_ref: (k, 0, 0)) for t in theirs))
    return pl.pallas_call(
        body, name=name, grid_spec=grid_spec,
        out_shape=tuple(jax.ShapeDtypeStruct(t.shape, m.dtype) for m, t in zip(mine, theirs)),
        compiler_params=_cparams(1),
    )(c_idx, *mine, *theirs)


def _peers_plan(arrays):
    n = len(arrays)

    def copies(ins, outs, sems):
        send_sems, recv_sems, local_sems = sems
        x, y, c = _my_place()
        me = 4 * x + 2 * y + c
        out = [pltpu.make_async_copy(ins[a], outs[a].at[me], local_sems.at[a]) for a in range(n)]
        for a in range(n):
            for k in range(N_DEV - 1):
                bits = k + 1
                peer = (x ^ (bits >> 2), y ^ ((bits >> 1) & 1), c ^ (bits & 1))
                out.append(pltpu.make_async_remote_copy(
                    src_ref=ins[a], dst_ref=outs[a].at[me], send_sem=send_sems.at[a, k], recv_sem=recv_sems.at[a, k],
                    device_id=peer, device_id_type=MESH))
        return out

    def start(ins, outs, sems):
        for cp in copies(ins, outs, sems):
            cp.start()

    def finish(ins, outs, sems):
        for cp in copies(ins, outs, sems):
            cp.wait()

    return _Plan(list(arrays), [jax.ShapeDtypeStruct((N_DEV,) + a.shape, a.dtype) for a in arrays],
                 [pltpu.SemaphoreType.DMA((n, N_DEV - 1)), pltpu.SemaphoreType.DMA((n, N_DEV - 1)),
                  pltpu.SemaphoreType.DMA((n,))], start, finish)


def _chips_plan(scatter, whole=()):
    ns = len(scatter)
    n = ns + len(whole)

    def copies(ins, outs, sems):
        send_sems, recv_sems, local_sems = sems
        x, y, c = _my_place()
        my_chip = 2 * x + y

        def src(a, k):
            return ins[a].at[k] if a < ns else ins[a]

        local = [pltpu.make_async_copy(src(a, my_chip), outs[a].at[my_chip], local_sems.at[a]) for a in range(n)]
        remote = []
        for a in range(n):
            for j, (px, py) in enumerate(_other_chips(x, y)):
                remote.append(pltpu.make_async_remote_copy(
                    src_ref=src(a, 2 * px + py), dst_ref=outs[a].at[my_chip],
                    send_sem=send_sems.at[a, j], recv_sem=recv_sems.at[a, j],
                    device_id=(px, py, c), device_id_type=MESH))
        return local + remote

    def start(ins, outs, sems):
        for cp in copies(ins, outs, sems):
            cp.start()

    def finish(ins, outs, sems):
        for cp in copies(ins, outs, sems):
            cp.wait()

    shapes = [jax.ShapeDtypeStruct(s.shape, s.dtype) for s in scatter]
    shapes += [jax.ShapeDtypeStruct((4,) + s.shape, s.dtype) for s in whole]
    return _Plan(list(scatter) + list(whole), shapes,
                 [pltpu.SemaphoreType.DMA((n, 3)), pltpu.SemaphoreType.DMA((n, 3)), pltpu.SemaphoreType.DMA((n,))],
                 start, finish)


def _chip_copies(src_ref, land_ref, send_sems, recv_sems):
    x, y, c = _my_place()
    return [pltpu.make_async_remote_copy(
        src_ref=src_ref.at[2 * px + py], dst_ref=land_ref.at[2 * x + y], send_sem=send_sems.at[j],
        recv_sem=recv_sems.at[j], device_id=(px, py, c), device_id_type=MESH)
        for j, (px, py) in enumerate(_other_chips(x, y))]


def _chips_start(src, name):
    def body(src_ref, land_ref, send_sems, recv_sems, src_thru, land_thru, token):
        for cp in _chip_copies(src_ref, land_ref, send_sems, recv_sems):
            cp.start()
        token[...] = jnp.zeros_like(token)

    hbm = pl.BlockSpec(memory_space=pltpu.HBM)
    sem = pl.BlockSpec(memory_space=pltpu.SEMAPHORE)
    return pl.pallas_call(
        body, name=name,
        out_shape=(pltpu.SemaphoreType.DMA((3,)), pltpu.SemaphoreType.DMA((3,)), pltpu.HBM(src.shape, src.dtype),
                   pltpu.HBM(src.shape, src.dtype), jax.ShapeDtypeStruct((8, LANES), F32)),
        in_specs=(hbm, hbm), out_specs=(sem, sem, hbm, hbm, pl.BlockSpec(memory_space=pltpu.VMEM)),
        input_output_aliases={0: 2, 1: 3},
        compiler_params=pltpu.CompilerParams(has_side_effects=pltpu.SideEffectType.DATAFLOW_SIDE_EFFECTING),
    )(pltpu.with_memory_space_constraint(src, pltpu.HBM),
      pltpu.with_memory_space_constraint(jnp.zeros(src.shape, src.dtype), pltpu.HBM))


def _chips_wait(send_sems, recv_sems, src_thru, land_thru, after, name):
    def body(src_ref, land_ref, send_sems, recv_sems, after_ref, src_dead, got_ref):
        for cp in _chip_copies(src_ref, land_ref, send_sems, recv_sems):
            cp.wait_send()
            cp.wait_recv()

    hbm = pl.BlockSpec(memory_space=pltpu.HBM)
    sem = pl.BlockSpec(memory_space=pltpu.SEMAPHORE)
    return pl.pallas_call(
        body, name=name,
        out_shape=(pltpu.HBM(src_thru.shape, src_thru.dtype), pltpu.HBM(land_thru.shape, land_thru.dtype)),
        in_specs=(hbm, hbm, sem, sem, pl.BlockSpec(memory_space=pl.ANY)), out_specs=(hbm, hbm),
        input_output_aliases={0: 0, 1: 1},
        compiler_params=pltpu.CompilerParams(has_side_effects=pltpu.SideEffectType.DATAFLOW_SIDE_EFFECTING),
    )(src_thru, land_thru, send_sems, recv_sems, after)


def _small_update(parts, w, m, v, loss_parts, name):
    n = len(parts)

    def total(ref):
        acc = ref[0]
        for k in range(1, ref.shape[0]):
            acc = acc + ref[k]
        return acc

    def body(*refs):
        p_refs, w_refs, m_refs, v_refs = (refs[i * n:(i + 1) * n] for i in range(4))
        lp_ref = refs[4 * n]
        outs = refs[4 * n + 1:]
        g_refs, d_refs, mo_refs, vo_refs = (outs[i * n:(i + 1) * n] for i in range(4))
        for a in range(n):
            g = total(p_refs[a])
            g_refs[a][...] = g
            d_refs[a][...], mo_refs[a][...], vo_refs[a][...] = _adam_math(w_refs[a][...], g, m_refs[a][...],
                                                                          v_refs[a][...])
        outs[4 * n][...] = total(lp_ref)

    vm = pl.BlockSpec(memory_space=pltpu.VMEM)
    shapes = [jax.ShapeDtypeStruct(t.shape, F32) for _ in range(4) for t in w]
    shapes.append(jax.ShapeDtypeStruct(loss_parts.shape[1:], F32))
    outs = pl.pallas_call(
        body, name=name, out_shape=tuple(shapes),
        in_specs=[vm] * (4 * n + 1), out_specs=tuple([vm] * (4 * n + 1)),
        compiler_params=pltpu.CompilerParams(vmem_limit_bytes=VMEM_LIMIT),
    )(*parts, *w, *m, *v, loss_parts)
    return outs[0:n], outs[n:2 * n], outs[2 * n:3 * n], outs[3 * n:4 * n], outs[4 * n]


BIG = [("w_in", 1), ("w_branch_att", 1), ("w_branch_sg", 1), ("w_out", 0), ("w_xq", 0), ("w_xkv", 1), ("w_xo", 0),
       ("w_ffn_in", 1), ("w_ffn_out", 0)]
SMALL = [("norm_mix_g", (1, D)), ("rel_bias", (8, NREL)), ("sg_ln_g", (8, 64)), ("sg_ln_b", (8, 64)),
         ("sg_w", (8, 128, 128)), ("sg_b", (8, 128)), ("norm_xattn_g", (1, D)), ("norm_mem_g", (1, D)),
         ("norm_ffn_g", (1, D)), ("norm_final_g", (1, D))]
ADAM_ROWS = {"w_in": 192, "w_branch_att": 512, "w_branch_sg": 512, "w_xkv": 1024, "w_ffn_in": 176}


def _full(gathered):
    return gathered.reshape(N_DEV * gathered.shape[1], gathered.shape[2])


def _blocks(grad):
    return grad.reshape(N_DEV, grad.shape[0] // N_DEV, grad.shape[1])


def kernel(x, mem, norm_mix_g, w_in, rel_bias, sg_ln_g, sg_ln_b, sg_w, sg_b, w_branch_att, w_branch_sg, w_out, norm_xattn_g, norm_mem_g, w_xq, w_xkv, w_xo, norm_ffn_g, w_ffn_in, w_ffn_out, norm_final_g, loss_target, m_norm_mix_g, m_w_in, m_rel_bias, m_sg_ln_g, m_sg_ln_b, m_sg_w, m_sg_b, m_w_branch_att, m_w_branch_sg, m_w_out, m_norm_xattn_g, m_norm_mem_g, m_w_xq, m_w_xkv, m_w_xo, m_norm_ffn_g, m_w_ffn_in, m_w_ffn_out, m_norm_final_g, v_norm_mix_g, v_w_in, v_rel_bias, v_sg_ln_g, v_sg_ln_b, v_sg_w, v_sg_b, v_w_branch_att, v_w_branch_sg, v_w_out, v_norm_xattn_g, v_norm_mem_g, v_w_xq, v_w_xkv, v_w_xo, v_norm_ffn_g, v_w_ffn_in, v_w_ffn_out, v_norm_final_g):
    args = dict(locals())
    big_names = [n for n, _ in BIG]
    small_names = [n for n, _ in SMALL]
    S = x.shape[1]

    x, mem, tgt = x.reshape(S, D), mem.reshape(MEM, D), loss_target.reshape(S, D)
    small = {n: args[n].reshape(shape) for n, shape in SMALL}
    g1, g2, g3 = small["norm_mix_g"], small["norm_xattn_g"], small["norm_ffn_g"]
    g_mem, g4 = small["norm_mem_g"], small["norm_final_g"]
    lng = small["sg_ln_g"].reshape(1, SG_W)
    lnb = small["sg_ln_b"].reshape(1, SG_W)
    b_exp = jnp.broadcast_to(small["sg_b"].T[:, :, None], (128, 8, 64)).reshape(128, SG_W)
    rel_pad = jnp.pad(small["rel_bias"], ((0, 0), (0, 384 - NREL)))
    c_idx = lax.axis_index("c").astype(jnp.int32).reshape(1)

    shard = {n: (args[n][0].T if axis == 1 else args[n][0]).astype(BF16) for n, axis in BIG}
    h, w_in_gathered = _norm_in(x, g1, plans=[_gather_plan([shard["w_in"]])])
    w_in_t = _full(w_in_gathered)
    bias = _bias_table(rel_pad)
    mix_names = ["w_branch_att", "w_branch_sg", "w_out", "w_xq", "w_xkv", "w_xo"]
    qkv, uv, gates, *got = _in_proj(h, w_in_t, plans=[_gather_plan([shard[n] for n in mix_names])])
    wba_t, wbs_t, w_out_f, w_xq_f, w_xkv_t, w_xo_f = (_full(g) for g in got)
    y_att, lse, *got = _attn_fwd(qkv, bias, plans=[_gather_plan([shard["w_ffn_in"], shard["w_ffn_out"]])])
    w_ffn_in_t, w_ffn_out_f = (_full(g) for g in got)
    y_sg = _sgu_fwd(uv, lng, lnb, small["sg_w"], b_exp)
    x1 = _merge_fwd(x, y_att, y_sg, gates, wba_t, wbs_t, w_out_f)
    kv, mn = _mem_kv(mem, g_mem, w_xkv_t)
    x2, hx, qx, o_x = _xattn_fwd(x1, g2, w_xq_f, kv, w_xo_f)
    dx3, gu, hf, act, loss_part, dg4 = _ffn_fwd(x2, tgt, g3, w_ffn_in_t, w_ffn_out_f, g4)


    dx2, dgu, dg3 = _ffn_bwd(dx3, gu, x2, g3, w_ffn_out_f, w_ffn_in_t)
    ffn_names = ["w_ffn_out", "w_ffn_in"]
    ffn_mine = [_blocks(_dw(act, dx3, 1408, 1024, DW_TOKENS, "dw_ffn_out")),
                _blocks(_dw(dgu, hf, 1408, 1024, DW_TOKENS, "dw_ffn_in"))]
    dx1, dq_x, dkv, dg2, *ffn_theirs = _xattn_bwd(dx2, x1, qx, g2, w_xq_f, w_xo_f, kv,
                                                  plans=[_sibling_plan(ffn_mine)])
    ffn_chip = _pair_sums(ffn_mine, ffn_theirs, c_idx, "rs_pair_ffn")
    d_xkv, dg_mem = _mem_kv_bwd(dkv, mem, g_mem, mn, w_xkv_t)
    merged, d_a, d_b, dy_att, dy_sg, dgates = _merge_bwd(dx1, y_att, y_sg, gates, wba_t, wbs_t, w_out_f)
    mid_names = ["w_xo", "w_xq", "w_xkv", "w_out", "w_branch_att", "w_branch_sg"]
    mid_mine = [_blocks(g) for g in (
        _dw(o_x, dx2, 1024, 1024, DW_TOKENS, "dw_xo"), _dw(hx, dq_x, 1024, 1024, DW_TOKENS, "dw_xq"), d_xkv,
        _dw(merged, dx1, 1024, 1024, DW_TOKENS, "dw_out"), _dw(d_a, y_att, 1024, 512, DW_TOKENS, "dw_branch_att"),
        _dw(d_b, y_sg, 1024, 512, DW_TOKENS, "dw_branch_sg"))]
    duv, d_sgw, d_bx, d_lng, d_lnb, *got = _sgu_bwd(uv, dy_sg, lng, lnb, small["sg_w"], b_exp,
                                                    plans=[_chips_plan(ffn_chip), _sibling_plan(mid_mine)])
    ffn_all, mid_theirs = got[:2], got[2:]
    mid_chip = _pair_sums(mid_mine, mid_theirs, c_idx, "rs_pair_mid")
    dq, dk, dv, ds_sum, *mid_all = _attn_bwd(qkv, dy_att, y_att, lse, bias, plans=[_chips_plan(mid_chip)])
    d_rel = _bias_grad(ds_sum)

    grad_x, dz, dg1 = _in_bwd(dq, dk, dv, duv, dgates, x, dx1, g1, w_in_t)

    gs = {"norm_mix_g": dg1, "rel_bias": d_rel[:, :NREL], "sg_ln_g": d_lng.reshape(8, 64),
          "sg_ln_b": d_lnb.reshape(8, 64), "sg_w": d_sgw, "sg_b": d_bx.reshape(128, 8, 64)[:, :, 0].T,
          "norm_xattn_g": dg2, "norm_mem_g": dg_mem, "norm_ffn_g": dg3, "norm_final_g": dg4}
    d_in, *everyone = _dw(dz, h, 1152, 1024, DW_TOKENS, "dw_in",
                          plans=[_peers_plan([gs[n] for n in small_names] + [loss_part])])

    in_mine = [_blocks(d_in)]
    (in_theirs,) = _run_plan(_sibling_plan(in_mine), "rs_sibling")
    (in_chip,) = _pair_sums(in_mine, [in_theirs], c_idx, "rs_pair_w_in")
    send_sems, recv_sems, in_chip_thru, landing, token = _chips_start(in_chip, "rs_chips_start")
    all_parts = dict(zip(ffn_names + mid_names, list(ffn_all) + list(mid_all)))

    def adam(n, axis, parts, after):
        wmv = [args[p + n][0] for p in ("", "m_", "v_")]
        if axis == 1 and wmv[0].shape[1] % LANES != 0:
            outs = _adam(parts, *(t.T for t in wmv), ADAM_ROWS[n], "adam_" + n, transposed=False, after=after)
            return [t.T[None] for t in outs]
        tr = ADAM_ROWS[n] if axis == 1 else wmv[0].shape[0]
        return [t[None] for t in _adam(parts, *wmv, tr, "adam_" + n, transposed=(axis == 1), after=after)]

    res = {n: adam(n, axis, all_parts[n], token) for n, axis in BIG if n != "w_in"}
    in_chip, landing = _chips_wait(send_sems, recv_sems, in_chip_thru, landing, res["w_ffn_in"][1], "rs_chips_wait")
    my_chip = 2 * lax.axis_index("x") + lax.axis_index("y")
    own = lax.dynamic_slice_in_dim(in_chip, my_chip, 1, axis=0)
    res["w_in"] = adam("w_in", 1, lax.dynamic_update_slice_in_dim(landing, own, my_chip, axis=0), None)
    small_res = _small_update(
        everyone[:-1], [small[n] for n in small_names],
        [args["m_" + n].reshape(s) for n, s in SMALL], [args["v_" + n].reshape(s) for n, s in SMALL],
        everyone[-1], "adam_small")
    for i, n in enumerate(small_names):
        res[n] = [small_res[k][i].reshape(args[n].shape) for k in range(4)]
    loss = small_res[4][0, 0]

    order = ["norm_mix_g", "w_in", "rel_bias", "sg_ln_g", "sg_ln_b", "sg_w", "sg_b", "w_branch_att", "w_branch_sg",
             "w_out", "norm_xattn_g", "norm_mem_g", "w_xq", "w_xkv", "w_xo", "norm_ffn_g", "w_ffn_in", "w_ffn_out",
             "norm_final_g"]
    outs = [loss, grad_x.reshape(1, S, D)]
    for k in range(4):
        outs += [res[n][k] for n in order]
    return tuple(outs)
```

```python
import math

import jax
import jax.numpy as jnp
from jax import lax
from jax.experimental import pallas as pl
from jax.experimental.pallas import tpu as pltpu

F32 = jnp.float32
BF16 = jnp.bfloat16

D = 1024
ATT_W = 512
SG_W = 512
IN_COLS = 4608
DFF = 2816
MEM = 256
XH = 4
CHUNK = 64
BAND_KEYS = 640
ATT_R = 4096
ATT_SUBS = ATT_R // 128
START_SUBS = 8 * CHUNK // 128
ATT_ROWS = 32
ATT_ROWS_BWD = 16
DW_TOKENS = 2048
IN_CHUNK = 512
FF_CHUNK = 256
REL_CLIP = 128
NREL = 2 * REL_CLIP + 1
EPS = 1e-6
NEG = -1e30
N_DEV = 8

ADAM_LR = 0.001
ADAM_B1 = 0.9
ADAM_B2 = 0.999
ADAM_EPS = 1e-08
ADAM_WD = 0.01
ADAM_STEP = 10

LANES = 128
VMEM_LIMIT = 56 * 1024 * 1024
MESH = pl.DeviceIdType.MESH


def _cparams(n_axes):
    return pltpu.CompilerParams(dimension_semantics=("arbitrary",) * n_axes, vmem_limit_bytes=VMEM_LIMIT)


def _resident(shape):
    zeros = (0,) * len(shape)
    return pl.BlockSpec(shape, lambda *_: zeros, pipeline_mode=pl.Buffered(1))


def _rows(tm, cols, col_block=0):
    return pl.BlockSpec((tm, cols), lambda i: (i, col_block))


def _sigmoid(x):
    return pl.reciprocal(1.0 + jnp.exp(-x), approx=True)


_GELU_C = math.sqrt(2.0 / math.pi)


def _gelu(x):
    t = jnp.tanh(_GELU_C * (x + 0.044715 * (x * x * x)))
    return x * (0.5 * (1.0 + t))


def _gelu_and_grad(x):
    x2 = x * x
    t = jnp.tanh(_GELU_C * (x + 0.044715 * (x2 * x)))
    cdf = 0.5 * (1.0 + t)
    dcdf = 0.5 * (1.0 - t * t) * (_GELU_C * (1.0 + 3.0 * 0.044715 * x2))
    return x * cdf, cdf + x * dcdf


def _rstd(x):
    return lax.rsqrt(jnp.mean(x * x, axis=-1, keepdims=True) + EPS)


def _rms_bwd(dh, x, r, g):
    xh = x * r
    dxh = dh * g
    dx = r * (dxh - xh * jnp.mean(dxh * xh, axis=-1, keepdims=True))
    dg = jnp.sum(dh * xh, axis=0, keepdims=True)
    return dx, dg


def _group_sum64(x):
    r = lax.broadcasted_iota(jnp.int32, (LANES, LANES), 0) // 64
    c = lax.broadcasted_iota(jnp.int32, (LANES, LANES), 1) // 64
    same_group = (r == c).astype(BF16)

    def one(v):
        hi = v.astype(BF16)
        rest = v - hi.astype(F32)
        mid = rest.astype(BF16)
        lo = (rest - mid.astype(F32)).astype(BF16)
        return _dot(hi, same_group) + _dot(mid, same_group) + _dot(lo, same_group)

    pieces = [one(x[:, LANES * j:LANES * (j + 1)]) for j in range(x.shape[1] // LANES)]
    return pieces[0] if len(pieces) == 1 else jnp.concatenate(pieces, axis=1)


def _dot(a, b):
    return jnp.dot(a, b, preferred_element_type=F32)


def _dot_nt(a, b):
    return lax.dot_general(a, b, (((1,), (1,)), ((), ())), preferred_element_type=F32)


def _dot_tn(a, b):
    return lax.dot_general(a, b, (((0,), (0,)), ((), ())), preferred_element_type=F32)


DIAGS = 768


def _diag_onehot():
    r_idx = lax.broadcasted_iota(jnp.int32, (384, DIAGS), 0)
    t_idx = lax.broadcasted_iota(jnp.int32, (384, DIAGS), 1)
    dist = (8 * CHUNK + 127) - t_idx
    return (jnp.clip(dist, -REL_CLIP, REL_CLIP) + REL_CLIP == r_idx).astype(F32)


def _shift_rows(x, reverse):
    row = lax.broadcasted_iota(jnp.int32, x.shape, 0)
    for k in range(7):
        amt = (DIAGS - (1 << k)) if reverse else (1 << k)
        x = jnp.where(((row >> k) & 1) == 1, pltpu.roll(x, amt, 1), x)
    return x


N_TABLES = 1 + START_SUBS


def _bias_table(rel_bias_pad):
    def body(rb_ref, out_ref):
        per_diag = jnp.dot(rb_ref[...], _diag_onehot(), preferred_element_type=F32,
                           precision=lax.Precision.HIGHEST)
        a = lax.broadcasted_iota(jnp.int32, (128, BAND_KEYS), 0)
        b = lax.broadcasted_iota(jnp.int32, (128, BAND_KEYS), 1)
        band = (b // CHUNK >= a // CHUNK) & (b // CHUNK <= a // CHUNK + 8)
        for h in range(8):
            rows = jnp.broadcast_to(per_diag[h:h + 1, :], (128, DIAGS))
            table = _shift_rows(pltpu.roll(rows, DIAGS - 127, 1), reverse=False)[:, :BAND_KEYS]
            out_ref[0, h] = jnp.where(band, table, NEG)
            for s in range(START_SUBS):
                out_ref[1 + s, h] = jnp.where(band & (b >= 8 * CHUNK - 128 * s), table, NEG)

    return pl.pallas_call(
        body, name="bias_table",
        out_shape=jax.ShapeDtypeStruct((N_TABLES, 8, 128, BAND_KEYS), F32),
        in_specs=[pl.BlockSpec(memory_space=pltpu.VMEM)],
        out_specs=pl.BlockSpec(memory_space=pltpu.VMEM),
        compiler_params=pltpu.CompilerParams(vmem_limit_bytes=VMEM_LIMIT),
    )(rel_bias_pad)


def _bias_grad(ds_sum):
    def body(ds_ref, out_ref):
        sums = []
        for h in range(8):
            padded = jnp.concatenate([ds_ref[h], jnp.zeros((128, DIAGS - BAND_KEYS), F32)], axis=1)
            skewed = pltpu.roll(_shift_rows(padded, reverse=True), 127, 1)
            sums.append(jnp.sum(skewed, axis=0, keepdims=True))
        per_diag = jnp.concatenate(sums, axis=0)
        out_ref[...] = lax.dot_general(per_diag, _diag_onehot(), (((1,), (1,)), ((), ())),
                                       preferred_element_type=F32, precision=lax.Precision.HIGHEST)

    return pl.pallas_call(
        body, name="bias_grad",
        out_shape=jax.ShapeDtypeStruct((8, 384), F32),
        in_specs=[pl.BlockSpec(memory_space=pltpu.VMEM)],
        out_specs=pl.BlockSpec(memory_space=pltpu.VMEM),
    )(ds_sum)


def _norm_in(x, g1, plans=(), tm=512):
    S = x.shape[0]

    def body(x_ref, g_ref, h_ref):
        xv = x_ref[...]
        h_ref[...] = (xv * _rstd(xv) * g_ref[...]).astype(BF16)

    return _call(
        body, name="norm_in", grid=(S // tm,),
        out_shape=(jax.ShapeDtypeStruct((S, D), BF16),),
        in_specs=[_rows(tm, D), _resident((1, D))], out_specs=(_rows(tm, D),),
        operands=(x, g1), plans=plans)


def _in_proj(h, w_in_t, plans=(), tm=512):
    S = h.shape[0]

    def body(h_ref, w_ref, qkv_ref, uv_ref, gate_ref):
        h = h_ref[...]
        w = IN_CHUNK
        for c in range(IN_COLS // w):
            zc = _dot_nt(h, w_ref[w * c:w * (c + 1), :])
            start = w * c
            if start < ATT_W:
                qkv_ref[:, start:start + w] = (zc * 0.125).astype(BF16)
            elif start < 3 * ATT_W:
                qkv_ref[:, start:start + w] = zc.astype(BF16)
            elif start < 3 * ATT_W + 2 * SG_W:
                uv_ref[:, start - 3 * ATT_W:start - 3 * ATT_W + w] = zc.astype(BF16)
            else:
                gate_ref[:, start - 3 * ATT_W - 2 * SG_W:start - 3 * ATT_W - 2 * SG_W + w] = zc.astype(BF16)

    return _call(
        body, name="in_proj", grid=(S // tm,),
        out_shape=(jax.ShapeDtypeStruct((S, 3 * ATT_W), BF16), jax.ShapeDtypeStruct((S, 2 * SG_W), BF16),
                   jax.ShapeDtypeStruct((S, 2 * D), BF16)),
        in_specs=[_rows(tm, D), _resident((IN_COLS, D))],
        out_specs=(_rows(tm, 3 * ATT_W), _rows(tm, 2 * SG_W), _rows(tm, 2 * D)),
        operands=(h, w_in_t), plans=plans)


def _two_heads(a, lo):
    return jnp.concatenate([jnp.where(lo, a, 0), jnp.where(lo, 0, a)], axis=0)


def _att_specs():
    R = ATT_R
    q = pl.BlockSpec((R, LANES), lambda j, i: (i, j))
    kp = pl.BlockSpec((R, LANES), lambda j, i: (jnp.maximum(i - 1, 0), 4 + j))
    kc = pl.BlockSpec((R, LANES), lambda j, i: (i, 4 + j))
    vp = pl.BlockSpec((R, LANES), lambda j, i: (jnp.maximum(i - 1, 0), 8 + j))
    vc = pl.BlockSpec((R, LANES), lambda j, i: (i, 8 + j))
    bias = pl.BlockSpec((N_TABLES, 2, 128, BAND_KEYS), lambda j, i: (0, j, 0, 0))
    return [q, kp, kc, vp, vc, bias]


def _attn_fwd(qkv, bias, plans=()):
    S = qkv.shape[0]
    R = ATT_R

    def body(q_ref, kp_ref, kc_ref, vp_ref, vc_ref, b_ref, o_ref, lse_ref, s_ref, p_ref, l_ref):
        i = pl.program_id(1)
        lo = lax.broadcasted_iota(jnp.int32, (1, LANES), 1) < 64
        kwin = jnp.concatenate([kp_ref[R - 8 * CHUNK:R, :], kc_ref[...]], axis=0)
        vwin = jnp.concatenate([vp_ref[R - 8 * CHUNK:R, :], vc_ref[...]], axis=0)
        for sub in range(ATT_SUBS):
            q2 = q_ref[128 * sub:128 * (sub + 1), :]
            kw = kwin[128 * sub:128 * sub + BAND_KEYS]
            vw = vwin[128 * sub:128 * sub + BAND_KEYS]
            table = jnp.where(i == 0, 1 + sub, 0) if sub < START_SUBS else 0
            s_ref[...] = _dot_nt(_two_heads(q2, lo), kw)
            for hh in range(2):
                for r0 in range(0, 128, ATT_ROWS):
                    rows = slice(128 * hh + r0, 128 * hh + r0 + ATT_ROWS)
                    s = s_ref[rows, :] + b_ref[table, hh, r0:r0 + ATT_ROWS, :]
                    top = jnp.max(s, axis=-1, keepdims=True)
                    p = jnp.exp(s - top)
                    total = jnp.sum(p, axis=-1, keepdims=True)
                    p_ref[rows, :] = (p / total).astype(BF16)
                    l_ref[rows, :] = jnp.broadcast_to(top + jnp.log(total), (ATT_ROWS, LANES))
            o = _dot(jnp.concatenate([p_ref[0:128, :], p_ref[128:256, :]], axis=1), _two_heads(vw, lo))
            o_ref[128 * sub:128 * (sub + 1), :] = o.astype(BF16)
            lse_ref[128 * sub:128 * (sub + 1), :] = jnp.where(lo, l_ref[0:128, :], l_ref[128:256, :])

    blk = pl.BlockSpec((R, LANES), lambda j, i: (i, j))
    return _call(
        body, name="attn_fwd", grid=(4, S // R),
        out_shape=(jax.ShapeDtypeStruct((S, ATT_W), BF16), jax.ShapeDtypeStruct((S, ATT_W), F32)),
        in_specs=_att_specs(), out_specs=(blk, blk),
        scratch_shapes=[pltpu.VMEM((256, BAND_KEYS), F32), pltpu.VMEM((256, BAND_KEYS), BF16),
                        pltpu.VMEM((256, LANES), F32)],
        operands=(qkv, qkv, qkv, qkv, qkv, bias), plans=plans)


def _sg_mask():
    t = lax.broadcasted_iota(jnp.int32, (128, 128), 0)
    s = lax.broadcasted_iota(jnp.int32, (128, 128), 1)
    return (s // CHUNK) <= (t // CHUNK)


def _sg_layernorm(gv, lng, lnb):
    mu = _group_sum64(gv) * (1.0 / 64)
    xc = gv - mu
    var = _group_sum64(xc * xc) * (1.0 / 64)
    rstd = lax.rsqrt(var + EPS)
    vhat = xc * rstd
    return vhat * lng + lnb, vhat, rstd


def _sgu_fwd(uv, lng, lnb, sg_w, b_exp, tm=512):
    S = uv.shape[0]

    def body(uv_ref, lng_ref, lnb_ref, w_ref, b_ref, y_ref):
        lane = lax.broadcasted_iota(jnp.int32, (1, LANES), 1)
        lo = lane < 64
        mask = _sg_mask()
        gu = _gelu(uv_ref[:, 0:SG_W].astype(F32))
        vln, _, _ = _sg_layernorm(_gelu(uv_ref[:, SG_W:2 * SG_W].astype(F32)), lng_ref[...], lnb_ref[...])
        for gp in range(4):
            w0 = jnp.where(mask, w_ref[2 * gp], 0).astype(BF16)
            w1 = jnp.where(mask, w_ref[2 * gp + 1], 0).astype(BF16)
            cols = slice(128 * gp, 128 * (gp + 1))
            for n in range(tm // 128):
                rows = slice(128 * n, 128 * (n + 1))
                vl = vln[rows, cols]
                sv = (_dot(w0, jnp.where(lo, vl, 0).astype(BF16)) + _dot(w1, jnp.where(lo, 0, vl).astype(BF16))
                      + b_ref[:, cols])
                y_ref[rows, cols] = (gu[rows, cols] * sv).astype(BF16)

    return pl.pallas_call(
        body, name="sgu_fwd", grid=(S // tm,),
        out_shape=jax.ShapeDtypeStruct((S, SG_W), BF16),
        in_specs=[_rows(tm, 2 * SG_W), _resident((1, SG_W)), _resident((1, SG_W)),
                  _resident((8, 128, 128)), _resident((128, SG_W))],
        out_specs=_rows(tm, SG_W),
        compiler_params=_cparams(1),
    )(uv, lng, lnb, sg_w, b_exp)


def _merge_fwd(x, y_att, y_sg, gates, wba_t, wbs_t, w_out, tm=512):
    S = x.shape[0]

    def body(x_ref, ya_ref, ys_ref, g_ref, wba_ref, wbs_ref, wo_ref, x1_ref):
        a = _dot_nt(ya_ref[...], wba_ref[...])
        b = _dot_nt(ys_ref[...], wbs_ref[...])
        merged = _sigmoid(g_ref[:, 0:D].astype(F32)) * a + _sigmoid(g_ref[:, D:2 * D].astype(F32)) * b
        x1_ref[...] = x_ref[...] + _dot(merged.astype(BF16), wo_ref[...])

    return pl.pallas_call(
        body, name="merge_fwd", grid=(S // tm,),
        out_shape=jax.ShapeDtypeStruct((S, D), F32),
        in_specs=[_rows(tm, D), _rows(tm, ATT_W), _rows(tm, SG_W), _rows(tm, 2 * D),
                  _resident((D, ATT_W)), _resident((D, SG_W)), _resident((D, D))],
        out_specs=_rows(tm, D),
        compiler_params=_cparams(1),
    )(x, y_att, y_sg, gates, wba_t, wbs_t, w_out)


def _mem_kv(mem, g_mem, w_xkv_t):
    def body(m_ref, g_ref, w_ref, kv_ref, mn_ref):
        mv = m_ref[...]
        mn = (mv * _rstd(mv) * g_ref[...]).astype(BF16)
        mn_ref[...] = mn
        kv_ref[...] = _dot_nt(mn, w_ref[...]).astype(BF16)

    vm = pl.BlockSpec(memory_space=pltpu.VMEM)
    return pl.pallas_call(
        body, name="mem_kv",
        out_shape=(jax.ShapeDtypeStruct((MEM, 2 * D), BF16), jax.ShapeDtypeStruct((MEM, D), BF16)),
        in_specs=[vm, vm, vm], out_specs=(vm, vm),
        compiler_params=pltpu.CompilerParams(vmem_limit_bytes=VMEM_LIMIT),
    )(mem, g_mem, w_xkv_t)


def _xatt_head(qx, kv_ref, h):
    hs = slice(256 * h, 256 * (h + 1))
    s = _dot_nt(qx[:, hs], kv_ref[:, hs])
    p = jnp.exp(s - jnp.max(s, axis=-1, keepdims=True))
    return p / jnp.sum(p, axis=-1, keepdims=True)


def _xattn_fwd(x1, g2, w_xq, kv, w_xo, tm=512):
    S = x1.shape[0]

    def body(x_ref, g_ref, wq_ref, kv_ref, wo_ref, x2_ref, hx_ref, qx_ref, o_ref):
        xv = x_ref[...]
        hx = (xv * _rstd(xv) * g_ref[...]).astype(BF16)
        hx_ref[...] = hx
        qx = (_dot(hx, wq_ref[...]) * (1.0 / 16)).astype(BF16)
        qx_ref[...] = qx
        for h in range(XH):
            p = _xatt_head(qx, kv_ref, h)
            o_ref[:, 256 * h:256 * (h + 1)] = _dot(p.astype(BF16), kv_ref[:, D + 256 * h:D + 256 * (h + 1)]).astype(BF16)
        x2_ref[...] = xv + _dot(o_ref[...], wo_ref[...])

    return pl.pallas_call(
        body, name="xattn_fwd", grid=(S // tm,),
        out_shape=(jax.ShapeDtypeStruct((S, D), F32),) + (jax.ShapeDtypeStruct((S, D), BF16),) * 3,
        in_specs=[_rows(tm, D), _resident((1, D)), _resident((D, D)), _resident((MEM, 2 * D)), _resident((D, D))],
        out_specs=(_rows(tm, D),) * 4,
        compiler_params=_cparams(1),
    )(x1, g2, w_xq, kv, w_xo)


def _ffn_fwd(x2, tgt, g3, w_ffn_in_t, w_ffn_out, g4, tm=256):
    S = x2.shape[0]

    def body(x_ref, t_ref, g3_ref, wi_ref, wo_ref, g4_ref, dx3_ref, gu_ref, hf_ref, act_ref, loss_ref, dg4_ref):
        i = pl.program_id(0)
        xv = x_ref[...]
        hf = (xv * _rstd(xv) * g3_ref[...]).astype(BF16)
        hf_ref[...] = hf
        for c in range(DFF // FF_CHUNK):
            cs = slice(FF_CHUNK * c, FF_CHUNK * (c + 1))
            us = slice(DFF + FF_CHUNK * c, DFF + FF_CHUNK * (c + 1))
            gate = _dot_nt(hf, wi_ref[cs, :])
            up = _dot_nt(hf, wi_ref[us, :])
            gu_ref[:, cs] = gate.astype(BF16)
            gu_ref[:, us] = up.astype(BF16)
            act_ref[:, cs] = ((gate * _sigmoid(gate)) * up).astype(BF16)
        acc = xv + _dot(act_ref[...], wo_ref[...])
        r4 = _rstd(acc)
        g4 = g4_ref[...]
        diff = acc * r4 * g4 - t_ref[...]
        dy = diff * (1.0 / D)
        dx3, dg4 = _rms_bwd(dy, acc, r4, g4)
        dx3_ref[...] = dx3
        part = 0.5 * jnp.sum(jnp.mean(diff * diff, axis=-1, keepdims=True))

        @pl.when(i == 0)
        def _():
            loss_ref[...] = jnp.zeros_like(loss_ref)
            dg4_ref[...] = jnp.zeros_like(dg4_ref)

        loss_ref[...] += jnp.full(loss_ref.shape, part, F32)
        dg4_ref[...] += dg4

    return pl.pallas_call(
        body, name="ffn_fwd", grid=(S // tm,),
        out_shape=(jax.ShapeDtypeStruct((S, D), F32), jax.ShapeDtypeStruct((S, 2 * DFF), BF16),
                   jax.ShapeDtypeStruct((S, D), BF16), jax.ShapeDtypeStruct((S, DFF), BF16),
                   jax.ShapeDtypeStruct((8, LANES), F32), jax.ShapeDtypeStruct((1, D), F32)),
        in_specs=[_rows(tm, D), _rows(tm, D), _resident((1, D)), _resident((2 * DFF, D)), _resident((DFF, D)),
                  _resident((1, D))],
        out_specs=(_rows(tm, D), _rows(tm, 2 * DFF), _rows(tm, D), _rows(tm, DFF),
                   pl.BlockSpec((8, LANES), lambda i: (0, 0)), pl.BlockSpec((1, D), lambda i: (0, 0))),
        compiler_params=_cparams(1),
    )(x2, tgt, g3, w_ffn_in_t, w_ffn_out, g4)


def _ffn_bwd(dx3, gu, x2, g3, w_ffn_out, w_ffn_in_t, tm=256):
    S = x2.shape[0]

    def body(d_ref, gu_ref, x_ref, g3_ref, wo_ref, wit_ref, dx2_ref, dgu_ref, dg3_ref):
        i = pl.program_id(0)
        d3 = d_ref[...]
        d3b = d3.astype(BF16)
        for c in range(DFF // FF_CHUNK):
            cs = slice(FF_CHUNK * c, FF_CHUNK * (c + 1))
            us = slice(DFF + FF_CHUNK * c, DFF + FF_CHUNK * (c + 1))
            da = _dot_nt(d3b, wo_ref[cs, :])
            gate = gu_ref[:, cs].astype(F32)
            up = gu_ref[:, us].astype(F32)
            sg = _sigmoid(gate)
            dgate = (da * up * (sg * (1.0 + gate * (1.0 - sg)))).astype(BF16)
            dup = (da * (gate * sg)).astype(BF16)
            dgu_ref[:, cs] = dgate
            dgu_ref[:, us] = dup
        dhf = _dot(dgu_ref[...], wit_ref[...])
        xv = x_ref[...]
        dx, dg3 = _rms_bwd(dhf, xv, _rstd(xv), g3_ref[...])
        dx2_ref[...] = d3 + dx

        @pl.when(i == 0)
        def _():
            dg3_ref[...] = jnp.zeros_like(dg3_ref)

        dg3_ref[...] += dg3

    return pl.pallas_call(
        body, name="ffn_bwd", grid=(S // tm,),
        out_shape=(jax.ShapeDtypeStruct((S, D), F32), jax.ShapeDtypeStruct((S, 2 * DFF), BF16),
                   jax.ShapeDtypeStruct((1, D), F32)),
        in_specs=[_rows(tm, D), _rows(tm, 2 * DFF), _rows(tm, D), _resident((1, D)),
                  _resident((DFF, D)), _resident((2 * DFF, D))],
        out_specs=(_rows(tm, D), _rows(tm, 2 * DFF), pl.BlockSpec((1, D), lambda i: (0, 0))),
        compiler_params=_cparams(1),
    )(dx3, gu, x2, g3, w_ffn_out, w_ffn_in_t)


def _dw(a, b, tmm, tn, ts, name, out_dtype=BF16, plans=()):
    S, M = a.shape
    N = b.shape[1]
    ts = min(ts, S)
    nk = S // ts

    def body(a_ref, b_ref, o_ref, acc_ref):
        k = pl.program_id(2)

        @pl.when(k == 0)
        def _():
            acc_ref[...] = jnp.zeros_like(acc_ref)

        acc_ref[...] += _dot_tn(a_ref[...].astype(BF16), b_ref[...].astype(BF16))

        @pl.when(k == nk - 1)
        def _():
            o_ref[...] = acc_ref[...].astype(out_dtype)

    out = _call(
        body, name=name, grid=(M // tmm, N // tn, nk),
        out_shape=(jax.ShapeDtypeStruct((M, N), out_dtype),),
        in_specs=[pl.BlockSpec((ts, tmm), lambda m, n, k: (k, m)), pl.BlockSpec((ts, tn), lambda m, n, k: (k, n))],
        out_specs=(pl.BlockSpec((tmm, tn), lambda m, n, k: (m, n)),),
        scratch_shapes=[pltpu.VMEM((tmm, tn), F32)],
        operands=(a, b), plans=plans)
    return out if plans else out[0]


def _xattn_bwd(dx2, x1, qx, g2, w_xq, w_xo, kv, plans=(), tm=512):
    S = x1.shape[0]

    def body(d_ref, x_ref, qx_ref, g_ref, wq_ref, wo_ref, kv_ref, dx1_ref, dq_ref, dkv_ref, dg2_ref):
        i = pl.program_id(0)

        @pl.when(i == 0)
        def _():
            dkv_ref[...] = jnp.zeros_like(dkv_ref)
            dg2_ref[...] = jnp.zeros_like(dg2_ref)

        d2 = d_ref[...]
        qx = qx_ref[...]
        do = _dot_nt(d2.astype(BF16), wo_ref[...]).astype(BF16)
        for h in range(XH):
            hs = slice(256 * h, 256 * (h + 1))
            vs = slice(D + 256 * h, D + 256 * (h + 1))
            p = _xatt_head(qx, kv_ref, h)
            dp = _dot_nt(do[:, hs], kv_ref[:, vs])
            ds = (p * (dp - jnp.sum(dp * p, axis=-1, keepdims=True))).astype(BF16)
            dq_ref[:, hs] = (_dot(ds, kv_ref[:, hs]) * (1.0 / 16)).astype(BF16)
            dkv_ref[:, hs] += _dot_tn(ds, qx[:, hs])
            dkv_ref[:, vs] += _dot_tn(p.astype(BF16), do[:, hs])
        dhx = _dot_nt(dq_ref[...], wq_ref[...])
        xv = x_ref[...]
        dx, dg2 = _rms_bwd(dhx, xv, _rstd(xv), g_ref[...])
        dx1_ref[...] = d2 + dx
        dg2_ref[...] += dg2

    return _call(
        body, name="xattn_bwd", grid=(S // tm,),
        out_shape=(jax.ShapeDtypeStruct((S, D), F32), jax.ShapeDtypeStruct((S, D), BF16),
                   jax.ShapeDtypeStruct((MEM, 2 * D), F32), jax.ShapeDtypeStruct((1, D), F32)),
        in_specs=[_rows(tm, D), _rows(tm, D), _rows(tm, D), _resident((1, D)), _resident((D, D)), _resident((D, D)),
                  _resident((MEM, 2 * D))],
        out_specs=(_rows(tm, D), _rows(tm, D),
                   pl.BlockSpec((MEM, 2 * D), lambda i: (0, 0)), pl.BlockSpec((1, D), lambda i: (0, 0))),
        operands=(dx2, x1, qx, g2, w_xq, w_xo, kv), plans=plans)


def _mem_kv_bwd(dkv, mem, g_mem, mn, w_xkv_t):
    def body(dkv_ref, m_ref, g_ref, mn_ref, wt_ref, dw_ref, dg_ref):
        dkvb = dkv_ref[...].astype(BF16)
        dw_ref[...] = _dot_tn(dkvb, mn_ref[...]).astype(BF16)
        dmn = _dot(dkvb, wt_ref[...])
        mv = m_ref[...]
        dg_ref[...] = jnp.sum(dmn * (mv * _rstd(mv)), axis=0, keepdims=True)

    vm = pl.BlockSpec(memory_space=pltpu.VMEM)
    return pl.pallas_call(
        body, name="mem_kv_bwd",
        out_shape=(jax.ShapeDtypeStruct((2 * D, D), BF16), jax.ShapeDtypeStruct((1, D), F32)),
        in_specs=[vm] * 5, out_specs=(vm, vm),
        compiler_params=pltpu.CompilerParams(vmem_limit_bytes=VMEM_LIMIT),
    )(dkv, mem, g_mem, mn, w_xkv_t)


def _merge_bwd(dx1, y_att, y_sg, gates, wba_t, wbs_t, w_out, tm=512):
    S = dx1.shape[0]

    def body(d_ref, ya_ref, ys_ref, g_ref, wbat_ref, wbst_ref, wo_ref,
             mg_ref, da_ref, db_ref, dya_ref, dys_ref, dg_ref):
        dm = _dot_nt(d_ref[...].astype(BF16), wo_ref[...])
        a = _dot_nt(ya_ref[...], wbat_ref[...])
        b = _dot_nt(ys_ref[...], wbst_ref[...])
        sa = _sigmoid(g_ref[:, 0:D].astype(F32))
        sb = _sigmoid(g_ref[:, D:2 * D].astype(F32))
        mg_ref[...] = (sa * a + sb * b).astype(BF16)
        da = (dm * sa).astype(BF16)
        db = (dm * sb).astype(BF16)
        da_ref[...] = da
        db_ref[...] = db
        dg_ref[:, 0:D] = (dm * a * sa * (1.0 - sa)).astype(BF16)
        dg_ref[:, D:2 * D] = (dm * b * sb * (1.0 - sb)).astype(BF16)
        dya_ref[...] = _dot(da, wbat_ref[...]).astype(BF16)
        dys_ref[...] = _dot(db, wbst_ref[...]).astype(BF16)

    return pl.pallas_call(
        body, name="merge_bwd", grid=(S // tm,),
        out_shape=(jax.ShapeDtypeStruct((S, D), BF16), jax.ShapeDtypeStruct((S, D), BF16),
                   jax.ShapeDtypeStruct((S, D), BF16), jax.ShapeDtypeStruct((S, ATT_W), BF16),
                   jax.ShapeDtypeStruct((S, SG_W), BF16), jax.ShapeDtypeStruct((S, 2 * D), BF16)),
        in_specs=[_rows(tm, D), _rows(tm, ATT_W), _rows(tm, SG_W), _rows(tm, 2 * D),
                  _resident((D, ATT_W)), _resident((D, SG_W)), _resident((D, D))],
        out_specs=(_rows(tm, D), _rows(tm, D), _rows(tm, D), _rows(tm, ATT_W), _rows(tm, SG_W), _rows(tm, 2 * D)),
        compiler_params=_cparams(1),
    )(dx1, y_att, y_sg, gates, wba_t, wbs_t, w_out)


def _sgu_bwd(uv, dy_sg, lng, lnb, sg_w, b_exp, plans=(), tm=512):
    S = uv.shape[0]
    n_steps = S // tm

    def body(uv_ref, dy_ref, lng_ref, lnb_ref, w_ref, b_ref, duv_ref, dw_ref, dbx_ref, dlng_ref, dlnb_ref, dvln_ref):
        i = pl.program_id(0)

        @pl.when(i == 0)
        def _():
            dw_ref[...] = jnp.zeros_like(dw_ref)
            dbx_ref[...] = jnp.zeros_like(dbx_ref)
            dlng_ref[...] = jnp.zeros_like(dlng_ref)
            dlnb_ref[...] = jnp.zeros_like(dlnb_ref)

        lane = lax.broadcasted_iota(jnp.int32, (1, LANES), 1)
        lo = lane < 64
        mask = _sg_mask()
        lng = lng_ref[...]
        gu, dgelu_u = _gelu_and_grad(uv_ref[:, 0:SG_W].astype(F32))
        gv, dgelu_v = _gelu_and_grad(uv_ref[:, SG_W:2 * SG_W].astype(F32))
        vln, vhat, rstd = _sg_layernorm(gv, lng, lnb_ref[...])
        dy = dy_ref[...].astype(F32)
        dsv_all = dy * gu
        for gp in range(4):
            wf0 = jnp.where(mask, w_ref[2 * gp], 0)
            wf1 = jnp.where(mask, w_ref[2 * gp + 1], 0)
            w0 = wf0.astype(BF16)
            w1 = wf1.astype(BF16)
            cols = slice(128 * gp, 128 * (gp + 1))
            dw0 = jnp.zeros((128, 128), F32)
            dw1 = jnp.zeros((128, 128), F32)
            dbx = jnp.zeros((128, LANES), F32)
            for n in range(tm // 128):
                rows = slice(128 * n, 128 * (n + 1))
                vl = vln[rows, cols]
                vl0 = jnp.where(lo, vl, 0).astype(BF16)
                vl1 = jnp.where(lo, 0, vl).astype(BF16)
                sv = _dot(w0, vl0) + _dot(w1, vl1) + b_ref[:, cols]
                duv_ref[rows, cols] = (dy[rows, cols] * sv * dgelu_u[rows, cols]).astype(BF16)
                dsv = dsv_all[rows, cols]
                dbx = dbx + dsv
                ds0 = jnp.where(lo, dsv, 0).astype(BF16)
                ds1 = jnp.where(lo, 0, dsv).astype(BF16)
                dw0 = dw0 + _dot_nt(ds0, vl0)
                dw1 = dw1 + _dot_nt(ds1, vl1)
                dvln_ref[rows, cols] = _dot_tn(w0, ds0) + _dot_tn(w1, ds1)
            dw_ref[2 * gp] += jnp.where(mask, dw0, 0)
            dw_ref[2 * gp + 1] += jnp.where(mask, dw1, 0)
            dbx_ref[:, cols] += dbx
        dvln = dvln_ref[...]
        dlng_ref[...] += jnp.sum(dvln * vhat, axis=0, keepdims=True)
        dlnb_ref[...] += jnp.sum(dvln, axis=0, keepdims=True)
        dvh = dvln * lng
        dgv = rstd * (dvh - _group_sum64(dvh) * (1.0 / 64) - vhat * (_group_sum64(dvh * vhat) * (1.0 / 64)))
        duv_ref[:, SG_W:2 * SG_W] = (dgv * dgelu_v).astype(BF16)

        @pl.when(i == n_steps - 1)
        def _():
            dbx_ref[...] = _group_sum64(dbx_ref[...])

    return _call(
        body, name="sgu_bwd", grid=(n_steps,),
        out_shape=(jax.ShapeDtypeStruct((S, 2 * SG_W), BF16), jax.ShapeDtypeStruct((8, 128, 128), F32),
                   jax.ShapeDtypeStruct((128, SG_W), F32), jax.ShapeDtypeStruct((1, SG_W), F32),
                   jax.ShapeDtypeStruct((1, SG_W), F32)),
        in_specs=[_rows(tm, 2 * SG_W), _rows(tm, SG_W), _resident((1, SG_W)), _resident((1, SG_W)),
                  _resident((8, 128, 128)), _resident((128, SG_W))],
        out_specs=(_rows(tm, 2 * SG_W), pl.BlockSpec((8, 128, 128), lambda i: (0, 0, 0)),
                   pl.BlockSpec((128, SG_W), lambda i: (0, 0)), pl.BlockSpec((1, SG_W), lambda i: (0, 0)),
                   pl.BlockSpec((1, SG_W), lambda i: (0, 0))),
        scratch_shapes=[pltpu.VMEM((tm, SG_W), F32)],
        operands=(uv, dy_sg, lng, lnb, sg_w, b_exp), plans=plans)


def _attn_bwd(qkv, dy_att, y_att, lse, bias, plans=()):
    S = qkv.shape[0]
    R = ATT_R

    def body(q_ref, kp_ref, kc_ref, vp_ref, vc_ref, b_ref, dy_ref, y_ref, lse_ref, dq_ref, dk_ref, dv_ref, dss_ref,
             s_ref, dp_ref, pb_ref, dsb_ref):
        i = pl.program_id(1)

        @pl.when(i == 0)
        def _():
            dk_ref[...] = jnp.zeros_like(dk_ref)
            dv_ref[...] = jnp.zeros_like(dv_ref)
            dss_ref[...] = jnp.zeros_like(dss_ref)

        lo = lax.broadcasted_iota(jnp.int32, (1, LANES), 1) < 64
        kwin = jnp.concatenate([kp_ref[R - 8 * CHUNK:R, :], kc_ref[...]], axis=0)
        vwin = jnp.concatenate([vp_ref[R - 8 * CHUNK:R, :], vc_ref[...]], axis=0)
        for sub in range(ATT_SUBS):
            rows = slice(128 * sub, 128 * (sub + 1))
            kw = kwin[128 * sub:128 * sub + BAND_KEYS]
            vw = vwin[128 * sub:128 * sub + BAND_KEYS]
            table = jnp.where(i == 0, 1 + sub, 0) if sub < START_SUBS else 0
            qs = _two_heads(q_ref[rows, :], lo)
            dos = _two_heads(dy_ref[rows, :], lo)
            dyy = dy_ref[rows, :].astype(F32) * y_ref[rows, :].astype(F32)
            delta = (jnp.sum(jnp.where(lo, dyy, 0.0), axis=-1, keepdims=True),
                     jnp.sum(jnp.where(lo, 0.0, dyy), axis=-1, keepdims=True))
            lse2 = lse_ref[rows, :]
            s_ref[...] = _dot_nt(qs, kw)
            dp_ref[...] = _dot_nt(dos, vw)
            for hh in range(2):
                lse = lse2[:, 64 * hh:64 * hh + 1]
                for r0 in range(0, 128, ATT_ROWS_BWD):
                    rr = slice(r0, r0 + ATT_ROWS_BWD)
                    both = slice(128 * hh + r0, 128 * hh + r0 + ATT_ROWS_BWD)
                    p = jnp.exp(s_ref[both, :] + b_ref[table, hh, rr, :] - lse[rr])
                    ds = p * (dp_ref[both, :] - delta[hh][rr])
                    dss_ref[hh, rr, :] += ds
                    pb_ref[both, :] = p.astype(BF16)
                    dsb_ref[both, :] = ds.astype(BF16)
            dq = _dot(jnp.concatenate([dsb_ref[0:128, :], dsb_ref[128:256, :]], axis=1), _two_heads(kw, lo))
            dq_ref[rows, :] = dq.astype(BF16)
            start = pl.multiple_of(i * R + 128 * sub, 128)
            dk_ref[pl.ds(start, BAND_KEYS), :] += _dot_tn(dsb_ref[...], qs)
            dv_ref[pl.ds(start, BAND_KEYS), :] += _dot_tn(pb_ref[...], dos)

    acc_spec = pl.BlockSpec((S + 8 * CHUNK, LANES), lambda j, i: (0, j))
    return _call(
        body, name="attn_bwd", grid=(4, S // R),
        out_shape=(jax.ShapeDtypeStruct((S, ATT_W), BF16), jax.ShapeDtypeStruct((S + 8 * CHUNK, ATT_W), F32),
                   jax.ShapeDtypeStruct((S + 8 * CHUNK, ATT_W), F32), jax.ShapeDtypeStruct((8, 128, BAND_KEYS), F32)),
        in_specs=_att_specs() + [pl.BlockSpec((R, LANES), lambda j, i: (i, j))] * 3,
        out_specs=(pl.BlockSpec((R, LANES), lambda j, i: (i, j)), acc_spec, acc_spec,
                   pl.BlockSpec((2, 128, BAND_KEYS), lambda j, i: (j, 0, 0))),
        scratch_shapes=[pltpu.VMEM((256, BAND_KEYS), F32), pltpu.VMEM((256, BAND_KEYS), F32),
                        pltpu.VMEM((256, BAND_KEYS), BF16), pltpu.VMEM((256, BAND_KEYS), BF16)],
        operands=(qkv, qkv, qkv, qkv, qkv, bias, dy_att, y_att, lse), plans=plans)


def _in_bwd(dq, dk, dv, duv, dgates, x, dx1, g1, w_in_t, plans=(), tm=512):
    S = x.shape[0]
    pad_blocks = (8 * CHUNK) // tm

    def body(dq_ref, dk_ref, dv_ref, duv_ref, dg_ref, x_ref, d1_ref, g_ref, wt_ref, dx_ref, dz_ref, dg1_ref):
        i = pl.program_id(0)
        dz_ref[:, 0:ATT_W] = (dq_ref[...].astype(F32) * 0.125).astype(BF16)
        dz_ref[:, ATT_W:2 * ATT_W] = dk_ref[...].astype(BF16)
        dz_ref[:, 2 * ATT_W:3 * ATT_W] = dv_ref[...].astype(BF16)
        dz_ref[:, 3 * ATT_W:3 * ATT_W + 2 * SG_W] = duv_ref[...]
        dz_ref[:, 3 * ATT_W + 2 * SG_W:IN_COLS] = dg_ref[...]
        dh = _dot(dz_ref[...], wt_ref[...])
        xv = x_ref[...]
        dx, dg1 = _rms_bwd(dh, xv, _rstd(xv), g_ref[...])
        dx_ref[...] = d1_ref[...] + dx

        @pl.when(i == 0)
        def _():
            dg1_ref[...] = jnp.zeros_like(dg1_ref)

        dg1_ref[...] += dg1

    shifted = pl.BlockSpec((tm, ATT_W), lambda i: (i + pad_blocks, 0))
    return _call(
        body, name="in_bwd", grid=(S // tm,),
        out_shape=(jax.ShapeDtypeStruct((S, D), F32), jax.ShapeDtypeStruct((S, IN_COLS), BF16),
                   jax.ShapeDtypeStruct((1, D), F32)),
        in_specs=[_rows(tm, ATT_W), shifted, shifted, _rows(tm, 2 * SG_W), _rows(tm, 2 * D), _rows(tm, D),
                  _rows(tm, D), _resident((1, D)), _resident((IN_COLS, D))],
        out_specs=(_rows(tm, D), _rows(tm, IN_COLS), pl.BlockSpec((1, D), lambda i: (0, 0))),
        operands=(dq, dk, dv, duv, dgates, x, dx1, g1, w_in_t), plans=plans)


def _adam_math(w, g, m, v):
    m = ADAM_B1 * m + (1.0 - ADAM_B1) * g
    v = ADAM_B2 * v + (1.0 - ADAM_B2) * (g * g)
    m_hat = m / (1.0 - ADAM_B1 ** ADAM_STEP)
    v_hat = v / (1.0 - ADAM_B2 ** ADAM_STEP)
    delta = -ADAM_LR * (m_hat / (jnp.sqrt(v_hat) + ADAM_EPS) + ADAM_WD * w)
    return delta, m, v


def _adam(parts, w, m, v, tr, name, transposed, after=None):
    P = parts.shape[0]
    R, C = w.shape

    def body(p_ref, w_ref, m_ref, v_ref, *rest):
        g_ref, d_ref, mo_ref, vo_ref = rest[-4:]
        if transposed:
            eye = (lax.broadcasted_iota(jnp.int32, (C, C), 0) == lax.broadcasted_iota(jnp.int32, (C, C), 1)).astype(BF16)
            part = lambda k: _dot_tn(p_ref[k], eye)
        else:
            part = lambda k: p_ref[k].astype(F32)
        g = part(0)
        for k in range(1, P):
            g = g + part(k)
        g_ref[...] = g
        d_ref[...], mo_ref[...], vo_ref[...] = _adam_math(w_ref[...], g, m_ref[...], v_ref[...])

    row = pl.BlockSpec((tr, C), lambda i: (i, 0))
    p_spec = pl.BlockSpec((P, C, tr), lambda i: (0, 0, i)) if transposed else pl.BlockSpec((P, tr, C), lambda i: (0, i, 0))
    extra = [] if after is None else [after]
    return pl.pallas_call(
        body, name=name, grid=(R // tr,),
        out_shape=tuple(jax.ShapeDtypeStruct((R, C), F32) for _ in range(4)),
        in_specs=[p_spec, row, row, row] + [pl.BlockSpec(memory_space=pl.ANY)] * len(extra),
        out_specs=(row, row, row, row),
        compiler_params=_cparams(1),
    )(parts, w, m, v, *extra)


def _my_place():
    return lax.axis_index("x"), lax.axis_index("y"), lax.axis_index("c")


def _other_chips(x, y):
    return [(1 - x, y), (x, 1 - y), (1 - x, 1 - y)]


class _Plan:
    def __init__(self, arrays, out_shapes, sems, start, finish, forward=None, forward_at=0.7):
        self.arrays, self.out_shapes, self.sems = list(arrays), list(out_shapes), list(sems)
        self.start, self.finish, self.forward, self.forward_at = start, finish, forward, forward_at


def _call(body, *, name, grid, in_specs, out_specs, out_shape, operands, scratch_shapes=(), plans=()):
    n_in, n_out, n_scr = len(operands), len(out_shape), len(scratch_shapes)
    p_in = [a for p in plans for a in p.arrays]
    p_out = [s for p in plans for s in p.out_shapes]
    p_sem = [s for p in plans for s in p.sems]
    steps = math.prod(grid)

    def wrapped(*refs):
        ins, refs = refs[:n_in], refs[n_in:]
        pins, refs = refs[:len(p_in)], refs[len(p_in):]
        outs, refs = refs[:n_out], refs[n_out:]
        pouts, refs = refs[:len(p_out)], refs[len(p_out):]
        scr, psems = refs[:n_scr], refs[n_scr:]
        step = 0
        for axis, size in enumerate(grid):
            step = step * size + pl.program_id(axis)
        bound = []
        for p in plans:
            bound.append((p, pins[:len(p.arrays)], pouts[:len(p.out_shapes)], psems[:len(p.sems)]))
            pins, pouts, psems = pins[len(p.arrays):], pouts[len(p.out_shapes):], psems[len(p.sems):]

        @pl.when(step == 0)
        def _():
            for p, a, b, s in bound:
                p.start(a, b, s)

        body(*ins, *outs, *scr)

        for p, a, b, s in bound:
            if p.forward is not None:
                @pl.when(step == min(int(p.forward_at * steps), steps - 1))
                def _(p=p, a=a, b=b, s=s):
                    p.forward(a, b, s)

        @pl.when(step == steps - 1)
        def _():
            for p, a, b, s in bound:
                p.finish(a, b, s)

    hbm = pl.BlockSpec(memory_space=pl.ANY)
    return pl.pallas_call(
        wrapped if plans else body, name=name, grid=grid,
        out_shape=tuple(out_shape) + tuple(p_out),
        in_specs=list(in_specs) + [hbm] * len(p_in),
        out_specs=tuple(out_specs) + tuple([hbm] * len(p_out)),
        scratch_shapes=list(scratch_shapes) + p_sem,
        compiler_params=_cparams(len(grid)),
    )(*operands, *p_in)


def _run_plan(plan, name):
    n_in, n_out = len(plan.arrays), len(plan.out_shapes)

    def body(*refs):
        a, b, s = refs[:n_in], refs[n_in:n_in + n_out], refs[n_in + n_out:]
        plan.start(a, b, s)
        if plan.forward is not None:
            plan.forward(a, b, s)
        plan.finish(a, b, s)

    hbm = pl.BlockSpec(memory_space=pl.ANY)
    return pl.pallas_call(
        body, name=name, out_shape=tuple(plan.out_shapes),
        in_specs=[hbm] * n_in, out_specs=tuple([hbm] * n_out), scratch_shapes=plan.sems,
    )(*plan.arrays)


def _gather_plan(shards, forward_at=0.7):
    n = len(shards)

    def copies(ins, outs, sems):
        send_sems, recv_sems, local_sems = sems
        x, y, c = _my_place()
        me, sibling = (x, y, c), (x, y, 1 - c)
        chips = _other_chips(x, y)

        def idx(p):
            return 4 * p[0] + 2 * p[1] + p[2]

        def copy(a, k, block, to, src=None):
            return pltpu.make_async_remote_copy(
                src_ref=outs[a].at[idx(block)] if src is None else src, dst_ref=outs[a].at[idx(block)],
                send_sem=send_sems.at[a, k], recv_sem=recv_sems.at[a, k], device_id=to, device_id_type=MESH)

        mine = [pltpu.make_async_copy(ins[a], outs[a].at[idx(me)], local_sems.at[a]) for a in range(n)]
        first = []
        for a in range(n):
            first.append(copy(a, 0, me, sibling, src=ins[a]))
            first += [copy(a, 1 + j, me, (*chip, c), src=ins[a]) for j, chip in enumerate(chips)]
        arrived = [copy(a, 1 + j, (*chip, c), me) for j, chip in enumerate(chips) for a in range(n)]
        passed = [copy(a, 4 + j, (*chip, c), sibling) for j, chip in enumerate(chips) for a in range(n)]
        from_sibling = []
        for a in range(n):
            from_sibling.append(copy(a, 0, sibling, me))
            from_sibling += [copy(a, 4 + j, (*chip, 1 - c), me) for j, chip in enumerate(chips)]
        return mine, first, arrived, passed, from_sibling

    def start(ins, outs, sems):
        mine, first, _, _, _ = copies(ins, outs, sems)
        for cp in mine + first:
            cp.start()

    def forward(ins, outs, sems):
        _, _, arrived, passed, _ = copies(ins, outs, sems)
        for landed, onward in zip(arrived, passed):
            landed.wait_recv()
            onward.start()

    def finish(ins, outs, sems):
        mine, first, _, passed, from_sibling = copies(ins, outs, sems)
        for cp in from_sibling:
            cp.wait_recv()
        for cp in first + passed:
            cp.wait_send()
        for cp in mine:
            cp.wait()

    return _Plan(shards, [jax.ShapeDtypeStruct((N_DEV,) + s.shape, s.dtype) for s in shards],
                 [pltpu.SemaphoreType.DMA((n, 7)), pltpu.SemaphoreType.DMA((n, 7)), pltpu.SemaphoreType.DMA((n,))],
                 start, finish, forward, forward_at)


def _sibling_plan(scatter, whole=()):
    ns = len(scatter)
    n = ns + len(whole)

    def copies(ins, outs, sems):
        send_sems, recv_sems = sems
        x, y, c = _my_place()
        out = []
        for a in range(n):
            for k in range(4 if a < ns else 1):
                src = ins[a].at[2 * k + (1 - c)] if a < ns else ins[a]
                dst = outs[a].at[k] if a < ns else outs[a]
                out.append(pltpu.make_async_remote_copy(
                    src_ref=src, dst_ref=dst, send_sem=send_sems.at[a, k], recv_sem=recv_sems.at[a, k],
                    device_id=(x, y, 1 - c), device_id_type=MESH))
        return out

    def start(ins, outs, sems):
        for cp in copies(ins, outs, sems):
            cp.start()

    def finish(ins, outs, sems):
        for cp in copies(ins, outs, sems):
            cp.wait()

    shapes = [jax.ShapeDtypeStruct((4,) + p.shape[1:], p.dtype) for p in scatter]
    shapes += [jax.ShapeDtypeStruct(p.shape, p.dtype) for p in whole]
    return _Plan(list(scatter) + list(whole), shapes,
                 [pltpu.SemaphoreType.DMA((n, 4)), pltpu.SemaphoreType.DMA((n, 4))], start, finish)


def _pair_sums(mine, theirs, c_idx, name):
    n = len(mine)

    def body(c_ref, *refs):
        for a in range(n):
            out = refs[2 * n + a]
            out[...] = (refs[a][...].astype(F32) + refs[n + a][...].astype(F32)).astype(out.dtype)

    def block(arr):
        return (1,) + arr.shape[1:]

    grid_spec = pltpu.PrefetchScalarGridSpec(
        num_scalar_prefetch=1, grid=(4,),
        in_specs=[pl.BlockSpec(block(m), lambda k, c_ref: (2 * k + c_ref[0], 0, 0)) for m in mine]
        + [pl.BlockSpec(block(t), lambda k, c_ref: (k, 0, 0)) for t in theirs],
        out_specs=tuple(pl.BlockSpec(block(t), lambda k, c_ref: (k, 0, 0)) for t in theirs))
    return pl.pallas_call(
        body, name=name, grid_spec=grid_spec,
        out_shape=tuple(jax.ShapeDtypeStruct(t.shape, m.dtype) for m, t in zip(mine, theirs)),
        compiler_params=_cparams(1),
    )(c_idx, *mine, *theirs)


def _peers_plan(arrays):
    n = len(arrays)

    def copies(ins, outs, sems):
        send_sems, recv_sems, local_sems = sems
        x, y, c = _my_place()
        me = 4 * x + 2 * y + c
        out = [pltpu.make_async_copy(ins[a], outs[a].at[me], local_sems.at[a]) for a in range(n)]
        for a in range(n):
            for k in range(N_DEV - 1):
                bits = k + 1
                peer = (x ^ (bits >> 2), y ^ ((bits >> 1) & 1), c ^ (bits & 1))
                out.append(pltpu.make_async_remote_copy(
                    src_ref=ins[a], dst_ref=outs[a].at[me], send_sem=send_sems.at[a, k], recv_sem=recv_sems.at[a, k],
                    device_id=peer, device_id_type=MESH))
        return out

    def start(ins, outs, sems):
        for cp in copies(ins, outs, sems):
            cp.start()

    def finish(ins, outs, sems):
        for cp in copies(ins, outs, sems):
            cp.wait()

    return _Plan(list(arrays), [jax.ShapeDtypeStruct((N_DEV,) + a.shape, a.dtype) for a in arrays],
                 [pltpu.SemaphoreType.DMA((n, N_DEV - 1)), pltpu.SemaphoreType.DMA((n, N_DEV - 1)),
                  pltpu.SemaphoreType.DMA((n,))], start, finish)


def _chips_plan(scatter, whole=()):
    ns = len(scatter)
    n = ns + len(whole)

    def copies(ins, outs, sems):
        send_sems, recv_sems, local_sems = sems
        x, y, c = _my_place()
        my_chip = 2 * x + y

        def src(a, k):
            return ins[a].at[k] if a < ns else ins[a]

        local = [pltpu.make_async_copy(src(a, my_chip), outs[a].at[my_chip], local_sems.at[a]) for a in range(n)]
        remote = []
        for a in range(n):
            for j, (px, py) in enumerate(_other_chips(x, y)):
                remote.append(pltpu.make_async_remote_copy(
                    src_ref=src(a, 2 * px + py), dst_ref=outs[a].at[my_chip],
                    send_sem=send_sems.at[a, j], recv_sem=recv_sems.at[a, j],
                    device_id=(px, py, c), device_id_type=MESH))
        return local + remote

    def start(ins, outs, sems):
        for cp in copies(ins, outs, sems):
            cp.start()

    def finish(ins, outs, sems):
        for cp in copies(ins, outs, sems):
            cp.wait()

    shapes = [jax.ShapeDtypeStruct(s.shape, s.dtype) for s in scatter]
    shapes += [jax.ShapeDtypeStruct((4,) + s.shape, s.dtype) for s in whole]
    return _Plan(list(scatter) + list(whole), shapes,
                 [pltpu.SemaphoreType.DMA((n, 3)), pltpu.SemaphoreType.DMA((n, 3)), pltpu.SemaphoreType.DMA((n,))],
                 start, finish)


def _chip_copies(src_ref, land_ref, send_sems, recv_sems):
    x, y, c = _my_place()
    return [pltpu.make_async_remote_copy(
        src_ref=src_ref.at[2 * px + py], dst_ref=land_ref.at[2 * x + y], send_sem=send_sems.at[j],
        recv_sem=recv_sems.at[j], device_id=(px, py, c), device_id_type=MESH)
        for j, (px, py) in enumerate(_other_chips(x, y))]


def _chips_start(src, name):
    def body(src_ref, land_ref, send_sems, recv_sems, src_thru, land_thru, token):
        for cp in _chip_copies(src_ref, land_ref, send_sems, recv_sems):
            cp.start()
        token[...] = jnp.zeros_like(token)

    hbm = pl.BlockSpec(memory_space=pltpu.HBM)
    sem = pl.BlockSpec(memory_space=pltpu.SEMAPHORE)
    return pl.pallas_call(
        body, name=name,
        out_shape=(pltpu.SemaphoreType.DMA((3,)), pltpu.SemaphoreType.DMA((3,)), pltpu.HBM(src.shape, src.dtype),
                   pltpu.HBM(src.shape, src.dtype), jax.ShapeDtypeStruct((8, LANES), F32)),
        in_specs=(hbm, hbm), out_specs=(sem, sem, hbm, hbm, pl.BlockSpec(memory_space=pltpu.VMEM)),
        input_output_aliases={0: 2, 1: 3},
        compiler_params=pltpu.CompilerParams(has_side_effects=pltpu.SideEffectType.DATAFLOW_SIDE_EFFECTING),
    )(pltpu.with_memory_space_constraint(src, pltpu.HBM),
      pltpu.with_memory_space_constraint(jnp.zeros(src.shape, src.dtype), pltpu.HBM))


def _chips_wait(send_sems, recv_sems, src_thru, land_thru, after, name):
    def body(src_ref, land_ref, send_sems, recv_sems, after_ref, src_dead, got_ref):
        for cp in _chip_copies(src_ref, land_ref, send_sems, recv_sems):
            cp.wait_send()
            cp.wait_recv()

    hbm = pl.BlockSpec(memory_space=pltpu.HBM)
    sem = pl.BlockSpec(memory_space=pltpu.SEMAPHORE)
    return pl.pallas_call(
        body, name=name,
        out_shape=(pltpu.HBM(src_thru.shape, src_thru.dtype), pltpu.HBM(land_thru.shape, land_thru.dtype)),
        in_specs=(hbm, hbm, sem, sem, pl.BlockSpec(memory_space=pl.ANY)), out_specs=(hbm, hbm),
        input_output_aliases={0: 0, 1: 1},
        compiler_params=pltpu.CompilerParams(has_side_effects=pltpu.SideEffectType.DATAFLOW_SIDE_EFFECTING),
    )(src_thru, land_thru, send_sems, recv_sems, after)


def _small_update(parts, w, m, v, loss_parts, name):
    n = len(parts)

    def total(ref):
        acc = ref[0]
        for k in range(1, ref.shape[0]):
            acc = acc + ref[k]
        return acc

    def body(*refs):
        p_refs, w_refs, m_refs, v_refs = (refs[i * n:(i + 1) * n] for i in range(4))
        lp_ref = refs[4 * n]
        outs = refs[4 * n + 1:]
        g_refs, d_refs, mo_refs, vo_refs = (outs[i * n:(i + 1) * n] for i in range(4))
        for a in range(n):
            g = total(p_refs[a])
            g_refs[a][...] = g
            d_refs[a][...], mo_refs[a][...], vo_refs[a][...] = _adam_math(w_refs[a][...], g, m_refs[a][...],
                                                                          v_refs[a][...])
        outs[4 * n][...] = total(lp_ref)

    vm = pl.BlockSpec(memory_space=pltpu.VMEM)
    shapes = [jax.ShapeDtypeStruct(t.shape, F32) for _ in range(4) for t in w]
    shapes.append(jax.ShapeDtypeStruct(loss_parts.shape[1:], F32))
    outs = pl.pallas_call(
        body, name=name, out_shape=tuple(shapes),
        in_specs=[vm] * (4 * n + 1), out_specs=tuple([vm] * (4 * n + 1)),
        compiler_params=pltpu.CompilerParams(vmem_limit_bytes=VMEM_LIMIT),
    )(*parts, *w, *m, *v, loss_parts)
    return outs[0:n], outs[n:2 * n], outs[2 * n:3 * n], outs[3 * n:4 * n], outs[4 * n]


BIG = [("w_in", 1), ("w_branch_att", 1), ("w_branch_sg", 1), ("w_out", 0), ("w_xq", 0), ("w_xkv", 1), ("w_xo", 0),
       ("w_ffn_in", 1), ("w_ffn_out", 0)]
SMALL = [("norm_mix_g", (1, D)), ("rel_bias", (8, NREL)), ("sg_ln_g", (8, 64)), ("sg_ln_b", (8, 64)),
         ("sg_w", (8, 128, 128)), ("sg_b", (8, 128)), ("norm_xattn_g", (1, D)), ("norm_mem_g", (1, D)),
         ("norm_ffn_g", (1, D)), ("norm_final_g", (1, D))]
ADAM_ROWS = {"w_in": 192, "w_branch_att": 512, "w_branch_sg": 512, "w_xkv": 1024, "w_ffn_in": 176}


def _full(gathered):
    return gathered.reshape(N_DEV * gathered.shape[1], gathered.shape[2])


def _blocks(grad):
    return grad.reshape(N_DEV, grad.shape[0] // N_DEV, grad.shape[1])


def kernel(x, mem, norm_mix_g, w_in, rel_bias, sg_ln_g, sg_ln_b, sg_w, sg_b, w_branch_att, w_branch_sg, w_out, norm_xattn_g, norm_mem_g, w_xq, w_xkv, w_xo, norm_ffn_g, w_ffn_in, w_ffn_out, norm_final_g, loss_target, m_norm_mix_g, m_w_in, m_rel_bias, m_sg_ln_g, m_sg_ln_b, m_sg_w, m_sg_b, m_w_branch_att, m_w_branch_sg, m_w_out, m_norm_xattn_g, m_norm_mem_g, m_w_xq, m_w_xkv, m_w_xo, m_norm_ffn_g, m_w_ffn_in, m_w_ffn_out, m_norm_final_g, v_norm_mix_g, v_w_in, v_rel_bias, v_sg_ln_g, v_sg_ln_b, v_sg_w, v_sg_b, v_w_branch_att, v_w_branch_sg, v_w_out, v_norm_xattn_g, v_norm_mem_g, v_w_xq, v_w_xkv, v_w_xo, v_norm_ffn_g, v_w_ffn_in, v_w_ffn_out, v_norm_final_g):
    args = dict(locals())
    big_names = [n for n, _ in BIG]
    small_names = [n for n, _ in SMALL]
    S = x.shape[1]

    x, mem, tgt = x.reshape(S, D), mem.reshape(MEM, D), loss_target.reshape(S, D)
    small = {n: args[n].reshape(shape) for n, shape in SMALL}
    g1, g2, g3 = small["norm_mix_g"], small["norm_xattn_g"], small["norm_ffn_g"]
    g_mem, g4 = small["norm_mem_g"], small["norm_final_g"]
    lng = small["sg_ln_g"].reshape(1, SG_W)
    lnb = small["sg_ln_b"].reshape(1, SG_W)
    b_exp = jnp.broadcast_to(small["sg_b"].T[:, :, None], (128, 8, 64)).reshape(128, SG_W)
    rel_pad = jnp.pad(small["rel_bias"], ((0, 0), (0, 384 - NREL)))
    c_idx = lax.axis_index("c").astype(jnp.int32).reshape(1)

    shard = {n: (args[n][0].T if axis == 1 else args[n][0]).astype(BF16) for n, axis in BIG}
    h, w_in_gathered = _norm_in(x, g1, plans=[_gather_plan([shard["w_in"]])])
    w_in_t = _full(w_in_gathered)
    bias = _bias_table(rel_pad)
    mix_names = ["w_branch_att", "w_branch_sg", "w_out", "w_xq", "w_xkv", "w_xo"]
    qkv, uv, gates, *got = _in_proj(h, w_in_t, plans=[_gather_plan([shard[n] for n in mix_names])])
    wba_t, wbs_t, w_out_f, w_xq_f, w_xkv_t, w_xo_f = (_full(g) for g in got)
    y_att, lse, *got = _attn_fwd(qkv, bias, plans=[_gather_plan([shard["w_ffn_in"], shard["w_ffn_out"]])])
    w_ffn_in_t, w_ffn_out_f = (_full(g) for g in got)
    y_sg = _sgu_fwd(uv, lng, lnb, small["sg_w"], b_exp)
    x1 = _merge_fwd(x, y_att, y_sg, gates, wba_t, wbs_t, w_out_f)
    kv, mn = _mem_kv(mem, g_mem, w_xkv_t)
    x2, hx, qx, o_x = _xattn_fwd(x1, g2, w_xq_f, kv, w_xo_f)
    dx3, gu, hf, act, loss_part, dg4 = _ffn_fwd(x2, tgt, g3, w_ffn_in_t, w_ffn_out_f, g4)


    dx2, dgu, dg3 = _ffn_bwd(dx3, gu, x2, g3, w_ffn_out_f, w_ffn_in_t)
    ffn_names = ["w_ffn_out", "w_ffn_in"]
    ffn_mine = [_blocks(_dw(act, dx3, 1408, 1024, DW_TOKENS, "dw_ffn_out")),
                _blocks(_dw(dgu, hf, 1408, 1024, DW_TOKENS, "dw_ffn_in"))]
    dx1, dq_x, dkv, dg2, *ffn_theirs = _xattn_bwd(dx2, x1, qx, g2, w_xq_f, w_xo_f, kv,
                                                  plans=[_sibling_plan(ffn_mine)])
    ffn_chip = _pair_sums(ffn_mine, ffn_theirs, c_idx, "rs_pair_ffn")
    d_xkv, dg_mem = _mem_kv_bwd(dkv, mem, g_mem, mn, w_xkv_t)
    merged, d_a, d_b, dy_att, dy_sg, dgates = _merge_bwd(dx1, y_att, y_sg, gates, wba_t, wbs_t, w_out_f)
    mid_names = ["w_xo", "w_xq", "w_xkv", "w_out", "w_branch_att", "w_branch_sg"]
    mid_mine = [_blocks(g) for g in (
        _dw(o_x, dx2, 1024, 1024, DW_TOKENS, "dw_xo"), _dw(hx, dq_x, 1024, 1024, DW_TOKENS, "dw_xq"), d_xkv,
        _dw(merged, dx1, 1024, 1024, DW_TOKENS, "dw_out"), _dw(d_a, y_att, 1024, 512, DW_TOKENS, "dw_branch_att"),
        _dw(d_b, y_sg, 1024, 512, DW_TOKENS, "dw_branch_sg"))]
    duv, d_sgw, d_bx, d_lng, d_lnb, *got = _sgu_bwd(uv, dy_sg, lng, lnb, small["sg_w"], b_exp,
                                                    plans=[_chips_plan(ffn_chip), _sibling_plan(mid_mine)])
    ffn_all, mid_theirs = got[:2], got[2:]
    mid_chip = _pair_sums(mid_mine, mid_theirs, c_idx, "rs_pair_mid")
    dq, dk, dv, ds_sum, *mid_all = _attn_bwd(qkv, dy_att, y_att, lse, bias, plans=[_chips_plan(mid_chip)])
    d_rel = _bias_grad(ds_sum)

    grad_x, dz, dg1 = _in_bwd(dq, dk, dv, duv, dgates, x, dx1, g1, w_in_t)

    gs = {"norm_mix_g": dg1, "rel_bias": d_rel[:, :NREL], "sg_ln_g": d_lng.reshape(8, 64),
          "sg_ln_b": d_lnb.reshape(8, 64), "sg_w": d_sgw, "sg_b": d_bx.reshape(128, 8, 64)[:, :, 0].T,
          "norm_xattn_g": dg2, "norm_mem_g": dg_mem, "norm_ffn_g": dg3, "norm_final_g": dg4}
    d_in, *everyone = _dw(dz, h, 1152, 1024, DW_TOKENS, "dw_in",
                          plans=[_peers_plan([gs[n] for n in small_names] + [loss_part])])

    in_mine = [_blocks(d_in)]
    (in_theirs,) = _run_plan(_sibling_plan(in_mine), "rs_sibling")
    (in_chip,) = _pair_sums(in_mine, [in_theirs], c_idx, "rs_pair_w_in")
    send_sems, recv_sems, in_chip_thru, landing, token = _chips_start(in_chip, "rs_chips_start")
    all_parts = dict(zip(ffn_names + mid_names, list(ffn_all) + list(mid_all)))

    def adam(n, axis, parts, after):
        wmv = [args[p + n][0] for p in ("", "m_", "v_")]
        if axis == 1 and wmv[0].shape[1] % LANES != 0:
            outs = _adam(parts, *(t.T for t in wmv), ADAM_ROWS[n], "adam_" + n, transposed=False, after=after)
            return [t.T[None] for t in outs]
        tr = ADAM_ROWS[n] if axis == 1 else wmv[0].shape[0]
        return [t[None] for t in _adam(parts, *wmv, tr, "adam_" + n, transposed=(axis == 1), after=after)]

    res = {n: adam(n, axis, all_parts[n], token) for n, axis in BIG if n != "w_in"}
    in_chip, landing = _chips_wait(send_sems, recv_sems, in_chip_thru, landing, res["w_ffn_in"][1], "rs_chips_wait")
    my_chip = 2 * lax.axis_index("x") + lax.axis_index("y")
    own = lax.dynamic_slice_in_dim(in_chip, my_chip, 1, axis=0)
    res["w_in"] = adam("w_in", 1, lax.dynamic_update_slice_in_dim(landing, own, my_chip, axis=0), None)
    small_res = _small_update(
        everyone[:-1], [small[n] for n in small_names],
        [args["m_" + n].reshape(s) for n, s in SMALL], [args["v_" + n].reshape(s) for n, s in SMALL],
        everyone[-1], "adam_small")
    for i, n in enumerate(small_names):
        res[n] = [small_res[k][i].reshape(args[n].shape) for k in range(4)]
    loss = small_res[4][0, 0]

    order = ["norm_mix_g", "w_in", "rel_bias", "sg_ln_g", "sg_ln_b", "sg_w", "sg_b", "w_branch_att", "w_branch_sg",
             "w_out", "norm_xattn_g", "norm_mem_g", "w_xq", "w_xkv", "w_xo", "norm_ffn_g", "w_ffn_in", "w_ffn_out",
             "norm_final_g"]
    outs = [loss, grad_x.reshape(1, S, D)]
    for k in range(4):
        outs += [res[n][k] for n in order]
    return tuple(outs)
```

```python
import math

import jax
import jax.numpy as jnp
from jax import lax
from jax.experimental import pallas as pl
from jax.experimental.pallas import tpu as pltpu

F32 = jnp.float32
BF16 = jnp.bfloat16

D = 1024
ATT_W = 512
SG_W = 512
IN_COLS = 4608
DFF = 2816
MEM = 256
XH = 4
CHUNK = 64
BAND_KEYS = 640
ATT_R = 4096
ATT_SUBS = ATT_R // 128
START_SUBS = 8 * CHUNK // 128
ATT_ROWS = 32
ATT_ROWS_BWD = 16
DW_TOKENS = 2048
IN_CHUNK = 512
FF_CHUNK = 256
REL_CLIP = 128
NREL = 2 * REL_CLIP + 1
EPS = 1e-6
NEG = -1e30
N_DEV = 8

ADAM_LR = 0.001
ADAM_B1 = 0.9
ADAM_B2 = 0.999
ADAM_EPS = 1e-08
ADAM_WD = 0.01
ADAM_STEP = 10

LANES = 128
VMEM_LIMIT = 56 * 1024 * 1024
MESH = pl.DeviceIdType.MESH


def _cparams(n_axes):
    return pltpu.CompilerParams(dimension_semantics=("arbitrary",) * n_axes, vmem_limit_bytes=VMEM_LIMIT)


def _resident(shape):
    zeros = (0,) * len(shape)
    return pl.BlockSpec(shape, lambda *_: zeros, pipeline_mode=pl.Buffered(1))


def _rows(tm, cols, col_block=0):
    return pl.BlockSpec((tm, cols), lambda i: (i, col_block))


def _sigmoid(x):
    return pl.reciprocal(1.0 + jnp.exp(-x), approx=True)


_GELU_C = math.sqrt(2.0 / math.pi)


def _gelu(x):
    t = jnp.tanh(_GELU_C * (x + 0.044715 * (x * x * x)))
    return x * (0.5 * (1.0 + t))


def _gelu_and_grad(x):
    x2 = x * x
    t = jnp.tanh(_GELU_C * (x + 0.044715 * (x2 * x)))
    cdf = 0.5 * (1.0 + t)
    dcdf = 0.5 * (1.0 - t * t) * (_GELU_C * (1.0 + 3.0 * 0.044715 * x2))
    return x * cdf, cdf + x * dcdf


def _rstd(x):
    return lax.rsqrt(jnp.mean(x * x, axis=-1, keepdims=True) + EPS)


def _rms_bwd(dh, x, r, g):
    xh = x * r
    dxh = dh * g
    dx = r * (dxh - xh * jnp.mean(dxh * xh, axis=-1, keepdims=True))
    dg = jnp.sum(dh * xh, axis=0, keepdims=True)
    return dx, dg


def _group_sum64(x):
    r = lax.broadcasted_iota(jnp.int32, (LANES, LANES), 0) // 64
    c = lax.broadcasted_iota(jnp.int32, (LANES, LANES), 1) // 64
    same_group = (r == c).astype(BF16)

    def one(v):
        hi = v.astype(BF16)
        rest = v - hi.astype(F32)
        mid = rest.astype(BF16)
        lo = (rest - mid.astype(F32)).astype(BF16)
        return _dot(hi, same_group) + _dot(mid, same_group) + _dot(lo, same_group)

    pieces = [one(x[:, LANES * j:LANES * (j + 1)]) for j in range(x.shape[1] // LANES)]
    return pieces[0] if len(pieces) == 1 else jnp.concatenate(pieces, axis=1)


def _dot(a, b):
    return jnp.dot(a, b, preferred_element_type=F32)


def _dot_nt(a, b):
    return lax.dot_general(a, b, (((1,), (1,)), ((), ())), preferred_element_type=F32)


def _dot_tn(a, b):
    return lax.dot_general(a, b, (((0,), (0,)), ((), ())), preferred_element_type=F32)


DIAGS = 768


def _diag_onehot():
    r_idx = lax.broadcasted_iota(jnp.int32, (384, DIAGS), 0)
    t_idx = lax.broadcasted_iota(jnp.int32, (384, DIAGS), 1)
    dist = (8 * CHUNK + 127) - t_idx
    return (jnp.clip(dist, -REL_CLIP, REL_CLIP) + REL_CLIP == r_idx).astype(F32)


def _shift_rows(x, reverse):
    row = lax.broadcasted_iota(jnp.int32, x.shape, 0)
    for k in range(7):
        amt = (DIAGS - (1 << k)) if reverse else (1 << k)
        x = jnp.where(((row >> k) & 1) == 1, pltpu.roll(x, amt, 1), x)
    return x


N_TABLES = 1 + START_SUBS


def _bias_table(rel_bias_pad):
    def body(rb_ref, out_ref):
        per_diag = jnp.dot(rb_ref[...], _diag_onehot(), preferred_element_type=F32,
                           precision=lax.Precision.HIGHEST)
        a = lax.broadcasted_iota(jnp.int32, (128, BAND_KEYS), 0)
        b = lax.broadcasted_iota(jnp.int32, (128, BAND_KEYS), 1)
        band = (b // CHUNK >= a // CHUNK) & (b // CHUNK <= a // CHUNK + 8)
        for h in range(8):
            rows = jnp.broadcast_to(per_diag[h:h + 1, :], (128, DIAGS))
            table = _shift_rows(pltpu.roll(rows, DIAGS - 127, 1), reverse=False)[:, :BAND_KEYS]
            out_ref[0, h] = jnp.where(band, table, NEG)
            for s in range(START_SUBS):
                out_ref[1 + s, h] = jnp.where(band & (b >= 8 * CHUNK - 128 * s), table, NEG)

    return pl.pallas_call(
        body, name="bias_table",
        out_shape=jax.ShapeDtypeStruct((N_TABLES, 8, 128, BAND_KEYS), F32),
        in_specs=[pl.BlockSpec(memory_space=pltpu.VMEM)],
        out_specs=pl.BlockSpec(memory_space=pltpu.VMEM),
        compiler_params=pltpu.CompilerParams(vmem_limit_bytes=VMEM_LIMIT),
    )(rel_bias_pad)


def _bias_grad(ds_sum):
    def body(ds_ref, out_ref):
        sums = []
        for h in range(8):
            padded = jnp.concatenate([ds_ref[h], jnp.zeros((128, DIAGS - BAND_KEYS), F32)], axis=1)
            skewed = pltpu.roll(_shift_rows(padded, reverse=True), 127, 1)
            sums.append(jnp.sum(skewed, axis=0, keepdims=True))
        per_diag = jnp.concatenate(sums, axis=0)
        out_ref[...] = lax.dot_general(per_diag, _diag_onehot(), (((1,), (1,)), ((), ())),
                                       preferred_element_type=F32, precision=lax.Precision.HIGHEST)

    return pl.pallas_call(
        body, name="bias_grad",
        out_shape=jax.ShapeDtypeStruct((8, 384), F32),
        in_specs=[pl.BlockSpec(memory_space=pltpu.VMEM)],
        out_specs=pl.BlockSpec(memory_space=pltpu.VMEM),
    )(ds_sum)


def _norm_in(x, g1, plans=(), tm=512):
    S = x.shape[0]

    def body(x_ref, g_ref, h_ref):
        xv = x_ref[...]
        h_ref[...] = (xv * _rstd(xv) * g_ref[...]).astype(BF16)

    return _call(
        body, name="norm_in", grid=(S // tm,),
        out_shape=(jax.ShapeDtypeStruct((S, D), BF16),),
        in_specs=[_rows(tm, D), _resident((1, D))], out_specs=(_rows(tm, D),),
        operands=(x, g1), plans=plans)


def _in_proj(h, w_in_t, plans=(), tm=1024):
    S = h.shape[0]

    def body(h_ref, w_ref, qkv_ref, uv_ref, gate_ref):
        h = h_ref[...]
        w = IN_CHUNK
        for c in range(IN_COLS // w):
            zc = _dot_nt(h, w_ref[w * c:w * (c + 1), :])
            start = w * c
            if start < ATT_W:
                qkv_ref[:, start:start + w] = (zc * 0.125).astype(BF16)
            elif start < 3 * ATT_W:
                qkv_ref[:, start:start + w] = zc.astype(BF16)
            elif start < 3 * ATT_W + 2 * SG_W:
                uv_ref[:, start - 3 * ATT_W:start - 3 * ATT_W + w] = zc.astype(BF16)
            else:
                gate_ref[:, start - 3 * ATT_W - 2 * SG_W:start - 3 * ATT_W - 2 * SG_W + w] = zc.astype(BF16)

    return _call(
        body, name="in_proj", grid=(S // tm,),
        out_shape=(jax.ShapeDtypeStruct((S, 3 * ATT_W), BF16), jax.ShapeDtypeStruct((S, 2 * SG_W), BF16),
                   jax.ShapeDtypeStruct((S, 2 * D), BF16)),
        in_specs=[_rows(tm, D), _resident((IN_COLS, D))],
        out_specs=(_rows(tm, 3 * ATT_W), _rows(tm, 2 * SG_W), _rows(tm, 2 * D)),
        operands=(h, w_in_t), plans=plans)


def _two_heads(a, lo):
    return jnp.concatenate([jnp.where(lo, a, 0), jnp.where(lo, 0, a)], axis=0)


def _att_specs():
    R = ATT_R
    q = pl.BlockSpec((R, LANES), lambda j, i: (i, j))
    kp = pl.BlockSpec((R, LANES), lambda j, i: (jnp.maximum(i - 1, 0), 4 + j))
    kc = pl.BlockSpec((R, LANES), lambda j, i: (i, 4 + j))
    vp = pl.BlockSpec((R, LANES), lambda j, i: (jnp.maximum(i - 1, 0), 8 + j))
    vc = pl.BlockSpec((R, LANES), lambda j, i: (i, 8 + j))
    bias = pl.BlockSpec((N_TABLES, 2, 128, BAND_KEYS), lambda j, i: (0, j, 0, 0))
    return [q, kp, kc, vp, vc, bias]


def _attn_fwd(qkv, bias, plans=()):
    S = qkv.shape[0]
    R = ATT_R

    def body(q_ref, kp_ref, kc_ref, vp_ref, vc_ref, b_ref, o_ref, lse_ref, s_ref, p_ref, l_ref):
        i = pl.program_id(1)
        lo = lax.broadcasted_iota(jnp.int32, (1, LANES), 1) < 64
        kwin = jnp.concatenate([kp_ref[R - 8 * CHUNK:R, :], kc_ref[...]], axis=0)
        vwin = jnp.concatenate([vp_ref[R - 8 * CHUNK:R, :], vc_ref[...]], axis=0)
        for sub in range(ATT_SUBS):
            q2 = q_ref[128 * sub:128 * (sub + 1), :]
            kw = kwin[128 * sub:128 * sub + BAND_KEYS]
            vw = vwin[128 * sub:128 * sub + BAND_KEYS]
            table = jnp.where(i == 0, 1 + sub, 0) if sub < START_SUBS else 0
            s_ref[...] = _dot_nt(_two_heads(q2, lo), kw)
            for hh in range(2):
                for r0 in range(0, 128, ATT_ROWS):
                    rows = slice(128 * hh + r0, 128 * hh + r0 + ATT_ROWS)
                    s = s_ref[rows, :] + b_ref[table, hh, r0:r0 + ATT_ROWS, :]
                    top = jnp.max(s, axis=-1, keepdims=True)
                    p = jnp.exp(s - top)
                    total = jnp.sum(p, axis=-1, keepdims=True)
                    p_ref[rows, :] = (p / total).astype(BF16)
                    l_ref[rows, :] = jnp.broadcast_to(top + jnp.log(total), (ATT_ROWS, LANES))
            o = _dot(jnp.concatenate([p_ref[0:128, :], p_ref[128:256, :]], axis=1), _two_heads(vw, lo))
            o_ref[128 * sub:128 * (sub + 1), :] = o.astype(BF16)
            lse_ref[128 * sub:128 * (sub + 1), :] = jnp.where(lo, l_ref[0:128, :], l_ref[128:256, :])

    blk = pl.BlockSpec((R, LANES), lambda j, i: (i, j))
    return _call(
        body, name="attn_fwd", grid=(4, S // R),
        out_shape=(jax.ShapeDtypeStruct((S, ATT_W), BF16), jax.ShapeDtypeStruct((S, ATT_W), F32)),
        in_specs=_att_specs(), out_specs=(blk, blk),
        scratch_shapes=[pltpu.VMEM((256, BAND_KEYS), F32), pltpu.VMEM((256, BAND_KEYS), BF16),
                        pltpu.VMEM((256, LANES), F32)],
        operands=(qkv, qkv, qkv, qkv, qkv, bias), plans=plans)


def _sg_mask():
    t = lax.broadcasted_iota(jnp.int32, (128, 128), 0)
    s = lax.broadcasted_iota(jnp.int32, (128, 128), 1)
    return (s // CHUNK) <= (t // CHUNK)


def _sg_layernorm(gv, lng, lnb):
    mu = _group_sum64(gv) * (1.0 / 64)
    xc = gv - mu
    var = _group_sum64(xc * xc) * (1.0 / 64)
    rstd = lax.rsqrt(var + EPS)
    vhat = xc * rstd
    return vhat * lng + lnb, vhat, rstd


def _sgu_fwd(uv, lng, lnb, sg_w, b_exp, tm=1024):
    S = uv.shape[0]

    def body(uv_ref, lng_ref, lnb_ref, w_ref, b_ref, y_ref):
        lane = lax.broadcasted_iota(jnp.int32, (1, LANES), 1)
        lo = lane < 64
        mask = _sg_mask()
        gu = _gelu(uv_ref[:, 0:SG_W].astype(F32))
        vln, _, _ = _sg_layernorm(_gelu(uv_ref[:, SG_W:2 * SG_W].astype(F32)), lng_ref[...], lnb_ref[...])
        for gp in range(4):
            w0 = jnp.where(mask, w_ref[2 * gp], 0).astype(BF16)
            w1 = jnp.where(mask, w_ref[2 * gp + 1], 0).astype(BF16)
            cols = slice(128 * gp, 128 * (gp + 1))
            for n in range(tm // 128):
                rows = slice(128 * n, 128 * (n + 1))
                vl = vln[rows, cols]
                sv = (_dot(w0, jnp.where(lo, vl, 0).astype(BF16)) + _dot(w1, jnp.where(lo, 0, vl).astype(BF16))
                      + b_ref[:, cols])
                y_ref[rows, cols] = (gu[rows, cols] * sv).astype(BF16)

    return pl.pallas_call(
        body, name="sgu_fwd", grid=(S // tm,),
        out_shape=jax.ShapeDtypeStruct((S, SG_W), BF16),
        in_specs=[_rows(tm, 2 * SG_W), _resident((1, SG_W)), _resident((1, SG_W)),
                  _resident((8, 128, 128)), _resident((128, SG_W))],
        out_specs=_rows(tm, SG_W),
        compiler_params=_cparams(1),
    )(uv, lng, lnb, sg_w, b_exp)


def _merge_fwd(x, y_att, y_sg, gates, wba_t, wbs_t, w_out, tm=512):
    S = x.shape[0]

    def body(x_ref, ya_ref, ys_ref, g_ref, wba_ref, wbs_ref, wo_ref, x1_ref):
        a = _dot_nt(ya_ref[...], wba_ref[...])
        b = _dot_nt(ys_ref[...], wbs_ref[...])
        merged = _sigmoid(g_ref[:, 0:D].astype(F32)) * a + _sigmoid(g_ref[:, D:2 * D].astype(F32)) * b
        x1_ref[...] = x_ref[...] + _dot(merged.astype(BF16), wo_ref[...])

    return pl.pallas_call(
        body, name="merge_fwd", grid=(S // tm,),
        out_shape=jax.ShapeDtypeStruct((S, D), F32),
        in_specs=[_rows(tm, D), _rows(tm, ATT_W), _rows(tm, SG_W), _rows(tm, 2 * D),
                  _resident((D, ATT_W)), _resident((D, SG_W)), _resident((D, D))],
        out_specs=_rows(tm, D),
        compiler_params=_cparams(1),
    )(x, y_att, y_sg, gates, wba_t, wbs_t, w_out)


def _mem_kv(mem, g_mem, w_xkv_t):
    def body(m_ref, g_ref, w_ref, kv_ref, mn_ref):
        mv = m_ref[...]
        mn = (mv * _rstd(mv) * g_ref[...]).astype(BF16)
        mn_ref[...] = mn
        kv_ref[...] = _dot_nt(mn, w_ref[...]).astype(BF16)

    vm = pl.BlockSpec(memory_space=pltpu.VMEM)
    return pl.pallas_call(
        body, name="mem_kv",
        out_shape=(jax.ShapeDtypeStruct((MEM, 2 * D), BF16), jax.ShapeDtypeStruct((MEM, D), BF16)),
        in_specs=[vm, vm, vm], out_specs=(vm, vm),
        compiler_params=pltpu.CompilerParams(vmem_limit_bytes=VMEM_LIMIT),
    )(mem, g_mem, w_xkv_t)


def _xatt_head(qx, kv_ref, h):
    hs = slice(256 * h, 256 * (h + 1))
    s = _dot_nt(qx[:, hs], kv_ref[:, hs])
    p = jnp.exp(s - jnp.max(s, axis=-1, keepdims=True))
    return p / jnp.sum(p, axis=-1, keepdims=True)


def _xattn_fwd(x1, g2, w_xq, kv, w_xo, tm=1024):
    S = x1.shape[0]

    def body(x_ref, g_ref, wq_ref, kv_ref, wo_ref, x2_ref, hx_ref, qx_ref, o_ref):
        xv = x_ref[...]
        hx = (xv * _rstd(xv) * g_ref[...]).astype(BF16)
        hx_ref[...] = hx
        qx = (_dot(hx, wq_ref[...]) * (1.0 / 16)).astype(BF16)
        qx_ref[...] = qx
        for h in range(XH):
            p = _xatt_head(qx, kv_ref, h)
            o_ref[:, 256 * h:256 * (h + 1)] = _dot(p.astype(BF16), kv_ref[:, D + 256 * h:D + 256 * (h + 1)]).astype(BF16)
        x2_ref[...] = xv + _dot(o_ref[...], wo_ref[...])

    return pl.pallas_call(
        body, name="xattn_fwd", grid=(S // tm,),
        out_shape=(jax.ShapeDtypeStruct((S, D), F32),) + (jax.ShapeDtypeStruct((S, D), BF16),) * 3,
        in_specs=[_rows(tm, D), _resident((1, D)), _resident((D, D)), _resident((MEM, 2 * D)), _resident((D, D))],
        out_specs=(_rows(tm, D),) * 4,
        compiler_params=_cparams(1),
    )(x1, g2, w_xq, kv, w_xo)


def _ffn_fwd(x2, tgt, g3, w_ffn_in_t, w_ffn_out, g4, tm=256):
    S = x2.shape[0]

    def body(x_ref, t_ref, g3_ref, wi_ref, wo_ref, g4_ref, dx3_ref, gu_ref, hf_ref, act_ref, loss_ref, dg4_ref):
        i = pl.program_id(0)
        xv = x_ref[...]
        hf = (xv * _rstd(xv) * g3_ref[...]).astype(BF16)
        hf_ref[...] = hf
        for c in range(DFF // FF_CHUNK):
            cs = slice(FF_CHUNK * c, FF_CHUNK * (c + 1))
            us = slice(DFF + FF_CHUNK * c, DFF + FF_CHUNK * (c + 1))
            gate = _dot_nt(hf, wi_ref[cs, :])
            up = _dot_nt(hf, wi_ref[us, :])
            gu_ref[:, cs] = gate.astype(BF16)
            gu_ref[:, us] = up.astype(BF16)
            act_ref[:, cs] = ((gate * _sigmoid(gate)) * up).astype(BF16)
        acc = xv + _dot(act_ref[...], wo_ref[...])
        r4 = _rstd(acc)
        g4 = g4_ref[...]
        diff = acc * r4 * g4 - t_ref[...]
        dy = diff * (1.0 / D)
        dx3, dg4 = _rms_bwd(dy, acc, r4, g4)
        dx3_ref[...] = dx3
        part = 0.5 * jnp.sum(jnp.mean(diff * diff, axis=-1, keepdims=True))

        @pl.when(i == 0)
        def _():
            loss_ref[...] = jnp.zeros_like(loss_ref)
            dg4_ref[...] = jnp.zeros_like(dg4_ref)

        loss_ref[...] += jnp.full(loss_ref.shape, part, F32)
        dg4_ref[...] += dg4

    return pl.pallas_call(
        body, name="ffn_fwd", grid=(S // tm,),
        out_shape=(jax.ShapeDtypeStruct((S, D), F32), jax.ShapeDtypeStruct((S, 2 * DFF), BF16),
                   jax.ShapeDtypeStruct((S, D), BF16), jax.ShapeDtypeStruct((S, DFF), BF16),
                   jax.ShapeDtypeStruct((8, LANES), F32), jax.ShapeDtypeStruct((1, D), F32)),
        in_specs=[_rows(tm, D), _rows(tm, D), _resident((1, D)), _resident((2 * DFF, D)), _resident((DFF, D)),
                  _resident((1, D))],
        out_specs=(_rows(tm, D), _rows(tm, 2 * DFF), _rows(tm, D), _rows(tm, DFF),
                   pl.BlockSpec((8, LANES), lambda i: (0, 0)), pl.BlockSpec((1, D), lambda i: (0, 0))),
        compiler_params=_cparams(1),
    )(x2, tgt, g3, w_ffn_in_t, w_ffn_out, g4)


def _ffn_bwd(dx3, gu, x2, g3, w_ffn_out, w_ffn_in_t, tm=256):
    S = x2.shape[0]

    def body(d_ref, gu_ref, x_ref, g3_ref, wo_ref, wit_ref, dx2_ref, dgu_ref, dg3_ref):
        i = pl.program_id(0)
        d3 = d_ref[...]
        d3b = d3.astype(BF16)
        for c in range(DFF // FF_CHUNK):
            cs = slice(FF_CHUNK * c, FF_CHUNK * (c + 1))
            us = slice(DFF + FF_CHUNK * c, DFF + FF_CHUNK * (c + 1))
            da = _dot_nt(d3b, wo_ref[cs, :])
            gate = gu_ref[:, cs].astype(F32)
            up = gu_ref[:, us].astype(F32)
            sg = _sigmoid(gate)
            dgate = (da * up * (sg * (1.0 + gate * (1.0 - sg)))).astype(BF16)
            dup = (da * (gate * sg)).astype(BF16)
            dgu_ref[:, cs] = dgate
            dgu_ref[:, us] = dup
        dhf = _dot(dgu_ref[...], wit_ref[...])
        xv = x_ref[...]
        dx, dg3 = _rms_bwd(dhf, xv, _rstd(xv), g3_ref[...])
        dx2_ref[...] = d3 + dx

        @pl.when(i == 0)
        def _():
            dg3_ref[...] = jnp.zeros_like(dg3_ref)

        dg3_ref[...] += dg3

    return pl.pallas_call(
        body, name="ffn_bwd", grid=(S // tm,),
        out_shape=(jax.ShapeDtypeStruct((S, D), F32), jax.ShapeDtypeStruct((S, 2 * DFF), BF16),
                   jax.ShapeDtypeStruct((1, D), F32)),
        in_specs=[_rows(tm, D), _rows(tm, 2 * DFF), _rows(tm, D), _resident((1, D)),
                  _resident((DFF, D)), _resident((2 * DFF, D))],
        out_specs=(_rows(tm, D), _rows(tm, 2 * DFF), pl.BlockSpec((1, D), lambda i: (0, 0))),
        compiler_params=_cparams(1),
    )(dx3, gu, x2, g3, w_ffn_out, w_ffn_in_t)


def _dw(a, b, tmm, tn, ts, name, out_dtype=BF16, plans=()):
    S, M = a.shape
    N = b.shape[1]
    ts = min(ts, S)
    nk = S // ts

    def body(a_ref, b_ref, o_ref, acc_ref):
        k = pl.program_id(2)

        @pl.when(k == 0)
        def _():
            acc_ref[...] = jnp.zeros_like(acc_ref)

        acc_ref[...] += _dot_tn(a_ref[...].astype(BF16), b_ref[...].astype(BF16))

        @pl.when(k == nk - 1)
        def _():
            o_ref[...] = acc_ref[...].astype(out_dtype)

    out = _call(
        body, name=name, grid=(M // tmm, N // tn, nk),
        out_shape=(jax.ShapeDtypeStruct((M, N), out_dtype),),
        in_specs=[pl.BlockSpec((ts, tmm), lambda m, n, k: (k, m)), pl.BlockSpec((ts, tn), lambda m, n, k: (k, n))],
        out_specs=(pl.BlockSpec((tmm, tn), lambda m, n, k: (m, n)),),
        scratch_shapes=[pltpu.VMEM((tmm, tn), F32)],
        operands=(a, b), plans=plans)
    return out if plans else out[0]


def _xattn_bwd(dx2, x1, qx, g2, w_xq, w_xo, kv, plans=(), tm=512):
    S = x1.shape[0]

    def body(d_ref, x_ref, qx_ref, g_ref, wq_ref, wo_ref, kv_ref, dx1_ref, dq_ref, dkv_ref, dg2_ref):
        i = pl.program_id(0)

        @pl.when(i == 0)
        def _():
            dkv_ref[...] = jnp.zeros_like(dkv_ref)
            dg2_ref[...] = jnp.zeros_like(dg2_ref)

        d2 = d_ref[...]
        qx = qx_ref[...]
        do = _dot_nt(d2.astype(BF16), wo_ref[...]).astype(BF16)
        for h in range(XH):
            hs = slice(256 * h, 256 * (h + 1))
            vs = slice(D + 256 * h, D + 256 * (h + 1))
            p = _xatt_head(qx, kv_ref, h)
            dp = _dot_nt(do[:, hs], kv_ref[:, vs])
            ds = (p * (dp - jnp.sum(dp * p, axis=-1, keepdims=True))).astype(BF16)
            dq_ref[:, hs] = (_dot(ds, kv_ref[:, hs]) * (1.0 / 16)).astype(BF16)
            dkv_ref[:, hs] += _dot_tn(ds, qx[:, hs])
            dkv_ref[:, vs] += _dot_tn(p.astype(BF16), do[:, hs])
        dhx = _dot_nt(dq_ref[...], wq_ref[...])
        xv = x_ref[...]
        dx, dg2 = _rms_bwd(dhx, xv, _rstd(xv), g_ref[...])
        dx1_ref[...] = d2 + dx
        dg2_ref[...] += dg2

    return _call(
        body, name="xattn_bwd", grid=(S // tm,),
        out_shape=(jax.ShapeDtypeStruct((S, D), F32), jax.ShapeDtypeStruct((S, D), BF16),
                   jax.ShapeDtypeStruct((MEM, 2 * D), F32), jax.ShapeDtypeStruct((1, D), F32)),
        in_specs=[_rows(tm, D), _rows(tm, D), _rows(tm, D), _resident((1, D)), _resident((D, D)), _resident((D, D)),
                  _resident((MEM, 2 * D))],
        out_specs=(_rows(tm, D), _rows(tm, D),
                   pl.BlockSpec((MEM, 2 * D), lambda i: (0, 0)), pl.BlockSpec((1, D), lambda i: (0, 0))),
        operands=(dx2, x1, qx, g2, w_xq, w_xo, kv), plans=plans)


def _mem_kv_bwd(dkv, mem, g_mem, mn, w_xkv_t):
    def body(dkv_ref, m_ref, g_ref, mn_ref, wt_ref, dw_ref, dg_ref):
        dkvb = dkv_ref[...].astype(BF16)
        dw_ref[...] = _dot_tn(dkvb, mn_ref[...]).astype(BF16)
        dmn = _dot(dkvb, wt_ref[...])
        mv = m_ref[...]
        dg_ref[...] = jnp.sum(dmn * (mv * _rstd(mv)), axis=0, keepdims=True)

    vm = pl.BlockSpec(memory_space=pltpu.VMEM)
    return pl.pallas_call(
        body, name="mem_kv_bwd",
        out_shape=(jax.ShapeDtypeStruct((2 * D, D), BF16), jax.ShapeDtypeStruct((1, D), F32)),
        in_specs=[vm] * 5, out_specs=(vm, vm),
        compiler_params=pltpu.CompilerParams(vmem_limit_bytes=VMEM_LIMIT),
    )(dkv, mem, g_mem, mn, w_xkv_t)


def _merge_bwd(dx1, y_att, y_sg, gates, wba_t, wbs_t, w_out, tm=512):
    S = dx1.shape[0]

    def body(d_ref, ya_ref, ys_ref, g_ref, wbat_ref, wbst_ref, wo_ref,
             mg_ref, da_ref, db_ref, dya_ref, dys_ref, dg_ref):
        dm = _dot_nt(d_ref[...].astype(BF16), wo_ref[...])
        a = _dot_nt(ya_ref[...], wbat_ref[...])
        b = _dot_nt(ys_ref[...], wbst_ref[...])
        sa = _sigmoid(g_ref[:, 0:D].astype(F32))
        sb = _sigmoid(g_ref[:, D:2 * D].astype(F32))
        mg_ref[...] = (sa * a + sb * b).astype(BF16)
        da = (dm * sa).astype(BF16)
        db = (dm * sb).astype(BF16)
        da_ref[...] = da
        db_ref[...] = db
        dg_ref[:, 0:D] = (dm * a * sa * (1.0 - sa)).astype(BF16)
        dg_ref[:, D:2 * D] = (dm * b * sb * (1.0 - sb)).astype(BF16)
        dya_ref[...] = _dot(da, wbat_ref[...]).astype(BF16)
        dys_ref[...] = _dot(db, wbst_ref[...]).astype(BF16)

    return pl.pallas_call(
        body, name="merge_bwd", grid=(S // tm,),
        out_shape=(jax.ShapeDtypeStruct((S, D), BF16), jax.ShapeDtypeStruct((S, D), BF16),
                   jax.ShapeDtypeStruct((S, D), BF16), jax.ShapeDtypeStruct((S, ATT_W), BF16),
                   jax.ShapeDtypeStruct((S, SG_W), BF16), jax.ShapeDtypeStruct((S, 2 * D), BF16)),
        in_specs=[_rows(tm, D), _rows(tm, ATT_W), _rows(tm, SG_W), _rows(tm, 2 * D),
                  _resident((D, ATT_W)), _resident((D, SG_W)), _resident((D, D))],
        out_specs=(_rows(tm, D), _rows(tm, D), _rows(tm, D), _rows(tm, ATT_W), _rows(tm, SG_W), _rows(tm, 2 * D)),
        compiler_params=_cparams(1),
    )(dx1, y_att, y_sg, gates, wba_t, wbs_t, w_out)


def _sgu_bwd(uv, dy_sg, lng, lnb, sg_w, b_exp, plans=(), tm=1024):
    S = uv.shape[0]
    n_steps = S // tm

    def body(uv_ref, dy_ref, lng_ref, lnb_ref, w_ref, b_ref, duv_ref, dw_ref, dbx_ref, dlng_ref, dlnb_ref, dvln_ref):
        i = pl.program_id(0)

        @pl.when(i == 0)
        def _():
            dw_ref[...] = jnp.zeros_like(dw_ref)
            dbx_ref[...] = jnp.zeros_like(dbx_ref)
            dlng_ref[...] = jnp.zeros_like(dlng_ref)
            dlnb_ref[...] = jnp.zeros_like(dlnb_ref)

        lane = lax.broadcasted_iota(jnp.int32, (1, LANES), 1)
        lo = lane < 64
        mask = _sg_mask()
        lng = lng_ref[...]
        gu, dgelu_u = _gelu_and_grad(uv_ref[:, 0:SG_W].astype(F32))
        gv, dgelu_v = _gelu_and_grad(uv_ref[:, SG_W:2 * SG_W].astype(F32))
        vln, vhat, rstd = _sg_layernorm(gv, lng, lnb_ref[...])
        dy = dy_ref[...].astype(F32)
        dsv_all = dy * gu
        for gp in range(4):
            wf0 = jnp.where(mask, w_ref[2 * gp], 0)
            wf1 = jnp.where(mask, w_ref[2 * gp + 1], 0)
            w0 = wf0.astype(BF16)
            w1 = wf1.astype(BF16)
            cols = slice(128 * gp, 128 * (gp + 1))
            dw0 = jnp.zeros((128, 128), F32)
            dw1 = jnp.zeros((128, 128), F32)
            dbx = jnp.zeros((128, LANES), F32)
            for n in range(tm // 128):
                rows = slice(128 * n, 128 * (n + 1))
                vl = vln[rows, cols]
                vl0 = jnp.where(lo, vl, 0).astype(BF16)
                vl1 = jnp.where(lo, 0, vl).astype(BF16)
                sv = _dot(w0, vl0) + _dot(w1, vl1) + b_ref[:, cols]
                duv_ref[rows, cols] = (dy[rows, cols] * sv * dgelu_u[rows, cols]).astype(BF16)
                dsv = dsv_all[rows, cols]
                dbx = dbx + dsv
                ds0 = jnp.where(lo, dsv, 0).astype(BF16)
                ds1 = jnp.where(lo, 0, dsv).astype(BF16)
                dw0 = dw0 + _dot_nt(ds0, vl0)
                dw1 = dw1 + _dot_nt(ds1, vl1)
                dvln_ref[rows, cols] = _dot_tn(w0, ds0) + _dot_tn(w1, ds1)
            dw_ref[2 * gp] += jnp.where(mask, dw0, 0)
            dw_ref[2 * gp + 1] += jnp.where(mask, dw1, 0)
            dbx_ref[:, cols] += dbx
        dvln = dvln_ref[...]
        dlng_ref[...] += jnp.sum(dvln * vhat, axis=0, keepdims=True)
        dlnb_ref[...] += jnp.sum(dvln, axis=0, keepdims=True)
        dvh = dvln * lng
        dgv = rstd * (dvh - _group_sum64(dvh) * (1.0 / 64) - vhat * (_group_sum64(dvh * vhat) * (1.0 / 64)))
        duv_ref[:, SG_W:2 * SG_W] = (dgv * dgelu_v).astype(BF16)

        @pl.when(i == n_steps - 1)
        def _():
            dbx_ref[...] = _group_sum64(dbx_ref[...])

    return _call(
        body, name="sgu_bwd", grid=(n_steps,),
        out_shape=(jax.ShapeDtypeStruct((S, 2 * SG_W), BF16), jax.ShapeDtypeStruct((8, 128, 128), F32),
                   jax.ShapeDtypeStruct((128, SG_W), F32), jax.ShapeDtypeStruct((1, SG_W), F32),
                   jax.ShapeDtypeStruct((1, SG_W), F32)),
        in_specs=[_rows(tm, 2 * SG_W), _rows(tm, SG_W), _resident((1, SG_W)), _resident((1, SG_W)),
                  _resident((8, 128, 128)), _resident((128, SG_W))],
        out_specs=(_rows(tm, 2 * SG_W), pl.BlockSpec((8, 128, 128), lambda i: (0, 0, 0)),
                   pl.BlockSpec((128, SG_W), lambda i: (0, 0)), pl.BlockSpec((1, SG_W), lambda i: (0, 0)),
                   pl.BlockSpec((1, SG_W), lambda i: (0, 0))),
        scratch_shapes=[pltpu.VMEM((tm, SG_W), F32)],
        operands=(uv, dy_sg, lng, lnb, sg_w, b_exp), plans=plans)


def _attn_bwd(qkv, dy_att, y_att, lse, bias, plans=()):
    S = qkv.shape[0]
    R = ATT_R

    def body(q_ref, kp_ref, kc_ref, vp_ref, vc_ref, b_ref, dy_ref, y_ref, lse_ref, dq_ref, dk_ref, dv_ref, dss_ref,
             s_ref, dp_ref, pb_ref, dsb_ref):
        i = pl.program_id(1)

        @pl.when(i == 0)
        def _():
            dk_ref[...] = jnp.zeros_like(dk_ref)
            dv_ref[...] = jnp.zeros_like(dv_ref)
            dss_ref[...] = jnp.zeros_like(dss_ref)

        lo = lax.broadcasted_iota(jnp.int32, (1, LANES), 1) < 64
        kwin = jnp.concatenate([kp_ref[R - 8 * CHUNK:R, :], kc_ref[...]], axis=0)
        vwin = jnp.concatenate([vp_ref[R - 8 * CHUNK:R, :], vc_ref[...]], axis=0)
        for sub in range(ATT_SUBS):
            rows = slice(128 * sub, 128 * (sub + 1))
            kw = kwin[128 * sub:128 * sub + BAND_KEYS]
            vw = vwin[128 * sub:128 * sub + BAND_KEYS]
            table = jnp.where(i == 0, 1 + sub, 0) if sub < START_SUBS else 0
            qs = _two_heads(q_ref[rows, :], lo)
            dos = _two_heads(dy_ref[rows, :], lo)
            dyy = dy_ref[rows, :].astype(F32) * y_ref[rows, :].astype(F32)
            delta = (jnp.sum(jnp.where(lo, dyy, 0.0), axis=-1, keepdims=True),
                     jnp.sum(jnp.where(lo, 0.0, dyy), axis=-1, keepdims=True))
            lse2 = lse_ref[rows, :]
            s_ref[...] = _dot_nt(qs, kw)
            dp_ref[...] = _dot_nt(dos, vw)
            for hh in range(2):
                lse = lse2[:, 64 * hh:64 * hh + 1]
                for r0 in range(0, 128, ATT_ROWS_BWD):
                    rr = slice(r0, r0 + ATT_ROWS_BWD)
                    both = slice(128 * hh + r0, 128 * hh + r0 + ATT_ROWS_BWD)
                    p = jnp.exp(s_ref[both, :] + b_ref[table, hh, rr, :] - lse[rr])
                    ds = p * (dp_ref[both, :] - delta[hh][rr])
                    dss_ref[hh, rr, :] += ds
                    pb_ref[both, :] = p.astype(BF16)
                    dsb_ref[both, :] = ds.astype(BF16)
            dq = _dot(jnp.concatenate([dsb_ref[0:128, :], dsb_ref[128:256, :]], axis=1), _two_heads(kw, lo))
            dq_ref[rows, :] = dq.astype(BF16)
            start = pl.multiple_of(i * R + 128 * sub, 128)
            dk_ref[pl.ds(start, BAND_KEYS), :] += _dot_tn(dsb_ref[...], qs)
            dv_ref[pl.ds(start, BAND_KEYS), :] += _dot_tn(pb_ref[...], dos)

    acc_spec = pl.BlockSpec((S + 8 * CHUNK, LANES), lambda j, i: (0, j))
    return _call(
        body, name="attn_bwd", grid=(4, S // R),
        out_shape=(jax.ShapeDtypeStruct((S, ATT_W), BF16), jax.ShapeDtypeStruct((S + 8 * CHUNK, ATT_W), F32),
                   jax.ShapeDtypeStruct((S + 8 * CHUNK, ATT_W), F32), jax.ShapeDtypeStruct((8, 128, BAND_KEYS), F32)),
        in_specs=_att_specs() + [pl.BlockSpec((R, LANES), lambda j, i: (i, j))] * 3,
        out_specs=(pl.BlockSpec((R, LANES), lambda j, i: (i, j)), acc_spec, acc_spec,
                   pl.BlockSpec((2, 128, BAND_KEYS), lambda j, i: (j, 0, 0))),
        scratch_shapes=[pltpu.VMEM((256, BAND_KEYS), F32), pltpu.VMEM((256, BAND_KEYS), F32),
                        pltpu.VMEM((256, BAND_KEYS), BF16), pltpu.VMEM((256, BAND_KEYS), BF16)],
        operands=(qkv, qkv, qkv, qkv, qkv, bias, dy_att, y_att, lse), plans=plans)


def _in_bwd(dq, dk, dv, duv, dgates, x, dx1, g1, w_in_t, plans=(), tm=512):
    S = x.shape[0]
    pad_blocks = (8 * CHUNK) // tm

    def body(dq_ref, dk_ref, dv_ref, duv_ref, dg_ref, x_ref, d1_ref, g_ref, wt_ref, dx_ref, dz_ref, dg1_ref):
        i = pl.program_id(0)
        dz_ref[:, 0:ATT_W] = (dq_ref[...].astype(F32) * 0.125).astype(BF16)
        dz_ref[:, ATT_W:2 * ATT_W] = dk_ref[...].astype(BF16)
        dz_ref[:, 2 * ATT_W:3 * ATT_W] = dv_ref[...].astype(BF16)
        dz_ref[:, 3 * ATT_W:3 * ATT_W + 2 * SG_W] = duv_ref[...]
        dz_ref[:, 3 * ATT_W + 2 * SG_W:IN_COLS] = dg_ref[...]
        dh = _dot(dz_ref[...], wt_ref[...])
        xv = x_ref[...]
        dx, dg1 = _rms_bwd(dh, xv, _rstd(xv), g_ref[...])
        dx_ref[...] = d1_ref[...] + dx

        @pl.when(i == 0)
        def _():
            dg1_ref[...] = jnp.zeros_like(dg1_ref)

        dg1_ref[...] += dg1

    shifted = pl.BlockSpec((tm, ATT_W), lambda i: (i + pad_blocks, 0))
    return _call(
        body, name="in_bwd", grid=(S // tm,),
        out_shape=(jax.ShapeDtypeStruct((S, D), F32), jax.ShapeDtypeStruct((S, IN_COLS), BF16),
                   jax.ShapeDtypeStruct((1, D), F32)),
        in_specs=[_rows(tm, ATT_W), shifted, shifted, _rows(tm, 2 * SG_W), _rows(tm, 2 * D), _rows(tm, D),
                  _rows(tm, D), _resident((1, D)), _resident((IN_COLS, D))],
        out_specs=(_rows(tm, D), _rows(tm, IN_COLS), pl.BlockSpec((1, D), lambda i: (0, 0))),
        operands=(dq, dk, dv, duv, dgates, x, dx1, g1, w_in_t), plans=plans)


def _adam_math(w, g, m, v):
    m = ADAM_B1 * m + (1.0 - ADAM_B1) * g
    v = ADAM_B2 * v + (1.0 - ADAM_B2) * (g * g)
    m_hat = m / (1.0 - ADAM_B1 ** ADAM_STEP)
    v_hat = v / (1.0 - ADAM_B2 ** ADAM_STEP)
    delta = -ADAM_LR * (m_hat / (jnp.sqrt(v_hat) + ADAM_EPS) + ADAM_WD * w)
    return delta, m, v


def _adam(parts, w, m, v, tr, name, transposed, after=None):
    P = parts.shape[0]
    R, C = w.shape

    def body(p_ref, w_ref, m_ref, v_ref, *rest):
        g_ref, d_ref, mo_ref, vo_ref = rest[-4:]
        if transposed:
            eye = (lax.broadcasted_iota(jnp.int32, (C, C), 0) == lax.broadcasted_iota(jnp.int32, (C, C), 1)).astype(BF16)
            part = lambda k: _dot_tn(p_ref[k], eye)
        else:
            part = lambda k: p_ref[k].astype(F32)
        g = part(0)
        for k in range(1, P):
            g = g + part(k)
        g_ref[...] = g
        d_ref[...], mo_ref[...], vo_ref[...] = _adam_math(w_ref[...], g, m_ref[...], v_ref[...])

    row = pl.BlockSpec((tr, C), lambda i: (i, 0))
    p_spec = pl.BlockSpec((P, C, tr), lambda i: (0, 0, i)) if transposed else pl.BlockSpec((P, tr, C), lambda i: (0, i, 0))
    extra = [] if after is None else [after]
    return pl.pallas_call(
        body, name=name, grid=(R // tr,),
        out_shape=tuple(jax.ShapeDtypeStruct((R, C), F32) for _ in range(4)),
        in_specs=[p_spec, row, row, row] + [pl.BlockSpec(memory_space=pl.ANY)] * len(extra),
        out_specs=(row, row, row, row),
        compiler_params=_cparams(1),
    )(parts, w, m, v, *extra)


def _my_place():
    return lax.axis_index("x"), lax.axis_index("y"), lax.axis_index("c")


def _other_chips(x, y):
    return [(1 - x, y), (x, 1 - y), (1 - x, 1 - y)]


class _Plan:
    def __init__(self, arrays, out_shapes, sems, start, finish, forward=None, forward_at=0.7):
        self.arrays, self.out_shapes, self.sems = list(arrays), list(out_shapes), list(sems)
        self.start, self.finish, self.forward, self.forward_at = start, finish, forward, forward_at


def _call(body, *, name, grid, in_specs, out_specs, out_shape, operands, scratch_shapes=(), plans=()):
    n_in, n_out, n_scr = len(operands), len(out_shape), len(scratch_shapes)
    p_in = [a for p in plans for a in p.arrays]
    p_out = [s for p in plans for s in p.out_shapes]
    p_sem = [s for p in plans for s in p.sems]
    steps = math.prod(grid)

    def wrapped(*refs):
        ins, refs = refs[:n_in], refs[n_in:]
        pins, refs = refs[:len(p_in)], refs[len(p_in):]
        outs, refs = refs[:n_out], refs[n_out:]
        pouts, refs = refs[:len(p_out)], refs[len(p_out):]
        scr, psems = refs[:n_scr], refs[n_scr:]
        step = 0
        for axis, size in enumerate(grid):
            step = step * size + pl.program_id(axis)
        bound = []
        for p in plans:
            bound.append((p, pins[:len(p.arrays)], pouts[:len(p.out_shapes)], psems[:len(p.sems)]))
            pins, pouts, psems = pins[len(p.arrays):], pouts[len(p.out_shapes):], psems[len(p.sems):]

        @pl.when(step == 0)
        def _():
            for p, a, b, s in bound:
                p.start(a, b, s)

        body(*ins, *outs, *scr)

        for p, a, b, s in bound:
            if p.forward is not None:
                @pl.when(step == min(int(p.forward_at * steps), steps - 1))
                def _(p=p, a=a, b=b, s=s):
                    p.forward(a, b, s)

        @pl.when(step == steps - 1)
        def _():
            for p, a, b, s in bound:
                p.finish(a, b, s)

    hbm = pl.BlockSpec(memory_space=pl.ANY)
    return pl.pallas_call(
        wrapped if plans else body, name=name, grid=grid,
        out_shape=tuple(out_shape) + tuple(p_out),
        in_specs=list(in_specs) + [hbm] * len(p_in),
        out_specs=tuple(out_specs) + tuple([hbm] * len(p_out)),
        scratch_shapes=list(scratch_shapes) + p_sem,
        compiler_params=_cparams(len(grid)),
    )(*operands, *p_in)


def _run_plan(plan, name):
    n_in, n_out = len(plan.arrays), len(plan.out_shapes)

    def body(*refs):
        a, b, s = refs[:n_in], refs[n_in:n_in + n_out], refs[n_in + n_out:]
        plan.start(a, b, s)
        if plan.forward is not None:
            plan.forward(a, b, s)
        plan.finish(a, b, s)

    hbm = pl.BlockSpec(memory_space=pl.ANY)
    return pl.pallas_call(
        body, name=name, out_shape=tuple(plan.out_shapes),
        in_specs=[hbm] * n_in, out_specs=tuple([hbm] * n_out), scratch_shapes=plan.sems,
    )(*plan.arrays)


def _gather_plan(shards, forward_at=0.7):
    n = len(shards)

    def copies(ins, outs, sems):
        send_sems, recv_sems, local_sems = sems
        x, y, c = _my_place()
        me, sibling = (x, y, c), (x, y, 1 - c)
        chips = _other_chips(x, y)

        def idx(p):
            return 4 * p[0] + 2 * p[1] + p[2]

        def copy(a, k, block, to, src=None):
            return pltpu.make_async_remote_copy(
                src_ref=outs[a].at[idx(block)] if src is None else src, dst_ref=outs[a].at[idx(block)],
                send_sem=send_sems.at[a, k], recv_sem=recv_sems.at[a, k], device_id=to, device_id_type=MESH)

        mine = [pltpu.make_async_copy(ins[a], outs[a].at[idx(me)], local_sems.at[a]) for a in range(n)]
        first = []
        for a in range(n):
            first.append(copy(a, 0, me, sibling, src=ins[a]))
            first += [copy(a, 1 + j, me, (*chip, c), src=ins[a]) for j, chip in enumerate(chips)]
        arrived = [copy(a, 1 + j, (*chip, c), me) for j, chip in enumerate(chips) for a in range(n)]
        passed = [copy(a, 4 + j, (*chip, c), sibling) for j, chip in enumerate(chips) for a in range(n)]
        from_sibling = []
        for a in range(n):
            from_sibling.append(copy(a, 0, sibling, me))
            from_sibling += [copy(a, 4 + j, (*chip, 1 - c), me) for j, chip in enumerate(chips)]
        return mine, first, arrived, passed, from_sibling

    def start(ins, outs, sems):
        mine, first, _, _, _ = copies(ins, outs, sems)
        for cp in mine + first:
            cp.start()

    def forward(ins, outs, sems):
        _, _, arrived, passed, _ = copies(ins, outs, sems)
        for landed, onward in zip(arrived, passed):
            landed.wait_recv()
            onward.start()

    def finish(ins, outs, sems):
        mine, first, _, passed, from_sibling = copies(ins, outs, sems)
        for cp in from_sibling:
            cp.wait_recv()
        for cp in first + passed:
            cp.wait_send()
        for cp in mine:
            cp.wait()

    return _Plan(shards, [jax.ShapeDtypeStruct((N_DEV,) + s.shape, s.dtype) for s in shards],
                 [pltpu.SemaphoreType.DMA((n, 7)), pltpu.SemaphoreType.DMA((n, 7)), pltpu.SemaphoreType.DMA((n,))],
                 start, finish, forward, forward_at)


def _sibling_plan(scatter, whole=()):
    ns = len(scatter)
    n = ns + len(whole)

    def copies(ins, outs, sems):
        send_sems, recv_sems = sems
        x, y, c = _my_place()
        out = []
        for a in range(n):
            for k in range(4 if a < ns else 1):
                src = ins[a].at[2 * k + (1 - c)] if a < ns else ins[a]
                dst = outs[a].at[k] if a < ns else outs[a]
                out.append(pltpu.make_async_remote_copy(
                    src_ref=src, dst_ref=dst, send_sem=send_sems.at[a, k], recv_sem=recv_sems.at[a, k],
                    device_id=(x, y, 1 - c), device_id_type=MESH))
        return out

    def start(ins, outs, sems):
        for cp in copies(ins, outs, sems):
            cp.start()

    def finish(ins, outs, sems):
        for cp in copies(ins, outs, sems):
            cp.wait()

    shapes = [jax.ShapeDtypeStruct((4,) + p.shape[1:], p.dtype) for p in scatter]
    shapes += [jax.ShapeDtypeStruct(p.shape, p.dtype) for p in whole]
    return _Plan(list(scatter) + list(whole), shapes,
                 [pltpu.SemaphoreType.DMA((n, 4)), pltpu.SemaphoreType.DMA((n, 4))], start, finish)


def _pair_sums(mine, theirs, c_idx, name):
    n = len(mine)

    def body(c_ref, *refs):
        for a in range(n):
            out = refs[2 * n + a]
            out[...] = (refs[a][...].astype(F32) + refs[n + a][...].astype(F32)).astype(out.dtype)

    def block(arr):
        return (1,) + arr.shape[1:]

    grid_spec = pltpu.PrefetchScalarGridSpec(
        num_scalar_prefetch=1, grid=(4,),
        in_specs=[pl.BlockSpec(block(m), lambda k, c_ref: (2 * k + c_ref[0], 0, 0)) for m in mine]
        + [pl.BlockSpec(block(t), lambda k, c_ref: (k, 0, 0)) for t in theirs],
        out_specs=tuple(pl.BlockSpec(block(t), lambda k, c_ref: (k, 0, 0)) for t in theirs))
    return pl.pallas_call(
        body, name=name, grid_spec=grid_spec,
        out_shape=tuple(jax.ShapeDtypeStruct(t.shape, m.dtype) for m, t in zip(mine, theirs)),
        compiler_params=_cparams(1),
    )(c_idx, *mine, *theirs)


def _peers_plan(arrays):
    n = len(arrays)

    def copies(ins, outs, sems):
        send_sems, recv_sems, local_sems = sems
        x, y, c = _my_place()
        me = 4 * x + 2 * y + c
        out = [pltpu.make_async_copy(ins[a], outs[a].at[me], local_sems.at[a]) for a in range(n)]
        for a in range(n):
            for k in range(N_DEV - 1):
                bits = k + 1
                peer = (x ^ (bits >> 2), y ^ ((bits >> 1) & 1), c ^ (bits & 1))
                out.append(pltpu.make_async_remote_copy(
                    src_ref=ins[a], dst_ref=outs[a].at[me], send_sem=send_sems.at[a, k], recv_sem=recv_sems.at[a, k],
                    device_id=peer, device_id_type=MESH))
        return out

    def start(ins, outs, sems):
        for cp in copies(ins, outs, sems):
            cp.start()

    def finish(ins, outs, sems):
        for cp in copies(ins, outs, sems):
            cp.wait()

    return _Plan(list(arrays), [jax.ShapeDtypeStruct((N_DEV,) + a.shape, a.dtype) for a in arrays],
                 [pltpu.SemaphoreType.DMA((n, N_DEV - 1)), pltpu.SemaphoreType.DMA((n, N_DEV - 1)),
                  pltpu.SemaphoreType.DMA((n,))], start, finish)


def _chips_plan(scatter, whole=()):
    ns = len(scatter)
    n = ns + len(whole)

    def copies(ins, outs, sems):
        send_sems, recv_sems, local_sems = sems
        x, y, c = _my_place()
        my_chip = 2 * x + y

        def src(a, k):
            return ins[a].at[k] if a < ns else ins[a]

        local = [pltpu.make_async_copy(src(a, my_chip), outs[a].at[my_chip], local_sems.at[a]) for a in range(n)]
        remote = []
        for a in range(n):
            for j, (px, py) in enumerate(_other_chips(x, y)):
                remote.append(pltpu.make_async_remote_copy(
                    src_ref=src(a, 2 * px + py), dst_ref=outs[a].at[my_chip],
                    send_sem=send_sems.at[a, j], recv_sem=recv_sems.at[a, j],
                    device_id=(px, py, c), device_id_type=MESH))
        return local + remote

    def start(ins, outs, sems):
        for cp in copies(ins, outs, sems):
            cp.start()

    def finish(ins, outs, sems):
        for cp in copies(ins, outs, sems):
            cp.wait()

    shapes = [jax.ShapeDtypeStruct(s.shape, s.dtype) for s in scatter]
    shapes += [jax.ShapeDtypeStruct((4,) + s.shape, s.dtype) for s in whole]
    return _Plan(list(scatter) + list(whole), shapes,
                 [pltpu.SemaphoreType.DMA((n, 3)), pltpu.SemaphoreType.DMA((n, 3)), pltpu.SemaphoreType.DMA((n,))],
                 start, finish)


def _chip_copies(src_ref, land_ref, send_sems, recv_sems):
    x, y, c = _my_place()
    return [pltpu.make_async_remote_copy(
        src_ref=src_ref.at[2 * px + py], dst_ref=land_ref.at[2 * x + y], send_sem=send_sems.at[j],
        recv_sem=recv_sems.at[j], device_id=(px, py, c), device_id_type=MESH)
        for j, (px, py) in enumerate(_other_chips(x, y))]


def _chips_start(src, name):
    def body(src_ref, land_ref, send_sems, recv_sems, src_thru, land_thru, token):
        for cp in _chip_copies(src_ref, land_ref, send_sems, recv_sems):
            cp.start()
        token[...] = jnp.zeros_like(token)

    hbm = pl.BlockSpec(memory_space=pltpu.HBM)
    sem = pl.BlockSpec(memory_space=pltpu.SEMAPHORE)
    return pl.pallas_call(
        body, name=name,
        out_shape=(pltpu.SemaphoreType.DMA((3,)), pltpu.SemaphoreType.DMA((3,)), pltpu.HBM(src.shape, src.dtype),
                   pltpu.HBM(src.shape, src.dtype), jax.ShapeDtypeStruct((8, LANES), F32)),
        in_specs=(hbm, hbm), out_specs=(sem, sem, hbm, hbm, pl.BlockSpec(memory_space=pltpu.VMEM)),
        input_output_aliases={0: 2, 1: 3},
        compiler_params=pltpu.CompilerParams(has_side_effects=pltpu.SideEffectType.DATAFLOW_SIDE_EFFECTING),
    )(pltpu.with_memory_space_constraint(src, pltpu.HBM),
      pltpu.with_memory_space_constraint(jnp.zeros(src.shape, src.dtype), pltpu.HBM))


def _chips_wait(send_sems, recv_sems, src_thru, land_thru, after, name):
    def body(src_ref, land_ref, send_sems, recv_sems, after_ref, src_dead, got_ref):
        for cp in _chip_copies(src_ref, land_ref, send_sems, recv_sems):
            cp.wait_send()
            cp.wait_recv()

    hbm = pl.BlockSpec(memory_space=pltpu.HBM)
    sem = pl.BlockSpec(memory_space=pltpu.SEMAPHORE)
    return pl.pallas_call(
        body, name=name,
        out_shape=(pltpu.HBM(src_thru.shape, src_thru.dtype), pltpu.HBM(land_thru.shape, land_thru.dtype)),
        in_specs=(hbm, hbm, sem, sem, pl.BlockSpec(memory_space=pl.ANY)), out_specs=(hbm, hbm),
        input_output_aliases={0: 0, 1: 1},
        compiler_params=pltpu.CompilerParams(has_side_effects=pltpu.SideEffectType.DATAFLOW_SIDE_EFFECTING),
    )(src_thru, land_thru, send_sems, recv_sems, after)


def _small_update(parts, w, m, v, loss_parts, name):
    n = len(parts)

    def total(ref):
        acc = ref[0]
        for k in range(1, ref.shape[0]):
            acc = acc + ref[k]
        return acc

    def body(*refs):
        p_refs, w_refs, m_refs, v_refs = (refs[i * n:(i + 1) * n] for i in range(4))
        lp_ref = refs[4 * n]
        outs = refs[4 * n + 1:]
        g_refs, d_refs, mo_refs, vo_refs = (outs[i * n:(i + 1) * n] for i in range(4))
        for a in range(n):
            g = total(p_refs[a])
            g_refs[a][...] = g
            d_refs[a][...], mo_refs[a][...], vo_refs[a][...] = _adam_math(w_refs[a][...], g, m_refs[a][...],
                                                                          v_refs[a][...])
        outs[4 * n][...] = total(lp_ref)

    vm = pl.BlockSpec(memory_space=pltpu.VMEM)
    shapes = [jax.ShapeDtypeStruct(t.shape, F32) for _ in range(4) for t in w]
    shapes.append(jax.ShapeDtypeStruct(loss_parts.shape[1:], F32))
    outs = pl.pallas_call(
        body, name=name, out_shape=tuple(shapes),
        in_specs=[vm] * (4 * n + 1), out_specs=tuple([vm] * (4 * n + 1)),
        compiler_params=pltpu.CompilerParams(vmem_limit_bytes=VMEM_LIMIT),
    )(*parts, *w, *m, *v, loss_parts)
    return outs[0:n], outs[n:2 * n], outs[2 * n:3 * n], outs[3 * n:4 * n], outs[4 * n]


BIG = [("w_in", 1), ("w_branch_att", 1), ("w_branch_sg", 1), ("w_out", 0), ("w_xq", 0), ("w_xkv", 1), ("w_xo", 0),
       ("w_ffn_in", 1), ("w_ffn_out", 0)]
SMALL = [("norm_mix_g", (1, D)), ("rel_bias", (8, NREL)), ("sg_ln_g", (8, 64)), ("sg_ln_b", (8, 64)),
         ("sg_w", (8, 128, 128)), ("sg_b", (8, 128)), ("norm_xattn_g", (1, D)), ("norm_mem_g", (1, D)),
         ("norm_ffn_g", (1, D)), ("norm_final_g", (1, D))]
ADAM_ROWS = {"w_in": 192, "w_branch_att": 512, "w_branch_sg": 512, "w_xkv": 1024, "w_ffn_in": 176}


def _full(gathered):
    return gathered.reshape(N_DEV * gathered.shape[1], gathered.shape[2])


def _blocks(grad):
    return grad.reshape(N_DEV, grad.shape[0] // N_DEV, grad.shape[1])


def kernel(x, mem, norm_mix_g, w_in, rel_bias, sg_ln_g, sg_ln_b, sg_w, sg_b, w_branch_att, w_branch_sg, w_out, norm_xattn_g, norm_mem_g, w_xq, w_xkv, w_xo, norm_ffn_g, w_ffn_in, w_ffn_out, norm_final_g, loss_target, m_norm_mix_g, m_w_in, m_rel_bias, m_sg_ln_g, m_sg_ln_b, m_sg_w, m_sg_b, m_w_branch_att, m_w_branch_sg, m_w_out, m_norm_xattn_g, m_norm_mem_g, m_w_xq, m_w_xkv, m_w_xo, m_norm_ffn_g, m_w_ffn_in, m_w_ffn_out, m_norm_final_g, v_norm_mix_g, v_w_in, v_rel_bias, v_sg_ln_g, v_sg_ln_b, v_sg_w, v_sg_b, v_w_branch_att, v_w_branch_sg, v_w_out, v_norm_xattn_g, v_norm_mem_g, v_w_xq, v_w_xkv, v_w_xo, v_norm_ffn_g, v_w_ffn_in, v_w_ffn_out, v_norm_final_g):
    args = dict(locals())
    big_names = [n for n, _ in BIG]
    small_names = [n for n, _ in SMALL]
    S = x.shape[1]

    x, mem, tgt = x.reshape(S, D), mem.reshape(MEM, D), loss_target.reshape(S, D)
    small = {n: args[n].reshape(shape) for n, shape in SMALL}
    g1, g2, g3 = small["norm_mix_g"], small["norm_xattn_g"], small["norm_ffn_g"]
    g_mem, g4 = small["norm_mem_g"], small["norm_final_g"]
    lng = small["sg_ln_g"].reshape(1, SG_W)
    lnb = small["sg_ln_b"].reshape(1, SG_W)
    b_exp = jnp.broadcast_to(small["sg_b"].T[:, :, None], (128, 8, 64)).reshape(128, SG_W)
    rel_pad = jnp.pad(small["rel_bias"], ((0, 0), (0, 384 - NREL)))
    c_idx = lax.axis_index("c").astype(jnp.int32).reshape(1)

    shard = {n: (args[n][0].T if axis == 1 else args[n][0]).astype(BF16) for n, axis in BIG}
    h, w_in_gathered = _norm_in(x, g1, plans=[_gather_plan([shard["w_in"]])])
    w_in_t = _full(w_in_gathered)
    bias = _bias_table(rel_pad)
    mix_names = ["w_branch_att", "w_branch_sg", "w_out", "w_xq", "w_xkv", "w_xo"]
    qkv, uv, gates, *got = _in_proj(h, w_in_t, plans=[_gather_plan([shard[n] for n in mix_names])])
    wba_t, wbs_t, w_out_f, w_xq_f, w_xkv_t, w_xo_f = (_full(g) for g in got)
    y_att, lse, *got = _attn_fwd(qkv, bias, plans=[_gather_plan([shard["w_ffn_in"], shard["w_ffn_out"]])])
    w_ffn_in_t, w_ffn_out_f = (_full(g) for g in got)
    y_sg = _sgu_fwd(uv, lng, lnb, small["sg_w"], b_exp)
    x1 = _merge_fwd(x, y_att, y_sg, gates, wba_t, wbs_t, w_out_f)
    kv, mn = _mem_kv(mem, g_mem, w_xkv_t)
    x2, hx, qx, o_x = _xattn_fwd(x1, g2, w_xq_f, kv, w_xo_f)
    dx3, gu, hf, act, loss_part, dg4 = _ffn_fwd(x2, tgt, g3, w_ffn_in_t, w_ffn_out_f, g4)


    dx2, dgu, dg3 = _ffn_bwd(dx3, gu, x2, g3, w_ffn_out_f, w_ffn_in_t)
    ffn_names = ["w_ffn_out", "w_ffn_in"]
    ffn_mine = [_blocks(_dw(act, dx3, 1408, 1024, DW_TOKENS, "dw_ffn_out")),
                _blocks(_dw(dgu, hf, 1408, 1024, DW_TOKENS, "dw_ffn_in"))]
    dx1, dq_x, dkv, dg2, *ffn_theirs = _xattn_bwd(dx2, x1, qx, g2, w_xq_f, w_xo_f, kv,
                                                  plans=[_sibling_plan(ffn_mine)])
    ffn_chip = _pair_sums(ffn_mine, ffn_theirs, c_idx, "rs_pair_ffn")
    d_xkv, dg_mem = _mem_kv_bwd(dkv, mem, g_mem, mn, w_xkv_t)
    merged, d_a, d_b, dy_att, dy_sg, dgates = _merge_bwd(dx1, y_att, y_sg, gates, wba_t, wbs_t, w_out_f)
    mid_names = ["w_xo", "w_xq", "w_xkv", "w_out", "w_branch_att", "w_branch_sg"]
    mid_mine = [_blocks(g) for g in (
        _dw(o_x, dx2, 1024, 1024, DW_TOKENS, "dw_xo"), _dw(hx, dq_x, 1024, 1024, DW_TOKENS, "dw_xq"), d_xkv,
        _dw(merged, dx1, 1024, 1024, DW_TOKENS, "dw_out"), _dw(d_a, y_att, 1024, 512, DW_TOKENS, "dw_branch_att"),
        _dw(d_b, y_sg, 1024, 512, DW_TOKENS, "dw_branch_sg"))]
    duv, d_sgw, d_bx, d_lng, d_lnb, *got = _sgu_bwd(uv, dy_sg, lng, lnb, small["sg_w"], b_exp,
                                                    plans=[_chips_plan(ffn_chip), _sibling_plan(mid_mine)])
    ffn_all, mid_theirs = got[:2], got[2:]
    mid_chip = _pair_sums(mid_mine, mid_theirs, c_idx, "rs_pair_mid")
    dq, dk, dv, ds_sum, *mid_all = _attn_bwd(qkv, dy_att, y_att, lse, bias, plans=[_chips_plan(mid_chip)])
    d_rel = _bias_grad(ds_sum)

    grad_x, dz, dg1 = _in_bwd(dq, dk, dv, duv, dgates, x, dx1, g1, w_in_t)

    gs = {"norm_mix_g": dg1, "rel_bias": d_rel[:, :NREL], "sg_ln_g": d_lng.reshape(8, 64),
          "sg_ln_b": d_lnb.reshape(8, 64), "sg_w": d_sgw, "sg_b": d_bx.reshape(128, 8, 64)[:, :, 0].T,
          "norm_xattn_g": dg2, "norm_mem_g": dg_mem, "norm_ffn_g": dg3, "norm_final_g": dg4}
    d_in, *everyone = _dw(dz, h, 1152, 1024, DW_TOKENS, "dw_in",
                          plans=[_peers_plan([gs[n] for n in small_names] + [loss_part])])

    in_mine = [_blocks(d_in)]
    (in_theirs,) = _run_plan(_sibling_plan(in_mine), "rs_sibling")
    (in_chip,) = _pair_sums(in_mine, [in_theirs], c_idx, "rs_pair_w_in")
    send_sems, recv_sems, in_chip_thru, landing, token = _chips_start(in_chip, "rs_chips_start")
    all_parts = dict(zip(ffn_names + mid_names, list(ffn_all) + list(mid_all)))

    def adam(n, axis, parts, after):
        wmv = [args[p + n][0] for p in ("", "m_", "v_")]
        if axis == 1 and wmv[0].shape[1] % LANES != 0:
            outs = _adam(parts, *(t.T for t in wmv), ADAM_ROWS[n], "adam_" + n, transposed=False, after=after)
            return [t.T[None] for t in outs]
        tr = ADAM_ROWS[n] if axis == 1 else wmv[0].shape[0]
        return [t[None] for t in _adam(parts, *wmv, tr, "adam_" + n, transposed=(axis == 1), after=after)]

    res = {n: adam(n, axis, all_parts[n], token) for n, axis in BIG if n != "w_in"}
    in_chip, landing = _chips_wait(send_sems, recv_sems, in_chip_thru, landing, res["w_ffn_in"][1], "rs_chips_wait")
    my_chip = 2 * lax.axis_index("x") + lax.axis_index("y")
    own = lax.dynamic_slice_in_dim(in_chip, my_chip, 1, axis=0)
    res["w_in"] = adam("w_in", 1, lax.dynamic_update_slice_in_dim(landing, own, my_chip, axis=0), None)
    small_res = _small_update(
        everyone[:-1], [small[n] for n in small_names],
        [args["m_" + n].reshape(s) for n, s in SMALL], [args["v_" + n].reshape(s) for n, s in SMALL],
        everyone[-1], "adam_small")
    for i, n in enumerate(small_names):
        res[n] = [small_res[k][i].reshape(args[n].shape) for k in range(4)]
    loss = small_res[4][0, 0]

    order = ["norm_mix_g", "w_in", "rel_bias", "sg_ln_g", "sg_ln_b", "sg_w", "sg_b", "w_branch_att", "w_branch_sg",
             "w_out", "norm_xattn_g", "norm_mem_g", "w_xq", "w_xkv", "w_xo", "norm_ffn_g", "w_ffn_in", "w_ffn_out",
             "norm_final_g"]
    outs = [loss, grad_x.reshape(1, S, D)]
    for k in range(4):
        outs += [res[n][k] for n in order]
    return tuple(outs)
```

```python
import math

import jax
import jax.numpy as jnp
from jax import lax
from jax.experimental import pallas as pl
from jax.experimental.pallas import tpu as pltpu

F32 = jnp.float32
BF16 = jnp.bfloat16

D = 1024
ATT_W = 512
SG_W = 512
IN_COLS = 4608
DFF = 2816
MEM = 256
XH = 4
CHUNK = 64
BAND_KEYS = 640
ATT_R = 4096
ATT_SUBS = ATT_R // 128
START_SUBS = 8 * CHUNK // 128
ATT_ROWS = 64
ATT_ROWS_BWD = 32
DW_TOKENS = 2048
IN_CHUNK = 512
FF_CHUNK = 256
REL_CLIP = 128
NREL = 2 * REL_CLIP + 1
EPS = 1e-6
NEG = -1e30
N_DEV = 8

ADAM_LR = 0.001
ADAM_B1 = 0.9
ADAM_B2 = 0.999
ADAM_EPS = 1e-08
ADAM_WD = 0.01
ADAM_STEP = 10

LANES = 128
VMEM_LIMIT = 56 * 1024 * 1024
MESH = pl.DeviceIdType.MESH


def _cparams(n_axes):
    return pltpu.CompilerParams(dimension_semantics=("arbitrary",) * n_axes, vmem_limit_bytes=VMEM_LIMIT)


def _resident(shape):
    zeros = (0,) * len(shape)
    return pl.BlockSpec(shape, lambda *_: zeros, pipeline_mode=pl.Buffered(1))


def _rows(tm, cols, col_block=0):
    return pl.BlockSpec((tm, cols), lambda i: (i, col_block))


def _sigmoid(x):
    return pl.reciprocal(1.0 + jnp.exp(-x), approx=True)


_GELU_C = math.sqrt(2.0 / math.pi)


def _gelu(x):
    t = jnp.tanh(_GELU_C * (x + 0.044715 * (x * x * x)))
    return x * (0.5 * (1.0 + t))


def _gelu_and_grad(x):
    x2 = x * x
    t = jnp.tanh(_GELU_C * (x + 0.044715 * (x2 * x)))
    cdf = 0.5 * (1.0 + t)
    dcdf = 0.5 * (1.0 - t * t) * (_GELU_C * (1.0 + 3.0 * 0.044715 * x2))
    return x * cdf, cdf + x * dcdf


def _rstd(x):
    return lax.rsqrt(jnp.mean(x * x, axis=-1, keepdims=True) + EPS)


def _rms_bwd(dh, x, r, g):
    xh = x * r
    dxh = dh * g
    dx = r * (dxh - xh * jnp.mean(dxh * xh, axis=-1, keepdims=True))
    dg = jnp.sum(dh * xh, axis=0, keepdims=True)
    return dx, dg


def _group_sum64(x):
    r = lax.broadcasted_iota(jnp.int32, (LANES, LANES), 0) // 64
    c = lax.broadcasted_iota(jnp.int32, (LANES, LANES), 1) // 64
    same_group = (r == c).astype(BF16)

    def one(v):
        hi = v.astype(BF16)
        rest = v - hi.astype(F32)
        mid = rest.astype(BF16)
        lo = (rest - mid.astype(F32)).astype(BF16)
        return _dot(hi, same_group) + _dot(mid, same_group) + _dot(lo, same_group)

    pieces = [one(x[:, LANES * j:LANES * (j + 1)]) for j in range(x.shape[1] // LANES)]
    return pieces[0] if len(pieces) == 1 else jnp.concatenate(pieces, axis=1)


def _dot(a, b):
    return jnp.dot(a, b, preferred_element_type=F32)


def _dot_nt(a, b):
    return lax.dot_general(a, b, (((1,), (1,)), ((), ())), preferred_element_type=F32)


def _dot_tn(a, b):
    return lax.dot_general(a, b, (((0,), (0,)), ((), ())), preferred_element_type=F32)


DIAGS = 768


def _diag_onehot():
    r_idx = lax.broadcasted_iota(jnp.int32, (384, DIAGS), 0)
    t_idx = lax.broadcasted_iota(jnp.int32, (384, DIAGS), 1)
    dist = (8 * CHUNK + 127) - t_idx
    return (jnp.clip(dist, -REL_CLIP, REL_CLIP) + REL_CLIP == r_idx).astype(F32)


def _shift_rows(x, reverse):
    row = lax.broadcasted_iota(jnp.int32, x.shape, 0)
    for k in range(7):
        amt = (DIAGS - (1 << k)) if reverse else (1 << k)
        x = jnp.where(((row >> k) & 1) == 1, pltpu.roll(x, amt, 1), x)
    return x


N_TABLES = 1 + START_SUBS


def _bias_table(rel_bias_pad):
    def body(rb_ref, out_ref):
        per_diag = jnp.dot(rb_ref[...], _diag_onehot(), preferred_element_type=F32,
                           precision=lax.Precision.HIGHEST)
        a = lax.broadcasted_iota(jnp.int32, (128, BAND_KEYS), 0)
        b = lax.broadcasted_iota(jnp.int32, (128, BAND_KEYS), 1)
        band = (b // CHUNK >= a // CHUNK) & (b // CHUNK <= a // CHUNK + 8)
        for h in range(8):
            rows = jnp.broadcast_to(per_diag[h:h + 1, :], (128, DIAGS))
            table = _shift_rows(pltpu.roll(rows, DIAGS - 127, 1), reverse=False)[:, :BAND_KEYS]
            out_ref[0, h] = jnp.where(band, table, NEG)
            for s in range(START_SUBS):
                out_ref[1 + s, h] = jnp.where(band & (b >= 8 * CHUNK - 128 * s), table, NEG)

    return pl.pallas_call(
        body, name="bias_table",
        out_shape=jax.ShapeDtypeStruct((N_TABLES, 8, 128, BAND_KEYS), F32),
        in_specs=[pl.BlockSpec(memory_space=pltpu.VMEM)],
        out_specs=pl.BlockSpec(memory_space=pltpu.VMEM),
        compiler_params=pltpu.CompilerParams(vmem_limit_bytes=VMEM_LIMIT),
    )(rel_bias_pad)


def _bias_grad(ds_sum):
    def body(ds_ref, out_ref):
        sums = []
        for h in range(8):
            padded = jnp.concatenate([ds_ref[h], jnp.zeros((128, DIAGS - BAND_KEYS), F32)], axis=1)
            skewed = pltpu.roll(_shift_rows(padded, reverse=True), 127, 1)
            sums.append(jnp.sum(skewed, axis=0, keepdims=True))
        per_diag = jnp.concatenate(sums, axis=0)
        out_ref[...] = lax.dot_general(per_diag, _diag_onehot(), (((1,), (1,)), ((), ())),
                                       preferred_element_type=F32, precision=lax.Precision.HIGHEST)

    return pl.pallas_call(
        body, name="bias_grad",
        out_shape=jax.ShapeDtypeStruct((8, 384), F32),
        in_specs=[pl.BlockSpec(memory_space=pltpu.VMEM)],
        out_specs=pl.BlockSpec(memory_space=pltpu.VMEM),
    )(ds_sum)


def _norm_in(x, g1, plans=(), tm=512):
    S = x.shape[0]

    def body(x_ref, g_ref, h_ref):
        xv = x_ref[...]
        h_ref[...] = (xv * _rstd(xv) * g_ref[...]).astype(BF16)

    return _call(
        body, name="norm_in", grid=(S // tm,),
        out_shape=(jax.ShapeDtypeStruct((S, D), BF16),),
        in_specs=[_rows(tm, D), _resident((1, D))], out_specs=(_rows(tm, D),),
        operands=(x, g1), plans=plans)


def _in_proj(h, w_in_t, plans=(), tm=1024):
    S = h.shape[0]

    def body(h_ref, w_ref, qkv_ref, uv_ref, gate_ref):
        h = h_ref[...]
        w = IN_CHUNK
        for c in range(IN_COLS // w):
            zc = _dot_nt(h, w_ref[w * c:w * (c + 1), :])
            start = w * c
            if start < ATT_W:
                qkv_ref[:, start:start + w] = (zc * 0.125).astype(BF16)
            elif start < 3 * ATT_W:
                qkv_ref[:, start:start + w] = zc.astype(BF16)
            elif start < 3 * ATT_W + 2 * SG_W:
                uv_ref[:, start - 3 * ATT_W:start - 3 * ATT_W + w] = zc.astype(BF16)
            else:
                gate_ref[:, start - 3 * ATT_W - 2 * SG_W:start - 3 * ATT_W - 2 * SG_W + w] = zc.astype(BF16)

    return _call(
        body, name="in_proj", grid=(S // tm,),
        out_shape=(jax.ShapeDtypeStruct((S, 3 * ATT_W), BF16), jax.ShapeDtypeStruct((S, 2 * SG_W), BF16),
                   jax.ShapeDtypeStruct((S, 2 * D), BF16)),
        in_specs=[_rows(tm, D), _resident((IN_COLS, D))],
        out_specs=(_rows(tm, 3 * ATT_W), _rows(tm, 2 * SG_W), _rows(tm, 2 * D)),
        operands=(h, w_in_t), plans=plans)


def _two_heads(a, lo):
    return jnp.concatenate([jnp.where(lo, a, 0), jnp.where(lo, 0, a)], axis=0)


def _att_specs():
    R = ATT_R
    q = pl.BlockSpec((R, LANES), lambda j, i: (i, j))
    kp = pl.BlockSpec((R, LANES), lambda j, i: (jnp.maximum(i - 1, 0), 4 + j))
    kc = pl.BlockSpec((R, LANES), lambda j, i: (i, 4 + j))
    vp = pl.BlockSpec((R, LANES), lambda j, i: (jnp.maximum(i - 1, 0), 8 + j))
    vc = pl.BlockSpec((R, LANES), lambda j, i: (i, 8 + j))
    bias = pl.BlockSpec((N_TABLES, 2, 128, BAND_KEYS), lambda j, i: (0, j, 0, 0))
    return [q, kp, kc, vp, vc, bias]


def _attn_fwd(qkv, bias, plans=()):
    S = qkv.shape[0]
    R = ATT_R

    def body(q_ref, kp_ref, kc_ref, vp_ref, vc_ref, b_ref, o_ref, lse_ref, s_ref, p_ref, l_ref):
        i = pl.program_id(1)
        lo = lax.broadcasted_iota(jnp.int32, (1, LANES), 1) < 64
        kwin = jnp.concatenate([kp_ref[R - 8 * CHUNK:R, :], kc_ref[...]], axis=0)
        vwin = jnp.concatenate([vp_ref[R - 8 * CHUNK:R, :], vc_ref[...]], axis=0)
        for sub in range(ATT_SUBS):
            q2 = q_ref[128 * sub:128 * (sub + 1), :]
            kw = kwin[128 * sub:128 * sub + BAND_KEYS]
            vw = vwin[128 * sub:128 * sub + BAND_KEYS]
            table = jnp.where(i == 0, 1 + sub, 0) if sub < START_SUBS else 0
            s_ref[...] = _dot_nt(_two_heads(q2, lo), kw)
            for hh in range(2):
                for r0 in range(0, 128, ATT_ROWS):
                    rows = slice(128 * hh + r0, 128 * hh + r0 + ATT_ROWS)
                    s = s_ref[rows, :] + b_ref[table, hh, r0:r0 + ATT_ROWS, :]
                    top = jnp.max(s, axis=-1, keepdims=True)
                    p = jnp.exp(s - top)
                    total = jnp.sum(p, axis=-1, keepdims=True)
                    p_ref[rows, :] = (p / total).astype(BF16)
                    l_ref[rows, :] = jnp.broadcast_to(top + jnp.log(total), (ATT_ROWS, LANES))
            o = _dot(jnp.concatenate([p_ref[0:128, :], p_ref[128:256, :]], axis=1), _two_heads(vw, lo))
            o_ref[128 * sub:128 * (sub + 1), :] = o.astype(BF16)
            lse_ref[128 * sub:128 * (sub + 1), :] = jnp.where(lo, l_ref[0:128, :], l_ref[128:256, :])

    blk = pl.BlockSpec((R, LANES), lambda j, i: (i, j))
    return _call(
        body, name="attn_fwd", grid=(4, S // R),
        out_shape=(jax.ShapeDtypeStruct((S, ATT_W), BF16), jax.ShapeDtypeStruct((S, ATT_W), F32)),
        in_specs=_att_specs(), out_specs=(blk, blk),
        scratch_shapes=[pltpu.VMEM((256, BAND_KEYS), F32), pltpu.VMEM((256, BAND_KEYS), BF16),
                        pltpu.VMEM((256, LANES), F32)],
        operands=(qkv, qkv, qkv, qkv, qkv, bias), plans=plans)


def _sg_mask():
    t = lax.broadcasted_iota(jnp.int32, (128, 128), 0)
    s = lax.broadcasted_iota(jnp.int32, (128, 128), 1)
    return (s // CHUNK) <= (t // CHUNK)


def _sg_layernorm(gv, lng, lnb):
    mu = _group_sum64(gv) * (1.0 / 64)
    xc = gv - mu
    var = _group_sum64(xc * xc) * (1.0 / 64)
    rstd = lax.rsqrt(var + EPS)
    vhat = xc * rstd
    return vhat * lng + lnb, vhat, rstd


def _sgu_fwd(uv, lng, lnb, sg_w, b_exp, tm=1024):
    S = uv.shape[0]

    def body(uv_ref, lng_ref, lnb_ref, w_ref, b_ref, y_ref):
        lane = lax.broadcasted_iota(jnp.int32, (1, LANES), 1)
        lo = lane < 64
        mask = _sg_mask()
        gu = _gelu(uv_ref[:, 0:SG_W].astype(F32))
        vln, _, _ = _sg_layernorm(_gelu(uv_ref[:, SG_W:2 * SG_W].astype(F32)), lng_ref[...], lnb_ref[...])
        for gp in range(4):
            w0 = jnp.where(mask, w_ref[2 * gp], 0).astype(BF16)
            w1 = jnp.where(mask, w_ref[2 * gp + 1], 0).astype(BF16)
            cols = slice(128 * gp, 128 * (gp + 1))
            for n in range(tm // 128):
                rows = slice(128 * n, 128 * (n + 1))
                vl = vln[rows, cols]
                sv = (_dot(w0, jnp.where(lo, vl, 0).astype(BF16)) + _dot(w1, jnp.where(lo, 0, vl).astype(BF16))
                      + b_ref[:, cols])
                y_ref[rows, cols] = (gu[rows, cols] * sv).astype(BF16)

    return pl.pallas_call(
        body, name="sgu_fwd", grid=(S // tm,),
        out_shape=jax.ShapeDtypeStruct((S, SG_W), BF16),
        in_specs=[_rows(tm, 2 * SG_W), _resident((1, SG_W)), _resident((1, SG_W)),
                  _resident((8, 128, 128)), _resident((128, SG_W))],
        out_specs=_rows(tm, SG_W),
        compiler_params=_cparams(1),
    )(uv, lng, lnb, sg_w, b_exp)


def _merge_fwd(x, y_att, y_sg, gates, wba_t, wbs_t, w_out, tm=512):
    S = x.shape[0]

    def body(x_ref, ya_ref, ys_ref, g_ref, wba_ref, wbs_ref, wo_ref, x1_ref):
        a = _dot_nt(ya_ref[...], wba_ref[...])
        b = _dot_nt(ys_ref[...], wbs_ref[...])
        merged = _sigmoid(g_ref[:, 0:D].astype(F32)) * a + _sigmoid(g_ref[:, D:2 * D].astype(F32)) * b
        x1_ref[...] = x_ref[...] + _dot(merged.astype(BF16), wo_ref[...])

    return pl.pallas_call(
        body, name="merge_fwd", grid=(S // tm,),
        out_shape=jax.ShapeDtypeStruct((S, D), F32),
        in_specs=[_rows(tm, D), _rows(tm, ATT_W), _rows(tm, SG_W), _rows(tm, 2 * D),
                  _resident((D, ATT_W)), _resident((D, SG_W)), _resident((D, D))],
        out_specs=_rows(tm, D),
        compiler_params=_cparams(1),
    )(x, y_att, y_sg, gates, wba_t, wbs_t, w_out)


def _mem_kv(mem, g_mem, w_xkv_t):
    def body(m_ref, g_ref, w_ref, kv_ref, mn_ref):
        mv = m_ref[...]
        mn = (mv * _rstd(mv) * g_ref[...]).astype(BF16)
        mn_ref[...] = mn
        kv_ref[...] = _dot_nt(mn, w_ref[...]).astype(BF16)

    vm = pl.BlockSpec(memory_space=pltpu.VMEM)
    return pl.pallas_call(
        body, name="mem_kv",
        out_shape=(jax.ShapeDtypeStruct((MEM, 2 * D), BF16), jax.ShapeDtypeStruct((MEM, D), BF16)),
        in_specs=[vm, vm, vm], out_specs=(vm, vm),
        compiler_params=pltpu.CompilerParams(vmem_limit_bytes=VMEM_LIMIT),
    )(mem, g_mem, w_xkv_t)


def _xatt_head(qx, kv_ref, h):
    hs = slice(256 * h, 256 * (h + 1))
    s = _dot_nt(qx[:, hs], kv_ref[:, hs])
    p = jnp.exp(s - jnp.max(s, axis=-1, keepdims=True))
    return p / jnp.sum(p, axis=-1, keepdims=True)


def _xattn_fwd(x1, g2, w_xq, kv, w_xo, tm=1024):
    S = x1.shape[0]

    def body(x_ref, g_ref, wq_ref, kv_ref, wo_ref, x2_ref, hx_ref, qx_ref, o_ref):
        xv = x_ref[...]
        hx = (xv * _rstd(xv) * g_ref[...]).astype(BF16)
        hx_ref[...] = hx
        qx = (_dot(hx, wq_ref[...]) * (1.0 / 16)).astype(BF16)
        qx_ref[...] = qx
        for h in range(XH):
            p = _xatt_head(qx, kv_ref, h)
            o_ref[:, 256 * h:256 * (h + 1)] = _dot(p.astype(BF16), kv_ref[:, D + 256 * h:D + 256 * (h + 1)]).astype(BF16)
        x2_ref[...] = xv + _dot(o_ref[...], wo_ref[...])

    return pl.pallas_call(
        body, name="xattn_fwd", grid=(S // tm,),
        out_shape=(jax.ShapeDtypeStruct((S, D), F32),) + (jax.ShapeDtypeStruct((S, D), BF16),) * 3,
        in_specs=[_rows(tm, D), _resident((1, D)), _resident((D, D)), _resident((MEM, 2 * D)), _resident((D, D))],
        out_specs=(_rows(tm, D),) * 4,
        compiler_params=_cparams(1),
    )(x1, g2, w_xq, kv, w_xo)


def _ffn_fwd(x2, tgt, g3, w_ffn_in_t, w_ffn_out, g4, tm=256):
    S = x2.shape[0]

    def body(x_ref, t_ref, g3_ref, wi_ref, wo_ref, g4_ref, dx3_ref, gu_ref, hf_ref, act_ref, loss_ref, dg4_ref):
        i = pl.program_id(0)
        xv = x_ref[...]
        hf = (xv * _rstd(xv) * g3_ref[...]).astype(BF16)
        hf_ref[...] = hf
        for c in range(DFF // FF_CHUNK):
            cs = slice(FF_CHUNK * c, FF_CHUNK * (c + 1))
            us = slice(DFF + FF_CHUNK * c, DFF + FF_CHUNK * (c + 1))
            gate = _dot_nt(hf, wi_ref[cs, :])
            up = _dot_nt(hf, wi_ref[us, :])
            gu_ref[:, cs] = gate.astype(BF16)
            gu_ref[:, us] = up.astype(BF16)
            act_ref[:, cs] = ((gate * _sigmoid(gate)) * up).astype(BF16)
        acc = xv + _dot(act_ref[...], wo_ref[...])
        r4 = _rstd(acc)
        g4 = g4_ref[...]
        diff = acc * r4 * g4 - t_ref[...]
        dy = diff * (1.0 / D)
        dx3, dg4 = _rms_bwd(dy, acc, r4, g4)
        dx3_ref[...] = dx3
        part = 0.5 * jnp.sum(jnp.mean(diff * diff, axis=-1, keepdims=True))

        @pl.when(i == 0)
        def _():
            loss_ref[...] = jnp.zeros_like(loss_ref)
            dg4_ref[...] = jnp.zeros_like(dg4_ref)

        loss_ref[...] += jnp.full(loss_ref.shape, part, F32)
        dg4_ref[...] += dg4

    return pl.pallas_call(
        body, name="ffn_fwd", grid=(S // tm,),
        out_shape=(jax.ShapeDtypeStruct((S, D), F32), jax.ShapeDtypeStruct((S, 2 * DFF), BF16),
                   jax.ShapeDtypeStruct((S, D), BF16), jax.ShapeDtypeStruct((S, DFF), BF16),
                   jax.ShapeDtypeStruct((8, LANES), F32), jax.ShapeDtypeStruct((1, D), F32)),
        in_specs=[_rows(tm, D), _rows(tm, D), _resident((1, D)), _resident((2 * DFF, D)), _resident((DFF, D)),
                  _resident((1, D))],
        out_specs=(_rows(tm, D), _rows(tm, 2 * DFF), _rows(tm, D), _rows(tm, DFF),
                   pl.BlockSpec((8, LANES), lambda i: (0, 0)), pl.BlockSpec((1, D), lambda i: (0, 0))),
        compiler_params=_cparams(1),
    )(x2, tgt, g3, w_ffn_in_t, w_ffn_out, g4)


def _ffn_bwd(dx3, gu, x2, g3, w_ffn_out, w_ffn_in_t, tm=256):
    S = x2.shape[0]

    def body(d_ref, gu_ref, x_ref, g3_ref, wo_ref, wit_ref, dx2_ref, dgu_ref, dg3_ref):
        i = pl.program_id(0)
        d3 = d_ref[...]
        d3b = d3.astype(BF16)
        for c in range(DFF // FF_CHUNK):
            cs = slice(FF_CHUNK * c, FF_CHUNK * (c + 1))
            us = slice(DFF + FF_CHUNK * c, DFF + FF_CHUNK * (c + 1))
            da = _dot_nt(d3b, wo_ref[cs, :])
            gate = gu_ref[:, cs].astype(F32)
            up = gu_ref[:, us].astype(F32)
            sg = _sigmoid(gate)
            dgate = (da * up * (sg * (1.0 + gate * (1.0 - sg)))).astype(BF16)
            dup = (da * (gate * sg)).astype(BF16)
            dgu_ref[:, cs] = dgate
            dgu_ref[:, us] = dup
        dhf = _dot(dgu_ref[...], wit_ref[...])
        xv = x_ref[...]
        dx, dg3 = _rms_bwd(dhf, xv, _rstd(xv), g3_ref[...])
        dx2_ref[...] = d3 + dx

        @pl.when(i == 0)
        def _():
            dg3_ref[...] = jnp.zeros_like(dg3_ref)

        dg3_ref[...] += dg3

    return pl.pallas_call(
        body, name="ffn_bwd", grid=(S // tm,),
        out_shape=(jax.ShapeDtypeStruct((S, D), F32), jax.ShapeDtypeStruct((S, 2 * DFF), BF16),
                   jax.ShapeDtypeStruct((1, D), F32)),
        in_specs=[_rows(tm, D), _rows(tm, 2 * DFF), _rows(tm, D), _resident((1, D)),
                  _resident((DFF, D)), _resident((2 * DFF, D))],
        out_specs=(_rows(tm, D), _rows(tm, 2 * DFF), pl.BlockSpec((1, D), lambda i: (0, 0))),
        compiler_params=_cparams(1),
    )(dx3, gu, x2, g3, w_ffn_out, w_ffn_in_t)


def _dw(a, b, tmm, tn, ts, name, out_dtype=BF16, plans=()):
    S, M = a.shape
    N = b.shape[1]
    ts = min(ts, S)
    nk = S // ts

    def body(a_ref, b_ref, o_ref, acc_ref):
        k = pl.program_id(2)

        @pl.when(k == 0)
        def _():
            acc_ref[...] = jnp.zeros_like(acc_ref)

        acc_ref[...] += _dot_tn(a_ref[...].astype(BF16), b_ref[...].astype(BF16))

        @pl.when(k == nk - 1)
        def _():
            o_ref[...] = acc_ref[...].astype(out_dtype)

    out = _call(
        body, name=name, grid=(M // tmm, N // tn, nk),
        out_shape=(jax.ShapeDtypeStruct((M, N), out_dtype),),
        in_specs=[pl.BlockSpec((ts, tmm), lambda m, n, k: (k, m)), pl.BlockSpec((ts, tn), lambda m, n, k: (k, n))],
        out_specs=(pl.BlockSpec((tmm, tn), lambda m, n, k: (m, n)),),
        scratch_shapes=[pltpu.VMEM((tmm, tn), F32)],
        operands=(a, b), plans=plans)
    return out if plans else out[0]


def _xattn_bwd(dx2, x1, qx, g2, w_xq, w_xo, kv, plans=(), tm=512):
    S = x1.shape[0]

    def body(d_ref, x_ref, qx_ref, g_ref, wq_ref, wo_ref, kv_ref, dx1_ref, dq_ref, dkv_ref, dg2_ref):
        i = pl.program_id(0)

        @pl.when(i == 0)
        def _():
            dkv_ref[...] = jnp.zeros_like(dkv_ref)
            dg2_ref[...] = jnp.zeros_like(dg2_ref)

        d2 = d_ref[...]
        qx = qx_ref[...]
        do = _dot_nt(d2.astype(BF16), wo_ref[...]).astype(BF16)
        for h in range(XH):
            hs = slice(256 * h, 256 * (h + 1))
            vs = slice(D + 256 * h, D + 256 * (h + 1))
            p = _xatt_head(qx, kv_ref, h)
            dp = _dot_nt(do[:, hs], kv_ref[:, vs])
            ds = (p * (dp - jnp.sum(dp * p, axis=-1, keepdims=True))).astype(BF16)
            dq_ref[:, hs] = (_dot(ds, kv_ref[:, hs]) * (1.0 / 16)).astype(BF16)
            dkv_ref[:, hs] += _dot_tn(ds, qx[:, hs])
            dkv_ref[:, vs] += _dot_tn(p.astype(BF16), do[:, hs])
        dhx = _dot_nt(dq_ref[...], wq_ref[...])
        xv = x_ref[...]
        dx, dg2 = _rms_bwd(dhx, xv, _rstd(xv), g_ref[...])
        dx1_ref[...] = d2 + dx
        dg2_ref[...] += dg2

    return _call(
        body, name="xattn_bwd", grid=(S // tm,),
        out_shape=(jax.ShapeDtypeStruct((S, D), F32), jax.ShapeDtypeStruct((S, D), BF16),
                   jax.ShapeDtypeStruct((MEM, 2 * D), F32), jax.ShapeDtypeStruct((1, D), F32)),
        in_specs=[_rows(tm, D), _rows(tm, D), _rows(tm, D), _resident((1, D)), _resident((D, D)), _resident((D, D)),
                  _resident((MEM, 2 * D))],
        out_specs=(_rows(tm, D), _rows(tm, D),
                   pl.BlockSpec((MEM, 2 * D), lambda i: (0, 0)), pl.BlockSpec((1, D), lambda i: (0, 0))),
        operands=(dx2, x1, qx, g2, w_xq, w_xo, kv), plans=plans)


def _mem_kv_bwd(dkv, mem, g_mem, mn, w_xkv_t):
    def body(dkv_ref, m_ref, g_ref, mn_ref, wt_ref, dw_ref, dg_ref):
        dkvb = dkv_ref[...].astype(BF16)
        dw_ref[...] = _dot_tn(dkvb, mn_ref[...]).astype(BF16)
        dmn = _dot(dkvb, wt_ref[...])
        mv = m_ref[...]
        dg_ref[...] = jnp.sum(dmn * (mv * _rstd(mv)), axis=0, keepdims=True)

    vm = pl.BlockSpec(memory_space=pltpu.VMEM)
    return pl.pallas_call(
        body, name="mem_kv_bwd",
        out_shape=(jax.ShapeDtypeStruct((2 * D, D), BF16), jax.ShapeDtypeStruct((1, D), F32)),
        in_specs=[vm] * 5, out_specs=(vm, vm),
        compiler_params=pltpu.CompilerParams(vmem_limit_bytes=VMEM_LIMIT),
    )(dkv, mem, g_mem, mn, w_xkv_t)


def _merge_bwd(dx1, y_att, y_sg, gates, wba_t, wbs_t, w_out, tm=512):
    S = dx1.shape[0]

    def body(d_ref, ya_ref, ys_ref, g_ref, wbat_ref, wbst_ref, wo_ref,
             mg_ref, da_ref, db_ref, dya_ref, dys_ref, dg_ref):
        dm = _dot_nt(d_ref[...].astype(BF16), wo_ref[...])
        a = _dot_nt(ya_ref[...], wbat_ref[...])
        b = _dot_nt(ys_ref[...], wbst_ref[...])
        sa = _sigmoid(g_ref[:, 0:D].astype(F32))
        sb = _sigmoid(g_ref[:, D:2 * D].astype(F32))
        mg_ref[...] = (sa * a + sb * b).astype(BF16)
        da = (dm * sa).astype(BF16)
        db = (dm * sb).astype(BF16)
        da_ref[...] = da
        db_ref[...] = db
        dg_ref[:, 0:D] = (dm * a * sa * (1.0 - sa)).astype(BF16)
        dg_ref[:, D:2 * D] = (dm * b * sb * (1.0 - sb)).astype(BF16)
        dya_ref[...] = _dot(da, wbat_ref[...]).astype(BF16)
        dys_ref[...] = _dot(db, wbst_ref[...]).astype(BF16)

    return pl.pallas_call(
        body, name="merge_bwd", grid=(S // tm,),
        out_shape=(jax.ShapeDtypeStruct((S, D), BF16), jax.ShapeDtypeStruct((S, D), BF16),
                   jax.ShapeDtypeStruct((S, D), BF16), jax.ShapeDtypeStruct((S, ATT_W), BF16),
                   jax.ShapeDtypeStruct((S, SG_W), BF16), jax.ShapeDtypeStruct((S, 2 * D), BF16)),
        in_specs=[_rows(tm, D), _rows(tm, ATT_W), _rows(tm, SG_W), _rows(tm, 2 * D),
                  _resident((D, ATT_W)), _resident((D, SG_W)), _resident((D, D))],
        out_specs=(_rows(tm, D), _rows(tm, D), _rows(tm, D), _rows(tm, ATT_W), _rows(tm, SG_W), _rows(tm, 2 * D)),
        compiler_params=_cparams(1),
    )(dx1, y_att, y_sg, gates, wba_t, wbs_t, w_out)


def _sgu_bwd(uv, dy_sg, lng, lnb, sg_w, b_exp, plans=(), tm=1024):
    S = uv.shape[0]
    n_steps = S // tm

    def body(uv_ref, dy_ref, lng_ref, lnb_ref, w_ref, b_ref, duv_ref, dw_ref, dbx_ref, dlng_ref, dlnb_ref, dvln_ref):
        i = pl.program_id(0)

        @pl.when(i == 0)
        def _():
            dw_ref[...] = jnp.zeros_like(dw_ref)
            dbx_ref[...] = jnp.zeros_like(dbx_ref)
            dlng_ref[...] = jnp.zeros_like(dlng_ref)
            dlnb_ref[...] = jnp.zeros_like(dlnb_ref)

        lane = lax.broadcasted_iota(jnp.int32, (1, LANES), 1)
        lo = lane < 64
        mask = _sg_mask()
        lng = lng_ref[...]
        gu, dgelu_u = _gelu_and_grad(uv_ref[:, 0:SG_W].astype(F32))
        gv, dgelu_v = _gelu_and_grad(uv_ref[:, SG_W:2 * SG_W].astype(F32))
        vln, vhat, rstd = _sg_layernorm(gv, lng, lnb_ref[...])
        dy = dy_ref[...].astype(F32)
        dsv_all = dy * gu
        for gp in range(4):
            wf0 = jnp.where(mask, w_ref[2 * gp], 0)
            wf1 = jnp.where(mask, w_ref[2 * gp + 1], 0)
            w0 = wf0.astype(BF16)
            w1 = wf1.astype(BF16)
            cols = slice(128 * gp, 128 * (gp + 1))
            dw0 = jnp.zeros((128, 128), F32)
            dw1 = jnp.zeros((128, 128), F32)
            dbx = jnp.zeros((128, LANES), F32)
            for n in range(tm // 128):
                rows = slice(128 * n, 128 * (n + 1))
                vl = vln[rows, cols]
                vl0 = jnp.where(lo, vl, 0).astype(BF16)
                vl1 = jnp.where(lo, 0, vl).astype(BF16)
                sv = _dot(w0, vl0) + _dot(w1, vl1) + b_ref[:, cols]
                duv_ref[rows, cols] = (dy[rows, cols] * sv * dgelu_u[rows, cols]).astype(BF16)
                dsv = dsv_all[rows, cols]
                dbx = dbx + dsv
                ds0 = jnp.where(lo, dsv, 0).astype(BF16)
                ds1 = jnp.where(lo, 0, dsv).astype(BF16)
                dw0 = dw0 + _dot_nt(ds0, vl0)
                dw1 = dw1 + _dot_nt(ds1, vl1)
                dvln_ref[rows, cols] = _dot_tn(w0, ds0) + _dot_tn(w1, ds1)
            dw_ref[2 * gp] += jnp.where(mask, dw0, 0)
            dw_ref[2 * gp + 1] += jnp.where(mask, dw1, 0)
            dbx_ref[:, cols] += dbx
        dvln = dvln_ref[...]
        dlng_ref[...] += jnp.sum(dvln * vhat, axis=0, keepdims=True)
        dlnb_ref[...] += jnp.sum(dvln, axis=0, keepdims=True)
        dvh = dvln * lng
        dgv = rstd * (dvh - _group_sum64(dvh) * (1.0 / 64) - vhat * (_group_sum64(dvh * vhat) * (1.0 / 64)))
        duv_ref[:, SG_W:2 * SG_W] = (dgv * dgelu_v).astype(BF16)

        @pl.when(i == n_steps - 1)
        def _():
            dbx_ref[...] = _group_sum64(dbx_ref[...])

    return _call(
        body, name="sgu_bwd", grid=(n_steps,),
        out_shape=(jax.ShapeDtypeStruct((S, 2 * SG_W), BF16), jax.ShapeDtypeStruct((8, 128, 128), F32),
                   jax.ShapeDtypeStruct((128, SG_W), F32), jax.ShapeDtypeStruct((1, SG_W), F32),
                   jax.ShapeDtypeStruct((1, SG_W), F32)),
        in_specs=[_rows(tm, 2 * SG_W), _rows(tm, SG_W), _resident((1, SG_W)), _resident((1, SG_W)),
                  _resident((8, 128, 128)), _resident((128, SG_W))],
        out_specs=(_rows(tm, 2 * SG_W), pl.BlockSpec((8, 128, 128), lambda i: (0, 0, 0)),
                   pl.BlockSpec((128, SG_W), lambda i: (0, 0)), pl.BlockSpec((1, SG_W), lambda i: (0, 0)),
                   pl.BlockSpec((1, SG_W), lambda i: (0, 0))),
        scratch_shapes=[pltpu.VMEM((tm, SG_W), F32)],
        operands=(uv, dy_sg, lng, lnb, sg_w, b_exp), plans=plans)


def _attn_bwd(qkv, dy_att, y_att, lse, bias, plans=()):
    S = qkv.shape[0]
    R = ATT_R

    def body(q_ref, kp_ref, kc_ref, vp_ref, vc_ref, b_ref, dy_ref, y_ref, lse_ref, dq_ref, dk_ref, dv_ref, dss_ref,
             s_ref, dp_ref, pb_ref, dsb_ref):
        i = pl.program_id(1)

        @pl.when(i == 0)
        def _():
            dk_ref[...] = jnp.zeros_like(dk_ref)
            dv_ref[...] = jnp.zeros_like(dv_ref)
            dss_ref[...] = jnp.zeros_like(dss_ref)

        lo = lax.broadcasted_iota(jnp.int32, (1, LANES), 1) < 64
        kwin = jnp.concatenate([kp_ref[R - 8 * CHUNK:R, :], kc_ref[...]], axis=0)
        vwin = jnp.concatenate([vp_ref[R - 8 * CHUNK:R, :], vc_ref[...]], axis=0)
        for sub in range(ATT_SUBS):
            rows = slice(128 * sub, 128 * (sub + 1))
            kw = kwin[128 * sub:128 * sub + BAND_KEYS]
            vw = vwin[128 * sub:128 * sub + BAND_KEYS]
            table = jnp.where(i == 0, 1 + sub, 0) if sub < START_SUBS else 0
            qs = _two_heads(q_ref[rows, :], lo)
            dos = _two_heads(dy_ref[rows, :], lo)
            dyy = dy_ref[rows, :].astype(F32) * y_ref[rows, :].astype(F32)
            delta = (jnp.sum(jnp.where(lo, dyy, 0.0), axis=-1, keepdims=True),
                     jnp.sum(jnp.where(lo, 0.0, dyy), axis=-1, keepdims=True))
            lse2 = lse_ref[rows, :]
            s_ref[...] = _dot_nt(qs, kw)
            dp_ref[...] = _dot_nt(dos, vw)
            for hh in range(2):
                lse = lse2[:, 64 * hh:64 * hh + 1]
                for r0 in range(0, 128, ATT_ROWS_BWD):
                    rr = slice(r0, r0 + ATT_ROWS_BWD)
                    both = slice(128 * hh + r0, 128 * hh + r0 + ATT_ROWS_BWD)
                    p = jnp.exp(s_ref[both, :] + b_ref[table, hh, rr, :] - lse[rr])
                    ds = p * (dp_ref[both, :] - delta[hh][rr])
                    dss_ref[hh, rr, :] += ds
                    pb_ref[both, :] = p.astype(BF16)
                    dsb_ref[both, :] = ds.astype(BF16)
            dq = _dot(jnp.concatenate([dsb_ref[0:128, :], dsb_ref[128:256, :]], axis=1), _two_heads(kw, lo))
            dq_ref[rows, :] = dq.astype(BF16)
            start = pl.multiple_of(i * R + 128 * sub, 128)
            dk_ref[pl.ds(start, BAND_KEYS), :] += _dot_tn(dsb_ref[...], qs)
            dv_ref[pl.ds(start, BAND_KEYS), :] += _dot_tn(pb_ref[...], dos)

    acc_spec = pl.BlockSpec((S + 8 * CHUNK, LANES), lambda j, i: (0, j))
    return _call(
        body, name="attn_bwd", grid=(4, S // R),
        out_shape=(jax.ShapeDtypeStruct((S, ATT_W), BF16), jax.ShapeDtypeStruct((S + 8 * CHUNK, ATT_W), F32),
                   jax.ShapeDtypeStruct((S + 8 * CHUNK, ATT_W), F32), jax.ShapeDtypeStruct((8, 128, BAND_KEYS), F32)),
        in_specs=_att_specs() + [pl.BlockSpec((R, LANES), lambda j, i: (i, j))] * 3,
        out_specs=(pl.BlockSpec((R, LANES), lambda j, i: (i, j)), acc_spec, acc_spec,
                   pl.BlockSpec((2, 128, BAND_KEYS), lambda j, i: (j, 0, 0))),
        scratch_shapes=[pltpu.VMEM((256, BAND_KEYS), F32), pltpu.VMEM((256, BAND_KEYS), F32),
                        pltpu.VMEM((256, BAND_KEYS), BF16), pltpu.VMEM((256, BAND_KEYS), BF16)],
        operands=(qkv, qkv, qkv, qkv, qkv, bias, dy_att, y_att, lse), plans=plans)


def _in_bwd(dq, dk, dv, duv, dgates, x, dx1, g1, w_in_t, plans=(), tm=512):
    S = x.shape[0]
    pad_blocks = (8 * CHUNK) // tm

    def body(dq_ref, dk_ref, dv_ref, duv_ref, dg_ref, x_ref, d1_ref, g_ref, wt_ref, dx_ref, dz_ref, dg1_ref):
        i = pl.program_id(0)
        dz_ref[:, 0:ATT_W] = (dq_ref[...].astype(F32) * 0.125).astype(BF16)
        dz_ref[:, ATT_W:2 * ATT_W] = dk_ref[...].astype(BF16)
        dz_ref[:, 2 * ATT_W:3 * ATT_W] = dv_ref[...].astype(BF16)
        dz_ref[:, 3 * ATT_W:3 * ATT_W + 2 * SG_W] = duv_ref[...]
        dz_ref[:, 3 * ATT_W + 2 * SG_W:IN_COLS] = dg_ref[...]
        dh = _dot(dz_ref[...], wt_ref[...])
        xv = x_ref[...]
        dx, dg1 = _rms_bwd(dh, xv, _rstd(xv), g_ref[...])
        dx_ref[...] = d1_ref[...] + dx

        @pl.when(i == 0)
        def _():
            dg1_ref[...] = jnp.zeros_like(dg1_ref)

        dg1_ref[...] += dg1

    shifted = pl.BlockSpec((tm, ATT_W), lambda i: (i + pad_blocks, 0))
    return _call(
        body, name="in_bwd", grid=(S // tm,),
        out_shape=(jax.ShapeDtypeStruct((S, D), F32), jax.ShapeDtypeStruct((S, IN_COLS), BF16),
                   jax.ShapeDtypeStruct((1, D), F32)),
        in_specs=[_rows(tm, ATT_W), shifted, shifted, _rows(tm, 2 * SG_W), _rows(tm, 2 * D), _rows(tm, D),
                  _rows(tm, D), _resident((1, D)), _resident((IN_COLS, D))],
        out_specs=(_rows(tm, D), _rows(tm, IN_COLS), pl.BlockSpec((1, D), lambda i: (0, 0))),
        operands=(dq, dk, dv, duv, dgates, x, dx1, g1, w_in_t), plans=plans)


def _adam_math(w, g, m, v):
    m = ADAM_B1 * m + (1.0 - ADAM_B1) * g
    v = ADAM_B2 * v + (1.0 - ADAM_B2) * (g * g)
    m_hat = m / (1.0 - ADAM_B1 ** ADAM_STEP)
    v_hat = v / (1.0 - ADAM_B2 ** ADAM_STEP)
    delta = -ADAM_LR * (m_hat / (jnp.sqrt(v_hat) + ADAM_EPS) + ADAM_WD * w)
    return delta, m, v


def _adam(parts, w, m, v, tr, name, transposed, after=None):
    P = parts.shape[0]
    R, C = w.shape

    def body(p_ref, w_ref, m_ref, v_ref, *rest):
        g_ref, d_ref, mo_ref, vo_ref = rest[-4:]
        if transposed:
            eye = (lax.broadcasted_iota(jnp.int32, (C, C), 0) == lax.broadcasted_iota(jnp.int32, (C, C), 1)).astype(BF16)
            part = lambda k: _dot_tn(p_ref[k], eye)
        else:
            part = lambda k: p_ref[k].astype(F32)
        g = part(0)
        for k in range(1, P):
            g = g + part(k)
        g_ref[...] = g
        d_ref[...], mo_ref[...], vo_ref[...] = _adam_math(w_ref[...], g, m_ref[...], v_ref[...])

    row = pl.BlockSpec((tr, C), lambda i: (i, 0))
    p_spec = pl.BlockSpec((P, C, tr), lambda i: (0, 0, i)) if transposed else pl.BlockSpec((P, tr, C), lambda i: (0, i, 0))
    extra = [] if after is None else [after]
    return pl.pallas_call(
        body, name=name, grid=(R // tr,),
        out_shape=tuple(jax.ShapeDtypeStruct((R, C), F32) for _ in range(4)),
        in_specs=[p_spec, row, row, row] + [pl.BlockSpec(memory_space=pl.ANY)] * len(extra),
        out_specs=(row, row, row, row),
        compiler_params=_cparams(1),
    )(parts, w, m, v, *extra)


def _my_place():
    return lax.axis_index("x"), lax.axis_index("y"), lax.axis_index("c")


def _other_chips(x, y):
    return [(1 - x, y), (x, 1 - y), (1 - x, 1 - y)]


class _Plan:
    def __init__(self, arrays, out_shapes, sems, start, finish, forward=None, forward_at=0.7):
        self.arrays, self.out_shapes, self.sems = list(arrays), list(out_shapes), list(sems)
        self.start, self.finish, self.forward, self.forward_at = start, finish, forward, forward_at


def _call(body, *, name, grid, in_specs, out_specs, out_shape, operands, scratch_shapes=(), plans=()):
    n_in, n_out, n_scr = len(operands), len(out_shape), len(scratch_shapes)
    p_in = [a for p in plans for a in p.arrays]
    p_out = [s for p in plans for s in p.out_shapes]
    p_sem = [s for p in plans for s in p.sems]
    steps = math.prod(grid)

    def wrapped(*refs):
        ins, refs = refs[:n_in], refs[n_in:]
        pins, refs = refs[:len(p_in)], refs[len(p_in):]
        outs, refs = refs[:n_out], refs[n_out:]
        pouts, refs = refs[:len(p_out)], refs[len(p_out):]
        scr, psems = refs[:n_scr], refs[n_scr:]
        step = 0
        for axis, size in enumerate(grid):
            step = step * size + pl.program_id(axis)
        bound = []
        for p in plans:
            bound.append((p, pins[:len(p.arrays)], pouts[:len(p.out_shapes)], psems[:len(p.sems)]))
            pins, pouts, psems = pins[len(p.arrays):], pouts[len(p.out_shapes):], psems[len(p.sems):]

        @pl.when(step == 0)
        def _():
            for p, a, b, s in bound:
                p.start(a, b, s)

        body(*ins, *outs, *scr)

        for p, a, b, s in bound:
            if p.forward is not None:
                @pl.when(step == min(int(p.forward_at * steps), steps - 1))
                def _(p=p, a=a, b=b, s=s):
                    p.forward(a, b, s)

        @pl.when(step == steps - 1)
        def _():
            for p, a, b, s in bound:
                p.finish(a, b, s)

    hbm = pl.BlockSpec(memory_space=pl.ANY)
    return pl.pallas_call(
        wrapped if plans else body, name=name, grid=grid,
        out_shape=tuple(out_shape) + tuple(p_out),
        in_specs=list(in_specs) + [hbm] * len(p_in),
        out_specs=tuple(out_specs) + tuple([hbm] * len(p_out)),
        scratch_shapes=list(scratch_shapes) + p_sem,
        compiler_params=_cparams(len(grid)),
    )(*operands, *p_in)


def _run_plan(plan, name):
    n_in, n_out = len(plan.arrays), len(plan.out_shapes)

    def body(*refs):
        a, b, s = refs[:n_in], refs[n_in:n_in + n_out], refs[n_in + n_out:]
        plan.start(a, b, s)
        if plan.forward is not None:
            plan.forward(a, b, s)
        plan.finish(a, b, s)

    hbm = pl.BlockSpec(memory_space=pl.ANY)
    return pl.pallas_call(
        body, name=name, out_shape=tuple(plan.out_shapes),
        in_specs=[hbm] * n_in, out_specs=tuple([hbm] * n_out), scratch_shapes=plan.sems,
    )(*plan.arrays)


def _gather_plan(shards, forward_at=0.7):
    n = len(shards)

    def copies(ins, outs, sems):
        send_sems, recv_sems, local_sems = sems
        x, y, c = _my_place()
        me, sibling = (x, y, c), (x, y, 1 - c)
        chips = _other_chips(x, y)

        def idx(p):
            return 4 * p[0] + 2 * p[1] + p[2]

        def copy(a, k, block, to, src=None):
            return pltpu.make_async_remote_copy(
                src_ref=outs[a].at[idx(block)] if src is None else src, dst_ref=outs[a].at[idx(block)],
                send_sem=send_sems.at[a, k], recv_sem=recv_sems.at[a, k], device_id=to, device_id_type=MESH)

        mine = [pltpu.make_async_copy(ins[a], outs[a].at[idx(me)], local_sems.at[a]) for a in range(n)]
        first = []
        for a in range(n):
            first.append(copy(a, 0, me, sibling, src=ins[a]))
            first += [copy(a, 1 + j, me, (*chip, c), src=ins[a]) for j, chip in enumerate(chips)]
        arrived = [copy(a, 1 + j, (*chip, c), me) for j, chip in enumerate(chips) for a in range(n)]
        passed = [copy(a, 4 + j, (*chip, c), sibling) for j, chip in enumerate(chips) for a in range(n)]
        from_sibling = []
        for a in range(n):
            from_sibling.append(copy(a, 0, sibling, me))
            from_sibling += [copy(a, 4 + j, (*chip, 1 - c), me) for j, chip in enumerate(chips)]
        return mine, first, arrived, passed, from_sibling

    def start(ins, outs, sems):
        mine, first, _, _, _ = copies(ins, outs, sems)
        for cp in mine + first:
            cp.start()

    def forward(ins, outs, sems):
        _, _, arrived, passed, _ = copies(ins, outs, sems)
        for landed, onward in zip(arrived, passed):
            landed.wait_recv()
            onward.start()

    def finish(ins, outs, sems):
        mine, first, _, passed, from_sibling = copies(ins, outs, sems)
        for cp in from_sibling:
            cp.wait_recv()
        for cp in first + passed:
            cp.wait_send()
        for cp in mine:
            cp.wait()

    return _Plan(shards, [jax.ShapeDtypeStruct((N_DEV,) + s.shape, s.dtype) for s in shards],
                 [pltpu.SemaphoreType.DMA((n, 7)), pltpu.SemaphoreType.DMA((n, 7)), pltpu.SemaphoreType.DMA((n,))],
                 start, finish, forward, forward_at)


def _sibling_plan(scatter, whole=()):
    ns = len(scatter)
    n = ns + len(whole)

    def copies(ins, outs, sems):
        send_sems, recv_sems = sems
        x, y, c = _my_place()
        out = []
        for a in range(n):
            for k in range(4 if a < ns else 1):
                src = ins[a].at[2 * k + (1 - c)] if a < ns else ins[a]
                dst = outs[a].at[k] if a < ns else outs[a]
                out.append(pltpu.make_async_remote_copy(
                    src_ref=src, dst_ref=dst, send_sem=send_sems.at[a, k], recv_sem=recv_sems.at[a, k],
                    device_id=(x, y, 1 - c), device_id_type=MESH))
        return out

    def start(ins, outs, sems):
        for cp in copies(ins, outs, sems):
            cp.start()

    def finish(ins, outs, sems):
        for cp in copies(ins, outs, sems):
            cp.wait()

    shapes = [jax.ShapeDtypeStruct((4,) + p.shape[1:], p.dtype) for p in scatter]
    shapes += [jax.ShapeDtypeStruct(p.shape, p.dtype) for p in whole]
    return _Plan(list(scatter) + list(whole), shapes,
                 [pltpu.SemaphoreType.DMA((n, 4)), pltpu.SemaphoreType.DMA((n, 4))], start, finish)


def _pair_sums(mine, theirs, c_idx, name):
    n = len(mine)

    def body(c_ref, *refs):
        for a in range(n):
            out = refs[2 * n + a]
            out[...] = (refs[a][...].astype(F32) + refs[n + a][...].astype(F32)).astype(out.dtype)

    def block(arr):
        return (1,) + arr.shape[1:]

    grid_spec = pltpu.PrefetchScalarGridSpec(
        num_scalar_prefetch=1, grid=(4,),
        in_specs=[pl.BlockSpec(block(m), lambda k, c_ref: (2 * k + c_ref[0], 0, 0)) for m in mine]
        + [pl.BlockSpec(block(t), lambda k, c_ref: (k, 0, 0)) for t in theirs],
        out_specs=tuple(pl.BlockSpec(block(t), lambda k, c_ref: (k, 0, 0)) for t in theirs))
    return pl.pallas_call(
        body, name=name, grid_spec=grid_spec,
        out_shape=tuple(jax.ShapeDtypeStruct(t.shape, m.dtype) for m, t in zip(mine, theirs)),
        compiler_params=_cparams(1),
    )(c_idx, *mine, *theirs)


def _peers_plan(arrays):
    n = len(arrays)

    def copies(ins, outs, sems):
        send_sems, recv_sems, local_sems = sems
        x, y, c = _my_place()
        me = 4 * x + 2 * y + c
        out = [pltpu.make_async_copy(ins[a], outs[a].at[me], local_sems.at[a]) for a in range(n)]
        for a in range(n):
            for k in range(N_DEV - 1):
                bits = k + 1
                peer = (x ^ (bits >> 2), y ^ ((bits >> 1) & 1), c ^ (bits & 1))
                out.append(pltpu.make_async_remote_copy(
                    src_ref=ins[a], dst_ref=outs[a].at[me], send_sem=send_sems.at[a, k], recv_sem=recv_sems.at[a, k],
                    device_id=peer, device_id_type=MESH))
        return out

    def start(ins, outs, sems):
        for cp in copies(ins, outs, sems):
            cp.start()

    def finish(ins, outs, sems):
        for cp in copies(ins, outs, sems):
            cp.wait()

    return _Plan(list(arrays), [jax.ShapeDtypeStruct((N_DEV,) + a.shape, a.dtype) for a in arrays],
                 [pltpu.SemaphoreType.DMA((n, N_DEV - 1)), pltpu.SemaphoreType.DMA((n, N_DEV - 1)),
                  pltpu.SemaphoreType.DMA((n,))], start, finish)


def _chips_plan(scatter, whole=()):
    ns = len(scatter)
    n = ns + len(whole)

    def copies(ins, outs, sems):
        send_sems, recv_sems, local_sems = sems
        x, y, c = _my_place()
        my_chip = 2 * x + y

        def src(a, k):
            return ins[a].at[k] if a < ns else ins[a]

        local = [pltpu.make_async_copy(src(a, my_chip), outs[a].at[my_chip], local_sems.at[a]) for a in range(n)]
        remote = []
        for a in range(n):
            for j, (px, py) in enumerate(_other_chips(x, y)):
                remote.append(pltpu.make_async_remote_copy(
                    src_ref=src(a, 2 * px + py), dst_ref=outs[a].at[my_chip],
                    send_sem=send_sems.at[a, j], recv_sem=recv_sems.at[a, j],
                    device_id=(px, py, c), device_id_type=MESH))
        return local + remote

    def start(ins, outs, sems):
        for cp in copies(ins, outs, sems):
            cp.start()

    def finish(ins, outs, sems):
        for cp in copies(ins, outs, sems):
            cp.wait()

    shapes = [jax.ShapeDtypeStruct(s.shape, s.dtype) for s in scatter]
    shapes += [jax.ShapeDtypeStruct((4,) + s.shape, s.dtype) for s in whole]
    return _Plan(list(scatter) + list(whole), shapes,
                 [pltpu.SemaphoreType.DMA((n, 3)), pltpu.SemaphoreType.DMA((n, 3)), pltpu.SemaphoreType.DMA((n,))],
                 start, finish)


def _chip_copies(src_ref, land_ref, send_sems, recv_sems):
    x, y, c = _my_place()
    return [pltpu.make_async_remote_copy(
        src_ref=src_ref.at[2 * px + py], dst_ref=land_ref.at[2 * x + y], send_sem=send_sems.at[j],
        recv_sem=recv_sems.at[j], device_id=(px, py, c), device_id_type=MESH)
        for j, (px, py) in enumerate(_other_chips(x, y))]


def _chips_start(src, name):
    def body(src_ref, land_ref, send_sems, recv_sems, src_thru, land_thru, token):
        for cp in _chip_copies(src_ref, land_ref, send_sems, recv_sems):
            cp.start()
        token[...] = jnp.zeros_like(token)

    hbm = pl.BlockSpec(memory_space=pltpu.HBM)
    sem = pl.BlockSpec(memory_space=pltpu.SEMAPHORE)
    return pl.pallas_call(
        body, name=name,
        out_shape=(pltpu.SemaphoreType.DMA((3,)), pltpu.SemaphoreType.DMA((3,)), pltpu.HBM(src.shape, src.dtype),
                   pltpu.HBM(src.shape, src.dtype), jax.ShapeDtypeStruct((8, LANES), F32)),
        in_specs=(hbm, hbm), out_specs=(sem, sem, hbm, hbm, pl.BlockSpec(memory_space=pltpu.VMEM)),
        input_output_aliases={0: 2, 1: 3},
        compiler_params=pltpu.CompilerParams(has_side_effects=pltpu.SideEffectType.DATAFLOW_SIDE_EFFECTING),
    )(pltpu.with_memory_space_constraint(src, pltpu.HBM),
      pltpu.with_memory_space_constraint(jnp.zeros(src.shape, src.dtype), pltpu.HBM))


def _chips_wait(send_sems, recv_sems, src_thru, land_thru, after, name):
    def body(src_ref, land_ref, send_sems, recv_sems, after_ref, src_dead, got_ref):
        for cp in _chip_copies(src_ref, land_ref, send_sems, recv_sems):
            cp.wait_send()
            cp.wait_recv()

    hbm = pl.BlockSpec(memory_space=pltpu.HBM)
    sem = pl.BlockSpec(memory_space=pltpu.SEMAPHORE)
    return pl.pallas_call(
        body, name=name,
        out_shape=(pltpu.HBM(src_thru.shape, src_thru.dtype), pltpu.HBM(land_thru.shape, land_thru.dtype)),
        in_specs=(hbm, hbm, sem, sem, pl.BlockSpec(memory_space=pl.ANY)), out_specs=(hbm, hbm),
        input_output_aliases={0: 0, 1: 1},
        compiler_params=pltpu.CompilerParams(has_side_effects=pltpu.SideEffectType.DATAFLOW_SIDE_EFFECTING),
    )(src_thru, land_thru, send_sems, recv_sems, after)


def _small_update(parts, w, m, v, loss_parts, name):
    n = len(parts)

    def total(ref):
        acc = ref[0]
        for k in range(1, ref.shape[0]):
            acc = acc + ref[k]
        return acc

    def body(*refs):
        p_refs, w_refs, m_refs, v_refs = (refs[i * n:(i + 1) * n] for i in range(4))
        lp_ref = refs[4 * n]
        outs = refs[4 * n + 1:]
        g_refs, d_refs, mo_refs, vo_refs = (outs[i * n:(i + 1) * n] for i in range(4))
        for a in range(n):
            g = total(p_refs[a])
            g_refs[a][...] = g
            d_refs[a][...], mo_refs[a][...], vo_refs[a][...] = _adam_math(w_refs[a][...], g, m_refs[a][...],
                                                                          v_refs[a][...])
        outs[4 * n][...] = total(lp_ref)

    vm = pl.BlockSpec(memory_space=pltpu.VMEM)
    shapes = [jax.ShapeDtypeStruct(t.shape, F32) for _ in range(4) for t in w]
    shapes.append(jax.ShapeDtypeStruct(loss_parts.shape[1:], F32))
    outs = pl.pallas_call(
        body, name=name, out_shape=tuple(shapes),
        in_specs=[vm] * (4 * n + 1), out_specs=tuple([vm] * (4 * n + 1)),
        compiler_params=pltpu.CompilerParams(vmem_limit_bytes=VMEM_LIMIT),
    )(*parts, *w, *m, *v, loss_parts)
    return outs[0:n], outs[n:2 * n], outs[2 * n:3 * n], outs[3 * n:4 * n], outs[4 * n]


BIG = [("w_in", 1), ("w_branch_att", 1), ("w_branch_sg", 1), ("w_out", 0), ("w_xq", 0), ("w_xkv", 1), ("w_xo", 0),
       ("w_ffn_in", 1), ("w_ffn_out", 0)]
SMALL = [("norm_mix_g", (1, D)), ("rel_bias", (8, NREL)), ("sg_ln_g", (8, 64)), ("sg_ln_b", (8, 64)),
         ("sg_w", (8, 128, 128)), ("sg_b", (8, 128)), ("norm_xattn_g", (1, D)), ("norm_mem_g", (1, D)),
         ("norm_ffn_g", (1, D)), ("norm_final_g", (1, D))]
ADAM_ROWS = {"w_in": 192, "w_branch_att": 512, "w_branch_sg": 512, "w_xkv": 1024, "w_ffn_in": 176}


def _full(gathered):
    return gathered.reshape(N_DEV * gathered.shape[1], gathered.shape[2])


def _blocks(grad):
    return grad.reshape(N_DEV, grad.shape[0] // N_DEV, grad.shape[1])


def kernel(x, mem, norm_mix_g, w_in, rel_bias, sg_ln_g, sg_ln_b, sg_w, sg_b, w_branch_att, w_branch_sg, w_out, norm_xattn_g, norm_mem_g, w_xq, w_xkv, w_xo, norm_ffn_g, w_ffn_in, w_ffn_out, norm_final_g, loss_target, m_norm_mix_g, m_w_in, m_rel_bias, m_sg_ln_g, m_sg_ln_b, m_sg_w, m_sg_b, m_w_branch_att, m_w_branch_sg, m_w_out, m_norm_xattn_g, m_norm_mem_g, m_w_xq, m_w_xkv, m_w_xo, m_norm_ffn_g, m_w_ffn_in, m_w_ffn_out, m_norm_final_g, v_norm_mix_g, v_w_in, v_rel_bias, v_sg_ln_g, v_sg_ln_b, v_sg_w, v_sg_b, v_w_branch_att, v_w_branch_sg, v_w_out, v_norm_xattn_g, v_norm_mem_g, v_w_xq, v_w_xkv, v_w_xo, v_norm_ffn_g, v_w_ffn_in, v_w_ffn_out, v_norm_final_g):
    args = dict(locals())
    big_names = [n for n, _ in BIG]
    small_names = [n for n, _ in SMALL]
    S = x.shape[1]

    x, mem, tgt = x.reshape(S, D), mem.reshape(MEM, D), loss_target.reshape(S, D)
    small = {n: args[n].reshape(shape) for n, shape in SMALL}
    g1, g2, g3 = small["norm_mix_g"], small["norm_xattn_g"], small["norm_ffn_g"]
    g_mem, g4 = small["norm_mem_g"], small["norm_final_g"]
    lng = small["sg_ln_g"].reshape(1, SG_W)
    lnb = small["sg_ln_b"].reshape(1, SG_W)
    b_exp = jnp.broadcast_to(small["sg_b"].T[:, :, None], (128, 8, 64)).reshape(128, SG_W)
    rel_pad = jnp.pad(small["rel_bias"], ((0, 0), (0, 384 - NREL)))
    c_idx = lax.axis_index("c").astype(jnp.int32).reshape(1)

    shard = {n: (args[n][0].T if axis == 1 else args[n][0]).astype(BF16) for n, axis in BIG}
    h, w_in_gathered = _norm_in(x, g1, plans=[_gather_plan([shard["w_in"]])])
    w_in_t = _full(w_in_gathered)
    bias = _bias_table(rel_pad)
    mix_names = ["w_branch_att", "w_branch_sg", "w_out", "w_xq", "w_xkv", "w_xo"]
    qkv, uv, gates, *got = _in_proj(h, w_in_t, plans=[_gather_plan([shard[n] for n in mix_names])])
    wba_t, wbs_t, w_out_f, w_xq_f, w_xkv_t, w_xo_f = (_full(g) for g in got)
    y_att, lse, *got = _attn_fwd(qkv, bias, plans=[_gather_plan([shard["w_ffn_in"], shard["w_ffn_out"]])])
    w_ffn_in_t, w_ffn_out_f = (_full(g) for g in got)
    y_sg = _sgu_fwd(uv, lng, lnb, small["sg_w"], b_exp)
    x1 = _merge_fwd(x, y_att, y_sg, gates, wba_t, wbs_t, w_out_f)
    kv, mn = _mem_kv(mem, g_mem, w_xkv_t)
    x2, hx, qx, o_x = _xattn_fwd(x1, g2, w_xq_f, kv, w_xo_f)
    dx3, gu, hf, act, loss_part, dg4 = _ffn_fwd(x2, tgt, g3, w_ffn_in_t, w_ffn_out_f, g4)


    dx2, dgu, dg3 = _ffn_bwd(dx3, gu, x2, g3, w_ffn_out_f, w_ffn_in_t)
    ffn_names = ["w_ffn_out", "w_ffn_in"]
    ffn_mine = [_blocks(_dw(act, dx3, 1408, 1024, DW_TOKENS, "dw_ffn_out")),
                _blocks(_dw(dgu, hf, 1408, 1024, DW_TOKENS, "dw_ffn_in"))]
    dx1, dq_x, dkv, dg2, *ffn_theirs = _xattn_bwd(dx2, x1, qx, g2, w_xq_f, w_xo_f, kv,
                                                  plans=[_sibling_plan(ffn_mine)])
    ffn_chip = _pair_sums(ffn_mine, ffn_theirs, c_idx, "rs_pair_ffn")
    d_xkv, dg_mem = _mem_kv_bwd(dkv, mem, g_mem, mn, w_xkv_t)
    merged, d_a, d_b, dy_att, dy_sg, dgates = _merge_bwd(dx1, y_att, y_sg, gates, wba_t, wbs_t, w_out_f)
    mid_names = ["w_xo", "w_xq", "w_xkv", "w_out", "w_branch_att", "w_branch_sg"]
    mid_mine = [_blocks(g) for g in (
        _dw(o_x, dx2, 1024, 1024, DW_TOKENS, "dw_xo"), _dw(hx, dq_x, 1024, 1024, DW_TOKENS, "dw_xq"), d_xkv,
        _dw(merged, dx1, 1024, 1024, DW_TOKENS, "dw_out"), _dw(d_a, y_att, 1024, 512, DW_TOKENS, "dw_branch_att"),
        _dw(d_b, y_sg, 1024, 512, DW_TOKENS, "dw_branch_sg"))]
    duv, d_sgw, d_bx, d_lng, d_lnb, *got = _sgu_bwd(uv, dy_sg, lng, lnb, small["sg_w"], b_exp,
                                                    plans=[_chips_plan(ffn_chip), _sibling_plan(mid_mine)])
    ffn_all, mid_theirs = got[:2], got[2:]
    mid_chip = _pair_sums(mid_mine, mid_theirs, c_idx, "rs_pair_mid")
    dq, dk, dv, ds_sum, *mid_all = _attn_bwd(qkv, dy_att, y_att, lse, bias, plans=[_chips_plan(mid_chip)])
    d_rel = _bias_grad(ds_sum)

    grad_x, dz, dg1 = _in_bwd(dq, dk, dv, duv, dgates, x, dx1, g1, w_in_t)

    gs = {"norm_mix_g": dg1, "rel_bias": d_rel[:, :NREL], "sg_ln_g": d_lng.reshape(8, 64),
          "sg_ln_b": d_lnb.reshape(8, 64), "sg_w": d_sgw, "sg_b": d_bx.reshape(128, 8, 64)[:, :, 0].T,
          "norm_xattn_g": dg2, "norm_mem_g": dg_mem, "norm_ffn_g": dg3, "norm_final_g": dg4}
    d_in, *everyone = _dw(dz, h, 1152, 1024, DW_TOKENS, "dw_in",
                          plans=[_peers_plan([gs[n] for n in small_names] + [loss_part])])

    in_mine = [_blocks(d_in)]
    (in_theirs,) = _run_plan(_sibling_plan(in_mine), "rs_sibling")
    (in_chip,) = _pair_sums(in_mine, [in_theirs], c_idx, "rs_pair_w_in")
    send_sems, recv_sems, in_chip_thru, landing, token = _chips_start(in_chip, "rs_chips_start")
    all_parts = dict(zip(ffn_names + mid_names, list(ffn_all) + list(mid_all)))

    def adam(n, axis, parts, after):
        wmv = [args[p + n][0] for p in ("", "m_", "v_")]
        if axis == 1 and wmv[0].shape[1] % LANES != 0:
            outs = _adam(parts, *(t.T for t in wmv), ADAM_ROWS[n], "adam_" + n, transposed=False, after=after)
            return [t.T[None] for t in outs]
        tr = ADAM_ROWS[n] if axis == 1 else wmv[0].shape[0]
        return [t[None] for t in _adam(parts, *wmv, tr, "adam_" + n, transposed=(axis == 1), after=after)]

    res = {n: adam(n, axis, all_parts[n], token) for n, axis in BIG if n != "w_in"}
    in_chip, landing = _chips_wait(send_sems, recv_sems, in_chip_thru, landing, res["w_ffn_in"][1], "rs_chips_wait")
    my_chip = 2 * lax.axis_index("x") + lax.axis_index("y")
    own = lax.dynamic_slice_in_dim(in_chip, my_chip, 1, axis=0)
    res["w_in"] = adam("w_in", 1, lax.dynamic_update_slice_in_dim(landing, own, my_chip, axis=0), None)
    small_res = _small_update(
        everyone[:-1], [small[n] for n in small_names],
        [args["m_" + n].reshape(s) for n, s in SMALL], [args["v_" + n].reshape(s) for n, s in SMALL],
        everyone[-1], "adam_small")
    for i, n in enumerate(small_names):
        res[n] = [small_res[k][i].reshape(args[n].shape) for k in range(4)]
    loss = small_res[4][0, 0]

    order = ["norm_mix_g", "w_in", "rel_bias", "sg_ln_g", "sg_ln_b", "sg_w", "sg_b", "w_branch_att", "w_branch_sg",
             "w_out", "norm_xattn_g", "norm_mem_g", "w_xq", "w_xkv", "w_xo", "norm_ffn_g", "w_ffn_in", "w_ffn_out",
             "norm_final_g"]
    outs = [loss, grad_x.reshape(1, S, D)]
    for k in range(4):
        outs += [res[n][k] for n in order]
    return tuple(outs)
```

```python
import math

import jax
import jax.numpy as jnp
from jax import lax
from jax.experimental import pallas as pl
from jax.experimental.pallas import tpu as pltpu

F32 = jnp.float32
BF16 = jnp.bfloat16

D = 1024
ATT_W = 512
SG_W = 512
IN_COLS = 4608
DFF = 2816
MEM = 256
XH = 4
CHUNK = 64
BAND_KEYS = 640
ATT_R = 4096
ATT_SUBS = ATT_R // 128
START_SUBS = 8 * CHUNK // 128
ATT_ROWS = 32
ATT_ROWS_BWD = 16
DW_TOKENS = 2048
IN_CHUNK = 512
FF_CHUNK = 256
REL_CLIP = 128
NREL = 2 * REL_CLIP + 1
EPS = 1e-6
NEG = -1e30
N_DEV = 8

ADAM_LR = 0.001
ADAM_B1 = 0.9
ADAM_B2 = 0.999
ADAM_EPS = 1e-08
ADAM_WD = 0.01
ADAM_STEP = 10

LANES = 128
VMEM_LIMIT = 56 * 1024 * 1024
MESH = pl.DeviceIdType.MESH


def _cparams(n_axes):
    return pltpu.CompilerParams(dimension_semantics=("arbitrary",) * n_axes, vmem_limit_bytes=VMEM_LIMIT)


def _resident(shape):
    zeros = (0,) * len(shape)
    return pl.BlockSpec(shape, lambda *_: zeros, pipeline_mode=pl.Buffered(1))


def _rows(tm, cols, col_block=0):
    return pl.BlockSpec((tm, cols), lambda i: (i, col_block))


def _sigmoid(x):
    return pl.reciprocal(1.0 + jnp.exp(-x), approx=True)


_GELU_C = math.sqrt(2.0 / math.pi)


def _gelu(x):
    t = jnp.tanh(_GELU_C * (x + 0.044715 * (x * x * x)))
    return x * (0.5 * (1.0 + t))


def _gelu_and_grad(x):
    x2 = x * x
    t = jnp.tanh(_GELU_C * (x + 0.044715 * (x2 * x)))
    cdf = 0.5 * (1.0 + t)
    dcdf = 0.5 * (1.0 - t * t) * (_GELU_C * (1.0 + 3.0 * 0.044715 * x2))
    return x * cdf, cdf + x * dcdf


def _rstd(x):
    return lax.rsqrt(jnp.mean(x * x, axis=-1, keepdims=True) + EPS)


def _rms_bwd(dh, x, r, g):
    xh = x * r
    dxh = dh * g
    dx = r * (dxh - xh * jnp.mean(dxh * xh, axis=-1, keepdims=True))
    dg = jnp.sum(dh * xh, axis=0, keepdims=True)
    return dx, dg


def _group_sum64(x):
    r = lax.broadcasted_iota(jnp.int32, (LANES, LANES), 0) // 64
    c = lax.broadcasted_iota(jnp.int32, (LANES, LANES), 1) // 64
    same_group = (r == c).astype(BF16)

    def one(v):
        hi = v.astype(BF16)
        rest = v - hi.astype(F32)
        mid = rest.astype(BF16)
        lo = (rest - mid.astype(F32)).astype(BF16)
        return _dot(hi, same_group) + _dot(mid, same_group) + _dot(lo, same_group)

    pieces = [one(x[:, LANES * j:LANES * (j + 1)]) for j in range(x.shape[1] // LANES)]
    return pieces[0] if len(pieces) == 1 else jnp.concatenate(pieces, axis=1)


def _dot(a, b):
    return jnp.dot(a, b, preferred_element_type=F32)


def _dot_nt(a, b):
    return lax.dot_general(a, b, (((1,), (1,)), ((), ())), preferred_element_type=F32)


def _dot_tn(a, b):
    return lax.dot_general(a, b, (((0,), (0,)), ((), ())), preferred_element_type=F32)


DIAGS = 768


def _diag_onehot():
    r_idx = lax.broadcasted_iota(jnp.int32, (384, DIAGS), 0)
    t_idx = lax.broadcasted_iota(jnp.int32, (384, DIAGS), 1)
    dist = (8 * CHUNK + 127) - t_idx
    return (jnp.clip(dist, -REL_CLIP, REL_CLIP) + REL_CLIP == r_idx).astype(F32)


def _shift_rows(x, reverse):
    row = lax.broadcasted_iota(jnp.int32, x.shape, 0)
    for k in range(7):
        amt = (DIAGS - (1 << k)) if reverse else (1 << k)
        x = jnp.where(((row >> k) & 1) == 1, pltpu.roll(x, amt, 1), x)
    return x


N_TABLES = 1 + START_SUBS


def _bias_table(rel_bias_pad):
    def body(rb_ref, out_ref):
        per_diag = jnp.dot(rb_ref[...], _diag_onehot(), preferred_element_type=F32,
                           precision=lax.Precision.HIGHEST)
        a = lax.broadcasted_iota(jnp.int32, (128, BAND_KEYS), 0)
        b = lax.broadcasted_iota(jnp.int32, (128, BAND_KEYS), 1)
        band = (b // CHUNK >= a // CHUNK) & (b // CHUNK <= a // CHUNK + 8)
        for h in range(8):
            rows = jnp.broadcast_to(per_diag[h:h + 1, :], (128, DIAGS))
            table = _shift_rows(pltpu.roll(rows, DIAGS - 127, 1), reverse=False)[:, :BAND_KEYS]
            out_ref[0, h] = jnp.where(band, table, NEG)
            for s in range(START_SUBS):
                out_ref[1 + s, h] = jnp.where(band & (b >= 8 * CHUNK - 128 * s), table, NEG)

    return pl.pallas_call(
        body, name="bias_table",
        out_shape=jax.ShapeDtypeStruct((N_TABLES, 8, 128, BAND_KEYS), F32),
        in_specs=[pl.BlockSpec(memory_space=pltpu.VMEM)],
        out_specs=pl.BlockSpec(memory_space=pltpu.VMEM),
        compiler_params=pltpu.CompilerParams(vmem_limit_bytes=VMEM_LIMIT),
    )(rel_bias_pad)


def _bias_grad(ds_sum):
    def body(ds_ref, out_ref):
        sums = []
        for h in range(8):
            padded = jnp.concatenate([ds_ref[h], jnp.zeros((128, DIAGS - BAND_KEYS), F32)], axis=1)
            skewed = pltpu.roll(_shift_rows(padded, reverse=True), 127, 1)
            sums.append(jnp.sum(skewed, axis=0, keepdims=True))
        per_diag = jnp.concatenate(sums, axis=0)
        out_ref[...] = lax.dot_general(per_diag, _diag_onehot(), (((1,), (1,)), ((), ())),
                                       preferred_element_type=F32, precision=lax.Precision.HIGHEST)

    return pl.pallas_call(
        body, name="bias_grad",
        out_shape=jax.ShapeDtypeStruct((8, 384), F32),
        in_specs=[pl.BlockSpec(memory_space=pltpu.VMEM)],
        out_specs=pl.BlockSpec(memory_space=pltpu.VMEM),
    )(ds_sum)


def _norm_in(x, g1, plans=(), tm=512):
    S = x.shape[0]

    def body(x_ref, g_ref, h_ref):
        xv = x_ref[...]
        h_ref[...] = (xv * _rstd(xv) * g_ref[...]).astype(BF16)

    return _call(
        body, name="norm_in", grid=(S // tm,),
        out_shape=(jax.ShapeDtypeStruct((S, D), BF16),),
        in_specs=[_rows(tm, D), _resident((1, D))], out_specs=(_rows(tm, D),),
        operands=(x, g1), plans=plans)


def _in_proj(h, w_in_t, plans=(), tm=1024):
    S = h.shape[0]

    def body(h_ref, w_ref, qkv_ref, uv_ref, gate_ref):
        h = h_ref[...]
        w = IN_CHUNK
        for c in range(IN_COLS // w):
            zc = _dot_nt(h, w_ref[w * c:w * (c + 1), :])
            start = w * c
            if start < ATT_W:
                qkv_ref[:, start:start + w] = (zc * 0.125).astype(BF16)
            elif start < 3 * ATT_W:
                qkv_ref[:, start:start + w] = zc.astype(BF16)
            elif start < 3 * ATT_W + 2 * SG_W:
                uv_ref[:, start - 3 * ATT_W:start - 3 * ATT_W + w] = zc.astype(BF16)
            else:
                gate_ref[:, start - 3 * ATT_W - 2 * SG_W:start - 3 * ATT_W - 2 * SG_W + w] = zc.astype(BF16)

    return _call(
        body, name="in_proj", grid=(S // tm,),
        out_shape=(jax.ShapeDtypeStruct((S, 3 * ATT_W), BF16), jax.ShapeDtypeStruct((S, 2 * SG_W), BF16),
                   jax.ShapeDtypeStruct((S, 2 * D), BF16)),
        in_specs=[_rows(tm, D), _resident((IN_COLS, D))],
        out_specs=(_rows(tm, 3 * ATT_W), _rows(tm, 2 * SG_W), _rows(tm, 2 * D)),
        operands=(h, w_in_t), plans=plans)


def _two_heads(a, lo):
    return jnp.concatenate([jnp.where(lo, a, 0), jnp.where(lo, 0, a)], axis=0)


def _att_specs():
    R = ATT_R
    q = pl.BlockSpec((R, LANES), lambda j, i: (i, j))
    kp = pl.BlockSpec((R, LANES), lambda j, i: (jnp.maximum(i - 1, 0), 4 + j))
    kc = pl.BlockSpec((R, LANES), lambda j, i: (i, 4 + j))
    vp = pl.BlockSpec((R, LANES), lambda j, i: (jnp.maximum(i - 1, 0), 8 + j))
    vc = pl.BlockSpec((R, LANES), lambda j, i: (i, 8 + j))
    bias = pl.BlockSpec((N_TABLES, 2, 128, BAND_KEYS), lambda j, i: (0, j, 0, 0))
    return [q, kp, kc, vp, vc, bias]


def _attn_fwd(qkv, bias, plans=()):
    S = qkv.shape[0]
    R = ATT_R

    def body(q_ref, kp_ref, kc_ref, vp_ref, vc_ref, b_ref, o_ref, lse_ref, s_ref, p_ref, l_ref):
        i = pl.program_id(1)
        lo = lax.broadcasted_iota(jnp.int32, (1, LANES), 1) < 64
        kwin = jnp.concatenate([kp_ref[R - 8 * CHUNK:R, :], kc_ref[...]], axis=0)
        vwin = jnp.concatenate([vp_ref[R - 8 * CHUNK:R, :], vc_ref[...]], axis=0)
        for sub in range(ATT_SUBS):
            q2 = q_ref[128 * sub:128 * (sub + 1), :]
            kw = kwin[128 * sub:128 * sub + BAND_KEYS]
            vw = vwin[128 * sub:128 * sub + BAND_KEYS]
            table = jnp.where(i == 0, 1 + sub, 0) if sub < START_SUBS else 0
            s_ref[...] = _dot_nt(_two_heads(q2, lo), kw)
            for hh in range(2):
                for r0 in range(0, 128, ATT_ROWS):
                    rows = slice(128 * hh + r0, 128 * hh + r0 + ATT_ROWS)
                    s = s_ref[rows, :] + b_ref[table, hh, r0:r0 + ATT_ROWS, :]
                    top = jnp.max(s, axis=-1, keepdims=True)
                    p = jnp.exp(s - top)
                    total = jnp.sum(p, axis=-1, keepdims=True)
                    p_ref[rows, :] = (p / total).astype(BF16)
                    l_ref[rows, :] = jnp.broadcast_to(top + jnp.log(total), (ATT_ROWS, LANES))
            o = _dot(jnp.concatenate([p_ref[0:128, :], p_ref[128:256, :]], axis=1), _two_heads(vw, lo))
            o_ref[128 * sub:128 * (sub + 1), :] = o.astype(BF16)
            lse_ref[128 * sub:128 * (sub + 1), :] = jnp.where(lo, l_ref[0:128, :], l_ref[128:256, :])

    blk = pl.BlockSpec((R, LANES), lambda j, i: (i, j))
    return _call(
        body, name="attn_fwd", grid=(4, S // R),
        out_shape=(jax.ShapeDtypeStruct((S, ATT_W), BF16), jax.ShapeDtypeStruct((S, ATT_W), F32)),
        in_specs=_att_specs(), out_specs=(blk, blk),
        scratch_shapes=[pltpu.VMEM((256, BAND_KEYS), F32), pltpu.VMEM((256, BAND_KEYS), BF16),
                        pltpu.VMEM((256, LANES), F32)],
        operands=(qkv, qkv, qkv, qkv, qkv, bias), plans=plans)


def _sg_mask():
    t = lax.broadcasted_iota(jnp.int32, (128, 128), 0)
    s = lax.broadcasted_iota(jnp.int32, (128, 128), 1)
    return (s // CHUNK) <= (t // CHUNK)


def _sg_layernorm(gv, lng, lnb):
    mu = _group_sum64(gv) * (1.0 / 64)
    xc = gv - mu
    var = _group_sum64(xc * xc) * (1.0 / 64)
    rstd = lax.rsqrt(var + EPS)
    vhat = xc * rstd
    return vhat * lng + lnb, vhat, rstd


def _sgu_fwd(uv, lng, lnb, sg_w, b_exp, tm=1024):
    S = uv.shape[0]

    def body(uv_ref, lng_ref, lnb_ref, w_ref, b_ref, y_ref):
        lane = lax.broadcasted_iota(jnp.int32, (1, LANES), 1)
        lo = lane < 64
        mask = _sg_mask()
        gu = _gelu(uv_ref[:, 0:SG_W].astype(F32))
        vln, _, _ = _sg_layernorm(_gelu(uv_ref[:, SG_W:2 * SG_W].astype(F32)), lng_ref[...], lnb_ref[...])
        for gp in range(4):
            w0 = jnp.where(mask, w_ref[2 * gp], 0).astype(BF16)
            w1 = jnp.where(mask, w_ref[2 * gp + 1], 0).astype(BF16)
            cols = slice(128 * gp, 128 * (gp + 1))
            for n in range(tm // 128):
                rows = slice(128 * n, 128 * (n + 1))
                vl = vln[rows, cols]
                sv = (_dot(w0, jnp.where(lo, vl, 0).astype(BF16)) + _dot(w1, jnp.where(lo, 0, vl).astype(BF16))
                      + b_ref[:, cols])
                y_ref[rows, cols] = (gu[rows, cols] * sv).astype(BF16)

    return pl.pallas_call(
        body, name="sgu_fwd", grid=(S // tm,),
        out_shape=jax.ShapeDtypeStruct((S, SG_W), BF16),
        in_specs=[_rows(tm, 2 * SG_W), _resident((1, SG_W)), _resident((1, SG_W)),
                  _resident((8, 128, 128)), _resident((128, SG_W))],
        out_specs=_rows(tm, SG_W),
        compiler_params=_cparams(1),
    )(uv, lng, lnb, sg_w, b_exp)


def _merge_fwd(x, y_att, y_sg, gates, wba_t, wbs_t, w_out, tm=512):
    S = x.shape[0]

    def body(x_ref, ya_ref, ys_ref, g_ref, wba_ref, wbs_ref, wo_ref, x1_ref):
        a = _dot_nt(ya_ref[...], wba_ref[...])
        b = _dot_nt(ys_ref[...], wbs_ref[...])
        merged = _sigmoid(g_ref[:, 0:D].astype(F32)) * a + _sigmoid(g_ref[:, D:2 * D].astype(F32)) * b
        x1_ref[...] = x_ref[...] + _dot(merged.astype(BF16), wo_ref[...])

    return pl.pallas_call(
        body, name="merge_fwd", grid=(S // tm,),
        out_shape=jax.ShapeDtypeStruct((S, D), F32),
        in_specs=[_rows(tm, D), _rows(tm, ATT_W), _rows(tm, SG_W), _rows(tm, 2 * D),
                  _resident((D, ATT_W)), _resident((D, SG_W)), _resident((D, D))],
        out_specs=_rows(tm, D),
        compiler_params=_cparams(1),
    )(x, y_att, y_sg, gates, wba_t, wbs_t, w_out)


def _mem_kv(mem, g_mem, w_xkv_t):
    def body(m_ref, g_ref, w_ref, kv_ref, mn_ref):
        mv = m_ref[...]
        mn = (mv * _rstd(mv) * g_ref[...]).astype(BF16)
        mn_ref[...] = mn
        kv_ref[...] = _dot_nt(mn, w_ref[...]).astype(BF16)

    vm = pl.BlockSpec(memory_space=pltpu.VMEM)
    return pl.pallas_call(
        body, name="mem_kv",
        out_shape=(jax.ShapeDtypeStruct((MEM, 2 * D), BF16), jax.ShapeDtypeStruct((MEM, D), BF16)),
        in_specs=[vm, vm, vm], out_specs=(vm, vm),
        compiler_params=pltpu.CompilerParams(vmem_limit_bytes=VMEM_LIMIT),
    )(mem, g_mem, w_xkv_t)


def _xatt_head(qx, kv_ref, h):
    hs = slice(256 * h, 256 * (h + 1))
    s = _dot_nt(qx[:, hs], kv_ref[:, hs])
    p = jnp.exp(s - jnp.max(s, axis=-1, keepdims=True))
    return p / jnp.sum(p, axis=-1, keepdims=True)


def _xattn_fwd(x1, g2, w_xq, kv, w_xo, tm=1024):
    S = x1.shape[0]

    def body(x_ref, g_ref, wq_ref, kv_ref, wo_ref, x2_ref, hx_ref, qx_ref, o_ref):
        xv = x_ref[...]
        hx = (xv * _rstd(xv) * g_ref[...]).astype(BF16)
        hx_ref[...] = hx
        qx = (_dot(hx, wq_ref[...]) * (1.0 / 16)).astype(BF16)
        qx_ref[...] = qx
        for h in range(XH):
            p = _xatt_head(qx, kv_ref, h)
            o_ref[:, 256 * h:256 * (h + 1)] = _dot(p.astype(BF16), kv_ref[:, D + 256 * h:D + 256 * (h + 1)]).astype(BF16)
        x2_ref[...] = xv + _dot(o_ref[...], wo_ref[...])

    return pl.pallas_call(
        body, name="xattn_fwd", grid=(S // tm,),
        out_shape=(jax.ShapeDtypeStruct((S, D), F32),) + (jax.ShapeDtypeStruct((S, D), BF16),) * 3,
        in_specs=[_rows(tm, D), _resident((1, D)), _resident((D, D)), _resident((MEM, 2 * D)), _resident((D, D))],
        out_specs=(_rows(tm, D),) * 4,
        compiler_params=_cparams(1),
    )(x1, g2, w_xq, kv, w_xo)


def _ffn_fwd(x2, tgt, g3, w_ffn_in_t, w_ffn_out, g4, tm=256):
    S = x2.shape[0]

    def body(x_ref, t_ref, g3_ref, wi_ref, wo_ref, g4_ref, dx3_ref, gu_ref, hf_ref, act_ref, loss_ref, dg4_ref):
        i = pl.program_id(0)
        xv = x_ref[...]
        hf = (xv * _rstd(xv) * g3_ref[...]).astype(BF16)
        hf_ref[...] = hf
        for c in range(DFF // FF_CHUNK):
            cs = slice(FF_CHUNK * c, FF_CHUNK * (c + 1))
            us = slice(DFF + FF_CHUNK * c, DFF + FF_CHUNK * (c + 1))
            gate = _dot_nt(hf, wi_ref[cs, :])
            up = _dot_nt(hf, wi_ref[us, :])
            gu_ref[:, cs] = gate.astype(BF16)
            gu_ref[:, us] = up.astype(BF16)
            act_ref[:, cs] = ((gate * _sigmoid(gate)) * up).astype(BF16)
        acc = xv + _dot(act_ref[...], wo_ref[...])
        r4 = _rstd(acc)
        g4 = g4_ref[...]
        diff = acc * r4 * g4 - t_ref[...]
        dy = diff * (1.0 / D)
        dx3, dg4 = _rms_bwd(dy, acc, r4, g4)
        dx3_ref[...] = dx3
        part = 0.5 * jnp.sum(jnp.mean(diff * diff, axis=-1, keepdims=True))

        @pl.when(i == 0)
        def _():
            loss_ref[...] = jnp.zeros_like(loss_ref)
            dg4_ref[...] = jnp.zeros_like(dg4_ref)

        loss_ref[...] += jnp.full(loss_ref.shape, part, F32)
        dg4_ref[...] += dg4

    return pl.pallas_call(
        body, name="ffn_fwd", grid=(S // tm,),
        out_shape=(jax.ShapeDtypeStruct((S, D), F32), jax.ShapeDtypeStruct((S, 2 * DFF), BF16),
                   jax.ShapeDtypeStruct((S, D), BF16), jax.ShapeDtypeStruct((S, DFF), BF16),
                   jax.ShapeDtypeStruct((8, LANES), F32), jax.ShapeDtypeStruct((1, D), F32)),
        in_specs=[_rows(tm, D), _rows(tm, D), _resident((1, D)), _resident((2 * DFF, D)), _resident((DFF, D)),
                  _resident((1, D))],
        out_specs=(_rows(tm, D), _rows(tm, 2 * DFF), _rows(tm, D), _rows(tm, DFF),
                   pl.BlockSpec((8, LANES), lambda i: (0, 0)), pl.BlockSpec((1, D), lambda i: (0, 0))),
        compiler_params=_cparams(1),
    )(x2, tgt, g3, w_ffn_in_t, w_ffn_out, g4)


def _ffn_bwd(dx3, gu, x2, g3, w_ffn_out, w_ffn_in_t, tm=256):
    S = x2.shape[0]

    def body(d_ref, gu_ref, x_ref, g3_ref, wo_ref, wit_ref, dx2_ref, dgu_ref, dg3_ref):
        i = pl.program_id(0)
        d3 = d_ref[...]
        d3b = d3.astype(BF16)
        for c in range(DFF // FF_CHUNK):
            cs = slice(FF_CHUNK * c, FF_CHUNK * (c + 1))
            us = slice(DFF + FF_CHUNK * c, DFF + FF_CHUNK * (c + 1))
            da = _dot_nt(d3b, wo_ref[cs, :])
            gate = gu_ref[:, cs].astype(F32)
            up = gu_ref[:, us].astype(F32)
            sg = _sigmoid(gate)
            dgate = (da * up * (sg * (1.0 + gate * (1.0 - sg)))).astype(BF16)
            dup = (da * (gate * sg)).astype(BF16)
            dgu_ref[:, cs] = dgate
            dgu_ref[:, us] = dup
        dhf = _dot(dgu_ref[...], wit_ref[...])
        xv = x_ref[...]
        dx, dg3 = _rms_bwd(dhf, xv, _rstd(xv), g3_ref[...])
        dx2_ref[...] = d3 + dx

        @pl.when(i == 0)
        def _():
            dg3_ref[...] = jnp.zeros_like(dg3_ref)

        dg3_ref[...] += dg3

    return pl.pallas_call(
        body, name="ffn_bwd", grid=(S // tm,),
        out_shape=(jax.ShapeDtypeStruct((S, D), F32), jax.ShapeDtypeStruct((S, 2 * DFF), BF16),
                   jax.ShapeDtypeStruct((1, D), F32)),
        in_specs=[_rows(tm, D), _rows(tm, 2 * DFF), _rows(tm, D), _resident((1, D)),
                  _resident((DFF, D)), _resident((2 * DFF, D))],
        out_specs=(_rows(tm, D), _rows(tm, 2 * DFF), pl.BlockSpec((1, D), lambda i: (0, 0))),
        compiler_params=_cparams(1),
    )(dx3, gu, x2, g3, w_ffn_out, w_ffn_in_t)


def _dw(a, b, tmm, tn, ts, name, out_dtype=BF16, plans=()):
    S, M = a.shape
    N = b.shape[1]
    ts = min(ts, S)
    nk = S // ts

    def body(a_ref, b_ref, o_ref, acc_ref):
        k = pl.program_id(2)

        @pl.when(k == 0)
        def _():
            acc_ref[...] = jnp.zeros_like(acc_ref)

        acc_ref[...] += _dot_tn(a_ref[...].astype(BF16), b_ref[...].astype(BF16))

        @pl.when(k == nk - 1)
        def _():
            o_ref[...] = acc_ref[...].astype(out_dtype)

    out = _call(
        body, name=name, grid=(M // tmm, N // tn, nk),
        out_shape=(jax.ShapeDtypeStruct((M, N), out_dtype),),
        in_specs=[pl.BlockSpec((ts, tmm), lambda m, n, k: (k, m)), pl.BlockSpec((ts, tn), lambda m, n, k: (k, n))],
        out_specs=(pl.BlockSpec((tmm, tn), lambda m, n, k: (m, n)),),
        scratch_shapes=[pltpu.VMEM((tmm, tn), F32)],
        operands=(a, b), plans=plans)
    return out if plans else out[0]


def _xattn_bwd(dx2, x1, qx, g2, w_xq, w_xo, kv, plans=(), tm=512):
    S = x1.shape[0]

    def body(d_ref, x_ref, qx_ref, g_ref, wq_ref, wo_ref, kv_ref, dx1_ref, dq_ref, dkv_ref, dg2_ref):
        i = pl.program_id(0)

        @pl.when(i == 0)
        def _():
            dkv_ref[...] = jnp.zeros_like(dkv_ref)
            dg2_ref[...] = jnp.zeros_like(dg2_ref)

        d2 = d_ref[...]
        qx = qx_ref[...]
        do = _dot_nt(d2.astype(BF16), wo_ref[...]).astype(BF16)
        for h in range(XH):
            hs = slice(256 * h, 256 * (h + 1))
            vs = slice(D + 256 * h, D + 256 * (h + 1))
            p = _xatt_head(qx, kv_ref, h)
            dp = _dot_nt(do[:, hs], kv_ref[:, vs])
            ds = (p * (dp - jnp.sum(dp * p, axis=-1, keepdims=True))).astype(BF16)
            dq_ref[:, hs] = (_dot(ds, kv_ref[:, hs]) * (1.0 / 16)).astype(BF16)
            dkv_ref[:, hs] += _dot_tn(ds, qx[:, hs])
            dkv_ref[:, vs] += _dot_tn(p.astype(BF16), do[:, hs])
        dhx = _dot_nt(dq_ref[...], wq_ref[...])
        xv = x_ref[...]
        dx, dg2 = _rms_bwd(dhx, xv, _rstd(xv), g_ref[...])
        dx1_ref[...] = d2 + dx
        dg2_ref[...] += dg2

    return _call(
        body, name="xattn_bwd", grid=(S // tm,),
        out_shape=(jax.ShapeDtypeStruct((S, D), F32), jax.ShapeDtypeStruct((S, D), BF16),
                   jax.ShapeDtypeStruct((MEM, 2 * D), F32), jax.ShapeDtypeStruct((1, D), F32)),
        in_specs=[_rows(tm, D), _rows(tm, D), _rows(tm, D), _resident((1, D)), _resident((D, D)), _resident((D, D)),
                  _resident((MEM, 2 * D))],
        out_specs=(_rows(tm, D), _rows(tm, D),
                   pl.BlockSpec((MEM, 2 * D), lambda i: (0, 0)), pl.BlockSpec((1, D), lambda i: (0, 0))),
        operands=(dx2, x1, qx, g2, w_xq, w_xo, kv), plans=plans)


def _mem_kv_bwd(dkv, mem, g_mem, mn, w_xkv_t):
    def body(dkv_ref, m_ref, g_ref, mn_ref, wt_ref, dw_ref, dg_ref):
        dkvb = dkv_ref[...].astype(BF16)
        dw_ref[...] = _dot_tn(dkvb, mn_ref[...]).astype(BF16)
        dmn = _dot(dkvb, wt_ref[...])
        mv = m_ref[...]
        dg_ref[...] = jnp.sum(dmn * (mv * _rstd(mv)), axis=0, keepdims=True)

    vm = pl.BlockSpec(memory_space=pltpu.VMEM)
    return pl.pallas_call(
        body, name="mem_kv_bwd",
        out_shape=(jax.ShapeDtypeStruct((2 * D, D), BF16), jax.ShapeDtypeStruct((1, D), F32)),
        in_specs=[vm] * 5, out_specs=(vm, vm),
        compiler_params=pltpu.CompilerParams(vmem_limit_bytes=VMEM_LIMIT),
    )(dkv, mem, g_mem, mn, w_xkv_t)


def _merge_bwd(dx1, y_att, y_sg, gates, wba_t, wbs_t, w_out, tm=512):
    S = dx1.shape[0]

    def body(d_ref, ya_ref, ys_ref, g_ref, wbat_ref, wbst_ref, wo_ref,
             mg_ref, da_ref, db_ref, dya_ref, dys_ref, dg_ref):
        dm = _dot_nt(d_ref[...].astype(BF16), wo_ref[...])
        a = _dot_nt(ya_ref[...], wbat_ref[...])
        b = _dot_nt(ys_ref[...], wbst_ref[...])
        sa = _sigmoid(g_ref[:, 0:D].astype(F32))
        sb = _sigmoid(g_ref[:, D:2 * D].astype(F32))
        mg_ref[...] = (sa * a + sb * b).astype(BF16)
        da = (dm * sa).astype(BF16)
        db = (dm * sb).astype(BF16)
        da_ref[...] = da
        db_ref[...] = db
        dg_ref[:, 0:D] = (dm * a * sa * (1.0 - sa)).astype(BF16)
        dg_ref[:, D:2 * D] = (dm * b * sb * (1.0 - sb)).astype(BF16)
        dya_ref[...] = _dot(da, wbat_ref[...]).astype(BF16)
        dys_ref[...] = _dot(db, wbst_ref[...]).astype(BF16)

    return pl.pallas_call(
        body, name="merge_bwd", grid=(S // tm,),
        out_shape=(jax.ShapeDtypeStruct((S, D), BF16), jax.ShapeDtypeStruct((S, D), BF16),
                   jax.ShapeDtypeStruct((S, D), BF16), jax.ShapeDtypeStruct((S, ATT_W), BF16),
                   jax.ShapeDtypeStruct((S, SG_W), BF16), jax.ShapeDtypeStruct((S, 2 * D), BF16)),
        in_specs=[_rows(tm, D), _rows(tm, ATT_W), _rows(tm, SG_W), _rows(tm, 2 * D),
                  _resident((D, ATT_W)), _resident((D, SG_W)), _resident((D, D))],
        out_specs=(_rows(tm, D), _rows(tm, D), _rows(tm, D), _rows(tm, ATT_W), _rows(tm, SG_W), _rows(tm, 2 * D)),
        compiler_params=_cparams(1),
    )(dx1, y_att, y_sg, gates, wba_t, wbs_t, w_out)


def _sgu_bwd(uv, dy_sg, lng, lnb, sg_w, b_exp, plans=(), tm=1024):
    S = uv.shape[0]
    n_steps = S // tm

    def body(uv_ref, dy_ref, lng_ref, lnb_ref, w_ref, b_ref, duv_ref, dw_ref, dbx_ref, dlng_ref, dlnb_ref, dvln_ref):
        i = pl.program_id(0)

        @pl.when(i == 0)
        def _():
            dw_ref[...] = jnp.zeros_like(dw_ref)
            dbx_ref[...] = jnp.zeros_like(dbx_ref)
            dlng_ref[...] = jnp.zeros_like(dlng_ref)
            dlnb_ref[...] = jnp.zeros_like(dlnb_ref)

        lane = lax.broadcasted_iota(jnp.int32, (1, LANES), 1)
        lo = lane < 64
        mask = _sg_mask()
        lng = lng_ref[...]
        gu, dgelu_u = _gelu_and_grad(uv_ref[:, 0:SG_W].astype(F32))
        gv, dgelu_v = _gelu_and_grad(uv_ref[:, SG_W:2 * SG_W].astype(F32))
        vln, vhat, rstd = _sg_layernorm(gv, lng, lnb_ref[...])
        dy = dy_ref[...].astype(F32)
        dsv_all = dy * gu
        for gp in range(4):
            wf0 = jnp.where(mask, w_ref[2 * gp], 0)
            wf1 = jnp.where(mask, w_ref[2 * gp + 1], 0)
            w0 = wf0.astype(BF16)
            w1 = wf1.astype(BF16)
            cols = slice(128 * gp, 128 * (gp + 1))
            dw0 = jnp.zeros((128, 128), F32)
            dw1 = jnp.zeros((128, 128), F32)
            dbx = jnp.zeros((128, LANES), F32)
            for n in range(tm // 128):
                rows = slice(128 * n, 128 * (n + 1))
                vl = vln[rows, cols]
                vl0 = jnp.where(lo, vl, 0).astype(BF16)
                vl1 = jnp.where(lo, 0, vl).astype(BF16)
                sv = _dot(w0, vl0) + _dot(w1, vl1) + b_ref[:, cols]
                duv_ref[rows, cols] = (dy[rows, cols] * sv * dgelu_u[rows, cols]).astype(BF16)
                dsv = dsv_all[rows, cols]
                dbx = dbx + dsv
                ds0 = jnp.where(lo, dsv, 0).astype(BF16)
                ds1 = jnp.where(lo, 0, dsv).astype(BF16)
                dw0 = dw0 + _dot_nt(ds0, vl0)
                dw1 = dw1 + _dot_nt(ds1, vl1)
                dvln_ref[rows, cols] = _dot_tn(w0, ds0) + _dot_tn(w1, ds1)
            dw_ref[2 * gp] += jnp.where(mask, dw0, 0)
            dw_ref[2 * gp + 1] += jnp.where(mask, dw1, 0)
            dbx_ref[:, cols] += dbx
        dvln = dvln_ref[...]
        dlng_ref[...] += jnp.sum(dvln * vhat, axis=0, keepdims=True)
        dlnb_ref[...] += jnp.sum(dvln, axis=0, keepdims=True)
        dvh = dvln * lng
        dgv = rstd * (dvh - _group_sum64(dvh) * (1.0 / 64) - vhat * (_group_sum64(dvh * vhat) * (1.0 / 64)))
        duv_ref[:, SG_W:2 * SG_W] = (dgv * dgelu_v).astype(BF16)

        @pl.when(i == n_steps - 1)
        def _():
            dbx_ref[...] = _group_sum64(dbx_ref[...])

    return _call(
        body, name="sgu_bwd", grid=(n_steps,),
        out_shape=(jax.ShapeDtypeStruct((S, 2 * SG_W), BF16), jax.ShapeDtypeStruct((8, 128, 128), F32),
                   jax.ShapeDtypeStruct((128, SG_W), F32), jax.ShapeDtypeStruct((1, SG_W), F32),
                   jax.ShapeDtypeStruct((1, SG_W), F32)),
        in_specs=[_rows(tm, 2 * SG_W), _rows(tm, SG_W), _resident((1, SG_W)), _resident((1, SG_W)),
                  _resident((8, 128, 128)), _resident((128, SG_W))],
        out_specs=(_rows(tm, 2 * SG_W), pl.BlockSpec((8, 128, 128), lambda i: (0, 0, 0)),
                   pl.BlockSpec((128, SG_W), lambda i: (0, 0)), pl.BlockSpec((1, SG_W), lambda i: (0, 0)),
                   pl.BlockSpec((1, SG_W), lambda i: (0, 0))),
        scratch_shapes=[pltpu.VMEM((tm, SG_W), F32)],
        operands=(uv, dy_sg, lng, lnb, sg_w, b_exp), plans=plans)


def _attn_bwd(qkv, dy_att, y_att, lse, bias, plans=()):
    S = qkv.shape[0]
    R = ATT_R

    def body(q_ref, kp_ref, kc_ref, vp_ref, vc_ref, b_ref, dy_ref, y_ref, lse_ref, dq_ref, dk_ref, dv_ref, dss_ref,
             s_ref, dp_ref, pb_ref, dsb_ref):
        i = pl.program_id(1)

        @pl.when(i == 0)
        def _():
            dk_ref[...] = jnp.zeros_like(dk_ref)
            dv_ref[...] = jnp.zeros_like(dv_ref)
            dss_ref[...] = jnp.zeros_like(dss_ref)

        lo = lax.broadcasted_iota(jnp.int32, (1, LANES), 1) < 64
        kwin = jnp.concatenate([kp_ref[R - 8 * CHUNK:R, :], kc_ref[...]], axis=0)
        vwin = jnp.concatenate([vp_ref[R - 8 * CHUNK:R, :], vc_ref[...]], axis=0)
        for sub in range(ATT_SUBS):
            rows = slice(128 * sub, 128 * (sub + 1))
            kw = kwin[128 * sub:128 * sub + BAND_KEYS]
            vw = vwin[128 * sub:128 * sub + BAND_KEYS]
            table = jnp.where(i == 0, 1 + sub, 0) if sub < START_SUBS else 0
            qs = _two_heads(q_ref[rows, :], lo)
            dos = _two_heads(dy_ref[rows, :], lo)
            dyy = dy_ref[rows, :].astype(F32) * y_ref[rows, :].astype(F32)
            delta = (jnp.sum(jnp.where(lo, dyy, 0.0), axis=-1, keepdims=True),
                     jnp.sum(jnp.where(lo, 0.0, dyy), axis=-1, keepdims=True))
            lse2 = lse_ref[rows, :]
            s_ref[...] = _dot_nt(qs, kw)
            dp_ref[...] = _dot_nt(dos, vw)
            for hh in range(2):
                lse = lse2[:, 64 * hh:64 * hh + 1]
                for r0 in range(0, 128, ATT_ROWS_BWD):
                    rr = slice(r0, r0 + ATT_ROWS_BWD)
                    both = slice(128 * hh + r0, 128 * hh + r0 + ATT_ROWS_BWD)
                    p = jnp.exp(s_ref[both, :] + b_ref[table, hh, rr, :] - lse[rr])
                    ds = p * (dp_ref[both, :] - delta[hh][rr])
                    dss_ref[hh, rr, :] += ds
                    pb_ref[both, :] = p.astype(BF16)
                    dsb_ref[both, :] = ds.astype(BF16)
            dq = _dot(jnp.concatenate([dsb_ref[0:128, :], dsb_ref[128:256, :]], axis=1), _two_heads(kw, lo))
            dq_ref[rows, :] = dq.astype(BF16)
            start = pl.multiple_of(i * R + 128 * sub, 128)
            dk_ref[pl.ds(start, BAND_KEYS), :] += _dot_tn(dsb_ref[...], qs)
            dv_ref[pl.ds(start, BAND_KEYS), :] += _dot_tn(pb_ref[...], dos)

    acc_spec = pl.BlockSpec((S + 8 * CHUNK, LANES), lambda j, i: (0, j))
    return _call(
        body, name="attn_bwd", grid=(4, S // R),
        out_shape=(jax.ShapeDtypeStruct((S, ATT_W), BF16), jax.ShapeDtypeStruct((S + 8 * CHUNK, ATT_W), F32),
                   jax.ShapeDtypeStruct((S + 8 * CHUNK, ATT_W), F32), jax.ShapeDtypeStruct((8, 128, BAND_KEYS), F32)),
        in_specs=_att_specs() + [pl.BlockSpec((R, LANES), lambda j, i: (i, j))] * 3,
        out_specs=(pl.BlockSpec((R, LANES), lambda j, i: (i, j)), acc_spec, acc_spec,
                   pl.BlockSpec((2, 128, BAND_KEYS), lambda j, i: (j, 0, 0))),
        scratch_shapes=[pltpu.VMEM((256, BAND_KEYS), F32), pltpu.VMEM((256, BAND_KEYS), F32),
                        pltpu.VMEM((256, BAND_KEYS), BF16), pltpu.VMEM((256, BAND_KEYS), BF16)],
        operands=(qkv, qkv, qkv, qkv, qkv, bias, dy_att, y_att, lse), plans=plans)


def _in_bwd(dq, dk, dv, duv, dgates, x, dx1, g1, w_in_t, plans=(), tm=512):
    S = x.shape[0]
    pad_blocks = (8 * CHUNK) // tm

    def body(dq_ref, dk_ref, dv_ref, duv_ref, dg_ref, x_ref, d1_ref, g_ref, wt_ref, dx_ref, dz_ref, dg1_ref):
        i = pl.program_id(0)
        dz_ref[:, 0:ATT_W] = (dq_ref[...].astype(F32) * 0.125).astype(BF16)
        dz_ref[:, ATT_W:2 * ATT_W] = dk_ref[...].astype(BF16)
        dz_ref[:, 2 * ATT_W:3 * ATT_W] = dv_ref[...].astype(BF16)
        dz_ref[:, 3 * ATT_W:3 * ATT_W + 2 * SG_W] = duv_ref[...]
        dz_ref[:, 3 * ATT_W + 2 * SG_W:IN_COLS] = dg_ref[...]
        dh = _dot(dz_ref[...], wt_ref[...])
        xv = x_ref[...]
        dx, dg1 = _rms_bwd(dh, xv, _rstd(xv), g_ref[...])
        dx_ref[...] = d1_ref[...] + dx

        @pl.when(i == 0)
        def _():
            dg1_ref[...] = jnp.zeros_like(dg1_ref)

        dg1_ref[...] += dg1

    shifted = pl.BlockSpec((tm, ATT_W), lambda i: (i + pad_blocks, 0))
    return _call(
        body, name="in_bwd", grid=(S // tm,),
        out_shape=(jax.ShapeDtypeStruct((S, D), F32), jax.ShapeDtypeStruct((S, IN_COLS), BF16),
                   jax.ShapeDtypeStruct((1, D), F32)),
        in_specs=[_rows(tm, ATT_W), shifted, shifted, _rows(tm, 2 * SG_W), _rows(tm, 2 * D), _rows(tm, D),
                  _rows(tm, D), _resident((1, D)), _resident((IN_COLS, D))],
        out_specs=(_rows(tm, D), _rows(tm, IN_COLS), pl.BlockSpec((1, D), lambda i: (0, 0))),
        operands=(dq, dk, dv, duv, dgates, x, dx1, g1, w_in_t), plans=plans)


def _adam_math(w, g, m, v):
    m = ADAM_B1 * m + (1.0 - ADAM_B1) * g
    v = ADAM_B2 * v + (1.0 - ADAM_B2) * (g * g)
    m_hat = m / (1.0 - ADAM_B1 ** ADAM_STEP)
    v_hat = v / (1.0 - ADAM_B2 ** ADAM_STEP)
    delta = -ADAM_LR * (m_hat / (jnp.sqrt(v_hat) + ADAM_EPS) + ADAM_WD * w)
    return delta, m, v


def _adam(parts, w, m, v, tr, name, transposed, after=None):
    P = parts.shape[0]
    R, C = w.shape

    def body(p_ref, w_ref, m_ref, v_ref, *rest):
        g_ref, d_ref, mo_ref, vo_ref = rest[-4:]
        if transposed:
            eye = (lax.broadcasted_iota(jnp.int32, (C, C), 0) == lax.broadcasted_iota(jnp.int32, (C, C), 1)).astype(BF16)
            part = lambda k: _dot_tn(p_ref[k], eye)
        else:
            part = lambda k: p_ref[k].astype(F32)
        g = part(0)
        for k in range(1, P):
            g = g + part(k)
        g_ref[...] = g
        d_ref[...], mo_ref[...], vo_ref[...] = _adam_math(w_ref[...], g, m_ref[...], v_ref[...])

    row = pl.BlockSpec((tr, C), lambda i: (i, 0))
    p_spec = pl.BlockSpec((P, C, tr), lambda i: (0, 0, i)) if transposed else pl.BlockSpec((P, tr, C), lambda i: (0, i, 0))
    extra = [] if after is None else [after]
    return pl.pallas_call(
        body, name=name, grid=(R // tr,),
        out_shape=tuple(jax.ShapeDtypeStruct((R, C), F32) for _ in range(4)),
        in_specs=[p_spec, row, row, row] + [pl.BlockSpec(memory_space=pl.ANY)] * len(extra),
        out_specs=(row, row, row, row),
        compiler_params=_cparams(1),
    )(parts, w, m, v, *extra)


def _my_place():
    return lax.axis_index("x"), lax.axis_index("y"), lax.axis_index("c")


def _other_chips(x, y):
    return [(1 - x, y), (x, 1 - y), (1 - x, 1 - y)]


class _Plan:
    def __init__(self, arrays, out_shapes, sems, start, finish, forward=None, forward_at=0.7):
        self.arrays, self.out_shapes, self.sems = list(arrays), list(out_shapes), list(sems)
        self.start, self.finish, self.forward, self.forward_at = start, finish, forward, forward_at


def _call(body, *, name, grid, in_specs, out_specs, out_shape, operands, scratch_shapes=(), plans=()):
    n_in, n_out, n_scr = len(operands), len(out_shape), len(scratch_shapes)
    p_in = [a for p in plans for a in p.arrays]
    p_out = [s for p in plans for s in p.out_shapes]
    p_sem = [s for p in plans for s in p.sems]
    steps = math.prod(grid)

    def wrapped(*refs):
        ins, refs = refs[:n_in], refs[n_in:]
        pins, refs = refs[:len(p_in)], refs[len(p_in):]
        outs, refs = refs[:n_out], refs[n_out:]
        pouts, refs = refs[:len(p_out)], refs[len(p_out):]
        scr, psems = refs[:n_scr], refs[n_scr:]
        step = 0
        for axis, size in enumerate(grid):
            step = step * size + pl.program_id(axis)
        bound = []
        for p in plans:
            bound.append((p, pins[:len(p.arrays)], pouts[:len(p.out_shapes)], psems[:len(p.sems)]))
            pins, pouts, psems = pins[len(p.arrays):], pouts[len(p.out_shapes):], psems[len(p.sems):]

        @pl.when(step == 0)
        def _():
            for p, a, b, s in bound:
                p.start(a, b, s)

        body(*ins, *outs, *scr)

        for p, a, b, s in bound:
            if p.forward is not None:
                @pl.when(step == min(int(p.forward_at * steps), steps - 1))
                def _(p=p, a=a, b=b, s=s):
                    p.forward(a, b, s)

        @pl.when(step == steps - 1)
        def _():
            for p, a, b, s in bound:
                p.finish(a, b, s)

    hbm = pl.BlockSpec(memory_space=pl.ANY)
    return pl.pallas_call(
        wrapped if plans else body, name=name, grid=grid,
        out_shape=tuple(out_shape) + tuple(p_out),
        in_specs=list(in_specs) + [hbm] * len(p_in),
        out_specs=tuple(out_specs) + tuple([hbm] * len(p_out)),
        scratch_shapes=list(scratch_shapes) + p_sem,
        compiler_params=_cparams(len(grid)),
    )(*operands, *p_in)


def _run_plan(plan, name):
    n_in, n_out = len(plan.arrays), len(plan.out_shapes)

    def body(*refs):
        a, b, s = refs[:n_in], refs[n_in:n_in + n_out], refs[n_in + n_out:]
        plan.start(a, b, s)
        if plan.forward is not None:
            plan.forward(a, b, s)
        plan.finish(a, b, s)

    hbm = pl.BlockSpec(memory_space=pl.ANY)
    return pl.pallas_call(
        body, name=name, out_shape=tuple(plan.out_shapes),
        in_specs=[hbm] * n_in, out_specs=tuple([hbm] * n_out), scratch_shapes=plan.sems,
    )(*plan.arrays)


def _gather_plan(shards, forward_at=0.7):
    n = len(shards)

    def copies(ins, outs, sems):
        send_sems, recv_sems, local_sems = sems
        x, y, c = _my_place()
        me, sibling = (x, y, c), (x, y, 1 - c)
        chips = _other_chips(x, y)

        def idx(p):
            return 4 * p[0] + 2 * p[1] + p[2]

        def copy(a, k, block, to, src=None):
            return pltpu.make_async_remote_copy(
                src_ref=outs[a].at[idx(block)] if src is None else src, dst_ref=outs[a].at[idx(block)],
                send_sem=send_sems.at[a, k], recv_sem=recv_sems.at[a, k], device_id=to, device_id_type=MESH)

        mine = [pltpu.make_async_copy(ins[a], outs[a].at[idx(me)], local_sems.at[a]) for a in range(n)]
        first = []
        for a in range(n):
            first.append(copy(a, 0, me, sibling, src=ins[a]))
            first += [copy(a, 1 + j, me, (*chip, c), src=ins[a]) for j, chip in enumerate(chips)]
        arrived = [copy(a, 1 + j, (*chip, c), me) for j, chip in enumerate(chips) for a in range(n)]
        passed = [copy(a, 4 + j, (*chip, c), sibling) for j, chip in enumerate(chips) for a in range(n)]
        from_sibling = []
        for a in range(n):
            from_sibling.append(copy(a, 0, sibling, me))
            from_sibling += [copy(a, 4 + j, (*chip, 1 - c), me) for j, chip in enumerate(chips)]
        return mine, first, arrived, passed, from_sibling

    def start(ins, outs, sems):
        mine, first, _, _, _ = copies(ins, outs, sems)
        for cp in mine + first:
            cp.start()

    def forward(ins, outs, sems):
        _, _, arrived, passed, _ = copies(ins, outs, sems)
        for landed, onward in zip(arrived, passed):
            landed.wait_recv()
            onward.start()

    def finish(ins, outs, sems):
        mine, first, _, passed, from_sibling = copies(ins, outs, sems)
        for cp in from_sibling:
            cp.wait_recv()
        for cp in first + passed:
            cp.wait_send()
        for cp in mine:
            cp.wait()

    return _Plan(shards, [jax.ShapeDtypeStruct((N_DEV,) + s.shape, s.dtype) for s in shards],
                 [pltpu.SemaphoreType.DMA((n, 7)), pltpu.SemaphoreType.DMA((n, 7)), pltpu.SemaphoreType.DMA((n,))],
                 start, finish, forward, forward_at)


def _sibling_plan(scatter, whole=()):
    ns = len(scatter)
    n = ns + len(whole)

    def copies(ins, outs, sems):
        send_sems, recv_sems = sems
        x, y, c = _my_place()
        out = []
        for a in range(n):
            for k in range(4 if a < ns else 1):
                src = ins[a].at[2 * k + (1 - c)] if a < ns else ins[a]
                dst = outs[a].at[k] if a < ns else outs[a]
                out.append(pltpu.make_async_remote_copy(
                    src_ref=src, dst_ref=dst, send_sem=send_sems.at[a, k], recv_sem=recv_sems.at[a, k],
                    device_id=(x, y, 1 - c), device_id_type=MESH))
        return out

    def start(ins, outs, sems):
        for cp in copies(ins, outs, sems):
            cp.start()

    def finish(ins, outs, sems):
        for cp in copies(ins, outs, sems):
            cp.wait()

    shapes = [jax.ShapeDtypeStruct((4,) + p.shape[1:], p.dtype) for p in scatter]
    shapes += [jax.ShapeDtypeStruct(p.shape, p.dtype) for p in whole]
    return _Plan(list(scatter) + list(whole), shapes,
                 [pltpu.SemaphoreType.DMA((n, 4)), pltpu.SemaphoreType.DMA((n, 4))], start, finish)


def _pair_sums(mine, theirs, c_idx, name):
    n = len(mine)

    def body(c_ref, *refs):
        for a in range(n):
            out = refs[2 * n + a]
            out[...] = (refs[a][...].astype(F32) + refs[n + a][...].astype(F32)).astype(out.dtype)

    def block(arr):
        return (1,) + arr.shape[1:]

    grid_spec = pltpu.PrefetchScalarGridSpec(
        num_scalar_prefetch=1, grid=(4,),
        in_specs=[pl.BlockSpec(block(m), lambda k, c_ref: (2 * k + c_ref[0], 0, 0)) for m in mine]
        + [pl.BlockSpec(block(t), lambda k, c_ref: (k, 0, 0)) for t in theirs],
        out_specs=tuple(pl.BlockSpec(block(t), lambda k, c_ref: (k, 0, 0)) for t in theirs))
    return pl.pallas_call(
        body, name=name, grid_spec=grid_spec,
        out_shape=tuple(jax.ShapeDtypeStruct(t.shape, m.dtype) for m, t in zip(mine, theirs)),
        compiler_params=_cparams(1),
    )(c_idx, *mine, *theirs)


def _peers_plan(arrays):
    n = len(arrays)

    def copies(ins, outs, sems):
        send_sems, recv_sems, local_sems = sems
        x, y, c = _my_place()
        me = 4 * x + 2 * y + c
        out = [pltpu.make_async_copy(ins[a], outs[a].at[me], local_sems.at[a]) for a in range(n)]
        for a in range(n):
            for k in range(N_DEV - 1):
                bits = k + 1
                peer = (x ^ (bits >> 2), y ^ ((bits >> 1) & 1), c ^ (bits & 1))
                out.append(pltpu.make_async_remote_copy(
                    src_ref=ins[a], dst_ref=outs[a].at[me], send_sem=send_sems.at[a, k], recv_sem=recv_sems.at[a, k],
                    device_id=peer, device_id_type=MESH))
        return out

    def start(ins, outs, sems):
        for cp in copies(ins, outs, sems):
            cp.start()

    def finish(ins, outs, sems):
        for cp in copies(ins, outs, sems):
            cp.wait()

    return _Plan(list(arrays), [jax.ShapeDtypeStruct((N_DEV,) + a.shape, a.dtype) for a in arrays],
                 [pltpu.SemaphoreType.DMA((n, N_DEV - 1)), pltpu.SemaphoreType.DMA((n, N_DEV - 1)),
                  pltpu.SemaphoreType.DMA((n,))], start, finish)


def _chips_plan(scatter, whole=()):
    ns = len(scatter)
    n = ns + len(whole)

    def copies(ins, outs, sems):
        send_sems, recv_sems, local_sems = sems
        x, y, c = _my_place()
        my_chip = 2 * x + y

        def src(a, k):
            return ins[a].at[k] if a < ns else ins[a]

        local = [pltpu.make_async_copy(src(a, my_chip), outs[a].at[my_chip], local_sems.at[a]) for a in range(n)]
        remote = []
        for a in range(n):
            for j, (px, py) in enumerate(_other_chips(x, y)):
                remote.append(pltpu.make_async_remote_copy(
                    src_ref=src(a, 2 * px + py), dst_ref=outs[a].at[my_chip],
                    send_sem=send_sems.at[a, j], recv_sem=recv_sems.at[a, j],
                    device_id=(px, py, c), device_id_type=MESH))
        return local + remote

    def start(ins, outs, sems):
        for cp in copies(ins, outs, sems):
            cp.start()

    def finish(ins, outs, sems):
        for cp in copies(ins, outs, sems):
            cp.wait()

    shapes = [jax.ShapeDtypeStruct(s.shape, s.dtype) for s in scatter]
    shapes += [jax.ShapeDtypeStruct((4,) + s.shape, s.dtype) for s in whole]
    return _Plan(list(scatter) + list(whole), shapes,
                 [pltpu.SemaphoreType.DMA((n, 3)), pltpu.SemaphoreType.DMA((n, 3)), pltpu.SemaphoreType.DMA((n,))],
                 start, finish)


def _chip_copies(src_ref, land_ref, send_sems, recv_sems):
    x, y, c = _my_place()
    return [pltpu.make_async_remote_copy(
        src_ref=src_ref.at[2 * px + py], dst_ref=land_ref.at[2 * x + y], send_sem=send_sems.at[j],
        recv_sem=recv_sems.at[j], device_id=(px, py, c), device_id_type=MESH)
        for j, (px, py) in enumerate(_other_chips(x, y))]


def _chips_start(src, name):
    def body(src_ref, land_ref, send_sems, recv_sems, src_thru, land_thru, token):
        for cp in _chip_copies(src_ref, land_ref, send_sems, recv_sems):
            cp.start()
        token[...] = jnp.zeros_like(token)

    hbm = pl.BlockSpec(memory_space=pltpu.HBM)
    sem = pl.BlockSpec(memory_space=pltpu.SEMAPHORE)
    return pl.pallas_call(
        body, name=name,
        out_shape=(pltpu.SemaphoreType.DMA((3,)), pltpu.SemaphoreType.DMA((3,)), pltpu.HBM(src.shape, src.dtype),
                   pltpu.HBM(src.shape, src.dtype), jax.ShapeDtypeStruct((8, LANES), F32)),
        in_specs=(hbm, hbm), out_specs=(sem, sem, hbm, hbm, pl.BlockSpec(memory_space=pltpu.VMEM)),
        input_output_aliases={0: 2, 1: 3},
        compiler_params=pltpu.CompilerParams(has_side_effects=pltpu.SideEffectType.DATAFLOW_SIDE_EFFECTING),
    )(pltpu.with_memory_space_constraint(src, pltpu.HBM),
      pltpu.with_memory_space_constraint(jnp.zeros(src.shape, src.dtype), pltpu.HBM))


def _chips_wait(send_sems, recv_sems, src_thru, land_thru, after, name):
    def body(src_ref, land_ref, send_sems, recv_sems, after_ref, src_dead, got_ref):
        for cp in _chip_copies(src_ref, land_ref, send_sems, recv_sems):
            cp.wait_send()
            cp.wait_recv()

    hbm = pl.BlockSpec(memory_space=pltpu.HBM)
    sem = pl.BlockSpec(memory_space=pltpu.SEMAPHORE)
    return pl.pallas_call(
        body, name=name,
        out_shape=(pltpu.HBM(src_thru.shape, src_thru.dtype), pltpu.HBM(land_thru.shape, land_thru.dtype)),
        in_specs=(hbm, hbm, sem, sem, pl.BlockSpec(memory_space=pl.ANY)), out_specs=(hbm, hbm),
        input_output_aliases={0: 0, 1: 1},
        compiler_params=pltpu.CompilerParams(has_side_effects=pltpu.SideEffectType.DATAFLOW_SIDE_EFFECTING),
    )(src_thru, land_thru, send_sems, recv_sems, after)


def _small_update(parts, w, m, v, loss_parts, name):
    n = len(parts)

    def total(ref):
        acc = ref[0]
        for k in range(1, ref.shape[0]):
            acc = acc + ref[k]
        return acc

    def body(*refs):
        p_refs, w_refs, m_refs, v_refs = (refs[i * n:(i + 1) * n] for i in range(4))
        lp_ref = refs[4 * n]
        outs = refs[4 * n + 1:]
        g_refs, d_refs, mo_refs, vo_refs = (outs[i * n:(i + 1) * n] for i in range(4))
        for a in range(n):
            g = total(p_refs[a])
            g_refs[a][...] = g
            d_refs[a][...], mo_refs[a][...], vo_refs[a][...] = _adam_math(w_refs[a][...], g, m_refs[a][...],
                                                                          v_refs[a][...])
        outs[4 * n][...] = total(lp_ref)

    vm = pl.BlockSpec(memory_space=pltpu.VMEM)
    shapes = [jax.ShapeDtypeStruct(t.shape, F32) for _ in range(4) for t in w]
    shapes.append(jax.ShapeDtypeStruct(loss_parts.shape[1:], F32))
    outs = pl.pallas_call(
        body, name=name, out_shape=tuple(shapes),
        in_specs=[vm] * (4 * n + 1), out_specs=tuple([vm] * (4 * n + 1)),
        compiler_params=pltpu.CompilerParams(vmem_limit_bytes=VMEM_LIMIT),
    )(*parts, *w, *m, *v, loss_parts)
    return outs[0:n], outs[n:2 * n], outs[2 * n:3 * n], outs[3 * n:4 * n], outs[4 * n]


BIG = [("w_in", 1), ("w_branch_att", 1), ("w_branch_sg", 1), ("w_out", 0), ("w_xq", 0), ("w_xkv", 1), ("w_xo", 0),
       ("w_ffn_in", 1), ("w_ffn_out", 0)]
SMALL = [("norm_mix_g", (1, D)), ("rel_bias", (8, NREL)), ("sg_ln_g", (8, 64)), ("sg_ln_b", (8, 64)),
         ("sg_w", (8, 128, 128)), ("sg_b", (8, 128)), ("norm_xattn_g", (1, D)), ("norm_mem_g", (1, D)),
         ("norm_ffn_g", (1, D)), ("norm_final_g", (1, D))]
ADAM_ROWS = {"w_in": 192, "w_branch_att": 512, "w_branch_sg": 512, "w_xkv": 1024, "w_ffn_in": 176}


def _full(gathered):
    return gathered.reshape(N_DEV * gathered.shape[1], gathered.shape[2])


def _blocks(grad):
    return grad.reshape(N_DEV, grad.shape[0] // N_DEV, grad.shape[1])


def kernel(x, mem, norm_mix_g, w_in, rel_bias, sg_ln_g, sg_ln_b, sg_w, sg_b, w_branch_att, w_branch_sg, w_out, norm_xattn_g, norm_mem_g, w_xq, w_xkv, w_xo, norm_ffn_g, w_ffn_in, w_ffn_out, norm_final_g, loss_target, m_norm_mix_g, m_w_in, m_rel_bias, m_sg_ln_g, m_sg_ln_b, m_sg_w, m_sg_b, m_w_branch_att, m_w_branch_sg, m_w_out, m_norm_xattn_g, m_norm_mem_g, m_w_xq, m_w_xkv, m_w_xo, m_norm_ffn_g, m_w_ffn_in, m_w_ffn_out, m_norm_final_g, v_norm_mix_g, v_w_in, v_rel_bias, v_sg_ln_g, v_sg_ln_b, v_sg_w, v_sg_b, v_w_branch_att, v_w_branch_sg, v_w_out, v_norm_xattn_g, v_norm_mem_g, v_w_xq, v_w_xkv, v_w_xo, v_norm_ffn_g, v_w_ffn_in, v_w_ffn_out, v_norm_final_g):
    args = dict(locals())
    big_names = [n for n, _ in BIG]
    small_names = [n for n, _ in SMALL]
    S = x.shape[1]

    x, mem, tgt = x.reshape(S, D), mem.reshape(MEM, D), loss_target.reshape(S, D)
    small = {n: args[n].reshape(shape) for n, shape in SMALL}
    g1, g2, g3 = small["norm_mix_g"], small["norm_xattn_g"], small["norm_ffn_g"]
    g_mem, g4 = small["norm_mem_g"], small["norm_final_g"]
    lng = small["sg_ln_g"].reshape(1, SG_W)
    lnb = small["sg_ln_b"].reshape(1, SG_W)
    b_exp = jnp.broadcast_to(small["sg_b"].T[:, :, None], (128, 8, 64)).reshape(128, SG_W)
    rel_pad = jnp.pad(small["rel_bias"], ((0, 0), (0, 384 - NREL)))
    c_idx = lax.axis_index("c").astype(jnp.int32).reshape(1)

    shard = {n: (args[n][0].T if axis == 1 else args[n][0]).astype(BF16) for n, axis in BIG}
    h, w_in_gathered = _norm_in(x, g1, plans=[_gather_plan([shard["w_in"]], forward_at=1.0)])
    w_in_t = _full(w_in_gathered)
    bias = _bias_table(rel_pad)
    mix_names = ["w_branch_att", "w_branch_sg", "w_out", "w_xq", "w_xkv", "w_xo"]
    qkv, uv, gates, *got = _in_proj(h, w_in_t, plans=[_gather_plan([shard[n] for n in mix_names])])
    wba_t, wbs_t, w_out_f, w_xq_f, w_xkv_t, w_xo_f = (_full(g) for g in got)
    y_att, lse, *got = _attn_fwd(qkv, bias, plans=[_gather_plan([shard["w_ffn_in"], shard["w_ffn_out"]])])
    w_ffn_in_t, w_ffn_out_f = (_full(g) for g in got)
    y_sg = _sgu_fwd(uv, lng, lnb, small["sg_w"], b_exp)
    x1 = _merge_fwd(x, y_att, y_sg, gates, wba_t, wbs_t, w_out_f)
    kv, mn = _mem_kv(mem, g_mem, w_xkv_t)
    x2, hx, qx, o_x = _xattn_fwd(x1, g2, w_xq_f, kv, w_xo_f)
    dx3, gu, hf, act, loss_part, dg4 = _ffn_fwd(x2, tgt, g3, w_ffn_in_t, w_ffn_out_f, g4)


    dx2, dgu, dg3 = _ffn_bwd(dx3, gu, x2, g3, w_ffn_out_f, w_ffn_in_t)
    ffn_names = ["w_ffn_out", "w_ffn_in"]
    ffn_mine = [_blocks(_dw(act, dx3, 1408, 1024, DW_TOKENS, "dw_ffn_out")),
                _blocks(_dw(dgu, hf, 1408, 1024, DW_TOKENS, "dw_ffn_in"))]
    dx1, dq_x, dkv, dg2, *ffn_theirs = _xattn_bwd(dx2, x1, qx, g2, w_xq_f, w_xo_f, kv,
                                                  plans=[_sibling_plan(ffn_mine)])
    ffn_chip = _pair_sums(ffn_mine, ffn_theirs, c_idx, "rs_pair_ffn")
    d_xkv, dg_mem = _mem_kv_bwd(dkv, mem, g_mem, mn, w_xkv_t)
    merged, d_a, d_b, dy_att, dy_sg, dgates = _merge_bwd(dx1, y_att, y_sg, gates, wba_t, wbs_t, w_out_f)
    mid_names = ["w_xo", "w_xq", "w_xkv", "w_out", "w_branch_att", "w_branch_sg"]
    mid_mine = [_blocks(g) for g in (
        _dw(o_x, dx2, 1024, 1024, DW_TOKENS, "dw_xo"), _dw(hx, dq_x, 1024, 1024, DW_TOKENS, "dw_xq"), d_xkv,
        _dw(merged, dx1, 1024, 1024, DW_TOKENS, "dw_out"), _dw(d_a, y_att, 1024, 512, DW_TOKENS, "dw_branch_att"),
        _dw(d_b, y_sg, 1024, 512, DW_TOKENS, "dw_branch_sg"))]
    duv, d_sgw, d_bx, d_lng, d_lnb, *got = _sgu_bwd(uv, dy_sg, lng, lnb, small["sg_w"], b_exp,
                                                    plans=[_chips_plan(ffn_chip), _sibling_plan(mid_mine)])
    ffn_all, mid_theirs = got[:2], got[2:]
    mid_chip = _pair_sums(mid_mine, mid_theirs, c_idx, "rs_pair_mid")
    dq, dk, dv, ds_sum, *mid_all = _attn_bwd(qkv, dy_att, y_att, lse, bias, plans=[_chips_plan(mid_chip)])
    d_rel = _bias_grad(ds_sum)

    grad_x, dz, dg1 = _in_bwd(dq, dk, dv, duv, dgates, x, dx1, g1, w_in_t)

    gs = {"norm_mix_g": dg1, "rel_bias": d_rel[:, :NREL], "sg_ln_g": d_lng.reshape(8, 64),
          "sg_ln_b": d_lnb.reshape(8, 64), "sg_w": d_sgw, "sg_b": d_bx.reshape(128, 8, 64)[:, :, 0].T,
          "norm_xattn_g": dg2, "norm_mem_g": dg_mem, "norm_ffn_g": dg3, "norm_final_g": dg4}
    d_in, *everyone = _dw(dz, h, 1152, 1024, DW_TOKENS, "dw_in",
                          plans=[_peers_plan([gs[n] for n in small_names] + [loss_part])])

    in_mine = [_blocks(d_in)]
    (in_theirs,) = _run_plan(_sibling_plan(in_mine), "rs_sibling")
    (in_chip,) = _pair_sums(in_mine, [in_theirs], c_idx, "rs_pair_w_in")
    send_sems, recv_sems, in_chip_thru, landing, token = _chips_start(in_chip, "rs_chips_start")
    all_parts = dict(zip(ffn_names + mid_names, list(ffn_all) + list(mid_all)))

    def adam(n, axis, parts, after):
        wmv = [args[p + n][0] for p in ("", "m_", "v_")]
        if axis == 1 and wmv[0].shape[1] % LANES != 0:
            outs = _adam(parts, *(t.T for t in wmv), ADAM_ROWS[n], "adam_" + n, transposed=False, after=after)
            return [t.T[None] for t in outs]
        tr = ADAM_ROWS[n] if axis == 1 else wmv[0].shape[0]
        return [t[None] for t in _adam(parts, *wmv, tr, "adam_" + n, transposed=(axis == 1), after=after)]

    res = {n: adam(n, axis, all_parts[n], token) for n, axis in BIG if n != "w_in"}
    in_chip, landing = _chips_wait(send_sems, recv_sems, in_chip_thru, landing, res["w_ffn_in"][1], "rs_chips_wait")
    my_chip = 2 * lax.axis_index("x") + lax.axis_index("y")
    own = lax.dynamic_slice_in_dim(in_chip, my_chip, 1, axis=0)
    res["w_in"] = adam("w_in", 1, lax.dynamic_update_slice_in_dim(landing, own, my_chip, axis=0), None)
    small_res = _small_update(
        everyone[:-1], [small[n] for n in small_names],
        [args["m_" + n].reshape(s) for n, s in SMALL], [args["v_" + n].reshape(s) for n, s in SMALL],
        everyone[-1], "adam_small")
    for i, n in enumerate(small_names):
        res[n] = [small_res[k][i].reshape(args[n].shape) for k in range(4)]
    loss = small_res[4][0, 0]

    order = ["norm_mix_g", "w_in", "rel_bias", "sg_ln_g", "sg_ln_b", "sg_w", "sg_b", "w_branch_att", "w_branch_sg",
             "w_out", "norm_xattn_g", "norm_mem_g", "w_xq", "w_xkv", "w_xo", "norm_ffn_g", "w_ffn_in", "w_ffn_out",
             "norm_final_g"]
    outs = [loss, grad_x.reshape(1, S, D)]
    for k in range(4):
        outs += [res[n][k] for n in order]
    return tuple(outs)
```

```python
import math

import jax
import jax.numpy as jnp
from jax import lax
from jax.experimental import pallas as pl
from jax.experimental.pallas import tpu as pltpu

F32 = jnp.float32
BF16 = jnp.bfloat16

D = 1024
ATT_W = 512
SG_W = 512
IN_COLS = 4608
DFF = 2816
MEM = 256
XH = 4
CHUNK = 64
BAND_KEYS = 640
ATT_R = 4096
ATT_SUBS = ATT_R // 128
START_SUBS = 8 * CHUNK // 128
ATT_ROWS = 32
ATT_ROWS_BWD = 16
DW_TOKENS = 2048
IN_CHUNK = 512
FF_CHUNK = 256
REL_CLIP = 128
NREL = 2 * REL_CLIP + 1
EPS = 1e-6
NEG = -1e30
N_DEV = 8

ADAM_LR = 0.001
ADAM_B1 = 0.9
ADAM_B2 = 0.999
ADAM_EPS = 1e-08
ADAM_WD = 0.01
ADAM_STEP = 10

LANES = 128
VMEM_LIMIT = 56 * 1024 * 1024
MESH = pl.DeviceIdType.MESH


def _cparams(n_axes):
    return pltpu.CompilerParams(dimension_semantics=("arbitrary",) * n_axes, vmem_limit_bytes=VMEM_LIMIT)


def _resident(shape):
    zeros = (0,) * len(shape)
    return pl.BlockSpec(shape, lambda *_: zeros, pipeline_mode=pl.Buffered(1))


def _rows(tm, cols, col_block=0):
    return pl.BlockSpec((tm, cols), lambda i: (i, col_block))


def _sigmoid(x):
    return pl.reciprocal(1.0 + jnp.exp(-x), approx=True)


_GELU_C = math.sqrt(2.0 / math.pi)


def _gelu(x):
    t = jnp.tanh(_GELU_C * (x + 0.044715 * (x * x * x)))
    return x * (0.5 * (1.0 + t))


def _gelu_and_grad(x):
    x2 = x * x
    t = jnp.tanh(_GELU_C * (x + 0.044715 * (x2 * x)))
    cdf = 0.5 * (1.0 + t)
    dcdf = 0.5 * (1.0 - t * t) * (_GELU_C * (1.0 + 3.0 * 0.044715 * x2))
    return x * cdf, cdf + x * dcdf


def _rstd(x):
    return lax.rsqrt(jnp.mean(x * x, axis=-1, keepdims=True) + EPS)


def _rms_bwd(dh, x, r, g):
    xh = x * r
    dxh = dh * g
    dx = r * (dxh - xh * jnp.mean(dxh * xh, axis=-1, keepdims=True))
    dg = jnp.sum(dh * xh, axis=0, keepdims=True)
    return dx, dg


def _group_sum64(x):
    r = lax.broadcasted_iota(jnp.int32, (LANES, LANES), 0) // 64
    c = lax.broadcasted_iota(jnp.int32, (LANES, LANES), 1) // 64
    same_group = (r == c).astype(BF16)

    def one(v):
        hi = v.astype(BF16)
        rest = v - hi.astype(F32)
        mid = rest.astype(BF16)
        lo = (rest - mid.astype(F32)).astype(BF16)
        return _dot(hi, same_group) + _dot(mid, same_group) + _dot(lo, same_group)

    pieces = [one(x[:, LANES * j:LANES * (j + 1)]) for j in range(x.shape[1] // LANES)]
    return pieces[0] if len(pieces) == 1 else jnp.concatenate(pieces, axis=1)


def _dot(a, b):
    return jnp.dot(a, b, preferred_element_type=F32)


def _dot_nt(a, b):
    return lax.dot_general(a, b, (((1,), (1,)), ((), ())), preferred_element_type=F32)


def _dot_tn(a, b):
    return lax.dot_general(a, b, (((0,), (0,)), ((), ())), preferred_element_type=F32)


DIAGS = 768


def _diag_onehot():
    r_idx = lax.broadcasted_iota(jnp.int32, (384, DIAGS), 0)
    t_idx = lax.broadcasted_iota(jnp.int32, (384, DIAGS), 1)
    dist = (8 * CHUNK + 127) - t_idx
    return (jnp.clip(dist, -REL_CLIP, REL_CLIP) + REL_CLIP == r_idx).astype(F32)


def _shift_rows(x, reverse):
    row = lax.broadcasted_iota(jnp.int32, x.shape, 0)
    for k in range(7):
        amt = (DIAGS - (1 << k)) if reverse else (1 << k)
        x = jnp.where(((row >> k) & 1) == 1, pltpu.roll(x, amt, 1), x)
    return x


N_TABLES = 1 + START_SUBS


def _bias_table(rel_bias_pad):
    def body(rb_ref, out_ref):
        per_diag = jnp.dot(rb_ref[...], _diag_onehot(), preferred_element_type=F32,
                           precision=lax.Precision.HIGHEST)
        a = lax.broadcasted_iota(jnp.int32, (128, BAND_KEYS), 0)
        b = lax.broadcasted_iota(jnp.int32, (128, BAND_KEYS), 1)
        band = (b // CHUNK >= a // CHUNK) & (b // CHUNK <= a // CHUNK + 8)
        for h in range(8):
            rows = jnp.broadcast_to(per_diag[h:h + 1, :], (128, DIAGS))
            table = _shift_rows(pltpu.roll(rows, DIAGS - 127, 1), reverse=False)[:, :BAND_KEYS]
            out_ref[0, h] = jnp.where(band, table, NEG)
            for s in range(START_SUBS):
                out_ref[1 + s, h] = jnp.where(band & (b >= 8 * CHUNK - 128 * s), table, NEG)

    return pl.pallas_call(
        body, name="bias_table",
        out_shape=jax.ShapeDtypeStruct((N_TABLES, 8, 128, BAND_KEYS), F32),
        in_specs=[pl.BlockSpec(memory_space=pltpu.VMEM)],
        out_specs=pl.BlockSpec(memory_space=pltpu.VMEM),
        compiler_params=pltpu.CompilerParams(vmem_limit_bytes=VMEM_LIMIT),
    )(rel_bias_pad)


def _bias_grad(ds_sum):
    def body(ds_ref, out_ref):
        sums = []
        for h in range(8):
            padded = jnp.concatenate([ds_ref[h], jnp.zeros((128, DIAGS - BAND_KEYS), F32)], axis=1)
            skewed = pltpu.roll(_shift_rows(padded, reverse=True), 127, 1)
            sums.append(jnp.sum(skewed, axis=0, keepdims=True))
        per_diag = jnp.concatenate(sums, axis=0)
        out_ref[...] = lax.dot_general(per_diag, _diag_onehot(), (((1,), (1,)), ((), ())),
                                       preferred_element_type=F32, precision=lax.Precision.HIGHEST)

    return pl.pallas_call(
        body, name="bias_grad",
        out_shape=jax.ShapeDtypeStruct((8, 384), F32),
        in_specs=[pl.BlockSpec(memory_space=pltpu.VMEM)],
        out_specs=pl.BlockSpec(memory_space=pltpu.VMEM),
    )(ds_sum)


def _norm_in(x, g1, plans=(), tm=512):
    S = x.shape[0]

    def body(x_ref, g_ref, h_ref):
        xv = x_ref[...]
        h_ref[...] = (xv * _rstd(xv) * g_ref[...]).astype(BF16)

    return _call(
        body, name="norm_in", grid=(S // tm,),
        out_shape=(jax.ShapeDtypeStruct((S, D), BF16),),
        in_specs=[_rows(tm, D), _resident((1, D))], out_specs=(_rows(tm, D),),
        operands=(x, g1), plans=plans)


def _in_proj(h, w_in_t, plans=(), tm=1024):
    S = h.shape[0]

    def body(h_ref, w_ref, qkv_ref, uv_ref, gate_ref):
        h = h_ref[...]
        w = IN_CHUNK
        for c in range(IN_COLS // w):
            zc = _dot_nt(h, w_ref[w * c:w * (c + 1), :])
            start = w * c
            if start < ATT_W:
                qkv_ref[:, start:start + w] = (zc * 0.125).astype(BF16)
            elif start < 3 * ATT_W:
                qkv_ref[:, start:start + w] = zc.astype(BF16)
            elif start < 3 * ATT_W + 2 * SG_W:
                uv_ref[:, start - 3 * ATT_W:start - 3 * ATT_W + w] = zc.astype(BF16)
            else:
                gate_ref[:, start - 3 * ATT_W - 2 * SG_W:start - 3 * ATT_W - 2 * SG_W + w] = zc.astype(BF16)

    return _call(
        body, name="in_proj", grid=(S // tm,),
        out_shape=(jax.ShapeDtypeStruct((S, 3 * ATT_W), BF16), jax.ShapeDtypeStruct((S, 2 * SG_W), BF16),
                   jax.ShapeDtypeStruct((S, 2 * D), BF16)),
        in_specs=[_rows(tm, D), _resident((IN_COLS, D))],
        out_specs=(_rows(tm, 3 * ATT_W), _rows(tm, 2 * SG_W), _rows(tm, 2 * D)),
        operands=(h, w_in_t), plans=plans)


def _two_heads(a, lo):
    return jnp.concatenate([jnp.where(lo, a, 0), jnp.where(lo, 0, a)], axis=0)


def _att_specs():
    R = ATT_R
    q = pl.BlockSpec((R, LANES), lambda j, i: (i, j))
    kp = pl.BlockSpec((R, LANES), lambda j, i: (jnp.maximum(i - 1, 0), 4 + j))
    kc = pl.BlockSpec((R, LANES), lambda j, i: (i, 4 + j))
    vp = pl.BlockSpec((R, LANES), lambda j, i: (jnp.maximum(i - 1, 0), 8 + j))
    vc = pl.BlockSpec((R, LANES), lambda j, i: (i, 8 + j))
    bias = pl.BlockSpec((N_TABLES, 2, 128, BAND_KEYS), lambda j, i: (0, j, 0, 0))
    return [q, kp, kc, vp, vc, bias]


def _attn_fwd(qkv, bias, plans=()):
    S = qkv.shape[0]
    R = ATT_R

    def body(q_ref, kp_ref, kc_ref, vp_ref, vc_ref, b_ref, o_ref, lse_ref, s_ref, p_ref, l_ref):
        i = pl.program_id(1)
        lo = lax.broadcasted_iota(jnp.int32, (1, LANES), 1) < 64
        kwin = jnp.concatenate([kp_ref[R - 8 * CHUNK:R, :], kc_ref[...]], axis=0)
        vwin = jnp.concatenate([vp_ref[R - 8 * CHUNK:R, :], vc_ref[...]], axis=0)
        for sub in range(ATT_SUBS):
            q2 = q_ref[128 * sub:128 * (sub + 1), :]
            kw = kwin[128 * sub:128 * sub + BAND_KEYS]
            vw = vwin[128 * sub:128 * sub + BAND_KEYS]
            table = jnp.where(i == 0, 1 + sub, 0) if sub < START_SUBS else 0
            s_ref[...] = _dot_nt(_two_heads(q2, lo), kw)
            for hh in range(2):
                for r0 in range(0, 128, ATT_ROWS):
                    rows = slice(128 * hh + r0, 128 * hh + r0 + ATT_ROWS)
                    s = s_ref[rows, :] + b_ref[table, hh, r0:r0 + ATT_ROWS, :]
                    top = jnp.max(s, axis=-1, keepdims=True)
                    p = jnp.exp(s - top)
                    total = jnp.sum(p, axis=-1, keepdims=True)
                    p_ref[rows, :] = (p / total).astype(BF16)
                    l_ref[rows, :] = jnp.broadcast_to(top + jnp.log(total), (ATT_ROWS, LANES))
            o = _dot(jnp.concatenate([p_ref[0:128, :], p_ref[128:256, :]], axis=1), _two_heads(vw, lo))
            o_ref[128 * sub:128 * (sub + 1), :] = o.astype(BF16)
            lse_ref[128 * sub:128 * (sub + 1), :] = jnp.where(lo, l_ref[0:128, :], l_ref[128:256, :])

    blk = pl.BlockSpec((R, LANES), lambda j, i: (i, j))
    return _call(
        body, name="attn_fwd", grid=(4, S // R),
        out_shape=(jax.ShapeDtypeStruct((S, ATT_W), BF16), jax.ShapeDtypeStruct((S, ATT_W), F32)),
        in_specs=_att_specs(), out_specs=(blk, blk),
        scratch_shapes=[pltpu.VMEM((256, BAND_KEYS), F32), pltpu.VMEM((256, BAND_KEYS), BF16),
                        pltpu.VMEM((256, LANES), F32)],
        operands=(qkv, qkv, qkv, qkv, qkv, bias), plans=plans)


def _sg_mask():
    t = lax.broadcasted_iota(jnp.int32, (128, 128), 0)
    s = lax.broadcasted_iota(jnp.int32, (128, 128), 1)
    return (s // CHUNK) <= (t // CHUNK)


def _sg_layernorm(gv, lng, lnb):
    mu = _group_sum64(gv) * (1.0 / 64)
    xc = gv - mu
    var = _group_sum64(xc * xc) * (1.0 / 64)
    rstd = lax.rsqrt(var + EPS)
    vhat = xc * rstd
    return vhat * lng + lnb, vhat, rstd


def _sgu_fwd(uv, lng, lnb, sg_w, b_exp, tm=1024):
    S = uv.shape[0]

    def body(uv_ref, lng_ref, lnb_ref, w_ref, b_ref, y_ref):
        lane = lax.broadcasted_iota(jnp.int32, (1, LANES), 1)
        lo = lane < 64
        mask = _sg_mask()
        gu = _gelu(uv_ref[:, 0:SG_W].astype(F32))
        vln, _, _ = _sg_layernorm(_gelu(uv_ref[:, SG_W:2 * SG_W].astype(F32)), lng_ref[...], lnb_ref[...])
        for gp in range(4):
            w0 = jnp.where(mask, w_ref[2 * gp], 0).astype(BF16)
            w1 = jnp.where(mask, w_ref[2 * gp + 1], 0).astype(BF16)
            cols = slice(128 * gp, 128 * (gp + 1))
            for n in range(tm // 128):
                rows = slice(128 * n, 128 * (n + 1))
                vl = vln[rows, cols]
                sv = (_dot(w0, jnp.where(lo, vl, 0).astype(BF16)) + _dot(w1, jnp.where(lo, 0, vl).astype(BF16))
                      + b_ref[:, cols])
                y_ref[rows, cols] = (gu[rows, cols] * sv).astype(BF16)

    return pl.pallas_call(
        body, name="sgu_fwd", grid=(S // tm,),
        out_shape=jax.ShapeDtypeStruct((S, SG_W), BF16),
        in_specs=[_rows(tm, 2 * SG_W), _resident((1, SG_W)), _resident((1, SG_W)),
                  _resident((8, 128, 128)), _resident((128, SG_W))],
        out_specs=_rows(tm, SG_W),
        compiler_params=_cparams(1),
    )(uv, lng, lnb, sg_w, b_exp)


def _merge_fwd(x, y_att, y_sg, gates, wba_t, wbs_t, w_out, tm=512):
    S = x.shape[0]

    def body(x_ref, ya_ref, ys_ref, g_ref, wba_ref, wbs_ref, wo_ref, x1_ref):
        a = _dot_nt(ya_ref[...], wba_ref[...])
        b = _dot_nt(ys_ref[...], wbs_ref[...])
        merged = _sigmoid(g_ref[:, 0:D].astype(F32)) * a + _sigmoid(g_ref[:, D:2 * D].astype(F32)) * b
        x1_ref[...] = x_ref[...] + _dot(merged.astype(BF16), wo_ref[...])

    return pl.pallas_call(
        body, name="merge_fwd", grid=(S // tm,),
        out_shape=jax.ShapeDtypeStruct((S, D), F32),
        in_specs=[_rows(tm, D), _rows(tm, ATT_W), _rows(tm, SG_W), _rows(tm, 2 * D),
                  _resident((D, ATT_W)), _resident((D, SG_W)), _resident((D, D))],
        out_specs=_rows(tm, D),
        compiler_params=_cparams(1),
    )(x, y_att, y_sg, gates, wba_t, wbs_t, w_out)


def _mem_kv(mem, g_mem, w_xkv_t):
    def body(m_ref, g_ref, w_ref, kv_ref, mn_ref):
        mv = m_ref[...]
        mn = (mv * _rstd(mv) * g_ref[...]).astype(BF16)
        mn_ref[...] = mn
        kv_ref[...] = _dot_nt(mn, w_ref[...]).astype(BF16)

    vm = pl.BlockSpec(memory_space=pltpu.VMEM)
    return pl.pallas_call(
        body, name="mem_kv",
        out_shape=(jax.ShapeDtypeStruct((MEM, 2 * D), BF16), jax.ShapeDtypeStruct((MEM, D), BF16)),
        in_specs=[vm, vm, vm], out_specs=(vm, vm),
        compiler_params=pltpu.CompilerParams(vmem_limit_bytes=VMEM_LIMIT),
    )(mem, g_mem, w_xkv_t)


def _xatt_head(qx, kv_ref, h):
    hs = slice(256 * h, 256 * (h + 1))
    s = _dot_nt(qx[:, hs], kv_ref[:, hs])
    p = jnp.exp(s - jnp.max(s, axis=-1, keepdims=True))
    return p / jnp.sum(p, axis=-1, keepdims=True)


def _xattn_fwd(x1, g2, w_xq, kv, w_xo, tm=1024):
    S = x1.shape[0]

    def body(x_ref, g_ref, wq_ref, kv_ref, wo_ref, x2_ref, hx_ref, qx_ref, o_ref):
        xv = x_ref[...]
        hx = (xv * _rstd(xv) * g_ref[...]).astype(BF16)
        hx_ref[...] = hx
        qx = (_dot(hx, wq_ref[...]) * (1.0 / 16)).astype(BF16)
        qx_ref[...] = qx
        for h in range(XH):
            p = _xatt_head(qx, kv_ref, h)
            o_ref[:, 256 * h:256 * (h + 1)] = _dot(p.astype(BF16), kv_ref[:, D + 256 * h:D + 256 * (h + 1)]).astype(BF16)
        x2_ref[...] = xv + _dot(o_ref[...], wo_ref[...])

    return pl.pallas_call(
        body, name="xattn_fwd", grid=(S // tm,),
        out_shape=(jax.ShapeDtypeStruct((S, D), F32),) + (jax.ShapeDtypeStruct((S, D), BF16),) * 3,
        in_specs=[_rows(tm, D), _resident((1, D)), _resident((D, D)), _resident((MEM, 2 * D)), _resident((D, D))],
        out_specs=(_rows(tm, D),) * 4,
        compiler_params=_cparams(1),
    )(x1, g2, w_xq, kv, w_xo)


def _ffn_fwd(x2, tgt, g3, w_ffn_in_t, w_ffn_out, g4, tm=256):
    S = x2.shape[0]

    def body(x_ref, t_ref, g3_ref, wi_ref, wo_ref, g4_ref, dx3_ref, gu_ref, hf_ref, act_ref, loss_ref, dg4_ref):
        i = pl.program_id(0)
        xv = x_ref[...]
        hf = (xv * _rstd(xv) * g3_ref[...]).astype(BF16)
        hf_ref[...] = hf
        for c in range(DFF // FF_CHUNK):
            cs = slice(FF_CHUNK * c, FF_CHUNK * (c + 1))
            us = slice(DFF + FF_CHUNK * c, DFF + FF_CHUNK * (c + 1))
            gate = _dot_nt(hf, wi_ref[cs, :])
            up = _dot_nt(hf, wi_ref[us, :])
            gu_ref[:, cs] = gate.astype(BF16)
            gu_ref[:, us] = up.astype(BF16)
            act_ref[:, cs] = ((gate * _sigmoid(gate)) * up).astype(BF16)
        acc = xv + _dot(act_ref[...], wo_ref[...])
        r4 = _rstd(acc)
        g4 = g4_ref[...]
        diff = acc * r4 * g4 - t_ref[...]
        dy = diff * (1.0 / D)
        dx3, dg4 = _rms_bwd(dy, acc, r4, g4)
        dx3_ref[...] = dx3
        part = 0.5 * jnp.sum(jnp.mean(diff * diff, axis=-1, keepdims=True))

        @pl.when(i == 0)
        def _():
            loss_ref[...] = jnp.zeros_like(loss_ref)
            dg4_ref[...] = jnp.zeros_like(dg4_ref)

        loss_ref[...] += jnp.full(loss_ref.shape, part, F32)
        dg4_ref[...] += dg4

    return pl.pallas_call(
        body, name="ffn_fwd", grid=(S // tm,),
        out_shape=(jax.ShapeDtypeStruct((S, D), F32), jax.ShapeDtypeStruct((S, 2 * DFF), BF16),
                   jax.ShapeDtypeStruct((S, D), BF16), jax.ShapeDtypeStruct((S, DFF), BF16),
                   jax.ShapeDtypeStruct((8, LANES), F32), jax.ShapeDtypeStruct((1, D), F32)),
        in_specs=[_rows(tm, D), _rows(tm, D), _resident((1, D)), _resident((2 * DFF, D)), _resident((DFF, D)),
                  _resident((1, D))],
        out_specs=(_rows(tm, D), _rows(tm, 2 * DFF), _rows(tm, D), _rows(tm, DFF),
                   pl.BlockSpec((8, LANES), lambda i: (0, 0)), pl.BlockSpec((1, D), lambda i: (0, 0))),
        compiler_params=_cparams(1),
    )(x2, tgt, g3, w_ffn_in_t, w_ffn_out, g4)


def _ffn_bwd(dx3, gu, x2, g3, w_ffn_out, w_ffn_in_t, tm=256):
    S = x2.shape[0]

    def body(d_ref, gu_ref, x_ref, g3_ref, wo_ref, wit_ref, dx2_ref, dgu_ref, dg3_ref):
        i = pl.program_id(0)
        d3 = d_ref[...]
        d3b = d3.astype(BF16)
        for c in range(DFF // FF_CHUNK):
            cs = slice(FF_CHUNK * c, FF_CHUNK * (c + 1))
            us = slice(DFF + FF_CHUNK * c, DFF + FF_CHUNK * (c + 1))
            da = _dot_nt(d3b, wo_ref[cs, :])
            gate = gu_ref[:, cs].astype(F32)
            up = gu_ref[:, us].astype(F32)
            sg = _sigmoid(gate)
            dgate = (da * up * (sg * (1.0 + gate * (1.0 - sg)))).astype(BF16)
            dup = (da * (gate * sg)).astype(BF16)
            dgu_ref[:, cs] = dgate
            dgu_ref[:, us] = dup
        dhf = _dot(dgu_ref[...], wit_ref[...])
        xv = x_ref[...]
        dx, dg3 = _rms_bwd(dhf, xv, _rstd(xv), g3_ref[...])
        dx2_ref[...] = d3 + dx

        @pl.when(i == 0)
        def _():
            dg3_ref[...] = jnp.zeros_like(dg3_ref)

        dg3_ref[...] += dg3

    return pl.pallas_call(
        body, name="ffn_bwd", grid=(S // tm,),
        out_shape=(jax.ShapeDtypeStruct((S, D), F32), jax.ShapeDtypeStruct((S, 2 * DFF), BF16),
                   jax.ShapeDtypeStruct((1, D), F32)),
        in_specs=[_rows(tm, D), _rows(tm, 2 * DFF), _rows(tm, D), _resident((1, D)),
                  _resident((DFF, D)), _resident((2 * DFF, D))],
        out_specs=(_rows(tm, D), _rows(tm, 2 * DFF), pl.BlockSpec((1, D), lambda i: (0, 0))),
        compiler_params=_cparams(1),
    )(dx3, gu, x2, g3, w_ffn_out, w_ffn_in_t)


def _dw(a, b, tmm, tn, ts, name, out_dtype=BF16, plans=()):
    S, M = a.shape
    N = b.shape[1]
    ts = min(ts, S)
    nk = S // ts

    def body(a_ref, b_ref, o_ref, acc_ref):
        k = pl.program_id(2)

        @pl.when(k == 0)
        def _():
            acc_ref[...] = jnp.zeros_like(acc_ref)

        acc_ref[...] += _dot_tn(a_ref[...].astype(BF16), b_ref[...].astype(BF16))

        @pl.when(k == nk - 1)
        def _():
            o_ref[...] = acc_ref[...].astype(out_dtype)

    out = _call(
        body, name=name, grid=(M // tmm, N // tn, nk),
        out_shape=(jax.ShapeDtypeStruct((M, N), out_dtype),),
        in_specs=[pl.BlockSpec((ts, tmm), lambda m, n, k: (k, m)), pl.BlockSpec((ts, tn), lambda m, n, k: (k, n))],
        out_specs=(pl.BlockSpec((tmm, tn), lambda m, n, k: (m, n)),),
        scratch_shapes=[pltpu.VMEM((tmm, tn), F32)],
        operands=(a, b), plans=plans)
    return out if plans else out[0]


def _dw_pair(a0, b0, a1, b1, ts, name):
    S, M = a0.shape
    N = b0.shape[1]
    ts = min(ts, S)
    nk = S // ts

    def body(a0_ref, b0_ref, a1_ref, b1_ref, o0_ref, o1_ref, acc0_ref, acc1_ref):
        k = pl.program_id(0)

        @pl.when(k == 0)
        def _():
            acc0_ref[...] = jnp.zeros_like(acc0_ref)
            acc1_ref[...] = jnp.zeros_like(acc1_ref)

        acc0_ref[...] += _dot_tn(a0_ref[...], b0_ref[...])
        acc1_ref[...] += _dot_tn(a1_ref[...], b1_ref[...])

        @pl.when(k == nk - 1)
        def _():
            o0_ref[...] = acc0_ref[...].astype(BF16)
            o1_ref[...] = acc1_ref[...].astype(BF16)

    a_spec = pl.BlockSpec((ts, M), lambda k: (k, 0))
    b_spec = pl.BlockSpec((ts, N), lambda k: (k, 0))
    o_spec = pl.BlockSpec((M, N), lambda k: (0, 0))
    return pl.pallas_call(
        body, name=name, grid=(nk,),
        out_shape=(jax.ShapeDtypeStruct((M, N), BF16), jax.ShapeDtypeStruct((M, N), BF16)),
        in_specs=[a_spec, b_spec, a_spec, b_spec], out_specs=(o_spec, o_spec),
        scratch_shapes=[pltpu.VMEM((M, N), F32), pltpu.VMEM((M, N), F32)],
        compiler_params=_cparams(1),
    )(a0, b0, a1, b1)


def _xattn_bwd(dx2, x1, qx, g2, w_xq, w_xo, kv, plans=(), tm=512):
    S = x1.shape[0]

    def body(d_ref, x_ref, qx_ref, g_ref, wq_ref, wo_ref, kv_ref, dx1_ref, dq_ref, dkv_ref, dg2_ref):
        i = pl.program_id(0)

        @pl.when(i == 0)
        def _():
            dkv_ref[...] = jnp.zeros_like(dkv_ref)
            dg2_ref[...] = jnp.zeros_like(dg2_ref)

        d2 = d_ref[...]
        qx = qx_ref[...]
        do = _dot_nt(d2.astype(BF16), wo_ref[...]).astype(BF16)
        for h in range(XH):
            hs = slice(256 * h, 256 * (h + 1))
            vs = slice(D + 256 * h, D + 256 * (h + 1))
            p = _xatt_head(qx, kv_ref, h)
            dp = _dot_nt(do[:, hs], kv_ref[:, vs])
            ds = (p * (dp - jnp.sum(dp * p, axis=-1, keepdims=True))).astype(BF16)
            dq_ref[:, hs] = (_dot(ds, kv_ref[:, hs]) * (1.0 / 16)).astype(BF16)
            dkv_ref[:, hs] += _dot_tn(ds, qx[:, hs])
            dkv_ref[:, vs] += _dot_tn(p.astype(BF16), do[:, hs])
        dhx = _dot_nt(dq_ref[...], wq_ref[...])
        xv = x_ref[...]
        dx, dg2 = _rms_bwd(dhx, xv, _rstd(xv), g_ref[...])
        dx1_ref[...] = d2 + dx
        dg2_ref[...] += dg2

    return _call(
        body, name="xattn_bwd", grid=(S // tm,),
        out_shape=(jax.ShapeDtypeStruct((S, D), F32), jax.ShapeDtypeStruct((S, D), BF16),
                   jax.ShapeDtypeStruct((MEM, 2 * D), F32), jax.ShapeDtypeStruct((1, D), F32)),
        in_specs=[_rows(tm, D), _rows(tm, D), _rows(tm, D), _resident((1, D)), _resident((D, D)), _resident((D, D)),
                  _resident((MEM, 2 * D))],
        out_specs=(_rows(tm, D), _rows(tm, D),
                   pl.BlockSpec((MEM, 2 * D), lambda i: (0, 0)), pl.BlockSpec((1, D), lambda i: (0, 0))),
        operands=(dx2, x1, qx, g2, w_xq, w_xo, kv), plans=plans)


def _mem_kv_bwd(dkv, mem, g_mem, mn, w_xkv_t):
    def body(dkv_ref, m_ref, g_ref, mn_ref, wt_ref, dw_ref, dg_ref):
        dkvb = dkv_ref[...].astype(BF16)
        dw_ref[...] = _dot_tn(dkvb, mn_ref[...]).astype(BF16)
        dmn = _dot(dkvb, wt_ref[...])
        mv = m_ref[...]
        dg_ref[...] = jnp.sum(dmn * (mv * _rstd(mv)), axis=0, keepdims=True)

    vm = pl.BlockSpec(memory_space=pltpu.VMEM)
    return pl.pallas_call(
        body, name="mem_kv_bwd",
        out_shape=(jax.ShapeDtypeStruct((2 * D, D), BF16), jax.ShapeDtypeStruct((1, D), F32)),
        in_specs=[vm] * 5, out_specs=(vm, vm),
        compiler_params=pltpu.CompilerParams(vmem_limit_bytes=VMEM_LIMIT),
    )(dkv, mem, g_mem, mn, w_xkv_t)


def _merge_bwd(dx1, y_att, y_sg, gates, wba_t, wbs_t, w_out, tm=512):
    S = dx1.shape[0]

    def body(d_ref, ya_ref, ys_ref, g_ref, wbat_ref, wbst_ref, wo_ref,
             mg_ref, da_ref, db_ref, dya_ref, dys_ref, dg_ref):
        dm = _dot_nt(d_ref[...].astype(BF16), wo_ref[...])
        a = _dot_nt(ya_ref[...], wbat_ref[...])
        b = _dot_nt(ys_ref[...], wbst_ref[...])
        sa = _sigmoid(g_ref[:, 0:D].astype(F32))
        sb = _sigmoid(g_ref[:, D:2 * D].astype(F32))
        mg_ref[...] = (sa * a + sb * b).astype(BF16)
        da = (dm * sa).astype(BF16)
        db = (dm * sb).astype(BF16)
        da_ref[...] = da
        db_ref[...] = db
        dg_ref[:, 0:D] = (dm * a * sa * (1.0 - sa)).astype(BF16)
        dg_ref[:, D:2 * D] = (dm * b * sb * (1.0 - sb)).astype(BF16)
        dya_ref[...] = _dot(da, wbat_ref[...]).astype(BF16)
        dys_ref[...] = _dot(db, wbst_ref[...]).astype(BF16)

    return pl.pallas_call(
        body, name="merge_bwd", grid=(S // tm,),
        out_shape=(jax.ShapeDtypeStruct((S, D), BF16), jax.ShapeDtypeStruct((S, D), BF16),
                   jax.ShapeDtypeStruct((S, D), BF16), jax.ShapeDtypeStruct((S, ATT_W), BF16),
                   jax.ShapeDtypeStruct((S, SG_W), BF16), jax.ShapeDtypeStruct((S, 2 * D), BF16)),
        in_specs=[_rows(tm, D), _rows(tm, ATT_W), _rows(tm, SG_W), _rows(tm, 2 * D),
                  _resident((D, ATT_W)), _resident((D, SG_W)), _resident((D, D))],
        out_specs=(_rows(tm, D), _rows(tm, D), _rows(tm, D), _rows(tm, ATT_W), _rows(tm, SG_W), _rows(tm, 2 * D)),
        compiler_params=_cparams(1),
    )(dx1, y_att, y_sg, gates, wba_t, wbs_t, w_out)


def _sgu_bwd(uv, dy_sg, lng, lnb, sg_w, b_exp, plans=(), tm=1024):
    S = uv.shape[0]
    n_steps = S // tm

    def body(uv_ref, dy_ref, lng_ref, lnb_ref, w_ref, b_ref, duv_ref, dw_ref, dbx_ref, dlng_ref, dlnb_ref, dvln_ref):
        i = pl.program_id(0)

        @pl.when(i == 0)
        def _():
            dw_ref[...] = jnp.zeros_like(dw_ref)
            dbx_ref[...] = jnp.zeros_like(dbx_ref)
            dlng_ref[...] = jnp.zeros_like(dlng_ref)
            dlnb_ref[...] = jnp.zeros_like(dlnb_ref)

        lane = lax.broadcasted_iota(jnp.int32, (1, LANES), 1)
        lo = lane < 64
        mask = _sg_mask()
        lng = lng_ref[...]
        gu, dgelu_u = _gelu_and_grad(uv_ref[:, 0:SG_W].astype(F32))
        gv, dgelu_v = _gelu_and_grad(uv_ref[:, SG_W:2 * SG_W].astype(F32))
        vln, vhat, rstd = _sg_layernorm(gv, lng, lnb_ref[...])
        dy = dy_ref[...].astype(F32)
        dsv_all = dy * gu
        for gp in range(4):
            wf0 = jnp.where(mask, w_ref[2 * gp], 0)
            wf1 = jnp.where(mask, w_ref[2 * gp + 1], 0)
            w0 = wf0.astype(BF16)
            w1 = wf1.astype(BF16)
            cols = slice(128 * gp, 128 * (gp + 1))
            dw0 = jnp.zeros((128, 128), F32)
            dw1 = jnp.zeros((128, 128), F32)
            dbx = jnp.zeros((128, LANES), F32)
            for n in range(tm // 128):
                rows = slice(128 * n, 128 * (n + 1))
                vl = vln[rows, cols]
                vl0 = jnp.where(lo, vl, 0).astype(BF16)
                vl1 = jnp.where(lo, 0, vl).astype(BF16)
                sv = _dot(w0, vl0) + _dot(w1, vl1) + b_ref[:, cols]
                duv_ref[rows, cols] = (dy[rows, cols] * sv * dgelu_u[rows, cols]).astype(BF16)
                dsv = dsv_all[rows, cols]
                dbx = dbx + dsv
                ds0 = jnp.where(lo, dsv, 0).astype(BF16)
                ds1 = jnp.where(lo, 0, dsv).astype(BF16)
                dw0 = dw0 + _dot_nt(ds0, vl0)
                dw1 = dw1 + _dot_nt(ds1, vl1)
                dvln_ref[rows, cols] = _dot_tn(w0, ds0) + _dot_tn(w1, ds1)
            dw_ref[2 * gp] += jnp.where(mask, dw0, 0)
            dw_ref[2 * gp + 1] += jnp.where(mask, dw1, 0)
            dbx_ref[:, cols] += dbx
        dvln = dvln_ref[...]
        dlng_ref[...] += jnp.sum(dvln * vhat, axis=0, keepdims=True)
        dlnb_ref[...] += jnp.sum(dvln, axis=0, keepdims=True)
        dvh = dvln * lng
        dgv = rstd * (dvh - _group_sum64(dvh) * (1.0 / 64) - vhat * (_group_sum64(dvh * vhat) * (1.0 / 64)))
        duv_ref[:, SG_W:2 * SG_W] = (dgv * dgelu_v).astype(BF16)

        @pl.when(i == n_steps - 1)
        def _():
            dbx_ref[...] = _group_sum64(dbx_ref[...])

    return _call(
        body, name="sgu_bwd", grid=(n_steps,),
        out_shape=(jax.ShapeDtypeStruct((S, 2 * SG_W), BF16), jax.ShapeDtypeStruct((8, 128, 128), F32),
                   jax.ShapeDtypeStruct((128, SG_W), F32), jax.ShapeDtypeStruct((1, SG_W), F32),
                   jax.ShapeDtypeStruct((1, SG_W), F32)),
        in_specs=[_rows(tm, 2 * SG_W), _rows(tm, SG_W), _resident((1, SG_W)), _resident((1, SG_W)),
                  _resident((8, 128, 128)), _resident((128, SG_W))],
        out_specs=(_rows(tm, 2 * SG_W), pl.BlockSpec((8, 128, 128), lambda i: (0, 0, 0)),
                   pl.BlockSpec((128, SG_W), lambda i: (0, 0)), pl.BlockSpec((1, SG_W), lambda i: (0, 0)),
                   pl.BlockSpec((1, SG_W), lambda i: (0, 0))),
        scratch_shapes=[pltpu.VMEM((tm, SG_W), F32)],
        operands=(uv, dy_sg, lng, lnb, sg_w, b_exp), plans=plans)


def _attn_bwd(qkv, dy_att, y_att, lse, bias, plans=()):
    S = qkv.shape[0]
    R = ATT_R

    def body(q_ref, kp_ref, kc_ref, vp_ref, vc_ref, b_ref, dy_ref, y_ref, lse_ref, dq_ref, dk_ref, dv_ref, dss_ref,
             s_ref, dp_ref, pb_ref, dsb_ref):
        i = pl.program_id(1)

        @pl.when(i == 0)
        def _():
            dk_ref[...] = jnp.zeros_like(dk_ref)
            dv_ref[...] = jnp.zeros_like(dv_ref)
            dss_ref[...] = jnp.zeros_like(dss_ref)

        lo = lax.broadcasted_iota(jnp.int32, (1, LANES), 1) < 64
        kwin = jnp.concatenate([kp_ref[R - 8 * CHUNK:R, :], kc_ref[...]], axis=0)
        vwin = jnp.concatenate([vp_ref[R - 8 * CHUNK:R, :], vc_ref[...]], axis=0)
        for sub in range(ATT_SUBS):
            rows = slice(128 * sub, 128 * (sub + 1))
            kw = kwin[128 * sub:128 * sub + BAND_KEYS]
            vw = vwin[128 * sub:128 * sub + BAND_KEYS]
            table = jnp.where(i == 0, 1 + sub, 0) if sub < START_SUBS else 0
            qs = _two_heads(q_ref[rows, :], lo)
            dos = _two_heads(dy_ref[rows, :], lo)
            dyy = dy_ref[rows, :].astype(F32) * y_ref[rows, :].astype(F32)
            delta = (jnp.sum(jnp.where(lo, dyy, 0.0), axis=-1, keepdims=True),
                     jnp.sum(jnp.where(lo, 0.0, dyy), axis=-1, keepdims=True))
            lse2 = lse_ref[rows, :]
            s_ref[...] = _dot_nt(qs, kw)
            dp_ref[...] = _dot_nt(dos, vw)
            for hh in range(2):
                lse = lse2[:, 64 * hh:64 * hh + 1]
                for r0 in range(0, 128, ATT_ROWS_BWD):
                    rr = slice(r0, r0 + ATT_ROWS_BWD)
                    both = slice(128 * hh + r0, 128 * hh + r0 + ATT_ROWS_BWD)
                    p = jnp.exp(s_ref[both, :] + b_ref[table, hh, rr, :] - lse[rr])
                    ds = p * (dp_ref[both, :] - delta[hh][rr])
                    dss_ref[hh, rr, :] += ds
                    pb_ref[both, :] = p.astype(BF16)
                    dsb_ref[both, :] = ds.astype(BF16)
            dq = _dot(jnp.concatenate([dsb_ref[0:128, :], dsb_ref[128:256, :]], axis=1), _two_heads(kw, lo))
            dq_ref[rows, :] = dq.astype(BF16)
            start = pl.multiple_of(i * R + 128 * sub, 128)
            dk_ref[pl.ds(start, BAND_KEYS), :] += _dot_tn(dsb_ref[...], qs)
            dv_ref[pl.ds(start, BAND_KEYS), :] += _dot_tn(pb_ref[...], dos)

    acc_spec = pl.BlockSpec((S + 8 * CHUNK, LANES), lambda j, i: (0, j))
    return _call(
        body, name="attn_bwd", grid=(4, S // R),
        out_shape=(jax.ShapeDtypeStruct((S, ATT_W), BF16), jax.ShapeDtypeStruct((S + 8 * CHUNK, ATT_W), F32),
                   jax.ShapeDtypeStruct((S + 8 * CHUNK, ATT_W), F32), jax.ShapeDtypeStruct((8, 128, BAND_KEYS), F32)),
        in_specs=_att_specs() + [pl.BlockSpec((R, LANES), lambda j, i: (i, j))] * 3,
        out_specs=(pl.BlockSpec((R, LANES), lambda j, i: (i, j)), acc_spec, acc_spec,
                   pl.BlockSpec((2, 128, BAND_KEYS), lambda j, i: (j, 0, 0))),
        scratch_shapes=[pltpu.VMEM((256, BAND_KEYS), F32), pltpu.VMEM((256, BAND_KEYS), F32),
                        pltpu.VMEM((256, BAND_KEYS), BF16), pltpu.VMEM((256, BAND_KEYS), BF16)],
        operands=(qkv, qkv, qkv, qkv, qkv, bias, dy_att, y_att, lse), plans=plans)


def _in_bwd(dq, dk, dv, duv, dgates, x, dx1, g1, w_in_t, plans=(), tm=512):
    S = x.shape[0]
    pad_blocks = (8 * CHUNK) // tm

    def body(dq_ref, dk_ref, dv_ref, duv_ref, dg_ref, x_ref, d1_ref, g_ref, wt_ref, dx_ref, dz_ref, dg1_ref):
        i = pl.program_id(0)
        dz_ref[:, 0:ATT_W] = (dq_ref[...].astype(F32) * 0.125).astype(BF16)
        dz_ref[:, ATT_W:2 * ATT_W] = dk_ref[...].astype(BF16)
        dz_ref[:, 2 * ATT_W:3 * ATT_W] = dv_ref[...].astype(BF16)
        dz_ref[:, 3 * ATT_W:3 * ATT_W + 2 * SG_W] = duv_ref[...]
        dz_ref[:, 3 * ATT_W + 2 * SG_W:IN_COLS] = dg_ref[...]
        dh = _dot(dz_ref[...], wt_ref[...])
        xv = x_ref[...]
        dx, dg1 = _rms_bwd(dh, xv, _rstd(xv), g_ref[...])
        dx_ref[...] = d1_ref[...] + dx

        @pl.when(i == 0)
        def _():
            dg1_ref[...] = jnp.zeros_like(dg1_ref)

        dg1_ref[...] += dg1

    shifted = pl.BlockSpec((tm, ATT_W), lambda i: (i + pad_blocks, 0))
    return _call(
        body, name="in_bwd", grid=(S // tm,),
        out_shape=(jax.ShapeDtypeStruct((S, D), F32), jax.ShapeDtypeStruct((S, IN_COLS), BF16),
                   jax.ShapeDtypeStruct((1, D), F32)),
        in_specs=[_rows(tm, ATT_W), shifted, shifted, _rows(tm, 2 * SG_W), _rows(tm, 2 * D), _rows(tm, D),
                  _rows(tm, D), _resident((1, D)), _resident((IN_COLS, D))],
        out_specs=(_rows(tm, D), _rows(tm, IN_COLS), pl.BlockSpec((1, D), lambda i: (0, 0))),
        operands=(dq, dk, dv, duv, dgates, x, dx1, g1, w_in_t), plans=plans)


def _adam_math(w, g, m, v):
    m = ADAM_B1 * m + (1.0 - ADAM_B1) * g
    v = ADAM_B2 * v + (1.0 - ADAM_B2) * (g * g)
    m_hat = m / (1.0 - ADAM_B1 ** ADAM_STEP)
    v_hat = v / (1.0 - ADAM_B2 ** ADAM_STEP)
    delta = -ADAM_LR * (m_hat / (jnp.sqrt(v_hat) + ADAM_EPS) + ADAM_WD * w)
    return delta, m, v


def _adam(parts, w, m, v, tr, name, transposed, after=None):
    P = parts.shape[0]
    R, C = w.shape

    def body(p_ref, w_ref, m_ref, v_ref, *rest):
        g_ref, d_ref, mo_ref, vo_ref = rest[-4:]
        if transposed:
            eye = (lax.broadcasted_iota(jnp.int32, (C, C), 0) == lax.broadcasted_iota(jnp.int32, (C, C), 1)).astype(BF16)
            part = lambda k: _dot_tn(p_ref[k], eye)
        else:
            part = lambda k: p_ref[k].astype(F32)
        g = part(0)
        for k in range(1, P):
            g = g + part(k)
        g_ref[...] = g
        d_ref[...], mo_ref[...], vo_ref[...] = _adam_math(w_ref[...], g, m_ref[...], v_ref[...])

    row = pl.BlockSpec((tr, C), lambda i: (i, 0))
    p_spec = pl.BlockSpec((P, C, tr), lambda i: (0, 0, i)) if transposed else pl.BlockSpec((P, tr, C), lambda i: (0, i, 0))
    extra = [] if after is None else [after]
    return pl.pallas_call(
        body, name=name, grid=(R // tr,),
        out_shape=tuple(jax.ShapeDtypeStruct((R, C), F32) for _ in range(4)),
        in_specs=[p_spec, row, row, row] + [pl.BlockSpec(memory_space=pl.ANY)] * len(extra),
        out_specs=(row, row, row, row),
        compiler_params=_cparams(1),
    )(parts, w, m, v, *extra)


def _my_place():
    return lax.axis_index("x"), lax.axis_index("y"), lax.axis_index("c")


def _other_chips(x, y):
    return [(1 - x, y), (x, 1 - y), (1 - x, 1 - y)]


class _Plan:
    def __init__(self, arrays, out_shapes, sems, start, finish, forward=None, forward_at=0.7):
        self.arrays, self.out_shapes, self.sems = list(arrays), list(out_shapes), list(sems)
        self.start, self.finish, self.forward, self.forward_at = start, finish, forward, forward_at


def _call(body, *, name, grid, in_specs, out_specs, out_shape, operands, scratch_shapes=(), plans=()):
    n_in, n_out, n_scr = len(operands), len(out_shape), len(scratch_shapes)
    p_in = [a for p in plans for a in p.arrays]
    p_out = [s for p in plans for s in p.out_shapes]
    p_sem = [s for p in plans for s in p.sems]
    steps = math.prod(grid)

    def wrapped(*refs):
        ins, refs = refs[:n_in], refs[n_in:]
        pins, refs = refs[:len(p_in)], refs[len(p_in):]
        outs, refs = refs[:n_out], refs[n_out:]
        pouts, refs = refs[:len(p_out)], refs[len(p_out):]
        scr, psems = refs[:n_scr], refs[n_scr:]
        step = 0
        for axis, size in enumerate(grid):
            step = step * size + pl.program_id(axis)
        bound = []
        for p in plans:
            bound.append((p, pins[:len(p.arrays)], pouts[:len(p.out_shapes)], psems[:len(p.sems)]))
            pins, pouts, psems = pins[len(p.arrays):], pouts[len(p.out_shapes):], psems[len(p.sems):]

        @pl.when(step == 0)
        def _():
            for p, a, b, s in bound:
                p.start(a, b, s)

        body(*ins, *outs, *scr)

        for p, a, b, s in bound:
            if p.forward is not None:
                @pl.when(step == min(int(p.forward_at * steps), steps - 1))
                def _(p=p, a=a, b=b, s=s):
                    p.forward(a, b, s)

        @pl.when(step == steps - 1)
        def _():
            for p, a, b, s in bound:
                p.finish(a, b, s)

    hbm = pl.BlockSpec(memory_space=pl.ANY)
    return pl.pallas_call(
        wrapped if plans else body, name=name, grid=grid,
        out_shape=tuple(out_shape) + tuple(p_out),
        in_specs=list(in_specs) + [hbm] * len(p_in),
        out_specs=tuple(out_specs) + tuple([hbm] * len(p_out)),
        scratch_shapes=list(scratch_shapes) + p_sem,
        compiler_params=_cparams(len(grid)),
    )(*operands, *p_in)


def _run_plan(plan, name):
    n_in, n_out = len(plan.arrays), len(plan.out_shapes)

    def body(*refs):
        a, b, s = refs[:n_in], refs[n_in:n_in + n_out], refs[n_in + n_out:]
        plan.start(a, b, s)
        if plan.forward is not None:
            plan.forward(a, b, s)
        plan.finish(a, b, s)

    hbm = pl.BlockSpec(memory_space=pl.ANY)
    return pl.pallas_call(
        body, name=name, out_shape=tuple(plan.out_shapes),
        in_specs=[hbm] * n_in, out_specs=tuple([hbm] * n_out), scratch_shapes=plan.sems,
    )(*plan.arrays)


def _gather_plan(shards, forward_at=0.7):
    n = len(shards)

    def copies(ins, outs, sems):
        send_sems, recv_sems, local_sems = sems
        x, y, c = _my_place()
        me, sibling = (x, y, c), (x, y, 1 - c)
        chips = _other_chips(x, y)

        def idx(p):
            return 4 * p[0] + 2 * p[1] + p[2]

        def copy(a, k, block, to, src=None):
            return pltpu.make_async_remote_copy(
                src_ref=outs[a].at[idx(block)] if src is None else src, dst_ref=outs[a].at[idx(block)],
                send_sem=send_sems.at[a, k], recv_sem=recv_sems.at[a, k], device_id=to, device_id_type=MESH)

        mine = [pltpu.make_async_copy(ins[a], outs[a].at[idx(me)], local_sems.at[a]) for a in range(n)]
        first = []
        for a in range(n):
            first.append(copy(a, 0, me, sibling, src=ins[a]))
            first += [copy(a, 1 + j, me, (*chip, c), src=ins[a]) for j, chip in enumerate(chips)]
        arrived = [copy(a, 1 + j, (*chip, c), me) for j, chip in enumerate(chips) for a in range(n)]
        passed = [copy(a, 4 + j, (*chip, c), sibling) for j, chip in enumerate(chips) for a in range(n)]
        from_sibling = []
        for a in range(n):
            from_sibling.append(copy(a, 0, sibling, me))
            from_sibling += [copy(a, 4 + j, (*chip, 1 - c), me) for j, chip in enumerate(chips)]
        return mine, first, arrived, passed, from_sibling

    def start(ins, outs, sems):
        mine, first, _, _, _ = copies(ins, outs, sems)
        for cp in mine + first:
            cp.start()

    def forward(ins, outs, sems):
        _, _, arrived, passed, _ = copies(ins, outs, sems)
        for landed, onward in zip(arrived, passed):
            landed.wait_recv()
            onward.start()

    def finish(ins, outs, sems):
        mine, first, _, passed, from_sibling = copies(ins, outs, sems)
        for cp in from_sibling:
            cp.wait_recv()
        for cp in first + passed:
            cp.wait_send()
        for cp in mine:
            cp.wait()

    return _Plan(shards, [jax.ShapeDtypeStruct((N_DEV,) + s.shape, s.dtype) for s in shards],
                 [pltpu.SemaphoreType.DMA((n, 7)), pltpu.SemaphoreType.DMA((n, 7)), pltpu.SemaphoreType.DMA((n,))],
                 start, finish, forward, forward_at)


def _sibling_plan(scatter, whole=()):
    ns = len(scatter)
    n = ns + len(whole)

    def copies(ins, outs, sems):
        send_sems, recv_sems = sems
        x, y, c = _my_place()
        out = []
        for a in range(n):
            for k in range(4 if a < ns else 1):
                src = ins[a].at[2 * k + (1 - c)] if a < ns else ins[a]
                dst = outs[a].at[k] if a < ns else outs[a]
                out.append(pltpu.make_async_remote_copy(
                    src_ref=src, dst_ref=dst, send_sem=send_sems.at[a, k], recv_sem=recv_sems.at[a, k],
                    device_id=(x, y, 1 - c), device_id_type=MESH))
        return out

    def start(ins, outs, sems):
        for cp in copies(ins, outs, sems):
            cp.start()

    def finish(ins, outs, sems):
        for cp in copies(ins, outs, sems):
            cp.wait()

    shapes = [jax.ShapeDtypeStruct((4,) + p.shape[1:], p.dtype) for p in scatter]
    shapes += [jax.ShapeDtypeStruct(p.shape, p.dtype) for p in whole]
    return _Plan(list(scatter) + list(whole), shapes,
                 [pltpu.SemaphoreType.DMA((n, 4)), pltpu.SemaphoreType.DMA((n, 4))], start, finish)


def _pair_sums(mine, theirs, c_idx, name):
    n = len(mine)

    def body(c_ref, *refs):
        for a in range(n):
            out = refs[2 * n + a]
            out[...] = (refs[a][...].astype(F32) + refs[n + a][...].astype(F32)).astype(out.dtype)

    def block(arr):
        return (1,) + arr.shape[1:]

    grid_spec = pltpu.PrefetchScalarGridSpec(
        num_scalar_prefetch=1, grid=(4,),
        in_specs=[pl.BlockSpec(block(m), lambda k, c_ref: (2 * k + c_ref[0], 0, 0)) for m in mine]
        + [pl.BlockSpec(block(t), lambda k, c_ref: (k, 0, 0)) for t in theirs],
        out_specs=tuple(pl.BlockSpec(block(t), lambda k, c_ref: (k, 0, 0)) for t in theirs))
    return pl.pallas_call(
        body, name=name, grid_spec=grid_spec,
        out_shape=tuple(jax.ShapeDtypeStruct(t.shape, m.dtype) for m, t in zip(mine, theirs)),
        compiler_params=_cparams(1),
    )(c_idx, *mine, *theirs)


def _peers_plan(arrays):
    n = len(arrays)

    def copies(ins, outs, sems):
        send_sems, recv_sems, local_sems = sems
        x, y, c = _my_place()
        me = 4 * x + 2 * y + c
        out = [pltpu.make_async_copy(ins[a], outs[a].at[me], local_sems.at[a]) for a in range(n)]
        for a in range(n):
            for k in range(N_DEV - 1):
                bits = k + 1
                peer = (x ^ (bits >> 2), y ^ ((bits >> 1) & 1), c ^ (bits & 1))
                out.append(pltpu.make_async_remote_copy(
                    src_ref=ins[a], dst_ref=outs[a].at[me], send_sem=send_sems.at[a, k], recv_sem=recv_sems.at[a, k],
                    device_id=peer, device_id_type=MESH))
        return out

    def start(ins, outs, sems):
        for cp in copies(ins, outs, sems):
            cp.start()

    def finish(ins, outs, sems):
        for cp in copies(ins, outs, sems):
            cp.wait()

    return _Plan(list(arrays), [jax.ShapeDtypeStruct((N_DEV,) + a.shape, a.dtype) for a in arrays],
                 [pltpu.SemaphoreType.DMA((n, N_DEV - 1)), pltpu.SemaphoreType.DMA((n, N_DEV - 1)),
                  pltpu.SemaphoreType.DMA((n,))], start, finish)


def _chips_plan(scatter, whole=()):
    ns = len(scatter)
    n = ns + len(whole)

    def copies(ins, outs, sems):
        send_sems, recv_sems, local_sems = sems
        x, y, c = _my_place()
        my_chip = 2 * x + y

        def src(a, k):
            return ins[a].at[k] if a < ns else ins[a]

        local = [pltpu.make_async_copy(src(a, my_chip), outs[a].at[my_chip], local_sems.at[a]) for a in range(n)]
        remote = []
        for a in range(n):
            for j, (px, py) in enumerate(_other_chips(x, y)):
                remote.append(pltpu.make_async_remote_copy(
                    src_ref=src(a, 2 * px + py), dst_ref=outs[a].at[my_chip],
                    send_sem=send_sems.at[a, j], recv_sem=recv_sems.at[a, j],
                    device_id=(px, py, c), device_id_type=MESH))
        return local + remote

    def start(ins, outs, sems):
        for cp in copies(ins, outs, sems):
            cp.start()

    def finish(ins, outs, sems):
        for cp in copies(ins, outs, sems):
            cp.wait()

    shapes = [jax.ShapeDtypeStruct(s.shape, s.dtype) for s in scatter]
    shapes += [jax.ShapeDtypeStruct((4,) + s.shape, s.dtype) for s in whole]
    return _Plan(list(scatter) + list(whole), shapes,
                 [pltpu.SemaphoreType.DMA((n, 3)), pltpu.SemaphoreType.DMA((n, 3)), pltpu.SemaphoreType.DMA((n,))],
                 start, finish)


def _chip_copies(src_ref, land_ref, send_sems, recv_sems):
    x, y, c = _my_place()
    return [pltpu.make_async_remote_copy(
        src_ref=src_ref.at[2 * px + py], dst_ref=land_ref.at[2 * x + y], send_sem=send_sems.at[j],
        recv_sem=recv_sems.at[j], device_id=(px, py, c), device_id_type=MESH)
        for j, (px, py) in enumerate(_other_chips(x, y))]


def _chips_start(src, name):
    def body(src_ref, land_ref, send_sems, recv_sems, src_thru, land_thru, token):
        for cp in _chip_copies(src_ref, land_ref, send_sems, recv_sems):
            cp.start()
        token[...] = jnp.zeros_like(token)

    hbm = pl.BlockSpec(memory_space=pltpu.HBM)
    sem = pl.BlockSpec(memory_space=pltpu.SEMAPHORE)
    return pl.pallas_call(
        body, name=name,
        out_shape=(pltpu.SemaphoreType.DMA((3,)), pltpu.SemaphoreType.DMA((3,)), pltpu.HBM(src.shape, src.dtype),
                   pltpu.HBM(src.shape, src.dtype), jax.ShapeDtypeStruct((8, LANES), F32)),
        in_specs=(hbm, hbm), out_specs=(sem, sem, hbm, hbm, pl.BlockSpec(memory_space=pltpu.VMEM)),
        input_output_aliases={0: 2, 1: 3},
        compiler_params=pltpu.CompilerParams(has_side_effects=pltpu.SideEffectType.DATAFLOW_SIDE_EFFECTING),
    )(pltpu.with_memory_space_constraint(src, pltpu.HBM),
      pltpu.with_memory_space_constraint(jnp.zeros(src.shape, src.dtype), pltpu.HBM))


def _chips_wait(send_sems, recv_sems, src_thru, land_thru, after, name):
    def body(src_ref, land_ref, send_sems, recv_sems, after_ref, src_dead, got_ref):
        for cp in _chip_copies(src_ref, land_ref, send_sems, recv_sems):
            cp.wait_send()
            cp.wait_recv()

    hbm = pl.BlockSpec(memory_space=pltpu.HBM)
    sem = pl.BlockSpec(memory_space=pltpu.SEMAPHORE)
    return pl.pallas_call(
        body, name=name,
        out_shape=(pltpu.HBM(src_thru.shape, src_thru.dtype), pltpu.HBM(land_thru.shape, land_thru.dtype)),
        in_specs=(hbm, hbm, sem, sem, pl.BlockSpec(memory_space=pl.ANY)), out_specs=(hbm, hbm),
        input_output_aliases={0: 0, 1: 1},
        compiler_params=pltpu.CompilerParams(has_side_effects=pltpu.SideEffectType.DATAFLOW_SIDE_EFFECTING),
    )(src_thru, land_thru, send_sems, recv_sems, after)


def _small_update(parts, w, m, v, loss_parts, name):
    n = len(parts)

    def total(ref):
        acc = ref[0]
        for k in range(1, ref.shape[0]):
            acc = acc + ref[k]
        return acc

    def body(*refs):
        p_refs, w_refs, m_refs, v_refs = (refs[i * n:(i + 1) * n] for i in range(4))
        lp_ref = refs[4 * n]
        outs = refs[4 * n + 1:]
        g_refs, d_refs, mo_refs, vo_refs = (outs[i * n:(i + 1) * n] for i in range(4))
        for a in range(n):
            g = total(p_refs[a])
            g_refs[a][...] = g
            d_refs[a][...], mo_refs[a][...], vo_refs[a][...] = _adam_math(w_refs[a][...], g, m_refs[a][...],
                                                                          v_refs[a][...])
        outs[4 * n][...] = total(lp_ref)

    vm = pl.BlockSpec(memory_space=pltpu.VMEM)
    shapes = [jax.ShapeDtypeStruct(t.shape, F32) for _ in range(4) for t in w]
    shapes.append(jax.ShapeDtypeStruct(loss_parts.shape[1:], F32))
    outs = pl.pallas_call(
        body, name=name, out_shape=tuple(shapes),
        in_specs=[vm] * (4 * n + 1), out_specs=tuple([vm] * (4 * n + 1)),
        compiler_params=pltpu.CompilerParams(vmem_limit_bytes=VMEM_LIMIT),
    )(*parts, *w, *m, *v, loss_parts)
    return outs[0:n], outs[n:2 * n], outs[2 * n:3 * n], outs[3 * n:4 * n], outs[4 * n]


BIG = [("w_in", 1), ("w_branch_att", 1), ("w_branch_sg", 1), ("w_out", 0), ("w_xq", 0), ("w_xkv", 1), ("w_xo", 0),
       ("w_ffn_in", 1), ("w_ffn_out", 0)]
SMALL = [("norm_mix_g", (1, D)), ("rel_bias", (8, NREL)), ("sg_ln_g", (8, 64)), ("sg_ln_b", (8, 64)),
         ("sg_w", (8, 128, 128)), ("sg_b", (8, 128)), ("norm_xattn_g", (1, D)), ("norm_mem_g", (1, D)),
         ("norm_ffn_g", (1, D)), ("norm_final_g", (1, D))]
ADAM_ROWS = {"w_in": 192, "w_branch_att": 512, "w_branch_sg": 512, "w_xkv": 1024, "w_ffn_in": 176}


def _full(gathered):
    return gathered.reshape(N_DEV * gathered.shape[1], gathered.shape[2])


def _blocks(grad):
    return grad.reshape(N_DEV, grad.shape[0] // N_DEV, grad.shape[1])


def kernel(x, mem, norm_mix_g, w_in, rel_bias, sg_ln_g, sg_ln_b, sg_w, sg_b, w_branch_att, w_branch_sg, w_out, norm_xattn_g, norm_mem_g, w_xq, w_xkv, w_xo, norm_ffn_g, w_ffn_in, w_ffn_out, norm_final_g, loss_target, m_norm_mix_g, m_w_in, m_rel_bias, m_sg_ln_g, m_sg_ln_b, m_sg_w, m_sg_b, m_w_branch_att, m_w_branch_sg, m_w_out, m_norm_xattn_g, m_norm_mem_g, m_w_xq, m_w_xkv, m_w_xo, m_norm_ffn_g, m_w_ffn_in, m_w_ffn_out, m_norm_final_g, v_norm_mix_g, v_w_in, v_rel_bias, v_sg_ln_g, v_sg_ln_b, v_sg_w, v_sg_b, v_w_branch_att, v_w_branch_sg, v_w_out, v_norm_xattn_g, v_norm_mem_g, v_w_xq, v_w_xkv, v_w_xo, v_norm_ffn_g, v_w_ffn_in, v_w_ffn_out, v_norm_final_g):
    args = dict(locals())
    big_names = [n for n, _ in BIG]
    small_names = [n for n, _ in SMALL]
    S = x.shape[1]

    x, mem, tgt = x.reshape(S, D), mem.reshape(MEM, D), loss_target.reshape(S, D)
    small = {n: args[n].reshape(shape) for n, shape in SMALL}
    g1, g2, g3 = small["norm_mix_g"], small["norm_xattn_g"], small["norm_ffn_g"]
    g_mem, g4 = small["norm_mem_g"], small["norm_final_g"]
    lng = small["sg_ln_g"].reshape(1, SG_W)
    lnb = small["sg_ln_b"].reshape(1, SG_W)
    b_exp = jnp.broadcast_to(small["sg_b"].T[:, :, None], (128, 8, 64)).reshape(128, SG_W)
    rel_pad = jnp.pad(small["rel_bias"], ((0, 0), (0, 384 - NREL)))
    c_idx = lax.axis_index("c").astype(jnp.int32).reshape(1)

    shard = {n: (args[n][0].T if axis == 1 else args[n][0]).astype(BF16) for n, axis in BIG}
    h, w_in_gathered = _norm_in(x, g1, plans=[_gather_plan([shard["w_in"]], forward_at=1.0)])
    w_in_t = _full(w_in_gathered)
    bias = _bias_table(rel_pad)
    mix_names = ["w_branch_att", "w_branch_sg", "w_out", "w_xq", "w_xkv", "w_xo"]
    qkv, uv, gates, *got = _in_proj(h, w_in_t, plans=[_gather_plan([shard[n] for n in mix_names])])
    wba_t, wbs_t, w_out_f, w_xq_f, w_xkv_t, w_xo_f = (_full(g) for g in got)
    y_att, lse, *got = _attn_fwd(qkv, bias, plans=[_gather_plan([shard["w_ffn_in"], shard["w_ffn_out"]])])
    w_ffn_in_t, w_ffn_out_f = (_full(g) for g in got)
    y_sg = _sgu_fwd(uv, lng, lnb, small["sg_w"], b_exp)
    x1 = _merge_fwd(x, y_att, y_sg, gates, wba_t, wbs_t, w_out_f)
    kv, mn = _mem_kv(mem, g_mem, w_xkv_t)
    x2, hx, qx, o_x = _xattn_fwd(x1, g2, w_xq_f, kv, w_xo_f)
    dx3, gu, hf, act, loss_part, dg4 = _ffn_fwd(x2, tgt, g3, w_ffn_in_t, w_ffn_out_f, g4)


    dx2, dgu, dg3 = _ffn_bwd(dx3, gu, x2, g3, w_ffn_out_f, w_ffn_in_t)
    ffn_names = ["w_ffn_out", "w_ffn_in"]
    ffn_mine = [_blocks(_dw(act, dx3, 1408, 1024, DW_TOKENS, "dw_ffn_out")),
                _blocks(_dw(dgu, hf, 1408, 1024, DW_TOKENS, "dw_ffn_in"))]
    dx1, dq_x, dkv, dg2, *ffn_theirs = _xattn_bwd(dx2, x1, qx, g2, w_xq_f, w_xo_f, kv,
                                                  plans=[_sibling_plan(ffn_mine)])
    ffn_chip = _pair_sums(ffn_mine, ffn_theirs, c_idx, "rs_pair_ffn")
    d_xkv, dg_mem = _mem_kv_bwd(dkv, mem, g_mem, mn, w_xkv_t)
    merged, d_a, d_b, dy_att, dy_sg, dgates = _merge_bwd(dx1, y_att, y_sg, gates, wba_t, wbs_t, w_out_f)
    mid_names = ["w_xo", "w_xq", "w_xkv", "w_out", "w_branch_att", "w_branch_sg"]
    mid_mine = [_blocks(g) for g in (
        _dw(o_x, dx2, 1024, 1024, DW_TOKENS, "dw_xo"), _dw(hx, dq_x, 1024, 1024, DW_TOKENS, "dw_xq"), d_xkv,
        _dw(merged, dx1, 1024, 1024, DW_TOKENS, "dw_out"),
        *_dw_pair(d_a, y_att, d_b, y_sg, DW_TOKENS, "dw_branch"))]
    duv, d_sgw, d_bx, d_lng, d_lnb, *got = _sgu_bwd(uv, dy_sg, lng, lnb, small["sg_w"], b_exp,
                                                    plans=[_chips_plan(ffn_chip), _sibling_plan(mid_mine)])
    ffn_all, mid_theirs = got[:2], got[2:]
    mid_chip = _pair_sums(mid_mine, mid_theirs, c_idx, "rs_pair_mid")
    dq, dk, dv, ds_sum, *mid_all = _attn_bwd(qkv, dy_att, y_att, lse, bias, plans=[_chips_plan(mid_chip)])
    d_rel = _bias_grad(ds_sum)

    grad_x, dz, dg1 = _in_bwd(dq, dk, dv, duv, dgates, x, dx1, g1, w_in_t)

    gs = {"norm_mix_g": dg1, "rel_bias": d_rel[:, :NREL], "sg_ln_g": d_lng.reshape(8, 64),
          "sg_ln_b": d_lnb.reshape(8, 64), "sg_w": d_sgw, "sg_b": d_bx.reshape(128, 8, 64)[:, :, 0].T,
          "norm_xattn_g": dg2, "norm_mem_g": dg_mem, "norm_ffn_g": dg3, "norm_final_g": dg4}
    d_in, *everyone = _dw(dz, h, 1152, 1024, DW_TOKENS, "dw_in",
                          plans=[_peers_plan([gs[n] for n in small_names] + [loss_part])])

    in_mine = [_blocks(d_in)]
    (in_theirs,) = _run_plan(_sibling_plan(in_mine), "rs_sibling")
    (in_chip,) = _pair_sums(in_mine, [in_theirs], c_idx, "rs_pair_w_in")
    send_sems, recv_sems, in_chip_thru, landing, token = _chips_start(in_chip, "rs_chips_start")
    all_parts = dict(zip(ffn_names + mid_names, list(ffn_all) + list(mid_all)))

    def adam(n, axis, parts, after):
        wmv = [args[p + n][0] for p in ("", "m_", "v_")]
        if axis == 1 and wmv[0].shape[1] % LANES != 0:
            outs = _adam(parts, *(t.T for t in wmv), ADAM_ROWS[n], "adam_" + n, transposed=False, after=after)
            return [t.T[None] for t in outs]
        tr = ADAM_ROWS[n] if axis == 1 else wmv[0].shape[0]
        return [t[None] for t in _adam(parts, *wmv, tr, "adam_" + n, transposed=(axis == 1), after=after)]

    res = {n: adam(n, axis, all_parts[n], token) for n, axis in BIG if n != "w_in"}
    in_chip, landing = _chips_wait(send_sems, recv_sems, in_chip_thru, landing, res["w_ffn_in"][1], "rs_chips_wait")
    my_chip = 2 * lax.axis_index("x") + lax.axis_index("y")
    own = lax.dynamic_slice_in_dim(in_chip, my_chip, 1, axis=0)
    res["w_in"] = adam("w_in", 1, lax.dynamic_update_slice_in_dim(landing, own, my_chip, axis=0), None)
    small_res = _small_update(
        everyone[:-1], [small[n] for n in small_names],
        [args["m_" + n].reshape(s) for n, s in SMALL], [args["v_" + n].reshape(s) for n, s in SMALL],
        everyone[-1], "adam_small")
    for i, n in enumerate(small_names):
        res[n] = [small_res[k][i].reshape(args[n].shape) for k in range(4)]
    loss = small_res[4][0, 0]

    order = ["norm_mix_g", "w_in", "rel_bias", "sg_ln_g", "sg_ln_b", "sg_w", "sg_b", "w_branch_att", "w_branch_sg",
             "w_out", "norm_xattn_g", "norm_mem_g", "w_xq", "w_xkv", "w_xo", "norm_ffn_g", "w_ffn_in", "w_ffn_out",
             "norm_final_g"]
    outs = [loss, grad_x.reshape(1, S, D)]
    for k in range(4):
        outs += [res[n][k] for n in order]
    return tuple(outs)
```
